```python
import jax, jax.numpy as jnp
from jax import lax
import numpy as np

D_MODEL = 1024
BATCH = 4
SEQ = 4096
DEPTH = 1
DEC_BATCH = 32
DEC_SEQ = 1
PAST_LEN = 16384
PAGE_SIZE = 128

HEAD_DIM = 64
MIX_WIDTH = D_MODEL
SGU_GROUPS = (MIX_WIDTH // 2) // HEAD_DIM
ATT_HEADS = (MIX_WIDTH // 2) // HEAD_DIM
SGU_WIDTH = SGU_GROUPS * HEAD_DIM
ATT_WIDTH = ATT_HEADS * HEAD_DIM
PROJ_COLS = 2 * SGU_WIDTH + 3 * ATT_WIDTH
CHUNK = 128
BLOCK = 128
DILATED_PATTERNS = ((128, 1), (512, 4), (2048, 16))
MAX_WINDOW = 2048
ROPE_THETA = 10000.0
N_EXPERT_GROUPS = 4
EXPERTS_PER_GROUP = 8
N_EXPERTS = N_EXPERT_GROUPS * EXPERTS_PER_GROUP
EXPERT_TOP_K = 2
D_EXPERT = 128
EPS = 1e-6

kernel_name = "hybrid_sgu_dilated_attn_hmoe_step"


def rmsnorm(x, g):
    xf = x.astype(jnp.float32)
    y = xf * lax.rsqrt(jnp.mean(xf * xf, axis=-1, keepdims=True) + EPS)
    return (y * g.astype(jnp.float32)).astype(x.dtype)


def rope(x, pos):
    half = HEAD_DIM // 2
    inv = ROPE_THETA ** (-jnp.arange(half, dtype=jnp.float32) * 2.0 / HEAD_DIM)
    ang = pos.astype(jnp.float32)[:, None] * inv[None, :]
    cos = jnp.cos(ang)[:, None, :]
    sin = jnp.sin(ang)[:, None, :]
    xf = x.astype(jnp.float32)
    x1, x2 = xf[..., :half], xf[..., half:]
    return jnp.concatenate([x1 * cos - x2 * sin, x2 * cos + x1 * sin], axis=-1).astype(x.dtype)


def project(h, w_in, pos):
    z = jnp.einsum('btd,dc->btc', h, w_in)
    B, T = z.shape[:2]
    cuts = [SGU_WIDTH, 2 * SGU_WIDTH, 2 * SGU_WIDTH + ATT_WIDTH, 2 * SGU_WIDTH + 2 * ATT_WIDTH]
    u, va, q, k, v = jnp.split(z, cuts, axis=-1)
    heads = lambda t: t.reshape(B, T, -1, HEAD_DIM)
    return heads(u), heads(va), rope(heads(q), pos), rope(heads(k), pos), heads(v)


def causal_spatial_weight(w_s):
    mask = jnp.tril(jnp.ones((CHUNK, CHUNK), dtype=bool))
    return jnp.where(mask[None], w_s, 0)


def sgu_prompt(u, va, g_v, w_s, b_s):
    B, S = u.shape[:2]
    vn = rmsnorm(va, g_v)
    vc = vn.reshape(B, S // CHUNK, CHUNK, SGU_GROUPS, HEAD_DIM)
    mix = jnp.einsum('gts,bcsgd->bctgd', causal_spatial_weight(w_s), vc) + b_s.T[:, :, None]
    return u * mix.reshape(u.shape)


def sgu_sample(u, va, g_v, w_s, b_s):
    T = u.shape[1]
    vn = rmsnorm(va, g_v)
    w = causal_spatial_weight(w_s)[:, :T, :T]
    mix = jnp.einsum('gts,bsgd->btgd', w, vn) + b_s[:, :T].T[None, :, :, None]
    return u * mix, vn


def dilated_attn_prompt(q, k, v, dilation, n_keys):
    B, S, H, Dh = q.shape
    span = dilation * BLOCK
    s_pad = -(-S // span) * span
    nb = s_pad // span

    def to_blocks(t):
        t = jnp.pad(t, ((0, 0), (0, s_pad - S), (0, 0), (0, 0)))
        return t.reshape(B, nb, BLOCK, dilation, H, Dh)

    def with_prev(t):
        prev = jnp.pad(t, ((0, 0), (1, 0), (0, 0), (0, 0), (0, 0), (0, 0)))[:, :-1]
        return jnp.concatenate([prev, t], axis=2)

    qb = to_blocks(q)
    kk = with_prev(to_blocks(k))
    vv = with_prev(to_blocks(v))
    s = jnp.einsum('bnqrhd,bnkrhd->bnrhqk', qb, kk,
                   preferred_element_type=jnp.float32) * (HEAD_DIM ** -0.5)
    qi = jnp.arange(BLOCK)[:, None]
    kj = jnp.arange(2 * BLOCK)[None, :]
    dist = BLOCK + qi - kj
    band = (dist >= 0) & (dist <= n_keys)
    exists = (jnp.arange(nb)[:, None, None] > 0) | (kj[None] >= BLOCK)
    mask = band[None] & exists
    s = jnp.where(mask[None, :, None, None], s, -jnp.inf)
    m = jnp.max(s, axis=-1, keepdims=True)
    e = jnp.exp(s - m)
    den = jnp.sum(e, axis=-1, keepdims=True)
    p = e / den
    lse = (m + jnp.log(den))[..., 0]
    o = jnp.einsum('bnrhqk,bnkrhd->bnqrhd', p.astype(v.dtype), vv)
    o = o.reshape(B, s_pad, H, Dh)[:, :S]
    lse = jnp.transpose(lse, (0, 1, 4, 2, 3)).reshape(B, s_pad, H)[:, :S]
    return o, lse


def dilated_attn_sample(q, k_all, v_all, buf_len, dilation, n_keys):
    T = q.shape[1]
    j = np.arange(n_keys + 1)
    idx = buf_len + np.arange(T)[:, None] - j[None, :] * dilation
    valid = jnp.asarray(idx >= 0)
    idx_c = jnp.asarray(np.maximum(idx, 0))
    kg = k_all[:, idx_c]
    vg = v_all[:, idx_c]
    s = jnp.einsum('bthd,btjhd->bhtj', q, kg,
                   preferred_element_type=jnp.float32) * (HEAD_DIM ** -0.5)
    s = jnp.where(valid[None, None], s, -jnp.inf)
    m = jnp.max(s, axis=-1, keepdims=True)
    e = jnp.exp(s - m)
    den = jnp.sum(e, axis=-1, keepdims=True)
    p = e / den
    lse = jnp.transpose((m + jnp.log(den))[..., 0], (0, 2, 1))
    o = jnp.einsum('bhtj,btjhd->bthd', p.astype(v_all.dtype), vg)
    return o, lse


def merge_by_denominator(outs, lses):
    w = jax.nn.softmax(jnp.stack(lses), axis=0)
    o = jnp.einsum('pbth,pbthd->bthd', w, jnp.stack(outs).astype(jnp.float32))
    return o.astype(outs[0].dtype)


def mix_out(a_out, b_out, w_out):
    B, T = a_out.shape[:2]
    cat = jnp.concatenate([a_out.reshape(B, T, -1), b_out.reshape(B, T, -1)], axis=-1)
    return jnp.einsum('btc,cd->btd', cat, w_out)


def hier_moe(h, w_rg, b_rg, w_re, b_re, w_gate, w_up, w_down):
    N = h.shape[0]
    hf = h.astype(jnp.float32)
    pg = jax.nn.softmax(hf @ w_rg.astype(jnp.float32) + b_rg.astype(jnp.float32), axis=-1)
    gi = jnp.argmax(pg, axis=-1)
    gp = jnp.take_along_axis(pg, gi[:, None], axis=-1)[:, 0]
    le = (hf @ w_re.astype(jnp.float32) + b_re.astype(jnp.float32)).reshape(N, N_EXPERT_GROUPS, EXPERTS_PER_GROUP)
    le_sel = jnp.take_along_axis(le, gi[:, None, None], axis=1)[:, 0]
    pe = jax.nn.softmax(le_sel, axis=-1)
    top_p, top_i = lax.top_k(pe, EXPERT_TOP_K)
    top_p = top_p / jnp.sum(top_p, axis=-1, keepdims=True)
    within = jnp.einsum('nk,nke->ne', top_p, jax.nn.one_hot(top_i, EXPERTS_PER_GROUP, dtype=jnp.float32))
    gates = (gp[:, None, None] * jax.nn.one_hot(gi, N_EXPERT_GROUPS, dtype=jnp.float32)[:, :, None]
             * within[:, None, :]).reshape(N, N_EXPERTS)
    a = jnp.einsum('nd,edf->nef', h, w_gate)
    b = jnp.einsum('nd,edf->nef', h, w_up)
    hid = jax.nn.silu(a) * b * gates[:, :, None].astype(h.dtype)
    return jnp.einsum('nef,efd->nd', hid, w_down)


def setup_inputs(seed: int = 0) -> dict:
    key = jax.random.key(seed)
    ks = jax.random.split(key, 20)
    f32 = jnp.float32
    nrm = lambda k, shape, scale: jax.random.normal(k, shape, f32) * scale
    buf = min(MAX_WINDOW, PAST_LEN)
    return {
        "x_prompt": nrm(ks[0], (BATCH, SEQ, D_MODEL), 1.0),
        "x_sample": nrm(ks[1], (DEC_BATCH, DEC_SEQ, D_MODEL), 1.0),
        "cache_win_k": nrm(ks[2], (DEPTH, DEC_BATCH, buf, ATT_HEADS, HEAD_DIM), 1.0),
        "cache_win_v": nrm(ks[3], (DEPTH, DEC_BATCH, buf, ATT_HEADS, HEAD_DIM), 1.0),
        "ln1_g": 1.0 + nrm(ks[4], (DEPTH, D_MODEL), 0.02),
        "w_in": nrm(ks[5], (DEPTH, D_MODEL, PROJ_COLS), D_MODEL ** -0.5),
        "sgu_norm_g": 1.0 + nrm(ks[6], (DEPTH, SGU_GROUPS, HEAD_DIM), 0.02),
        "sgu_w": nrm(ks[7], (DEPTH, SGU_GROUPS, CHUNK, CHUNK), CHUNK ** -0.5),
        "sgu_b": 1.0 + nrm(ks[8], (DEPTH, SGU_GROUPS, CHUNK), 0.02),
        "w_out": nrm(ks[9], (DEPTH, MIX_WIDTH, D_MODEL), MIX_WIDTH ** -0.5),
        "ln2_g": 1.0 + nrm(ks[10], (DEPTH, D_MODEL), 0.02),
        "w_router_group": nrm(ks[11], (DEPTH, D_MODEL, N_EXPERT_GROUPS), D_MODEL ** -0.5),
        "b_router_group": nrm(ks[12], (DEPTH, N_EXPERT_GROUPS), 0.01),
        "w_router_expert": nrm(ks[13], (DEPTH, D_MODEL, N_EXPERTS), D_MODEL ** -0.5),
        "b_router_expert": nrm(ks[14], (DEPTH, N_EXPERTS), 0.01),
        "w_gate": nrm(ks[15], (DEPTH, N_EXPERTS, D_MODEL, D_EXPERT), D_MODEL ** -0.5),
        "w_up": nrm(ks[16], (DEPTH, N_EXPERTS, D_MODEL, D_EXPERT), D_MODEL ** -0.5),
        "w_down": nrm(ks[17], (DEPTH, N_EXPERTS, D_EXPERT, D_MODEL), D_EXPERT ** -0.5),
        "lnf_g": 1.0 + nrm(ks[18], (D_MODEL,), 0.02),
    }


def reference(x_prompt, x_sample, cache_win_k, cache_win_v, ln1_g, w_in, sgu_norm_g, sgu_w, sgu_b,
              w_out, ln2_g, w_router_group, b_router_group, w_router_expert, b_router_expert,
              w_gate, w_up, w_down, lnf_g):
    S = x_prompt.shape[1]
    T = x_sample.shape[1]
    buf_p = min(MAX_WINDOW, S)
    buf_s = cache_win_k.shape[2]
    pos_p = jnp.arange(S)
    pos_s = PAST_LEN + jnp.arange(T)
    xp, xs = x_prompt, x_sample
    kp_l, vp_l, ks_l, vs_l, sv_l = [], [], [], [], []
    for l in range(DEPTH):
        moe = lambda t: hier_moe(t.reshape(-1, D_MODEL), w_router_group[l], b_router_group[l],
                                 w_router_expert[l], b_router_expert[l], w_gate[l], w_up[l],
                                 w_down[l]).reshape(t.shape)
        h = rmsnorm(xp, ln1_g[l])
        u, va, q, k, v = project(h, w_in[l], pos_p)
        a_out = sgu_prompt(u, va, sgu_norm_g[l], sgu_w[l], sgu_b[l])
        outs, lses = [], []
        for window, dil in DILATED_PATTERNS:
            o, lse = dilated_attn_prompt(q, k, v, dil, window // dil)
            outs.append(o)
            lses.append(lse)
        b_out = merge_by_denominator(outs, lses)
        xp = xp + mix_out(a_out, b_out, w_out[l])
        xp = xp + moe(rmsnorm(xp, ln2_g[l]))
        kp_l.append(k[:, S - buf_p:])
        vp_l.append(v[:, S - buf_p:])
        h = rmsnorm(xs, ln1_g[l])
        u, va, q, k, v = project(h, w_in[l], pos_s)
        a_out, vn = sgu_sample(u, va, sgu_norm_g[l], sgu_w[l], sgu_b[l])
        k_all = jnp.concatenate([cache_win_k[l].astype(k.dtype), k], axis=1)
        v_all = jnp.concatenate([cache_win_v[l].astype(v.dtype), v], axis=1)
        outs, lses = [], []
        for window, dil in DILATED_PATTERNS:
            o, lse = dilated_attn_sample(q, k_all, v_all, buf_s, dil, window // dil)
            outs.append(o)
            lses.append(lse)
        b_out = merge_by_denominator(outs, lses)
        xs = xs + mix_out(a_out, b_out, w_out[l])
        xs = xs + moe(rmsnorm(xs, ln2_g[l]))
        ks_l.append(k)
        vs_l.append(v)
        sv_l.append(vn)
    y_prompt = rmsnorm(xp, lnf_g)
    y_sample = rmsnorm(xs, lnf_g)
    return (y_prompt, y_sample, jnp.stack(kp_l), jnp.stack(vp_l), jnp.stack(ks_l), jnp.stack(vs_l), jnp.stack(sv_l))
```

```python
import functools

import jax
import jax.numpy as jnp
from jax import lax
from jax.experimental import pallas as pl
from jax.experimental.pallas import tpu as pltpu

F32 = jnp.float32
BF16 = jnp.bfloat16

D_MODEL = 1024
HEAD_DIM = 64
N_HEADS = 8
WIDTH = N_HEADS * HEAD_DIM
PROJ_COLS = 5 * WIDTH
CHUNK = 128
DILATIONS = (1, 4, 16)
N_KEYS = 128
MAX_WINDOW = 2048
PAST_LEN = 16384
ROPE_THETA = 10000.0
N_GROUPS = 4
EXPERTS_PER_GROUP = 8
N_EXPERTS = N_GROUPS * EXPERTS_PER_GROUP
D_EXPERT = 128
EPS = 1e-6
NEG = -1e30
ROUTER_LANES = 128
V7X_VMEM_LIMIT = 56 * 1024 * 1024


def _rmsnorm(x, g):
    return x * lax.rsqrt(jnp.mean(x * x, axis=-1, keepdims=True) + EPS) * g


def _tile_lanes(t, reps):
    return jnp.concatenate([t] * reps, axis=1)


def _rope(t, cos, sin_signed):
    lane = lax.broadcasted_iota(jnp.int32, t.shape, 1)
    first_half = (lane % HEAD_DIM) < (HEAD_DIM // 2)
    n = t.shape[1]
    partner = jnp.where(first_half, pltpu.roll(t, n - HEAD_DIM // 2, 1), pltpu.roll(t, HEAD_DIM // 2, 1))
    return t * cos + partner * sin_signed


def _dot(a, b, precise=False):
    if precise:
        return jnp.dot(a.astype(F32), b.astype(F32), preferred_element_type=F32,
                       precision=lax.Precision.HIGHEST)
    return jnp.dot(a.astype(BF16), b.astype(BF16), preferred_element_type=F32)


def _group_rmsnorm(va, ones_bd, gv, precise=False):
    ms = _dot(va * va, ones_bd, precise)
    return va * lax.rsqrt(ms + EPS) * gv


def _proj_kernel(x_ref, g1_ref, w_ref, cos_ref, sin_ref, gv_ref, ones_ref, wp_ref, bias_ref,
                 a_ref, q_ref, k_ref, v_ref, *, tm):
    h = _rmsnorm(x_ref[0], g1_ref[...]).astype(BF16)

    def proj(i):
        return jnp.dot(h, w_ref[:, i * WIDTH:(i + 1) * WIDTH], preferred_element_type=F32)

    cos = _tile_lanes(cos_ref[...], WIDTH // 128)
    sin = _tile_lanes(sin_ref[...], WIDTH // 128)
    q_ref[0] = _rope(proj(2), cos, sin) * (HEAD_DIM ** -0.5)
    k_ref[0] = _rope(proj(3), cos, sin)
    v_ref[0] = proj(4)

    u = proj(0)
    vn = _group_rmsnorm(proj(1), ones_ref[...], gv_ref[...]).astype(BF16)

    lane = lax.broadcasted_iota(jnp.int32, (CHUNK, 128), 1)
    left = lane < HEAD_DIM
    row = lax.broadcasted_iota(jnp.int32, (CHUNK, 2 * CHUNK), 0)
    col = lax.broadcasted_iota(jnp.int32, (CHUNK, 2 * CHUNK), 1)
    causal = (col % CHUNK) <= row
    zero = jnp.zeros((CHUNK, 128), BF16)
    wps = [jnp.where(causal, wp_ref[gp], 0.0).astype(BF16) for gp in range(N_HEADS // 2)]
    for c in range(tm // CHUNK):
        rows = slice(c * CHUNK, (c + 1) * CHUNK)
        mixes = []
        for gp in range(N_HEADS // 2):
            vv = vn[rows, gp * 128:(gp + 1) * 128]
            v2 = jnp.concatenate([jnp.where(left, vv, zero), jnp.where(left, zero, vv)], axis=0)
            mixes.append(jnp.dot(wps[gp], v2, preferred_element_type=F32))
        mix = jnp.concatenate(mixes, axis=1) + bias_ref[...]
        a_ref[0, rows, :] = (u[rows, :] * mix).astype(a_ref.dtype)


def _prompt_proj(x, g1, w_in_b, cos, sin, gv, ones_bd, wp, bias, *, tm=512):
    B, S, _ = x.shape
    const2 = lambda b, j: (0, 0)
    out_sds = lambda dt: jax.ShapeDtypeStruct((B, S, WIDTH), dt)
    tile = pl.BlockSpec((1, tm, WIDTH), lambda b, j: (b, j, 0))
    return pl.pallas_call(
        functools.partial(_proj_kernel, tm=tm),
        grid=(B, S // tm),
        in_specs=[
            pl.BlockSpec((1, tm, D_MODEL), lambda b, j: (b, j, 0)),
            pl.BlockSpec((1, D_MODEL), const2),
            pl.BlockSpec((D_MODEL, PROJ_COLS), const2),
            pl.BlockSpec((tm, 128), lambda b, j: (j, 0)),
            pl.BlockSpec((tm, 128), lambda b, j: (j, 0)),
            pl.BlockSpec((1, WIDTH), const2),
            pl.BlockSpec((WIDTH, WIDTH), const2),
            pl.BlockSpec((N_HEADS // 2, CHUNK, 2 * CHUNK), lambda b, j: (0, 0, 0)),
            pl.BlockSpec((CHUNK, WIDTH), const2),
        ],
        out_specs=[tile, tile, tile, tile],
        out_shape=[out_sds(BF16), out_sds(F32), out_sds(F32), out_sds(F32)],
        compiler_params=pltpu.CompilerParams(
            dimension_semantics=("arbitrary", "arbitrary"), vmem_limit_bytes=V7X_VMEM_LIMIT),
        name="prompt_proj_sgu",
    )(x, g1, w_in_b, cos, sin, gv, ones_bd, wp, bias)


def _attn_kernel(q_ref, k_ref, v_ref, o_ref, qd, kd, vd, res_o, res_l, nat_o, nat_l, *, seq):
    n_tiles = seq // CHUNK
    lane = lax.broadcasted_iota(jnp.int32, (CHUNK, 128), 1)
    left = lane < HEAD_DIM
    qi = lax.broadcasted_iota(jnp.int32, (CHUNK, 2 * CHUNK), 0)
    kj = lax.broadcasted_iota(jnp.int32, (CHUNK, 2 * CHUNK), 1)
    dist = CHUNK + qi - kj
    band = (dist >= 0) & (dist <= N_KEYS)
    zero_q = jnp.zeros((CHUNK, 128), BF16)

    kd[0:CHUNK, :] = jnp.zeros((CHUNK, 128), BF16)
    vd[0:CHUNK, :] = jnp.zeros((CHUNK, 128), BF16)

    for p, dil in enumerate(DILATIONS):
        sub = seq // dil
        nb = sub // CHUNK
        for r in range(dil):
            src = pl.ds(r, sub, stride=dil) if dil > 1 else pl.ds(0, sub)
            qd[r * sub:(r + 1) * sub, :] = q_ref[0, src, :].astype(BF16)
            kd[CHUNK + r * sub:CHUNK + (r + 1) * sub, :] = k_ref[0, src, :].astype(BF16)
            vd[CHUNK + r * sub:CHUNK + (r + 1) * sub, :] = v_ref[0, src, :].astype(BF16)

        def tile_body(t, carry, p=p, nb=nb):
            row = pl.multiple_of(t * CHUNK, CHUNK)
            qt = qd[pl.ds(row, CHUNK), :]
            k2 = kd[pl.ds(row, 2 * CHUNK), :]
            v2 = vd[pl.ds(row, 2 * CHUNK), :]
            first_key = jnp.where(t % nb == 0, CHUNK, 0)
            mask = band & (kj >= first_key)
            outs, lses = [], []
            for hsel in (left, jnp.logical_not(left)):
                qh = jnp.where(hsel, qt, zero_q)
                s = lax.dot_general(qh, k2, (((1,), (1,)), ((), ())), preferred_element_type=F32)
                s = jnp.where(mask, s, NEG)
                m = jnp.max(s, axis=1, keepdims=True)
                e = jnp.exp(s - m)
                den = jnp.sum(e, axis=1, keepdims=True)
                pv = jnp.dot(e.astype(BF16), v2, preferred_element_type=F32)
                outs.append(pv / den)
                lses.append(jnp.broadcast_to(m + jnp.log(den), (CHUNK, 128)))
            res_o[p, pl.ds(row, CHUNK), :] = jnp.where(left, outs[0], outs[1])
            res_l[p, pl.ds(row, CHUNK), :] = jnp.where(left, lses[0], lses[1])
            return carry

        lax.fori_loop(0, n_tiles, tile_body, 0)

    for p, dil in enumerate(DILATIONS):
        if dil == 1:
            continue
        sub = seq // dil
        for r in range(dil):
            nat_o[p - 1, pl.ds(r, sub, stride=dil), :] = res_o[p, r * sub:(r + 1) * sub, :]
            nat_l[p - 1, pl.ds(r, sub, stride=dil), :] = res_l[p, r * sub:(r + 1) * sub, :]

    rows_per_step = 256

    def merge_body(c, carry):
        rows = pl.ds(pl.multiple_of(c * rows_per_step, rows_per_step), rows_per_step)
        l0, l1, l2 = res_l[0, rows, :], nat_l[0, rows, :], nat_l[1, rows, :]
        top = jnp.maximum(jnp.maximum(l0, l1), l2)
        w0, w1, w2 = jnp.exp(l0 - top), jnp.exp(l1 - top), jnp.exp(l2 - top)
        num = w0 * res_o[0, rows, :] + w1 * nat_o[0, rows, :] + w2 * nat_o[1, rows, :]
        o_ref[0, rows, :] = (num / (w0 + w1 + w2)).astype(o_ref.dtype)
        return carry

    lax.fori_loop(0, seq // rows_per_step, merge_body, 0)


def _prompt_attention(q, k, v):
    B, S, _ = q.shape
    blk = pl.BlockSpec((1, S, 128), lambda b, hp: (b, 0, hp))
    return pl.pallas_call(
        functools.partial(_attn_kernel, seq=S),
        grid=(B, WIDTH // 128),
        in_specs=[blk, blk, blk],
        out_specs=blk,
        out_shape=jax.ShapeDtypeStruct((B, S, WIDTH), BF16),
        scratch_shapes=[
            pltpu.VMEM((S, 128), BF16),
            pltpu.VMEM((S + CHUNK, 128), BF16),
            pltpu.VMEM((S + CHUNK, 128), BF16),
            pltpu.VMEM((len(DILATIONS), S, 128), F32),
            pltpu.VMEM((len(DILATIONS), S, 128), F32),
            pltpu.VMEM((len(DILATIONS) - 1, S, 128), F32),
            pltpu.VMEM((len(DILATIONS) - 1, S, 128), F32),
        ],
        compiler_params=pltpu.CompilerParams(
            dimension_semantics=("arbitrary", "arbitrary"), vmem_limit_bytes=V7X_VMEM_LIMIT),
        name="prompt_dilated_attention",
    )(q, k, v)


def _route(logits):
    lane = lax.broadcasted_iota(jnp.int32, logits.shape, 1)
    big = jnp.int32(ROUTER_LANES)
    lg = jnp.where(lane < N_GROUPS, logits, NEG)
    gmax = jnp.max(lg, axis=1, keepdims=True)
    gp = 1.0 / jnp.sum(jnp.exp(lg - gmax), axis=1, keepdims=True)
    gi = jnp.min(jnp.where(lg == gmax, lane, big), axis=1, keepdims=True)
    lo = N_GROUPS + EXPERTS_PER_GROUP * gi
    le = jnp.where((lane >= lo) & (lane < lo + EXPERTS_PER_GROUP), logits, NEG)
    m1 = jnp.max(le, axis=1, keepdims=True)
    i1 = jnp.min(jnp.where(le == m1, lane, big), axis=1, keepdims=True)
    le2 = jnp.where(lane == i1, NEG, le)
    m2 = jnp.max(le2, axis=1, keepdims=True)
    i2 = jnp.min(jnp.where(le2 == m2, lane, big), axis=1, keepdims=True)
    e2 = jnp.exp(m2 - m1)
    w1 = 1.0 / (1.0 + e2)
    w2 = e2 / (1.0 + e2)
    return jnp.where(lane == i1, gp * w1, jnp.where(lane == i2, gp * w2, 0.0))


def _mix_router_kernel(a_ref, b_ref, x_ref, wo_ref, g2_ref, wr_ref, br_ref, xp_ref, h2_ref, gates_ref, *, precise):
    mixed = (_dot(a_ref[...], wo_ref[0:WIDTH, :], precise)
             + _dot(b_ref[...], wo_ref[WIDTH:2 * WIDTH, :], precise))
    xp = x_ref[...] + mixed
    xp_ref[...] = xp
    h2 = _rmsnorm(xp, g2_ref[...])
    h2_ref[...] = h2.astype(h2_ref.dtype)
    logits = jnp.dot(h2, wr_ref[...], preferred_element_type=F32,
                     precision=lax.Precision.HIGHEST) + br_ref[...]
    gates_ref[...] = _route(logits)


def _mix_router(a, b, x, w_out, g2, w_router, b_router, *, tm, precise=False):
    n = x.shape[0]
    const = lambda i: (0, 0)
    row_blk = lambda w: pl.BlockSpec((tm, w), lambda i: (i, 0))
    return pl.pallas_call(
        functools.partial(_mix_router_kernel, precise=precise),
        grid=(n // tm,),
        in_specs=[row_blk(WIDTH), row_blk(WIDTH), row_blk(D_MODEL),
                  pl.BlockSpec((2 * WIDTH, D_MODEL), const),
                  pl.BlockSpec((1, D_MODEL), const),
                  pl.BlockSpec((D_MODEL, ROUTER_LANES), const),
                  pl.BlockSpec((1, ROUTER_LANES), const)],
        out_specs=[row_blk(D_MODEL), row_blk(D_MODEL), row_blk(ROUTER_LANES)],
        out_shape=[jax.ShapeDtypeStruct((n, D_MODEL), F32),
                   jax.ShapeDtypeStruct((n, D_MODEL), BF16),
                   jax.ShapeDtypeStruct((n, ROUTER_LANES), F32)],
        compiler_params=pltpu.CompilerParams(
            dimension_semantics=("arbitrary",), vmem_limit_bytes=V7X_VMEM_LIMIT),
        name="outproj_router",
    )(a, b, x, w_out, g2, w_router, b_router)


def _moe_kernel(h_ref, gates_ref, xp_ref, w1_ref, w2_ref, gf_ref, y_ref, acc_ref):
    e = pl.program_id(1)

    @pl.when(e == 0)
    def _():
        acc_ref[...] = xp_ref[...]

    ab = jnp.dot(h_ref[...], w1_ref[0], preferred_element_type=F32)
    a = ab[:, :D_EXPERT]
    b = ab[:, D_EXPERT:]
    gates = gates_ref[...]
    lane = lax.broadcasted_iota(jnp.int32, gates.shape, 1)
    gcol = jnp.sum(jnp.where(lane == e + N_GROUPS, gates, 0.0), axis=1, keepdims=True)
    hid = a * (1.0 / (1.0 + jnp.exp(-a))) * b * gcol
    acc_ref[...] += jnp.dot(hid.astype(BF16), w2_ref[0], preferred_element_type=F32)

    @pl.when(e == N_EXPERTS - 1)
    def _():
        y_ref[...] = _rmsnorm(acc_ref[...], gf_ref[...])


def _moe(h2, gates, xp, w1_b, w2_b, gf, *, tm):
    n = h2.shape[0]
    row_blk = lambda w: pl.BlockSpec((tm, w), lambda i, e: (i, 0))
    return pl.pallas_call(
        _moe_kernel,
        grid=(n // tm, N_EXPERTS),
        in_specs=[row_blk(D_MODEL), row_blk(ROUTER_LANES), row_blk(D_MODEL),
                  pl.BlockSpec((1, D_MODEL, 2 * D_EXPERT), lambda i, e: (e, 0, 0)),
                  pl.BlockSpec((1, D_EXPERT, D_MODEL), lambda i, e: (e, 0, 0)),
                  pl.BlockSpec((1, D_MODEL), lambda i, e: (0, 0))],
        out_specs=row_blk(D_MODEL),
        out_shape=jax.ShapeDtypeStruct((n, D_MODEL), F32),
        scratch_shapes=[pltpu.VMEM((tm, D_MODEL), F32)],
        compiler_params=pltpu.CompilerParams(
            dimension_semantics=("arbitrary", "arbitrary"), vmem_limit_bytes=V7X_VMEM_LIMIT),
        name="moe_experts",
    )(h2, gates, xp, w1_b, w2_b, gf)


def _sample_proj_kernel(x_ref, g1_ref, w_ref, cos_ref, sin_ref, gv_ref, ones_ref, w00_ref, b0_ref,
                        rep_ref, fold_ref, a_ref, k_ref, v_ref, vn_ref, q2_ref, k2_ref, v2_ref):
    h = _rmsnorm(x_ref[...], g1_ref[...])

    def proj(i):
        return _dot(h, w_ref[:, i * WIDTH:(i + 1) * WIDTH], precise=True)

    cos = _tile_lanes(cos_ref[...], WIDTH // 128)
    sin = _tile_lanes(sin_ref[...], WIDTH // 128)
    q = _rope(proj(2), cos, sin) * (HEAD_DIM ** -0.5)
    k = _rope(proj(3), cos, sin)
    v = proj(4)
    vn = _group_rmsnorm(proj(1), ones_ref[...], gv_ref[...], precise=True)
    a_ref[...] = proj(0) * (w00_ref[...] * vn + b0_ref[...])
    k_ref[...] = k
    v_ref[...] = v
    vn_ref[...] = vn

    n_rep = rep_ref.shape[0]
    r_idx = lax.broadcasted_iota(jnp.int32, (n_rep, WIDTH), 0)
    l_idx = lax.broadcasted_iota(jnp.int32, (n_rep, WIDTH), 1)
    own = (l_idx // HEAD_DIM) == (r_idx % N_HEADS)
    for src, dst in ((q, q2_ref), (k, k2_ref), (v, v2_ref)):
        rep = _dot(rep_ref[...], src, precise=True)
        dst[...] = _dot(jnp.where(own, rep, 0.0), fold_ref[...], precise=True)


def _sample_proj(x, g1, w_in, cos, sin, gv, ones_bd, w00, b0, rep, fold):
    bd = x.shape[0]
    sds = lambda r, c: jax.ShapeDtypeStruct((r, c), F32)
    return pl.pallas_call(
        _sample_proj_kernel,
        out_shape=[sds(bd, WIDTH), sds(bd, WIDTH), sds(bd, WIDTH), sds(bd, WIDTH),
                   sds(bd * N_HEADS, HEAD_DIM), sds(bd * N_HEADS, HEAD_DIM), sds(bd * N_HEADS, HEAD_DIM)],
        compiler_params=pltpu.CompilerParams(vmem_limit_bytes=V7X_VMEM_LIMIT),
        name="sample_proj",
    )(x, g1, w_in, cos, sin, gv, ones_bd, w00, b0, rep, fold)


def _sample_attn_kernel(q2_ref, k2_ref, v2_ref, k1_ref, v1_ref, k4_ref, v4_ref, k16_ref, v16_ref, o_ref):
    b = pl.program_id(0)
    rows = pl.ds(pl.multiple_of(b * N_HEADS, N_HEADS), N_HEADS)
    qb = q2_ref[rows, :]
    kb = k2_ref[rows, :]
    vb = v2_ref[rows, :]
    s_self = jnp.sum(qb * kb, axis=1, keepdims=True)

    outs, lses = [], []
    for kc_ref, vc_ref in ((k1_ref, v1_ref), (k4_ref, v4_ref), (k16_ref, v16_ref)):
        s = jnp.sum(kc_ref[...] * qb[None], axis=2, keepdims=True)
        m = jnp.maximum(jnp.max(s, axis=0), s_self)
        e = jnp.exp(s - m[None])
        e_self = jnp.exp(s_self - m)
        den = jnp.sum(e, axis=0) + e_self
        pv = jnp.sum(e * vc_ref[...], axis=0) + e_self * vb
        outs.append(pv / den)
        lses.append(m + jnp.log(den))
    top = jnp.maximum(jnp.maximum(lses[0], lses[1]), lses[2])
    ws = [jnp.exp(l - top) for l in lses]
    o_ref[...] = (ws[0] * outs[0] + ws[1] * outs[1] + ws[2] * outs[2]) / (ws[0] + ws[1] + ws[2])


def _sample_attention(q2, k2, v2, cache_k, cache_v):
    bd, win = cache_k.shape[0], cache_k.shape[1]
    full = pl.BlockSpec((bd * N_HEADS, HEAD_DIM), lambda b: (0, 0))
    views, specs = [], []
    for dil in DILATIONS:
        n_sub = win // dil
        last_blk = n_sub // N_KEYS - 1
        for c in (cache_k, cache_v):
            views.append(c.reshape(bd, n_sub, dil, N_HEADS, HEAD_DIM))
            specs.append(pl.BlockSpec((None, N_KEYS, None, N_HEADS, HEAD_DIM),
                                      lambda b, last_blk=last_blk: (b, last_blk, 0, 0, 0)))
    return pl.pallas_call(
        _sample_attn_kernel,
        grid=(bd,),
        in_specs=[full, full, full] + specs,
        out_specs=pl.BlockSpec((N_HEADS, HEAD_DIM), lambda b: (b, 0)),
        out_shape=jax.ShapeDtypeStruct((bd * N_HEADS, HEAD_DIM), F32),
        compiler_params=pltpu.CompilerParams(
            dimension_semantics=("arbitrary",), vmem_limit_bytes=V7X_VMEM_LIMIT),
        name="sample_window_attention",
    )(q2, k2, v2, *views)


def _rope_tables(pos):
    half = HEAD_DIM // 2
    inv = ROPE_THETA ** (-jnp.arange(half, dtype=F32) * 2.0 / HEAD_DIM)
    ang = pos.astype(F32)[:, None] * inv[None, :]
    cos, sin = jnp.cos(ang), jnp.sin(ang)
    cos128 = jnp.concatenate([cos, cos, cos, cos], axis=1)
    sin128 = jnp.concatenate([-sin, sin, -sin, sin], axis=1)
    return cos128, sin128


def kernel(x_prompt, x_sample, cache_win_k, cache_win_v, ln1_g, w_in, sgu_norm_g, sgu_w, sgu_b, w_out, ln2_g,
           w_router_group, b_router_group, w_router_expert, b_router_expert, w_gate, w_up, w_down, lnf_g):
    depth = w_in.shape[0]
    assert depth == 1 and x_sample.shape[1] == 1
    B, S, _ = x_prompt.shape
    bd = x_sample.shape[0]
    win = cache_win_k.shape[2]
    assert S % (max(DILATIONS) * CHUNK) == 0 and win >= max(DILATIONS) * N_KEYS and PAST_LEN % CHUNK == 0
    l = 0

    w_in_b = w_in[l].astype(BF16)
    w_out_b = w_out[l].astype(BF16)
    w1_b = jnp.concatenate([w_gate[l], w_up[l]], axis=-1).astype(BF16)
    w2_b = w_down[l].astype(BF16)
    pad = ROUTER_LANES - N_GROUPS - N_EXPERTS
    w_router = jnp.pad(jnp.concatenate([w_router_group[l], w_router_expert[l]], axis=1), ((0, 0), (0, pad)))
    b_router = jnp.pad(jnp.concatenate([b_router_group[l], b_router_expert[l]]), (0, pad))[None, :]
    g1 = ln1_g[l][None, :]
    g2 = ln2_g[l][None, :]
    gf = lnf_g[None, :]
    gv = sgu_norm_g[l].reshape(1, WIDTH)
    grp = jnp.arange(WIDTH) // HEAD_DIM
    ones_bd = jnp.where(grp[:, None] == grp[None, :], 1.0 / HEAD_DIM, 0.0).astype(BF16)
    wp = jnp.concatenate([sgu_w[l][0::2], sgu_w[l][1::2]], axis=-1)
    bias = jnp.repeat(sgu_b[l].T, HEAD_DIM, axis=1)
    w00 = jnp.repeat(sgu_w[l][:, 0, 0], HEAD_DIM)[None, :]
    b0 = jnp.repeat(sgu_b[l][:, 0], HEAD_DIM)[None, :]

    cos_p, sin_p = _rope_tables(jnp.arange(S))
    a_p, q_p, k_p, v_p = _prompt_proj(x_prompt, g1, w_in_b, cos_p, sin_p, gv, ones_bd, wp, bias)
    b_p = _prompt_attention(q_p, k_p, v_p)
    n = B * S
    xp2, h2, gates = _mix_router(a_p.reshape(n, WIDTH), b_p.reshape(n, WIDTH), x_prompt.reshape(n, D_MODEL),
                                 w_out_b, g2, w_router, b_router, tm=512)
    y_prompt = _moe(h2, gates, xp2, w1_b, w2_b, gf, tm=1024).reshape(B, S, D_MODEL)
    buf_p = min(MAX_WINDOW, S)
    new_k_p = k_p[:, S - buf_p:].reshape(1, B, buf_p, N_HEADS, HEAD_DIM)
    new_v_p = v_p[:, S - buf_p:].reshape(1, B, buf_p, N_HEADS, HEAD_DIM)

    cos_s, sin_s = _rope_tables(PAST_LEN + jnp.arange(1))
    rep = (jnp.arange(bd * N_HEADS)[:, None] // N_HEADS == jnp.arange(bd)[None, :]).astype(F32)
    fold = (jnp.arange(WIDTH)[:, None] % HEAD_DIM == jnp.arange(HEAD_DIM)[None, :]).astype(F32)
    xs = x_sample.reshape(bd, D_MODEL)
    a_s, k_s, v_s, vn_s, q2, k2, v2 = _sample_proj(xs, g1, w_in[l], cos_s, sin_s, gv, ones_bd, w00, b0, rep, fold)
    o2 = _sample_attention(q2, k2, v2, cache_win_k[l], cache_win_v[l])
    b_s = o2.reshape(bd, WIDTH)
    xs2, hs2, gates_s = _mix_router(a_s, b_s, xs, w_out[l], g2, w_router, b_router, tm=bd, precise=True)
    y_sample = _moe(hs2, gates_s, xs2, w1_b, w2_b, gf, tm=bd).reshape(bd, 1, D_MODEL)

    shape_s = (1, bd, 1, N_HEADS, HEAD_DIM)
    return (y_prompt, y_sample, new_k_p, new_v_p,
            k_s.reshape(shape_s), v_s.reshape(shape_s), vn_s.reshape(shape_s))
```

```python
import functools

import jax
import jax.numpy as jnp
from jax import lax
from jax.experimental import pallas as pl
from jax.experimental.pallas import tpu as pltpu

F32 = jnp.float32
BF16 = jnp.bfloat16

D_MODEL = 1024
HEAD_DIM = 64
N_HEADS = 8
WIDTH = N_HEADS * HEAD_DIM
PROJ_COLS = 5 * WIDTH
CHUNK = 128
DILATIONS = (1, 4, 16)
N_KEYS = 128
MAX_WINDOW = 2048
PAST_LEN = 16384
ROPE_THETA = 10000.0
N_GROUPS = 4
EXPERTS_PER_GROUP = 8
N_EXPERTS = N_GROUPS * EXPERTS_PER_GROUP
D_EXPERT = 128
EPS = 1e-6
NEG = -1e30
TILES_PER_STEP = 8
MOE_BLOCK = 512
MOE_ROW_ALIGN = 16
MOE_CHUNK = 64
MOE_EXPERTS_PER_STEP = 16
MOE_SORT_ROWS = -(-(2 * MOE_BLOCK + N_EXPERTS * (MOE_ROW_ALIGN - 1)) // 512) * 512
ROUTER_LANES = 128
V7X_VMEM_LIMIT = 56 * 1024 * 1024


def _rmsnorm(x, g):
    return x * lax.rsqrt(jnp.mean(x * x, axis=-1, keepdims=True) + EPS) * g


def _tile_lanes(t, reps):
    return jnp.concatenate([t] * reps, axis=1)


def _rope(t, cos, sin_signed):
    lane = lax.broadcasted_iota(jnp.int32, t.shape, 1)
    first_half = (lane % HEAD_DIM) < (HEAD_DIM // 2)
    n = t.shape[1]
    partner = jnp.where(first_half, pltpu.roll(t, n - HEAD_DIM // 2, 1), pltpu.roll(t, HEAD_DIM // 2, 1))
    return t * cos + partner * sin_signed


def _dot(a, b, precise=False):
    if precise:
        return jnp.dot(a.astype(F32), b.astype(F32), preferred_element_type=F32,
                       precision=lax.Precision.HIGHEST)
    return jnp.dot(a.astype(BF16), b.astype(BF16), preferred_element_type=F32)


def _group_rmsnorm(va, ones_bd, gv, precise=False):
    ms = _dot(va * va, ones_bd, precise)
    return va * lax.rsqrt(ms + EPS) * gv


def _proj_kernel(x_ref, g1_ref, w_ref, cos_ref, sin_ref, gv_ref, ones_ref, wp_ref, bias_ref,
                 a_ref, q_ref, k_ref, v_ref, kt_ref, vt_ref, *, tm, first_win_tile):
    h = _rmsnorm(x_ref[0], g1_ref[...]).astype(BF16)

    def proj(i):
        return jnp.dot(h, w_ref[:, i * WIDTH:(i + 1) * WIDTH], preferred_element_type=F32)

    cos = _tile_lanes(cos_ref[...], WIDTH // 128)
    sin = _tile_lanes(sin_ref[...], WIDTH // 128)
    q_ref[0] = _rope(proj(2), cos, sin) * (HEAD_DIM ** -0.5)
    k = _rope(proj(3), cos, sin)
    v = proj(4)
    k_ref[0] = k
    v_ref[0] = v

    @pl.when(pl.program_id(1) >= first_win_tile)
    def _():
        kt_ref[0] = k.T
        vt_ref[0] = v.T

    u = proj(0)
    vn = _group_rmsnorm(proj(1), ones_ref[...], gv_ref[...]).astype(BF16)

    lane = lax.broadcasted_iota(jnp.int32, (CHUNK, 128), 1)
    left = lane < HEAD_DIM
    row = lax.broadcasted_iota(jnp.int32, (CHUNK, 2 * CHUNK), 0)
    col = lax.broadcasted_iota(jnp.int32, (CHUNK, 2 * CHUNK), 1)
    causal = (col % CHUNK) <= row
    zero = jnp.zeros((CHUNK, 128), BF16)
    wps = [jnp.where(causal, wp_ref[gp], 0.0).astype(BF16) for gp in range(N_HEADS // 2)]
    for c in range(tm // CHUNK):
        rows = slice(c * CHUNK, (c + 1) * CHUNK)
        mixes = []
        for gp in range(N_HEADS // 2):
            vv = vn[rows, gp * 128:(gp + 1) * 128]
            v2 = jnp.concatenate([jnp.where(left, vv, zero), jnp.where(left, zero, vv)], axis=0)
            mixes.append(jnp.dot(wps[gp], v2, preferred_element_type=F32))
        mix = jnp.concatenate(mixes, axis=1) + bias_ref[...]
        a_ref[0, rows, :] = (u[rows, :] * mix).astype(a_ref.dtype)


def _prompt_proj(x, g1, w_in_b, cos, sin, gv, ones_bd, wp, bias, *, tm=512):
    B, S, _ = x.shape
    const2 = lambda b, j: (0, 0)
    out_sds = lambda dt: jax.ShapeDtypeStruct((B, S, WIDTH), dt)
    tile = pl.BlockSpec((1, tm, WIDTH), lambda b, j: (b, j, 0))
    win = min(MAX_WINDOW, S)
    first_win_tile = (S - win) // tm
    tile_t = pl.BlockSpec((1, WIDTH, tm), lambda b, j: (b, 0, jnp.maximum(j - first_win_tile, 0)))
    win_sds = jax.ShapeDtypeStruct((B, WIDTH, win), F32)
    return pl.pallas_call(
        functools.partial(_proj_kernel, tm=tm, first_win_tile=first_win_tile),
        grid=(B, S // tm),
        in_specs=[
            pl.BlockSpec((1, tm, D_MODEL), lambda b, j: (b, j, 0)),
            pl.BlockSpec((1, D_MODEL), const2),
            pl.BlockSpec((D_MODEL, PROJ_COLS), const2),
            pl.BlockSpec((tm, 128), lambda b, j: (j, 0)),
            pl.BlockSpec((tm, 128), lambda b, j: (j, 0)),
            pl.BlockSpec((1, WIDTH), const2),
            pl.BlockSpec((WIDTH, WIDTH), const2),
            pl.BlockSpec((N_HEADS // 2, CHUNK, 2 * CHUNK), lambda b, j: (0, 0, 0)),
            pl.BlockSpec((CHUNK, WIDTH), const2),
        ],
        out_specs=[tile, tile, tile, tile, tile_t, tile_t],
        out_shape=[out_sds(BF16), out_sds(F32), out_sds(F32), out_sds(F32), win_sds, win_sds],
        compiler_params=pltpu.CompilerParams(
            dimension_semantics=("arbitrary", "arbitrary"), vmem_limit_bytes=V7X_VMEM_LIMIT),
        name="prompt_proj_sgu",
    )(x, g1, w_in_b, cos, sin, gv, ones_bd, wp, bias)


def _attn_kernel(q_ref, k_ref, v_ref, o_ref, qd, kd, vd, res_o, res_l, nat_o, nat_l, *, seq):
    n_tiles = seq // CHUNK
    lane = lax.broadcasted_iota(jnp.int32, (CHUNK, 128), 1)
    left = lane < HEAD_DIM
    qi = lax.broadcasted_iota(jnp.int32, (CHUNK, 2 * CHUNK), 0)
    kj = lax.broadcasted_iota(jnp.int32, (CHUNK, 2 * CHUNK), 1)
    dist = CHUNK + qi - kj
    band = (dist >= 0) & (dist <= N_KEYS)
    zero_q = jnp.zeros((CHUNK, 128), BF16)

    kd[0:CHUNK, :] = jnp.zeros((CHUNK, 128), BF16)
    vd[0:CHUNK, :] = jnp.zeros((CHUNK, 128), BF16)

    for p, dil in enumerate(DILATIONS):
        sub = seq // dil
        nb = sub // CHUNK
        for r in range(dil):
            src = pl.ds(r, sub, stride=dil) if dil > 1 else pl.ds(0, sub)
            qd[r * sub:(r + 1) * sub, :] = q_ref[0, src, :].astype(BF16)
            kd[CHUNK + r * sub:CHUNK + (r + 1) * sub, :] = k_ref[0, src, :].astype(BF16)
            vd[CHUNK + r * sub:CHUNK + (r + 1) * sub, :] = v_ref[0, src, :].astype(BF16)

        def tile_body(t, p=p, nb=nb):
            row = pl.multiple_of(t * CHUNK, CHUNK)
            qt = qd[pl.ds(row, CHUNK), :]
            k2 = kd[pl.ds(row, 2 * CHUNK), :]
            v2 = vd[pl.ds(row, 2 * CHUNK), :]
            first_key = jnp.where(t % nb == 0, CHUNK, 0)
            mask = band & (kj >= first_key)
            outs, lses = [], []
            for hsel in (left, jnp.logical_not(left)):
                qh = jnp.where(hsel, qt, zero_q)
                s = lax.dot_general(qh, k2, (((1,), (1,)), ((), ())), preferred_element_type=F32)
                s = jnp.where(mask, s, NEG)
                m = jnp.max(s, axis=1, keepdims=True)
                e = jnp.exp(s - m)
                den = jnp.sum(e, axis=1, keepdims=True)
                pv = jnp.dot(e.astype(BF16), v2, preferred_element_type=F32)
                outs.append(pv / den)
                lses.append(jnp.broadcast_to(m + jnp.log(den), (CHUNK, 128)))
            res_o[p, pl.ds(row, CHUNK), :] = jnp.where(left, outs[0], outs[1])
            res_l[p, pl.ds(row, CHUNK), :] = jnp.where(left, lses[0], lses[1])

        def group_body(g, carry, tile_body=tile_body):
            for i in range(TILES_PER_STEP):
                tile_body(g * TILES_PER_STEP + i)
            return carry

        lax.fori_loop(0, n_tiles // TILES_PER_STEP, group_body, 0)

    for p, dil in enumerate(DILATIONS):
        if dil == 1:
            continue
        sub = seq // dil
        for r in range(dil):
            nat_o[p - 1, pl.ds(r, sub, stride=dil), :] = res_o[p, r * sub:(r + 1) * sub, :]
            nat_l[p - 1, pl.ds(r, sub, stride=dil), :] = res_l[p, r * sub:(r + 1) * sub, :]

    rows_per_step = 256

    def merge_body(c, carry):
        rows = pl.ds(pl.multiple_of(c * rows_per_step, rows_per_step), rows_per_step)
        l0, l1, l2 = res_l[0, rows, :], nat_l[0, rows, :], nat_l[1, rows, :]
        top = jnp.maximum(jnp.maximum(l0, l1), l2)
        w0, w1, w2 = jnp.exp(l0 - top), jnp.exp(l1 - top), jnp.exp(l2 - top)
        num = w0 * res_o[0, rows, :] + w1 * nat_o[0, rows, :] + w2 * nat_o[1, rows, :]
        o_ref[0, rows, :] = (num / (w0 + w1 + w2)).astype(o_ref.dtype)
        return carry

    lax.fori_loop(0, seq // rows_per_step, merge_body, 0)


def _prompt_attention(q, k, v):
    B, S, _ = q.shape
    blk = pl.BlockSpec((1, S, 128), lambda b, hp: (b, 0, hp))
    return pl.pallas_call(
        functools.partial(_attn_kernel, seq=S),
        grid=(B, WIDTH // 128),
        in_specs=[blk, blk, blk],
        out_specs=blk,
        out_shape=jax.ShapeDtypeStruct((B, S, WIDTH), BF16),
        scratch_shapes=[
            pltpu.VMEM((S, 128), BF16),
            pltpu.VMEM((S + CHUNK, 128), BF16),
            pltpu.VMEM((S + CHUNK, 128), BF16),
            pltpu.VMEM((len(DILATIONS), S, 128), F32),
            pltpu.VMEM((len(DILATIONS), S, 128), F32),
            pltpu.VMEM((len(DILATIONS) - 1, S, 128), F32),
            pltpu.VMEM((len(DILATIONS) - 1, S, 128), F32),
        ],
        compiler_params=pltpu.CompilerParams(
            dimension_semantics=("arbitrary", "arbitrary"), vmem_limit_bytes=V7X_VMEM_LIMIT),
        name="prompt_dilated_attention",
    )(q, k, v)


def _route(logits):
    lane = lax.broadcasted_iota(jnp.int32, logits.shape, 1)
    big = jnp.int32(ROUTER_LANES)
    lg = jnp.where(lane < N_GROUPS, logits, NEG)
    gmax = jnp.max(lg, axis=1, keepdims=True)
    gp = 1.0 / jnp.sum(jnp.exp(lg - gmax), axis=1, keepdims=True)
    gi = jnp.min(jnp.where(lg == gmax, lane, big), axis=1, keepdims=True)
    lo = N_GROUPS + EXPERTS_PER_GROUP * gi
    le = jnp.where((lane >= lo) & (lane < lo + EXPERTS_PER_GROUP), logits, NEG)
    m1 = jnp.max(le, axis=1, keepdims=True)
    i1 = jnp.min(jnp.where(le == m1, lane, big), axis=1, keepdims=True)
    le2 = jnp.where(lane == i1, NEG, le)
    m2 = jnp.max(le2, axis=1, keepdims=True)
    i2 = jnp.min(jnp.where(le2 == m2, lane, big), axis=1, keepdims=True)
    e2 = jnp.exp(m2 - m1)
    w1 = 1.0 / (1.0 + e2)
    w2 = e2 / (1.0 + e2)
    return jnp.where(lane == i1, gp * w1, jnp.where(lane == i2, gp * w2, 0.0))


def _mix_router_kernel(a_ref, b_ref, x_ref, wo_ref, g2_ref, wr_ref, br_ref, xp_ref, h2_ref, gates_ref, *, precise):
    mixed = (_dot(a_ref[...], wo_ref[0:WIDTH, :], precise)
             + _dot(b_ref[...], wo_ref[WIDTH:2 * WIDTH, :], precise))
    xp = x_ref[...] + mixed
    xp_ref[...] = xp
    h2 = _rmsnorm(xp, g2_ref[...])
    h2_ref[...] = h2.astype(h2_ref.dtype)
    logits = jnp.dot(h2, wr_ref[...], preferred_element_type=F32,
                     precision=lax.Precision.HIGHEST) + br_ref[...]
    gates_ref[...] = _route(logits)


def _mix_router(a, b, x, w_out, g2, w_router, b_router, *, tm, precise=False):
    n = x.shape[0]
    const = lambda i: (0, 0)
    row_blk = lambda w: pl.BlockSpec((tm, w), lambda i: (i, 0))
    return pl.pallas_call(
        functools.partial(_mix_router_kernel, precise=precise),
        grid=(n // tm,),
        in_specs=[row_blk(WIDTH), row_blk(WIDTH), row_blk(D_MODEL),
                  pl.BlockSpec((2 * WIDTH, D_MODEL), const),
                  pl.BlockSpec((1, D_MODEL), const),
                  pl.BlockSpec((D_MODEL, ROUTER_LANES), const),
                  pl.BlockSpec((1, ROUTER_LANES), const)],
        out_specs=[row_blk(D_MODEL), row_blk(D_MODEL), row_blk(ROUTER_LANES)],
        out_shape=[jax.ShapeDtypeStruct((n, D_MODEL), F32),
                   jax.ShapeDtypeStruct((n, D_MODEL), BF16),
                   jax.ShapeDtypeStruct((n, ROUTER_LANES), F32)],
        compiler_params=pltpu.CompilerParams(
            dimension_semantics=("arbitrary",), vmem_limit_bytes=V7X_VMEM_LIMIT),
        name="outproj_router",
    )(a, b, x, w_out, g2, w_router, b_router)


def _moe_kernel(h_ref, gates_ref, xp_ref, w1_ref, w2_ref, gf_ref, y_ref, acc_ref):
    e = pl.program_id(1)

    @pl.when(e == 0)
    def _():
        acc_ref[...] = xp_ref[...]

    ab = jnp.dot(h_ref[...], w1_ref[0], preferred_element_type=F32)
    a = ab[:, :D_EXPERT]
    b = ab[:, D_EXPERT:]
    gates = gates_ref[...]
    lane = lax.broadcasted_iota(jnp.int32, gates.shape, 1)
    gcol = jnp.sum(jnp.where(lane == e + N_GROUPS, gates, 0.0), axis=1, keepdims=True)
    hid = a * (1.0 / (1.0 + jnp.exp(-a))) * b * gcol
    acc_ref[...] += jnp.dot(hid.astype(BF16), w2_ref[0], preferred_element_type=F32)

    @pl.when(e == N_EXPERTS - 1)
    def _():
        y_ref[...] = _rmsnorm(acc_ref[...], gf_ref[...])


def _moe(h2, gates, xp, w1_b, w2_b, gf, *, tm):
    n = h2.shape[0]
    row_blk = lambda w: pl.BlockSpec((tm, w), lambda i, e: (i, 0))
    return pl.pallas_call(
        _moe_kernel,
        grid=(n // tm, N_EXPERTS),
        in_specs=[row_blk(D_MODEL), row_blk(ROUTER_LANES), row_blk(D_MODEL),
                  pl.BlockSpec((1, D_MODEL, 2 * D_EXPERT), lambda i, e: (e, 0, 0)),
                  pl.BlockSpec((1, D_EXPERT, D_MODEL), lambda i, e: (e, 0, 0)),
                  pl.BlockSpec((1, D_MODEL), lambda i, e: (0, 0))],
        out_specs=row_blk(D_MODEL),
        out_shape=jax.ShapeDtypeStruct((n, D_MODEL), F32),
        scratch_shapes=[pltpu.VMEM((tm, D_MODEL), F32)],
        compiler_params=pltpu.CompilerParams(
            dimension_semantics=("arbitrary", "arbitrary"), vmem_limit_bytes=V7X_VMEM_LIMIT),
        name="moe_experts",
    )(h2, gates, xp, w1_b, w2_b, gf)


def _nt_dot(w, t):
    return lax.dot_general(w, t, (((1,), (1,)), ((), ())), preferred_element_type=F32)


def _route_t(logits_t):
    row = lax.broadcasted_iota(jnp.int32, logits_t.shape, 0)
    big = jnp.int32(ROUTER_LANES)
    lg = jnp.where(row < N_GROUPS, logits_t, NEG)
    gmax = jnp.max(lg, axis=0, keepdims=True)
    gp = 1.0 / jnp.sum(jnp.exp(lg - gmax), axis=0, keepdims=True)
    gi = jnp.min(jnp.where(lg == gmax, row, big), axis=0, keepdims=True)
    lo = N_GROUPS + EXPERTS_PER_GROUP * gi
    le = jnp.where((row >= lo) & (row < lo + EXPERTS_PER_GROUP), logits_t, NEG)
    m1 = jnp.max(le, axis=0, keepdims=True)
    i1 = jnp.min(jnp.where(le == m1, row, big), axis=0, keepdims=True)
    le2 = jnp.where(row == i1, NEG, le)
    m2 = jnp.max(le2, axis=0, keepdims=True)
    i2 = jnp.min(jnp.where(le2 == m2, row, big), axis=0, keepdims=True)
    e2 = jnp.exp(m2 - m1)
    return i1 - N_GROUPS, i2 - N_GROUPS, gp / (1.0 + e2), gp * e2 / (1.0 + e2)


def _mix_route_sort_kernel(a_ref, b_ref, x_ref, wo_ref, g2_ref, wrh_ref, wrl_ref, brc_ref, tri_ref, ltri_ref,
                           xp_ref, h2_ref, mrow_ref, mcol_ref, tab_ref):
    t = MOE_BLOCK
    xp = x_ref[...] + _dot(a_ref[...], wo_ref[0:WIDTH, :]) + _dot(b_ref[...], wo_ref[WIDTH:2 * WIDTH, :])
    xp_ref[...] = xp
    h2 = _rmsnorm(xp, g2_ref[...])
    hi = h2.astype(BF16)
    h2_ref[...] = hi
    lo = (h2 - hi.astype(F32)).astype(BF16)
    logits_t = (_nt_dot(wrh_ref[...], hi) + _nt_dot(wrh_ref[...], lo) + _nt_dot(wrl_ref[...], hi)
                + brc_ref[...])
    ex1, ex2, gate1, gate2 = _route_t(logits_t)

    pair_e = jnp.concatenate([ex1, ex2], axis=1)
    row = lax.broadcasted_iota(jnp.int32, (ROUTER_LANES, 2 * t), 0)
    onehot = jnp.where(row == pair_e, 1.0, 0.0)
    cum = _dot(onehot, tri_ref[...])
    rank = jnp.sum(onehot * cum, axis=0, keepdims=True) - 1.0
    counts = cum[:, 2 * t - 1:2 * t]
    units = jnp.floor((counts + (MOE_ROW_ALIGN - 1)) * (1.0 / MOE_ROW_ALIGN))
    off_units = _dot(ltri_ref[...], jnp.broadcast_to(units, (ROUTER_LANES, 128)))
    off = off_units[:, 0:1] * MOE_ROW_ALIGN
    dst = jnp.sum(onehot * off, axis=0, keepdims=True) + rank

    r8 = lax.broadcasted_iota(jnp.int32, (8, t), 0)
    mrow_ref[0] = jnp.where(r8 == 0, dst[:, 0:t], jnp.where(r8 == 1, dst[:, t:2 * t],
                            jnp.where(r8 == 2, gate1, jnp.where(r8 == 3, gate2, 0.0))))
    r128 = lax.broadcasted_iota(jnp.int32, (ROUTER_LANES, t), 0)
    meta = jnp.where(r128 == 0, dst[:, 0:t], jnp.where(r128 == 1, dst[:, t:2 * t],
                     jnp.where(r128 == 2, gate1, jnp.where(r128 == 3, gate2, 0.0))))
    mcol_ref[...] = meta.T
    lane = lax.broadcasted_iota(jnp.int32, (ROUTER_LANES, 128), 1)
    chunks = jnp.floor((units * MOE_ROW_ALIGN + (MOE_CHUNK - 1)) * (1.0 / MOE_CHUNK))
    tab_ref[0] = jnp.where(lane == 0, off, jnp.where(lane == 1, chunks,
                           jnp.where(lane == 2, off + units * MOE_ROW_ALIGN, 0.0)))


def _mix_route_sort(a, b, x, w_out_b, g2, wr_hi, wr_lo, br_col, tri, ltri):
    n = x.shape[0]
    t = MOE_BLOCK
    nblk = n // t
    const = lambda i: (0, 0)
    row_blk = lambda w: pl.BlockSpec((t, w), lambda i: (i, 0))
    return pl.pallas_call(
        _mix_route_sort_kernel,
        grid=(nblk,),
        in_specs=[row_blk(WIDTH), row_blk(WIDTH), row_blk(D_MODEL),
                  pl.BlockSpec((2 * WIDTH, D_MODEL), const),
                  pl.BlockSpec((1, D_MODEL), const),
                  pl.BlockSpec((ROUTER_LANES, D_MODEL), const),
                  pl.BlockSpec((ROUTER_LANES, D_MODEL), const),
                  pl.BlockSpec((ROUTER_LANES, 1), const),
                  pl.BlockSpec((2 * t, 2 * t), const),
                  pl.BlockSpec((ROUTER_LANES, ROUTER_LANES), const)],
        out_specs=[row_blk(D_MODEL), row_blk(D_MODEL),
                   pl.BlockSpec((1, 8, t), lambda i: (i, 0, 0)),
                   row_blk(ROUTER_LANES),
                   pl.BlockSpec((1, ROUTER_LANES, 128), lambda i: (i, 0, 0))],
        out_shape=[jax.ShapeDtypeStruct((n, D_MODEL), F32),
                   jax.ShapeDtypeStruct((n, D_MODEL), BF16),
                   jax.ShapeDtypeStruct((nblk, 8, t), F32),
                   jax.ShapeDtypeStruct((n, ROUTER_LANES), F32),
                   jax.ShapeDtypeStruct((nblk, ROUTER_LANES, 128), F32)],
        compiler_params=pltpu.CompilerParams(
            dimension_semantics=("arbitrary",), vmem_limit_bytes=V7X_VMEM_LIMIT),
        name="outproj_route_sort",
    )(a, b, x, w_out_b, g2, wr_hi, wr_lo, br_col, tri, ltri)


def _moe_sparse_kernel(tab_ref, h_ref, xp_ref, mrow_ref, mcol_ref, w1_ref, w2_ref, gf_ref, y_ref, xs, hs, os):
    t = MOE_BLOCK
    blk = pl.program_id(0)
    tab = lambda e, c: tab_ref[(blk * N_EXPERTS + e) * 3 + c]

    mrow = mrow_ref[0]
    dst1 = mrow[0:1, :].astype(jnp.int32)
    dst2 = mrow[1:2, :].astype(jnp.int32)
    piece = 512
    for r0 in range(0, MOE_SORT_ROWS, piece):
        d_idx = lax.broadcasted_iota(jnp.int32, (piece, t), 0) + r0
        sel = jnp.where((d_idx == dst1) | (d_idx == dst2), 1.0, 0.0)
        xs[r0:r0 + piece, :] = _dot(sel, h_ref[...]).astype(BF16)
    tail = slice(MOE_SORT_ROWS, MOE_SORT_ROWS + MOE_CHUNK)
    xs[tail, :] = jnp.zeros((MOE_CHUNK, D_MODEL), BF16)
    os[...] = jnp.zeros(os.shape, BF16)

    def gate_up(e, r0):
        ab = jnp.dot(xs[pl.ds(r0, MOE_CHUNK), :], w1_ref[e], preferred_element_type=F32)
        a = ab[:, :D_EXPERT]
        return (a * (1.0 / (1.0 + jnp.exp(-a))) * ab[:, D_EXPERT:]).astype(BF16)

    def down(e, hid):
        return jnp.dot(hid, w2_ref[e], preferred_element_type=F32).astype(BF16)

    def first_gate_up(g, carry):
        for i in range(MOE_EXPERTS_PER_STEP):
            e = g * MOE_EXPERTS_PER_STEP + i
            r0 = pl.multiple_of(tab(e, 0), MOE_ROW_ALIGN)
            hs[pl.ds(r0, MOE_CHUNK), :] = gate_up(e, r0)
        return carry

    def first_down(g, carry):
        for i in range(MOE_EXPERTS_PER_STEP):
            e = g * MOE_EXPERTS_PER_STEP + i
            r0 = pl.multiple_of(tab(e, 0), MOE_ROW_ALIGN)
            os[pl.ds(r0, MOE_CHUNK), :] = down(e, hs[pl.ds(r0, MOE_CHUNK), :])
        return carry

    lax.fori_loop(0, N_EXPERTS // MOE_EXPERTS_PER_STEP, first_gate_up, 0)
    lax.fori_loop(0, N_EXPERTS // MOE_EXPERTS_PER_STEP, first_down, 0)

    def more_chunks(e, carry):
        off, n_chunks, end = tab(e, 0), tab(e, 1), tab(e, 2)

        def chunk(c, carry):
            r0 = pl.multiple_of(off + c * MOE_CHUNK, MOE_ROW_ALIGN)
            rows = r0 + lax.broadcasted_iota(jnp.int32, (MOE_CHUNK, D_MODEL), 0)
            os[pl.ds(r0, MOE_CHUNK), :] = jnp.where(rows < end, down(e, gate_up(e, r0)), os[pl.ds(r0, MOE_CHUNK), :])
            return carry

        return lax.fori_loop(1, n_chunks, chunk, carry)

    lax.fori_loop(0, N_EXPERTS, more_chunks, 0)

    mcol = mcol_ref[...]
    d1c = mcol[:, 0:1].astype(jnp.int32)
    d2c = mcol[:, 1:2].astype(jnp.int32)
    l_idx = lax.broadcasted_iota(jnp.int32, (t, MOE_SORT_ROWS), 1)
    comb = jnp.where(l_idx == d1c, mcol[:, 2:3], 0.0) + jnp.where(l_idx == d2c, mcol[:, 3:4], 0.0)
    y = xp_ref[...] + _dot(comb, os[0:MOE_SORT_ROWS, :])
    y_ref[...] = _rmsnorm(y, gf_ref[...])


def _moe_sparse(tab, h2, xp, mrow, mcol, w1_b, w2_b, gf):
    n = h2.shape[0]
    t = MOE_BLOCK
    row_blk = lambda w: pl.BlockSpec((t, w), lambda i, tab: (i, 0))
    resident = lambda shape: pl.BlockSpec(shape, lambda i, tab: (0,) * len(shape), pipeline_mode=pl.Buffered(1))
    return pl.pallas_call(
        _moe_sparse_kernel,
        grid_spec=pltpu.PrefetchScalarGridSpec(
            num_scalar_prefetch=1,
            grid=(n // t,),
            in_specs=[row_blk(D_MODEL), row_blk(D_MODEL),
                      pl.BlockSpec((1, 8, t), lambda i, tab: (i, 0, 0)),
                      row_blk(ROUTER_LANES),
                      resident(w1_b.shape), resident(w2_b.shape),
                      pl.BlockSpec((1, D_MODEL), lambda i, tab: (0, 0))],
            out_specs=row_blk(D_MODEL),
            scratch_shapes=[pltpu.VMEM((MOE_SORT_ROWS + MOE_CHUNK, D_MODEL), BF16),
                            pltpu.VMEM((MOE_SORT_ROWS + MOE_CHUNK, D_EXPERT), BF16),
                            pltpu.VMEM((MOE_SORT_ROWS + MOE_CHUNK, D_MODEL), BF16)]),
        out_shape=jax.ShapeDtypeStruct((n, D_MODEL), F32),
        compiler_params=pltpu.CompilerParams(
            dimension_semantics=("arbitrary",), vmem_limit_bytes=V7X_VMEM_LIMIT),
        name="moe_sparse",
    )(tab, h2, xp, mrow, mcol, w1_b, w2_b, gf)


def _sample_proj_kernel(x_ref, g1_ref, w_ref, cos_ref, sin_ref, gv_ref, ones_ref, w00_ref, b0_ref,
                        rep_ref, foldt_ref, a_ref, k_ref, v_ref, vn_ref, qkvt_ref):
    h = _rmsnorm(x_ref[...], g1_ref[...])

    def proj(i):
        return _dot(h, w_ref[:, i * WIDTH:(i + 1) * WIDTH], precise=True)

    cos = _tile_lanes(cos_ref[...], WIDTH // 128)
    sin = _tile_lanes(sin_ref[...], WIDTH // 128)
    q = _rope(proj(2), cos, sin) * (HEAD_DIM ** -0.5)
    k = _rope(proj(3), cos, sin)
    v = proj(4)
    vn = _group_rmsnorm(proj(1), ones_ref[...], gv_ref[...], precise=True)
    a_ref[...] = proj(0) * (w00_ref[...] * vn + b0_ref[...])
    k_ref[...] = k
    v_ref[...] = v
    vn_ref[...] = vn

    n_rep = rep_ref.shape[0]
    r_idx = lax.broadcasted_iota(jnp.int32, (n_rep, WIDTH), 0)
    l_idx = lax.broadcasted_iota(jnp.int32, (n_rep, WIDTH), 1)
    own = (l_idx // HEAD_DIM) == (r_idx % N_HEADS)
    for t, src in enumerate((q, k, v)):
        rep = _dot(rep_ref[...], src, precise=True)
        qkvt_ref[t * HEAD_DIM:(t + 1) * HEAD_DIM, :] = lax.dot_general(
            foldt_ref[...], jnp.where(own, rep, 0.0), (((1,), (1,)), ((), ())),
            preferred_element_type=F32, precision=lax.Precision.HIGHEST)


def _sample_proj(x, g1, w_in, cos, sin, gv, ones_bd, w00, b0, rep, foldt):
    bd = x.shape[0]
    sds = lambda r, c: jax.ShapeDtypeStruct((r, c), F32)
    return pl.pallas_call(
        _sample_proj_kernel,
        out_shape=[sds(bd, WIDTH), sds(bd, WIDTH), sds(bd, WIDTH), sds(bd, WIDTH),
                   sds(3 * HEAD_DIM, bd * N_HEADS)],
        compiler_params=pltpu.CompilerParams(vmem_limit_bytes=V7X_VMEM_LIMIT),
        name="sample_proj",
    )(x, g1, w_in, cos, sin, gv, ones_bd, w00, b0, rep, foldt)


def _sample_attn_kernel(qkvt_ref, k_ref, v_ref, o_ref, *, win):
    b = pl.program_id(0)
    n_col = qkvt_ref.shape[1]
    c_idx = lax.broadcasted_iota(jnp.int32, (n_col, 128), 0)
    l_idx = lax.broadcasted_iota(jnp.int32, (n_col, 128), 1)
    pick = jnp.where((c_idx == b * N_HEADS + l_idx) & (l_idx < N_HEADS), 1.0, 0.0)
    cols = _dot(qkvt_ref[...], pick, precise=True)
    dist = win - lax.broadcasted_iota(jnp.int32, (1, win), 1)
    members = [(dist <= N_KEYS * dil) & (dist % dil == 0) for dil in DILATIONS]

    for h in range(N_HEADS):
        qc = cols[0:HEAD_DIM, h:h + 1]
        kc = cols[HEAD_DIM:2 * HEAD_DIM, h:h + 1]
        vc = cols[2 * HEAD_DIM:3 * HEAD_DIM, h:h + 1]
        s = jnp.sum(k_ref[0, h] * qc, axis=0, keepdims=True)
        s_self = jnp.sum(qc * kc, axis=0, keepdims=True)
        es, e_selfs, dens, lses = [], [], [], []
        for mem in members:
            sm = jnp.where(mem, s, NEG)
            m = jnp.maximum(jnp.max(sm, axis=1, keepdims=True), s_self)
            e = jnp.exp(sm - m)
            e_self = jnp.exp(s_self - m)
            den = jnp.sum(e, axis=1, keepdims=True) + e_self
            es.append(e)
            e_selfs.append(e_self)
            dens.append(den)
            lses.append(m + jnp.log(den))
        top = jnp.maximum(jnp.maximum(lses[0], lses[1]), lses[2])
        ws = [jnp.exp(l - top) for l in lses]
        wsum = ws[0] + ws[1] + ws[2]
        coef = [w / (den * wsum) for w, den in zip(ws, dens)]
        p_keys = coef[0] * es[0] + coef[1] * es[1] + coef[2] * es[2]
        p_self = coef[0] * e_selfs[0] + coef[1] * e_selfs[1] + coef[2] * e_selfs[2]
        o_ref[0, :, h:h + 1] = jnp.sum(v_ref[0, h] * p_keys, axis=1, keepdims=True) + p_self * vc


def _sample_attention(qkvt, cache_k_t, cache_v_t):
    bd, _, _, win = cache_k_t.shape
    blk = pl.BlockSpec((1, N_HEADS, HEAD_DIM, win), lambda b: (b, 0, 0, 0))
    return pl.pallas_call(
        functools.partial(_sample_attn_kernel, win=win),
        grid=(bd,),
        in_specs=[pl.BlockSpec(qkvt.shape, lambda b: (0, 0)), blk, blk],
        out_specs=pl.BlockSpec((1, HEAD_DIM, N_HEADS), lambda b: (b, 0, 0)),
        out_shape=jax.ShapeDtypeStruct((bd, HEAD_DIM, N_HEADS), F32),
        compiler_params=pltpu.CompilerParams(
            dimension_semantics=("arbitrary",), vmem_limit_bytes=V7X_VMEM_LIMIT),
        name="sample_window_attention",
    )(qkvt, cache_k_t, cache_v_t)


def _rope_tables(pos):
    half = HEAD_DIM // 2
    inv = ROPE_THETA ** (-jnp.arange(half, dtype=F32) * 2.0 / HEAD_DIM)
    ang = pos.astype(F32)[:, None] * inv[None, :]
    cos, sin = jnp.cos(ang), jnp.sin(ang)
    cos128 = jnp.concatenate([cos, cos, cos, cos], axis=1)
    sin128 = jnp.concatenate([-sin, sin, -sin, sin], axis=1)
    return cos128, sin128


def kernel(x_prompt, x_sample, cache_win_k, cache_win_v, ln1_g, w_in, sgu_norm_g, sgu_w, sgu_b, w_out, ln2_g,
           w_router_group, b_router_group, w_router_expert, b_router_expert, w_gate, w_up, w_down, lnf_g):
    depth = w_in.shape[0]
    assert depth == 1 and x_sample.shape[1] == 1
    B, S, _ = x_prompt.shape
    bd = x_sample.shape[0]
    win = cache_win_k.shape[2]
    assert S % (max(DILATIONS) * CHUNK) == 0 and win >= max(DILATIONS) * N_KEYS and PAST_LEN % CHUNK == 0
    l = 0

    w_in_b = w_in[l].astype(BF16)
    w_out_b = w_out[l].astype(BF16)
    w1_b = jnp.concatenate([w_gate[l], w_up[l]], axis=-1).astype(BF16)
    w2_b = w_down[l].astype(BF16)
    pad = ROUTER_LANES - N_GROUPS - N_EXPERTS
    w_router = jnp.pad(jnp.concatenate([w_router_group[l], w_router_expert[l]], axis=1), ((0, 0), (0, pad)))
    b_router = jnp.pad(jnp.concatenate([b_router_group[l], b_router_expert[l]]), (0, pad))[None, :]
    g1 = ln1_g[l][None, :]
    g2 = ln2_g[l][None, :]
    gf = lnf_g[None, :]
    gv = sgu_norm_g[l].reshape(1, WIDTH)
    grp = jnp.arange(WIDTH) // HEAD_DIM
    ones_bd = jnp.where(grp[:, None] == grp[None, :], 1.0 / HEAD_DIM, 0.0).astype(BF16)
    wp = jnp.concatenate([sgu_w[l][0::2], sgu_w[l][1::2]], axis=-1)
    bias = jnp.repeat(sgu_b[l].T, HEAD_DIM, axis=1)
    w00 = jnp.repeat(sgu_w[l][:, 0, 0], HEAD_DIM)[None, :]
    b0 = jnp.repeat(sgu_b[l][:, 0], HEAD_DIM)[None, :]

    cos_p, sin_p = _rope_tables(jnp.arange(S))
    a_p, q_p, k_p, v_p, kt_p, vt_p = _prompt_proj(x_prompt, g1, w_in_b, cos_p, sin_p, gv, ones_bd, wp, bias)
    b_p = _prompt_attention(q_p, k_p, v_p)
    n = B * S
    assert n % MOE_BLOCK == 0
    wr_t = w_router.T
    wr_hi = wr_t.astype(BF16)
    wr_lo = (wr_t - wr_hi.astype(F32)).astype(BF16)
    pair_idx = jnp.arange(2 * MOE_BLOCK)
    tri = (pair_idx[:, None] <= pair_idx[None, :]).astype(BF16)
    lane_idx = jnp.arange(ROUTER_LANES)
    ltri = (lane_idx[None, :] < lane_idx[:, None]).astype(BF16)
    xp2, h2, mrow, mcol, tab_f = _mix_route_sort(
        a_p.reshape(n, WIDTH), b_p.reshape(n, WIDTH), x_prompt.reshape(n, D_MODEL),
        w_out_b, g2, wr_hi, wr_lo, b_router.reshape(ROUTER_LANES, 1), tri, ltri)
    tab = tab_f[:, :N_EXPERTS, 0:3].astype(jnp.int32).reshape(-1)
    y_prompt = _moe_sparse(tab, h2, xp2, mrow, mcol, w1_b, w2_b, gf).reshape(B, S, D_MODEL)
    buf_p = min(MAX_WINDOW, S)
    to_win = lambda t: jnp.transpose(t.reshape(1, B, N_HEADS, HEAD_DIM, buf_p), (0, 1, 4, 2, 3))
    new_k_p = to_win(kt_p)
    new_v_p = to_win(vt_p)

    cos_s, sin_s = _rope_tables(PAST_LEN + jnp.arange(1))
    rep = (jnp.arange(bd * N_HEADS)[:, None] // N_HEADS == jnp.arange(bd)[None, :]).astype(F32)
    foldt = (jnp.arange(HEAD_DIM)[:, None] == jnp.arange(WIDTH)[None, :] % HEAD_DIM).astype(F32)
    xs = x_sample.reshape(bd, D_MODEL)
    a_s, k_s, v_s, vn_s, qkvt = _sample_proj(xs, g1, w_in[l], cos_s, sin_s, gv, ones_bd, w00, b0, rep, foldt)
    to_pos_minor = lambda c: jnp.transpose(c, (0, 2, 3, 1))
    o3 = _sample_attention(qkvt, to_pos_minor(cache_win_k[l]), to_pos_minor(cache_win_v[l]))
    b_s = jnp.transpose(o3, (0, 2, 1)).reshape(bd, WIDTH)
    xs2, hs2, gates_s = _mix_router(a_s, b_s, xs, w_out[l], g2, w_router, b_router, tm=bd, precise=True)
    y_sample = _moe(hs2, gates_s, xs2, w1_b, w2_b, gf, tm=bd).reshape(bd, 1, D_MODEL)

    shape_s = (1, bd, 1, N_HEADS, HEAD_DIM)
    return (y_prompt, y_sample, new_k_p, new_v_p,
            k_s.reshape(shape_s), v_s.reshape(shape_s), vn_s.reshape(shape_s))
```

```python
import functools

import jax
import jax.numpy as jnp
from jax import lax
from jax.experimental import pallas as pl
from jax.experimental.pallas import tpu as pltpu

F32 = jnp.float32
BF16 = jnp.bfloat16

D_MODEL = 1024
HEAD_DIM = 64
N_HEADS = 8
WIDTH = N_HEADS * HEAD_DIM
PROJ_COLS = 5 * WIDTH
CHUNK = 128
DILATIONS = (1, 4, 16)
N_KEYS = 128
MAX_WINDOW = 2048
PAST_LEN = 16384
ROPE_THETA = 10000.0
N_GROUPS = 4
EXPERTS_PER_GROUP = 8
N_EXPERTS = N_GROUPS * EXPERTS_PER_GROUP
D_EXPERT = 128
EPS = 1e-6
NEG = -1e30
TILES_PER_STEP = 8
MOE_BLOCK = 512
MOE_ROW_ALIGN = 16
MOE_CHUNK = 64
MOE_EXPERTS_PER_STEP = 16
MOE_SORT_ROWS = -(-(2 * MOE_BLOCK + N_EXPERTS * (MOE_ROW_ALIGN - 1)) // 512) * 512
ROUTER_LANES = 128
V7X_VMEM_LIMIT = 56 * 1024 * 1024


def _rmsnorm(x, g):
    return x * lax.rsqrt(jnp.mean(x * x, axis=-1, keepdims=True) + EPS) * g


def _tile_lanes(t, reps):
    return jnp.concatenate([t] * reps, axis=1)


def _rope(t, cos, sin_signed):
    lane = lax.broadcasted_iota(jnp.int32, t.shape, 1)
    first_half = (lane % HEAD_DIM) < (HEAD_DIM // 2)
    n = t.shape[1]
    partner = jnp.where(first_half, pltpu.roll(t, n - HEAD_DIM // 2, 1), pltpu.roll(t, HEAD_DIM // 2, 1))
    return t * cos + partner * sin_signed


def _dot(a, b, precise=False):
    if precise:
        return jnp.dot(a.astype(F32), b.astype(F32), preferred_element_type=F32,
                       precision=lax.Precision.HIGHEST)
    return jnp.dot(a.astype(BF16), b.astype(BF16), preferred_element_type=F32)


def _group_rmsnorm(va, ones_bd, gv, precise=False):
    ms = _dot(va * va, ones_bd, precise)
    return va * lax.rsqrt(ms + EPS) * gv


def _proj_kernel(x_ref, g1_ref, w_ref, cos_ref, sin_ref, gv_ref, ones_ref, wp_ref, bias_ref,
                 a_ref, q_ref, k_ref, v_ref, kt_ref, vt_ref, *, tm, first_win_tile):
    h = _rmsnorm(x_ref[0], g1_ref[...]).astype(BF16)

    def proj(i):
        return jnp.dot(h, w_ref[:, i * WIDTH:(i + 1) * WIDTH], preferred_element_type=F32)

    cos = _tile_lanes(cos_ref[...], WIDTH // 128)
    sin = _tile_lanes(sin_ref[...], WIDTH // 128)
    q_ref[0] = _rope(proj(2), cos, sin) * (HEAD_DIM ** -0.5)
    k_ref[0] = _rope(proj(3), cos, sin)
    v_ref[0] = proj(4)

    u = proj(0)
    vn = _group_rmsnorm(proj(1), ones_ref[...], gv_ref[...]).astype(BF16)

    lane = lax.broadcasted_iota(jnp.int32, (CHUNK, 128), 1)
    left = lane < HEAD_DIM
    row = lax.broadcasted_iota(jnp.int32, (CHUNK, 2 * CHUNK), 0)
    col = lax.broadcasted_iota(jnp.int32, (CHUNK, 2 * CHUNK), 1)
    causal = (col % CHUNK) <= row
    zero = jnp.zeros((CHUNK, 128), BF16)
    wps = [jnp.where(causal, wp_ref[gp], 0.0).astype(BF16) for gp in range(N_HEADS // 2)]
    for c in range(tm // CHUNK):
        rows = slice(c * CHUNK, (c + 1) * CHUNK)
        mixes = []
        for gp in range(N_HEADS // 2):
            vv = vn[rows, gp * 128:(gp + 1) * 128]
            v2 = jnp.concatenate([jnp.where(left, vv, zero), jnp.where(left, zero, vv)], axis=0)
            mixes.append(jnp.dot(wps[gp], v2, preferred_element_type=F32))
        mix = jnp.concatenate(mixes, axis=1) + bias_ref[...]
        a_ref[0, rows, :] = (u[rows, :] * mix).astype(a_ref.dtype)

    @pl.when(pl.program_id(1) >= first_win_tile)
    def _():
        kt_ref[0] = k_ref[0].T
        vt_ref[0] = v_ref[0].T


def _prompt_proj(x, g1, w_in_b, cos, sin, gv, ones_bd, wp, bias, *, tm=512):
    B, S, _ = x.shape
    const2 = lambda b, j: (0, 0)
    out_sds = lambda dt: jax.ShapeDtypeStruct((B, S, WIDTH), dt)
    tile = pl.BlockSpec((1, tm, WIDTH), lambda b, j: (b, j, 0))
    win = min(MAX_WINDOW, S)
    first_win_tile = (S - win) // tm
    tile_t = pl.BlockSpec((1, WIDTH, tm), lambda b, j: (b, 0, jnp.maximum(j - first_win_tile, 0)))
    win_sds = jax.ShapeDtypeStruct((B, WIDTH, win), F32)
    return pl.pallas_call(
        functools.partial(_proj_kernel, tm=tm, first_win_tile=first_win_tile),
        grid=(B, S // tm),
        in_specs=[
            pl.BlockSpec((1, tm, D_MODEL), lambda b, j: (b, j, 0)),
            pl.BlockSpec((1, D_MODEL), const2),
            pl.BlockSpec((D_MODEL, PROJ_COLS), const2),
            pl.BlockSpec((tm, 128), lambda b, j: (j, 0)),
            pl.BlockSpec((tm, 128), lambda b, j: (j, 0)),
            pl.BlockSpec((1, WIDTH), const2),
            pl.BlockSpec((WIDTH, WIDTH), const2),
            pl.BlockSpec((N_HEADS // 2, CHUNK, 2 * CHUNK), lambda b, j: (0, 0, 0)),
            pl.BlockSpec((CHUNK, WIDTH), const2),
        ],
        out_specs=[tile, tile, tile, tile, tile_t, tile_t],
        out_shape=[out_sds(BF16), out_sds(F32), out_sds(F32), out_sds(F32), win_sds, win_sds],
        compiler_params=pltpu.CompilerParams(
            dimension_semantics=("arbitrary", "arbitrary"), vmem_limit_bytes=V7X_VMEM_LIMIT),
        name="prompt_proj_sgu",
    )(x, g1, w_in_b, cos, sin, gv, ones_bd, wp, bias)


def _attn_kernel(q_ref, k_ref, v_ref, o_ref, qd, kd, vd, res_o, res_l, nat_o, nat_l, *, seq):
    n_tiles = seq // CHUNK
    lane = lax.broadcasted_iota(jnp.int32, (CHUNK, 128), 1)
    left = lane < HEAD_DIM
    qi = lax.broadcasted_iota(jnp.int32, (CHUNK, 2 * CHUNK), 0)
    kj = lax.broadcasted_iota(jnp.int32, (CHUNK, 2 * CHUNK), 1)
    dist = CHUNK + qi - kj
    band = (dist >= 0) & (dist <= N_KEYS)
    zero_q = jnp.zeros((CHUNK, 128), BF16)

    kd[0:CHUNK, :] = jnp.zeros((CHUNK, 128), BF16)
    vd[0:CHUNK, :] = jnp.zeros((CHUNK, 128), BF16)

    for p, dil in enumerate(DILATIONS):
        sub = seq // dil
        nb = sub // CHUNK
        for r in range(dil):
            src = pl.ds(r, sub, stride=dil) if dil > 1 else pl.ds(0, sub)
            qd[r * sub:(r + 1) * sub, :] = q_ref[0, src, :].astype(BF16)
            kd[CHUNK + r * sub:CHUNK + (r + 1) * sub, :] = k_ref[0, src, :].astype(BF16)
            vd[CHUNK + r * sub:CHUNK + (r + 1) * sub, :] = v_ref[0, src, :].astype(BF16)

        def tile_body(t, p=p, nb=nb):
            row = pl.multiple_of(t * CHUNK, CHUNK)
            qt = qd[pl.ds(row, CHUNK), :]
            k2 = kd[pl.ds(row, 2 * CHUNK), :]
            v2 = vd[pl.ds(row, 2 * CHUNK), :]
            first_key = jnp.where(t % nb == 0, CHUNK, 0)
            mask = band & (kj >= first_key)
            outs, lses = [], []
            for hsel in (left, jnp.logical_not(left)):
                qh = jnp.where(hsel, qt, zero_q)
                s = lax.dot_general(qh, k2, (((1,), (1,)), ((), ())), preferred_element_type=F32)
                s = jnp.where(mask, s, NEG)
                m = jnp.max(s, axis=1, keepdims=True)
                e = jnp.exp(s - m)
                den = jnp.sum(e, axis=1, keepdims=True)
                pv = jnp.dot(e.astype(BF16), v2, preferred_element_type=F32)
                outs.append(pv / den)
                lses.append(jnp.broadcast_to(m + jnp.log(den), (CHUNK, 128)))
            res_o[p, pl.ds(row, CHUNK), :] = jnp.where(left, outs[0], outs[1])
            res_l[p, pl.ds(row, CHUNK), :] = jnp.where(left, lses[0], lses[1])

        def group_body(g, carry, tile_body=tile_body):
            for i in range(TILES_PER_STEP):
                tile_body(g * TILES_PER_STEP + i)
            return carry

        lax.fori_loop(0, n_tiles // TILES_PER_STEP, group_body, 0)

    for p, dil in enumerate(DILATIONS):
        if dil == 1:
            continue
        sub = seq // dil
        for r in range(dil):
            nat_o[p - 1, pl.ds(r, sub, stride=dil), :] = res_o[p, r * sub:(r + 1) * sub, :]
            nat_l[p - 1, pl.ds(r, sub, stride=dil), :] = res_l[p, r * sub:(r + 1) * sub, :]

    rows_per_step = 256

    def merge_body(c, carry):
        rows = pl.ds(pl.multiple_of(c * rows_per_step, rows_per_step), rows_per_step)
        l0, l1, l2 = res_l[0, rows, :], nat_l[0, rows, :], nat_l[1, rows, :]
        top = jnp.maximum(jnp.maximum(l0, l1), l2)
        w0, w1, w2 = jnp.exp(l0 - top), jnp.exp(l1 - top), jnp.exp(l2 - top)
        num = w0 * res_o[0, rows, :] + w1 * nat_o[0, rows, :] + w2 * nat_o[1, rows, :]
        o_ref[0, rows, :] = (num / (w0 + w1 + w2)).astype(o_ref.dtype)
        return carry

    lax.fori_loop(0, seq // rows_per_step, merge_body, 0)


def _prompt_attention(q, k, v):
    B, S, _ = q.shape
    blk = pl.BlockSpec((1, S, 128), lambda b, hp: (b, 0, hp))
    return pl.pallas_call(
        functools.partial(_attn_kernel, seq=S),
        grid=(B, WIDTH // 128),
        in_specs=[blk, blk, blk],
        out_specs=blk,
        out_shape=jax.ShapeDtypeStruct((B, S, WIDTH), BF16),
        scratch_shapes=[
            pltpu.VMEM((S, 128), BF16),
            pltpu.VMEM((S + CHUNK, 128), BF16),
            pltpu.VMEM((S + CHUNK, 128), BF16),
            pltpu.VMEM((len(DILATIONS), S, 128), F32),
            pltpu.VMEM((len(DILATIONS), S, 128), F32),
            pltpu.VMEM((len(DILATIONS) - 1, S, 128), F32),
            pltpu.VMEM((len(DILATIONS) - 1, S, 128), F32),
        ],
        compiler_params=pltpu.CompilerParams(
            dimension_semantics=("arbitrary", "arbitrary"), vmem_limit_bytes=V7X_VMEM_LIMIT),
        name="prompt_dilated_attention",
    )(q, k, v)


def _route(logits):
    lane = lax.broadcasted_iota(jnp.int32, logits.shape, 1)
    big = jnp.int32(ROUTER_LANES)
    lg = jnp.where(lane < N_GROUPS, logits, NEG)
    gmax = jnp.max(lg, axis=1, keepdims=True)
    gp = 1.0 / jnp.sum(jnp.exp(lg - gmax), axis=1, keepdims=True)
    gi = jnp.min(jnp.where(lg == gmax, lane, big), axis=1, keepdims=True)
    lo = N_GROUPS + EXPERTS_PER_GROUP * gi
    le = jnp.where((lane >= lo) & (lane < lo + EXPERTS_PER_GROUP), logits, NEG)
    m1 = jnp.max(le, axis=1, keepdims=True)
    i1 = jnp.min(jnp.where(le == m1, lane, big), axis=1, keepdims=True)
    le2 = jnp.where(lane == i1, NEG, le)
    m2 = jnp.max(le2, axis=1, keepdims=True)
    i2 = jnp.min(jnp.where(le2 == m2, lane, big), axis=1, keepdims=True)
    e2 = jnp.exp(m2 - m1)
    w1 = 1.0 / (1.0 + e2)
    w2 = e2 / (1.0 + e2)
    return jnp.where(lane == i1, gp * w1, jnp.where(lane == i2, gp * w2, 0.0))


def _mix_router_kernel(a_ref, b_ref, x_ref, wo_ref, g2_ref, wr_ref, br_ref, xp_ref, h2_ref, gates_ref, *, precise):
    mixed = (_dot(a_ref[...], wo_ref[0:WIDTH, :], precise)
             + _dot(b_ref[...], wo_ref[WIDTH:2 * WIDTH, :], precise))
    xp = x_ref[...] + mixed
    xp_ref[...] = xp
    h2 = _rmsnorm(xp, g2_ref[...])
    h2_ref[...] = h2.astype(h2_ref.dtype)
    logits = jnp.dot(h2, wr_ref[...], preferred_element_type=F32,
                     precision=lax.Precision.HIGHEST) + br_ref[...]
    gates_ref[...] = _route(logits)


def _mix_router(a, b, x, w_out, g2, w_router, b_router, *, tm, precise=False):
    n = x.shape[0]
    const = lambda i: (0, 0)
    row_blk = lambda w: pl.BlockSpec((tm, w), lambda i: (i, 0))
    return pl.pallas_call(
        functools.partial(_mix_router_kernel, precise=precise),
        grid=(n // tm,),
        in_specs=[row_blk(WIDTH), row_blk(WIDTH), row_blk(D_MODEL),
                  pl.BlockSpec((2 * WIDTH, D_MODEL), const),
                  pl.BlockSpec((1, D_MODEL), const),
                  pl.BlockSpec((D_MODEL, ROUTER_LANES), const),
                  pl.BlockSpec((1, ROUTER_LANES), const)],
        out_specs=[row_blk(D_MODEL), row_blk(D_MODEL), row_blk(ROUTER_LANES)],
        out_shape=[jax.ShapeDtypeStruct((n, D_MODEL), F32),
                   jax.ShapeDtypeStruct((n, D_MODEL), BF16),
                   jax.ShapeDtypeStruct((n, ROUTER_LANES), F32)],
        compiler_params=pltpu.CompilerParams(
            dimension_semantics=("arbitrary",), vmem_limit_bytes=V7X_VMEM_LIMIT),
        name="outproj_router",
    )(a, b, x, w_out, g2, w_router, b_router)


def _nt_dot(w, t):
    return lax.dot_general(w, t, (((1,), (1,)), ((), ())), preferred_element_type=F32)


def _route_t(logits_t):
    row = lax.broadcasted_iota(jnp.int32, logits_t.shape, 0)
    big = jnp.int32(ROUTER_LANES)
    lg = jnp.where(row < N_GROUPS, logits_t, NEG)
    gmax = jnp.max(lg, axis=0, keepdims=True)
    gp = 1.0 / jnp.sum(jnp.exp(lg - gmax), axis=0, keepdims=True)
    gi = jnp.min(jnp.where(lg == gmax, row, big), axis=0, keepdims=True)
    lo = N_GROUPS + EXPERTS_PER_GROUP * gi
    le = jnp.where((row >= lo) & (row < lo + EXPERTS_PER_GROUP), logits_t, NEG)
    m1 = jnp.max(le, axis=0, keepdims=True)
    i1 = jnp.min(jnp.where(le == m1, row, big), axis=0, keepdims=True)
    le2 = jnp.where(row == i1, NEG, le)
    m2 = jnp.max(le2, axis=0, keepdims=True)
    i2 = jnp.min(jnp.where(le2 == m2, row, big), axis=0, keepdims=True)
    e2 = jnp.exp(m2 - m1)
    return i1 - N_GROUPS, i2 - N_GROUPS, gp / (1.0 + e2), gp * e2 / (1.0 + e2)


def _mix_route_sort_kernel(a_ref, b_ref, x_ref, wo_ref, g2_ref, wrh_ref, wrl_ref, brc_ref, tri_ref, ltri_ref,
                           qkvt_ref, ck_ref, cv_ref,
                           xp_ref, h2_ref, mrow_ref, mcol_ref, tab_ref, so_ref, *, n_seq, win):
    t = MOE_BLOCK
    xp = x_ref[...] + _dot(a_ref[...], wo_ref[0:WIDTH, :]) + _dot(b_ref[...], wo_ref[WIDTH:2 * WIDTH, :])
    xp_ref[...] = xp
    h2 = _rmsnorm(xp, g2_ref[...])
    hi = h2.astype(BF16)
    h2_ref[...] = hi
    lo = (h2 - hi.astype(F32)).astype(BF16)
    logits_t = (_nt_dot(wrh_ref[...], hi) + _nt_dot(wrh_ref[...], lo) + _nt_dot(wrl_ref[...], hi)
                + brc_ref[...])
    ex1, ex2, gate1, gate2 = _route_t(logits_t)

    pair_e = jnp.concatenate([ex1, ex2], axis=1)
    row = lax.broadcasted_iota(jnp.int32, (ROUTER_LANES, 2 * t), 0)
    onehot = jnp.where(row == pair_e, 1.0, 0.0)
    cum = _dot(onehot, tri_ref[...])
    rank = jnp.sum(onehot * cum, axis=0, keepdims=True) - 1.0
    counts = cum[:, 2 * t - 1:2 * t]
    units = jnp.floor((counts + (MOE_ROW_ALIGN - 1)) * (1.0 / MOE_ROW_ALIGN))
    off_units = _dot(ltri_ref[...], jnp.broadcast_to(units, (ROUTER_LANES, 128)))
    off = off_units[:, 0:1] * MOE_ROW_ALIGN
    dst = jnp.sum(onehot * off, axis=0, keepdims=True) + rank

    r8 = lax.broadcasted_iota(jnp.int32, (8, t), 0)
    mrow_ref[0] = jnp.where(r8 == 0, dst[:, 0:t], jnp.where(r8 == 1, dst[:, t:2 * t],
                            jnp.where(r8 == 2, gate1, jnp.where(r8 == 3, gate2, 0.0))))
    r128 = lax.broadcasted_iota(jnp.int32, (ROUTER_LANES, t), 0)
    meta = jnp.where(r128 == 0, dst[:, 0:t], jnp.where(r128 == 1, dst[:, t:2 * t],
                     jnp.where(r128 == 2, gate1, jnp.where(r128 == 3, gate2, 0.0))))
    mcol_ref[...] = meta.T
    lane = lax.broadcasted_iota(jnp.int32, (ROUTER_LANES, 128), 1)
    chunks = jnp.floor((units * MOE_ROW_ALIGN + (MOE_CHUNK - 1)) * (1.0 / MOE_CHUNK))
    tab_ref[0] = jnp.where(lane == 0, off, jnp.where(lane == 1, chunks,
                           jnp.where(lane == 2, off + units * MOE_ROW_ALIGN, 0.0)))

    _sample_attend(qkvt_ref, ck_ref, cv_ref, so_ref, jnp.minimum(pl.program_id(0), n_seq - 1), win)


def _mix_route_sort(a, b, x, w_out_b, g2, wr_hi, wr_lo, br_col, tri, ltri, qkvt, cache_k_t, cache_v_t):
    n = x.shape[0]
    t = MOE_BLOCK
    nblk = n // t
    n_seq, _, _, win = cache_k_t.shape
    assert n_seq <= nblk, "one sample sequence rides on each prompt block"
    const = lambda i: (0, 0)
    row_blk = lambda w: pl.BlockSpec((t, w), lambda i: (i, 0))
    seq_blk = lambda shape: pl.BlockSpec((1,) + shape, lambda i: (jnp.minimum(i, n_seq - 1),) + (0,) * len(shape))
    return pl.pallas_call(
        functools.partial(_mix_route_sort_kernel, n_seq=n_seq, win=win),
        grid=(nblk,),
        in_specs=[row_blk(WIDTH), row_blk(WIDTH), row_blk(D_MODEL),
                  pl.BlockSpec((2 * WIDTH, D_MODEL), const),
                  pl.BlockSpec((1, D_MODEL), const),
                  pl.BlockSpec((ROUTER_LANES, D_MODEL), const),
                  pl.BlockSpec((ROUTER_LANES, D_MODEL), const),
                  pl.BlockSpec((ROUTER_LANES, 1), const),
                  pl.BlockSpec((2 * t, 2 * t), const),
                  pl.BlockSpec((ROUTER_LANES, ROUTER_LANES), const),
                  pl.BlockSpec(qkvt.shape, const),
                  seq_blk((N_HEADS, HEAD_DIM, win)), seq_blk((N_HEADS, HEAD_DIM, win))],
        out_specs=[row_blk(D_MODEL), row_blk(D_MODEL),
                   pl.BlockSpec((1, 8, t), lambda i: (i, 0, 0)),
                   row_blk(ROUTER_LANES),
                   pl.BlockSpec((1, ROUTER_LANES, 128), lambda i: (i, 0, 0)),
                   seq_blk((HEAD_DIM, N_HEADS))],
        out_shape=[jax.ShapeDtypeStruct((n, D_MODEL), F32),
                   jax.ShapeDtypeStruct((n, D_MODEL), BF16),
                   jax.ShapeDtypeStruct((nblk, 8, t), F32),
                   jax.ShapeDtypeStruct((n, ROUTER_LANES), F32),
                   jax.ShapeDtypeStruct((nblk, ROUTER_LANES, 128), F32),
                   jax.ShapeDtypeStruct((n_seq, HEAD_DIM, N_HEADS), F32)],
        compiler_params=pltpu.CompilerParams(
            dimension_semantics=("arbitrary",), vmem_limit_bytes=V7X_VMEM_LIMIT),
        name="outproj_route_sort",
    )(a, b, x, w_out_b, g2, wr_hi, wr_lo, br_col, tri, ltri, qkvt, cache_k_t, cache_v_t)


def _moe_sparse_kernel(tab_ref, h_ref, xp_ref, mrow_ref, mcol_ref, w1_ref, w2_ref, gf_ref, sh_ref, sg_ref, sx_ref,
                       y_ref, sy_ref, xs, hs, os, shid):
    t = MOE_BLOCK
    blk = pl.program_id(0)
    tab = lambda e, c: tab_ref[(blk * N_EXPERTS + e) * 3 + c]

    mrow = mrow_ref[0]
    dst1 = mrow[0:1, :].astype(jnp.int32)
    dst2 = mrow[1:2, :].astype(jnp.int32)
    piece = 512
    for r0 in range(0, MOE_SORT_ROWS, piece):
        d_idx = lax.broadcasted_iota(jnp.int32, (piece, t), 0) + r0
        sel = jnp.where((d_idx == dst1) | (d_idx == dst2), 1.0, 0.0)
        xs[r0:r0 + piece, :] = _dot(sel, h_ref[...]).astype(BF16)
    tail = slice(MOE_SORT_ROWS, MOE_SORT_ROWS + MOE_CHUNK)
    xs[tail, :] = jnp.zeros((MOE_CHUNK, D_MODEL), BF16)
    os[...] = jnp.zeros(os.shape, BF16)

    def gate_up(e, r0):
        ab = jnp.dot(xs[pl.ds(r0, MOE_CHUNK), :], w1_ref[e], preferred_element_type=F32)
        a = ab[:, :D_EXPERT]
        return (a * (1.0 / (1.0 + jnp.exp(-a))) * ab[:, D_EXPERT:]).astype(BF16)

    def down(e, hid):
        return jnp.dot(hid, w2_ref[e], preferred_element_type=F32).astype(BF16)

    def first_gate_up(g, carry):
        for i in range(MOE_EXPERTS_PER_STEP):
            e = g * MOE_EXPERTS_PER_STEP + i
            r0 = pl.multiple_of(tab(e, 0), MOE_ROW_ALIGN)
            hs[pl.ds(r0, MOE_CHUNK), :] = gate_up(e, r0)
        return carry

    def first_down(g, carry):
        for i in range(MOE_EXPERTS_PER_STEP):
            e = g * MOE_EXPERTS_PER_STEP + i
            r0 = pl.multiple_of(tab(e, 0), MOE_ROW_ALIGN)
            os[pl.ds(r0, MOE_CHUNK), :] = down(e, hs[pl.ds(r0, MOE_CHUNK), :])
        return carry

    lax.fori_loop(0, N_EXPERTS // MOE_EXPERTS_PER_STEP, first_gate_up, 0)
    lax.fori_loop(0, N_EXPERTS // MOE_EXPERTS_PER_STEP, first_down, 0)

    def more_chunks(e, carry):
        off, n_chunks, end = tab(e, 0), tab(e, 1), tab(e, 2)

        def chunk(c, carry):
            r0 = pl.multiple_of(off + c * MOE_CHUNK, MOE_ROW_ALIGN)
            rows = r0 + lax.broadcasted_iota(jnp.int32, (MOE_CHUNK, D_MODEL), 0)
            os[pl.ds(r0, MOE_CHUNK), :] = jnp.where(rows < end, down(e, gate_up(e, r0)), os[pl.ds(r0, MOE_CHUNK), :])
            return carry

        return lax.fori_loop(1, n_chunks, chunk, carry)

    lax.fori_loop(0, N_EXPERTS, more_chunks, 0)

    mcol = mcol_ref[...]
    d1c = mcol[:, 0:1].astype(jnp.int32)
    d2c = mcol[:, 1:2].astype(jnp.int32)
    l_idx = lax.broadcasted_iota(jnp.int32, (t, MOE_SORT_ROWS), 1)
    comb = jnp.where(l_idx == d1c, mcol[:, 2:3], 0.0) + jnp.where(l_idx == d2c, mcol[:, 3:4], 0.0)
    y = xp_ref[...] + _dot(comb, os[0:MOE_SORT_ROWS, :])
    y_ref[...] = _rmsnorm(y, gf_ref[...])

    @pl.when(blk == 0)
    def _():
        sh = sh_ref[...]
        gates = sg_ref[...]
        for e in range(N_EXPERTS):
            ab = jnp.dot(sh, w1_ref[e], preferred_element_type=F32)
            a = ab[:, :D_EXPERT]
            gate = gates[:, N_GROUPS + e:N_GROUPS + e + 1]
            shid[:, e * D_EXPERT:(e + 1) * D_EXPERT] = (
                a * (1.0 / (1.0 + jnp.exp(-a))) * ab[:, D_EXPERT:] * gate).astype(BF16)
        w2_all = w2_ref[...].reshape(N_EXPERTS * D_EXPERT, D_MODEL)
        ys = sx_ref[...] + jnp.dot(shid[...], w2_all, preferred_element_type=F32)
        sy_ref[...] = _rmsnorm(ys, gf_ref[...])


def _moe_sparse(tab, h2, xp, mrow, mcol, w1_b, w2_b, gf, s_h, s_gates, s_x):
    n = h2.shape[0]
    n_s = s_h.shape[0]
    t = MOE_BLOCK
    row_blk = lambda w: pl.BlockSpec((t, w), lambda i, tab: (i, 0))
    whole = lambda shape: pl.BlockSpec(shape, lambda i, tab: (0,) * len(shape))
    resident = lambda shape: pl.BlockSpec(shape, lambda i, tab: (0,) * len(shape), pipeline_mode=pl.Buffered(1))
    return pl.pallas_call(
        _moe_sparse_kernel,
        grid_spec=pltpu.PrefetchScalarGridSpec(
            num_scalar_prefetch=1,
            grid=(n // t,),
            in_specs=[row_blk(D_MODEL), row_blk(D_MODEL),
                      pl.BlockSpec((1, 8, t), lambda i, tab: (i, 0, 0)),
                      row_blk(ROUTER_LANES),
                      resident(w1_b.shape), resident(w2_b.shape),
                      whole((1, D_MODEL)),
                      whole(s_h.shape), whole(s_gates.shape), whole(s_x.shape)],
            out_specs=[row_blk(D_MODEL), whole((n_s, D_MODEL))],
            scratch_shapes=[pltpu.VMEM((MOE_SORT_ROWS + MOE_CHUNK, D_MODEL), BF16),
                            pltpu.VMEM((MOE_SORT_ROWS + MOE_CHUNK, D_EXPERT), BF16),
                            pltpu.VMEM((MOE_SORT_ROWS + MOE_CHUNK, D_MODEL), BF16),
                            pltpu.VMEM((n_s, N_EXPERTS * D_EXPERT), BF16)]),
        out_shape=[jax.ShapeDtypeStruct((n, D_MODEL), F32), jax.ShapeDtypeStruct((n_s, D_MODEL), F32)],
        compiler_params=pltpu.CompilerParams(
            dimension_semantics=("arbitrary",), vmem_limit_bytes=V7X_VMEM_LIMIT),
        name="moe_sparse",
    )(tab, h2, xp, mrow, mcol, w1_b, w2_b, gf, s_h, s_gates, s_x)


def _sample_proj_kernel(x_ref, g1_ref, w_ref, cos_ref, sin_ref, gv_ref, ones_ref, w00_ref, b0_ref,
                        rep_ref, foldt_ref, a_ref, k_ref, v_ref, vn_ref, qkvt_ref):
    h = _rmsnorm(x_ref[...], g1_ref[...])

    def proj(i):
        return _dot(h, w_ref[:, i * WIDTH:(i + 1) * WIDTH], precise=True)

    cos = _tile_lanes(cos_ref[...], WIDTH // 128)
    sin = _tile_lanes(sin_ref[...], WIDTH // 128)
    q = _rope(proj(2), cos, sin) * (HEAD_DIM ** -0.5)
    k = _rope(proj(3), cos, sin)
    v = proj(4)
    vn = _group_rmsnorm(proj(1), ones_ref[...], gv_ref[...], precise=True)
    a_ref[...] = proj(0) * (w00_ref[...] * vn + b0_ref[...])
    k_ref[...] = k
    v_ref[...] = v
    vn_ref[...] = vn

    n_rep = rep_ref.shape[0]
    r_idx = lax.broadcasted_iota(jnp.int32, (n_rep, WIDTH), 0)
    l_idx = lax.broadcasted_iota(jnp.int32, (n_rep, WIDTH), 1)
    own = (l_idx // HEAD_DIM) == (r_idx % N_HEADS)
    for t, src in enumerate((q, k, v)):
        rep = _dot(rep_ref[...], src, precise=True)
        qkvt_ref[t * HEAD_DIM:(t + 1) * HEAD_DIM, :] = lax.dot_general(
            foldt_ref[...], jnp.where(own, rep, 0.0), (((1,), (1,)), ((), ())),
            preferred_element_type=F32, precision=lax.Precision.HIGHEST)


def _sample_proj(x, g1, w_in, cos, sin, gv, ones_bd, w00, b0, rep, foldt):
    bd = x.shape[0]
    sds = lambda r, c: jax.ShapeDtypeStruct((r, c), F32)
    return pl.pallas_call(
        _sample_proj_kernel,
        out_shape=[sds(bd, WIDTH), sds(bd, WIDTH), sds(bd, WIDTH), sds(bd, WIDTH),
                   sds(3 * HEAD_DIM, bd * N_HEADS)],
        compiler_params=pltpu.CompilerParams(vmem_limit_bytes=V7X_VMEM_LIMIT),
        name="sample_proj",
    )(x, g1, w_in, cos, sin, gv, ones_bd, w00, b0, rep, foldt)


def _sample_attend(qkvt_ref, k_ref, v_ref, o_ref, b, win):
    n_col = qkvt_ref.shape[1]
    c_idx = lax.broadcasted_iota(jnp.int32, (n_col, 128), 0)
    l_idx = lax.broadcasted_iota(jnp.int32, (n_col, 128), 1)
    pick = jnp.where((c_idx == b * N_HEADS + l_idx) & (l_idx < N_HEADS), 1.0, 0.0)
    cols = _dot(qkvt_ref[...], pick, precise=True)
    dist = win - lax.broadcasted_iota(jnp.int32, (1, win), 1)
    members = [(dist <= N_KEYS * dil) & (dist % dil == 0) for dil in DILATIONS]

    head = lax.broadcasted_iota(jnp.int32, (N_HEADS, win), 0)
    s = jnp.zeros((N_HEADS, win), F32)
    s_self = jnp.zeros((N_HEADS, 1), F32)
    for h in range(N_HEADS):
        qc = cols[0:HEAD_DIM, h:h + 1]
        kc = cols[HEAD_DIM:2 * HEAD_DIM, h:h + 1]
        s_h = jnp.sum(k_ref[0, h] * qc, axis=0, keepdims=True)
        s = jnp.where(head == h, s_h, s)
        s_self = jnp.where(head[:, 0:1] == h, jnp.sum(qc * kc, axis=0, keepdims=True), s_self)

    es, e_selfs, dens, lses = [], [], [], []
    for mem in members:
        sm = jnp.where(mem, s, NEG)
        m = jnp.maximum(jnp.max(sm, axis=1, keepdims=True), s_self)
        e = jnp.exp(sm - m)
        e_self = jnp.exp(s_self - m)
        den = jnp.sum(e, axis=1, keepdims=True) + e_self
        es.append(e)
        e_selfs.append(e_self)
        dens.append(den)
        lses.append(m + jnp.log(den))
    top = jnp.maximum(jnp.maximum(lses[0], lses[1]), lses[2])
    ws = [jnp.exp(l - top) for l in lses]
    wsum = ws[0] + ws[1] + ws[2]
    coef = [w / (den * wsum) for w, den in zip(ws, dens)]
    p_keys = coef[0] * es[0] + coef[1] * es[1] + coef[2] * es[2]
    p_self = coef[0] * e_selfs[0] + coef[1] * e_selfs[1] + coef[2] * e_selfs[2]

    for h in range(N_HEADS):
        vc = cols[2 * HEAD_DIM:3 * HEAD_DIM, h:h + 1]
        o_ref[0, :, h:h + 1] = (jnp.sum(v_ref[0, h] * p_keys[h:h + 1, :], axis=1, keepdims=True)
                                + p_self[h:h + 1, :] * vc)


def _rope_tables(pos):
    half = HEAD_DIM // 2
    inv = ROPE_THETA ** (-jnp.arange(half, dtype=F32) * 2.0 / HEAD_DIM)
    ang = pos.astype(F32)[:, None] * inv[None, :]
    cos, sin = jnp.cos(ang), jnp.sin(ang)
    cos128 = jnp.concatenate([cos, cos, cos, cos], axis=1)
    sin128 = jnp.concatenate([-sin, sin, -sin, sin], axis=1)
    return cos128, sin128


def kernel(x_prompt, x_sample, cache_win_k, cache_win_v, ln1_g, w_in, sgu_norm_g, sgu_w, sgu_b, w_out, ln2_g,
           w_router_group, b_router_group, w_router_expert, b_router_expert, w_gate, w_up, w_down, lnf_g):
    depth = w_in.shape[0]
    assert depth == 1 and x_sample.shape[1] == 1
    B, S, _ = x_prompt.shape
    bd = x_sample.shape[0]
    win = cache_win_k.shape[2]
    assert S % (max(DILATIONS) * CHUNK) == 0 and win >= max(DILATIONS) * N_KEYS and PAST_LEN % CHUNK == 0
    l = 0

    w_in_b = w_in[l].astype(BF16)
    w_out_b = w_out[l].astype(BF16)
    w1_b = jnp.concatenate([w_gate[l], w_up[l]], axis=-1).astype(BF16)
    w2_b = w_down[l].astype(BF16)
    pad = ROUTER_LANES - N_GROUPS - N_EXPERTS
    w_router = jnp.pad(jnp.concatenate([w_router_group[l], w_router_expert[l]], axis=1), ((0, 0), (0, pad)))
    b_router = jnp.pad(jnp.concatenate([b_router_group[l], b_router_expert[l]]), (0, pad))[None, :]
    g1 = ln1_g[l][None, :]
    g2 = ln2_g[l][None, :]
    gf = lnf_g[None, :]
    gv = sgu_norm_g[l].reshape(1, WIDTH)
    grp = jnp.arange(WIDTH) // HEAD_DIM
    ones_bd = jnp.where(grp[:, None] == grp[None, :], 1.0 / HEAD_DIM, 0.0).astype(BF16)
    wp = jnp.concatenate([sgu_w[l][0::2], sgu_w[l][1::2]], axis=-1)
    bias = jnp.repeat(sgu_b[l].T, HEAD_DIM, axis=1)
    w00 = jnp.repeat(sgu_w[l][:, 0, 0], HEAD_DIM)[None, :]
    b0 = jnp.repeat(sgu_b[l][:, 0], HEAD_DIM)[None, :]

    cos_s, sin_s = _rope_tables(PAST_LEN + jnp.arange(1))
    rep = (jnp.arange(bd * N_HEADS)[:, None] // N_HEADS == jnp.arange(bd)[None, :]).astype(F32)
    foldt = (jnp.arange(HEAD_DIM)[:, None] == jnp.arange(WIDTH)[None, :] % HEAD_DIM).astype(F32)
    xs = x_sample.reshape(bd, D_MODEL)
    a_s, k_s, v_s, vn_s, qkvt = _sample_proj(xs, g1, w_in[l], cos_s, sin_s, gv, ones_bd, w00, b0, rep, foldt)
    to_pos_minor = lambda c: jnp.transpose(c, (0, 2, 3, 1))

    cos_p, sin_p = _rope_tables(jnp.arange(S))
    a_p, q_p, k_p, v_p, kt_p, vt_p = _prompt_proj(x_prompt, g1, w_in_b, cos_p, sin_p, gv, ones_bd, wp, bias)
    b_p = _prompt_attention(q_p, k_p, v_p)
    n = B * S
    assert n % MOE_BLOCK == 0
    wr_t = w_router.T
    wr_hi = wr_t.astype(BF16)
    wr_lo = (wr_t - wr_hi.astype(F32)).astype(BF16)
    pair_idx = jnp.arange(2 * MOE_BLOCK)
    tri = (pair_idx[:, None] <= pair_idx[None, :]).astype(BF16)
    lane_idx = jnp.arange(ROUTER_LANES)
    ltri = (lane_idx[None, :] < lane_idx[:, None]).astype(BF16)
    xp2, h2, mrow, mcol, tab_f, o3 = _mix_route_sort(
        a_p.reshape(n, WIDTH), b_p.reshape(n, WIDTH), x_prompt.reshape(n, D_MODEL),
        w_out_b, g2, wr_hi, wr_lo, b_router.reshape(ROUTER_LANES, 1), tri, ltri,
        qkvt, to_pos_minor(cache_win_k[l]), to_pos_minor(cache_win_v[l]))
    tab = tab_f[:, :N_EXPERTS, 0:3].astype(jnp.int32).reshape(-1)

    b_s = jnp.transpose(o3, (0, 2, 1)).reshape(bd, WIDTH)
    xs2, hs2, gates_s = _mix_router(a_s, b_s, xs, w_out[l], g2, w_router, b_router, tm=bd, precise=True)

    y_prompt, y_sample = _moe_sparse(tab, h2, xp2, mrow, mcol, w1_b, w2_b, gf, hs2, gates_s, xs2)
    y_prompt = y_prompt.reshape(B, S, D_MODEL)
    y_sample = y_sample.reshape(bd, 1, D_MODEL)
    buf_p = min(MAX_WINDOW, S)
    to_win = lambda t: jnp.transpose(t.reshape(1, B, N_HEADS, HEAD_DIM, buf_p), (0, 1, 4, 2, 3))
    new_k_p = to_win(kt_p)
    new_v_p = to_win(vt_p)

    shape_s = (1, bd, 1, N_HEADS, HEAD_DIM)
    return (y_prompt, y_sample, new_k_p, new_v_p,
            k_s.reshape(shape_s), v_s.reshape(shape_s), vn_s.reshape(shape_s))
```

```python
import functools

import jax
import jax.numpy as jnp
import numpy as np
from jax import lax
from jax.experimental import pallas as pl
from jax.experimental.pallas import tpu as pltpu

F32 = jnp.float32
BF16 = jnp.bfloat16

D_MODEL = 1024
HEAD_DIM = 64
N_HEADS = 8
WIDTH = N_HEADS * HEAD_DIM
PROJ_COLS = 5 * WIDTH
CHUNK = 128
DILATIONS = (1, 4, 16)
N_KEYS = 128
MAX_WINDOW = 2048
PAST_LEN = 16384
ROPE_THETA = 10000.0
N_GROUPS = 4
EXPERTS_PER_GROUP = 8
N_EXPERTS = N_GROUPS * EXPERTS_PER_GROUP
D_EXPERT = 128
EPS = 1e-6
NEG = -1e30
TILES_PER_STEP = 16
MOE_BLOCK = 512
MOE_ROW_ALIGN = 16
MOE_CHUNK = 64
MOE_EXPERTS_PER_STEP = 16
MOE_SORT_ROWS = -(-(2 * MOE_BLOCK + N_EXPERTS * (MOE_ROW_ALIGN - 1)) // 512) * 512
ROUTER_LANES = 128
V7X_VMEM_LIMIT = 56 * 1024 * 1024


def _rmsnorm(x, g):
    return x * lax.rsqrt(jnp.mean(x * x, axis=-1, keepdims=True) + EPS) * g


def _tile_lanes(t, reps):
    return jnp.concatenate([t] * reps, axis=1)


def _rope(t, cos, sin_signed):
    lane = lax.broadcasted_iota(jnp.int32, t.shape, 1)
    first_half = (lane % HEAD_DIM) < (HEAD_DIM // 2)
    n = t.shape[1]
    partner = jnp.where(first_half, pltpu.roll(t, n - HEAD_DIM // 2, 1), pltpu.roll(t, HEAD_DIM // 2, 1))
    return t * cos + partner * sin_signed


def _dot(a, b, precise=False):
    if precise:
        return jnp.dot(a.astype(F32), b.astype(F32), preferred_element_type=F32,
                       precision=lax.Precision.HIGHEST)
    return jnp.dot(a.astype(BF16), b.astype(BF16), preferred_element_type=F32)


def _group_rmsnorm(va, ones_bd, gv, precise=False):
    ms = _dot(va * va, ones_bd, precise)
    return va * lax.rsqrt(ms + EPS) * gv


def _proj_kernel(x_ref, g1_ref, w_ref, cos_ref, sin_ref, gv_ref, ones_ref, wp_ref, bias_ref,
                 a_ref, q_ref, k_ref, v_ref, kt_ref, vt_ref, wb, *, tm, first_win_tile):
    @pl.when((pl.program_id(0) == 0) & (pl.program_id(1) == 0))
    def _():
        wb[...] = w_ref[...].astype(BF16)

    h = _rmsnorm(x_ref[0], g1_ref[...]).astype(BF16)

    def proj(i):
        return jnp.dot(h, wb[:, i * WIDTH:(i + 1) * WIDTH], preferred_element_type=F32)

    cos = _tile_lanes(cos_ref[...], WIDTH // 128)
    sin = _tile_lanes(sin_ref[...], WIDTH // 128)
    q_ref[0] = _rope(proj(2), cos, sin) * (HEAD_DIM ** -0.5)
    k_ref[0] = _rope(proj(3), cos, sin)
    v_ref[0] = proj(4)

    u = proj(0)
    vn = _group_rmsnorm(proj(1), ones_ref[...], gv_ref[...]).astype(BF16)

    lane = lax.broadcasted_iota(jnp.int32, (CHUNK, 128), 1)
    left = lane < HEAD_DIM
    row = lax.broadcasted_iota(jnp.int32, (CHUNK, 2 * CHUNK), 0)
    col = lax.broadcasted_iota(jnp.int32, (CHUNK, 2 * CHUNK), 1)
    causal = (col % CHUNK) <= row
    zero = jnp.zeros((CHUNK, 128), BF16)
    wps = [jnp.where(causal, wp_ref[gp], 0.0).astype(BF16) for gp in range(N_HEADS // 2)]
    for c in range(tm // CHUNK):
        rows = slice(c * CHUNK, (c + 1) * CHUNK)
        mixes = []
        for gp in range(N_HEADS // 2):
            vv = vn[rows, gp * 128:(gp + 1) * 128]
            v2 = jnp.concatenate([jnp.where(left, vv, zero), jnp.where(left, zero, vv)], axis=0)
            mixes.append(jnp.dot(wps[gp], v2, preferred_element_type=F32))
        mix = jnp.concatenate(mixes, axis=1) + bias_ref[...]
        a_ref[0, rows, :] = (u[rows, :] * mix).astype(a_ref.dtype)

    @pl.when(pl.program_id(1) >= first_win_tile)
    def _():
        kt_ref[0] = k_ref[0].T
        vt_ref[0] = v_ref[0].T


def _prompt_proj(x, g1, w_in_b, cos, sin, gv, ones_bd, wp, bias, *, tm=512):
    B, S, _ = x.shape
    const2 = lambda b, j: (0, 0)
    out_sds = lambda dt: jax.ShapeDtypeStruct((B, S, WIDTH), dt)
    tile = pl.BlockSpec((1, tm, WIDTH), lambda b, j: (b, j, 0))
    win = min(MAX_WINDOW, S)
    first_win_tile = (S - win) // tm
    tile_t = pl.BlockSpec((1, WIDTH, tm), lambda b, j: (b, 0, jnp.maximum(j - first_win_tile, 0)))
    win_sds = jax.ShapeDtypeStruct((B, WIDTH, win), F32)
    return pl.pallas_call(
        functools.partial(_proj_kernel, tm=tm, first_win_tile=first_win_tile),
        grid=(B, S // tm),
        in_specs=[
            pl.BlockSpec((1, tm, D_MODEL), lambda b, j: (b, j, 0)),
            pl.BlockSpec((1, D_MODEL), const2),
            pl.BlockSpec((D_MODEL, PROJ_COLS), const2, pipeline_mode=pl.Buffered(1)),
            pl.BlockSpec((tm, 128), lambda b, j: (j, 0)),
            pl.BlockSpec((tm, 128), lambda b, j: (j, 0)),
            pl.BlockSpec((1, WIDTH), const2),
            pl.BlockSpec((WIDTH, WIDTH), const2),
            pl.BlockSpec((N_HEADS // 2, CHUNK, 2 * CHUNK), lambda b, j: (0, 0, 0)),
            pl.BlockSpec((CHUNK, WIDTH), const2),
        ],
        out_specs=[tile, tile, tile, tile, tile_t, tile_t],
        out_shape=[out_sds(BF16), out_sds(F32), out_sds(F32), out_sds(F32), win_sds, win_sds],
        scratch_shapes=[pltpu.VMEM((D_MODEL, PROJ_COLS), BF16)],
        compiler_params=pltpu.CompilerParams(
            dimension_semantics=("arbitrary", "arbitrary"), vmem_limit_bytes=V7X_VMEM_LIMIT),
        name="prompt_proj_sgu",
    )(x, g1, w_in_b, cos, sin, gv, ones_bd, wp, bias)


def _attn_kernel(q_ref, k_ref, v_ref, o_ref, qd, kd, vd, res_o, res_l, nat_o, nat_l, bias, *, seq):
    n_tiles = seq // CHUNK
    lane = lax.broadcasted_iota(jnp.int32, (CHUNK, 128), 1)
    left = lane < HEAD_DIM
    qi2 = lax.broadcasted_iota(jnp.int32, (2 * CHUNK, 2 * CHUNK), 0) % CHUNK
    kj2 = lax.broadcasted_iota(jnp.int32, (2 * CHUNK, 2 * CHUNK), 1)
    dist2 = CHUNK + qi2 - kj2
    band2 = (dist2 >= 0) & (dist2 <= N_KEYS)
    zero_q = jnp.zeros((CHUNK, 128), BF16)
    bias[0] = jnp.where(band2, 0.0, NEG)
    bias[1] = jnp.where(band2 & (kj2 >= CHUNK), 0.0, NEG)

    kd[0:CHUNK, :] = jnp.zeros((CHUNK, 128), BF16)
    vd[0:CHUNK, :] = jnp.zeros((CHUNK, 128), BF16)

    for p, dil in enumerate(DILATIONS):
        sub = seq // dil
        nb = sub // CHUNK
        for r in range(dil):
            src = pl.ds(r, sub, stride=dil) if dil > 1 else pl.ds(0, sub)
            qd[r * sub:(r + 1) * sub, :] = q_ref[0, src, :].astype(BF16)
            kd[CHUNK + r * sub:CHUNK + (r + 1) * sub, :] = k_ref[0, src, :].astype(BF16)
            vd[CHUNK + r * sub:CHUNK + (r + 1) * sub, :] = v_ref[0, src, :].astype(BF16)

        def tile_body(g, i, p=p, nb=nb):
            t = g * TILES_PER_STEP + i
            row = pl.multiple_of(t * CHUNK, CHUNK)
            qt = qd[pl.ds(row, CHUNK), :]
            k2 = kd[pl.ds(row, 2 * CHUNK), :]
            v2 = vd[pl.ds(row, 2 * CHUNK), :]
            if TILES_PER_STEP % nb == 0:
                variant = 1 if i % nb == 0 else 0
            elif i == 0:
                variant = jnp.where((g * TILES_PER_STEP) % nb == 0, 1, 0)
            else:
                variant = 0
            q2 = jnp.concatenate([jnp.where(left, qt, zero_q), jnp.where(left, zero_q, qt)], axis=0)
            s = lax.dot_general(q2, k2, (((1,), (1,)), ((), ())), preferred_element_type=F32)
            s = s + bias[variant]
            m = jnp.max(s, axis=1, keepdims=True)
            e = jnp.exp(s - m)
            den = jnp.sum(e, axis=1, keepdims=True)
            pv = jnp.dot(e.astype(BF16), v2, preferred_element_type=F32) / den
            lse = jnp.broadcast_to(m + jnp.log(den), (2 * CHUNK, 128))
            res_o[p, pl.ds(row, CHUNK), :] = jnp.where(left, pv[0:CHUNK], pv[CHUNK:2 * CHUNK])
            res_l[p, pl.ds(row, CHUNK), :] = jnp.where(left, lse[0:CHUNK], lse[CHUNK:2 * CHUNK])

        def group_body(g, carry, tile_body=tile_body):
            for i in range(TILES_PER_STEP):
                tile_body(g, i)
            return carry

        lax.fori_loop(0, n_tiles // TILES_PER_STEP, group_body, 0)

    for p, dil in enumerate(DILATIONS):
        if dil == 1:
            continue
        sub = seq // dil
        for r in range(dil):
            nat_o[p - 1, pl.ds(r, sub, stride=dil), :] = res_o[p, r * sub:(r + 1) * sub, :]
            nat_l[p - 1, pl.ds(r, sub, stride=dil), :] = res_l[p, r * sub:(r + 1) * sub, :]

    rows_per_step = 256

    def merge_body(c, carry):
        rows = pl.ds(pl.multiple_of(c * rows_per_step, rows_per_step), rows_per_step)
        l0, l1, l2 = res_l[0, rows, :], nat_l[0, rows, :], nat_l[1, rows, :]
        top = jnp.maximum(jnp.maximum(l0, l1), l2)
        w0, w1, w2 = jnp.exp(l0 - top), jnp.exp(l1 - top), jnp.exp(l2 - top)
        num = w0 * res_o[0, rows, :] + w1 * nat_o[0, rows, :] + w2 * nat_o[1, rows, :]
        o_ref[0, rows, :] = (num / (w0 + w1 + w2)).astype(o_ref.dtype)
        return carry

    lax.fori_loop(0, seq // rows_per_step, merge_body, 0)


def _prompt_attention(q, k, v):
    B, S, _ = q.shape
    blk = pl.BlockSpec((1, S, 128), lambda b, hp: (b, 0, hp))
    return pl.pallas_call(
        functools.partial(_attn_kernel, seq=S),
        grid=(B, WIDTH // 128),
        in_specs=[blk, blk, blk],
        out_specs=blk,
        out_shape=jax.ShapeDtypeStruct((B, S, WIDTH), BF16),
        scratch_shapes=[
            pltpu.VMEM((S, 128), BF16),
            pltpu.VMEM((S + CHUNK, 128), BF16),
            pltpu.VMEM((S + CHUNK, 128), BF16),
            pltpu.VMEM((len(DILATIONS), S, 128), F32),
            pltpu.VMEM((len(DILATIONS), S, 128), F32),
            pltpu.VMEM((len(DILATIONS) - 1, S, 128), F32),
            pltpu.VMEM((len(DILATIONS) - 1, S, 128), F32),
            pltpu.VMEM((2, 2 * CHUNK, 2 * CHUNK), F32),
        ],
        compiler_params=pltpu.CompilerParams(
            dimension_semantics=("arbitrary", "arbitrary"), vmem_limit_bytes=V7X_VMEM_LIMIT),
        name="prompt_dilated_attention",
    )(q, k, v)


def _route(logits):
    lane = lax.broadcasted_iota(jnp.int32, logits.shape, 1)
    big = jnp.int32(ROUTER_LANES)
    lg = jnp.where(lane < N_GROUPS, logits, NEG)
    gmax = jnp.max(lg, axis=1, keepdims=True)
    gp = 1.0 / jnp.sum(jnp.exp(lg - gmax), axis=1, keepdims=True)
    gi = jnp.min(jnp.where(lg == gmax, lane, big), axis=1, keepdims=True)
    lo = N_GROUPS + EXPERTS_PER_GROUP * gi
    le = jnp.where((lane >= lo) & (lane < lo + EXPERTS_PER_GROUP), logits, NEG)
    m1 = jnp.max(le, axis=1, keepdims=True)
    i1 = jnp.min(jnp.where(le == m1, lane, big), axis=1, keepdims=True)
    le2 = jnp.where(lane == i1, NEG, le)
    m2 = jnp.max(le2, axis=1, keepdims=True)
    i2 = jnp.min(jnp.where(le2 == m2, lane, big), axis=1, keepdims=True)
    e2 = jnp.exp(m2 - m1)
    w1 = 1.0 / (1.0 + e2)
    w2 = e2 / (1.0 + e2)
    return jnp.where(lane == i1, gp * w1, jnp.where(lane == i2, gp * w2, 0.0))


def _mix_router_kernel(a_ref, b_ref, x_ref, wo_ref, g2_ref, wr_ref, br_ref, xp_ref, h2_ref, gates_ref, *, precise):
    mixed = (_dot(a_ref[...], wo_ref[0:WIDTH, :], precise)
             + _dot(b_ref[...], wo_ref[WIDTH:2 * WIDTH, :], precise))
    xp = x_ref[...] + mixed
    xp_ref[...] = xp
    h2 = _rmsnorm(xp, g2_ref[...])
    h2_ref[...] = h2.astype(h2_ref.dtype)
    logits = jnp.dot(h2, wr_ref[...], preferred_element_type=F32,
                     precision=lax.Precision.HIGHEST) + br_ref[...]
    gates_ref[...] = _route(logits)


def _mix_router(a, b, x, w_out, g2, w_router, b_router, *, tm, precise=False):
    n = x.shape[0]
    const = lambda i: (0, 0)
    row_blk = lambda w: pl.BlockSpec((tm, w), lambda i: (i, 0))
    return pl.pallas_call(
        functools.partial(_mix_router_kernel, precise=precise),
        grid=(n // tm,),
        in_specs=[row_blk(WIDTH), row_blk(WIDTH), row_blk(D_MODEL),
                  pl.BlockSpec((2 * WIDTH, D_MODEL), const),
                  pl.BlockSpec((1, D_MODEL), const),
                  pl.BlockSpec((D_MODEL, ROUTER_LANES), const),
                  pl.BlockSpec((1, ROUTER_LANES), const)],
        out_specs=[row_blk(D_MODEL), row_blk(D_MODEL), row_blk(ROUTER_LANES)],
        out_shape=[jax.ShapeDtypeStruct((n, D_MODEL), F32),
                   jax.ShapeDtypeStruct((n, D_MODEL), BF16),
                   jax.ShapeDtypeStruct((n, ROUTER_LANES), F32)],
        compiler_params=pltpu.CompilerParams(
            dimension_semantics=("arbitrary",), vmem_limit_bytes=V7X_VMEM_LIMIT),
        name="outproj_router",
    )(a, b, x, w_out, g2, w_router, b_router)


def _nt_dot(w, t):
    return lax.dot_general(w, t, (((1,), (1,)), ((), ())), preferred_element_type=F32)


def _route_t(logits_t):
    row = lax.broadcasted_iota(jnp.int32, logits_t.shape, 0)
    big = jnp.int32(ROUTER_LANES)
    lg = jnp.where(row < N_GROUPS, logits_t, NEG)
    gmax = jnp.max(lg, axis=0, keepdims=True)
    gp = 1.0 / jnp.sum(jnp.exp(lg - gmax), axis=0, keepdims=True)
    gi = jnp.min(jnp.where(lg == gmax, row, big), axis=0, keepdims=True)
    lo = N_GROUPS + EXPERTS_PER_GROUP * gi
    le = jnp.where((row >= lo) & (row < lo + EXPERTS_PER_GROUP), logits_t, NEG)
    m1 = jnp.max(le, axis=0, keepdims=True)
    i1 = jnp.min(jnp.where(le == m1, row, big), axis=0, keepdims=True)
    le2 = jnp.where(row == i1, NEG, le)
    m2 = jnp.max(le2, axis=0, keepdims=True)
    i2 = jnp.min(jnp.where(le2 == m2, row, big), axis=0, keepdims=True)
    e2 = jnp.exp(m2 - m1)
    return i1 - N_GROUPS, i2 - N_GROUPS, gp / (1.0 + e2), gp * e2 / (1.0 + e2)


def _mix_route_sort_kernel(a_ref, b_ref, x_ref, wo_ref, g2_ref, wrh_ref, wrl_ref, brc_ref, tri_ref, ltri_ref,
                           qkvt_ref, ck_ref, cv_ref,
                           xp_ref, h2_ref, mrow_ref, mcol_ref, tab_ref, so_ref, wob, *, n_seq, win):
    t = MOE_BLOCK

    @pl.when(pl.program_id(0) == 0)
    def _():
        wob[...] = wo_ref[...].astype(BF16)

    xp = x_ref[...] + _dot(a_ref[...], wob[0:WIDTH, :]) + _dot(b_ref[...], wob[WIDTH:2 * WIDTH, :])
    xp_ref[...] = xp
    h2 = _rmsnorm(xp, g2_ref[...])
    hi = h2.astype(BF16)
    h2_ref[...] = hi
    lo = (h2 - hi.astype(F32)).astype(BF16)
    logits_t = (_nt_dot(wrh_ref[...], hi) + _nt_dot(wrh_ref[...], lo) + _nt_dot(wrl_ref[...], hi)
                + brc_ref[...])
    ex1, ex2, gate1, gate2 = _route_t(logits_t)

    pair_e = jnp.concatenate([ex1, ex2], axis=1)
    row = lax.broadcasted_iota(jnp.int32, (ROUTER_LANES, 2 * t), 0)
    onehot = jnp.where(row == pair_e, 1.0, 0.0)
    cum = _dot(onehot, tri_ref[...])
    rank = jnp.sum(onehot * cum, axis=0, keepdims=True) - 1.0
    counts = cum[:, 2 * t - 1:2 * t]
    units = jnp.floor((counts + (MOE_ROW_ALIGN - 1)) * (1.0 / MOE_ROW_ALIGN))
    off_units = _dot(ltri_ref[...], jnp.broadcast_to(units, (ROUTER_LANES, 128)))
    off = off_units[:, 0:1] * MOE_ROW_ALIGN
    dst = jnp.sum(onehot * off, axis=0, keepdims=True) + rank

    r8 = lax.broadcasted_iota(jnp.int32, (8, t), 0)
    mrow_ref[0] = jnp.where(r8 == 0, dst[:, 0:t], jnp.where(r8 == 1, dst[:, t:2 * t],
                            jnp.where(r8 == 2, gate1, jnp.where(r8 == 3, gate2, 0.0))))
    r128 = lax.broadcasted_iota(jnp.int32, (ROUTER_LANES, t), 0)
    meta = jnp.where(r128 == 0, dst[:, 0:t], jnp.where(r128 == 1, dst[:, t:2 * t],
                     jnp.where(r128 == 2, gate1, jnp.where(r128 == 3, gate2, 0.0))))
    mcol_ref[...] = meta.T
    lane = lax.broadcasted_iota(jnp.int32, (ROUTER_LANES, 128), 1)
    chunks = jnp.floor((units * MOE_ROW_ALIGN + (MOE_CHUNK - 1)) * (1.0 / MOE_CHUNK))
    tab_ref[0] = jnp.where(lane == 0, off, jnp.where(lane == 1, chunks,
                           jnp.where(lane == 2, off + units * MOE_ROW_ALIGN, 0.0)))

    _sample_attend(qkvt_ref, ck_ref, cv_ref, so_ref, jnp.minimum(pl.program_id(0), n_seq - 1), win)


def _mix_route_sort(a, b, x, w_out, g2, wr_hi, wr_lo, br_col, tri, ltri, qkvt, cache_k_t, cache_v_t):
    n = x.shape[0]
    t = MOE_BLOCK
    nblk = n // t
    n_seq, _, _, win = cache_k_t.shape
    assert n_seq <= nblk, "one sample sequence rides on each prompt block"
    const = lambda i: (0, 0)
    row_blk = lambda w: pl.BlockSpec((t, w), lambda i: (i, 0))
    seq_blk = lambda shape: pl.BlockSpec((1,) + shape, lambda i: (jnp.minimum(i, n_seq - 1),) + (0,) * len(shape))
    return pl.pallas_call(
        functools.partial(_mix_route_sort_kernel, n_seq=n_seq, win=win),
        grid=(nblk,),
        in_specs=[row_blk(WIDTH), row_blk(WIDTH), row_blk(D_MODEL),
                  pl.BlockSpec((2 * WIDTH, D_MODEL), const, pipeline_mode=pl.Buffered(1)),
                  pl.BlockSpec((1, D_MODEL), const),
                  pl.BlockSpec((ROUTER_LANES, D_MODEL), const),
                  pl.BlockSpec((ROUTER_LANES, D_MODEL), const),
                  pl.BlockSpec((ROUTER_LANES, 1), const),
                  pl.BlockSpec((2 * t, 2 * t), const),
                  pl.BlockSpec((ROUTER_LANES, ROUTER_LANES), const),
                  pl.BlockSpec(qkvt.shape, const),
                  seq_blk((N_HEADS, HEAD_DIM, win)), seq_blk((N_HEADS, HEAD_DIM, win))],
        out_specs=[row_blk(D_MODEL), row_blk(D_MODEL),
                   pl.BlockSpec((1, 8, t), lambda i: (i, 0, 0)),
                   row_blk(ROUTER_LANES),
                   pl.BlockSpec((1, ROUTER_LANES, 128), lambda i: (i, 0, 0)),
                   seq_blk((HEAD_DIM, N_HEADS))],
        out_shape=[jax.ShapeDtypeStruct((n, D_MODEL), F32),
                   jax.ShapeDtypeStruct((n, D_MODEL), BF16),
                   jax.ShapeDtypeStruct((nblk, 8, t), F32),
                   jax.ShapeDtypeStruct((n, ROUTER_LANES), F32),
                   jax.ShapeDtypeStruct((nblk, ROUTER_LANES, 128), F32),
                   jax.ShapeDtypeStruct((n_seq, HEAD_DIM, N_HEADS), F32)],
        scratch_shapes=[pltpu.VMEM((2 * WIDTH, D_MODEL), BF16)],
        compiler_params=pltpu.CompilerParams(
            dimension_semantics=("arbitrary",), vmem_limit_bytes=V7X_VMEM_LIMIT),
        name="outproj_route_sort",
    )(a, b, x, w_out, g2, wr_hi, wr_lo, br_col, tri, ltri, qkvt, cache_k_t, cache_v_t)


def _moe_sparse_kernel(tab_ref, h_ref, xp_ref, mrow_ref, mcol_ref, w1_ref, w2_ref, gf_ref, sh_ref, sg_ref, sx_ref,
                       y_ref, sy_ref, xs, hs, os, shid):
    t = MOE_BLOCK
    blk = pl.program_id(0)
    tab = lambda e, c: tab_ref[(blk * N_EXPERTS + e) * 3 + c]

    mrow = mrow_ref[0]
    dst1 = mrow[0:1, :].astype(jnp.int32)
    dst2 = mrow[1:2, :].astype(jnp.int32)
    piece = 512
    for r0 in range(0, MOE_SORT_ROWS, piece):
        d_idx = lax.broadcasted_iota(jnp.int32, (piece, t), 0) + r0
        sel = jnp.where((d_idx == dst1) | (d_idx == dst2), 1.0, 0.0)
        xs[r0:r0 + piece, :] = _dot(sel, h_ref[...]).astype(BF16)
    tail = slice(MOE_SORT_ROWS, MOE_SORT_ROWS + MOE_CHUNK)
    xs[tail, :] = jnp.zeros((MOE_CHUNK, D_MODEL), BF16)
    os[...] = jnp.zeros(os.shape, BF16)

    def gate_up(e, r0):
        ab = jnp.dot(xs[pl.ds(r0, MOE_CHUNK), :], w1_ref[e], preferred_element_type=F32)
        a = ab[:, :D_EXPERT]
        return (a * (1.0 / (1.0 + jnp.exp(-a))) * ab[:, D_EXPERT:]).astype(BF16)

    def down(e, hid):
        return jnp.dot(hid, w2_ref[e], preferred_element_type=F32).astype(BF16)

    def first_gate_up(g, carry):
        for i in range(MOE_EXPERTS_PER_STEP):
            e = g * MOE_EXPERTS_PER_STEP + i
            r0 = pl.multiple_of(tab(e, 0), MOE_ROW_ALIGN)
            hs[pl.ds(r0, MOE_CHUNK), :] = gate_up(e, r0)
        return carry

    def first_down(g, carry):
        for i in range(MOE_EXPERTS_PER_STEP):
            e = g * MOE_EXPERTS_PER_STEP + i
            r0 = pl.multiple_of(tab(e, 0), MOE_ROW_ALIGN)
            os[pl.ds(r0, MOE_CHUNK), :] = down(e, hs[pl.ds(r0, MOE_CHUNK), :])
        return carry

    lax.fori_loop(0, N_EXPERTS // MOE_EXPERTS_PER_STEP, first_gate_up, 0)
    lax.fori_loop(0, N_EXPERTS // MOE_EXPERTS_PER_STEP, first_down, 0)

    def more_chunks(e, carry):
        off, n_chunks, end = tab(e, 0), tab(e, 1), tab(e, 2)

        def chunk(c, carry):
            r0 = pl.multiple_of(off + c * MOE_CHUNK, MOE_ROW_ALIGN)
            rows = r0 + lax.broadcasted_iota(jnp.int32, (MOE_CHUNK, D_MODEL), 0)
            os[pl.ds(r0, MOE_CHUNK), :] = jnp.where(rows < end, down(e, gate_up(e, r0)), os[pl.ds(r0, MOE_CHUNK), :])
            return carry

        return lax.fori_loop(1, n_chunks, chunk, carry)

    lax.fori_loop(0, N_EXPERTS, more_chunks, 0)

    mcol = mcol_ref[...]
    d1c = mcol[:, 0:1].astype(jnp.int32)
    d2c = mcol[:, 1:2].astype(jnp.int32)
    l_idx = lax.broadcasted_iota(jnp.int32, (t, MOE_SORT_ROWS), 1)
    comb = jnp.where(l_idx == d1c, mcol[:, 2:3], 0.0) + jnp.where(l_idx == d2c, mcol[:, 3:4], 0.0)
    y = xp_ref[...] + _dot(comb, os[0:MOE_SORT_ROWS, :])
    y_ref[...] = _rmsnorm(y, gf_ref[...])

    @pl.when(blk == 0)
    def _():
        sh = sh_ref[...]
        gates = sg_ref[...]
        for e in range(N_EXPERTS):
            ab = jnp.dot(sh, w1_ref[e], preferred_element_type=F32)
            a = ab[:, :D_EXPERT]
            gate = gates[:, N_GROUPS + e:N_GROUPS + e + 1]
            shid[:, e * D_EXPERT:(e + 1) * D_EXPERT] = (
                a * (1.0 / (1.0 + jnp.exp(-a))) * ab[:, D_EXPERT:] * gate).astype(BF16)
        w2_all = w2_ref[...].reshape(N_EXPERTS * D_EXPERT, D_MODEL)
        ys = sx_ref[...] + jnp.dot(shid[...], w2_all, preferred_element_type=F32)
        sy_ref[...] = _rmsnorm(ys, gf_ref[...])


def _moe_sparse(tab, h2, xp, mrow, mcol, w1_b, w2_b, gf, s_h, s_gates, s_x):
    n = h2.shape[0]
    n_s = s_h.shape[0]
    t = MOE_BLOCK
    row_blk = lambda w: pl.BlockSpec((t, w), lambda i, tab: (i, 0))
    whole = lambda shape: pl.BlockSpec(shape, lambda i, tab: (0,) * len(shape))
    resident = lambda shape: pl.BlockSpec(shape, lambda i, tab: (0,) * len(shape), pipeline_mode=pl.Buffered(1))
    return pl.pallas_call(
        _moe_sparse_kernel,
        grid_spec=pltpu.PrefetchScalarGridSpec(
            num_scalar_prefetch=1,
            grid=(n // t,),
            in_specs=[row_blk(D_MODEL), row_blk(D_MODEL),
                      pl.BlockSpec((1, 8, t), lambda i, tab: (i, 0, 0)),
                      row_blk(ROUTER_LANES),
                      resident(w1_b.shape), resident(w2_b.shape),
                      whole((1, D_MODEL)),
                      whole(s_h.shape), whole(s_gates.shape), whole(s_x.shape)],
            out_specs=[row_blk(D_MODEL), whole((n_s, D_MODEL))],
            scratch_shapes=[pltpu.VMEM((MOE_SORT_ROWS + MOE_CHUNK, D_MODEL), BF16),
                            pltpu.VMEM((MOE_SORT_ROWS + MOE_CHUNK, D_EXPERT), BF16),
                            pltpu.VMEM((MOE_SORT_ROWS + MOE_CHUNK, D_MODEL), BF16),
                            pltpu.VMEM((n_s, N_EXPERTS * D_EXPERT), BF16)]),
        out_shape=[jax.ShapeDtypeStruct((n, D_MODEL), F32), jax.ShapeDtypeStruct((n_s, D_MODEL), F32)],
        compiler_params=pltpu.CompilerParams(
            dimension_semantics=("arbitrary",), vmem_limit_bytes=V7X_VMEM_LIMIT),
        name="moe_sparse",
    )(tab, h2, xp, mrow, mcol, w1_b, w2_b, gf, s_h, s_gates, s_x)


def _sample_proj_kernel(x_ref, g1_ref, w_ref, cos_ref, sin_ref, gv_ref, ones_ref, w00_ref, b0_ref,
                        rep_ref, foldt_ref, a_ref, k_ref, v_ref, vn_ref, qkvt_ref):
    h = _rmsnorm(x_ref[...], g1_ref[...])

    def proj(i):
        return _dot(h, w_ref[:, i * WIDTH:(i + 1) * WIDTH], precise=True)

    cos = _tile_lanes(cos_ref[...], WIDTH // 128)
    sin = _tile_lanes(sin_ref[...], WIDTH // 128)
    q = _rope(proj(2), cos, sin) * (HEAD_DIM ** -0.5)
    k = _rope(proj(3), cos, sin)
    v = proj(4)
    vn = _group_rmsnorm(proj(1), ones_ref[...], gv_ref[...], precise=True)
    a_ref[...] = proj(0) * (w00_ref[...] * vn + b0_ref[...])
    k_ref[...] = k
    v_ref[...] = v
    vn_ref[...] = vn

    n_rep = rep_ref.shape[0]
    r_idx = lax.broadcasted_iota(jnp.int32, (n_rep, WIDTH), 0)
    l_idx = lax.broadcasted_iota(jnp.int32, (n_rep, WIDTH), 1)
    own = (l_idx // HEAD_DIM) == (r_idx % N_HEADS)
    for t, src in enumerate((q, k, v)):
        rep = _dot(rep_ref[...], src, precise=True)
        qkvt_ref[t * HEAD_DIM:(t + 1) * HEAD_DIM, :] = lax.dot_general(
            foldt_ref[...], jnp.where(own, rep, 0.0), (((1,), (1,)), ((), ())),
            preferred_element_type=F32, precision=lax.Precision.HIGHEST)


def _sample_proj(x, g1, w_in, cos, sin, gv, ones_bd, w00, b0, rep, foldt):
    bd = x.shape[0]
    sds = lambda r, c: jax.ShapeDtypeStruct((r, c), F32)
    return pl.pallas_call(
        _sample_proj_kernel,
        out_shape=[sds(bd, WIDTH), sds(bd, WIDTH), sds(bd, WIDTH), sds(bd, WIDTH),
                   sds(3 * HEAD_DIM, bd * N_HEADS)],
        compiler_params=pltpu.CompilerParams(vmem_limit_bytes=V7X_VMEM_LIMIT),
        name="sample_proj",
    )(x, g1, w_in, cos, sin, gv, ones_bd, w00, b0, rep, foldt)


def _sample_attend(qkvt_ref, k_ref, v_ref, o_ref, b, win):
    n_col = qkvt_ref.shape[1]
    c_idx = lax.broadcasted_iota(jnp.int32, (n_col, 128), 0)
    l_idx = lax.broadcasted_iota(jnp.int32, (n_col, 128), 1)
    pick = jnp.where((c_idx == b * N_HEADS + l_idx) & (l_idx < N_HEADS), 1.0, 0.0)
    cols = _dot(qkvt_ref[...], pick, precise=True)
    dist = win - lax.broadcasted_iota(jnp.int32, (1, win), 1)
    members = [(dist <= N_KEYS * dil) & (dist % dil == 0) for dil in DILATIONS]

    head = lax.broadcasted_iota(jnp.int32, (N_HEADS, win), 0)
    s = jnp.zeros((N_HEADS, win), F32)
    s_self = jnp.zeros((N_HEADS, 1), F32)
    for h in range(N_HEADS):
        qc = cols[0:HEAD_DIM, h:h + 1]
        kc = cols[HEAD_DIM:2 * HEAD_DIM, h:h + 1]
        s_h = jnp.sum(k_ref[0, h] * qc, axis=0, keepdims=True)
        s = jnp.where(head == h, s_h, s)
        s_self = jnp.where(head[:, 0:1] == h, jnp.sum(qc * kc, axis=0, keepdims=True), s_self)

    es, e_selfs, dens, lses = [], [], [], []
    for mem in members:
        sm = jnp.where(mem, s, NEG)
        m = jnp.maximum(jnp.max(sm, axis=1, keepdims=True), s_self)
        e = jnp.exp(sm - m)
        e_self = jnp.exp(s_self - m)
        den = jnp.sum(e, axis=1, keepdims=True) + e_self
        es.append(e)
        e_selfs.append(e_self)
        dens.append(den)
        lses.append(m + jnp.log(den))
    top = jnp.maximum(jnp.maximum(lses[0], lses[1]), lses[2])
    ws = [jnp.exp(l - top) for l in lses]
    wsum = ws[0] + ws[1] + ws[2]
    coef = [w / (den * wsum) for w, den in zip(ws, dens)]
    p_keys = coef[0] * es[0] + coef[1] * es[1] + coef[2] * es[2]
    p_self = coef[0] * e_selfs[0] + coef[1] * e_selfs[1] + coef[2] * e_selfs[2]

    for h in range(N_HEADS):
        vc = cols[2 * HEAD_DIM:3 * HEAD_DIM, h:h + 1]
        o_ref[0, :, h:h + 1] = (jnp.sum(v_ref[0, h] * p_keys[h:h + 1, :], axis=1, keepdims=True)
                                + p_self[h:h + 1, :] * vc)


def _rope_tables(first_pos, count):
    half = HEAD_DIM // 2
    inv = ROPE_THETA ** (-np.arange(half, dtype=np.float64) * 2.0 / HEAD_DIM)
    ang = (first_pos + np.arange(count, dtype=np.float64))[:, None] * inv[None, :]
    cos, sin = np.cos(ang), np.sin(ang)
    cos128 = np.concatenate([cos, cos, cos, cos], axis=1).astype(np.float32)
    sin128 = np.concatenate([-sin, sin, -sin, sin], axis=1).astype(np.float32)
    return jnp.asarray(cos128), jnp.asarray(sin128)


def kernel(x_prompt, x_sample, cache_win_k, cache_win_v, ln1_g, w_in, sgu_norm_g, sgu_w, sgu_b, w_out, ln2_g,
           w_router_group, b_router_group, w_router_expert, b_router_expert, w_gate, w_up, w_down, lnf_g):
    depth = w_in.shape[0]
    assert depth == 1 and x_sample.shape[1] == 1
    B, S, _ = x_prompt.shape
    bd = x_sample.shape[0]
    win = cache_win_k.shape[2]
    assert S % (max(DILATIONS) * CHUNK) == 0 and win >= max(DILATIONS) * N_KEYS and PAST_LEN % CHUNK == 0
    l = 0

    w1_b =jnp.concatenate([w_gate[l], w_up[l]], axis=-1).astype(BF16)
    w2_b = w_down[l].astype(BF16)
    pad = ROUTER_LANES - N_GROUPS - N_EXPERTS
    w_router = jnp.pad(jnp.concatenate([w_router_group[l], w_router_expert[l]], axis=1), ((0, 0), (0, pad)))
    b_router = jnp.pad(jnp.concatenate([b_router_group[l], b_router_expert[l]]), (0, pad))[None, :]
    g1 = ln1_g[l][None, :]
    g2 = ln2_g[l][None, :]
    gf = lnf_g[None, :]
    gv = sgu_norm_g[l].reshape(1, WIDTH)
    grp = jnp.arange(WIDTH) // HEAD_DIM
    ones_bd = jnp.where(grp[:, None] == grp[None, :], 1.0 / HEAD_DIM, 0.0).astype(BF16)
    wp = jnp.concatenate([sgu_w[l][0::2], sgu_w[l][1::2]], axis=-1)
    bias = jnp.repeat(sgu_b[l].T, HEAD_DIM, axis=1)
    w00 = jnp.repeat(sgu_w[l][:, 0, 0], HEAD_DIM)[None, :]
    b0 = jnp.repeat(sgu_b[l][:, 0], HEAD_DIM)[None, :]

    cos_s, sin_s = _rope_tables(PAST_LEN, 1)
    rep = (jnp.arange(bd * N_HEADS)[:, None] // N_HEADS == jnp.arange(bd)[None, :]).astype(F32)
    foldt = (jnp.arange(HEAD_DIM)[:, None] == jnp.arange(WIDTH)[None, :] % HEAD_DIM).astype(F32)
    xs = x_sample.reshape(bd, D_MODEL)
    a_s, k_s, v_s, vn_s, qkvt = _sample_proj(xs, g1, w_in[l], cos_s, sin_s, gv, ones_bd, w00, b0, rep, foldt)
    to_pos_minor = lambda c: jnp.transpose(c, (0, 2, 3, 1))

    cos_p, sin_p = _rope_tables(0, S)
    a_p, q_p, k_p, v_p, kt_p, vt_p = _prompt_proj(x_prompt, g1, w_in[l], cos_p, sin_p, gv, ones_bd, wp, bias)
    b_p = _prompt_attention(q_p, k_p, v_p)
    n = B * S
    assert n % MOE_BLOCK == 0
    wr_t = w_router.T
    wr_hi = wr_t.astype(BF16)
    wr_lo = (wr_t - wr_hi.astype(F32)).astype(BF16)
    pair_idx = jnp.arange(2 * MOE_BLOCK)
    tri = (pair_idx[:, None] <= pair_idx[None, :]).astype(BF16)
    lane_idx = jnp.arange(ROUTER_LANES)
    ltri = (lane_idx[None, :] < lane_idx[:, None]).astype(BF16)
    xp2, h2, mrow, mcol, tab_f, o3 = _mix_route_sort(
        a_p.reshape(n, WIDTH), b_p.reshape(n, WIDTH), x_prompt.reshape(n, D_MODEL),
        w_out[l], g2, wr_hi, wr_lo, b_router.reshape(ROUTER_LANES, 1), tri, ltri,
        qkvt, to_pos_minor(cache_win_k[l]), to_pos_minor(cache_win_v[l]))
    tab = tab_f[:, :N_EXPERTS, 0:3].astype(jnp.int32).reshape(-1)

    b_s = jnp.transpose(o3, (0, 2, 1)).reshape(bd, WIDTH)
    xs2, hs2, gates_s = _mix_router(a_s, b_s, xs, w_out[l], g2, w_router, b_router, tm=bd, precise=True)

    y_prompt, y_sample = _moe_sparse(tab, h2, xp2, mrow, mcol, w1_b, w2_b, gf, hs2, gates_s, xs2)
    y_prompt = y_prompt.reshape(B, S, D_MODEL)
    y_sample = y_sample.reshape(bd, 1, D_MODEL)
    buf_p = min(MAX_WINDOW, S)
    to_win = lambda t: jnp.transpose(t.reshape(1, B, N_HEADS, HEAD_DIM, buf_p), (0, 1, 4, 2, 3))
    new_k_p = to_win(kt_p)
    new_v_p = to_win(vt_p)

    shape_s = (1, bd, 1, N_HEADS, HEAD_DIM)
    return (y_prompt, y_sample, new_k_p, new_v_p,
            k_s.reshape(shape_s), v_s.reshape(shape_s), vn_s.reshape(shape_s))
```

```python
import functools

import jax
import jax.numpy as jnp
import numpy as np
from jax import lax
from jax.experimental import pallas as pl
from jax.experimental.pallas import tpu as pltpu

F32 = jnp.float32
BF16 = jnp.bfloat16

D_MODEL = 1024
HEAD_DIM = 64
N_HEADS = 8
WIDTH = N_HEADS * HEAD_DIM
PROJ_COLS = 5 * WIDTH
CHUNK = 128
DILATIONS = (1, 4, 16)
N_KEYS = 128
MAX_WINDOW = 2048
PAST_LEN = 16384
ROPE_THETA = 10000.0
N_GROUPS = 4
EXPERTS_PER_GROUP = 8
N_EXPERTS = N_GROUPS * EXPERTS_PER_GROUP
D_EXPERT = 128
EPS = 1e-6
NEG = -1e30
TILES_PER_STEP = 32
MOE_BLOCK = 512
MOE_ROW_ALIGN = 16
MOE_CHUNK = 64
MOE_EXPERTS_PER_STEP = 16
MOE_SORT_ROWS = -(-(2 * MOE_BLOCK + N_EXPERTS * (MOE_ROW_ALIGN - 1)) // 512) * 512
ROUTER_LANES = 128
V7X_VMEM_LIMIT = 56 * 1024 * 1024


def _rmsnorm(x, g):
    return x * lax.rsqrt(jnp.mean(x * x, axis=-1, keepdims=True) + EPS) * g


def _tile_lanes(t, reps):
    return jnp.concatenate([t] * reps, axis=1)


def _rope(t, cos, sin_signed):
    lane = lax.broadcasted_iota(jnp.int32, t.shape, 1)
    first_half = (lane % HEAD_DIM) < (HEAD_DIM // 2)
    n = t.shape[1]
    partner = jnp.where(first_half, pltpu.roll(t, n - HEAD_DIM // 2, 1), pltpu.roll(t, HEAD_DIM // 2, 1))
    return t * cos + partner * sin_signed


def _dot(a, b, precise=False):
    if precise:
        return jnp.dot(a.astype(F32), b.astype(F32), preferred_element_type=F32,
                       precision=lax.Precision.HIGHEST)
    return jnp.dot(a.astype(BF16), b.astype(BF16), preferred_element_type=F32)


def _group_rmsnorm(va, ones_bd, gv, precise=False):
    ms = _dot(va * va, ones_bd, precise)
    return va * lax.rsqrt(ms + EPS) * gv


def _proj_kernel(x_ref, g1_ref, w_ref, cos_ref, sin_ref, gv_ref, ones_ref, wp_ref, bias_ref,
                 a_ref, q_ref, k_ref, v_ref, kt_ref, vt_ref, wb, *, tm, first_win_tile):
    @pl.when((pl.program_id(0) == 0) & (pl.program_id(1) == 0))
    def _():
        wb[...] = w_ref[...].astype(BF16)

    h = _rmsnorm(x_ref[0], g1_ref[...]).astype(BF16)

    def proj(i):
        return jnp.dot(h, wb[:, i * WIDTH:(i + 1) * WIDTH], preferred_element_type=F32)

    cos = _tile_lanes(cos_ref[...], WIDTH // 128)
    sin = _tile_lanes(sin_ref[...], WIDTH // 128)
    q_ref[0] = _rope(proj(2), cos, sin) * (HEAD_DIM ** -0.5)
    k_ref[0] = _rope(proj(3), cos, sin)
    v_ref[0] = proj(4)

    u = proj(0)
    vn = _group_rmsnorm(proj(1), ones_ref[...], gv_ref[...]).astype(BF16)

    lane = lax.broadcasted_iota(jnp.int32, (CHUNK, 128), 1)
    left = lane < HEAD_DIM
    row = lax.broadcasted_iota(jnp.int32, (CHUNK, 2 * CHUNK), 0)
    col = lax.broadcasted_iota(jnp.int32, (CHUNK, 2 * CHUNK), 1)
    causal = (col % CHUNK) <= row
    zero = jnp.zeros((CHUNK, 128), BF16)
    wps = [jnp.where(causal, wp_ref[gp], 0.0).astype(BF16) for gp in range(N_HEADS // 2)]
    for c in range(tm // CHUNK):
        rows = slice(c * CHUNK, (c + 1) * CHUNK)
        mixes = []
        for gp in range(N_HEADS // 2):
            vv = vn[rows, gp * 128:(gp + 1) * 128]
            v2 = jnp.concatenate([jnp.where(left, vv, zero), jnp.where(left, zero, vv)], axis=0)
            mixes.append(jnp.dot(wps[gp], v2, preferred_element_type=F32))
        mix = jnp.concatenate(mixes, axis=1) + bias_ref[...]
        a_ref[0, rows, :] = (u[rows, :] * mix).astype(a_ref.dtype)

    @pl.when(pl.program_id(1) >= first_win_tile)
    def _():
        kt_ref[0] = k_ref[0].T
        vt_ref[0] = v_ref[0].T


def _prompt_proj(x, g1, w_in_b, cos, sin, gv, ones_bd, wp, bias, *, tm=512):
    B, S, _ = x.shape
    const2 = lambda b, j: (0, 0)
    out_sds = lambda dt: jax.ShapeDtypeStruct((B, S, WIDTH), dt)
    tile = pl.BlockSpec((1, tm, WIDTH), lambda b, j: (b, j, 0))
    win = min(MAX_WINDOW, S)
    first_win_tile = (S - win) // tm
    tile_t = pl.BlockSpec((1, WIDTH, tm), lambda b, j: (b, 0, jnp.maximum(j - first_win_tile, 0)))
    win_sds = jax.ShapeDtypeStruct((B, WIDTH, win), F32)
    return pl.pallas_call(
        functools.partial(_proj_kernel, tm=tm, first_win_tile=first_win_tile),
        grid=(B, S // tm),
        in_specs=[
            pl.BlockSpec((1, tm, D_MODEL), lambda b, j: (b, j, 0)),
            pl.BlockSpec((1, D_MODEL), const2),
            pl.BlockSpec((D_MODEL, PROJ_COLS), const2, pipeline_mode=pl.Buffered(1)),
            pl.BlockSpec((tm, 128), lambda b, j: (j, 0)),
            pl.BlockSpec((tm, 128), lambda b, j: (j, 0)),
            pl.BlockSpec((1, WIDTH), const2),
            pl.BlockSpec((WIDTH, WIDTH), const2),
            pl.BlockSpec((N_HEADS // 2, CHUNK, 2 * CHUNK), lambda b, j: (0, 0, 0)),
            pl.BlockSpec((CHUNK, WIDTH), const2),
        ],
        out_specs=[tile, tile, tile, tile, tile_t, tile_t],
        out_shape=[out_sds(BF16), out_sds(F32), out_sds(F32), out_sds(F32), win_sds, win_sds],
        scratch_shapes=[pltpu.VMEM((D_MODEL, PROJ_COLS), BF16)],
        compiler_params=pltpu.CompilerParams(
            dimension_semantics=("arbitrary", "arbitrary"), vmem_limit_bytes=V7X_VMEM_LIMIT),
        name="prompt_proj_sgu",
    )(x, g1, w_in_b, cos, sin, gv, ones_bd, wp, bias)


def _attn_kernel(q_ref, k_ref, v_ref, o_ref, qd, kd, vd, res_o, res_l, nat_o, nat_l, bias, *, seq):
    n_tiles = seq // CHUNK
    lane = lax.broadcasted_iota(jnp.int32, (CHUNK, 128), 1)
    left = lane < HEAD_DIM
    qi2 = lax.broadcasted_iota(jnp.int32, (2 * CHUNK, 2 * CHUNK), 0) % CHUNK
    kj2 = lax.broadcasted_iota(jnp.int32, (2 * CHUNK, 2 * CHUNK), 1)
    dist2 = CHUNK + qi2 - kj2
    band2 = (dist2 >= 0) & (dist2 <= N_KEYS)
    zero_q = jnp.zeros((CHUNK, 128), BF16)
    bias[0] = jnp.where(band2, 0.0, NEG)
    bias[1] = jnp.where(band2 & (kj2 >= CHUNK), 0.0, NEG)

    kd[0:CHUNK, :] = jnp.zeros((CHUNK, 128), BF16)
    vd[0:CHUNK, :] = jnp.zeros((CHUNK, 128), BF16)

    for p, dil in enumerate(DILATIONS):
        sub = seq // dil
        nb = sub // CHUNK
        for r in range(dil):
            src = pl.ds(r, sub, stride=dil) if dil > 1 else pl.ds(0, sub)
            qd[r * sub:(r + 1) * sub, :] = q_ref[0, src, :].astype(BF16)
            kd[CHUNK + r * sub:CHUNK + (r + 1) * sub, :] = k_ref[0, src, :].astype(BF16)
            vd[CHUNK + r * sub:CHUNK + (r + 1) * sub, :] = v_ref[0, src, :].astype(BF16)

        def tile_body(g, i, p=p, nb=nb):
            t = g * TILES_PER_STEP + i
            row = pl.multiple_of(t * CHUNK, CHUNK)
            qt = qd[pl.ds(row, CHUNK), :]
            k2 = kd[pl.ds(row, 2 * CHUNK), :]
            v2 = vd[pl.ds(row, 2 * CHUNK), :]
            if TILES_PER_STEP % nb == 0:
                variant = 1 if i % nb == 0 else 0
            elif i == 0:
                variant = jnp.where((g * TILES_PER_STEP) % nb == 0, 1, 0)
            else:
                variant = 0
            q2 = jnp.concatenate([jnp.where(left, qt, zero_q), jnp.where(left, zero_q, qt)], axis=0)
            s = lax.dot_general(q2, k2, (((1,), (1,)), ((), ())), preferred_element_type=F32)
            s = s + bias[variant]
            m = jnp.max(s, axis=1, keepdims=True)
            e = jnp.exp(s - m)
            den = jnp.sum(e, axis=1, keepdims=True)
            pv = jnp.dot(e.astype(BF16), v2, preferred_element_type=F32) / den
            lse = jnp.broadcast_to(m + jnp.log(den), (2 * CHUNK, 128))
            res_o[p, pl.ds(row, CHUNK), :] = jnp.where(left, pv[0:CHUNK], pv[CHUNK:2 * CHUNK])
            res_l[p, pl.ds(row, CHUNK), :] = jnp.where(left, lse[0:CHUNK], lse[CHUNK:2 * CHUNK])

        def group_body(g, carry, tile_body=tile_body):
            for i in range(TILES_PER_STEP):
                tile_body(g, i)
            return carry

        lax.fori_loop(0, n_tiles // TILES_PER_STEP, group_body, 0)

    for p, dil in enumerate(DILATIONS):
        if dil == 1:
            continue
        sub = seq // dil
        for r in range(dil):
            nat_o[p - 1, pl.ds(r, sub, stride=dil), :] = res_o[p, r * sub:(r + 1) * sub, :]
            nat_l[p - 1, pl.ds(r, sub, stride=dil), :] = res_l[p, r * sub:(r + 1) * sub, :]

    rows_per_step = 256

    def merge_body(c, carry):
        rows = pl.ds(pl.multiple_of(c * rows_per_step, rows_per_step), rows_per_step)
        l0, l1, l2 = res_l[0, rows, :], nat_l[0, rows, :], nat_l[1, rows, :]
        top = jnp.maximum(jnp.maximum(l0, l1), l2)
        w0, w1, w2 = jnp.exp(l0 - top), jnp.exp(l1 - top), jnp.exp(l2 - top)
        num = w0 * res_o[0, rows, :] + w1 * nat_o[0, rows, :] + w2 * nat_o[1, rows, :]
        o_ref[0, rows, :] = (num / (w0 + w1 + w2)).astype(o_ref.dtype)
        return carry

    lax.fori_loop(0, seq // rows_per_step, merge_body, 0)


def _prompt_attention(q, k, v):
    B, S, _ = q.shape
    blk = pl.BlockSpec((1, S, 128), lambda b, hp: (b, 0, hp))
    return pl.pallas_call(
        functools.partial(_attn_kernel, seq=S),
        grid=(B, WIDTH // 128),
        in_specs=[blk, blk, blk],
        out_specs=blk,
        out_shape=jax.ShapeDtypeStruct((B, S, WIDTH), BF16),
        scratch_shapes=[
            pltpu.VMEM((S, 128), BF16),
            pltpu.VMEM((S + CHUNK, 128), BF16),
            pltpu.VMEM((S + CHUNK, 128), BF16),
            pltpu.VMEM((len(DILATIONS), S, 128), F32),
            pltpu.VMEM((len(DILATIONS), S, 128), F32),
            pltpu.VMEM((len(DILATIONS) - 1, S, 128), F32),
            pltpu.VMEM((len(DILATIONS) - 1, S, 128), F32),
            pltpu.VMEM((2, 2 * CHUNK, 2 * CHUNK), F32),
        ],
        compiler_params=pltpu.CompilerParams(
            dimension_semantics=("arbitrary", "arbitrary"), vmem_limit_bytes=V7X_VMEM_LIMIT),
        name="prompt_dilated_attention",
    )(q, k, v)


def _route(logits):
    lane = lax.broadcasted_iota(jnp.int32, logits.shape, 1)
    big = jnp.int32(ROUTER_LANES)
    lg = jnp.where(lane < N_GROUPS, logits, NEG)
    gmax = jnp.max(lg, axis=1, keepdims=True)
    gp = 1.0 / jnp.sum(jnp.exp(lg - gmax), axis=1, keepdims=True)
    gi = jnp.min(jnp.where(lg == gmax, lane, big), axis=1, keepdims=True)
    lo = N_GROUPS + EXPERTS_PER_GROUP * gi
    le = jnp.where((lane >= lo) & (lane < lo + EXPERTS_PER_GROUP), logits, NEG)
    m1 = jnp.max(le, axis=1, keepdims=True)
    i1 = jnp.min(jnp.where(le == m1, lane, big), axis=1, keepdims=True)
    le2 = jnp.where(lane == i1, NEG, le)
    m2 = jnp.max(le2, axis=1, keepdims=True)
    i2 = jnp.min(jnp.where(le2 == m2, lane, big), axis=1, keepdims=True)
    e2 = jnp.exp(m2 - m1)
    w1 = 1.0 / (1.0 + e2)
    w2 = e2 / (1.0 + e2)
    return jnp.where(lane == i1, gp * w1, jnp.where(lane == i2, gp * w2, 0.0))


def _mix_router_kernel(a_ref, b_ref, x_ref, wo_ref, g2_ref, wr_ref, br_ref, xp_ref, h2_ref, gates_ref, *, precise):
    mixed = (_dot(a_ref[...], wo_ref[0:WIDTH, :], precise)
             + _dot(b_ref[...], wo_ref[WIDTH:2 * WIDTH, :], precise))
    xp = x_ref[...] + mixed
    xp_ref[...] = xp
    h2 = _rmsnorm(xp, g2_ref[...])
    h2_ref[...] = h2.astype(h2_ref.dtype)
    logits = jnp.dot(h2, wr_ref[...], preferred_element_type=F32,
                     precision=lax.Precision.HIGHEST) + br_ref[...]
    gates_ref[...] = _route(logits)


def _mix_router(a, b, x, w_out, g2, w_router, b_router, *, tm, precise=False):
    n = x.shape[0]
    const = lambda i: (0, 0)
    row_blk = lambda w: pl.BlockSpec((tm, w), lambda i: (i, 0))
    return pl.pallas_call(
        functools.partial(_mix_router_kernel, precise=precise),
        grid=(n // tm,),
        in_specs=[row_blk(WIDTH), row_blk(WIDTH), row_blk(D_MODEL),
                  pl.BlockSpec((2 * WIDTH, D_MODEL), const),
                  pl.BlockSpec((1, D_MODEL), const),
                  pl.BlockSpec((D_MODEL, ROUTER_LANES), const),
                  pl.BlockSpec((1, ROUTER_LANES), const)],
        out_specs=[row_blk(D_MODEL), row_blk(D_MODEL), row_blk(ROUTER_LANES)],
        out_shape=[jax.ShapeDtypeStruct((n, D_MODEL), F32),
                   jax.ShapeDtypeStruct((n, D_MODEL), BF16),
                   jax.ShapeDtypeStruct((n, ROUTER_LANES), F32)],
        compiler_params=pltpu.CompilerParams(
            dimension_semantics=("arbitrary",), vmem_limit_bytes=V7X_VMEM_LIMIT),
        name="outproj_router",
    )(a, b, x, w_out, g2, w_router, b_router)


def _nt_dot(w, t):
    return lax.dot_general(w, t, (((1,), (1,)), ((), ())), preferred_element_type=F32)


def _route_t(logits_t):
    row = lax.broadcasted_iota(jnp.int32, logits_t.shape, 0)
    big = jnp.int32(ROUTER_LANES)
    lg = jnp.where(row < N_GROUPS, logits_t, NEG)
    gmax = jnp.max(lg, axis=0, keepdims=True)
    gp = 1.0 / jnp.sum(jnp.exp(lg - gmax), axis=0, keepdims=True)
    gi = jnp.min(jnp.where(lg == gmax, row, big), axis=0, keepdims=True)
    lo = N_GROUPS + EXPERTS_PER_GROUP * gi
    le = jnp.where((row >= lo) & (row < lo + EXPERTS_PER_GROUP), logits_t, NEG)
    m1 = jnp.max(le, axis=0, keepdims=True)
    i1 = jnp.min(jnp.where(le == m1, row, big), axis=0, keepdims=True)
    le2 = jnp.where(row == i1, NEG, le)
    m2 = jnp.max(le2, axis=0, keepdims=True)
    i2 = jnp.min(jnp.where(le2 == m2, row, big), axis=0, keepdims=True)
    e2 = jnp.exp(m2 - m1)
    return i1 - N_GROUPS, i2 - N_GROUPS, gp / (1.0 + e2), gp * e2 / (1.0 + e2)


def _mix_route_sort_kernel(a_ref, b_ref, x_ref, wo_ref, g2_ref, wrh_ref, wrl_ref, brc_ref, tri_ref, ltri_ref,
                           qkvt_ref, ck_ref, cv_ref,
                           xp_ref, h2_ref, mrow_ref, mcol_ref, tab_ref, so_ref, wob, *, n_seq, win):
    t = MOE_BLOCK

    @pl.when(pl.program_id(0) == 0)
    def _():
        wob[...] = wo_ref[...].astype(BF16)

    xp = x_ref[...] + _dot(a_ref[...], wob[0:WIDTH, :]) + _dot(b_ref[...], wob[WIDTH:2 * WIDTH, :])
    xp_ref[...] = xp
    h2 = _rmsnorm(xp, g2_ref[...])
    hi = h2.astype(BF16)
    h2_ref[...] = hi
    lo = (h2 - hi.astype(F32)).astype(BF16)
    logits_t = (_nt_dot(wrh_ref[...], hi) + _nt_dot(wrh_ref[...], lo) + _nt_dot(wrl_ref[...], hi)
                + brc_ref[...])
    ex1, ex2, gate1, gate2 = _route_t(logits_t)

    pair_e = jnp.concatenate([ex1, ex2], axis=1)
    row = lax.broadcasted_iota(jnp.int32, (ROUTER_LANES, 2 * t), 0)
    onehot = jnp.where(row == pair_e, 1.0, 0.0)
    cum = _dot(onehot, tri_ref[...])
    rank = jnp.sum(onehot * cum, axis=0, keepdims=True) - 1.0
    counts = cum[:, 2 * t - 1:2 * t]
    units = jnp.floor((counts + (MOE_ROW_ALIGN - 1)) * (1.0 / MOE_ROW_ALIGN))
    off_units = _dot(ltri_ref[...], jnp.broadcast_to(units, (ROUTER_LANES, 128)))
    off = off_units[:, 0:1] * MOE_ROW_ALIGN
    dst = jnp.sum(onehot * off, axis=0, keepdims=True) + rank

    r8 = lax.broadcasted_iota(jnp.int32, (8, t), 0)
    mrow_ref[0] = jnp.where(r8 == 0, dst[:, 0:t], jnp.where(r8 == 1, dst[:, t:2 * t],
                            jnp.where(r8 == 2, gate1, jnp.where(r8 == 3, gate2, 0.0))))
    r128 = lax.broadcasted_iota(jnp.int32, (ROUTER_LANES, t), 0)
    meta = jnp.where(r128 == 0, dst[:, 0:t], jnp.where(r128 == 1, dst[:, t:2 * t],
                     jnp.where(r128 == 2, gate1, jnp.where(r128 == 3, gate2, 0.0))))
    mcol_ref[...] = meta.T
    lane = lax.broadcasted_iota(jnp.int32, (ROUTER_LANES, 128), 1)
    chunks = jnp.floor((units * MOE_ROW_ALIGN + (MOE_CHUNK - 1)) * (1.0 / MOE_CHUNK))
    tab_ref[0] = jnp.where(lane == 0, off, jnp.where(lane == 1, chunks,
                           jnp.where(lane == 2, off + units * MOE_ROW_ALIGN, 0.0)))

    _sample_attend(qkvt_ref, ck_ref, cv_ref, so_ref, jnp.minimum(pl.program_id(0), n_seq - 1), win)


def _mix_route_sort(a, b, x, w_out, g2, wr_hi, wr_lo, br_col, tri, ltri, qkvt, cache_k_t, cache_v_t):
    n = x.shape[0]
    t = MOE_BLOCK
    nblk = n // t
    n_seq, _, _, win = cache_k_t.shape
    assert n_seq <= nblk, "one sample sequence rides on each prompt block"
    const = lambda i: (0, 0)
    row_blk = lambda w: pl.BlockSpec((t, w), lambda i: (i, 0))
    seq_blk = lambda shape: pl.BlockSpec((1,) + shape, lambda i: (jnp.minimum(i, n_seq - 1),) + (0,) * len(shape))
    return pl.pallas_call(
        functools.partial(_mix_route_sort_kernel, n_seq=n_seq, win=win),
        grid=(nblk,),
        in_specs=[row_blk(WIDTH), row_blk(WIDTH), row_blk(D_MODEL),
                  pl.BlockSpec((2 * WIDTH, D_MODEL), const, pipeline_mode=pl.Buffered(1)),
                  pl.BlockSpec((1, D_MODEL), const),
                  pl.BlockSpec((ROUTER_LANES, D_MODEL), const),
                  pl.BlockSpec((ROUTER_LANES, D_MODEL), const),
                  pl.BlockSpec((ROUTER_LANES, 1), const),
                  pl.BlockSpec((2 * t, 2 * t), const),
                  pl.BlockSpec((ROUTER_LANES, ROUTER_LANES), const),
                  pl.BlockSpec(qkvt.shape, const),
                  seq_blk((N_HEADS, HEAD_DIM, win)), seq_blk((N_HEADS, HEAD_DIM, win))],
        out_specs=[row_blk(D_MODEL), row_blk(D_MODEL),
                   pl.BlockSpec((1, 8, t), lambda i: (i, 0, 0)),
                   row_blk(ROUTER_LANES),
                   pl.BlockSpec((1, ROUTER_LANES, 128), lambda i: (i, 0, 0)),
                   seq_blk((HEAD_DIM, N_HEADS))],
        out_shape=[jax.ShapeDtypeStruct((n, D_MODEL), F32),
                   jax.ShapeDtypeStruct((n, D_MODEL), BF16),
                   jax.ShapeDtypeStruct((nblk, 8, t), F32),
                   jax.ShapeDtypeStruct((n, ROUTER_LANES), F32),
                   jax.ShapeDtypeStruct((nblk, ROUTER_LANES, 128), F32),
                   jax.ShapeDtypeStruct((n_seq, HEAD_DIM, N_HEADS), F32)],
        scratch_shapes=[pltpu.VMEM((2 * WIDTH, D_MODEL), BF16)],
        compiler_params=pltpu.CompilerParams(
            dimension_semantics=("arbitrary",), vmem_limit_bytes=V7X_VMEM_LIMIT),
        name="outproj_route_sort",
    )(a, b, x, w_out, g2, wr_hi, wr_lo, br_col, tri, ltri, qkvt, cache_k_t, cache_v_t)


def _moe_sparse_kernel(tab_ref, h_ref, xp_ref, mrow_ref, mcol_ref, w1_ref, w2_ref, gf_ref, sh_ref, sg_ref, sx_ref,
                       y_ref, sy_ref, xs, hs, os, shid):
    t = MOE_BLOCK
    blk = pl.program_id(0)
    tab = lambda e, c: tab_ref[(blk * N_EXPERTS + e) * 3 + c]

    mrow = mrow_ref[0]
    dst1 = mrow[0:1, :].astype(jnp.int32)
    dst2 = mrow[1:2, :].astype(jnp.int32)
    piece = 512
    for r0 in range(0, MOE_SORT_ROWS, piece):
        d_idx = lax.broadcasted_iota(jnp.int32, (piece, t), 0) + r0
        sel = jnp.where((d_idx == dst1) | (d_idx == dst2), 1.0, 0.0)
        xs[r0:r0 + piece, :] = _dot(sel, h_ref[...]).astype(BF16)
    tail = slice(MOE_SORT_ROWS, MOE_SORT_ROWS + MOE_CHUNK)
    xs[tail, :] = jnp.zeros((MOE_CHUNK, D_MODEL), BF16)
    os[...] = jnp.zeros(os.shape, BF16)

    def gate_up(e, r0):
        ab = jnp.dot(xs[pl.ds(r0, MOE_CHUNK), :], w1_ref[e], preferred_element_type=F32)
        a = ab[:, :D_EXPERT]
        return (a * (1.0 / (1.0 + jnp.exp(-a))) * ab[:, D_EXPERT:]).astype(BF16)

    def down(e, hid):
        return jnp.dot(hid, w2_ref[e], preferred_element_type=F32).astype(BF16)

    def first_gate_up(g, carry):
        for i in range(MOE_EXPERTS_PER_STEP):
            e = g * MOE_EXPERTS_PER_STEP + i
            r0 = pl.multiple_of(tab(e, 0), MOE_ROW_ALIGN)
            hs[pl.ds(r0, MOE_CHUNK), :] = gate_up(e, r0)
        return carry

    def first_down(g, carry):
        for i in range(MOE_EXPERTS_PER_STEP):
            e = g * MOE_EXPERTS_PER_STEP + i
            r0 = pl.multiple_of(tab(e, 0), MOE_ROW_ALIGN)
            os[pl.ds(r0, MOE_CHUNK), :] = down(e, hs[pl.ds(r0, MOE_CHUNK), :])
        return carry

    lax.fori_loop(0, N_EXPERTS // MOE_EXPERTS_PER_STEP, first_gate_up, 0)
    lax.fori_loop(0, N_EXPERTS // MOE_EXPERTS_PER_STEP, first_down, 0)

    def more_chunks(e, carry):
        off, n_chunks, end = tab(e, 0), tab(e, 1), tab(e, 2)

        def chunk(c, carry):
            r0 = pl.multiple_of(off + c * MOE_CHUNK, MOE_ROW_ALIGN)
            rows = r0 + lax.broadcasted_iota(jnp.int32, (MOE_CHUNK, D_MODEL), 0)
            os[pl.ds(r0, MOE_CHUNK), :] = jnp.where(rows < end, down(e, gate_up(e, r0)), os[pl.ds(r0, MOE_CHUNK), :])
            return carry

        return lax.fori_loop(1, n_chunks, chunk, carry)

    lax.fori_loop(0, N_EXPERTS, more_chunks, 0)

    mcol = mcol_ref[...]
    d1c = mcol[:, 0:1].astype(jnp.int32)
    d2c = mcol[:, 1:2].astype(jnp.int32)
    l_idx = lax.broadcasted_iota(jnp.int32, (t, MOE_SORT_ROWS), 1)
    comb = jnp.where(l_idx == d1c, mcol[:, 2:3], 0.0) + jnp.where(l_idx == d2c, mcol[:, 3:4], 0.0)
    y = xp_ref[...] + _dot(comb, os[0:MOE_SORT_ROWS, :])
    y_ref[...] = _rmsnorm(y, gf_ref[...])

    @pl.when(blk == 0)
    def _():
        sh = sh_ref[...]
        gates = sg_ref[...]
        for e in range(N_EXPERTS):
            ab = jnp.dot(sh, w1_ref[e], preferred_element_type=F32)
            a = ab[:, :D_EXPERT]
            gate = gates[:, N_GROUPS + e:N_GROUPS + e + 1]
            shid[:, e * D_EXPERT:(e + 1) * D_EXPERT] = (
                a * (1.0 / (1.0 + jnp.exp(-a))) * ab[:, D_EXPERT:] * gate).astype(BF16)
        w2_all = w2_ref[...].reshape(N_EXPERTS * D_EXPERT, D_MODEL)
        ys = sx_ref[...] + jnp.dot(shid[...], w2_all, preferred_element_type=F32)
        sy_ref[...] = _rmsnorm(ys, gf_ref[...])


def _moe_sparse(tab, h2, xp, mrow, mcol, w1_b, w2_b, gf, s_h, s_gates, s_x):
    n = h2.shape[0]
    n_s = s_h.shape[0]
    t = MOE_BLOCK
    row_blk = lambda w: pl.BlockSpec((t, w), lambda i, tab: (i, 0))
    whole = lambda shape: pl.BlockSpec(shape, lambda i, tab: (0,) * len(shape))
    resident = lambda shape: pl.BlockSpec(shape, lambda i, tab: (0,) * len(shape), pipeline_mode=pl.Buffered(1))
    return pl.pallas_call(
        _moe_sparse_kernel,
        grid_spec=pltpu.PrefetchScalarGridSpec(
            num_scalar_prefetch=1,
            grid=(n // t,),
            in_specs=[row_blk(D_MODEL), row_blk(D_MODEL),
                      pl.BlockSpec((1, 8, t), lambda i, tab: (i, 0, 0)),
                      row_blk(ROUTER_LANES),
                      resident(w1_b.shape), resident(w2_b.shape),
                      whole((1, D_MODEL)),
                      whole(s_h.shape), whole(s_gates.shape), whole(s_x.shape)],
            out_specs=[row_blk(D_MODEL), whole((n_s, D_MODEL))],
            scratch_shapes=[pltpu.VMEM((MOE_SORT_ROWS + MOE_CHUNK, D_MODEL), BF16),
                            pltpu.VMEM((MOE_SORT_ROWS + MOE_CHUNK, D_EXPERT), BF16),
                            pltpu.VMEM((MOE_SORT_ROWS + MOE_CHUNK, D_MODEL), BF16),
                            pltpu.VMEM((n_s, N_EXPERTS * D_EXPERT), BF16)]),
        out_shape=[jax.ShapeDtypeStruct((n, D_MODEL), F32), jax.ShapeDtypeStruct((n_s, D_MODEL), F32)],
        compiler_params=pltpu.CompilerParams(
            dimension_semantics=("arbitrary",), vmem_limit_bytes=V7X_VMEM_LIMIT),
        name="moe_sparse",
    )(tab, h2, xp, mrow, mcol, w1_b, w2_b, gf, s_h, s_gates, s_x)


def _sample_proj_kernel(x_ref, g1_ref, w_ref, cos_ref, sin_ref, gv_ref, ones_ref, w00_ref, b0_ref,
                        rep_ref, foldt_ref, a_ref, k_ref, v_ref, vn_ref, qkvt_ref):
    h = _rmsnorm(x_ref[...], g1_ref[...])

    def proj(i):
        return _dot(h, w_ref[:, i * WIDTH:(i + 1) * WIDTH], precise=True)

    cos = _tile_lanes(cos_ref[...], WIDTH // 128)
    sin = _tile_lanes(sin_ref[...], WIDTH // 128)
    q = _rope(proj(2), cos, sin) * (HEAD_DIM ** -0.5)
    k = _rope(proj(3), cos, sin)
    v = proj(4)
    vn = _group_rmsnorm(proj(1), ones_ref[...], gv_ref[...], precise=True)
    a_ref[...] = proj(0) * (w00_ref[...] * vn + b0_ref[...])
    k_ref[...] = k
    v_ref[...] = v
    vn_ref[...] = vn

    n_rep = rep_ref.shape[0]
    r_idx = lax.broadcasted_iota(jnp.int32, (n_rep, WIDTH), 0)
    l_idx = lax.broadcasted_iota(jnp.int32, (n_rep, WIDTH), 1)
    own = (l_idx // HEAD_DIM) == (r_idx % N_HEADS)
    for t, src in enumerate((q, k, v)):
        rep = _dot(rep_ref[...], src, precise=True)
        qkvt_ref[t * HEAD_DIM:(t + 1) * HEAD_DIM, :] = lax.dot_general(
            foldt_ref[...], jnp.where(own, rep, 0.0), (((1,), (1,)), ((), ())),
            preferred_element_type=F32, precision=lax.Precision.HIGHEST)


def _sample_proj(x, g1, w_in, cos, sin, gv, ones_bd, w00, b0, rep, foldt):
    bd = x.shape[0]
    sds = lambda r, c: jax.ShapeDtypeStruct((r, c), F32)
    return pl.pallas_call(
        _sample_proj_kernel,
        out_shape=[sds(bd, WIDTH), sds(bd, WIDTH), sds(bd, WIDTH), sds(bd, WIDTH),
                   sds(3 * HEAD_DIM, bd * N_HEADS)],
        compiler_params=pltpu.CompilerParams(vmem_limit_bytes=V7X_VMEM_LIMIT),
        name="sample_proj",
    )(x, g1, w_in, cos, sin, gv, ones_bd, w00, b0, rep, foldt)


def _sample_attend(qkvt_ref, k_ref, v_ref, o_ref, b, win):
    n_col = qkvt_ref.shape[1]
    c_idx = lax.broadcasted_iota(jnp.int32, (n_col, 128), 0)
    l_idx = lax.broadcasted_iota(jnp.int32, (n_col, 128), 1)
    pick = jnp.where((c_idx == b * N_HEADS + l_idx) & (l_idx < N_HEADS), 1.0, 0.0)
    cols = _dot(qkvt_ref[...], pick, precise=True)
    dist = win - lax.broadcasted_iota(jnp.int32, (1, win), 1)
    members = [(dist <= N_KEYS * dil) & (dist % dil == 0) for dil in DILATIONS]

    head = lax.broadcasted_iota(jnp.int32, (N_HEADS, win), 0)
    s = jnp.zeros((N_HEADS, win), F32)
    s_self = jnp.zeros((N_HEADS, 1), F32)
    for h in range(N_HEADS):
        qc = cols[0:HEAD_DIM, h:h + 1]
        kc = cols[HEAD_DIM:2 * HEAD_DIM, h:h + 1]
        s_h = jnp.sum(k_ref[0, h] * qc, axis=0, keepdims=True)
        s = jnp.where(head == h, s_h, s)
        s_self = jnp.where(head[:, 0:1] == h, jnp.sum(qc * kc, axis=0, keepdims=True), s_self)

    es, e_selfs, dens, lses = [], [], [], []
    for mem in members:
        sm = jnp.where(mem, s, NEG)
        m = jnp.maximum(jnp.max(sm, axis=1, keepdims=True), s_self)
        e = jnp.exp(sm - m)
        e_self = jnp.exp(s_self - m)
        den = jnp.sum(e, axis=1, keepdims=True) + e_self
        es.append(e)
        e_selfs.append(e_self)
        dens.append(den)
        lses.append(m + jnp.log(den))
    top = jnp.maximum(jnp.maximum(lses[0], lses[1]), lses[2])
    ws = [jnp.exp(l - top) for l in lses]
    wsum = ws[0] + ws[1] + ws[2]
    coef = [w / (den * wsum) for w, den in zip(ws, dens)]
    p_keys = coef[0] * es[0] + coef[1] * es[1] + coef[2] * es[2]
    p_self = coef[0] * e_selfs[0] + coef[1] * e_selfs[1] + coef[2] * e_selfs[2]

    for h in range(N_HEADS):
        vc = cols[2 * HEAD_DIM:3 * HEAD_DIM, h:h + 1]
        o_ref[0, :, h:h + 1] = (jnp.sum(v_ref[0, h] * p_keys[h:h + 1, :], axis=1, keepdims=True)
                                + p_self[h:h + 1, :] * vc)


def _rope_tables(first_pos, count):
    half = HEAD_DIM // 2
    inv = ROPE_THETA ** (-np.arange(half, dtype=np.float64) * 2.0 / HEAD_DIM)
    ang = (first_pos + np.arange(count, dtype=np.float64))[:, None] * inv[None, :]
    cos, sin = np.cos(ang), np.sin(ang)
    cos128 = np.concatenate([cos, cos, cos, cos], axis=1).astype(np.float32)
    sin128 = np.concatenate([-sin, sin, -sin, sin], axis=1).astype(np.float32)
    return jnp.asarray(cos128), jnp.asarray(sin128)


def kernel(x_prompt, x_sample, cache_win_k, cache_win_v, ln1_g, w_in, sgu_norm_g, sgu_w, sgu_b, w_out, ln2_g,
           w_router_group, b_router_group, w_router_expert, b_router_expert, w_gate, w_up, w_down, lnf_g):
    depth = w_in.shape[0]
    assert depth == 1 and x_sample.shape[1] == 1
    B, S, _ = x_prompt.shape
    bd = x_sample.shape[0]
    win = cache_win_k.shape[2]
    assert S % (max(DILATIONS) * CHUNK) == 0 and win >= max(DILATIONS) * N_KEYS and PAST_LEN % CHUNK == 0
    l = 0

    w1_b =jnp.concatenate([w_gate[l], w_up[l]], axis=-1).astype(BF16)
    w2_b = w_down[l].astype(BF16)
    pad = ROUTER_LANES - N_GROUPS - N_EXPERTS
    w_router = jnp.pad(jnp.concatenate([w_router_group[l], w_router_expert[l]], axis=1), ((0, 0), (0, pad)))
    b_router = jnp.pad(jnp.concatenate([b_router_group[l], b_router_expert[l]]), (0, pad))[None, :]
    g1 = ln1_g[l][None, :]
    g2 = ln2_g[l][None, :]
    gf = lnf_g[None, :]
    gv = sgu_norm_g[l].reshape(1, WIDTH)
    grp = jnp.arange(WIDTH) // HEAD_DIM
    ones_bd = jnp.where(grp[:, None] == grp[None, :], 1.0 / HEAD_DIM, 0.0).astype(BF16)
    wp = jnp.concatenate([sgu_w[l][0::2], sgu_w[l][1::2]], axis=-1)
    bias = jnp.repeat(sgu_b[l].T, HEAD_DIM, axis=1)
    w00 = jnp.repeat(sgu_w[l][:, 0, 0], HEAD_DIM)[None, :]
    b0 = jnp.repeat(sgu_b[l][:, 0], HEAD_DIM)[None, :]

    cos_s, sin_s = _rope_tables(PAST_LEN, 1)
    rep = (jnp.arange(bd * N_HEADS)[:, None] // N_HEADS == jnp.arange(bd)[None, :]).astype(F32)
    foldt = (jnp.arange(HEAD_DIM)[:, None] == jnp.arange(WIDTH)[None, :] % HEAD_DIM).astype(F32)
    xs = x_sample.reshape(bd, D_MODEL)
    a_s, k_s, v_s, vn_s, qkvt = _sample_proj(xs, g1, w_in[l], cos_s, sin_s, gv, ones_bd, w00, b0, rep, foldt)
    to_pos_minor = lambda c: jnp.transpose(c, (0, 2, 3, 1))

    cos_p, sin_p = _rope_tables(0, S)
    a_p, q_p, k_p, v_p, kt_p, vt_p = _prompt_proj(x_prompt, g1, w_in[l], cos_p, sin_p, gv, ones_bd, wp, bias)
    b_p = _prompt_attention(q_p, k_p, v_p)
    n = B * S
    assert n % MOE_BLOCK == 0
    wr_t = w_router.T
    wr_hi = wr_t.astype(BF16)
    wr_lo = (wr_t - wr_hi.astype(F32)).astype(BF16)
    pair_idx = jnp.arange(2 * MOE_BLOCK)
    tri = (pair_idx[:, None] <= pair_idx[None, :]).astype(BF16)
    lane_idx = jnp.arange(ROUTER_LANES)
    ltri = (lane_idx[None, :] < lane_idx[:, None]).astype(BF16)
    xp2, h2, mrow, mcol, tab_f, o3 = _mix_route_sort(
        a_p.reshape(n, WIDTH), b_p.reshape(n, WIDTH), x_prompt.reshape(n, D_MODEL),
        w_out[l], g2, wr_hi, wr_lo, b_router.reshape(ROUTER_LANES, 1), tri, ltri,
        qkvt, to_pos_minor(cache_win_k[l]), to_pos_minor(cache_win_v[l]))
    tab = tab_f[:, :N_EXPERTS, 0:3].astype(jnp.int32).reshape(-1)

    b_s = jnp.transpose(o3, (0, 2, 1)).reshape(bd, WIDTH)
    xs2, hs2, gates_s = _mix_router(a_s, b_s, xs, w_out[l], g2, w_router, b_router, tm=bd, precise=True)

    y_prompt, y_sample = _moe_sparse(tab, h2, xp2, mrow, mcol, w1_b, w2_b, gf, hs2, gates_s, xs2)
    y_prompt = y_prompt.reshape(B, S, D_MODEL)
    y_sample = y_sample.reshape(bd, 1, D_MODEL)
    buf_p = min(MAX_WINDOW, S)
    to_win = lambda t: jnp.transpose(t.reshape(1, B, N_HEADS, HEAD_DIM, buf_p), (0, 1, 4, 2, 3))
    new_k_p = to_win(kt_p)
    new_v_p = to_win(vt_p)

    shape_s = (1, bd, 1, N_HEADS, HEAD_DIM)
    return (y_prompt, y_sample, new_k_p, new_v_p,
            k_s.reshape(shape_s), v_s.reshape(shape_s), vn_s.reshape(shape_s))
```

```python
import functools

import jax
import jax.numpy as jnp
import numpy as np
from jax import lax
from jax.experimental import pallas as pl
from jax.experimental.pallas import tpu as pltpu

F32 = jnp.float32
BF16 = jnp.bfloat16

D_MODEL = 1024
HEAD_DIM = 64
N_HEADS = 8
WIDTH = N_HEADS * HEAD_DIM
PROJ_COLS = 5 * WIDTH
CHUNK = 128
DILATIONS = (1, 4, 16)
N_KEYS = 128
MAX_WINDOW = 2048
PAST_LEN = 16384
ROPE_THETA = 10000.0
N_GROUPS = 4
EXPERTS_PER_GROUP = 8
N_EXPERTS = N_GROUPS * EXPERTS_PER_GROUP
D_EXPERT = 128
EPS = 1e-6
NEG = -1e30
TILES_PER_STEP = 32
SUBLANE_STRIDE = 4
assert DILATIONS == (1, SUBLANE_STRIDE, SUBLANE_STRIDE ** 2)
MOE_BLOCK = 512
MOE_ROW_ALIGN = 16
MOE_CHUNK = 64
MOE_EXPERTS_PER_STEP = 16
MOE_SORT_ROWS = -(-(2 * MOE_BLOCK + N_EXPERTS * (MOE_ROW_ALIGN - 1)) // 512) * 512
ROUTER_ROWS = 40
ROUTER_LANES = 128
V7X_VMEM_LIMIT = 56 * 1024 * 1024


def _rmsnorm(x, g):
    return x * lax.rsqrt(jnp.mean(x * x, axis=-1, keepdims=True) + EPS) * g


def _tile_lanes(t, reps):
    return jnp.concatenate([t] * reps, axis=1)


def _rope(t, cos, sin_signed):
    lane = lax.broadcasted_iota(jnp.int32, t.shape, 1)
    first_half = (lane % HEAD_DIM) < (HEAD_DIM // 2)
    n = t.shape[1]
    partner = jnp.where(first_half, pltpu.roll(t, n - HEAD_DIM // 2, 1), pltpu.roll(t, HEAD_DIM // 2, 1))
    return t * cos + partner * sin_signed


def _dot(a, b, precise=False):
    if precise:
        return jnp.dot(a.astype(F32), b.astype(F32), preferred_element_type=F32,
                       precision=lax.Precision.HIGHEST)
    return jnp.dot(a.astype(BF16), b.astype(BF16), preferred_element_type=F32)


def _group_rmsnorm(va, ones_bd, gv, precise=False):
    ms = _dot(va * va, ones_bd, precise)
    return va * lax.rsqrt(ms + EPS) * gv


def _proj_kernel(x_ref, g1_ref, w_ref, cos_ref, sin_ref, gv_ref, ones_ref, wp_ref, bias_ref,
                 a_ref, q_ref, k_ref, v_ref, kt_ref, vt_ref, wb, *, tm, first_win_tile):
    @pl.when((pl.program_id(0) == 0) & (pl.program_id(1) == 0))
    def _():
        wb[...] = w_ref[...].astype(BF16)

    h = _rmsnorm(x_ref[0], g1_ref[...]).astype(BF16)

    def proj(i):
        return jnp.dot(h, wb[:, i * WIDTH:(i + 1) * WIDTH], preferred_element_type=F32)

    cos = _tile_lanes(cos_ref[...], WIDTH // 128)
    sin = _tile_lanes(sin_ref[...], WIDTH // 128)
    q_ref[0] = _rope(proj(2), cos, sin) * (HEAD_DIM ** -0.5)
    k_ref[0] = _rope(proj(3), cos, sin)
    v_ref[0] = proj(4)

    u = proj(0)
    vn = _group_rmsnorm(proj(1), ones_ref[...], gv_ref[...]).astype(BF16)

    lane = lax.broadcasted_iota(jnp.int32, (CHUNK, 128), 1)
    left = lane < HEAD_DIM
    row = lax.broadcasted_iota(jnp.int32, (CHUNK, 2 * CHUNK), 0)
    col = lax.broadcasted_iota(jnp.int32, (CHUNK, 2 * CHUNK), 1)
    causal = (col % CHUNK) <= row
    zero = jnp.zeros((CHUNK, 128), BF16)
    wps = [jnp.where(causal, wp_ref[gp], 0.0).astype(BF16) for gp in range(N_HEADS // 2)]
    for c in range(tm // CHUNK):
        rows = slice(c * CHUNK, (c + 1) * CHUNK)
        mixes = []
        for gp in range(N_HEADS // 2):
            vv = vn[rows, gp * 128:(gp + 1) * 128]
            v2 = jnp.concatenate([jnp.where(left, vv, zero), jnp.where(left, zero, vv)], axis=0)
            mixes.append(jnp.dot(wps[gp], v2, preferred_element_type=F32))
        mix = jnp.concatenate(mixes, axis=1) + bias_ref[...]
        a_ref[0, rows, :] = (u[rows, :] * mix).astype(a_ref.dtype)

    @pl.when(pl.program_id(1) >= first_win_tile)
    def _():
        kt_ref[0] = k_ref[0].T
        vt_ref[0] = v_ref[0].T


def _prompt_proj(x, g1, w_in_b, cos, sin, gv, ones_bd, wp, bias, *, tm=512):
    B, S, _ = x.shape
    const2 = lambda b, j: (0, 0)
    out_sds = lambda dt: jax.ShapeDtypeStruct((B, S, WIDTH), dt)
    tile = pl.BlockSpec((1, tm, WIDTH), lambda b, j: (b, j, 0))
    win = min(MAX_WINDOW, S)
    first_win_tile = (S - win) // tm
    tile_t = pl.BlockSpec((1, WIDTH, tm), lambda b, j: (b, 0, jnp.maximum(j - first_win_tile, 0)))
    win_sds = jax.ShapeDtypeStruct((B, WIDTH, win), F32)
    return pl.pallas_call(
        functools.partial(_proj_kernel, tm=tm, first_win_tile=first_win_tile),
        grid=(B, S // tm),
        in_specs=[
            pl.BlockSpec((1, tm, D_MODEL), lambda b, j: (b, j, 0)),
            pl.BlockSpec((1, D_MODEL), const2),
            pl.BlockSpec((D_MODEL, PROJ_COLS), const2, pipeline_mode=pl.Buffered(1)),
            pl.BlockSpec((tm, 128), lambda b, j: (j, 0)),
            pl.BlockSpec((tm, 128), lambda b, j: (j, 0)),
            pl.BlockSpec((1, WIDTH), const2),
            pl.BlockSpec((WIDTH, WIDTH), const2),
            pl.BlockSpec((N_HEADS // 2, CHUNK, 2 * CHUNK), lambda b, j: (0, 0, 0)),
            pl.BlockSpec((CHUNK, WIDTH), const2),
        ],
        out_specs=[tile, tile, tile, tile, tile_t, tile_t],
        out_shape=[out_sds(BF16), out_sds(F32), out_sds(F32), out_sds(F32), win_sds, win_sds],
        scratch_shapes=[pltpu.VMEM((D_MODEL, PROJ_COLS), BF16)],
        compiler_params=pltpu.CompilerParams(
            dimension_semantics=("arbitrary", "arbitrary"), vmem_limit_bytes=V7X_VMEM_LIMIT),
        name="prompt_proj_sgu",
    )(x, g1, w_in_b, cos, sin, gv, ones_bd, wp, bias)


def _attn_kernel(q_ref, k_ref, v_ref, o_ref, qd, kd, vd, res_o, res_l, nat_o, nat_l, bias, stage, *, seq):
    n_tiles = seq // CHUNK
    lane = lax.broadcasted_iota(jnp.int32, (CHUNK, 128), 1)
    left = lane < HEAD_DIM
    qi2 = lax.broadcasted_iota(jnp.int32, (2 * CHUNK, 2 * CHUNK), 0) % CHUNK
    kj2 = lax.broadcasted_iota(jnp.int32, (2 * CHUNK, 2 * CHUNK), 1)
    dist2 = CHUNK + qi2 - kj2
    band2 = (dist2 >= 0) & (dist2 <= N_KEYS)
    zero_q = jnp.zeros((CHUNK, 128), BF16)
    bias[0] = jnp.where(band2, 0.0, NEG)
    bias[1] = jnp.where(band2 & (kj2 >= CHUNK), 0.0, NEG)

    kd[0:CHUNK, :] = jnp.zeros((CHUNK, 128), BF16)
    vd[0:CHUNK, :] = jnp.zeros((CHUNK, 128), BF16)

    for p, dil in enumerate(DILATIONS):
        sub = seq // dil
        nb = sub // CHUNK
        for ti, (src_ref, dst, pad) in enumerate(((q_ref, qd, 0), (k_ref, kd, CHUNK), (v_ref, vd, CHUNK))):
            if dil == 1:
                dst[pad:pad + seq, :] = src_ref[0].astype(BF16)
            elif dil == SUBLANE_STRIDE:
                for r in range(dil):
                    val = src_ref[0, pl.ds(r, sub, stride=dil), :]
                    stage[ti, r * sub:(r + 1) * sub, :] = val
                    dst[pad + r * sub:pad + (r + 1) * sub, :] = val.astype(BF16)
            else:
                coarse = seq // SUBLANE_STRIDE
                for r_lo in range(SUBLANE_STRIDE):
                    for r_hi in range(SUBLANE_STRIDE):
                        r = r_lo + SUBLANE_STRIDE * r_hi
                        val = stage[ti, pl.ds(r_lo * coarse + r_hi, sub, stride=SUBLANE_STRIDE), :]
                        dst[pad + r * sub:pad + (r + 1) * sub, :] = val.astype(BF16)

        def tile_body(g, i, p=p, nb=nb):
            t = g * TILES_PER_STEP + i
            row = pl.multiple_of(t * CHUNK, CHUNK)
            qt = qd[pl.ds(row, CHUNK), :]
            k2 = kd[pl.ds(row, 2 * CHUNK), :]
            v2 = vd[pl.ds(row, 2 * CHUNK), :]
            if TILES_PER_STEP % nb == 0:
                variant = 1 if i % nb == 0 else 0
            elif i == 0:
                variant = jnp.where((g * TILES_PER_STEP) % nb == 0, 1, 0)
            else:
                variant = 0
            q2 = jnp.concatenate([jnp.where(left, qt, zero_q), jnp.where(left, zero_q, qt)], axis=0)
            s = lax.dot_general(q2, k2, (((1,), (1,)), ((), ())), preferred_element_type=F32)
            s = s + bias[variant]
            m = jnp.max(s, axis=1, keepdims=True)
            e = jnp.exp(s - m)
            den = jnp.sum(e, axis=1, keepdims=True)
            pv = jnp.dot(e.astype(BF16), v2, preferred_element_type=F32) / den
            lse = jnp.broadcast_to(m + jnp.log(den), (2 * CHUNK, 128))
            res_o[p, pl.ds(row, CHUNK), :] = jnp.where(left, pv[0:CHUNK], pv[CHUNK:2 * CHUNK])
            res_l[p, pl.ds(row, CHUNK), :] = jnp.where(left, lse[0:CHUNK], lse[CHUNK:2 * CHUNK])

        def group_body(g, carry, tile_body=tile_body):
            for i in range(TILES_PER_STEP):
                tile_body(g, i)
            return carry

        lax.fori_loop(0, n_tiles // TILES_PER_STEP, group_body, 0)

    for p, dil in enumerate(DILATIONS):
        if dil == 1:
            continue
        sub = seq // dil
        for si, (res, nat) in enumerate(((res_o, nat_o), (res_l, nat_l))):
            if dil == SUBLANE_STRIDE:
                for r in range(dil):
                    nat[p - 1, pl.ds(r, sub, stride=dil), :] = res[p, r * sub:(r + 1) * sub, :]
            else:
                coarse = seq // SUBLANE_STRIDE
                for r_lo in range(SUBLANE_STRIDE):
                    for r_hi in range(SUBLANE_STRIDE):
                        r = r_lo + SUBLANE_STRIDE * r_hi
                        stage[si, pl.ds(r_lo * coarse + r_hi, sub, stride=SUBLANE_STRIDE), :] = (
                            res[p, r * sub:(r + 1) * sub, :])
                for r_lo in range(SUBLANE_STRIDE):
                    nat[p - 1, pl.ds(r_lo, coarse, stride=SUBLANE_STRIDE), :] = (
                        stage[si, r_lo * coarse:(r_lo + 1) * coarse, :])

    rows_per_step = 256

    def merge_body(c, carry):
        rows = pl.ds(pl.multiple_of(c * rows_per_step, rows_per_step), rows_per_step)
        l0, l1, l2 = res_l[0, rows, :], nat_l[0, rows, :], nat_l[1, rows, :]
        top = jnp.maximum(jnp.maximum(l0, l1), l2)
        w0, w1, w2 = jnp.exp(l0 - top), jnp.exp(l1 - top), jnp.exp(l2 - top)
        num = w0 * res_o[0, rows, :] + w1 * nat_o[0, rows, :] + w2 * nat_o[1, rows, :]
        o_ref[0, rows, :] = (num / (w0 + w1 + w2)).astype(o_ref.dtype)
        return carry

    lax.fori_loop(0, seq // rows_per_step, merge_body, 0)


def _prompt_attention(q, k, v):
    B, S, _ = q.shape
    blk = pl.BlockSpec((1, S, 128), lambda b, hp: (b, 0, hp))
    return pl.pallas_call(
        functools.partial(_attn_kernel, seq=S),
        grid=(B, WIDTH // 128),
        in_specs=[blk, blk, blk],
        out_specs=blk,
        out_shape=jax.ShapeDtypeStruct((B, S, WIDTH), BF16),
        scratch_shapes=[
            pltpu.VMEM((S, 128), BF16),
            pltpu.VMEM((S + CHUNK, 128), BF16),
            pltpu.VMEM((S + CHUNK, 128), BF16),
            pltpu.VMEM((len(DILATIONS), S, 128), F32),
            pltpu.VMEM((len(DILATIONS), S, 128), F32),
            pltpu.VMEM((len(DILATIONS) - 1, S, 128), F32),
            pltpu.VMEM((len(DILATIONS) - 1, S, 128), F32),
            pltpu.VMEM((2, 2 * CHUNK, 2 * CHUNK), F32),
            pltpu.VMEM((3, S, 128), F32),
        ],
        compiler_params=pltpu.CompilerParams(
            dimension_semantics=("arbitrary", "arbitrary"), vmem_limit_bytes=V7X_VMEM_LIMIT),
        name="prompt_dilated_attention",
    )(q, k, v)


def _route(logits):
    lane = lax.broadcasted_iota(jnp.int32, logits.shape, 1)
    big = jnp.int32(ROUTER_LANES)
    lg = jnp.where(lane < N_GROUPS, logits, NEG)
    gmax = jnp.max(lg, axis=1, keepdims=True)
    gp = 1.0 / jnp.sum(jnp.exp(lg - gmax), axis=1, keepdims=True)
    gi = jnp.min(jnp.where(lg == gmax, lane, big), axis=1, keepdims=True)
    lo = N_GROUPS + EXPERTS_PER_GROUP * gi
    le = jnp.where((lane >= lo) & (lane < lo + EXPERTS_PER_GROUP), logits, NEG)
    m1 = jnp.max(le, axis=1, keepdims=True)
    i1 = jnp.min(jnp.where(le == m1, lane, big), axis=1, keepdims=True)
    le2 = jnp.where(lane == i1, NEG, le)
    m2 = jnp.max(le2, axis=1, keepdims=True)
    i2 = jnp.min(jnp.where(le2 == m2, lane, big), axis=1, keepdims=True)
    e2 = jnp.exp(m2 - m1)
    w1 = 1.0 / (1.0 + e2)
    w2 = e2 / (1.0 + e2)
    return jnp.where(lane == i1, gp * w1, jnp.where(lane == i2, gp * w2, 0.0))


def _mix_router_kernel(a_ref, b_ref, x_ref, wo_ref, g2_ref, wr_ref, br_ref, xp_ref, h2_ref, gates_ref, *, precise):
    mixed = (_dot(a_ref[...], wo_ref[0:WIDTH, :], precise)
             + _dot(b_ref[...], wo_ref[WIDTH:2 * WIDTH, :], precise))
    xp = x_ref[...] + mixed
    xp_ref[...] = xp
    h2 = _rmsnorm(xp, g2_ref[...])
    h2_ref[...] = h2.astype(h2_ref.dtype)
    logits = jnp.dot(h2, wr_ref[...], preferred_element_type=F32,
                     precision=lax.Precision.HIGHEST) + br_ref[...]
    gates_ref[...] = _route(logits)


def _mix_router(a, b, x, w_out, g2, w_router, b_router, *, tm, precise=False):
    n = x.shape[0]
    const = lambda i: (0, 0)
    row_blk = lambda w: pl.BlockSpec((tm, w), lambda i: (i, 0))
    return pl.pallas_call(
        functools.partial(_mix_router_kernel, precise=precise),
        grid=(n // tm,),
        in_specs=[row_blk(WIDTH), row_blk(WIDTH), row_blk(D_MODEL),
                  pl.BlockSpec((2 * WIDTH, D_MODEL), const),
                  pl.BlockSpec((1, D_MODEL), const),
                  pl.BlockSpec((D_MODEL, ROUTER_LANES), const),
                  pl.BlockSpec((1, ROUTER_LANES), const)],
        out_specs=[row_blk(D_MODEL), row_blk(D_MODEL), row_blk(ROUTER_LANES)],
        out_shape=[jax.ShapeDtypeStruct((n, D_MODEL), F32),
                   jax.ShapeDtypeStruct((n, D_MODEL), BF16),
                   jax.ShapeDtypeStruct((n, ROUTER_LANES), F32)],
        compiler_params=pltpu.CompilerParams(
            dimension_semantics=("arbitrary",), vmem_limit_bytes=V7X_VMEM_LIMIT),
        name="outproj_router",
    )(a, b, x, w_out, g2, w_router, b_router)


def _nt_dot(w, t):
    return lax.dot_general(w, t, (((1,), (1,)), ((), ())), preferred_element_type=F32)


def _route_t(logits_t):
    row = lax.broadcasted_iota(jnp.int32, logits_t.shape, 0)
    big = jnp.int32(ROUTER_LANES)
    lg = jnp.where(row < N_GROUPS, logits_t, NEG)
    gmax = jnp.max(lg, axis=0, keepdims=True)
    gp = 1.0 / jnp.sum(jnp.exp(lg - gmax), axis=0, keepdims=True)
    gi = jnp.min(jnp.where(lg == gmax, row, big), axis=0, keepdims=True)
    lo = N_GROUPS + EXPERTS_PER_GROUP * gi
    le = jnp.where((row >= lo) & (row < lo + EXPERTS_PER_GROUP), logits_t, NEG)
    m1 = jnp.max(le, axis=0, keepdims=True)
    i1 = jnp.min(jnp.where(le == m1, row, big), axis=0, keepdims=True)
    le2 = jnp.where(row == i1, NEG, le)
    m2 = jnp.max(le2, axis=0, keepdims=True)
    i2 = jnp.min(jnp.where(le2 == m2, row, big), axis=0, keepdims=True)
    e2 = jnp.exp(m2 - m1)
    return i1 - N_GROUPS, i2 - N_GROUPS, gp / (1.0 + e2), gp * e2 / (1.0 + e2)


def _mix_route_sort_kernel(a_ref, b_ref, x_ref, wo_ref, g2_ref, wrh_ref, wrl_ref, brc_ref, tri_ref, ltri_ref,
                           qkvt_ref, ck_ref, cv_ref,
                           xp_ref, h2_ref, mrow_ref, mcol_ref, tab_ref, so_ref, wob, *, n_seq, win):
    t = MOE_BLOCK

    @pl.when(pl.program_id(0) == 0)
    def _():
        wob[...] = wo_ref[...].astype(BF16)

    xp = x_ref[...] + _dot(a_ref[...], wob[0:WIDTH, :]) + _dot(b_ref[...], wob[WIDTH:2 * WIDTH, :])
    xp_ref[...] = xp
    h2 = _rmsnorm(xp, g2_ref[...])
    hi = h2.astype(BF16)
    h2_ref[...] = hi
    lo = (h2 - hi.astype(F32)).astype(BF16)
    logits_t = (_nt_dot(wrh_ref[...], hi) + _nt_dot(wrh_ref[...], lo) + _nt_dot(wrl_ref[...], hi)
                + brc_ref[...])
    ex1, ex2, gate1, gate2 = _route_t(logits_t[0:ROUTER_ROWS])

    pair_e = jnp.concatenate([ex1, ex2], axis=1)
    row = lax.broadcasted_iota(jnp.int32, (N_EXPERTS, 2 * t), 0)
    onehot = jnp.where(row == pair_e, 1.0, 0.0)
    cum = _dot(onehot, tri_ref[...])
    rank = jnp.sum(onehot * cum, axis=0, keepdims=True) - 1.0
    counts = cum[:, 2 * t - 1:2 * t]
    units32 = jnp.floor((counts + (MOE_ROW_ALIGN - 1)) * (1.0 / MOE_ROW_ALIGN))
    units = jnp.concatenate([jnp.broadcast_to(units32, (N_EXPERTS, 128)),
                             jnp.zeros((ROUTER_LANES - N_EXPERTS, 128), F32)], axis=0)
    off = _dot(ltri_ref[...], units) * MOE_ROW_ALIGN
    dst = jnp.sum(onehot * off[0:N_EXPERTS, 0:1], axis=0, keepdims=True) + rank

    r8 = lax.broadcasted_iota(jnp.int32, (8, t), 0)
    mrow_ref[0] = jnp.where(r8 == 0, dst[:, 0:t], jnp.where(r8 == 1, dst[:, t:2 * t],
                            jnp.where(r8 == 2, gate1, jnp.where(r8 == 3, gate2, 0.0))))
    r128 = lax.broadcasted_iota(jnp.int32, (ROUTER_LANES, t), 0)
    meta = jnp.where(r128 == 0, dst[:, 0:t], jnp.where(r128 == 1, dst[:, t:2 * t],
                     jnp.where(r128 == 2, gate1, jnp.where(r128 == 3, gate2, 0.0))))
    mcol_ref[...] = meta.T
    lane = lax.broadcasted_iota(jnp.int32, (ROUTER_LANES, 128), 1)
    n_rows = units * MOE_ROW_ALIGN
    chunks = jnp.floor((n_rows + (MOE_CHUNK - 1)) * (1.0 / MOE_CHUNK))
    tab_ref[0] = jnp.where(lane == 0, off, jnp.where(lane == 1, chunks, jnp.where(lane == 2, off + n_rows, 0.0)))

    _sample_attend(qkvt_ref, ck_ref, cv_ref, so_ref, jnp.minimum(pl.program_id(0), n_seq - 1), win)


def _mix_route_sort(a, b, x, w_out, g2, wr_hi, wr_lo, br_col, tri, ltri, qkvt, cache_k_t, cache_v_t):
    n = x.shape[0]
    t = MOE_BLOCK
    nblk = n // t
    n_seq, _, _, win = cache_k_t.shape
    assert n_seq <= nblk, "one sample sequence rides on each prompt block"
    const = lambda i: (0, 0)
    row_blk = lambda w: pl.BlockSpec((t, w), lambda i: (i, 0))
    seq_blk = lambda shape: pl.BlockSpec((1,) + shape, lambda i: (jnp.minimum(i, n_seq - 1),) + (0,) * len(shape))
    return pl.pallas_call(
        functools.partial(_mix_route_sort_kernel, n_seq=n_seq, win=win),
        grid=(nblk,),
        in_specs=[row_blk(WIDTH), row_blk(WIDTH), row_blk(D_MODEL),
                  pl.BlockSpec((2 * WIDTH, D_MODEL), const, pipeline_mode=pl.Buffered(1)),
                  pl.BlockSpec((1, D_MODEL), const),
                  pl.BlockSpec((ROUTER_LANES, D_MODEL), const),
                  pl.BlockSpec((ROUTER_LANES, D_MODEL), const),
                  pl.BlockSpec((ROUTER_LANES, 1), const),
                  pl.BlockSpec((2 * t, 2 * t), const),
                  pl.BlockSpec((ROUTER_LANES, ROUTER_LANES), const),
                  pl.BlockSpec(qkvt.shape, const),
                  seq_blk((N_HEADS, HEAD_DIM, win)), seq_blk((N_HEADS, HEAD_DIM, win))],
        out_specs=[row_blk(D_MODEL), row_blk(D_MODEL),
                   pl.BlockSpec((1, 8, t), lambda i: (i, 0, 0)),
                   row_blk(ROUTER_LANES),
                   pl.BlockSpec((1, ROUTER_LANES, 128), lambda i: (i, 0, 0)),
                   seq_blk((HEAD_DIM, N_HEADS))],
        out_shape=[jax.ShapeDtypeStruct((n, D_MODEL), F32),
                   jax.ShapeDtypeStruct((n, D_MODEL), BF16),
                   jax.ShapeDtypeStruct((nblk, 8, t), F32),
                   jax.ShapeDtypeStruct((n, ROUTER_LANES), F32),
                   jax.ShapeDtypeStruct((nblk, ROUTER_LANES, 128), F32),
                   jax.ShapeDtypeStruct((n_seq, HEAD_DIM, N_HEADS), F32)],
        scratch_shapes=[pltpu.VMEM((2 * WIDTH, D_MODEL), BF16)],
        compiler_params=pltpu.CompilerParams(
            dimension_semantics=("arbitrary",), vmem_limit_bytes=V7X_VMEM_LIMIT),
        name="outproj_route_sort",
    )(a, b, x, w_out, g2, wr_hi, wr_lo, br_col, tri, ltri, qkvt, cache_k_t, cache_v_t)


def _moe_sparse_kernel(tab_ref, h_ref, xp_ref, mrow_ref, mcol_ref, w1_ref, w2_ref, gf_ref, sh_ref, sg_ref, sx_ref,
                       y_ref, sy_ref, xs, hs, os, shid):
    t = MOE_BLOCK
    blk = pl.program_id(0)
    tab = lambda e, c: tab_ref[(blk * N_EXPERTS + e) * 3 + c]

    mrow = mrow_ref[0]
    dst1 = mrow[0:1, :].astype(jnp.int32)
    dst2 = mrow[1:2, :].astype(jnp.int32)
    piece = 512
    for r0 in range(0, MOE_SORT_ROWS, piece):
        d_idx = lax.broadcasted_iota(jnp.int32, (piece, t), 0) + r0
        sel = jnp.where((d_idx == dst1) | (d_idx == dst2), 1.0, 0.0)
        xs[r0:r0 + piece, :] = _dot(sel, h_ref[...]).astype(BF16)
    tail = slice(MOE_SORT_ROWS, MOE_SORT_ROWS + MOE_CHUNK)
    xs[tail, :] = jnp.zeros((MOE_CHUNK, D_MODEL), BF16)
    os[...] = jnp.zeros(os.shape, BF16)

    def gate_up(e, r0):
        ab = jnp.dot(xs[pl.ds(r0, MOE_CHUNK), :], w1_ref[e], preferred_element_type=F32)
        a = ab[:, :D_EXPERT]
        return (a * (1.0 / (1.0 + jnp.exp(-a))) * ab[:, D_EXPERT:]).astype(BF16)

    def down(e, hid):
        return jnp.dot(hid, w2_ref[e], preferred_element_type=F32).astype(BF16)

    def first_gate_up(g, carry):
        for i in range(MOE_EXPERTS_PER_STEP):
            e = g * MOE_EXPERTS_PER_STEP + i
            r0 = pl.multiple_of(tab(e, 0), MOE_ROW_ALIGN)
            hs[pl.ds(r0, MOE_CHUNK), :] = gate_up(e, r0)
        return carry

    def first_down(g, carry):
        for i in range(MOE_EXPERTS_PER_STEP):
            e = g * MOE_EXPERTS_PER_STEP + i
            r0 = pl.multiple_of(tab(e, 0), MOE_ROW_ALIGN)
            os[pl.ds(r0, MOE_CHUNK), :] = down(e, hs[pl.ds(r0, MOE_CHUNK), :])
        return carry

    lax.fori_loop(0, N_EXPERTS // MOE_EXPERTS_PER_STEP, first_gate_up, 0)
    lax.fori_loop(0, N_EXPERTS // MOE_EXPERTS_PER_STEP, first_down, 0)

    def more_chunks(e, carry):
        off, n_chunks, end = tab(e, 0), tab(e, 1), tab(e, 2)

        def chunk(c, carry):
            r0 = pl.multiple_of(off + c * MOE_CHUNK, MOE_ROW_ALIGN)
            rows = r0 + lax.broadcasted_iota(jnp.int32, (MOE_CHUNK, D_MODEL), 0)
            os[pl.ds(r0, MOE_CHUNK), :] = jnp.where(rows < end, down(e, gate_up(e, r0)), os[pl.ds(r0, MOE_CHUNK), :])
            return carry

        return lax.fori_loop(1, n_chunks, chunk, carry)

    lax.fori_loop(0, N_EXPERTS, more_chunks, 0)

    mcol = mcol_ref[...]
    d1c = mcol[:, 0:1].astype(jnp.int32)
    d2c = mcol[:, 1:2].astype(jnp.int32)
    l_idx = lax.broadcasted_iota(jnp.int32, (t, MOE_SORT_ROWS), 1)
    comb = jnp.where(l_idx == d1c, mcol[:, 2:3], 0.0) + jnp.where(l_idx == d2c, mcol[:, 3:4], 0.0)
    y = xp_ref[...] + _dot(comb, os[0:MOE_SORT_ROWS, :])
    y_ref[...] = _rmsnorm(y, gf_ref[...])

    @pl.when(blk == 0)
    def _():
        sh = sh_ref[...]
        gates = sg_ref[...]
        for e in range(N_EXPERTS):
            ab = jnp.dot(sh, w1_ref[e], preferred_element_type=F32)
            a = ab[:, :D_EXPERT]
            gate = gates[:, N_GROUPS + e:N_GROUPS + e + 1]
            shid[:, e * D_EXPERT:(e + 1) * D_EXPERT] = (
                a * (1.0 / (1.0 + jnp.exp(-a))) * ab[:, D_EXPERT:] * gate).astype(BF16)
        w2_all = w2_ref[...].reshape(N_EXPERTS * D_EXPERT, D_MODEL)
        ys = sx_ref[...] + jnp.dot(shid[...], w2_all, preferred_element_type=F32)
        sy_ref[...] = _rmsnorm(ys, gf_ref[...])


def _moe_sparse(tab, h2, xp, mrow, mcol, w1_b, w2_b, gf, s_h, s_gates, s_x):
    n = h2.shape[0]
    n_s = s_h.shape[0]
    t = MOE_BLOCK
    row_blk = lambda w: pl.BlockSpec((t, w), lambda i, tab: (i, 0))
    whole = lambda shape: pl.BlockSpec(shape, lambda i, tab: (0,) * len(shape))
    resident = lambda shape: pl.BlockSpec(shape, lambda i, tab: (0,) * len(shape), pipeline_mode=pl.Buffered(1))
    return pl.pallas_call(
        _moe_sparse_kernel,
        grid_spec=pltpu.PrefetchScalarGridSpec(
            num_scalar_prefetch=1,
            grid=(n // t,),
            in_specs=[row_blk(D_MODEL), row_blk(D_MODEL),
                      pl.BlockSpec((1, 8, t), lambda i, tab: (i, 0, 0)),
                      row_blk(ROUTER_LANES),
                      resident(w1_b.shape), resident(w2_b.shape),
                      whole((1, D_MODEL)),
                      whole(s_h.shape), whole(s_gates.shape), whole(s_x.shape)],
            out_specs=[row_blk(D_MODEL), whole((n_s, D_MODEL))],
            scratch_shapes=[pltpu.VMEM((MOE_SORT_ROWS + MOE_CHUNK, D_MODEL), BF16),
                            pltpu.VMEM((MOE_SORT_ROWS + MOE_CHUNK, D_EXPERT), BF16),
                            pltpu.VMEM((MOE_SORT_ROWS + MOE_CHUNK, D_MODEL), BF16),
                            pltpu.VMEM((n_s, N_EXPERTS * D_EXPERT), BF16)]),
        out_shape=[jax.ShapeDtypeStruct((n, D_MODEL), F32), jax.ShapeDtypeStruct((n_s, D_MODEL), F32)],
        compiler_params=pltpu.CompilerParams(
            dimension_semantics=("arbitrary",), vmem_limit_bytes=V7X_VMEM_LIMIT),
        name="moe_sparse",
    )(tab, h2, xp, mrow, mcol, w1_b, w2_b, gf, s_h, s_gates, s_x)


def _sample_proj_kernel(x_ref, g1_ref, w_ref, cos_ref, sin_ref, gv_ref, ones_ref, w00_ref, b0_ref,
                        rep_ref, foldt_ref, a_ref, k_ref, v_ref, vn_ref, qkvt_ref):
    h = _rmsnorm(x_ref[...], g1_ref[...])

    def proj(i):
        return _dot(h, w_ref[:, i * WIDTH:(i + 1) * WIDTH], precise=True)

    cos = _tile_lanes(cos_ref[...], WIDTH // 128)
    sin = _tile_lanes(sin_ref[...], WIDTH // 128)
    q = _rope(proj(2), cos, sin) * (HEAD_DIM ** -0.5)
    k = _rope(proj(3), cos, sin)
    v = proj(4)
    vn = _group_rmsnorm(proj(1), ones_ref[...], gv_ref[...], precise=True)
    a_ref[...] = proj(0) * (w00_ref[...] * vn + b0_ref[...])
    k_ref[...] = k
    v_ref[...] = v
    vn_ref[...] = vn

    n_rep = rep_ref.shape[0]
    r_idx = lax.broadcasted_iota(jnp.int32, (n_rep, WIDTH), 0)
    l_idx = lax.broadcasted_iota(jnp.int32, (n_rep, WIDTH), 1)
    own = (l_idx // HEAD_DIM) == (r_idx % N_HEADS)
    for t, src in enumerate((q, k, v)):
        rep = _dot(rep_ref[...], src, precise=True)
        qkvt_ref[t * HEAD_DIM:(t + 1) * HEAD_DIM, :] = lax.dot_general(
            foldt_ref[...], jnp.where(own, rep, 0.0), (((1,), (1,)), ((), ())),
            preferred_element_type=F32, precision=lax.Precision.HIGHEST)


def _sample_proj(x, g1, w_in, cos, sin, gv, ones_bd, w00, b0, rep, foldt):
    bd = x.shape[0]
    sds = lambda r, c: jax.ShapeDtypeStruct((r, c), F32)
    return pl.pallas_call(
        _sample_proj_kernel,
        out_shape=[sds(bd, WIDTH), sds(bd, WIDTH), sds(bd, WIDTH), sds(bd, WIDTH),
                   sds(3 * HEAD_DIM, bd * N_HEADS)],
        compiler_params=pltpu.CompilerParams(vmem_limit_bytes=V7X_VMEM_LIMIT),
        name="sample_proj",
    )(x, g1, w_in, cos, sin, gv, ones_bd, w00, b0, rep, foldt)


def _sample_attend(qkvt_ref, k_ref, v_ref, o_ref, b, win):
    n_col = qkvt_ref.shape[1]
    c_idx = lax.broadcasted_iota(jnp.int32, (n_col, 128), 0)
    l_idx = lax.broadcasted_iota(jnp.int32, (n_col, 128), 1)
    pick = jnp.where((c_idx == b * N_HEADS + l_idx) & (l_idx < N_HEADS), 1.0, 0.0)
    cols = _dot(qkvt_ref[...], pick, precise=True)
    dist = win - lax.broadcasted_iota(jnp.int32, (1, win), 1)
    members = [(dist <= N_KEYS * dil) & (dist % dil == 0) for dil in DILATIONS]

    head = lax.broadcasted_iota(jnp.int32, (N_HEADS, win), 0)
    s = jnp.zeros((N_HEADS, win), F32)
    s_self = jnp.zeros((N_HEADS, 1), F32)
    for h in range(N_HEADS):
        qc = cols[0:HEAD_DIM, h:h + 1]
        kc = cols[HEAD_DIM:2 * HEAD_DIM, h:h + 1]
        s_h = jnp.sum(k_ref[0, h] * qc, axis=0, keepdims=True)
        s = jnp.where(head == h, s_h, s)
        s_self = jnp.where(head[:, 0:1] == h, jnp.sum(qc * kc, axis=0, keepdims=True), s_self)

    es, e_selfs, dens, lses = [], [], [], []
    for mem in members:
        sm = jnp.where(mem, s, NEG)
        m = jnp.maximum(jnp.max(sm, axis=1, keepdims=True), s_self)
        e = jnp.exp(sm - m)
        e_self = jnp.exp(s_self - m)
        den = jnp.sum(e, axis=1, keepdims=True) + e_self
        es.append(e)
        e_selfs.append(e_self)
        dens.append(den)
        lses.append(m + jnp.log(den))
    top = jnp.maximum(jnp.maximum(lses[0], lses[1]), lses[2])
    ws = [jnp.exp(l - top) for l in lses]
    wsum = ws[0] + ws[1] + ws[2]
    coef = [w / (den * wsum) for w, den in zip(ws, dens)]
    p_keys = coef[0] * es[0] + coef[1] * es[1] + coef[2] * es[2]
    p_self = coef[0] * e_selfs[0] + coef[1] * e_selfs[1] + coef[2] * e_selfs[2]

    for h in range(N_HEADS):
        vc = cols[2 * HEAD_DIM:3 * HEAD_DIM, h:h + 1]
        o_ref[0, :, h:h + 1] = (jnp.sum(v_ref[0, h] * p_keys[h:h + 1, :], axis=1, keepdims=True)
                                + p_self[h:h + 1, :] * vc)


def _rope_tables(first_pos, count):
    half = HEAD_DIM // 2
    inv = ROPE_THETA ** (-np.arange(half, dtype=np.float64) * 2.0 / HEAD_DIM)
    ang = (first_pos + np.arange(count, dtype=np.float64))[:, None] * inv[None, :]
    cos, sin = np.cos(ang), np.sin(ang)
    cos128 = np.concatenate([cos, cos, cos, cos], axis=1).astype(np.float32)
    sin128 = np.concatenate([-sin, sin, -sin, sin], axis=1).astype(np.float32)
    return jnp.asarray(cos128), jnp.asarray(sin128)


def kernel(x_prompt, x_sample, cache_win_k, cache_win_v, ln1_g, w_in, sgu_norm_g, sgu_w, sgu_b, w_out, ln2_g,
           w_router_group, b_router_group, w_router_expert, b_router_expert, w_gate, w_up, w_down, lnf_g):
    depth = w_in.shape[0]
    assert depth == 1 and x_sample.shape[1] == 1
    B, S, _ = x_prompt.shape
    bd = x_sample.shape[0]
    win = cache_win_k.shape[2]
    assert S % (max(DILATIONS) * CHUNK) == 0 and win >= max(DILATIONS) * N_KEYS and PAST_LEN % CHUNK == 0
    l = 0

    w1_b =jnp.concatenate([w_gate[l], w_up[l]], axis=-1).astype(BF16)
    w2_b = w_down[l].astype(BF16)
    pad = ROUTER_LANES - N_GROUPS - N_EXPERTS
    w_router = jnp.pad(jnp.concatenate([w_router_group[l], w_router_expert[l]], axis=1), ((0, 0), (0, pad)))
    b_router = jnp.pad(jnp.concatenate([b_router_group[l], b_router_expert[l]]), (0, pad))[None, :]
    g1 = ln1_g[l][None, :]
    g2 = ln2_g[l][None, :]
    gf = lnf_g[None, :]
    gv = sgu_norm_g[l].reshape(1, WIDTH)
    grp = jnp.arange(WIDTH) // HEAD_DIM
    ones_bd = jnp.where(grp[:, None] == grp[None, :], 1.0 / HEAD_DIM, 0.0).astype(BF16)
    wp = jnp.concatenate([sgu_w[l][0::2], sgu_w[l][1::2]], axis=-1)
    bias = jnp.repeat(sgu_b[l].T, HEAD_DIM, axis=1)
    w00 = jnp.repeat(sgu_w[l][:, 0, 0], HEAD_DIM)[None, :]
    b0 = jnp.repeat(sgu_b[l][:, 0], HEAD_DIM)[None, :]

    cos_s, sin_s = _rope_tables(PAST_LEN, 1)
    rep = (jnp.arange(bd * N_HEADS)[:, None] // N_HEADS == jnp.arange(bd)[None, :]).astype(F32)
    foldt = (jnp.arange(HEAD_DIM)[:, None] == jnp.arange(WIDTH)[None, :] % HEAD_DIM).astype(F32)
    xs = x_sample.reshape(bd, D_MODEL)
    a_s, k_s, v_s, vn_s, qkvt = _sample_proj(xs, g1, w_in[l], cos_s, sin_s, gv, ones_bd, w00, b0, rep, foldt)
    to_pos_minor = lambda c: jnp.transpose(c, (0, 2, 3, 1))

    cos_p, sin_p = _rope_tables(0, S)
    a_p, q_p, k_p, v_p, kt_p, vt_p = _prompt_proj(x_prompt, g1, w_in[l], cos_p, sin_p, gv, ones_bd, wp, bias)
    b_p = _prompt_attention(q_p, k_p, v_p)
    n = B * S
    assert n % MOE_BLOCK == 0
    wr_t = w_router.T
    wr_hi = wr_t.astype(BF16)
    wr_lo = (wr_t - wr_hi.astype(F32)).astype(BF16)
    pair_idx = jnp.arange(2 * MOE_BLOCK)
    tri = (pair_idx[:, None] <= pair_idx[None, :]).astype(BF16)
    lane_idx = jnp.arange(ROUTER_LANES)
    ltri = (lane_idx[None, :] < lane_idx[:, None]).astype(BF16)
    xp2, h2, mrow, mcol, tab_f, o3 = _mix_route_sort(
        a_p.reshape(n, WIDTH), b_p.reshape(n, WIDTH), x_prompt.reshape(n, D_MODEL),
        w_out[l], g2, wr_hi, wr_lo, b_router.reshape(ROUTER_LANES, 1), tri, ltri,
        qkvt, to_pos_minor(cache_win_k[l]), to_pos_minor(cache_win_v[l]))
    tab = tab_f[:, :N_EXPERTS, 0:3].astype(jnp.int32).reshape(-1)

    b_s = jnp.transpose(o3, (0, 2, 1)).reshape(bd, WIDTH)
    xs2, hs2, gates_s = _mix_router(a_s, b_s, xs, w_out[l], g2, w_router, b_router, tm=bd, precise=True)

    y_prompt, y_sample = _moe_sparse(tab, h2, xp2, mrow, mcol, w1_b, w2_b, gf, hs2, gates_s, xs2)
    y_prompt = y_prompt.reshape(B, S, D_MODEL)
    y_sample = y_sample.reshape(bd, 1, D_MODEL)
    buf_p = min(MAX_WINDOW, S)
    to_win = lambda t: jnp.transpose(t.reshape(1, B, N_HEADS, HEAD_DIM, buf_p), (0, 1, 4, 2, 3))
    new_k_p = to_win(kt_p)
    new_v_p = to_win(vt_p)

    shape_s = (1, bd, 1, N_HEADS, HEAD_DIM)
    return (y_prompt, y_sample, new_k_p, new_v_p,
            k_s.reshape(shape_s), v_s.reshape(shape_s), vn_s.reshape(shape_s))
```

```python
import functools

import jax
import jax.numpy as jnp
import numpy as np
from jax import lax
from jax.experimental import pallas as pl
from jax.experimental.pallas import tpu as pltpu

F32 = jnp.float32
BF16 = jnp.bfloat16

D_MODEL = 1024
HEAD_DIM = 64
N_HEADS = 8
WIDTH = N_HEADS * HEAD_DIM
PROJ_COLS = 5 * WIDTH
CHUNK = 128
DILATIONS = (1, 4, 16)
N_KEYS = 128
MAX_WINDOW = 2048
PAST_LEN = 16384
ROPE_THETA = 10000.0
N_GROUPS = 4
EXPERTS_PER_GROUP = 8
N_EXPERTS = N_GROUPS * EXPERTS_PER_GROUP
D_EXPERT = 128
EPS = 1e-6
NEG = -1e30
TILES_PER_STEP = 32
SUBLANE_STRIDE = 4
assert DILATIONS == (1, SUBLANE_STRIDE, SUBLANE_STRIDE ** 2)
MOE_BLOCK = 512
MOE_ROW_ALIGN = 16
MOE_CHUNK = 64
MOE_EXPERTS_PER_STEP = 16
MOE_GATE_UP_BLOCKS = 4
MOE_DOWN_BLOCKS = 2
MOE_SORT_ROWS = -(-(2 * MOE_BLOCK + N_EXPERTS * (MOE_ROW_ALIGN - 1)) // 512) * 512
MOE_ROWS = MOE_SORT_ROWS + MOE_CHUNK
ROUTER_ROWS = 40
ROUTER_LANES = 128
V7X_VMEM_LIMIT = 56 * 1024 * 1024


def _rmsnorm(x, g):
    return x * lax.rsqrt(jnp.mean(x * x, axis=-1, keepdims=True) + EPS) * g


def _tile_lanes(t, reps):
    return jnp.concatenate([t] * reps, axis=1)


def _rope(t, cos, sin_signed):
    lane = lax.broadcasted_iota(jnp.int32, t.shape, 1)
    first_half = (lane % HEAD_DIM) < (HEAD_DIM // 2)
    n = t.shape[1]
    partner = jnp.where(first_half, pltpu.roll(t, n - HEAD_DIM // 2, 1), pltpu.roll(t, HEAD_DIM // 2, 1))
    return t * cos + partner * sin_signed


def _dot(a, b, precise=False):
    if precise:
        return jnp.dot(a.astype(F32), b.astype(F32), preferred_element_type=F32,
                       precision=lax.Precision.HIGHEST)
    return jnp.dot(a.astype(BF16), b.astype(BF16), preferred_element_type=F32)


def _group_rmsnorm(va, ones_bd, gv, precise=False):
    ms = _dot(va * va, ones_bd, precise)
    return va * lax.rsqrt(ms + EPS) * gv


def _proj_kernel(x_ref, g1_ref, w_ref, cos_ref, sin_ref, gv_ref, ones_ref, wp_ref, bias_ref,
                 a_ref, q_ref, k_ref, v_ref, kt_ref, vt_ref, wb, *, tm, first_win_tile):
    @pl.when((pl.program_id(0) == 0) & (pl.program_id(1) == 0))
    def _():
        wb[...] = w_ref[...].astype(BF16)

    h = _rmsnorm(x_ref[0], g1_ref[...]).astype(BF16)

    def proj(i):
        return jnp.dot(h, wb[:, i * WIDTH:(i + 1) * WIDTH], preferred_element_type=F32)

    cos = _tile_lanes(cos_ref[...], WIDTH // 128)
    sin = _tile_lanes(sin_ref[...], WIDTH // 128)
    q_ref[0] = _rope(proj(2), cos, sin) * (HEAD_DIM ** -0.5)
    k_ref[0] = _rope(proj(3), cos, sin)
    v_ref[0] = proj(4)

    u = proj(0)
    vn = _group_rmsnorm(proj(1), ones_ref[...], gv_ref[...]).astype(BF16)

    lane = lax.broadcasted_iota(jnp.int32, (CHUNK, 128), 1)
    left = lane < HEAD_DIM
    row = lax.broadcasted_iota(jnp.int32, (CHUNK, 2 * CHUNK), 0)
    col = lax.broadcasted_iota(jnp.int32, (CHUNK, 2 * CHUNK), 1)
    causal = (col % CHUNK) <= row
    zero = jnp.zeros((CHUNK, 128), BF16)
    wps = [jnp.where(causal, wp_ref[gp], 0.0).astype(BF16) for gp in range(N_HEADS // 2)]
    for c in range(tm // CHUNK):
        rows = slice(c * CHUNK, (c + 1) * CHUNK)
        mixes = []
        for gp in range(N_HEADS // 2):
            vv = vn[rows, gp * 128:(gp + 1) * 128]
            v2 = jnp.concatenate([jnp.where(left, vv, zero), jnp.where(left, zero, vv)], axis=0)
            mixes.append(jnp.dot(wps[gp], v2, preferred_element_type=F32))
        mix = jnp.concatenate(mixes, axis=1) + bias_ref[...]
        a_ref[0, rows, :] = (u[rows, :] * mix).astype(a_ref.dtype)

    @pl.when(pl.program_id(1) >= first_win_tile)
    def _():
        kt_ref[0] = k_ref[0].T
        vt_ref[0] = v_ref[0].T


def _prompt_proj(x, g1, w_in_b, cos, sin, gv, ones_bd, wp, bias, *, tm=512):
    B, S, _ = x.shape
    const2 = lambda b, j: (0, 0)
    out_sds = lambda dt: jax.ShapeDtypeStruct((B, S, WIDTH), dt)
    tile = pl.BlockSpec((1, tm, WIDTH), lambda b, j: (b, j, 0))
    win = min(MAX_WINDOW, S)
    first_win_tile = (S - win) // tm
    tile_t = pl.BlockSpec((1, WIDTH, tm), lambda b, j: (b, 0, jnp.maximum(j - first_win_tile, 0)))
    win_sds = jax.ShapeDtypeStruct((B, WIDTH, win), F32)
    return pl.pallas_call(
        functools.partial(_proj_kernel, tm=tm, first_win_tile=first_win_tile),
        grid=(B, S // tm),
        in_specs=[
            pl.BlockSpec((1, tm, D_MODEL), lambda b, j: (b, j, 0)),
            pl.BlockSpec((1, D_MODEL), const2),
            pl.BlockSpec((D_MODEL, PROJ_COLS), const2, pipeline_mode=pl.Buffered(1)),
            pl.BlockSpec((tm, 128), lambda b, j: (j, 0)),
            pl.BlockSpec((tm, 128), lambda b, j: (j, 0)),
            pl.BlockSpec((1, WIDTH), const2),
            pl.BlockSpec((WIDTH, WIDTH), const2),
            pl.BlockSpec((N_HEADS // 2, CHUNK, 2 * CHUNK), lambda b, j: (0, 0, 0)),
            pl.BlockSpec((CHUNK, WIDTH), const2),
        ],
        out_specs=[tile, tile, tile, tile, tile_t, tile_t],
        out_shape=[out_sds(BF16), out_sds(F32), out_sds(F32), out_sds(F32), win_sds, win_sds],
        scratch_shapes=[pltpu.VMEM((D_MODEL, PROJ_COLS), BF16)],
        compiler_params=pltpu.CompilerParams(
            dimension_semantics=("arbitrary", "arbitrary"), vmem_limit_bytes=V7X_VMEM_LIMIT),
        name="prompt_proj_sgu",
    )(x, g1, w_in_b, cos, sin, gv, ones_bd, wp, bias)


def _attn_kernel(q_ref, k_ref, v_ref, o_ref, qd, kd, vd, res_o, res_l, nat_o, nat_l, bias, stage, *, seq):
    n_tiles = seq // CHUNK
    lane = lax.broadcasted_iota(jnp.int32, (CHUNK, 128), 1)
    left = lane < HEAD_DIM
    qi2 = lax.broadcasted_iota(jnp.int32, (2 * CHUNK, 2 * CHUNK), 0) % CHUNK
    kj2 = lax.broadcasted_iota(jnp.int32, (2 * CHUNK, 2 * CHUNK), 1)
    dist2 = CHUNK + qi2 - kj2
    band2 = (dist2 >= 0) & (dist2 <= N_KEYS)
    zero_q = jnp.zeros((CHUNK, 128), BF16)
    bias[0] = jnp.where(band2, 0.0, NEG)
    bias[1] = jnp.where(band2 & (kj2 >= CHUNK), 0.0, NEG)

    kd[0:CHUNK, :] = jnp.zeros((CHUNK, 128), BF16)
    vd[0:CHUNK, :] = jnp.zeros((CHUNK, 128), BF16)

    for p, dil in enumerate(DILATIONS):
        sub = seq // dil
        nb = sub // CHUNK
        for ti, (src_ref, dst, pad) in enumerate(((q_ref, qd, 0), (k_ref, kd, CHUNK), (v_ref, vd, CHUNK))):
            if dil == 1:
                dst[pad:pad + seq, :] = src_ref[0].astype(BF16)
            elif dil == SUBLANE_STRIDE:
                for r in range(dil):
                    val = src_ref[0, pl.ds(r, sub, stride=dil), :]
                    stage[ti, r * sub:(r + 1) * sub, :] = val
                    dst[pad + r * sub:pad + (r + 1) * sub, :] = val.astype(BF16)
            else:
                coarse = seq // SUBLANE_STRIDE
                for r_lo in range(SUBLANE_STRIDE):
                    for r_hi in range(SUBLANE_STRIDE):
                        r = r_lo + SUBLANE_STRIDE * r_hi
                        val = stage[ti, pl.ds(r_lo * coarse + r_hi, sub, stride=SUBLANE_STRIDE), :]
                        dst[pad + r * sub:pad + (r + 1) * sub, :] = val.astype(BF16)

        def tile_body(g, i, p=p, nb=nb):
            t = g * TILES_PER_STEP + i
            row = pl.multiple_of(t * CHUNK, CHUNK)
            qt = qd[pl.ds(row, CHUNK), :]
            k2 = kd[pl.ds(row, 2 * CHUNK), :]
            v2 = vd[pl.ds(row, 2 * CHUNK), :]
            if TILES_PER_STEP % nb == 0:
                variant = 1 if i % nb == 0 else 0
            elif i == 0:
                variant = jnp.where((g * TILES_PER_STEP) % nb == 0, 1, 0)
            else:
                variant = 0
            q2 = jnp.concatenate([jnp.where(left, qt, zero_q), jnp.where(left, zero_q, qt)], axis=0)
            s = lax.dot_general(q2, k2, (((1,), (1,)), ((), ())), preferred_element_type=F32)
            s = s + bias[variant]
            m = jnp.max(s, axis=1, keepdims=True)
            e = jnp.exp(s - m)
            den = jnp.sum(e, axis=1, keepdims=True)
            pv = jnp.dot(e.astype(BF16), v2, preferred_element_type=F32) / den
            lse = jnp.broadcast_to(m + jnp.log(den), (2 * CHUNK, 128))
            res_o[p, pl.ds(row, CHUNK), :] = jnp.where(left, pv[0:CHUNK], pv[CHUNK:2 * CHUNK])
            res_l[p, pl.ds(row, CHUNK), :] = jnp.where(left, lse[0:CHUNK], lse[CHUNK:2 * CHUNK])

        def group_body(g, carry, tile_body=tile_body):
            for i in range(TILES_PER_STEP):
                tile_body(g, i)
            return carry

        lax.fori_loop(0, n_tiles // TILES_PER_STEP, group_body, 0)

    for p, dil in enumerate(DILATIONS):
        if dil == 1:
            continue
        sub = seq // dil
        for si, (res, nat) in enumerate(((res_o, nat_o), (res_l, nat_l))):
            if dil == SUBLANE_STRIDE:
                for r in range(dil):
                    nat[p - 1, pl.ds(r, sub, stride=dil), :] = res[p, r * sub:(r + 1) * sub, :]
            else:
                coarse = seq // SUBLANE_STRIDE
                for r_lo in range(SUBLANE_STRIDE):
                    for r_hi in range(SUBLANE_STRIDE):
                        r = r_lo + SUBLANE_STRIDE * r_hi
                        stage[si, pl.ds(r_lo * coarse + r_hi, sub, stride=SUBLANE_STRIDE), :] = (
                            res[p, r * sub:(r + 1) * sub, :])
                for r_lo in range(SUBLANE_STRIDE):
                    nat[p - 1, pl.ds(r_lo, coarse, stride=SUBLANE_STRIDE), :] = (
                        stage[si, r_lo * coarse:(r_lo + 1) * coarse, :])

    rows_per_step = 256

    def merge_body(c, carry):
        rows = pl.ds(pl.multiple_of(c * rows_per_step, rows_per_step), rows_per_step)
        l0, l1, l2 = res_l[0, rows, :], nat_l[0, rows, :], nat_l[1, rows, :]
        top = jnp.maximum(jnp.maximum(l0, l1), l2)
        w0, w1, w2 = jnp.exp(l0 - top), jnp.exp(l1 - top), jnp.exp(l2 - top)
        num = w0 * res_o[0, rows, :] + w1 * nat_o[0, rows, :] + w2 * nat_o[1, rows, :]
        o_ref[0, rows, :] = (num / (w0 + w1 + w2)).astype(o_ref.dtype)
        return carry

    lax.fori_loop(0, seq // rows_per_step, merge_body, 0)


def _prompt_attention(q, k, v):
    B, S, _ = q.shape
    blk = pl.BlockSpec((1, S, 128), lambda b, hp: (b, 0, hp))
    return pl.pallas_call(
        functools.partial(_attn_kernel, seq=S),
        grid=(B, WIDTH // 128),
        in_specs=[blk, blk, blk],
        out_specs=blk,
        out_shape=jax.ShapeDtypeStruct((B, S, WIDTH), BF16),
        scratch_shapes=[
            pltpu.VMEM((S, 128), BF16),
            pltpu.VMEM((S + CHUNK, 128), BF16),
            pltpu.VMEM((S + CHUNK, 128), BF16),
            pltpu.VMEM((len(DILATIONS), S, 128), F32),
            pltpu.VMEM((len(DILATIONS), S, 128), F32),
            pltpu.VMEM((len(DILATIONS) - 1, S, 128), F32),
            pltpu.VMEM((len(DILATIONS) - 1, S, 128), F32),
            pltpu.VMEM((2, 2 * CHUNK, 2 * CHUNK), F32),
            pltpu.VMEM((3, S, 128), F32),
        ],
        compiler_params=pltpu.CompilerParams(
            dimension_semantics=("arbitrary", "arbitrary"), vmem_limit_bytes=V7X_VMEM_LIMIT),
        name="prompt_dilated_attention",
    )(q, k, v)


def _route(logits):
    lane = lax.broadcasted_iota(jnp.int32, logits.shape, 1)
    big = jnp.int32(ROUTER_LANES)
    lg = jnp.where(lane < N_GROUPS, logits, NEG)
    gmax = jnp.max(lg, axis=1, keepdims=True)
    gp = 1.0 / jnp.sum(jnp.exp(lg - gmax), axis=1, keepdims=True)
    gi = jnp.min(jnp.where(lg == gmax, lane, big), axis=1, keepdims=True)
    lo = N_GROUPS + EXPERTS_PER_GROUP * gi
    le = jnp.where((lane >= lo) & (lane < lo + EXPERTS_PER_GROUP), logits, NEG)
    m1 = jnp.max(le, axis=1, keepdims=True)
    i1 = jnp.min(jnp.where(le == m1, lane, big), axis=1, keepdims=True)
    le2 = jnp.where(lane == i1, NEG, le)
    m2 = jnp.max(le2, axis=1, keepdims=True)
    i2 = jnp.min(jnp.where(le2 == m2, lane, big), axis=1, keepdims=True)
    e2 = jnp.exp(m2 - m1)
    w1 = 1.0 / (1.0 + e2)
    w2 = e2 / (1.0 + e2)
    return jnp.where(lane == i1, gp * w1, jnp.where(lane == i2, gp * w2, 0.0))


def _mix_router_kernel(a_ref, b_ref, x_ref, wo_ref, g2_ref, wr_ref, br_ref, xp_ref, h2_ref, gates_ref, *, precise):
    mixed = (_dot(a_ref[...], wo_ref[0:WIDTH, :], precise)
             + _dot(b_ref[...], wo_ref[WIDTH:2 * WIDTH, :], precise))
    xp = x_ref[...] + mixed
    xp_ref[...] = xp
    h2 = _rmsnorm(xp, g2_ref[...])
    h2_ref[...] = h2.astype(h2_ref.dtype)
    logits = jnp.dot(h2, wr_ref[...], preferred_element_type=F32,
                     precision=lax.Precision.HIGHEST) + br_ref[...]
    gates_ref[...] = _route(logits)


def _mix_router(a, b, x, w_out, g2, w_router, b_router, *, tm, precise=False):
    n = x.shape[0]
    const = lambda i: (0, 0)
    row_blk = lambda w: pl.BlockSpec((tm, w), lambda i: (i, 0))
    return pl.pallas_call(
        functools.partial(_mix_router_kernel, precise=precise),
        grid=(n // tm,),
        in_specs=[row_blk(WIDTH), row_blk(WIDTH), row_blk(D_MODEL),
                  pl.BlockSpec((2 * WIDTH, D_MODEL), const),
                  pl.BlockSpec((1, D_MODEL), const),
                  pl.BlockSpec((D_MODEL, ROUTER_LANES), const),
                  pl.BlockSpec((1, ROUTER_LANES), const)],
        out_specs=[row_blk(D_MODEL), row_blk(D_MODEL), row_blk(ROUTER_LANES)],
        out_shape=[jax.ShapeDtypeStruct((n, D_MODEL), F32),
                   jax.ShapeDtypeStruct((n, D_MODEL), BF16),
                   jax.ShapeDtypeStruct((n, ROUTER_LANES), F32)],
        compiler_params=pltpu.CompilerParams(
            dimension_semantics=("arbitrary",), vmem_limit_bytes=V7X_VMEM_LIMIT),
        name="outproj_router",
    )(a, b, x, w_out, g2, w_router, b_router)


def _nt_dot(w, t):
    return lax.dot_general(w, t, (((1,), (1,)), ((), ())), preferred_element_type=F32)


def _route_t(logits_t):
    row = lax.broadcasted_iota(jnp.int32, logits_t.shape, 0)
    big = jnp.int32(ROUTER_LANES)
    lg = jnp.where(row < N_GROUPS, logits_t, NEG)
    gmax = jnp.max(lg, axis=0, keepdims=True)
    gp = 1.0 / jnp.sum(jnp.exp(lg - gmax), axis=0, keepdims=True)
    gi = jnp.min(jnp.where(lg == gmax, row, big), axis=0, keepdims=True)
    lo = N_GROUPS + EXPERTS_PER_GROUP * gi
    le = jnp.where((row >= lo) & (row < lo + EXPERTS_PER_GROUP), logits_t, NEG)
    m1 = jnp.max(le, axis=0, keepdims=True)
    i1 = jnp.min(jnp.where(le == m1, row, big), axis=0, keepdims=True)
    le2 = jnp.where(row == i1, NEG, le)
    m2 = jnp.max(le2, axis=0, keepdims=True)
    i2 = jnp.min(jnp.where(le2 == m2, row, big), axis=0, keepdims=True)
    e2 = jnp.exp(m2 - m1)
    return i1 - N_GROUPS, i2 - N_GROUPS, gp / (1.0 + e2), gp * e2 / (1.0 + e2)


def _mix_route_sort_kernel(a_ref, b_ref, x_ref, wo_ref, g2_ref, wrh_ref, wrl_ref, brc_ref, tri_ref, ltri_ref,
                           qkvt_ref, ck_ref, cv_ref,
                           xp_ref, h2_ref, mrow_ref, mcol_ref, tab_ref, so_ref, wob, *, n_seq, win):
    t = MOE_BLOCK

    @pl.when(pl.program_id(0) == 0)
    def _():
        wob[...] = wo_ref[...].astype(BF16)

    xp = x_ref[...] + _dot(a_ref[...], wob[0:WIDTH, :]) + _dot(b_ref[...], wob[WIDTH:2 * WIDTH, :])
    xp_ref[...] = xp
    h2 = _rmsnorm(xp, g2_ref[...])
    hi = h2.astype(BF16)
    h2_ref[...] = hi
    lo = (h2 - hi.astype(F32)).astype(BF16)
    logits_t = (_nt_dot(wrh_ref[...], hi) + _nt_dot(wrh_ref[...], lo) + _nt_dot(wrl_ref[...], hi)
                + brc_ref[...])
    ex1, ex2, gate1, gate2 = _route_t(logits_t[0:ROUTER_ROWS])

    pair_e = jnp.concatenate([ex1, ex2], axis=1)
    row = lax.broadcasted_iota(jnp.int32, (N_EXPERTS, 2 * t), 0)
    onehot = jnp.where(row == pair_e, 1.0, 0.0)
    cum = _dot(onehot, tri_ref[...])
    rank = jnp.sum(onehot * cum, axis=0, keepdims=True) - 1.0
    counts = cum[:, 2 * t - 1:2 * t]
    units32 = jnp.floor((counts + (MOE_ROW_ALIGN - 1)) * (1.0 / MOE_ROW_ALIGN))
    units = jnp.concatenate([jnp.broadcast_to(units32, (N_EXPERTS, 128)),
                             jnp.zeros((ROUTER_LANES - N_EXPERTS, 128), F32)], axis=0)
    off = _dot(ltri_ref[...], units) * MOE_ROW_ALIGN
    dst = jnp.sum(onehot * off[0:N_EXPERTS, 0:1], axis=0, keepdims=True) + rank

    r8 = lax.broadcasted_iota(jnp.int32, (8, t), 0)
    mrow_ref[0] = jnp.where(r8 == 0, dst[:, 0:t], jnp.where(r8 == 1, dst[:, t:2 * t],
                            jnp.where(r8 == 2, gate1, jnp.where(r8 == 3, gate2, 0.0))))
    r128 = lax.broadcasted_iota(jnp.int32, (ROUTER_LANES, t), 0)
    meta = jnp.where(r128 == 0, dst[:, 0:t], jnp.where(r128 == 1, dst[:, t:2 * t],
                     jnp.where(r128 == 2, gate1, jnp.where(r128 == 3, gate2, 0.0))))
    mcol_ref[...] = meta.T
    lane = lax.broadcasted_iota(jnp.int32, (ROUTER_LANES, 128), 1)
    n_rows = units * MOE_ROW_ALIGN
    chunks = jnp.floor((n_rows + (MOE_CHUNK - 1)) * (1.0 / MOE_CHUNK))
    tab_ref[0] = jnp.where(lane == 0, off, jnp.where(lane == 1, chunks, jnp.where(lane == 2, off + n_rows, 0.0)))

    _sample_attend(qkvt_ref, ck_ref, cv_ref, so_ref, jnp.minimum(pl.program_id(0), n_seq - 1), win)


def _mix_route_sort(a, b, x, w_out, g2, wr_hi, wr_lo, br_col, tri, ltri, qkvt, cache_k_t, cache_v_t):
    n = x.shape[0]
    t = MOE_BLOCK
    nblk = n // t
    n_seq, _, _, win = cache_k_t.shape
    assert n_seq <= nblk, "one sample sequence rides on each prompt block"
    const = lambda i: (0, 0)
    row_blk = lambda w: pl.BlockSpec((t, w), lambda i: (i, 0))
    seq_blk = lambda shape: pl.BlockSpec((1,) + shape, lambda i: (jnp.minimum(i, n_seq - 1),) + (0,) * len(shape))
    return pl.pallas_call(
        functools.partial(_mix_route_sort_kernel, n_seq=n_seq, win=win),
        grid=(nblk,),
        in_specs=[row_blk(WIDTH), row_blk(WIDTH), row_blk(D_MODEL),
                  pl.BlockSpec((2 * WIDTH, D_MODEL), const, pipeline_mode=pl.Buffered(1)),
                  pl.BlockSpec((1, D_MODEL), const),
                  pl.BlockSpec((ROUTER_LANES, D_MODEL), const),
                  pl.BlockSpec((ROUTER_LANES, D_MODEL), const),
                  pl.BlockSpec((ROUTER_LANES, 1), const),
                  pl.BlockSpec((2 * t, 2 * t), const),
                  pl.BlockSpec((ROUTER_LANES, ROUTER_LANES), const),
                  pl.BlockSpec(qkvt.shape, const),
                  seq_blk((N_HEADS, HEAD_DIM, win)), seq_blk((N_HEADS, HEAD_DIM, win))],
        out_specs=[row_blk(D_MODEL), row_blk(D_MODEL),
                   pl.BlockSpec((1, 8, t), lambda i: (i, 0, 0)),
                   row_blk(ROUTER_LANES),
                   pl.BlockSpec((1, ROUTER_LANES, 128), lambda i: (i, 0, 0)),
                   seq_blk((HEAD_DIM, N_HEADS))],
        out_shape=[jax.ShapeDtypeStruct((n, D_MODEL), F32),
                   jax.ShapeDtypeStruct((n, D_MODEL), BF16),
                   jax.ShapeDtypeStruct((nblk, 8, t), F32),
                   jax.ShapeDtypeStruct((n, ROUTER_LANES), F32),
                   jax.ShapeDtypeStruct((nblk, ROUTER_LANES, 128), F32),
                   jax.ShapeDtypeStruct((n_seq, HEAD_DIM, N_HEADS), F32)],
        scratch_shapes=[pltpu.VMEM((2 * WIDTH, D_MODEL), BF16)],
        compiler_params=pltpu.CompilerParams(
            dimension_semantics=("arbitrary",), vmem_limit_bytes=V7X_VMEM_LIMIT),
        name="outproj_route_sort",
    )(a, b, x, w_out, g2, wr_hi, wr_lo, br_col, tri, ltri, qkvt, cache_k_t, cache_v_t)


def _silu_mul(ab):
    a = ab[:, :D_EXPERT]
    return a * (1.0 / (1.0 + jnp.exp(-a))) * ab[:, D_EXPERT:]


def _chunk_offsets(tab, n_blk, e):
    return [pl.multiple_of(tab(j, e, 0), MOE_ROW_ALIGN) for j in range(n_blk)]


def _moe_gate_up_kernel(tab_ref, h_ref, mrow_ref, w1_ref, sh_ref, sg_ref, hs_ref, shid_ref, xs, *, flags_at):
    t = MOE_BLOCK
    n_blk = MOE_GATE_UP_BLOCKS
    first = pl.program_id(0) * n_blk
    tab = lambda j, e, c: tab_ref[((first + j) * N_EXPERTS + e) * 3 + c]

    piece = 512
    for j in range(n_blk):
        mrow = mrow_ref[j]
        dst1 = mrow[0:1, :].astype(jnp.int32)
        dst2 = mrow[1:2, :].astype(jnp.int32)
        h = h_ref[j * t:(j + 1) * t, :]
        for r0 in range(0, MOE_SORT_ROWS, piece):
            d_idx = lax.broadcasted_iota(jnp.int32, (piece, t), 0) + r0
            sel = jnp.where((d_idx == dst1) | (d_idx == dst2), 1.0, 0.0)
            xs[j, r0:r0 + piece, :] = _dot(sel, h).astype(BF16)
        xs[j, MOE_SORT_ROWS:MOE_ROWS, :] = jnp.zeros((MOE_CHUNK, D_MODEL), BF16)
    hs_ref[...] = jnp.zeros(hs_ref.shape, BF16)

    def first_chunks(g, carry):
        for i in range(MOE_EXPERTS_PER_STEP):
            e = g * MOE_EXPERTS_PER_STEP + i
            offs = _chunk_offsets(tab, n_blk, e)
            x = jnp.concatenate([xs[j, pl.ds(offs[j], MOE_CHUNK), :] for j in range(n_blk)], axis=0)
            hid = _silu_mul(jnp.dot(x, w1_ref[e], preferred_element_type=F32)).astype(BF16)
            for j in range(n_blk):
                hs_ref[j, pl.ds(offs[j], MOE_CHUNK), :] = hid[j * MOE_CHUNK:(j + 1) * MOE_CHUNK, :]
        return carry

    lax.fori_loop(0, N_EXPERTS // MOE_EXPERTS_PER_STEP, first_chunks, 0)

    def more_chunks(j, e, carry):
        off, n_chunks, end = tab(j, e, 0), tab(j, e, 1), tab(j, e, 2)

        def chunk(c, carry):
            r0 = pl.multiple_of(off + c * MOE_CHUNK, MOE_ROW_ALIGN)
            rows = r0 + lax.broadcasted_iota(jnp.int32, (MOE_CHUNK, D_EXPERT), 0)
            hid = _silu_mul(jnp.dot(xs[j, pl.ds(r0, MOE_CHUNK), :], w1_ref[e], preferred_element_type=F32))
            hs_ref[j, pl.ds(r0, MOE_CHUNK), :] = jnp.where(rows < end, hid.astype(BF16),
                                                           hs_ref[j, pl.ds(r0, MOE_CHUNK), :])
            return carry

        return lax.fori_loop(1, n_chunks, chunk, carry)

    for j in range(n_blk):
        @pl.when(tab_ref[flags_at + first + j] > 0)
        def _(j=j):
            lax.fori_loop(0, N_EXPERTS, functools.partial(more_chunks, j), 0)

    @pl.when(pl.program_id(0) == 0)
    def _():
        sh = sh_ref[...]
        gates = sg_ref[...]
        for e in range(N_EXPERTS):
            gate = gates[:, N_GROUPS + e:N_GROUPS + e + 1]
            hid = _silu_mul(jnp.dot(sh, w1_ref[e], preferred_element_type=F32)) * gate
            shid_ref[:, e * D_EXPERT:(e + 1) * D_EXPERT] = hid.astype(BF16)


def _moe_gate_up(tab, h2, mrow, w1_b, s_h, s_gates):
    n = h2.shape[0]
    n_s = s_h.shape[0]
    nblk = n // MOE_BLOCK
    g = MOE_GATE_UP_BLOCKS
    assert nblk % g == 0
    whole = lambda shape: pl.BlockSpec(shape, lambda i, tab: (0,) * len(shape))
    return pl.pallas_call(
        functools.partial(_moe_gate_up_kernel, flags_at=nblk * N_EXPERTS * 3),
        grid_spec=pltpu.PrefetchScalarGridSpec(
            num_scalar_prefetch=1,
            grid=(nblk // g,),
            in_specs=[pl.BlockSpec((g * MOE_BLOCK, D_MODEL), lambda i, tab: (i, 0)),
                      pl.BlockSpec((g, 8, MOE_BLOCK), lambda i, tab: (i, 0, 0)),
                      pl.BlockSpec(w1_b.shape, lambda i, tab: (0, 0, 0), pipeline_mode=pl.Buffered(1)),
                      whole(s_h.shape), whole(s_gates.shape)],
            out_specs=[pl.BlockSpec((g, MOE_ROWS, D_EXPERT), lambda i, tab: (i, 0, 0)),
                       whole((n_s, N_EXPERTS * D_EXPERT))],
            scratch_shapes=[pltpu.VMEM((g, MOE_ROWS, D_MODEL), BF16)]),
        out_shape=[jax.ShapeDtypeStruct((nblk, MOE_ROWS, D_EXPERT), BF16),
                   jax.ShapeDtypeStruct((n_s, N_EXPERTS * D_EXPERT), BF16)],
        compiler_params=pltpu.CompilerParams(
            dimension_semantics=("arbitrary",), vmem_limit_bytes=V7X_VMEM_LIMIT),
        name="moe_gate_up",
    )(tab, h2, mrow, w1_b, s_h, s_gates)


def _moe_down_kernel(tab_ref, hs_ref, xp_ref, mcol_ref, w2_ref, gf_ref, shid_ref, sx_ref, y_ref, sy_ref, os, *,
                     flags_at):
    t = MOE_BLOCK
    n_blk = MOE_DOWN_BLOCKS
    first = pl.program_id(0) * n_blk
    tab = lambda j, e, c: tab_ref[((first + j) * N_EXPERTS + e) * 3 + c]
    os[...] = jnp.zeros(os.shape, BF16)

    def first_chunks(g, carry):
        for i in range(MOE_EXPERTS_PER_STEP):
            e = g * MOE_EXPERTS_PER_STEP + i
            offs = _chunk_offsets(tab, n_blk, e)
            hid = jnp.concatenate([hs_ref[j, pl.ds(offs[j], MOE_CHUNK), :] for j in range(n_blk)], axis=0)
            out = jnp.dot(hid, w2_ref[e], preferred_element_type=F32).astype(BF16)
            for j in range(n_blk):
                os[j, pl.ds(offs[j], MOE_CHUNK), :] = out[j * MOE_CHUNK:(j + 1) * MOE_CHUNK, :]
        return carry

    lax.fori_loop(0, N_EXPERTS // MOE_EXPERTS_PER_STEP, first_chunks, 0)

    def more_chunks(j, e, carry):
        off, n_chunks, end = tab(j, e, 0), tab(j, e, 1), tab(j, e, 2)

        def chunk(c, carry):
            r0 = pl.multiple_of(off + c * MOE_CHUNK, MOE_ROW_ALIGN)
            rows = r0 + lax.broadcasted_iota(jnp.int32, (MOE_CHUNK, D_MODEL), 0)
            out = jnp.dot(hs_ref[j, pl.ds(r0, MOE_CHUNK), :], w2_ref[e], preferred_element_type=F32)
            os[j, pl.ds(r0, MOE_CHUNK), :] = jnp.where(rows < end, out.astype(BF16), os[j, pl.ds(r0, MOE_CHUNK), :])
            return carry

        return lax.fori_loop(1, n_chunks, chunk, carry)

    for j in range(n_blk):
        @pl.when(tab_ref[flags_at + first + j] > 0)
        def _(j=j):
            lax.fori_loop(0, N_EXPERTS, functools.partial(more_chunks, j), 0)

    l_idx = lax.broadcasted_iota(jnp.int32, (t, MOE_SORT_ROWS), 1)
    for j in range(n_blk):
        rows = slice(j * t, (j + 1) * t)
        mcol = mcol_ref[rows, :]
        d1c = mcol[:, 0:1].astype(jnp.int32)
        d2c = mcol[:, 1:2].astype(jnp.int32)
        comb = jnp.where(l_idx == d1c, mcol[:, 2:3], 0.0) + jnp.where(l_idx == d2c, mcol[:, 3:4], 0.0)
        y = xp_ref[rows, :] + _dot(comb, os[j, 0:MOE_SORT_ROWS, :])
        y_ref[rows, :] = _rmsnorm(y, gf_ref[...])

    @pl.when(pl.program_id(0) == 0)
    def _():
        w2_all = w2_ref[...].reshape(N_EXPERTS * D_EXPERT, D_MODEL)
        ys = sx_ref[...] + jnp.dot(shid_ref[...], w2_all, preferred_element_type=F32)
        sy_ref[...] = _rmsnorm(ys, gf_ref[...])


def _moe_down(tab, hs_sorted, xp, mcol, w2_b, gf, s_hid, s_x):
    n = xp.shape[0]
    n_s = s_x.shape[0]
    nblk = n // MOE_BLOCK
    g = MOE_DOWN_BLOCKS
    assert nblk % g == 0
    whole = lambda shape: pl.BlockSpec(shape, lambda i, tab: (0,) * len(shape))
    row_blk = lambda w: pl.BlockSpec((g * MOE_BLOCK, w), lambda i, tab: (i, 0))
    return pl.pallas_call(
        functools.partial(_moe_down_kernel, flags_at=nblk * N_EXPERTS * 3),
        grid_spec=pltpu.PrefetchScalarGridSpec(
            num_scalar_prefetch=1,
            grid=(nblk // g,),
            in_specs=[pl.BlockSpec((g, MOE_ROWS, D_EXPERT), lambda i, tab: (i, 0, 0)),
                      row_blk(D_MODEL), row_blk(ROUTER_LANES),
                      pl.BlockSpec(w2_b.shape, lambda i, tab: (0, 0, 0), pipeline_mode=pl.Buffered(1)),
                      whole((1, D_MODEL)), whole(s_hid.shape), whole(s_x.shape)],
            out_specs=[row_blk(D_MODEL), whole((n_s, D_MODEL))],
            scratch_shapes=[pltpu.VMEM((g, MOE_ROWS, D_MODEL), BF16)]),
        out_shape=[jax.ShapeDtypeStruct((n, D_MODEL), F32), jax.ShapeDtypeStruct((n_s, D_MODEL), F32)],
        compiler_params=pltpu.CompilerParams(
            dimension_semantics=("arbitrary",), vmem_limit_bytes=V7X_VMEM_LIMIT),
        name="moe_down_combine",
    )(tab, hs_sorted, xp, mcol, w2_b, gf, s_hid, s_x)


def _sample_proj_kernel(x_ref, g1_ref, w_ref, cos_ref, sin_ref, gv_ref, ones_ref, w00_ref, b0_ref,
                        rep_ref, foldt_ref, a_ref, k_ref, v_ref, vn_ref, qkvt_ref):
    h = _rmsnorm(x_ref[...], g1_ref[...])

    def proj(i):
        return _dot(h, w_ref[:, i * WIDTH:(i + 1) * WIDTH], precise=True)

    cos = _tile_lanes(cos_ref[...], WIDTH // 128)
    sin = _tile_lanes(sin_ref[...], WIDTH // 128)
    q = _rope(proj(2), cos, sin) * (HEAD_DIM ** -0.5)
    k = _rope(proj(3), cos, sin)
    v = proj(4)
    vn = _group_rmsnorm(proj(1), ones_ref[...], gv_ref[...], precise=True)
    a_ref[...] = proj(0) * (w00_ref[...] * vn + b0_ref[...])
    k_ref[...] = k
    v_ref[...] = v
    vn_ref[...] = vn

    n_rep = rep_ref.shape[0]
    r_idx = lax.broadcasted_iota(jnp.int32, (n_rep, WIDTH), 0)
    l_idx = lax.broadcasted_iota(jnp.int32, (n_rep, WIDTH), 1)
    own = (l_idx // HEAD_DIM) == (r_idx % N_HEADS)
    for t, src in enumerate((q, k, v)):
        rep = _dot(rep_ref[...], src, precise=True)
        qkvt_ref[t * HEAD_DIM:(t + 1) * HEAD_DIM, :] = lax.dot_general(
            foldt_ref[...], jnp.where(own, rep, 0.0), (((1,), (1,)), ((), ())),
            preferred_element_type=F32, precision=lax.Precision.HIGHEST)


def _sample_proj(x, g1, w_in, cos, sin, gv, ones_bd, w00, b0, rep, foldt):
    bd = x.shape[0]
    sds = lambda r, c: jax.ShapeDtypeStruct((r, c), F32)
    return pl.pallas_call(
        _sample_proj_kernel,
        out_shape=[sds(bd, WIDTH), sds(bd, WIDTH), sds(bd, WIDTH), sds(bd, WIDTH),
                   sds(3 * HEAD_DIM, bd * N_HEADS)],
        compiler_params=pltpu.CompilerParams(vmem_limit_bytes=V7X_VMEM_LIMIT),
        name="sample_proj",
    )(x, g1, w_in, cos, sin, gv, ones_bd, w00, b0, rep, foldt)


def _sample_attend(qkvt_ref, k_ref, v_ref, o_ref, b, win):
    n_col = qkvt_ref.shape[1]
    c_idx = lax.broadcasted_iota(jnp.int32, (n_col, 128), 0)
    l_idx = lax.broadcasted_iota(jnp.int32, (n_col, 128), 1)
    pick = jnp.where((c_idx == b * N_HEADS + l_idx) & (l_idx < N_HEADS), 1.0, 0.0)
    cols = _dot(qkvt_ref[...], pick, precise=True)
    dist = win - lax.broadcasted_iota(jnp.int32, (1, win), 1)
    members = [(dist <= N_KEYS * dil) & (dist % dil == 0) for dil in DILATIONS]

    head = lax.broadcasted_iota(jnp.int32, (N_HEADS, win), 0)
    s = jnp.zeros((N_HEADS, win), F32)
    s_self = jnp.zeros((N_HEADS, 1), F32)
    for h in range(N_HEADS):
        qc = cols[0:HEAD_DIM, h:h + 1]
        kc = cols[HEAD_DIM:2 * HEAD_DIM, h:h + 1]
        s_h = jnp.sum(k_ref[0, h] * qc, axis=0, keepdims=True)
        s = jnp.where(head == h, s_h, s)
        s_self = jnp.where(head[:, 0:1] == h, jnp.sum(qc * kc, axis=0, keepdims=True), s_self)

    es, e_selfs, dens, lses = [], [], [], []
    for mem in members:
        sm = jnp.where(mem, s, NEG)
        m = jnp.maximum(jnp.max(sm, axis=1, keepdims=True), s_self)
        e = jnp.exp(sm - m)
        e_self = jnp.exp(s_self - m)
        den = jnp.sum(e, axis=1, keepdims=True) + e_self
        es.append(e)
        e_selfs.append(e_self)
        dens.append(den)
        lses.append(m + jnp.log(den))
    top = jnp.maximum(jnp.maximum(lses[0], lses[1]), lses[2])
    ws = [jnp.exp(l - top) for l in lses]
    wsum = ws[0] + ws[1] + ws[2]
    coef = [w / (den * wsum) for w, den in zip(ws, dens)]
    p_keys = coef[0] * es[0] + coef[1] * es[1] + coef[2] * es[2]
    p_self = coef[0] * e_selfs[0] + coef[1] * e_selfs[1] + coef[2] * e_selfs[2]

    for h in range(N_HEADS):
        vc = cols[2 * HEAD_DIM:3 * HEAD_DIM, h:h + 1]
        o_ref[0, :, h:h + 1] = (jnp.sum(v_ref[0, h] * p_keys[h:h + 1, :], axis=1, keepdims=True)
                                + p_self[h:h + 1, :] * vc)


def _rope_tables(first_pos, count):
    half = HEAD_DIM // 2
    inv = ROPE_THETA ** (-np.arange(half, dtype=np.float64) * 2.0 / HEAD_DIM)
    ang = (first_pos + np.arange(count, dtype=np.float64))[:, None] * inv[None, :]
    cos, sin = np.cos(ang), np.sin(ang)
    cos128 = np.concatenate([cos, cos, cos, cos], axis=1).astype(np.float32)
    sin128 = np.concatenate([-sin, sin, -sin, sin], axis=1).astype(np.float32)
    return jnp.asarray(cos128), jnp.asarray(sin128)


def kernel(x_prompt, x_sample, cache_win_k, cache_win_v, ln1_g, w_in, sgu_norm_g, sgu_w, sgu_b, w_out, ln2_g,
           w_router_group, b_router_group, w_router_expert, b_router_expert, w_gate, w_up, w_down, lnf_g):
    depth = w_in.shape[0]
    assert depth == 1 and x_sample.shape[1] == 1
    B, S, _ = x_prompt.shape
    bd = x_sample.shape[0]
    win = cache_win_k.shape[2]
    assert S % (max(DILATIONS) * CHUNK) == 0 and win >= max(DILATIONS) * N_KEYS and PAST_LEN % CHUNK == 0
    l = 0

    w1_b =jnp.concatenate([w_gate[l], w_up[l]], axis=-1).astype(BF16)
    w2_b = w_down[l].astype(BF16)
    pad = ROUTER_LANES - N_GROUPS - N_EXPERTS
    w_router = jnp.pad(jnp.concatenate([w_router_group[l], w_router_expert[l]], axis=1), ((0, 0), (0, pad)))
    b_router = jnp.pad(jnp.concatenate([b_router_group[l], b_router_expert[l]]), (0, pad))[None, :]
    g1 = ln1_g[l][None, :]
    g2 = ln2_g[l][None, :]
    gf = lnf_g[None, :]
    gv = sgu_norm_g[l].reshape(1, WIDTH)
    grp = jnp.arange(WIDTH) // HEAD_DIM
    ones_bd = jnp.where(grp[:, None] == grp[None, :], 1.0 / HEAD_DIM, 0.0).astype(BF16)
    wp = jnp.concatenate([sgu_w[l][0::2], sgu_w[l][1::2]], axis=-1)
    bias = jnp.repeat(sgu_b[l].T, HEAD_DIM, axis=1)
    w00 = jnp.repeat(sgu_w[l][:, 0, 0], HEAD_DIM)[None, :]
    b0 = jnp.repeat(sgu_b[l][:, 0], HEAD_DIM)[None, :]

    cos_s, sin_s = _rope_tables(PAST_LEN, 1)
    rep = (jnp.arange(bd * N_HEADS)[:, None] // N_HEADS == jnp.arange(bd)[None, :]).astype(F32)
    foldt = (jnp.arange(HEAD_DIM)[:, None] == jnp.arange(WIDTH)[None, :] % HEAD_DIM).astype(F32)
    xs = x_sample.reshape(bd, D_MODEL)
    a_s, k_s, v_s, vn_s, qkvt = _sample_proj(xs, g1, w_in[l], cos_s, sin_s, gv, ones_bd, w00, b0, rep, foldt)
    to_pos_minor = lambda c: jnp.transpose(c, (0, 2, 3, 1))

    cos_p, sin_p = _rope_tables(0, S)
    a_p, q_p, k_p, v_p, kt_p, vt_p = _prompt_proj(x_prompt, g1, w_in[l], cos_p, sin_p, gv, ones_bd, wp, bias)
    b_p = _prompt_attention(q_p, k_p, v_p)
    n = B * S
    assert n % MOE_BLOCK == 0
    wr_t = w_router.T
    wr_hi = wr_t.astype(BF16)
    wr_lo = (wr_t - wr_hi.astype(F32)).astype(BF16)
    pair_idx = jnp.arange(2 * MOE_BLOCK)
    tri = (pair_idx[:, None] <= pair_idx[None, :]).astype(BF16)
    lane_idx = jnp.arange(ROUTER_LANES)
    ltri = (lane_idx[None, :] < lane_idx[:, None]).astype(BF16)
    xp2, h2, mrow, mcol, tab_f, o3 = _mix_route_sort(
        a_p.reshape(n, WIDTH), b_p.reshape(n, WIDTH), x_prompt.reshape(n, D_MODEL),
        w_out[l], g2, wr_hi, wr_lo, b_router.reshape(ROUTER_LANES, 1), tri, ltri,
        qkvt, to_pos_minor(cache_win_k[l]), to_pos_minor(cache_win_v[l]))
    tab_i = tab_f[:, :N_EXPERTS, 0:3].astype(jnp.int32)
    multi_chunk = (jnp.max(tab_i[:, :, 1], axis=1) > 1).astype(jnp.int32)
    tab = jnp.concatenate([tab_i.reshape(-1), multi_chunk])

    b_s = jnp.transpose(o3, (0, 2, 1)).reshape(bd, WIDTH)
    xs2, hs2, gates_s = _mix_router(a_s, b_s, xs, w_out[l], g2, w_router, b_router, tm=bd, precise=True)

    hid_sorted, hid_s = _moe_gate_up(tab, h2, mrow, w1_b, hs2, gates_s)
    y_prompt, y_sample = _moe_down(tab, hid_sorted, xp2, mcol, w2_b, gf, hid_s, xs2)
    y_prompt = y_prompt.reshape(B, S, D_MODEL)
    y_sample = y_sample.reshape(bd, 1, D_MODEL)
    buf_p = min(MAX_WINDOW, S)
    to_win = lambda t: jnp.transpose(t.reshape(1, B, N_HEADS, HEAD_DIM, buf_p), (0, 1, 4, 2, 3))
    new_k_p = to_win(kt_p)
    new_v_p = to_win(vt_p)

    shape_s = (1, bd, 1, N_HEADS, HEAD_DIM)
    return (y_prompt, y_sample, new_k_p, new_v_p,
            k_s.reshape(shape_s), v_s.reshape(shape_s), vn_s.reshape(shape_s))
```

```python
import functools

import jax
import jax.numpy as jnp
import numpy as np
from jax import lax
from jax.experimental import pallas as pl
from jax.experimental.pallas import tpu as pltpu

F32 = jnp.float32
BF16 = jnp.bfloat16

D_MODEL = 1024
HEAD_DIM = 64
N_HEADS = 8
WIDTH = N_HEADS * HEAD_DIM
PROJ_COLS = 5 * WIDTH
CHUNK = 128
DILATIONS = (1, 4, 16)
N_KEYS = 128
MAX_WINDOW = 2048
PAST_LEN = 16384
ROPE_THETA = 10000.0
N_GROUPS = 4
EXPERTS_PER_GROUP = 8
N_EXPERTS = N_GROUPS * EXPERTS_PER_GROUP
D_EXPERT = 128
EPS = 1e-6
NEG = -1e30
TILES_PER_STEP = 32
SUBLANE_STRIDE = 4
assert DILATIONS == (1, SUBLANE_STRIDE, SUBLANE_STRIDE ** 2)
MOE_BLOCK = 512
MOE_ROW_ALIGN = 16
MOE_CHUNK = 64
MOE_EXPERTS_PER_STEP = 16
MOE_GATE_UP_BLOCKS = 4
MOE_DOWN_BLOCKS = 2
MOE_SORT_ROWS = -(-(2 * MOE_BLOCK + N_EXPERTS * (MOE_ROW_ALIGN - 1)) // 512) * 512
MOE_ROWS = MOE_SORT_ROWS + MOE_CHUNK
ROUTER_ROWS = 48
ROUTER_LANES = 128
V7X_VMEM_LIMIT = 56 * 1024 * 1024


def _rmsnorm(x, g):
    return x * lax.rsqrt(jnp.mean(x * x, axis=-1, keepdims=True) + EPS) * g


def _tile_lanes(t, reps):
    return jnp.concatenate([t] * reps, axis=1)


def _rope(t, cos, sin_signed):
    lane = lax.broadcasted_iota(jnp.int32, t.shape, 1)
    first_half = (lane % HEAD_DIM) < (HEAD_DIM // 2)
    n = t.shape[1]
    partner = jnp.where(first_half, pltpu.roll(t, n - HEAD_DIM // 2, 1), pltpu.roll(t, HEAD_DIM // 2, 1))
    return t * cos + partner * sin_signed


def _dot(a, b, precise=False):
    if precise:
        return jnp.dot(a.astype(F32), b.astype(F32), preferred_element_type=F32,
                       precision=lax.Precision.HIGHEST)
    return jnp.dot(a.astype(BF16), b.astype(BF16), preferred_element_type=F32)


def _group_rmsnorm(va, ones_bd, gv, precise=False):
    ms = _dot(va * va, ones_bd, precise)
    return va * lax.rsqrt(ms + EPS) * gv


def _proj_kernel(x_ref, g1_ref, w_ref, cos_ref, sin_ref, gv_ref, ones_ref, wp_ref, bias_ref,
                 qkvt_ref, ck_ref, cv_ref,
                 a_ref, q_ref, k_ref, v_ref, kt_ref, vt_ref, so_ref, wb, *, tm, first_win_tile, n_seq, win):
    @pl.when((pl.program_id(0) == 0) & (pl.program_id(1) == 0))
    def _():
        wb[...] = w_ref[...].astype(BF16)

    s_seq = jnp.minimum(pl.program_id(0) * pl.num_programs(1) + pl.program_id(1), n_seq - 1)
    s_scores, s_self, s_vcols = _sample_scores(qkvt_ref, ck_ref, s_seq, win)

    h = _rmsnorm(x_ref[0], g1_ref[...]).astype(BF16)

    def proj(i):
        return jnp.dot(h, wb[:, i * WIDTH:(i + 1) * WIDTH], preferred_element_type=F32)

    cos = _tile_lanes(cos_ref[...], WIDTH // 128)
    sin = _tile_lanes(sin_ref[...], WIDTH // 128)
    q_ref[0] = _rope(proj(2), cos, sin) * (HEAD_DIM ** -0.5)
    k_ref[0] = _rope(proj(3), cos, sin)
    v_ref[0] = proj(4)

    u = proj(0)
    vn = _group_rmsnorm(proj(1), ones_ref[...], gv_ref[...]).astype(BF16)

    lane = lax.broadcasted_iota(jnp.int32, (CHUNK, 128), 1)
    left = lane < HEAD_DIM
    row = lax.broadcasted_iota(jnp.int32, (CHUNK, 2 * CHUNK), 0)
    col = lax.broadcasted_iota(jnp.int32, (CHUNK, 2 * CHUNK), 1)
    causal = (col % CHUNK) <= row
    zero = jnp.zeros((CHUNK, 128), BF16)
    wps = [jnp.where(causal, wp_ref[gp], 0.0).astype(BF16) for gp in range(N_HEADS // 2)]
    def block_diag(vv):
        return jnp.concatenate([jnp.where(left, vv, zero), jnp.where(left, zero, vv)], axis=0)

    for c in range(0, tm // CHUNK, 2):
        rows = [slice((c + i) * CHUNK, (c + i + 1) * CHUNK) for i in range(2)]
        mixes = [[], []]
        for gp in range(N_HEADS // 2):
            lanes = slice(gp * 128, (gp + 1) * 128)
            v2 = jnp.concatenate([block_diag(vn[rows[0], lanes]), block_diag(vn[rows[1], lanes])], axis=1)
            both = jnp.dot(wps[gp], v2, preferred_element_type=F32)
            mixes[0].append(both[:, 0:128])
            mixes[1].append(both[:, 128:256])
        for i in range(2):
            mix = jnp.concatenate(mixes[i], axis=1) + bias_ref[...]
            a_ref[0, rows[i], :] = (u[rows[i], :] * mix).astype(a_ref.dtype)

    _sample_finish(s_scores, s_self, s_vcols, cv_ref, so_ref, win)

    @pl.when(pl.program_id(1) >= first_win_tile)
    def _():
        kt_ref[0] = k_ref[0].T
        vt_ref[0] = v_ref[0].T


def _prompt_proj(x, g1, w_in_b, cos, sin, gv, ones_bd, wp, bias, qkvt, cache_k_t, cache_v_t, *, tm=512):
    B, S, _ = x.shape
    const2 = lambda b, j: (0, 0)
    out_sds = lambda dt: jax.ShapeDtypeStruct((B, S, WIDTH), dt)
    tile = pl.BlockSpec((1, tm, WIDTH), lambda b, j: (b, j, 0))
    win = min(MAX_WINDOW, S)
    first_win_tile = (S - win) // tm
    tile_t = pl.BlockSpec((1, WIDTH, tm), lambda b, j: (b, 0, jnp.maximum(j - first_win_tile, 0)))
    win_sds = jax.ShapeDtypeStruct((B, WIDTH, win), F32)
    n_seq, _, _, cache_win = cache_k_t.shape
    n_tiles = S // tm
    assert n_seq <= B * n_tiles, "one sample sequence rides on each grid step"
    seq_blk = lambda shape: pl.BlockSpec(
        (1,) + shape, lambda b, j: (jnp.minimum(b * n_tiles + j, n_seq - 1),) + (0,) * len(shape))
    cache_blk = seq_blk((N_HEADS, HEAD_DIM, cache_win))
    return pl.pallas_call(
        functools.partial(_proj_kernel, tm=tm, first_win_tile=first_win_tile, n_seq=n_seq, win=cache_win),
        grid=(B, n_tiles),
        in_specs=[
            pl.BlockSpec((1, tm, D_MODEL), lambda b, j: (b, j, 0)),
            pl.BlockSpec((1, D_MODEL), const2),
            pl.BlockSpec((D_MODEL, PROJ_COLS), const2, pipeline_mode=pl.Buffered(1)),
            pl.BlockSpec((tm, 128), lambda b, j: (j, 0)),
            pl.BlockSpec((tm, 128), lambda b, j: (j, 0)),
            pl.BlockSpec((1, WIDTH), const2),
            pl.BlockSpec((WIDTH, WIDTH), const2),
            pl.BlockSpec((N_HEADS // 2, CHUNK, 2 * CHUNK), lambda b, j: (0, 0, 0)),
            pl.BlockSpec((CHUNK, WIDTH), const2),
            pl.BlockSpec(qkvt.shape, const2), cache_blk, cache_blk,
        ],
        out_specs=[tile, tile, tile, tile, tile_t, tile_t, seq_blk((HEAD_DIM, N_HEADS))],
        out_shape=[out_sds(BF16), out_sds(F32), out_sds(F32), out_sds(F32), win_sds, win_sds,
                   jax.ShapeDtypeStruct((n_seq, HEAD_DIM, N_HEADS), F32)],
        scratch_shapes=[pltpu.VMEM((D_MODEL, PROJ_COLS), BF16)],
        compiler_params=pltpu.CompilerParams(
            dimension_semantics=("arbitrary", "arbitrary"), vmem_limit_bytes=V7X_VMEM_LIMIT),
        name="prompt_proj_sgu",
    )(x, g1, w_in_b, cos, sin, gv, ones_bd, wp, bias, qkvt, cache_k_t, cache_v_t)


def _attn_kernel(q_ref, k_ref, v_ref, o_ref, qd, kd, vd, res_o, res_l, nat_o, nat_l, bias, stage, *, seq):
    n_tiles = seq // CHUNK
    lane = lax.broadcasted_iota(jnp.int32, (CHUNK, 128), 1)
    left = lane < HEAD_DIM
    qi2 = lax.broadcasted_iota(jnp.int32, (2 * CHUNK, 2 * CHUNK), 0) % CHUNK
    kj2 = lax.broadcasted_iota(jnp.int32, (2 * CHUNK, 2 * CHUNK), 1)
    dist2 = CHUNK + qi2 - kj2
    band2 = (dist2 >= 0) & (dist2 <= N_KEYS)
    zero_q = jnp.zeros((CHUNK, 128), BF16)
    bias[0] = jnp.where(band2, 0.0, NEG)
    bias[1] = jnp.where(band2 & (kj2 >= CHUNK), 0.0, NEG)

    kd[0:CHUNK, :] = jnp.zeros((CHUNK, 128), BF16)
    vd[0:CHUNK, :] = jnp.zeros((CHUNK, 128), BF16)

    for p, dil in enumerate(DILATIONS):
        sub = seq // dil
        nb = sub // CHUNK
        for ti, (src_ref, dst, pad) in enumerate(((q_ref, qd, 0), (k_ref, kd, CHUNK), (v_ref, vd, CHUNK))):
            if dil == 1:
                dst[pad:pad + seq, :] = src_ref[0].astype(BF16)
            elif dil == SUBLANE_STRIDE:
                for r in range(dil):
                    val = src_ref[0, pl.ds(r, sub, stride=dil), :]
                    stage[ti, r * sub:(r + 1) * sub, :] = val
                    dst[pad + r * sub:pad + (r + 1) * sub, :] = val.astype(BF16)
            else:
                coarse = seq // SUBLANE_STRIDE
                for r_lo in range(SUBLANE_STRIDE):
                    for r_hi in range(SUBLANE_STRIDE):
                        r = r_lo + SUBLANE_STRIDE * r_hi
                        val = stage[ti, pl.ds(r_lo * coarse + r_hi, sub, stride=SUBLANE_STRIDE), :]
                        dst[pad + r * sub:pad + (r + 1) * sub, :] = val.astype(BF16)

        def tile_body(g, i, p=p, nb=nb):
            t = g * TILES_PER_STEP + i
            row = pl.multiple_of(t * CHUNK, CHUNK)
            qt = qd[pl.ds(row, CHUNK), :]
            k2 = kd[pl.ds(row, 2 * CHUNK), :]
            v2 = vd[pl.ds(row, 2 * CHUNK), :]
            if TILES_PER_STEP % nb == 0:
                variant = 1 if i % nb == 0 else 0
            elif i == 0:
                variant = jnp.where((g * TILES_PER_STEP) % nb == 0, 1, 0)
            else:
                variant = 0
            q2 = jnp.concatenate([jnp.where(left, qt, zero_q), jnp.where(left, zero_q, qt)], axis=0)
            s = lax.dot_general(q2, k2, (((1,), (1,)), ((), ())), preferred_element_type=F32)
            s = s + bias[variant]
            m = jnp.max(s, axis=1, keepdims=True)
            e = jnp.exp(s - m)
            den = jnp.sum(e, axis=1, keepdims=True)
            pv = jnp.dot(e.astype(BF16), v2, preferred_element_type=F32) / den
            lse = jnp.broadcast_to(m + jnp.log(den), (2 * CHUNK, 128))
            res_o[p, pl.ds(row, CHUNK), :] = jnp.where(left, pv[0:CHUNK], pv[CHUNK:2 * CHUNK])
            res_l[p, pl.ds(row, CHUNK), :] = jnp.where(left, lse[0:CHUNK], lse[CHUNK:2 * CHUNK])

        def group_body(g, carry, tile_body=tile_body):
            for i in range(TILES_PER_STEP):
                tile_body(g, i)
            return carry

        lax.fori_loop(0, n_tiles // TILES_PER_STEP, group_body, 0)

    for p, dil in enumerate(DILATIONS):
        if dil == 1:
            continue
        sub = seq // dil
        for si, (res, nat) in enumerate(((res_o, nat_o), (res_l, nat_l))):
            if dil == SUBLANE_STRIDE:
                for r in range(dil):
                    nat[p - 1, pl.ds(r, sub, stride=dil), :] = res[p, r * sub:(r + 1) * sub, :]
            else:
                coarse = seq // SUBLANE_STRIDE
                for r_lo in range(SUBLANE_STRIDE):
                    for r_hi in range(SUBLANE_STRIDE):
                        r = r_lo + SUBLANE_STRIDE * r_hi
                        stage[si, pl.ds(r_lo * coarse + r_hi, sub, stride=SUBLANE_STRIDE), :] = (
                            res[p, r * sub:(r + 1) * sub, :])
                for r_lo in range(SUBLANE_STRIDE):
                    nat[p - 1, pl.ds(r_lo, coarse, stride=SUBLANE_STRIDE), :] = (
                        stage[si, r_lo * coarse:(r_lo + 1) * coarse, :])

    rows_per_step = 256

    def merge_body(c, carry):
        rows = pl.ds(pl.multiple_of(c * rows_per_step, rows_per_step), rows_per_step)
        l0, l1, l2 = res_l[0, rows, :], nat_l[0, rows, :], nat_l[1, rows, :]
        top = jnp.maximum(jnp.maximum(l0, l1), l2)
        w0, w1, w2 = jnp.exp(l0 - top), jnp.exp(l1 - top), jnp.exp(l2 - top)
        num = w0 * res_o[0, rows, :] + w1 * nat_o[0, rows, :] + w2 * nat_o[1, rows, :]
        o_ref[0, rows, :] = (num / (w0 + w1 + w2)).astype(o_ref.dtype)
        return carry

    lax.fori_loop(0, seq // rows_per_step, merge_body, 0)


def _prompt_attention(q, k, v):
    B, S, _ = q.shape
    blk = pl.BlockSpec((1, S, 128), lambda b, hp: (b, 0, hp))
    return pl.pallas_call(
        functools.partial(_attn_kernel, seq=S),
        grid=(B, WIDTH // 128),
        in_specs=[blk, blk, blk],
        out_specs=blk,
        out_shape=jax.ShapeDtypeStruct((B, S, WIDTH), BF16),
        scratch_shapes=[
            pltpu.VMEM((S, 128), BF16),
            pltpu.VMEM((S + CHUNK, 128), BF16),
            pltpu.VMEM((S + CHUNK, 128), BF16),
            pltpu.VMEM((len(DILATIONS), S, 128), F32),
            pltpu.VMEM((len(DILATIONS), S, 128), F32),
            pltpu.VMEM((len(DILATIONS) - 1, S, 128), F32),
            pltpu.VMEM((len(DILATIONS) - 1, S, 128), F32),
            pltpu.VMEM((2, 2 * CHUNK, 2 * CHUNK), F32),
            pltpu.VMEM((3, S, 128), F32),
        ],
        compiler_params=pltpu.CompilerParams(
            dimension_semantics=("arbitrary", "arbitrary"), vmem_limit_bytes=V7X_VMEM_LIMIT),
        name="prompt_dilated_attention",
    )(q, k, v)


def _route(logits):
    lane = lax.broadcasted_iota(jnp.int32, logits.shape, 1)
    big = jnp.int32(ROUTER_LANES)
    lg = jnp.where(lane < N_GROUPS, logits, NEG)
    gmax = jnp.max(lg, axis=1, keepdims=True)
    gp = 1.0 / jnp.sum(jnp.exp(lg - gmax), axis=1, keepdims=True)
    gi = jnp.min(jnp.where(lg == gmax, lane, big), axis=1, keepdims=True)
    lo = N_GROUPS + EXPERTS_PER_GROUP * gi
    le = jnp.where((lane >= lo) & (lane < lo + EXPERTS_PER_GROUP), logits, NEG)
    m1 = jnp.max(le, axis=1, keepdims=True)
    i1 = jnp.min(jnp.where(le == m1, lane, big), axis=1, keepdims=True)
    le2 = jnp.where(lane == i1, NEG, le)
    m2 = jnp.max(le2, axis=1, keepdims=True)
    i2 = jnp.min(jnp.where(le2 == m2, lane, big), axis=1, keepdims=True)
    e2 = jnp.exp(m2 - m1)
    w1 = 1.0 / (1.0 + e2)
    w2 = e2 / (1.0 + e2)
    return jnp.where(lane == i1, gp * w1, jnp.where(lane == i2, gp * w2, 0.0))


def _mix_router_kernel(a_ref, b_ref, x_ref, wo_ref, g2_ref, wr_ref, br_ref, xp_ref, h2_ref, gates_ref, *, precise):
    mixed = (_dot(a_ref[...], wo_ref[0:WIDTH, :], precise)
             + _dot(b_ref[...], wo_ref[WIDTH:2 * WIDTH, :], precise))
    xp = x_ref[...] + mixed
    xp_ref[...] = xp
    h2 = _rmsnorm(xp, g2_ref[...])
    h2_ref[...] = h2.astype(h2_ref.dtype)
    logits = jnp.dot(h2, wr_ref[...], preferred_element_type=F32,
                     precision=lax.Precision.HIGHEST) + br_ref[...]
    gates_ref[...] = _route(logits)


def _mix_router(a, b, x, w_out, g2, w_router, b_router, *, tm, precise=False):
    n = x.shape[0]
    const = lambda i: (0, 0)
    row_blk = lambda w: pl.BlockSpec((tm, w), lambda i: (i, 0))
    return pl.pallas_call(
        functools.partial(_mix_router_kernel, precise=precise),
        grid=(n // tm,),
        in_specs=[row_blk(WIDTH), row_blk(WIDTH), row_blk(D_MODEL),
                  pl.BlockSpec((2 * WIDTH, D_MODEL), const),
                  pl.BlockSpec((1, D_MODEL), const),
                  pl.BlockSpec((D_MODEL, ROUTER_LANES), const),
                  pl.BlockSpec((1, ROUTER_LANES), const)],
        out_specs=[row_blk(D_MODEL), row_blk(D_MODEL), row_blk(ROUTER_LANES)],
        out_shape=[jax.ShapeDtypeStruct((n, D_MODEL), F32),
                   jax.ShapeDtypeStruct((n, D_MODEL), BF16),
                   jax.ShapeDtypeStruct((n, ROUTER_LANES), F32)],
        compiler_params=pltpu.CompilerParams(
            dimension_semantics=("arbitrary",), vmem_limit_bytes=V7X_VMEM_LIMIT),
        name="outproj_router",
    )(a, b, x, w_out, g2, w_router, b_router)


def _nt_dot(w, t):
    return lax.dot_general(w, t, (((1,), (1,)), ((), ())), preferred_element_type=F32)


def _route_t(logits_t):
    row = lax.broadcasted_iota(jnp.int32, logits_t.shape, 0)
    big = jnp.int32(ROUTER_LANES)
    lg = jnp.where(row < N_GROUPS, logits_t, NEG)
    gmax = jnp.max(lg, axis=0, keepdims=True)
    gp = 1.0 / jnp.sum(jnp.exp(lg - gmax), axis=0, keepdims=True)
    gi = jnp.min(jnp.where(lg == gmax, row, big), axis=0, keepdims=True)
    lo = N_GROUPS + EXPERTS_PER_GROUP * gi
    le = jnp.where((row >= lo) & (row < lo + EXPERTS_PER_GROUP), logits_t, NEG)
    m1 = jnp.max(le, axis=0, keepdims=True)
    i1 = jnp.min(jnp.where(le == m1, row, big), axis=0, keepdims=True)
    le2 = jnp.where(row == i1, NEG, le)
    m2 = jnp.max(le2, axis=0, keepdims=True)
    i2 = jnp.min(jnp.where(le2 == m2, row, big), axis=0, keepdims=True)
    e2 = jnp.exp(m2 - m1)
    return i1 - N_GROUPS, i2 - N_GROUPS, gp / (1.0 + e2), gp * e2 / (1.0 + e2)


def _mix_route_sort_kernel(a_ref, b_ref, x_ref, wo_ref, g2_ref, wr_ref, brc_ref, tri_ref, ltri_ref,
                           xp_ref, h2_ref, mrow_ref, mcol_ref, tab_ref, wob):
    t = MOE_BLOCK

    @pl.when(pl.program_id(0) == 0)
    def _():
        wob[...] = wo_ref[...].astype(BF16)

    xp = x_ref[...] + _dot(a_ref[...], wob[0:WIDTH, :]) + _dot(b_ref[...], wob[WIDTH:2 * WIDTH, :])
    xp_ref[...] = xp
    h2 = _rmsnorm(xp, g2_ref[...])
    hi = h2.astype(BF16)
    h2_ref[...] = hi
    lo = (h2 - hi.astype(F32)).astype(BF16)
    prod_hi = _nt_dot(wr_ref[...], hi)
    logits_t = (prod_hi[0:ROUTER_ROWS] + prod_hi[ROUTER_ROWS:2 * ROUTER_ROWS]
                + _nt_dot(wr_ref[0:ROUTER_ROWS, :], lo) + brc_ref[...])
    ex1, ex2, gate1, gate2 = _route_t(logits_t)

    pair_e = jnp.concatenate([ex1, ex2], axis=1)
    row = lax.broadcasted_iota(jnp.int32, (N_EXPERTS, 2 * t), 0)
    onehot = jnp.where(row == pair_e, 1.0, 0.0)
    n_lane_tiles = 2 * t // 128
    local = _dot(jnp.concatenate([onehot[:, k * 128:(k + 1) * 128] for k in range(n_lane_tiles)], axis=0),
                 tri_ref[...])
    carry = jnp.zeros((N_EXPERTS, 1), F32)
    cums = []
    for k in range(n_lane_tiles):
        tile = local[k * N_EXPERTS:(k + 1) * N_EXPERTS, :]
        cums.append(tile + carry)
        carry = carry + tile[:, 127:128]
    cum = jnp.concatenate(cums, axis=1)
    rank = jnp.sum(onehot * cum, axis=0, keepdims=True) - 1.0
    counts = carry
    units32 = jnp.floor((counts + (MOE_ROW_ALIGN - 1)) * (1.0 / MOE_ROW_ALIGN))
    units = jnp.concatenate([jnp.broadcast_to(units32, (N_EXPERTS, 128)),
                             jnp.zeros((ROUTER_LANES - N_EXPERTS, 128), F32)], axis=0)
    off = _dot(ltri_ref[...], units) * MOE_ROW_ALIGN
    dst = jnp.sum(onehot * off[0:N_EXPERTS, 0:1], axis=0, keepdims=True) + rank

    r8 = lax.broadcasted_iota(jnp.int32, (8, t), 0)
    mrow_ref[0] = jnp.where(r8 == 0, dst[:, 0:t], jnp.where(r8 == 1, dst[:, t:2 * t],
                            jnp.where(r8 == 2, gate1, jnp.where(r8 == 3, gate2, 0.0))))
    r128 = lax.broadcasted_iota(jnp.int32, (ROUTER_LANES, t), 0)
    meta = jnp.where(r128 == 0, dst[:, 0:t], jnp.where(r128 == 1, dst[:, t:2 * t],
                     jnp.where(r128 == 2, gate1, jnp.where(r128 == 3, gate2, 0.0))))
    mcol_ref[...] = meta.T
    lane = lax.broadcasted_iota(jnp.int32, (ROUTER_LANES, 128), 1)
    n_rows = units * MOE_ROW_ALIGN
    chunks = jnp.floor((n_rows + (MOE_CHUNK - 1)) * (1.0 / MOE_CHUNK))
    tab_ref[0] = jnp.where(lane == 0, off, jnp.where(lane == 1, chunks, jnp.where(lane == 2, off + n_rows, 0.0)))


def _mix_route_sort(a, b, x, w_out, g2, wr_hl, br_col, tri, ltri):
    n = x.shape[0]
    t = MOE_BLOCK
    nblk = n // t
    const = lambda i: (0, 0)
    row_blk = lambda w: pl.BlockSpec((t, w), lambda i: (i, 0))
    return pl.pallas_call(
        _mix_route_sort_kernel,
        grid=(nblk,),
        in_specs=[row_blk(WIDTH), row_blk(WIDTH), row_blk(D_MODEL),
                  pl.BlockSpec((2 * WIDTH, D_MODEL), const, pipeline_mode=pl.Buffered(1)),
                  pl.BlockSpec((1, D_MODEL), const),
                  pl.BlockSpec((2 * ROUTER_ROWS, D_MODEL), const),
                  pl.BlockSpec((ROUTER_ROWS, 1), const),
                  pl.BlockSpec((ROUTER_LANES, ROUTER_LANES), const),
                  pl.BlockSpec((ROUTER_LANES, ROUTER_LANES), const)],
        out_specs=[row_blk(D_MODEL), row_blk(D_MODEL),
                   pl.BlockSpec((1, 8, t), lambda i: (i, 0, 0)),
                   row_blk(ROUTER_LANES),
                   pl.BlockSpec((1, ROUTER_LANES, 128), lambda i: (i, 0, 0))],
        out_shape=[jax.ShapeDtypeStruct((n, D_MODEL), F32),
                   jax.ShapeDtypeStruct((n, D_MODEL), BF16),
                   jax.ShapeDtypeStruct((nblk, 8, t), F32),
                   jax.ShapeDtypeStruct((n, ROUTER_LANES), F32),
                   jax.ShapeDtypeStruct((nblk, ROUTER_LANES, 128), F32)],
        scratch_shapes=[pltpu.VMEM((2 * WIDTH, D_MODEL), BF16)],
        compiler_params=pltpu.CompilerParams(
            dimension_semantics=("arbitrary",), vmem_limit_bytes=V7X_VMEM_LIMIT),
        name="outproj_route_sort",
    )(a, b, x, w_out, g2, wr_hl, br_col, tri, ltri)


def _silu_mul(ab):
    a = ab[:, :D_EXPERT]
    return a * (1.0 / (1.0 + jnp.exp(-a))) * ab[:, D_EXPERT:]


def _chunk_offsets(tab, n_blk, e):
    return [pl.multiple_of(tab(j, e, 0), MOE_ROW_ALIGN) for j in range(n_blk)]


def _moe_gate_up_kernel(tab_ref, h_ref, mrow_ref, w1_ref, sh_ref, sg_ref, hs_ref, shid_ref, xs, *, flags_at):
    t = MOE_BLOCK
    n_blk = MOE_GATE_UP_BLOCKS
    first = pl.program_id(0) * n_blk
    tab = lambda j, e, c: tab_ref[((first + j) * N_EXPERTS + e) * 3 + c]

    piece = 512
    for j in range(n_blk):
        mrow = mrow_ref[j]
        dst1 = mrow[0:1, :].astype(jnp.int32)
        dst2 = mrow[1:2, :].astype(jnp.int32)
        h = h_ref[j * t:(j + 1) * t, :]
        for r0 in range(0, MOE_SORT_ROWS, piece):
            d_idx = lax.broadcasted_iota(jnp.int32, (piece, t), 0) + r0
            sel = jnp.where((d_idx == dst1) | (d_idx == dst2), 1.0, 0.0)
            xs[j, r0:r0 + piece, :] = _dot(sel, h).astype(BF16)
        xs[j, MOE_SORT_ROWS:MOE_ROWS, :] = jnp.zeros((MOE_CHUNK, D_MODEL), BF16)
    hs_ref[...] = jnp.zeros(hs_ref.shape, BF16)

    def first_chunks(g, carry):
        for i in range(MOE_EXPERTS_PER_STEP):
            e = g * MOE_EXPERTS_PER_STEP + i
            offs = _chunk_offsets(tab, n_blk, e)
            x = jnp.concatenate([xs[j, pl.ds(offs[j], MOE_CHUNK), :] for j in range(n_blk)], axis=0)
            hid = _silu_mul(jnp.dot(x, w1_ref[e], preferred_element_type=F32)).astype(BF16)
            for j in range(n_blk):
                hs_ref[j, pl.ds(offs[j], MOE_CHUNK), :] = hid[j * MOE_CHUNK:(j + 1) * MOE_CHUNK, :]
        return carry

    lax.fori_loop(0, N_EXPERTS // MOE_EXPERTS_PER_STEP, first_chunks, 0)

    def more_chunks(j, e, carry):
        off, n_chunks, end = tab(j, e, 0), tab(j, e, 1), tab(j, e, 2)

        def chunk(c, carry):
            r0 = pl.multiple_of(off + c * MOE_CHUNK, MOE_ROW_ALIGN)
            rows = r0 + lax.broadcasted_iota(jnp.int32, (MOE_CHUNK, D_EXPERT), 0)
            hid = _silu_mul(jnp.dot(xs[j, pl.ds(r0, MOE_CHUNK), :], w1_ref[e], preferred_element_type=F32))
            hs_ref[j, pl.ds(r0, MOE_CHUNK), :] = jnp.where(rows < end, hid.astype(BF16),
                                                           hs_ref[j, pl.ds(r0, MOE_CHUNK), :])
            return carry

        return lax.fori_loop(1, n_chunks, chunk, carry)

    for j in range(n_blk):
        @pl.when(tab_ref[flags_at + first + j] > 0)
        def _(j=j):
            lax.fori_loop(0, N_EXPERTS, functools.partial(more_chunks, j), 0)

    @pl.when(pl.program_id(0) == 0)
    def _():
        sh = sh_ref[...]
        gates = sg_ref[...]
        for e in range(N_EXPERTS):
            gate = gates[:, N_GROUPS + e:N_GROUPS + e + 1]
            hid = _silu_mul(jnp.dot(sh, w1_ref[e], preferred_element_type=F32)) * gate
            shid_ref[:, e * D_EXPERT:(e + 1) * D_EXPERT] = hid.astype(BF16)


def _moe_gate_up(tab, h2, mrow, w1_b, s_h, s_gates):
    n = h2.shape[0]
    n_s = s_h.shape[0]
    nblk = n // MOE_BLOCK
    g = MOE_GATE_UP_BLOCKS
    assert nblk % g == 0
    whole = lambda shape: pl.BlockSpec(shape, lambda i, tab: (0,) * len(shape))
    return pl.pallas_call(
        functools.partial(_moe_gate_up_kernel, flags_at=nblk * N_EXPERTS * 3),
        grid_spec=pltpu.PrefetchScalarGridSpec(
            num_scalar_prefetch=1,
            grid=(nblk // g,),
            in_specs=[pl.BlockSpec((g * MOE_BLOCK, D_MODEL), lambda i, tab: (i, 0)),
                      pl.BlockSpec((g, 8, MOE_BLOCK), lambda i, tab: (i, 0, 0)),
                      pl.BlockSpec(w1_b.shape, lambda i, tab: (0, 0, 0), pipeline_mode=pl.Buffered(1)),
                      whole(s_h.shape), whole(s_gates.shape)],
            out_specs=[pl.BlockSpec((g, MOE_ROWS, D_EXPERT), lambda i, tab: (i, 0, 0)),
                       whole((n_s, N_EXPERTS * D_EXPERT))],
            scratch_shapes=[pltpu.VMEM((g, MOE_ROWS, D_MODEL), BF16)]),
        out_shape=[jax.ShapeDtypeStruct((nblk, MOE_ROWS, D_EXPERT), BF16),
                   jax.ShapeDtypeStruct((n_s, N_EXPERTS * D_EXPERT), BF16)],
        compiler_params=pltpu.CompilerParams(
            dimension_semantics=("arbitrary",), vmem_limit_bytes=V7X_VMEM_LIMIT),
        name="moe_gate_up",
    )(tab, h2, mrow, w1_b, s_h, s_gates)


def _moe_down_kernel(tab_ref, hs_ref, xp_ref, mcol_ref, w2_ref, gf_ref, shid_ref, sx_ref, y_ref, sy_ref, os, *,
                     flags_at):
    t = MOE_BLOCK
    n_blk = MOE_DOWN_BLOCKS
    first = pl.program_id(0) * n_blk
    tab = lambda j, e, c: tab_ref[((first + j) * N_EXPERTS + e) * 3 + c]
    os[...] = jnp.zeros(os.shape, BF16)

    def first_chunks(g, carry):
        for i in range(MOE_EXPERTS_PER_STEP):
            e = g * MOE_EXPERTS_PER_STEP + i
            offs = _chunk_offsets(tab, n_blk, e)
            hid = jnp.concatenate([hs_ref[j, pl.ds(offs[j], MOE_CHUNK), :] for j in range(n_blk)], axis=0)
            out = jnp.dot(hid, w2_ref[e], preferred_element_type=F32).astype(BF16)
            for j in range(n_blk):
                os[j, pl.ds(offs[j], MOE_CHUNK), :] = out[j * MOE_CHUNK:(j + 1) * MOE_CHUNK, :]
        return carry

    lax.fori_loop(0, N_EXPERTS // MOE_EXPERTS_PER_STEP, first_chunks, 0)

    def more_chunks(j, e, carry):
        off, n_chunks, end = tab(j, e, 0), tab(j, e, 1), tab(j, e, 2)

        def chunk(c, carry):
            r0 = pl.multiple_of(off + c * MOE_CHUNK, MOE_ROW_ALIGN)
            rows = r0 + lax.broadcasted_iota(jnp.int32, (MOE_CHUNK, D_MODEL), 0)
            out = jnp.dot(hs_ref[j, pl.ds(r0, MOE_CHUNK), :], w2_ref[e], preferred_element_type=F32)
            os[j, pl.ds(r0, MOE_CHUNK), :] = jnp.where(rows < end, out.astype(BF16), os[j, pl.ds(r0, MOE_CHUNK), :])
            return carry

        return lax.fori_loop(1, n_chunks, chunk, carry)

    for j in range(n_blk):
        @pl.when(tab_ref[flags_at + first + j] > 0)
        def _(j=j):
            lax.fori_loop(0, N_EXPERTS, functools.partial(more_chunks, j), 0)

    l_idx = lax.broadcasted_iota(jnp.int32, (t, MOE_SORT_ROWS), 1)
    for j in range(n_blk):
        rows = slice(j * t, (j + 1) * t)
        mcol = mcol_ref[rows, :]
        d1c = mcol[:, 0:1].astype(jnp.int32)
        d2c = mcol[:, 1:2].astype(jnp.int32)
        comb = jnp.where(l_idx == d1c, mcol[:, 2:3], 0.0) + jnp.where(l_idx == d2c, mcol[:, 3:4], 0.0)
        y = xp_ref[rows, :] + _dot(comb, os[j, 0:MOE_SORT_ROWS, :])
        y_ref[rows, :] = _rmsnorm(y, gf_ref[...])

    @pl.when(pl.program_id(0) == 0)
    def _():
        w2_all = w2_ref[...].reshape(N_EXPERTS * D_EXPERT, D_MODEL)
        ys = sx_ref[...] + jnp.dot(shid_ref[...], w2_all, preferred_element_type=F32)
        sy_ref[...] = _rmsnorm(ys, gf_ref[...])


def _moe_down(tab, hs_sorted, xp, mcol, w2_b, gf, s_hid, s_x):
    n = xp.shape[0]
    n_s = s_x.shape[0]
    nblk = n // MOE_BLOCK
    g = MOE_DOWN_BLOCKS
    assert nblk % g == 0
    whole = lambda shape: pl.BlockSpec(shape, lambda i, tab: (0,) * len(shape))
    row_blk = lambda w: pl.BlockSpec((g * MOE_BLOCK, w), lambda i, tab: (i, 0))
    return pl.pallas_call(
        functools.partial(_moe_down_kernel, flags_at=nblk * N_EXPERTS * 3),
        grid_spec=pltpu.PrefetchScalarGridSpec(
            num_scalar_prefetch=1,
            grid=(nblk // g,),
            in_specs=[pl.BlockSpec((g, MOE_ROWS, D_EXPERT), lambda i, tab: (i, 0, 0)),
                      row_blk(D_MODEL), row_blk(ROUTER_LANES),
                      pl.BlockSpec(w2_b.shape, lambda i, tab: (0, 0, 0), pipeline_mode=pl.Buffered(1)),
                      whole((1, D_MODEL)), whole(s_hid.shape), whole(s_x.shape)],
            out_specs=[row_blk(D_MODEL), whole((n_s, D_MODEL))],
            scratch_shapes=[pltpu.VMEM((g, MOE_ROWS, D_MODEL), BF16)]),
        out_shape=[jax.ShapeDtypeStruct((n, D_MODEL), F32), jax.ShapeDtypeStruct((n_s, D_MODEL), F32)],
        compiler_params=pltpu.CompilerParams(
            dimension_semantics=("arbitrary",), vmem_limit_bytes=V7X_VMEM_LIMIT),
        name="moe_down_combine",
    )(tab, hs_sorted, xp, mcol, w2_b, gf, s_hid, s_x)


def _sample_proj_kernel(x_ref, g1_ref, w_ref, cos_ref, sin_ref, gv_ref, ones_ref, w00_ref, b0_ref,
                        rep_ref, foldt_ref, a_ref, k_ref, v_ref, vn_ref, qkvt_ref):
    h = _rmsnorm(x_ref[...], g1_ref[...])

    def proj(i):
        return _dot(h, w_ref[:, i * WIDTH:(i + 1) * WIDTH], precise=True)

    cos = _tile_lanes(cos_ref[...], WIDTH // 128)
    sin = _tile_lanes(sin_ref[...], WIDTH // 128)
    q = _rope(proj(2), cos, sin) * (HEAD_DIM ** -0.5)
    k = _rope(proj(3), cos, sin)
    v = proj(4)
    vn = _group_rmsnorm(proj(1), ones_ref[...], gv_ref[...], precise=True)
    a_ref[...] = proj(0) * (w00_ref[...] * vn + b0_ref[...])
    k_ref[...] = k
    v_ref[...] = v
    vn_ref[...] = vn

    n_rep = rep_ref.shape[0]
    r_idx = lax.broadcasted_iota(jnp.int32, (n_rep, WIDTH), 0)
    l_idx = lax.broadcasted_iota(jnp.int32, (n_rep, WIDTH), 1)
    own = (l_idx // HEAD_DIM) == (r_idx % N_HEADS)
    for t, src in enumerate((q, k, v)):
        rep = _dot(rep_ref[...], src, precise=True)
        qkvt_ref[t * HEAD_DIM:(t + 1) * HEAD_DIM, :] = lax.dot_general(
            foldt_ref[...], jnp.where(own, rep, 0.0), (((1,), (1,)), ((), ())),
            preferred_element_type=F32, precision=lax.Precision.HIGHEST)


def _sample_proj(x, g1, w_in, cos, sin, gv, ones_bd, w00, b0, rep, foldt):
    bd = x.shape[0]
    sds = lambda r, c: jax.ShapeDtypeStruct((r, c), F32)
    return pl.pallas_call(
        _sample_proj_kernel,
        out_shape=[sds(bd, WIDTH), sds(bd, WIDTH), sds(bd, WIDTH), sds(bd, WIDTH),
                   sds(3 * HEAD_DIM, bd * N_HEADS)],
        compiler_params=pltpu.CompilerParams(vmem_limit_bytes=V7X_VMEM_LIMIT),
        name="sample_proj",
    )(x, g1, w_in, cos, sin, gv, ones_bd, w00, b0, rep, foldt)


def _sample_scores(qkvt_ref, k_ref, b, win):
    n_col = qkvt_ref.shape[1]
    c_idx = lax.broadcasted_iota(jnp.int32, (n_col, 128), 0)
    l_idx = lax.broadcasted_iota(jnp.int32, (n_col, 128), 1)
    pick = jnp.where((c_idx == b * N_HEADS + l_idx) & (l_idx < N_HEADS), 1.0, 0.0)
    cols = _dot(qkvt_ref[...], pick, precise=True)

    head = lax.broadcasted_iota(jnp.int32, (N_HEADS, win), 0)
    s = jnp.zeros((N_HEADS, win), F32)
    s_self = jnp.zeros((N_HEADS, 1), F32)
    for h in range(N_HEADS):
        qc = cols[0:HEAD_DIM, h:h + 1]
        kc = cols[HEAD_DIM:2 * HEAD_DIM, h:h + 1]
        s_h = jnp.sum(k_ref[0, h] * qc, axis=0, keepdims=True)
        s = jnp.where(head == h, s_h, s)
        s_self = jnp.where(head[:, 0:1] == h, jnp.sum(qc * kc, axis=0, keepdims=True), s_self)
    return s, s_self, cols[2 * HEAD_DIM:3 * HEAD_DIM, :]


def _sample_finish(s, s_self, v_cols, v_ref, o_ref, win):
    dist = win - lax.broadcasted_iota(jnp.int32, (1, win), 1)
    members = [(dist <= N_KEYS * dil) & (dist % dil == 0) for dil in DILATIONS]
    es, e_selfs, dens, lses = [], [], [], []
    for mem in members:
        sm = jnp.where(mem, s, NEG)
        m = jnp.maximum(jnp.max(sm, axis=1, keepdims=True), s_self)
        e = jnp.exp(sm - m)
        e_self = jnp.exp(s_self - m)
        den = jnp.sum(e, axis=1, keepdims=True) + e_self
        es.append(e)
        e_selfs.append(e_self)
        dens.append(den)
        lses.append(m + jnp.log(den))
    top = jnp.maximum(jnp.maximum(lses[0], lses[1]), lses[2])
    ws = [jnp.exp(l - top) for l in lses]
    wsum = ws[0] + ws[1] + ws[2]
    coef = [w / (den * wsum) for w, den in zip(ws, dens)]
    p_keys = coef[0] * es[0] + coef[1] * es[1] + coef[2] * es[2]
    p_self = coef[0] * e_selfs[0] + coef[1] * e_selfs[1] + coef[2] * e_selfs[2]

    for h in range(N_HEADS):
        o_ref[0, :, h:h + 1] = (jnp.sum(v_ref[0, h] * p_keys[h:h + 1, :], axis=1, keepdims=True)
                                + p_self[h:h + 1, :] * v_cols[:, h:h + 1])


def _rope_tables(first_pos, count):
    half = HEAD_DIM // 2
    inv = ROPE_THETA ** (-np.arange(half, dtype=np.float64) * 2.0 / HEAD_DIM)
    ang = (first_pos + np.arange(count, dtype=np.float64))[:, None] * inv[None, :]
    cos, sin = np.cos(ang), np.sin(ang)
    cos128 = np.concatenate([cos, cos, cos, cos], axis=1).astype(np.float32)
    sin128 = np.concatenate([-sin, sin, -sin, sin], axis=1).astype(np.float32)
    return jnp.asarray(cos128), jnp.asarray(sin128)


def kernel(x_prompt, x_sample, cache_win_k, cache_win_v, ln1_g, w_in, sgu_norm_g, sgu_w, sgu_b, w_out, ln2_g,
           w_router_group, b_router_group, w_router_expert, b_router_expert, w_gate, w_up, w_down, lnf_g):
    depth = w_in.shape[0]
    assert depth == 1 and x_sample.shape[1] == 1
    B, S, _ = x_prompt.shape
    bd = x_sample.shape[0]
    win = cache_win_k.shape[2]
    assert S % (max(DILATIONS) * CHUNK) == 0 and win >= max(DILATIONS) * N_KEYS and PAST_LEN % CHUNK == 0
    l = 0

    w1_b =jnp.concatenate([w_gate[l], w_up[l]], axis=-1).astype(BF16)
    w2_b = w_down[l].astype(BF16)
    pad = ROUTER_LANES - N_GROUPS - N_EXPERTS
    w_router = jnp.pad(jnp.concatenate([w_router_group[l], w_router_expert[l]], axis=1), ((0, 0), (0, pad)))
    b_router = jnp.pad(jnp.concatenate([b_router_group[l], b_router_expert[l]]), (0, pad))[None, :]
    g1 = ln1_g[l][None, :]
    g2 = ln2_g[l][None, :]
    gf = lnf_g[None, :]
    gv = sgu_norm_g[l].reshape(1, WIDTH)
    grp = jnp.arange(WIDTH) // HEAD_DIM
    ones_bd = jnp.where(grp[:, None] == grp[None, :], 1.0 / HEAD_DIM, 0.0).astype(BF16)
    wp = jnp.concatenate([sgu_w[l][0::2], sgu_w[l][1::2]], axis=-1)
    bias = jnp.repeat(sgu_b[l].T, HEAD_DIM, axis=1)
    w00 = jnp.repeat(sgu_w[l][:, 0, 0], HEAD_DIM)[None, :]
    b0 = jnp.repeat(sgu_b[l][:, 0], HEAD_DIM)[None, :]

    cos_s, sin_s = _rope_tables(PAST_LEN, 1)
    rep = (jnp.arange(bd * N_HEADS)[:, None] // N_HEADS == jnp.arange(bd)[None, :]).astype(F32)
    foldt = (jnp.arange(HEAD_DIM)[:, None] == jnp.arange(WIDTH)[None, :] % HEAD_DIM).astype(F32)
    xs = x_sample.reshape(bd, D_MODEL)
    a_s, k_s, v_s, vn_s, qkvt = _sample_proj(xs, g1, w_in[l], cos_s, sin_s, gv, ones_bd, w00, b0, rep, foldt)
    to_pos_minor = lambda c: jnp.transpose(c, (0, 2, 3, 1))

    cos_p, sin_p = _rope_tables(0, S)
    a_p, q_p, k_p, v_p, kt_p, vt_p, o3 = _prompt_proj(
        x_prompt, g1, w_in[l], cos_p, sin_p, gv, ones_bd, wp, bias,
        qkvt, to_pos_minor(cache_win_k[l]), to_pos_minor(cache_win_v[l]))
    b_p = _prompt_attention(q_p, k_p, v_p)
    n = B * S
    assert n % MOE_BLOCK == 0
    wr_t = w_router.T[:ROUTER_ROWS]
    wr_hi = wr_t.astype(BF16)
    wr_hl = jnp.concatenate([wr_hi, (wr_t - wr_hi.astype(F32)).astype(BF16)], axis=0)
    lane_idx = jnp.arange(ROUTER_LANES)
    tri = (lane_idx[:, None] <= lane_idx[None, :]).astype(BF16)
    ltri = (lane_idx[None, :] < lane_idx[:, None]).astype(BF16)
    xp2, h2, mrow, mcol, tab_f = _mix_route_sort(
        a_p.reshape(n, WIDTH), b_p.reshape(n, WIDTH), x_prompt.reshape(n, D_MODEL),
        w_out[l], g2, wr_hl, b_router.reshape(ROUTER_LANES, 1)[:ROUTER_ROWS], tri, ltri)
    tab_i = tab_f[:, :N_EXPERTS, 0:3].astype(jnp.int32)
    multi_chunk = (jnp.max(tab_i[:, :, 1], axis=1) > 1).astype(jnp.int32)
    tab = jnp.concatenate([tab_i.reshape(-1), multi_chunk])

    b_s = jnp.transpose(o3, (0, 2, 1)).reshape(bd, WIDTH)
    xs2, hs2, gates_s = _mix_router(a_s, b_s, xs, w_out[l], g2, w_router, b_router, tm=bd, precise=True)

    hid_sorted, hid_s = _moe_gate_up(tab, h2, mrow, w1_b, hs2, gates_s)
    y_prompt, y_sample = _moe_down(tab, hid_sorted, xp2, mcol, w2_b, gf, hid_s, xs2)
    y_prompt = y_prompt.reshape(B, S, D_MODEL)
    y_sample = y_sample.reshape(bd, 1, D_MODEL)
    buf_p = min(MAX_WINDOW, S)
    to_win = lambda t: jnp.transpose(t.reshape(1, B, N_HEADS, HEAD_DIM, buf_p), (0, 1, 4, 2, 3))
    new_k_p = to_win(kt_p)
    new_v_p = to_win(vt_p)

    shape_s = (1, bd, 1, N_HEADS, HEAD_DIM)
    return (y_prompt, y_sample, new_k_p, new_v_p,
            k_s.reshape(shape_s), v_s.reshape(shape_s), vn_s.reshape(shape_s))
```

```python
import functools

import jax
import jax.numpy as jnp
import numpy as np
from jax import lax
from jax.experimental import pallas as pl
from jax.experimental.pallas import tpu as pltpu

F32 = jnp.float32
BF16 = jnp.bfloat16

D_MODEL = 1024
HEAD_DIM = 64
N_HEADS = 8
WIDTH = N_HEADS * HEAD_DIM
PROJ_COLS = 5 * WIDTH
CHUNK = 128
DILATIONS = (1, 4, 16)
N_KEYS = 128
MAX_WINDOW = 2048
PAST_LEN = 16384
ROPE_THETA = 10000.0
N_GROUPS = 4
EXPERTS_PER_GROUP = 8
N_EXPERTS = N_GROUPS * EXPERTS_PER_GROUP
D_EXPERT = 128
EPS = 1e-6
NEG = -1e30
TILES_PER_STEP = 32
SUBLANE_STRIDE = 4
assert DILATIONS == (1, SUBLANE_STRIDE, SUBLANE_STRIDE ** 2)
MOE_BLOCK = 512
MOE_ROW_ALIGN = 16
MOE_CHUNK = 64
MOE_EXPERTS_PER_STEP = 16
MOE_ROUTE_BLOCKS = 2
MOE_GATE_UP_BLOCKS = 4
MOE_DOWN_BLOCKS = 2
MOE_SORT_ROWS = -(-(2 * MOE_BLOCK + N_EXPERTS * (MOE_ROW_ALIGN - 1)) // 512) * 512
MOE_ROWS = MOE_SORT_ROWS + MOE_CHUNK
ROUTER_ROWS = 48
ROUTER_LANES = 128
V7X_VMEM_LIMIT = 56 * 1024 * 1024


def _rmsnorm(x, g):
    return x * lax.rsqrt(jnp.mean(x * x, axis=-1, keepdims=True) + EPS) * g


def _tile_lanes(t, reps):
    return jnp.concatenate([t] * reps, axis=1)


def _rope(t, cos, sin_signed):
    lane = lax.broadcasted_iota(jnp.int32, t.shape, 1)
    first_half = (lane % HEAD_DIM) < (HEAD_DIM // 2)
    n = t.shape[1]
    partner = jnp.where(first_half, pltpu.roll(t, n - HEAD_DIM // 2, 1), pltpu.roll(t, HEAD_DIM // 2, 1))
    return t * cos + partner * sin_signed


def _dot(a, b, precise=False):
    if precise:
        return jnp.dot(a.astype(F32), b.astype(F32), preferred_element_type=F32,
                       precision=lax.Precision.HIGHEST)
    return jnp.dot(a.astype(BF16), b.astype(BF16), preferred_element_type=F32)


def _group_rmsnorm(va, ones_bd, gv, precise=False):
    ms = _dot(va * va, ones_bd, precise)
    return va * lax.rsqrt(ms + EPS) * gv


def _proj_kernel(x_ref, g1_ref, w_ref, cos_ref, sin_ref, gv_ref, ones_ref, wp_ref, bias_ref,
                 qkvt_ref, ck_ref, cv_ref,
                 a_ref, q_ref, k_ref, v_ref, kt_ref, vt_ref, so_ref, wb, *, tm, first_win_tile, n_seq, win):
    @pl.when((pl.program_id(0) == 0) & (pl.program_id(1) == 0))
    def _():
        wb[...] = w_ref[...].astype(BF16)

    s_seq = jnp.minimum(pl.program_id(0) * pl.num_programs(1) + pl.program_id(1), n_seq - 1)
    s_scores, s_self, s_vcols = _sample_scores(qkvt_ref, ck_ref, s_seq, win)

    h = _rmsnorm(x_ref[0], g1_ref[...]).astype(BF16)

    def proj(i):
        return jnp.dot(h, wb[:, i * WIDTH:(i + 1) * WIDTH], preferred_element_type=F32)

    cos = _tile_lanes(cos_ref[...], WIDTH // 128)
    sin = _tile_lanes(sin_ref[...], WIDTH // 128)
    q_ref[0] = _rope(proj(2), cos, sin) * (HEAD_DIM ** -0.5)
    k_ref[0] = _rope(proj(3), cos, sin)
    v_ref[0] = proj(4)

    u = proj(0)
    vn = _group_rmsnorm(proj(1), ones_ref[...], gv_ref[...]).astype(BF16)

    lane = lax.broadcasted_iota(jnp.int32, (CHUNK, 128), 1)
    left = lane < HEAD_DIM
    row = lax.broadcasted_iota(jnp.int32, (CHUNK, 2 * CHUNK), 0)
    col = lax.broadcasted_iota(jnp.int32, (CHUNK, 2 * CHUNK), 1)
    causal = (col % CHUNK) <= row
    zero = jnp.zeros((CHUNK, 128), BF16)
    wps = [jnp.where(causal, wp_ref[gp], 0.0).astype(BF16) for gp in range(N_HEADS // 2)]
    def block_diag(vv):
        return jnp.concatenate([jnp.where(left, vv, zero), jnp.where(left, zero, vv)], axis=0)

    for c in range(0, tm // CHUNK, 2):
        rows = [slice((c + i) * CHUNK, (c + i + 1) * CHUNK) for i in range(2)]
        mixes = [[], []]
        for gp in range(N_HEADS // 2):
            lanes = slice(gp * 128, (gp + 1) * 128)
            v2 = jnp.concatenate([block_diag(vn[rows[0], lanes]), block_diag(vn[rows[1], lanes])], axis=1)
            both = jnp.dot(wps[gp], v2, preferred_element_type=F32)
            mixes[0].append(both[:, 0:128])
            mixes[1].append(both[:, 128:256])
        for i in range(2):
            mix = jnp.concatenate(mixes[i], axis=1) + bias_ref[...]
            a_ref[0, rows[i], :] = (u[rows[i], :] * mix).astype(a_ref.dtype)

    _sample_finish(s_scores, s_self, s_vcols, cv_ref, so_ref, win)

    @pl.when(pl.program_id(1) >= first_win_tile)
    def _():
        kt_ref[0] = k_ref[0].T
        vt_ref[0] = v_ref[0].T


def _prompt_proj(x, g1, w_in_b, cos, sin, gv, ones_bd, wp, bias, qkvt, cache_k_t, cache_v_t, *, tm=512):
    B, S, _ = x.shape
    const2 = lambda b, j: (0, 0)
    out_sds = lambda dt: jax.ShapeDtypeStruct((B, S, WIDTH), dt)
    tile = pl.BlockSpec((1, tm, WIDTH), lambda b, j: (b, j, 0))
    win = min(MAX_WINDOW, S)
    first_win_tile = (S - win) // tm
    tile_t = pl.BlockSpec((1, WIDTH, tm), lambda b, j: (b, 0, jnp.maximum(j - first_win_tile, 0)))
    win_sds = jax.ShapeDtypeStruct((B, WIDTH, win), F32)
    n_seq, _, _, cache_win = cache_k_t.shape
    n_tiles = S // tm
    assert n_seq <= B * n_tiles, "one sample sequence rides on each grid step"
    seq_blk = lambda shape: pl.BlockSpec(
        (1,) + shape, lambda b, j: (jnp.minimum(b * n_tiles + j, n_seq - 1),) + (0,) * len(shape))
    cache_blk = seq_blk((N_HEADS, HEAD_DIM, cache_win))
    return pl.pallas_call(
        functools.partial(_proj_kernel, tm=tm, first_win_tile=first_win_tile, n_seq=n_seq, win=cache_win),
        grid=(B, n_tiles),
        in_specs=[
            pl.BlockSpec((1, tm, D_MODEL), lambda b, j: (b, j, 0)),
            pl.BlockSpec((1, D_MODEL), const2),
            pl.BlockSpec((D_MODEL, PROJ_COLS), const2, pipeline_mode=pl.Buffered(1)),
            pl.BlockSpec((tm, 128), lambda b, j: (j, 0)),
            pl.BlockSpec((tm, 128), lambda b, j: (j, 0)),
            pl.BlockSpec((1, WIDTH), const2),
            pl.BlockSpec((WIDTH, WIDTH), const2),
            pl.BlockSpec((N_HEADS // 2, CHUNK, 2 * CHUNK), lambda b, j: (0, 0, 0)),
            pl.BlockSpec((CHUNK, WIDTH), const2),
            pl.BlockSpec(qkvt.shape, const2), cache_blk, cache_blk,
        ],
        out_specs=[tile, tile, tile, tile, tile_t, tile_t, seq_blk((HEAD_DIM, N_HEADS))],
        out_shape=[out_sds(BF16), out_sds(F32), out_sds(F32), out_sds(F32), win_sds, win_sds,
                   jax.ShapeDtypeStruct((n_seq, HEAD_DIM, N_HEADS), F32)],
        scratch_shapes=[pltpu.VMEM((D_MODEL, PROJ_COLS), BF16)],
        compiler_params=pltpu.CompilerParams(
            dimension_semantics=("arbitrary", "arbitrary"), vmem_limit_bytes=V7X_VMEM_LIMIT),
        name="prompt_proj_sgu",
    )(x, g1, w_in_b, cos, sin, gv, ones_bd, wp, bias, qkvt, cache_k_t, cache_v_t)


def _attn_kernel(q_ref, k_ref, v_ref, o_ref, qd, kd, vd, res_o, res_l, nat_o, nat_l, bias, stage, *, seq):
    n_tiles = seq // CHUNK
    lane = lax.broadcasted_iota(jnp.int32, (CHUNK, 128), 1)
    left = lane < HEAD_DIM
    qi2 = lax.broadcasted_iota(jnp.int32, (2 * CHUNK, 2 * CHUNK), 0) % CHUNK
    kj2 = lax.broadcasted_iota(jnp.int32, (2 * CHUNK, 2 * CHUNK), 1)
    dist2 = CHUNK + qi2 - kj2
    band2 = (dist2 >= 0) & (dist2 <= N_KEYS)
    zero_q = jnp.zeros((CHUNK, 128), BF16)
    bias[0] = jnp.where(band2, 0.0, NEG)
    bias[1] = jnp.where(band2 & (kj2 >= CHUNK), 0.0, NEG)

    kd[0:CHUNK, :] = jnp.zeros((CHUNK, 128), BF16)
    vd[0:CHUNK, :] = jnp.zeros((CHUNK, 128), BF16)

    for p, dil in enumerate(DILATIONS):
        sub = seq // dil
        nb = sub // CHUNK
        for ti, (src_ref, dst, pad) in enumerate(((q_ref, qd, 0), (k_ref, kd, CHUNK), (v_ref, vd, CHUNK))):
            if dil == 1:
                dst[pad:pad + seq, :] = src_ref[0].astype(BF16)
            elif dil == SUBLANE_STRIDE:
                for r in range(dil):
                    val = src_ref[0, pl.ds(r, sub, stride=dil), :]
                    stage[ti, r * sub:(r + 1) * sub, :] = val
                    dst[pad + r * sub:pad + (r + 1) * sub, :] = val.astype(BF16)
            else:
                coarse = seq // SUBLANE_STRIDE
                for r_lo in range(SUBLANE_STRIDE):
                    for r_hi in range(SUBLANE_STRIDE):
                        r = r_lo + SUBLANE_STRIDE * r_hi
                        val = stage[ti, pl.ds(r_lo * coarse + r_hi, sub, stride=SUBLANE_STRIDE), :]
                        dst[pad + r * sub:pad + (r + 1) * sub, :] = val.astype(BF16)

        def tile_body(g, i, p=p, nb=nb):
            t = g * TILES_PER_STEP + i
            row = pl.multiple_of(t * CHUNK, CHUNK)
            qt = qd[pl.ds(row, CHUNK), :]
            k2 = kd[pl.ds(row, 2 * CHUNK), :]
            v2 = vd[pl.ds(row, 2 * CHUNK), :]
            if TILES_PER_STEP % nb == 0:
                variant = 1 if i % nb == 0 else 0
            elif i == 0:
                variant = jnp.where((g * TILES_PER_STEP) % nb == 0, 1, 0)
            else:
                variant = 0
            q2 = jnp.concatenate([jnp.where(left, qt, zero_q), jnp.where(left, zero_q, qt)], axis=0)
            s = lax.dot_general(q2, k2, (((1,), (1,)), ((), ())), preferred_element_type=F32)
            s = s + bias[variant]
            m = jnp.max(s, axis=1, keepdims=True)
            e = jnp.exp(s - m)
            den = jnp.sum(e, axis=1, keepdims=True)
            pv = jnp.dot(e.astype(BF16), v2, preferred_element_type=F32) / den
            lse = jnp.broadcast_to(m + jnp.log(den), (2 * CHUNK, 128))
            res_o[p, pl.ds(row, CHUNK), :] = jnp.where(left, pv[0:CHUNK], pv[CHUNK:2 * CHUNK])
            res_l[p, pl.ds(row, CHUNK), :] = jnp.where(left, lse[0:CHUNK], lse[CHUNK:2 * CHUNK])

        def group_body(g, carry, tile_body=tile_body):
            for i in range(TILES_PER_STEP):
                tile_body(g, i)
            return carry

        lax.fori_loop(0, n_tiles // TILES_PER_STEP, group_body, 0)

    for p, dil in enumerate(DILATIONS):
        if dil == 1:
            continue
        sub = seq // dil
        for si, (res, nat) in enumerate(((res_o, nat_o), (res_l, nat_l))):
            if dil == SUBLANE_STRIDE:
                for r in range(dil):
                    nat[p - 1, pl.ds(r, sub, stride=dil), :] = res[p, r * sub:(r + 1) * sub, :]
            else:
                coarse = seq // SUBLANE_STRIDE
                for r_lo in range(SUBLANE_STRIDE):
                    for r_hi in range(SUBLANE_STRIDE):
                        r = r_lo + SUBLANE_STRIDE * r_hi
                        stage[si, pl.ds(r_lo * coarse + r_hi, sub, stride=SUBLANE_STRIDE), :] = (
                            res[p, r * sub:(r + 1) * sub, :])
                for r_lo in range(SUBLANE_STRIDE):
                    nat[p - 1, pl.ds(r_lo, coarse, stride=SUBLANE_STRIDE), :] = (
                        stage[si, r_lo * coarse:(r_lo + 1) * coarse, :])

    rows_per_step = 256

    def merge_body(c, carry):
        rows = pl.ds(pl.multiple_of(c * rows_per_step, rows_per_step), rows_per_step)
        l0, l1, l2 = res_l[0, rows, :], nat_l[0, rows, :], nat_l[1, rows, :]
        top = jnp.maximum(jnp.maximum(l0, l1), l2)
        w0, w1, w2 = jnp.exp(l0 - top), jnp.exp(l1 - top), jnp.exp(l2 - top)
        num = w0 * res_o[0, rows, :] + w1 * nat_o[0, rows, :] + w2 * nat_o[1, rows, :]
        o_ref[0, rows, :] = (num / (w0 + w1 + w2)).astype(o_ref.dtype)
        return carry

    lax.fori_loop(0, seq // rows_per_step, merge_body, 0)


def _prompt_attention(q, k, v):
    B, S, _ = q.shape
    blk = pl.BlockSpec((1, S, 128), lambda b, hp: (b, 0, hp))
    return pl.pallas_call(
        functools.partial(_attn_kernel, seq=S),
        grid=(B, WIDTH // 128),
        in_specs=[blk, blk, blk],
        out_specs=blk,
        out_shape=jax.ShapeDtypeStruct((B, S, WIDTH), BF16),
        scratch_shapes=[
            pltpu.VMEM((S, 128), BF16),
            pltpu.VMEM((S + CHUNK, 128), BF16),
            pltpu.VMEM((S + CHUNK, 128), BF16),
            pltpu.VMEM((len(DILATIONS), S, 128), F32),
            pltpu.VMEM((len(DILATIONS), S, 128), F32),
            pltpu.VMEM((len(DILATIONS) - 1, S, 128), F32),
            pltpu.VMEM((len(DILATIONS) - 1, S, 128), F32),
            pltpu.VMEM((2, 2 * CHUNK, 2 * CHUNK), F32),
            pltpu.VMEM((3, S, 128), F32),
        ],
        compiler_params=pltpu.CompilerParams(
            dimension_semantics=("arbitrary", "arbitrary"), vmem_limit_bytes=V7X_VMEM_LIMIT),
        name="prompt_dilated_attention",
    )(q, k, v)


def _route(logits):
    lane = lax.broadcasted_iota(jnp.int32, logits.shape, 1)
    big = jnp.int32(ROUTER_LANES)
    lg = jnp.where(lane < N_GROUPS, logits, NEG)
    gmax = jnp.max(lg, axis=1, keepdims=True)
    gp = 1.0 / jnp.sum(jnp.exp(lg - gmax), axis=1, keepdims=True)
    gi = jnp.min(jnp.where(lg == gmax, lane, big), axis=1, keepdims=True)
    lo = N_GROUPS + EXPERTS_PER_GROUP * gi
    le = jnp.where((lane >= lo) & (lane < lo + EXPERTS_PER_GROUP), logits, NEG)
    m1 = jnp.max(le, axis=1, keepdims=True)
    i1 = jnp.min(jnp.where(le == m1, lane, big), axis=1, keepdims=True)
    le2 = jnp.where(lane == i1, NEG, le)
    m2 = jnp.max(le2, axis=1, keepdims=True)
    i2 = jnp.min(jnp.where(le2 == m2, lane, big), axis=1, keepdims=True)
    e2 = jnp.exp(m2 - m1)
    w1 = 1.0 / (1.0 + e2)
    w2 = e2 / (1.0 + e2)
    return jnp.where(lane == i1, gp * w1, jnp.where(lane == i2, gp * w2, 0.0))


def _mix_router_kernel(a_ref, b_ref, x_ref, wo_ref, g2_ref, wr_ref, br_ref, xp_ref, h2_ref, gates_ref, *, precise):
    mixed = (_dot(a_ref[...], wo_ref[0:WIDTH, :], precise)
             + _dot(b_ref[...], wo_ref[WIDTH:2 * WIDTH, :], precise))
    xp = x_ref[...] + mixed
    xp_ref[...] = xp
    h2 = _rmsnorm(xp, g2_ref[...])
    h2_ref[...] = h2.astype(h2_ref.dtype)
    logits = jnp.dot(h2, wr_ref[...], preferred_element_type=F32,
                     precision=lax.Precision.HIGHEST) + br_ref[...]
    gates_ref[...] = _route(logits)


def _mix_router(a, b, x, w_out, g2, w_router, b_router, *, tm, precise=False):
    n = x.shape[0]
    const = lambda i: (0, 0)
    row_blk = lambda w: pl.BlockSpec((tm, w), lambda i: (i, 0))
    return pl.pallas_call(
        functools.partial(_mix_router_kernel, precise=precise),
        grid=(n // tm,),
        in_specs=[row_blk(WIDTH), row_blk(WIDTH), row_blk(D_MODEL),
                  pl.BlockSpec((2 * WIDTH, D_MODEL), const),
                  pl.BlockSpec((1, D_MODEL), const),
                  pl.BlockSpec((D_MODEL, ROUTER_LANES), const),
                  pl.BlockSpec((1, ROUTER_LANES), const)],
        out_specs=[row_blk(D_MODEL), row_blk(D_MODEL), row_blk(ROUTER_LANES)],
        out_shape=[jax.ShapeDtypeStruct((n, D_MODEL), F32),
                   jax.ShapeDtypeStruct((n, D_MODEL), BF16),
                   jax.ShapeDtypeStruct((n, ROUTER_LANES), F32)],
        compiler_params=pltpu.CompilerParams(
            dimension_semantics=("arbitrary",), vmem_limit_bytes=V7X_VMEM_LIMIT),
        name="outproj_router",
    )(a, b, x, w_out, g2, w_router, b_router)


def _nt_dot(w, t):
    return lax.dot_general(w, t, (((1,), (1,)), ((), ())), preferred_element_type=F32)


def _route_t(logits_t):
    row = lax.broadcasted_iota(jnp.int32, logits_t.shape, 0)
    big = jnp.int32(ROUTER_LANES)
    lg = jnp.where(row < N_GROUPS, logits_t, NEG)
    gmax = jnp.max(lg, axis=0, keepdims=True)
    gp = 1.0 / jnp.sum(jnp.exp(lg - gmax), axis=0, keepdims=True)
    gi = jnp.min(jnp.where(lg == gmax, row, big), axis=0, keepdims=True)
    lo = N_GROUPS + EXPERTS_PER_GROUP * gi
    le = jnp.where((row >= lo) & (row < lo + EXPERTS_PER_GROUP), logits_t, NEG)
    m1 = jnp.max(le, axis=0, keepdims=True)
    i1 = jnp.min(jnp.where(le == m1, row, big), axis=0, keepdims=True)
    le2 = jnp.where(row == i1, NEG, le)
    m2 = jnp.max(le2, axis=0, keepdims=True)
    i2 = jnp.min(jnp.where(le2 == m2, row, big), axis=0, keepdims=True)
    e2 = jnp.exp(m2 - m1)
    return i1 - N_GROUPS, i2 - N_GROUPS, gp / (1.0 + e2), gp * e2 / (1.0 + e2)


def _mix_route_sort_kernel(a_ref, b_ref, x_ref, wo_ref, g2_ref, wr_ref, brc_ref, tri_ref, ltri_ref,
                           xp_ref, h2_ref, mrow_ref, mcol_ref, tab_ref, wob):
    t = MOE_BLOCK

    @pl.when(pl.program_id(0) == 0)
    def _():
        wob[...] = wo_ref[...].astype(BF16)

    def project(j):
        rows = slice(j * t, (j + 1) * t)
        xp = x_ref[rows, :] + _dot(a_ref[rows, :], wob[0:WIDTH, :]) + _dot(b_ref[rows, :], wob[WIDTH:2 * WIDTH, :])
        xp_ref[rows, :] = xp
        h2 = _rmsnorm(xp, g2_ref[...])
        hi = h2.astype(BF16)
        h2_ref[rows, :] = hi
        return hi, (h2 - hi.astype(F32)).astype(BF16)

    def route(hi, lo):
        prod_hi = _nt_dot(wr_ref[...], hi)
        logits_t = (prod_hi[0:ROUTER_ROWS] + prod_hi[ROUTER_ROWS:2 * ROUTER_ROWS]
                    + _nt_dot(wr_ref[0:ROUTER_ROWS, :], lo) + brc_ref[...])
        return _route_t(logits_t)

    def sort_meta(j, ex1, ex2, gate1, gate2):
        rows = slice(j * t, (j + 1) * t)
        pair_e = jnp.concatenate([ex1, ex2], axis=1)
        row = lax.broadcasted_iota(jnp.int32, (N_EXPERTS, 2 * t), 0)
        onehot = jnp.where(row == pair_e, 1.0, 0.0)
        n_lane_tiles = 2 * t // 128
        local = _dot(jnp.concatenate([onehot[:, k * 128:(k + 1) * 128] for k in range(n_lane_tiles)], axis=0),
                     tri_ref[...])
        carry = jnp.zeros((N_EXPERTS, 1), F32)
        cums = []
        for k in range(n_lane_tiles):
            tile = local[k * N_EXPERTS:(k + 1) * N_EXPERTS, :]
            cums.append(tile + carry)
            carry = carry + tile[:, 127:128]
        cum = jnp.concatenate(cums, axis=1)
        rank = jnp.sum(onehot * cum, axis=0, keepdims=True) - 1.0
        counts = carry
        units32 = jnp.floor((counts + (MOE_ROW_ALIGN - 1)) * (1.0 / MOE_ROW_ALIGN))
        units = jnp.concatenate([jnp.broadcast_to(units32, (N_EXPERTS, 128)),
                                 jnp.zeros((ROUTER_LANES - N_EXPERTS, 128), F32)], axis=0)
        off = _dot(ltri_ref[...], units) * MOE_ROW_ALIGN
        dst = jnp.sum(onehot * off[0:N_EXPERTS, 0:1], axis=0, keepdims=True) + rank

        r8 = lax.broadcasted_iota(jnp.int32, (8, t), 0)
        mrow_ref[j] = jnp.where(r8 == 0, dst[:, 0:t], jnp.where(r8 == 1, dst[:, t:2 * t],
                                jnp.where(r8 == 2, gate1, jnp.where(r8 == 3, gate2, 0.0))))
        r128 = lax.broadcasted_iota(jnp.int32, (ROUTER_LANES, t), 0)
        meta = jnp.where(r128 == 0, dst[:, 0:t], jnp.where(r128 == 1, dst[:, t:2 * t],
                         jnp.where(r128 == 2, gate1, jnp.where(r128 == 3, gate2, 0.0))))
        mcol_ref[rows, :] = meta.T
        lane = lax.broadcasted_iota(jnp.int32, (ROUTER_LANES, 128), 1)
        n_rows = units * MOE_ROW_ALIGN
        chunks = jnp.floor((n_rows + (MOE_CHUNK - 1)) * (1.0 / MOE_CHUNK))
        tab_ref[j] = jnp.where(lane == 0, off, jnp.where(lane == 1, chunks, jnp.where(lane == 2, off + n_rows, 0.0)))

    blocks = range(MOE_ROUTE_BLOCKS)
    projected = [project(j) for j in blocks]
    routed = [route(hi, lo) for hi, lo in projected]
    for j in blocks:
        sort_meta(j, *routed[j])


def _mix_route_sort(a, b, x, w_out, g2, wr_hl, br_col, tri, ltri):
    n = x.shape[0]
    t = MOE_BLOCK
    nblk = n // t
    g = MOE_ROUTE_BLOCKS
    assert nblk % g == 0
    const = lambda i: (0, 0)
    row_blk = lambda w: pl.BlockSpec((g * t, w), lambda i: (i, 0))
    return pl.pallas_call(
        _mix_route_sort_kernel,
        grid=(nblk // g,),
        in_specs=[row_blk(WIDTH), row_blk(WIDTH), row_blk(D_MODEL),
                  pl.BlockSpec((2 * WIDTH, D_MODEL), const, pipeline_mode=pl.Buffered(1)),
                  pl.BlockSpec((1, D_MODEL), const),
                  pl.BlockSpec((2 * ROUTER_ROWS, D_MODEL), const),
                  pl.BlockSpec((ROUTER_ROWS, 1), const),
                  pl.BlockSpec((ROUTER_LANES, ROUTER_LANES), const),
                  pl.BlockSpec((ROUTER_LANES, ROUTER_LANES), const)],
        out_specs=[row_blk(D_MODEL), row_blk(D_MODEL),
                   pl.BlockSpec((g, 8, t), lambda i: (i, 0, 0)),
                   row_blk(ROUTER_LANES),
                   pl.BlockSpec((g, ROUTER_LANES, 128), lambda i: (i, 0, 0))],
        out_shape=[jax.ShapeDtypeStruct((n, D_MODEL), F32),
                   jax.ShapeDtypeStruct((n, D_MODEL), BF16),
                   jax.ShapeDtypeStruct((nblk, 8, t), F32),
                   jax.ShapeDtypeStruct((n, ROUTER_LANES), F32),
                   jax.ShapeDtypeStruct((nblk, ROUTER_LANES, 128), F32)],
        scratch_shapes=[pltpu.VMEM((2 * WIDTH, D_MODEL), BF16)],
        compiler_params=pltpu.CompilerParams(
            dimension_semantics=("arbitrary",), vmem_limit_bytes=V7X_VMEM_LIMIT),
        name="outproj_route_sort",
    )(a, b, x, w_out, g2, wr_hl, br_col, tri, ltri)


def _silu_mul(ab):
    a = ab[:, :D_EXPERT]
    return a * (1.0 / (1.0 + jnp.exp(-a))) * ab[:, D_EXPERT:]


def _chunk_offsets(tab, n_blk, e):
    return [pl.multiple_of(tab(j, e, 0), MOE_ROW_ALIGN) for j in range(n_blk)]


def _moe_gate_up_kernel(tab_ref, h_ref, mrow_ref, w1_ref, sh_ref, sg_ref, hs_ref, shid_ref, xs, *, flags_at):
    t = MOE_BLOCK
    n_blk = MOE_GATE_UP_BLOCKS
    first = pl.program_id(0) * n_blk
    tab = lambda j, e, c: tab_ref[((first + j) * N_EXPERTS + e) * 3 + c]

    piece = 512
    for j in range(n_blk):
        mrow = mrow_ref[j]
        dst1 = mrow[0:1, :].astype(jnp.int32)
        dst2 = mrow[1:2, :].astype(jnp.int32)
        h = h_ref[j * t:(j + 1) * t, :]
        for r0 in range(0, MOE_SORT_ROWS, piece):
            d_idx = lax.broadcasted_iota(jnp.int32, (piece, t), 0) + r0
            sel = jnp.where((d_idx == dst1) | (d_idx == dst2), 1.0, 0.0)
            xs[j, r0:r0 + piece, :] = _dot(sel, h).astype(BF16)
        xs[j, MOE_SORT_ROWS:MOE_ROWS, :] = jnp.zeros((MOE_CHUNK, D_MODEL), BF16)
    hs_ref[...] = jnp.zeros(hs_ref.shape, BF16)

    def first_chunks(g, carry):
        for i in range(MOE_EXPERTS_PER_STEP):
            e = g * MOE_EXPERTS_PER_STEP + i
            offs = _chunk_offsets(tab, n_blk, e)
            x = jnp.concatenate([xs[j, pl.ds(offs[j], MOE_CHUNK), :] for j in range(n_blk)], axis=0)
            hid = _silu_mul(jnp.dot(x, w1_ref[e], preferred_element_type=F32)).astype(BF16)
            for j in range(n_blk):
                hs_ref[j, pl.ds(offs[j], MOE_CHUNK), :] = hid[j * MOE_CHUNK:(j + 1) * MOE_CHUNK, :]
        return carry

    lax.fori_loop(0, N_EXPERTS // MOE_EXPERTS_PER_STEP, first_chunks, 0)

    def more_chunks(j, e, carry):
        off, n_chunks, end = tab(j, e, 0), tab(j, e, 1), tab(j, e, 2)

        def chunk(c, carry):
            r0 = pl.multiple_of(off + c * MOE_CHUNK, MOE_ROW_ALIGN)
            rows = r0 + lax.broadcasted_iota(jnp.int32, (MOE_CHUNK, D_EXPERT), 0)
            hid = _silu_mul(jnp.dot(xs[j, pl.ds(r0, MOE_CHUNK), :], w1_ref[e], preferred_element_type=F32))
            hs_ref[j, pl.ds(r0, MOE_CHUNK), :] = jnp.where(rows < end, hid.astype(BF16),
                                                           hs_ref[j, pl.ds(r0, MOE_CHUNK), :])
            return carry

        return lax.fori_loop(1, n_chunks, chunk, carry)

    for j in range(n_blk):
        @pl.when(tab_ref[flags_at + first + j] > 0)
        def _(j=j):
            lax.fori_loop(0, N_EXPERTS, functools.partial(more_chunks, j), 0)

    @pl.when(pl.program_id(0) == 0)
    def _():
        sh = sh_ref[...]
        gates = sg_ref[...]
        for e in range(N_EXPERTS):
            gate = gates[:, N_GROUPS + e:N_GROUPS + e + 1]
            hid = _silu_mul(jnp.dot(sh, w1_ref[e], preferred_element_type=F32)) * gate
            shid_ref[:, e * D_EXPERT:(e + 1) * D_EXPERT] = hid.astype(BF16)


def _moe_gate_up(tab, h2, mrow, w1_b, s_h, s_gates):
    n = h2.shape[0]
    n_s = s_h.shape[0]
    nblk = n // MOE_BLOCK
    g = MOE_GATE_UP_BLOCKS
    assert nblk % g == 0
    whole = lambda shape: pl.BlockSpec(shape, lambda i, tab: (0,) * len(shape))
    return pl.pallas_call(
        functools.partial(_moe_gate_up_kernel, flags_at=nblk * N_EXPERTS * 3),
        grid_spec=pltpu.PrefetchScalarGridSpec(
            num_scalar_prefetch=1,
            grid=(nblk // g,),
            in_specs=[pl.BlockSpec((g * MOE_BLOCK, D_MODEL), lambda i, tab: (i, 0)),
                      pl.BlockSpec((g, 8, MOE_BLOCK), lambda i, tab: (i, 0, 0)),
                      pl.BlockSpec(w1_b.shape, lambda i, tab: (0, 0, 0), pipeline_mode=pl.Buffered(1)),
                      whole(s_h.shape), whole(s_gates.shape)],
            out_specs=[pl.BlockSpec((g, MOE_ROWS, D_EXPERT), lambda i, tab: (i, 0, 0)),
                       whole((n_s, N_EXPERTS * D_EXPERT))],
            scratch_shapes=[pltpu.VMEM((g, MOE_ROWS, D_MODEL), BF16)]),
        out_shape=[jax.ShapeDtypeStruct((nblk, MOE_ROWS, D_EXPERT), BF16),
                   jax.ShapeDtypeStruct((n_s, N_EXPERTS * D_EXPERT), BF16)],
        compiler_params=pltpu.CompilerParams(
            dimension_semantics=("arbitrary",), vmem_limit_bytes=V7X_VMEM_LIMIT),
        name="moe_gate_up",
    )(tab, h2, mrow, w1_b, s_h, s_gates)


def _moe_down_kernel(tab_ref, hs_ref, xp_ref, mcol_ref, w2_ref, gf_ref, shid_ref, sx_ref, y_ref, sy_ref, os, *,
                     flags_at):
    t = MOE_BLOCK
    n_blk = MOE_DOWN_BLOCKS
    first = pl.program_id(0) * n_blk
    tab = lambda j, e, c: tab_ref[((first + j) * N_EXPERTS + e) * 3 + c]
    os[...] = jnp.zeros(os.shape, BF16)

    def first_chunks(g, carry):
        for i in range(MOE_EXPERTS_PER_STEP):
            e = g * MOE_EXPERTS_PER_STEP + i
            offs = _chunk_offsets(tab, n_blk, e)
            hid = jnp.concatenate([hs_ref[j, pl.ds(offs[j], MOE_CHUNK), :] for j in range(n_blk)], axis=0)
            out = jnp.dot(hid, w2_ref[e], preferred_element_type=F32).astype(BF16)
            for j in range(n_blk):
                os[j, pl.ds(offs[j], MOE_CHUNK), :] = out[j * MOE_CHUNK:(j + 1) * MOE_CHUNK, :]
        return carry

    lax.fori_loop(0, N_EXPERTS // MOE_EXPERTS_PER_STEP, first_chunks, 0)

    def more_chunks(j, e, carry):
        off, n_chunks, end = tab(j, e, 0), tab(j, e, 1), tab(j, e, 2)

        def chunk(c, carry):
            r0 = pl.multiple_of(off + c * MOE_CHUNK, MOE_ROW_ALIGN)
            rows = r0 + lax.broadcasted_iota(jnp.int32, (MOE_CHUNK, D_MODEL), 0)
            out = jnp.dot(hs_ref[j, pl.ds(r0, MOE_CHUNK), :], w2_ref[e], preferred_element_type=F32)
            os[j, pl.ds(r0, MOE_CHUNK), :] = jnp.where(rows < end, out.astype(BF16), os[j, pl.ds(r0, MOE_CHUNK), :])
            return carry

        return lax.fori_loop(1, n_chunks, chunk, carry)

    for j in range(n_blk):
        @pl.when(tab_ref[flags_at + first + j] > 0)
        def _(j=j):
            lax.fori_loop(0, N_EXPERTS, functools.partial(more_chunks, j), 0)

    l_idx = lax.broadcasted_iota(jnp.int32, (t, MOE_SORT_ROWS), 1)

    def scatter_matrix(j):
        mcol = mcol_ref[j * t:(j + 1) * t, :]
        d1c = mcol[:, 0:1].astype(jnp.int32)
        d2c = mcol[:, 1:2].astype(jnp.int32)
        comb = jnp.where(l_idx == d1c, mcol[:, 2:3], 0.0) + jnp.where(l_idx == d2c, mcol[:, 3:4], 0.0)
        return comb.astype(BF16)

    combs = [scatter_matrix(j) for j in range(n_blk)]
    moes = [_dot(combs[j], os[j, 0:MOE_SORT_ROWS, :]) for j in range(n_blk)]
    for j in range(n_blk):
        rows = slice(j * t, (j + 1) * t)
        y_ref[rows, :] = _rmsnorm(xp_ref[rows, :] + moes[j], gf_ref[...])

    @pl.when(pl.program_id(0) == 0)
    def _():
        w2_all = w2_ref[...].reshape(N_EXPERTS * D_EXPERT, D_MODEL)
        ys = sx_ref[...] + jnp.dot(shid_ref[...], w2_all, preferred_element_type=F32)
        sy_ref[...] = _rmsnorm(ys, gf_ref[...])


def _moe_down(tab, hs_sorted, xp, mcol, w2_b, gf, s_hid, s_x):
    n = xp.shape[0]
    n_s = s_x.shape[0]
    nblk = n // MOE_BLOCK
    g = MOE_DOWN_BLOCKS
    assert nblk % g == 0
    whole = lambda shape: pl.BlockSpec(shape, lambda i, tab: (0,) * len(shape))
    row_blk = lambda w: pl.BlockSpec((g * MOE_BLOCK, w), lambda i, tab: (i, 0))
    return pl.pallas_call(
        functools.partial(_moe_down_kernel, flags_at=nblk * N_EXPERTS * 3),
        grid_spec=pltpu.PrefetchScalarGridSpec(
            num_scalar_prefetch=1,
            grid=(nblk // g,),
            in_specs=[pl.BlockSpec((g, MOE_ROWS, D_EXPERT), lambda i, tab: (i, 0, 0)),
                      row_blk(D_MODEL), row_blk(ROUTER_LANES),
                      pl.BlockSpec(w2_b.shape, lambda i, tab: (0, 0, 0), pipeline_mode=pl.Buffered(1)),
                      whole((1, D_MODEL)), whole(s_hid.shape), whole(s_x.shape)],
            out_specs=[row_blk(D_MODEL), whole((n_s, D_MODEL))],
            scratch_shapes=[pltpu.VMEM((g, MOE_ROWS, D_MODEL), BF16)]),
        out_shape=[jax.ShapeDtypeStruct((n, D_MODEL), F32), jax.ShapeDtypeStruct((n_s, D_MODEL), F32)],
        compiler_params=pltpu.CompilerParams(
            dimension_semantics=("arbitrary",), vmem_limit_bytes=V7X_VMEM_LIMIT),
        name="moe_down_combine",
    )(tab, hs_sorted, xp, mcol, w2_b, gf, s_hid, s_x)


def _sample_proj_kernel(x_ref, g1_ref, w_ref, cos_ref, sin_ref, gv_ref, ones_ref, w00_ref, b0_ref,
                        rep_ref, foldt_ref, a_ref, k_ref, v_ref, vn_ref, qkvt_ref):
    h = _rmsnorm(x_ref[...], g1_ref[...])

    def proj(i):
        return _dot(h, w_ref[:, i * WIDTH:(i + 1) * WIDTH], precise=True)

    cos = _tile_lanes(cos_ref[...], WIDTH // 128)
    sin = _tile_lanes(sin_ref[...], WIDTH // 128)
    q = _rope(proj(2), cos, sin) * (HEAD_DIM ** -0.5)
    k = _rope(proj(3), cos, sin)
    v = proj(4)
    vn = _group_rmsnorm(proj(1), ones_ref[...], gv_ref[...], precise=True)
    a_ref[...] = proj(0) * (w00_ref[...] * vn + b0_ref[...])
    k_ref[...] = k
    v_ref[...] = v
    vn_ref[...] = vn

    n_rep = rep_ref.shape[0]
    r_idx = lax.broadcasted_iota(jnp.int32, (n_rep, WIDTH), 0)
    l_idx = lax.broadcasted_iota(jnp.int32, (n_rep, WIDTH), 1)
    own = (l_idx // HEAD_DIM) == (r_idx % N_HEADS)
    for t, src in enumerate((q, k, v)):
        rep = _dot(rep_ref[...], src, precise=True)
        qkvt_ref[t * HEAD_DIM:(t + 1) * HEAD_DIM, :] = lax.dot_general(
            foldt_ref[...], jnp.where(own, rep, 0.0), (((1,), (1,)), ((), ())),
            preferred_element_type=F32, precision=lax.Precision.HIGHEST)


def _sample_proj(x, g1, w_in, cos, sin, gv, ones_bd, w00, b0, rep, foldt):
    bd = x.shape[0]
    sds = lambda r, c: jax.ShapeDtypeStruct((r, c), F32)
    return pl.pallas_call(
        _sample_proj_kernel,
        out_shape=[sds(bd, WIDTH), sds(bd, WIDTH), sds(bd, WIDTH), sds(bd, WIDTH),
                   sds(3 * HEAD_DIM, bd * N_HEADS)],
        compiler_params=pltpu.CompilerParams(vmem_limit_bytes=V7X_VMEM_LIMIT),
        name="sample_proj",
    )(x, g1, w_in, cos, sin, gv, ones_bd, w00, b0, rep, foldt)


def _sample_scores(qkvt_ref, k_ref, b, win):
    n_col = qkvt_ref.shape[1]
    c_idx = lax.broadcasted_iota(jnp.int32, (n_col, 128), 0)
    l_idx = lax.broadcasted_iota(jnp.int32, (n_col, 128), 1)
    pick = jnp.where((c_idx == b * N_HEADS + l_idx) & (l_idx < N_HEADS), 1.0, 0.0)
    cols = _dot(qkvt_ref[...], pick, precise=True)

    head = lax.broadcasted_iota(jnp.int32, (N_HEADS, win), 0)
    s = jnp.zeros((N_HEADS, win), F32)
    s_self = jnp.zeros((N_HEADS, 1), F32)
    for h in range(N_HEADS):
        qc = cols[0:HEAD_DIM, h:h + 1]
        kc = cols[HEAD_DIM:2 * HEAD_DIM, h:h + 1]
        s_h = jnp.sum(k_ref[0, h] * qc, axis=0, keepdims=True)
        s = jnp.where(head == h, s_h, s)
        s_self = jnp.where(head[:, 0:1] == h, jnp.sum(qc * kc, axis=0, keepdims=True), s_self)
    return s, s_self, cols[2 * HEAD_DIM:3 * HEAD_DIM, :]


def _sample_finish(s, s_self, v_cols, v_ref, o_ref, win):
    dist = win - lax.broadcasted_iota(jnp.int32, (1, win), 1)
    members = [(dist <= N_KEYS * dil) & (dist % dil == 0) for dil in DILATIONS]
    es, e_selfs, dens, lses = [], [], [], []
    for mem in members:
        sm = jnp.where(mem, s, NEG)
        m = jnp.maximum(jnp.max(sm, axis=1, keepdims=True), s_self)
        e = jnp.exp(sm - m)
        e_self = jnp.exp(s_self - m)
        den = jnp.sum(e, axis=1, keepdims=True) + e_self
        es.append(e)
        e_selfs.append(e_self)
        dens.append(den)
        lses.append(m + jnp.log(den))
    top = jnp.maximum(jnp.maximum(lses[0], lses[1]), lses[2])
    ws = [jnp.exp(l - top) for l in lses]
    wsum = ws[0] + ws[1] + ws[2]
    coef = [w / (den * wsum) for w, den in zip(ws, dens)]
    p_keys = coef[0] * es[0] + coef[1] * es[1] + coef[2] * es[2]
    p_self = coef[0] * e_selfs[0] + coef[1] * e_selfs[1] + coef[2] * e_selfs[2]

    for h in range(N_HEADS):
        o_ref[0, :, h:h + 1] = (jnp.sum(v_ref[0, h] * p_keys[h:h + 1, :], axis=1, keepdims=True)
                                + p_self[h:h + 1, :] * v_cols[:, h:h + 1])


def _rope_tables(first_pos, count):
    half = HEAD_DIM // 2
    inv = ROPE_THETA ** (-np.arange(half, dtype=np.float64) * 2.0 / HEAD_DIM)
    ang = (first_pos + np.arange(count, dtype=np.float64))[:, None] * inv[None, :]
    cos, sin = np.cos(ang), np.sin(ang)
    cos128 = np.concatenate([cos, cos, cos, cos], axis=1).astype(np.float32)
    sin128 = np.concatenate([-sin, sin, -sin, sin], axis=1).astype(np.float32)
    return jnp.asarray(cos128), jnp.asarray(sin128)


def kernel(x_prompt, x_sample, cache_win_k, cache_win_v, ln1_g, w_in, sgu_norm_g, sgu_w, sgu_b, w_out, ln2_g,
           w_router_group, b_router_group, w_router_expert, b_router_expert, w_gate, w_up, w_down, lnf_g):
    depth = w_in.shape[0]
    assert depth == 1 and x_sample.shape[1] == 1
    B, S, _ = x_prompt.shape
    bd = x_sample.shape[0]
    win = cache_win_k.shape[2]
    assert S % (max(DILATIONS) * CHUNK) == 0 and win >= max(DILATIONS) * N_KEYS and PAST_LEN % CHUNK == 0
    l = 0

    w1_b =jnp.concatenate([w_gate[l], w_up[l]], axis=-1).astype(BF16)
    w2_b = w_down[l].astype(BF16)
    pad = ROUTER_LANES - N_GROUPS - N_EXPERTS
    w_router = jnp.pad(jnp.concatenate([w_router_group[l], w_router_expert[l]], axis=1), ((0, 0), (0, pad)))
    b_router = jnp.pad(jnp.concatenate([b_router_group[l], b_router_expert[l]]), (0, pad))[None, :]
    g1 = ln1_g[l][None, :]
    g2 = ln2_g[l][None, :]
    gf = lnf_g[None, :]
    gv = sgu_norm_g[l].reshape(1, WIDTH)
    grp = jnp.arange(WIDTH) // HEAD_DIM
    ones_bd = jnp.where(grp[:, None] == grp[None, :], 1.0 / HEAD_DIM, 0.0).astype(BF16)
    wp = jnp.concatenate([sgu_w[l][0::2], sgu_w[l][1::2]], axis=-1)
    bias = jnp.repeat(sgu_b[l].T, HEAD_DIM, axis=1)
    w00 = jnp.repeat(sgu_w[l][:, 0, 0], HEAD_DIM)[None, :]
    b0 = jnp.repeat(sgu_b[l][:, 0], HEAD_DIM)[None, :]

    cos_s, sin_s = _rope_tables(PAST_LEN, 1)
    rep = (jnp.arange(bd * N_HEADS)[:, None] // N_HEADS == jnp.arange(bd)[None, :]).astype(F32)
    foldt = (jnp.arange(HEAD_DIM)[:, None] == jnp.arange(WIDTH)[None, :] % HEAD_DIM).astype(F32)
    xs = x_sample.reshape(bd, D_MODEL)
    a_s, k_s, v_s, vn_s, qkvt = _sample_proj(xs, g1, w_in[l], cos_s, sin_s, gv, ones_bd, w00, b0, rep, foldt)
    to_pos_minor = lambda c: jnp.transpose(c, (0, 2, 3, 1))

    cos_p, sin_p = _rope_tables(0, S)
    a_p, q_p, k_p, v_p, kt_p, vt_p, o3 = _prompt_proj(
        x_prompt, g1, w_in[l], cos_p, sin_p, gv, ones_bd, wp, bias,
        qkvt, to_pos_minor(cache_win_k[l]), to_pos_minor(cache_win_v[l]))
    b_p = _prompt_attention(q_p, k_p, v_p)
    n = B * S
    assert n % MOE_BLOCK == 0
    wr_t = w_router.T[:ROUTER_ROWS]
    wr_hi = wr_t.astype(BF16)
    wr_hl = jnp.concatenate([wr_hi, (wr_t - wr_hi.astype(F32)).astype(BF16)], axis=0)
    lane_idx = jnp.arange(ROUTER_LANES)
    tri = (lane_idx[:, None] <= lane_idx[None, :]).astype(BF16)
    ltri = (lane_idx[None, :] < lane_idx[:, None]).astype(BF16)
    xp2, h2, mrow, mcol, tab_f = _mix_route_sort(
        a_p.reshape(n, WIDTH), b_p.reshape(n, WIDTH), x_prompt.reshape(n, D_MODEL),
        w_out[l], g2, wr_hl, b_router.reshape(ROUTER_LANES, 1)[:ROUTER_ROWS], tri, ltri)
    tab_i = tab_f[:, :N_EXPERTS, 0:3].astype(jnp.int32)
    multi_chunk = (jnp.max(tab_i[:, :, 1], axis=1) > 1).astype(jnp.int32)
    tab = jnp.concatenate([tab_i.reshape(-1), multi_chunk])

    b_s = jnp.transpose(o3, (0, 2, 1)).reshape(bd, WIDTH)
    xs2, hs2, gates_s = _mix_router(a_s, b_s, xs, w_out[l], g2, w_router, b_router, tm=bd, precise=True)

    hid_sorted, hid_s = _moe_gate_up(tab, h2, mrow, w1_b, hs2, gates_s)
    y_prompt, y_sample = _moe_down(tab, hid_sorted, xp2, mcol, w2_b, gf, hid_s, xs2)
    y_prompt = y_prompt.reshape(B, S, D_MODEL)
    y_sample = y_sample.reshape(bd, 1, D_MODEL)
    buf_p = min(MAX_WINDOW, S)
    to_win = lambda t: jnp.transpose(t.reshape(1, B, N_HEADS, HEAD_DIM, buf_p), (0, 1, 4, 2, 3))
    new_k_p = to_win(kt_p)
    new_v_p = to_win(vt_p)

    shape_s = (1, bd, 1, N_HEADS, HEAD_DIM)
    return (y_prompt, y_sample, new_k_p, new_v_p,
            k_s.reshape(shape_s), v_s.reshape(shape_s), vn_s.reshape(shape_s))
```

```python
import functools

import jax
import jax.numpy as jnp
import numpy as np
from jax import lax
from jax.experimental import pallas as pl
from jax.experimental.pallas import tpu as pltpu

F32 = jnp.float32
BF16 = jnp.bfloat16

D_MODEL = 1024
HEAD_DIM = 64
N_HEADS = 8
WIDTH = N_HEADS * HEAD_DIM
PROJ_COLS = 5 * WIDTH
CHUNK = 128
DILATIONS = (1, 4, 16)
N_KEYS = 128
MAX_WINDOW = 2048
PAST_LEN = 16384
ROPE_THETA = 10000.0
N_GROUPS = 4
EXPERTS_PER_GROUP = 8
N_EXPERTS = N_GROUPS * EXPERTS_PER_GROUP
D_EXPERT = 128
EPS = 1e-6
NEG = -1e30
TILES_PER_STEP = 32
SUBLANE_STRIDE = 4
assert DILATIONS == (1, SUBLANE_STRIDE, SUBLANE_STRIDE ** 2)
MOE_BLOCK = 512
MOE_ROW_ALIGN = 16
MOE_CHUNK = 64
MOE_EXPERTS_PER_STEP = 16
MOE_ROUTE_BLOCKS = 2
MOE_GATE_UP_BLOCKS = 4
MOE_DOWN_BLOCKS = 2
MOE_SORT_ROWS = -(-(2 * MOE_BLOCK + N_EXPERTS * (MOE_ROW_ALIGN - 1)) // 512) * 512
MOE_ROWS = MOE_SORT_ROWS + MOE_CHUNK
ROUTER_ROWS = 48
ROUTER_LANES = 128
V7X_VMEM_LIMIT = 56 * 1024 * 1024


def _rmsnorm(x, g):
    return x * lax.rsqrt(jnp.mean(x * x, axis=-1, keepdims=True) + EPS) * g


def _tile_lanes(t, reps):
    return jnp.concatenate([t] * reps, axis=1)


def _rope(t, cos, sin_signed):
    lane = lax.broadcasted_iota(jnp.int32, t.shape, 1)
    first_half = (lane % HEAD_DIM) < (HEAD_DIM // 2)
    n = t.shape[1]
    partner = jnp.where(first_half, pltpu.roll(t, n - HEAD_DIM // 2, 1), pltpu.roll(t, HEAD_DIM // 2, 1))
    return t * cos + partner * sin_signed


def _dot(a, b, precise=False):
    if precise:
        return jnp.dot(a.astype(F32), b.astype(F32), preferred_element_type=F32,
                       precision=lax.Precision.HIGHEST)
    return jnp.dot(a.astype(BF16), b.astype(BF16), preferred_element_type=F32)


def _group_rmsnorm(va, ones_bd, gv, precise=False):
    ms = _dot(va * va, ones_bd, precise)
    return va * lax.rsqrt(ms + EPS) * gv


def _proj_kernel(x_ref, g1_ref, w_ref, cos_ref, sin_ref, gv_ref, ones_ref, wp_ref, bias_ref,
                 qkvt_ref, ck_ref, cv_ref,
                 a_ref, q_ref, k_ref, v_ref, kt_ref, vt_ref, so_ref, wb, *, tm, first_win_tile, n_seq, win):
    @pl.when((pl.program_id(0) == 0) & (pl.program_id(1) == 0))
    def _():
        wb[...] = w_ref[...].astype(BF16)

    h = _rmsnorm(x_ref[0], g1_ref[...]).astype(BF16)

    def proj(i):
        return jnp.dot(h, wb[:, i * WIDTH:(i + 1) * WIDTH], preferred_element_type=F32)

    cos = _tile_lanes(cos_ref[...], WIDTH // 128)
    sin = _tile_lanes(sin_ref[...], WIDTH // 128)
    q_ref[0] = _rope(proj(2), cos, sin) * (HEAD_DIM ** -0.5)
    k_ref[0] = _rope(proj(3), cos, sin)
    v_ref[0] = proj(4)

    s_seq = jnp.minimum(pl.program_id(0) * pl.num_programs(1) + pl.program_id(1), n_seq - 1)
    s_scores, s_self, s_vcols = _sample_scores(qkvt_ref, ck_ref, s_seq, win)

    u = proj(0)
    vn = _group_rmsnorm(proj(1), ones_ref[...], gv_ref[...]).astype(BF16)

    lane = lax.broadcasted_iota(jnp.int32, (CHUNK, 128), 1)
    left = lane < HEAD_DIM
    row = lax.broadcasted_iota(jnp.int32, (CHUNK, 2 * CHUNK), 0)
    col = lax.broadcasted_iota(jnp.int32, (CHUNK, 2 * CHUNK), 1)
    causal = (col % CHUNK) <= row
    zero = jnp.zeros((CHUNK, 128), BF16)
    wps = [jnp.where(causal, wp_ref[gp], 0.0).astype(BF16) for gp in range(N_HEADS // 2)]
    def block_diag(vv):
        return jnp.concatenate([jnp.where(left, vv, zero), jnp.where(left, zero, vv)], axis=0)

    for c in range(0, tm // CHUNK, 2):
        rows = [slice((c + i) * CHUNK, (c + i + 1) * CHUNK) for i in range(2)]
        mixes = [[], []]
        for gp in range(N_HEADS // 2):
            lanes = slice(gp * 128, (gp + 1) * 128)
            v2 = jnp.concatenate([block_diag(vn[rows[0], lanes]), block_diag(vn[rows[1], lanes])], axis=1)
            both = jnp.dot(wps[gp], v2, preferred_element_type=F32)
            mixes[0].append(both[:, 0:128])
            mixes[1].append(both[:, 128:256])
        for i in range(2):
            mix = jnp.concatenate(mixes[i], axis=1) + bias_ref[...]
            a_ref[0, rows[i], :] = (u[rows[i], :] * mix).astype(a_ref.dtype)

    _sample_finish(s_scores, s_self, s_vcols, cv_ref, so_ref, win)

    @pl.when(pl.program_id(1) >= first_win_tile)
    def _():
        kt_ref[0] = k_ref[0].T
        vt_ref[0] = v_ref[0].T


def _prompt_proj(x, g1, w_in_b, cos, sin, gv, ones_bd, wp, bias, qkvt, cache_k_t, cache_v_t, *, tm=512):
    B, S, _ = x.shape
    const2 = lambda b, j: (0, 0)
    out_sds = lambda dt: jax.ShapeDtypeStruct((B, S, WIDTH), dt)
    tile = pl.BlockSpec((1, tm, WIDTH), lambda b, j: (b, j, 0))
    win = min(MAX_WINDOW, S)
    first_win_tile = (S - win) // tm
    tile_t = pl.BlockSpec((1, WIDTH, tm), lambda b, j: (b, 0, jnp.maximum(j - first_win_tile, 0)))
    win_sds = jax.ShapeDtypeStruct((B, WIDTH, win), F32)
    n_seq, _, _, cache_win = cache_k_t.shape
    n_tiles = S // tm
    assert n_seq <= B * n_tiles, "one sample sequence rides on each grid step"
    seq_blk = lambda shape: pl.BlockSpec(
        (1,) + shape, lambda b, j: (jnp.minimum(b * n_tiles + j, n_seq - 1),) + (0,) * len(shape))
    cache_blk = seq_blk((N_HEADS, HEAD_DIM, cache_win))
    return pl.pallas_call(
        functools.partial(_proj_kernel, tm=tm, first_win_tile=first_win_tile, n_seq=n_seq, win=cache_win),
        grid=(B, n_tiles),
        in_specs=[
            pl.BlockSpec((1, tm, D_MODEL), lambda b, j: (b, j, 0)),
            pl.BlockSpec((1, D_MODEL), const2),
            pl.BlockSpec((D_MODEL, PROJ_COLS), const2, pipeline_mode=pl.Buffered(1)),
            pl.BlockSpec((tm, 128), lambda b, j: (j, 0)),
            pl.BlockSpec((tm, 128), lambda b, j: (j, 0)),
            pl.BlockSpec((1, WIDTH), const2),
            pl.BlockSpec((WIDTH, WIDTH), const2),
            pl.BlockSpec((N_HEADS // 2, CHUNK, 2 * CHUNK), lambda b, j: (0, 0, 0)),
            pl.BlockSpec((CHUNK, WIDTH), const2),
            pl.BlockSpec(qkvt.shape, const2), cache_blk, cache_blk,
        ],
        out_specs=[tile, tile, tile, tile, tile_t, tile_t, seq_blk((HEAD_DIM, N_HEADS))],
        out_shape=[out_sds(BF16), out_sds(F32), out_sds(F32), out_sds(F32), win_sds, win_sds,
                   jax.ShapeDtypeStruct((n_seq, HEAD_DIM, N_HEADS), F32)],
        scratch_shapes=[pltpu.VMEM((D_MODEL, PROJ_COLS), BF16)],
        compiler_params=pltpu.CompilerParams(
            dimension_semantics=("arbitrary", "arbitrary"), vmem_limit_bytes=V7X_VMEM_LIMIT),
        name="prompt_proj_sgu",
    )(x, g1, w_in_b, cos, sin, gv, ones_bd, wp, bias, qkvt, cache_k_t, cache_v_t)


def _attn_kernel(q_ref, k_ref, v_ref, o_ref, qd, kd, vd, res_o, res_l, nat_o, nat_l, bias, stage, *, seq):
    n_tiles = seq // CHUNK
    lane = lax.broadcasted_iota(jnp.int32, (CHUNK, 128), 1)
    left = lane < HEAD_DIM
    qi2 = lax.broadcasted_iota(jnp.int32, (2 * CHUNK, 2 * CHUNK), 0) % CHUNK
    kj2 = lax.broadcasted_iota(jnp.int32, (2 * CHUNK, 2 * CHUNK), 1)
    dist2 = CHUNK + qi2 - kj2
    band2 = (dist2 >= 0) & (dist2 <= N_KEYS)
    zero_q = jnp.zeros((CHUNK, 128), BF16)
    bias[0] = jnp.where(band2, 0.0, NEG)
    bias[1] = jnp.where(band2 & (kj2 >= CHUNK), 0.0, NEG)

    kd[0:CHUNK, :] = jnp.zeros((CHUNK, 128), BF16)
    vd[0:CHUNK, :] = jnp.zeros((CHUNK, 128), BF16)

    for p, dil in enumerate(DILATIONS):
        sub = seq // dil
        nb = sub // CHUNK
        for ti, (src_ref, dst, pad) in enumerate(((q_ref, qd, 0), (k_ref, kd, CHUNK), (v_ref, vd, CHUNK))):
            if dil == 1:
                dst[pad:pad + seq, :] = src_ref[0].astype(BF16)
            elif dil == SUBLANE_STRIDE:
                for r in range(dil):
                    val = src_ref[0, pl.ds(r, sub, stride=dil), :]
                    stage[ti, r * sub:(r + 1) * sub, :] = val
                    dst[pad + r * sub:pad + (r + 1) * sub, :] = val.astype(BF16)
            else:
                coarse = seq // SUBLANE_STRIDE
                for r_lo in range(SUBLANE_STRIDE):
                    for r_hi in range(SUBLANE_STRIDE):
                        r = r_lo + SUBLANE_STRIDE * r_hi
                        val = stage[ti, pl.ds(r_lo * coarse + r_hi, sub, stride=SUBLANE_STRIDE), :]
                        dst[pad + r * sub:pad + (r + 1) * sub, :] = val.astype(BF16)

        def tile_body(g, i, p=p, nb=nb):
            t = g * TILES_PER_STEP + i
            row = pl.multiple_of(t * CHUNK, CHUNK)
            qt = qd[pl.ds(row, CHUNK), :]
            k2 = kd[pl.ds(row, 2 * CHUNK), :]
            v2 = vd[pl.ds(row, 2 * CHUNK), :]
            if TILES_PER_STEP % nb == 0:
                variant = 1 if i % nb == 0 else 0
            elif i == 0:
                variant = jnp.where((g * TILES_PER_STEP) % nb == 0, 1, 0)
            else:
                variant = 0
            q2 = jnp.concatenate([jnp.where(left, qt, zero_q), jnp.where(left, zero_q, qt)], axis=0)
            s = lax.dot_general(q2, k2, (((1,), (1,)), ((), ())), preferred_element_type=F32)
            s = s + bias[variant]
            m = jnp.max(s, axis=1, keepdims=True)
            e = jnp.exp(s - m)
            den = jnp.sum(e, axis=1, keepdims=True)
            pv = jnp.dot(e.astype(BF16), v2, preferred_element_type=F32) / den
            lse = jnp.broadcast_to(m + jnp.log(den), (2 * CHUNK, 128))
            res_o[p, pl.ds(row, CHUNK), :] = jnp.where(left, pv[0:CHUNK], pv[CHUNK:2 * CHUNK])
            res_l[p, pl.ds(row, CHUNK), :] = jnp.where(left, lse[0:CHUNK], lse[CHUNK:2 * CHUNK])

        def group_body(g, carry, tile_body=tile_body):
            for i in range(TILES_PER_STEP):
                tile_body(g, i)
            return carry

        lax.fori_loop(0, n_tiles // TILES_PER_STEP, group_body, 0)

    for p, dil in enumerate(DILATIONS):
        if dil == 1:
            continue
        sub = seq // dil
        for si, (res, nat) in enumerate(((res_o, nat_o), (res_l, nat_l))):
            if dil == SUBLANE_STRIDE:
                for r in range(dil):
                    nat[p - 1, pl.ds(r, sub, stride=dil), :] = res[p, r * sub:(r + 1) * sub, :]
            else:
                coarse = seq // SUBLANE_STRIDE
                for r_lo in range(SUBLANE_STRIDE):
                    for r_hi in range(SUBLANE_STRIDE):
                        r = r_lo + SUBLANE_STRIDE * r_hi
                        stage[si, pl.ds(r_lo * coarse + r_hi, sub, stride=SUBLANE_STRIDE), :] = (
                            res[p, r * sub:(r + 1) * sub, :])
                for r_lo in range(SUBLANE_STRIDE):
                    nat[p - 1, pl.ds(r_lo, coarse, stride=SUBLANE_STRIDE), :] = (
                        stage[si, r_lo * coarse:(r_lo + 1) * coarse, :])

    rows_per_step = 256

    def merge_body(c, carry):
        rows = pl.ds(pl.multiple_of(c * rows_per_step, rows_per_step), rows_per_step)
        l0, l1, l2 = res_l[0, rows, :], nat_l[0, rows, :], nat_l[1, rows, :]
        top = jnp.maximum(jnp.maximum(l0, l1), l2)
        w0, w1, w2 = jnp.exp(l0 - top), jnp.exp(l1 - top), jnp.exp(l2 - top)
        num = w0 * res_o[0, rows, :] + w1 * nat_o[0, rows, :] + w2 * nat_o[1, rows, :]
        o_ref[0, rows, :] = (num / (w0 + w1 + w2)).astype(o_ref.dtype)
        return carry

    lax.fori_loop(0, seq // rows_per_step, merge_body, 0)


def _prompt_attention(q, k, v):
    B, S, _ = q.shape
    blk = pl.BlockSpec((1, S, 128), lambda b, hp: (b, 0, hp))
    return pl.pallas_call(
        functools.partial(_attn_kernel, seq=S),
        grid=(B, WIDTH // 128),
        in_specs=[blk, blk, blk],
        out_specs=blk,
        out_shape=jax.ShapeDtypeStruct((B, S, WIDTH), BF16),
        scratch_shapes=[
            pltpu.VMEM((S, 128), BF16),
            pltpu.VMEM((S + CHUNK, 128), BF16),
            pltpu.VMEM((S + CHUNK, 128), BF16),
            pltpu.VMEM((len(DILATIONS), S, 128), F32),
            pltpu.VMEM((len(DILATIONS), S, 128), F32),
            pltpu.VMEM((len(DILATIONS) - 1, S, 128), F32),
            pltpu.VMEM((len(DILATIONS) - 1, S, 128), F32),
            pltpu.VMEM((2, 2 * CHUNK, 2 * CHUNK), F32),
            pltpu.VMEM((3, S, 128), F32),
        ],
        compiler_params=pltpu.CompilerParams(
            dimension_semantics=("arbitrary", "arbitrary"), vmem_limit_bytes=V7X_VMEM_LIMIT),
        name="prompt_dilated_attention",
    )(q, k, v)


def _route(logits):
    lane = lax.broadcasted_iota(jnp.int32, logits.shape, 1)
    big = jnp.int32(ROUTER_LANES)
    lg = jnp.where(lane < N_GROUPS, logits, NEG)
    gmax = jnp.max(lg, axis=1, keepdims=True)
    gp = 1.0 / jnp.sum(jnp.exp(lg - gmax), axis=1, keepdims=True)
    gi = jnp.min(jnp.where(lg == gmax, lane, big), axis=1, keepdims=True)
    lo = N_GROUPS + EXPERTS_PER_GROUP * gi
    le = jnp.where((lane >= lo) & (lane < lo + EXPERTS_PER_GROUP), logits, NEG)
    m1 = jnp.max(le, axis=1, keepdims=True)
    i1 = jnp.min(jnp.where(le == m1, lane, big), axis=1, keepdims=True)
    le2 = jnp.where(lane == i1, NEG, le)
    m2 = jnp.max(le2, axis=1, keepdims=True)
    i2 = jnp.min(jnp.where(le2 == m2, lane, big), axis=1, keepdims=True)
    e2 = jnp.exp(m2 - m1)
    w1 = 1.0 / (1.0 + e2)
    w2 = e2 / (1.0 + e2)
    return jnp.where(lane == i1, gp * w1, jnp.where(lane == i2, gp * w2, 0.0))


def _mix_router_kernel(a_ref, b_ref, x_ref, wo_ref, g2_ref, wr_ref, br_ref, xp_ref, h2_ref, gates_ref, *, precise):
    mixed = (_dot(a_ref[...], wo_ref[0:WIDTH, :], precise)
             + _dot(b_ref[...], wo_ref[WIDTH:2 * WIDTH, :], precise))
    xp = x_ref[...] + mixed
    xp_ref[...] = xp
    h2 = _rmsnorm(xp, g2_ref[...])
    h2_ref[...] = h2.astype(h2_ref.dtype)
    logits = jnp.dot(h2, wr_ref[...], preferred_element_type=F32,
                     precision=lax.Precision.HIGHEST) + br_ref[...]
    gates_ref[...] = _route(logits)


def _mix_router(a, b, x, w_out, g2, w_router, b_router, *, tm, precise=False):
    n = x.shape[0]
    const = lambda i: (0, 0)
    row_blk = lambda w: pl.BlockSpec((tm, w), lambda i: (i, 0))
    return pl.pallas_call(
        functools.partial(_mix_router_kernel, precise=precise),
        grid=(n // tm,),
        in_specs=[row_blk(WIDTH), row_blk(WIDTH), row_blk(D_MODEL),
                  pl.BlockSpec((2 * WIDTH, D_MODEL), const),
                  pl.BlockSpec((1, D_MODEL), const),
                  pl.BlockSpec((D_MODEL, ROUTER_LANES), const),
                  pl.BlockSpec((1, ROUTER_LANES), const)],
        out_specs=[row_blk(D_MODEL), row_blk(D_MODEL), row_blk(ROUTER_LANES)],
        out_shape=[jax.ShapeDtypeStruct((n, D_MODEL), F32),
                   jax.ShapeDtypeStruct((n, D_MODEL), BF16),
                   jax.ShapeDtypeStruct((n, ROUTER_LANES), F32)],
        compiler_params=pltpu.CompilerParams(
            dimension_semantics=("arbitrary",), vmem_limit_bytes=V7X_VMEM_LIMIT),
        name="outproj_router",
    )(a, b, x, w_out, g2, w_router, b_router)


def _nt_dot(w, t):
    return lax.dot_general(w, t, (((1,), (1,)), ((), ())), preferred_element_type=F32)


def _route_t(logits_t):
    row = lax.broadcasted_iota(jnp.int32, logits_t.shape, 0)
    big = jnp.int32(ROUTER_LANES)
    lg = jnp.where(row < N_GROUPS, logits_t, NEG)
    gmax = jnp.max(lg, axis=0, keepdims=True)
    gp = 1.0 / jnp.sum(jnp.exp(lg - gmax), axis=0, keepdims=True)
    gi = jnp.min(jnp.where(lg == gmax, row, big), axis=0, keepdims=True)
    lo = N_GROUPS + EXPERTS_PER_GROUP * gi
    le = jnp.where((row >= lo) & (row < lo + EXPERTS_PER_GROUP), logits_t, NEG)
    m1 = jnp.max(le, axis=0, keepdims=True)
    i1 = jnp.min(jnp.where(le == m1, row, big), axis=0, keepdims=True)
    le2 = jnp.where(row == i1, NEG, le)
    m2 = jnp.max(le2, axis=0, keepdims=True)
    i2 = jnp.min(jnp.where(le2 == m2, row, big), axis=0, keepdims=True)
    e2 = jnp.exp(m2 - m1)
    return i1 - N_GROUPS, i2 - N_GROUPS, gp / (1.0 + e2), gp * e2 / (1.0 + e2)


def _mix_route_sort_kernel(a_ref, b_ref, x_ref, wo_ref, g2_ref, wr_ref, brc_ref, tri_ref, ltri_ref,
                           xp_ref, h2_ref, mrow_ref, mcol_ref, tab_ref, wob):
    t = MOE_BLOCK

    @pl.when(pl.program_id(0) == 0)
    def _():
        wob[...] = wo_ref[...].astype(BF16)

    def project(j):
        rows = slice(j * t, (j + 1) * t)
        xp = x_ref[rows, :] + _dot(a_ref[rows, :], wob[0:WIDTH, :]) + _dot(b_ref[rows, :], wob[WIDTH:2 * WIDTH, :])
        xp_ref[rows, :] = xp
        h2 = _rmsnorm(xp, g2_ref[...])
        hi = h2.astype(BF16)
        h2_ref[rows, :] = hi
        return hi, (h2 - hi.astype(F32)).astype(BF16)

    def route(hi, lo):
        prod_hi = _nt_dot(wr_ref[...], hi)
        logits_t = (prod_hi[0:ROUTER_ROWS] + prod_hi[ROUTER_ROWS:2 * ROUTER_ROWS]
                    + _nt_dot(wr_ref[0:ROUTER_ROWS, :], lo) + brc_ref[...])
        return _route_t(logits_t)

    def sort_meta(j, ex1, ex2, gate1, gate2):
        rows = slice(j * t, (j + 1) * t)
        pair_e = jnp.concatenate([ex1, ex2], axis=1)
        row = lax.broadcasted_iota(jnp.int32, (N_EXPERTS, 2 * t), 0)
        onehot = jnp.where(row == pair_e, 1.0, 0.0)
        n_lane_tiles = 2 * t // 128
        local = _dot(jnp.concatenate([onehot[:, k * 128:(k + 1) * 128] for k in range(n_lane_tiles)], axis=0),
                     tri_ref[...])
        carry = jnp.zeros((N_EXPERTS, 1), F32)
        cums = []
        for k in range(n_lane_tiles):
            tile = local[k * N_EXPERTS:(k + 1) * N_EXPERTS, :]
            cums.append(tile + carry)
            carry = carry + tile[:, 127:128]
        cum = jnp.concatenate(cums, axis=1)
        rank = jnp.sum(onehot * cum, axis=0, keepdims=True) - 1.0
        counts = carry
        units32 = jnp.floor((counts + (MOE_ROW_ALIGN - 1)) * (1.0 / MOE_ROW_ALIGN))
        units = jnp.concatenate([jnp.broadcast_to(units32, (N_EXPERTS, 128)),
                                 jnp.zeros((ROUTER_LANES - N_EXPERTS, 128), F32)], axis=0)
        off = _dot(ltri_ref[...], units) * MOE_ROW_ALIGN
        dst = jnp.sum(onehot * off[0:N_EXPERTS, 0:1], axis=0, keepdims=True) + rank

        r8 = lax.broadcasted_iota(jnp.int32, (8, t), 0)
        mrow_ref[j] = jnp.where(r8 == 0, dst[:, 0:t], jnp.where(r8 == 1, dst[:, t:2 * t],
                                jnp.where(r8 == 2, gate1, jnp.where(r8 == 3, gate2, 0.0))))
        r128 = lax.broadcasted_iota(jnp.int32, (ROUTER_LANES, t), 0)
        meta = jnp.where(r128 == 0, dst[:, 0:t], jnp.where(r128 == 1, dst[:, t:2 * t],
                         jnp.where(r128 == 2, gate1, jnp.where(r128 == 3, gate2, 0.0))))
        mcol_ref[rows, :] = meta.T
        lane = lax.broadcasted_iota(jnp.int32, (ROUTER_LANES, 128), 1)
        n_rows = units * MOE_ROW_ALIGN
        chunks = jnp.floor((n_rows + (MOE_CHUNK - 1)) * (1.0 / MOE_CHUNK))
        tab_ref[j] = jnp.where(lane == 0, off, jnp.where(lane == 1, chunks, jnp.where(lane == 2, off + n_rows, 0.0)))

    blocks = range(MOE_ROUTE_BLOCKS)
    projected = [project(j) for j in blocks]
    routed = [route(hi, lo) for hi, lo in projected]
    for j in blocks:
        sort_meta(j, *routed[j])


def _mix_route_sort(a, b, x, w_out, g2, wr_hl, br_col, tri, ltri):
    n = x.shape[0]
    t = MOE_BLOCK
    nblk = n // t
    g = MOE_ROUTE_BLOCKS
    assert nblk % g == 0
    const = lambda i: (0, 0)
    row_blk = lambda w: pl.BlockSpec((g * t, w), lambda i: (i, 0))
    return pl.pallas_call(
        _mix_route_sort_kernel,
        grid=(nblk // g,),
        in_specs=[row_blk(WIDTH), row_blk(WIDTH), row_blk(D_MODEL),
                  pl.BlockSpec((2 * WIDTH, D_MODEL), const, pipeline_mode=pl.Buffered(1)),
                  pl.BlockSpec((1, D_MODEL), const),
                  pl.BlockSpec((2 * ROUTER_ROWS, D_MODEL), const),
                  pl.BlockSpec((ROUTER_ROWS, 1), const),
                  pl.BlockSpec((ROUTER_LANES, ROUTER_LANES), const),
                  pl.BlockSpec((ROUTER_LANES, ROUTER_LANES), const)],
        out_specs=[row_blk(D_MODEL), row_blk(D_MODEL),
                   pl.BlockSpec((g, 8, t), lambda i: (i, 0, 0)),
                   row_blk(ROUTER_LANES),
                   pl.BlockSpec((g, ROUTER_LANES, 128), lambda i: (i, 0, 0))],
        out_shape=[jax.ShapeDtypeStruct((n, D_MODEL), F32),
                   jax.ShapeDtypeStruct((n, D_MODEL), BF16),
                   jax.ShapeDtypeStruct((nblk, 8, t), F32),
                   jax.ShapeDtypeStruct((n, ROUTER_LANES), F32),
                   jax.ShapeDtypeStruct((nblk, ROUTER_LANES, 128), F32)],
        scratch_shapes=[pltpu.VMEM((2 * WIDTH, D_MODEL), BF16)],
        compiler_params=pltpu.CompilerParams(
            dimension_semantics=("arbitrary",), vmem_limit_bytes=V7X_VMEM_LIMIT),
        name="outproj_route_sort",
    )(a, b, x, w_out, g2, wr_hl, br_col, tri, ltri)


def _silu_mul(ab):
    a = ab[:, :D_EXPERT]
    return a * (1.0 / (1.0 + jnp.exp(-a))) * ab[:, D_EXPERT:]


def _chunk_offsets(tab, n_blk, e):
    return [pl.multiple_of(tab(j, e, 0), MOE_ROW_ALIGN) for j in range(n_blk)]


def _moe_gate_up_kernel(tab_ref, h_ref, mrow_ref, w1_ref, sh_ref, sg_ref, hs_ref, shid_ref, xs, *, flags_at):
    t = MOE_BLOCK
    n_blk = MOE_GATE_UP_BLOCKS
    first = pl.program_id(0) * n_blk
    tab = lambda j, e, c: tab_ref[((first + j) * N_EXPERTS + e) * 3 + c]

    piece = 512
    for j in range(n_blk):
        mrow = mrow_ref[j]
        dst1 = mrow[0:1, :].astype(jnp.int32)
        dst2 = mrow[1:2, :].astype(jnp.int32)
        h = h_ref[j * t:(j + 1) * t, :]
        for r0 in range(0, MOE_SORT_ROWS, piece):
            d_idx = lax.broadcasted_iota(jnp.int32, (piece, t), 0) + r0
            sel = jnp.where((d_idx == dst1) | (d_idx == dst2), 1.0, 0.0)
            xs[j, r0:r0 + piece, :] = _dot(sel, h).astype(BF16)
        xs[j, MOE_SORT_ROWS:MOE_ROWS, :] = jnp.zeros((MOE_CHUNK, D_MODEL), BF16)
    hs_ref[...] = jnp.zeros(hs_ref.shape, BF16)

    def first_chunks(g, carry):
        for i in range(MOE_EXPERTS_PER_STEP):
            e = g * MOE_EXPERTS_PER_STEP + i
            offs = _chunk_offsets(tab, n_blk, e)
            x = jnp.concatenate([xs[j, pl.ds(offs[j], MOE_CHUNK), :] for j in range(n_blk)], axis=0)
            hid = _silu_mul(jnp.dot(x, w1_ref[e], preferred_element_type=F32)).astype(BF16)
            for j in range(n_blk):
                hs_ref[j, pl.ds(offs[j], MOE_CHUNK), :] = hid[j * MOE_CHUNK:(j + 1) * MOE_CHUNK, :]
        return carry

    lax.fori_loop(0, N_EXPERTS // MOE_EXPERTS_PER_STEP, first_chunks, 0)

    def more_chunks(j, e, carry):
        off, n_chunks, end = tab(j, e, 0), tab(j, e, 1), tab(j, e, 2)

        def chunk(c, carry):
            r0 = pl.multiple_of(off + c * MOE_CHUNK, MOE_ROW_ALIGN)
            rows = r0 + lax.broadcasted_iota(jnp.int32, (MOE_CHUNK, D_EXPERT), 0)
            hid = _silu_mul(jnp.dot(xs[j, pl.ds(r0, MOE_CHUNK), :], w1_ref[e], preferred_element_type=F32))
            hs_ref[j, pl.ds(r0, MOE_CHUNK), :] = jnp.where(rows < end, hid.astype(BF16),
                                                           hs_ref[j, pl.ds(r0, MOE_CHUNK), :])
            return carry

        return lax.fori_loop(1, n_chunks, chunk, carry)

    for j in range(n_blk):
        @pl.when(tab_ref[flags_at + first + j] > 0)
        def _(j=j):
            lax.fori_loop(0, N_EXPERTS, functools.partial(more_chunks, j), 0)

    @pl.when(pl.program_id(0) == 0)
    def _():
        sh = sh_ref[...]
        gates = sg_ref[...]
        for e in range(N_EXPERTS):
            gate = gates[:, N_GROUPS + e:N_GROUPS + e + 1]
            hid = _silu_mul(jnp.dot(sh, w1_ref[e], preferred_element_type=F32)) * gate
            shid_ref[:, e * D_EXPERT:(e + 1) * D_EXPERT] = hid.astype(BF16)


def _moe_gate_up(tab, h2, mrow, w1_b, s_h, s_gates):
    n = h2.shape[0]
    n_s = s_h.shape[0]
    nblk = n // MOE_BLOCK
    g = MOE_GATE_UP_BLOCKS
    assert nblk % g == 0
    whole = lambda shape: pl.BlockSpec(shape, lambda i, tab: (0,) * len(shape))
    return pl.pallas_call(
        functools.partial(_moe_gate_up_kernel, flags_at=nblk * N_EXPERTS * 3),
        grid_spec=pltpu.PrefetchScalarGridSpec(
            num_scalar_prefetch=1,
            grid=(nblk // g,),
            in_specs=[pl.BlockSpec((g * MOE_BLOCK, D_MODEL), lambda i, tab: (i, 0)),
                      pl.BlockSpec((g, 8, MOE_BLOCK), lambda i, tab: (i, 0, 0)),
                      pl.BlockSpec(w1_b.shape, lambda i, tab: (0, 0, 0), pipeline_mode=pl.Buffered(1)),
                      whole(s_h.shape), whole(s_gates.shape)],
            out_specs=[pl.BlockSpec((g, MOE_ROWS, D_EXPERT), lambda i, tab: (i, 0, 0)),
                       whole((n_s, N_EXPERTS * D_EXPERT))],
            scratch_shapes=[pltpu.VMEM((g, MOE_ROWS, D_MODEL), BF16)]),
        out_shape=[jax.ShapeDtypeStruct((nblk, MOE_ROWS, D_EXPERT), BF16),
                   jax.ShapeDtypeStruct((n_s, N_EXPERTS * D_EXPERT), BF16)],
        compiler_params=pltpu.CompilerParams(
            dimension_semantics=("arbitrary",), vmem_limit_bytes=V7X_VMEM_LIMIT),
        name="moe_gate_up",
    )(tab, h2, mrow, w1_b, s_h, s_gates)


def _moe_down_kernel(tab_ref, hs_ref, xp_ref, mcol_ref, w2_ref, gf_ref, shid_ref, sx_ref, y_ref, sy_ref, os, *,
                     flags_at):
    t = MOE_BLOCK
    n_blk = MOE_DOWN_BLOCKS
    first = pl.program_id(0) * n_blk
    tab = lambda j, e, c: tab_ref[((first + j) * N_EXPERTS + e) * 3 + c]
    os[...] = jnp.zeros(os.shape, BF16)

    def first_chunks(g, carry):
        for i in range(MOE_EXPERTS_PER_STEP):
            e = g * MOE_EXPERTS_PER_STEP + i
            offs = _chunk_offsets(tab, n_blk, e)
            hid = jnp.concatenate([hs_ref[j, pl.ds(offs[j], MOE_CHUNK), :] for j in range(n_blk)], axis=0)
            out = jnp.dot(hid, w2_ref[e], preferred_element_type=F32).astype(BF16)
            for j in range(n_blk):
                os[j, pl.ds(offs[j], MOE_CHUNK), :] = out[j * MOE_CHUNK:(j + 1) * MOE_CHUNK, :]
        return carry

    lax.fori_loop(0, N_EXPERTS // MOE_EXPERTS_PER_STEP, first_chunks, 0)

    def more_chunks(j, e, carry):
        off, n_chunks, end = tab(j, e, 0), tab(j, e, 1), tab(j, e, 2)

        def chunk(c, carry):
            r0 = pl.multiple_of(off + c * MOE_CHUNK, MOE_ROW_ALIGN)
            rows = r0 + lax.broadcasted_iota(jnp.int32, (MOE_CHUNK, D_MODEL), 0)
            out = jnp.dot(hs_ref[j, pl.ds(r0, MOE_CHUNK), :], w2_ref[e], preferred_element_type=F32)
            os[j, pl.ds(r0, MOE_CHUNK), :] = jnp.where(rows < end, out.astype(BF16), os[j, pl.ds(r0, MOE_CHUNK), :])
            return carry

        return lax.fori_loop(1, n_chunks, chunk, carry)

    for j in range(n_blk):
        @pl.when(tab_ref[flags_at + first + j] > 0)
        def _(j=j):
            lax.fori_loop(0, N_EXPERTS, functools.partial(more_chunks, j), 0)

    l_idx = lax.broadcasted_iota(jnp.int32, (t, MOE_SORT_ROWS), 1)

    def scatter_matrix(j):
        mcol = mcol_ref[j * t:(j + 1) * t, :]
        d1c = mcol[:, 0:1].astype(jnp.int32)
        d2c = mcol[:, 1:2].astype(jnp.int32)
        comb = jnp.where(l_idx == d1c, mcol[:, 2:3], 0.0) + jnp.where(l_idx == d2c, mcol[:, 3:4], 0.0)
        return comb.astype(BF16)

    combs = [scatter_matrix(j) for j in range(n_blk)]
    moes = [_dot(combs[j], os[j, 0:MOE_SORT_ROWS, :]) for j in range(n_blk)]
    for j in range(n_blk):
        rows = slice(j * t, (j + 1) * t)
        y_ref[rows, :] = _rmsnorm(xp_ref[rows, :] + moes[j], gf_ref[...])

    @pl.when(pl.program_id(0) == 0)
    def _():
        w2_all = w2_ref[...].reshape(N_EXPERTS * D_EXPERT, D_MODEL)
        ys = sx_ref[...] + jnp.dot(shid_ref[...], w2_all, preferred_element_type=F32)
        sy_ref[...] = _rmsnorm(ys, gf_ref[...])


def _moe_down(tab, hs_sorted, xp, mcol, w2_b, gf, s_hid, s_x):
    n = xp.shape[0]
    n_s = s_x.shape[0]
    nblk = n // MOE_BLOCK
    g = MOE_DOWN_BLOCKS
    assert nblk % g == 0
    whole = lambda shape: pl.BlockSpec(shape, lambda i, tab: (0,) * len(shape))
    row_blk = lambda w: pl.BlockSpec((g * MOE_BLOCK, w), lambda i, tab: (i, 0))
    return pl.pallas_call(
        functools.partial(_moe_down_kernel, flags_at=nblk * N_EXPERTS * 3),
        grid_spec=pltpu.PrefetchScalarGridSpec(
            num_scalar_prefetch=1,
            grid=(nblk // g,),
            in_specs=[pl.BlockSpec((g, MOE_ROWS, D_EXPERT), lambda i, tab: (i, 0, 0)),
                      row_blk(D_MODEL), row_blk(ROUTER_LANES),
                      pl.BlockSpec(w2_b.shape, lambda i, tab: (0, 0, 0), pipeline_mode=pl.Buffered(1)),
                      whole((1, D_MODEL)), whole(s_hid.shape), whole(s_x.shape)],
            out_specs=[row_blk(D_MODEL), whole((n_s, D_MODEL))],
            scratch_shapes=[pltpu.VMEM((g, MOE_ROWS, D_MODEL), BF16)]),
        out_shape=[jax.ShapeDtypeStruct((n, D_MODEL), F32), jax.ShapeDtypeStruct((n_s, D_MODEL), F32)],
        compiler_params=pltpu.CompilerParams(
            dimension_semantics=("arbitrary",), vmem_limit_bytes=V7X_VMEM_LIMIT),
        name="moe_down_combine",
    )(tab, hs_sorted, xp, mcol, w2_b, gf, s_hid, s_x)


def _sample_proj_kernel(x_ref, g1_ref, w_ref, cos_ref, sin_ref, gv_ref, ones_ref, w00_ref, b0_ref,
                        rep_ref, foldt_ref, a_ref, k_ref, v_ref, vn_ref, qkvt_ref):
    h = _rmsnorm(x_ref[...], g1_ref[...])

    def proj(i):
        return _dot(h, w_ref[:, i * WIDTH:(i + 1) * WIDTH], precise=True)

    cos = _tile_lanes(cos_ref[...], WIDTH // 128)
    sin = _tile_lanes(sin_ref[...], WIDTH // 128)
    q = _rope(proj(2), cos, sin) * (HEAD_DIM ** -0.5)
    k = _rope(proj(3), cos, sin)
    v = proj(4)
    vn = _group_rmsnorm(proj(1), ones_ref[...], gv_ref[...], precise=True)
    a_ref[...] = proj(0) * (w00_ref[...] * vn + b0_ref[...])
    k_ref[...] = k
    v_ref[...] = v
    vn_ref[...] = vn

    n_rep = rep_ref.shape[0]
    r_idx = lax.broadcasted_iota(jnp.int32, (n_rep, WIDTH), 0)
    l_idx = lax.broadcasted_iota(jnp.int32, (n_rep, WIDTH), 1)
    own = (l_idx // HEAD_DIM) == (r_idx % N_HEADS)
    for t, src in enumerate((q, k, v)):
        rep = _dot(rep_ref[...], src, precise=True)
        qkvt_ref[t * HEAD_DIM:(t + 1) * HEAD_DIM, :] = lax.dot_general(
            foldt_ref[...], jnp.where(own, rep, 0.0), (((1,), (1,)), ((), ())),
            preferred_element_type=F32, precision=lax.Precision.HIGHEST)


def _sample_proj(x, g1, w_in, cos, sin, gv, ones_bd, w00, b0, rep, foldt):
    bd = x.shape[0]
    sds = lambda r, c: jax.ShapeDtypeStruct((r, c), F32)
    return pl.pallas_call(
        _sample_proj_kernel,
        out_shape=[sds(bd, WIDTH), sds(bd, WIDTH), sds(bd, WIDTH), sds(bd, WIDTH),
                   sds(3 * HEAD_DIM, bd * N_HEADS)],
        compiler_params=pltpu.CompilerParams(vmem_limit_bytes=V7X_VMEM_LIMIT),
        name="sample_proj",
    )(x, g1, w_in, cos, sin, gv, ones_bd, w00, b0, rep, foldt)


def _sample_scores(qkvt_ref, k_ref, b, win):
    n_col = qkvt_ref.shape[1]
    c_idx = lax.broadcasted_iota(jnp.int32, (n_col, 128), 0)
    l_idx = lax.broadcasted_iota(jnp.int32, (n_col, 128), 1)
    pick = jnp.where((c_idx == b * N_HEADS + l_idx) & (l_idx < N_HEADS), 1.0, 0.0)
    cols = _dot(qkvt_ref[...], pick, precise=True)

    head = lax.broadcasted_iota(jnp.int32, (N_HEADS, win), 0)
    s = jnp.zeros((N_HEADS, win), F32)
    s_self = jnp.zeros((N_HEADS, 1), F32)
    for h in range(N_HEADS):
        qc = cols[0:HEAD_DIM, h:h + 1]
        kc = cols[HEAD_DIM:2 * HEAD_DIM, h:h + 1]
        s_h = jnp.sum(k_ref[0, h] * qc, axis=0, keepdims=True)
        s = jnp.where(head == h, s_h, s)
        s_self = jnp.where(head[:, 0:1] == h, jnp.sum(qc * kc, axis=0, keepdims=True), s_self)
    return s, s_self, cols[2 * HEAD_DIM:3 * HEAD_DIM, :]


def _sample_finish(s, s_self, v_cols, v_ref, o_ref, win):
    dist = win - lax.broadcasted_iota(jnp.int32, (1, win), 1)
    members = [(dist <= N_KEYS * dil) & (dist % dil == 0) for dil in DILATIONS]
    es, e_selfs, dens, lses = [], [], [], []
    for mem in members:
        sm = jnp.where(mem, s, NEG)
        m = jnp.maximum(jnp.max(sm, axis=1, keepdims=True), s_self)
        e = jnp.exp(sm - m)
        e_self = jnp.exp(s_self - m)
        den = jnp.sum(e, axis=1, keepdims=True) + e_self
        es.append(e)
        e_selfs.append(e_self)
        dens.append(den)
        lses.append(m + jnp.log(den))
    top = jnp.maximum(jnp.maximum(lses[0], lses[1]), lses[2])
    ws = [jnp.exp(l - top) for l in lses]
    wsum = ws[0] + ws[1] + ws[2]
    coef = [w / (den * wsum) for w, den in zip(ws, dens)]
    p_keys = coef[0] * es[0] + coef[1] * es[1] + coef[2] * es[2]
    p_self = coef[0] * e_selfs[0] + coef[1] * e_selfs[1] + coef[2] * e_selfs[2]

    for h in range(N_HEADS):
        o_ref[0, :, h:h + 1] = (jnp.sum(v_ref[0, h] * p_keys[h:h + 1, :], axis=1, keepdims=True)
                                + p_self[h:h + 1, :] * v_cols[:, h:h + 1])


def _rope_tables(first_pos, count):
    half = HEAD_DIM // 2
    inv = ROPE_THETA ** (-np.arange(half, dtype=np.float64) * 2.0 / HEAD_DIM)
    ang = (first_pos + np.arange(count, dtype=np.float64))[:, None] * inv[None, :]
    cos, sin = np.cos(ang), np.sin(ang)
    cos128 = np.concatenate([cos, cos, cos, cos], axis=1).astype(np.float32)
    sin128 = np.concatenate([-sin, sin, -sin, sin], axis=1).astype(np.float32)
    return jnp.asarray(cos128), jnp.asarray(sin128)


def kernel(x_prompt, x_sample, cache_win_k, cache_win_v, ln1_g, w_in, sgu_norm_g, sgu_w, sgu_b, w_out, ln2_g,
           w_router_group, b_router_group, w_router_expert, b_router_expert, w_gate, w_up, w_down, lnf_g):
    depth = w_in.shape[0]
    assert depth == 1 and x_sample.shape[1] == 1
    B, S, _ = x_prompt.shape
    bd = x_sample.shape[0]
    win = cache_win_k.shape[2]
    assert S % (max(DILATIONS) * CHUNK) == 0 and win >= max(DILATIONS) * N_KEYS and PAST_LEN % CHUNK == 0
    l = 0

    w1_b =jnp.concatenate([w_gate[l], w_up[l]], axis=-1).astype(BF16)
    w2_b = w_down[l].astype(BF16)
    pad = ROUTER_LANES - N_GROUPS - N_EXPERTS
    w_router = jnp.pad(jnp.concatenate([w_router_group[l], w_router_expert[l]], axis=1), ((0, 0), (0, pad)))
    b_router = jnp.pad(jnp.concatenate([b_router_group[l], b_router_expert[l]]), (0, pad))[None, :]
    g1 = ln1_g[l][None, :]
    g2 = ln2_g[l][None, :]
    gf = lnf_g[None, :]
    gv = sgu_norm_g[l].reshape(1, WIDTH)
    grp = jnp.arange(WIDTH) // HEAD_DIM
    ones_bd = jnp.where(grp[:, None] == grp[None, :], 1.0 / HEAD_DIM, 0.0).astype(BF16)
    wp = jnp.concatenate([sgu_w[l][0::2], sgu_w[l][1::2]], axis=-1)
    bias = jnp.repeat(sgu_b[l].T, HEAD_DIM, axis=1)
    w00 = jnp.repeat(sgu_w[l][:, 0, 0], HEAD_DIM)[None, :]
    b0 = jnp.repeat(sgu_b[l][:, 0], HEAD_DIM)[None, :]

    cos_s, sin_s = _rope_tables(PAST_LEN, 1)
    rep = (jnp.arange(bd * N_HEADS)[:, None] // N_HEADS == jnp.arange(bd)[None, :]).astype(F32)
    foldt = (jnp.arange(HEAD_DIM)[:, None] == jnp.arange(WIDTH)[None, :] % HEAD_DIM).astype(F32)
    xs = x_sample.reshape(bd, D_MODEL)
    a_s, k_s, v_s, vn_s, qkvt = _sample_proj(xs, g1, w_in[l], cos_s, sin_s, gv, ones_bd, w00, b0, rep, foldt)
    to_pos_minor = lambda c: jnp.transpose(c, (0, 2, 3, 1))

    cos_p, sin_p = _rope_tables(0, S)
    a_p, q_p, k_p, v_p, kt_p, vt_p, o3 = _prompt_proj(
        x_prompt, g1, w_in[l], cos_p, sin_p, gv, ones_bd, wp, bias,
        qkvt, to_pos_minor(cache_win_k[l]), to_pos_minor(cache_win_v[l]))
    b_p = _prompt_attention(q_p, k_p, v_p)
    n = B * S
    assert n % MOE_BLOCK == 0
    wr_t = w_router.T[:ROUTER_ROWS]
    wr_hi = wr_t.astype(BF16)
    wr_hl = jnp.concatenate([wr_hi, (wr_t - wr_hi.astype(F32)).astype(BF16)], axis=0)
    lane_idx = jnp.arange(ROUTER_LANES)
    tri = (lane_idx[:, None] <= lane_idx[None, :]).astype(BF16)
    ltri = (lane_idx[None, :] < lane_idx[:, None]).astype(BF16)
    xp2, h2, mrow, mcol, tab_f = _mix_route_sort(
        a_p.reshape(n, WIDTH), b_p.reshape(n, WIDTH), x_prompt.reshape(n, D_MODEL),
        w_out[l], g2, wr_hl, b_router.reshape(ROUTER_LANES, 1)[:ROUTER_ROWS], tri, ltri)
    tab_i = tab_f[:, :N_EXPERTS, 0:3].astype(jnp.int32)
    multi_chunk = (jnp.max(tab_i[:, :, 1], axis=1) > 1).astype(jnp.int32)
    tab = jnp.concatenate([tab_i.reshape(-1), multi_chunk])

    b_s = jnp.transpose(o3, (0, 2, 1)).reshape(bd, WIDTH)
    xs2, hs2, gates_s = _mix_router(a_s, b_s, xs, w_out[l], g2, w_router, b_router, tm=bd, precise=True)

    hid_sorted, hid_s = _moe_gate_up(tab, h2, mrow, w1_b, hs2, gates_s)
    y_prompt, y_sample = _moe_down(tab, hid_sorted, xp2, mcol, w2_b, gf, hid_s, xs2)
    y_prompt = y_prompt.reshape(B, S, D_MODEL)
    y_sample = y_sample.reshape(bd, 1, D_MODEL)
    buf_p = min(MAX_WINDOW, S)
    to_win = lambda t: jnp.transpose(t.reshape(1, B, N_HEADS, HEAD_DIM, buf_p), (0, 1, 4, 2, 3))
    new_k_p = to_win(kt_p)
    new_v_p = to_win(vt_p)

    shape_s = (1, bd, 1, N_HEADS, HEAD_DIM)
    return (y_prompt, y_sample, new_k_p, new_v_p,
            k_s.reshape(shape_s), v_s.reshape(shape_s), vn_s.reshape(shape_s))
```

```python
import functools

import jax
import jax.numpy as jnp
import numpy as np
from jax import lax
from jax.experimental import pallas as pl
from jax.experimental.pallas import tpu as pltpu

F32 = jnp.float32
BF16 = jnp.bfloat16

D_MODEL = 1024
HEAD_DIM = 64
N_HEADS = 8
WIDTH = N_HEADS * HEAD_DIM
PROJ_COLS = 5 * WIDTH
CHUNK = 128
DILATIONS = (1, 4, 16)
N_KEYS = 128
MAX_WINDOW = 2048
PAST_LEN = 16384
ROPE_THETA = 10000.0
N_GROUPS = 4
EXPERTS_PER_GROUP = 8
N_EXPERTS = N_GROUPS * EXPERTS_PER_GROUP
D_EXPERT = 128
EPS = 1e-6
NEG = -1e30
TILES_PER_STEP = 32
SUBLANE_STRIDE = 4
assert DILATIONS == (1, SUBLANE_STRIDE, SUBLANE_STRIDE ** 2)
MOE_BLOCK = 512
MOE_ROW_ALIGN = 16
MOE_CHUNK = 64
MOE_EXPERTS_PER_STEP = 16
MOE_ROUTE_BLOCKS = 2
MOE_GATE_UP_BLOCKS = 4
MOE_DOWN_BLOCKS = 2
MOE_SORT_ROWS = -(-(2 * MOE_BLOCK + N_EXPERTS * (MOE_ROW_ALIGN - 1)) // 512) * 512
MOE_ROWS = MOE_SORT_ROWS + MOE_CHUNK
ROUTER_ROWS = 48
ROUTER_LANES = 128
V7X_VMEM_LIMIT = 56 * 1024 * 1024


def _rmsnorm(x, g):
    return x * lax.rsqrt(jnp.mean(x * x, axis=-1, keepdims=True) + EPS) * g


def _tile_lanes(t, reps):
    return jnp.concatenate([t] * reps, axis=1)


def _rope(t, cos, sin_signed):
    lane = lax.broadcasted_iota(jnp.int32, t.shape, 1)
    first_half = (lane % HEAD_DIM) < (HEAD_DIM // 2)
    n = t.shape[1]
    partner = jnp.where(first_half, pltpu.roll(t, n - HEAD_DIM // 2, 1), pltpu.roll(t, HEAD_DIM // 2, 1))
    return t * cos + partner * sin_signed


def _dot(a, b, precise=False):
    if precise:
        return jnp.dot(a.astype(F32), b.astype(F32), preferred_element_type=F32,
                       precision=lax.Precision.HIGHEST)
    return jnp.dot(a.astype(BF16), b.astype(BF16), preferred_element_type=F32)


def _group_rmsnorm(va, ones_bd, gv, precise=False):
    ms = _dot(va * va, ones_bd, precise)
    return va * lax.rsqrt(ms + EPS) * gv


def _proj_kernel(x_ref, g1_ref, w_ref, cos_ref, sin_ref, gv_ref, ones_ref, wp_ref, bias_ref,
                 qkvt_ref, ck_ref, cv_ref,
                 a_ref, q_ref, k_ref, v_ref, kt_ref, vt_ref, so_ref, wb, *, tm, first_win_tile, n_seq, win):
    @pl.when((pl.program_id(0) == 0) & (pl.program_id(1) == 0))
    def _():
        wb[...] = w_ref[...].astype(BF16)

    h = _rmsnorm(x_ref[0], g1_ref[...]).astype(BF16)

    def proj(i):
        return jnp.dot(h, wb[:, i * WIDTH:(i + 1) * WIDTH], preferred_element_type=F32)

    cos = _tile_lanes(cos_ref[...], WIDTH // 128)
    sin = _tile_lanes(sin_ref[...], WIDTH // 128)
    q_ref[0] = _rope(proj(2), cos, sin) * (HEAD_DIM ** -0.5)
    k_ref[0] = _rope(proj(3), cos, sin)
    v_ref[0] = proj(4)

    s_seq = jnp.minimum(pl.program_id(0) * pl.num_programs(1) + pl.program_id(1), n_seq - 1)
    s_scores, s_self, s_vcols = _sample_scores(qkvt_ref, ck_ref, s_seq, win)

    u = proj(0)
    vn = _group_rmsnorm(proj(1), ones_ref[...], gv_ref[...]).astype(BF16)

    lane = lax.broadcasted_iota(jnp.int32, (CHUNK, 128), 1)
    left = lane < HEAD_DIM
    row = lax.broadcasted_iota(jnp.int32, (CHUNK, 2 * CHUNK), 0)
    col = lax.broadcasted_iota(jnp.int32, (CHUNK, 2 * CHUNK), 1)
    causal = (col % CHUNK) <= row
    zero = jnp.zeros((CHUNK, 128), BF16)
    wps = [jnp.where(causal, wp_ref[gp], 0.0).astype(BF16) for gp in range(N_HEADS // 2)]
    def block_diag(vv):
        return jnp.concatenate([jnp.where(left, vv, zero), jnp.where(left, zero, vv)], axis=0)

    for c in range(0, tm // CHUNK, 2):
        rows = [slice((c + i) * CHUNK, (c + i + 1) * CHUNK) for i in range(2)]
        mixes = [[], []]
        for gp in range(N_HEADS // 2):
            lanes = slice(gp * 128, (gp + 1) * 128)
            v2 = jnp.concatenate([block_diag(vn[rows[0], lanes]), block_diag(vn[rows[1], lanes])], axis=1)
            both = jnp.dot(wps[gp], v2, preferred_element_type=F32)
            mixes[0].append(both[:, 0:128])
            mixes[1].append(both[:, 128:256])
        for i in range(2):
            mix = jnp.concatenate(mixes[i], axis=1) + bias_ref[...]
            a_ref[0, rows[i], :] = (u[rows[i], :] * mix).astype(a_ref.dtype)

    _sample_finish(s_scores, s_self, s_vcols, cv_ref, so_ref, win)

    @pl.when(pl.program_id(1) >= first_win_tile)
    def _():
        kt_ref[0] = k_ref[0].T
        vt_ref[0] = v_ref[0].T


def _prompt_proj(x, g1, w_in_b, cos, sin, gv, ones_bd, wp, bias, qkvt, cache_k_t, cache_v_t, *, tm=512):
    B, S, _ = x.shape
    const2 = lambda b, j: (0, 0)
    out_sds = lambda dt: jax.ShapeDtypeStruct((B, S, WIDTH), dt)
    tile = pl.BlockSpec((1, tm, WIDTH), lambda b, j: (b, j, 0))
    win = min(MAX_WINDOW, S)
    first_win_tile = (S - win) // tm
    tile_t = pl.BlockSpec((1, WIDTH, tm), lambda b, j: (b, 0, jnp.maximum(j - first_win_tile, 0)))
    win_sds = jax.ShapeDtypeStruct((B, WIDTH, win), F32)
    n_seq, _, _, cache_win = cache_k_t.shape
    n_tiles = S // tm
    assert n_seq <= B * n_tiles, "one sample sequence rides on each grid step"
    seq_blk = lambda shape: pl.BlockSpec(
        (1,) + shape, lambda b, j: (jnp.minimum(b * n_tiles + j, n_seq - 1),) + (0,) * len(shape))
    cache_blk = seq_blk((N_HEADS, HEAD_DIM, cache_win))
    return pl.pallas_call(
        functools.partial(_proj_kernel, tm=tm, first_win_tile=first_win_tile, n_seq=n_seq, win=cache_win),
        grid=(B, n_tiles),
        in_specs=[
            pl.BlockSpec((1, tm, D_MODEL), lambda b, j: (b, j, 0)),
            pl.BlockSpec((1, D_MODEL), const2),
            pl.BlockSpec((D_MODEL, PROJ_COLS), const2, pipeline_mode=pl.Buffered(1)),
            pl.BlockSpec((tm, 128), lambda b, j: (j, 0)),
            pl.BlockSpec((tm, 128), lambda b, j: (j, 0)),
            pl.BlockSpec((1, WIDTH), const2),
            pl.BlockSpec((WIDTH, WIDTH), const2),
            pl.BlockSpec((N_HEADS // 2, CHUNK, 2 * CHUNK), lambda b, j: (0, 0, 0)),
            pl.BlockSpec((CHUNK, WIDTH), const2),
            pl.BlockSpec(qkvt.shape, const2), cache_blk, cache_blk,
        ],
        out_specs=[tile, tile, tile, tile, tile_t, tile_t, seq_blk((HEAD_DIM, N_HEADS))],
        out_shape=[out_sds(BF16), out_sds(F32), out_sds(F32), out_sds(F32), win_sds, win_sds,
                   jax.ShapeDtypeStruct((n_seq, HEAD_DIM, N_HEADS), F32)],
        scratch_shapes=[pltpu.VMEM((D_MODEL, PROJ_COLS), BF16)],
        compiler_params=pltpu.CompilerParams(
            dimension_semantics=("arbitrary", "arbitrary"), vmem_limit_bytes=V7X_VMEM_LIMIT),
        name="prompt_proj_sgu",
    )(x, g1, w_in_b, cos, sin, gv, ones_bd, wp, bias, qkvt, cache_k_t, cache_v_t)


def _attn_kernel(q_ref, k_ref, v_ref, o_ref, qd, kd, vd, res_o, res_l, nat_o, nat_l, bias, stage, *, seq):
    n_tiles = seq // CHUNK
    lane = lax.broadcasted_iota(jnp.int32, (CHUNK, 128), 1)
    left = lane < HEAD_DIM
    qi2 = lax.broadcasted_iota(jnp.int32, (2 * CHUNK, 2 * CHUNK), 0) % CHUNK
    kj2 = lax.broadcasted_iota(jnp.int32, (2 * CHUNK, 2 * CHUNK), 1)
    dist2 = CHUNK + qi2 - kj2
    band2 = (dist2 >= 0) & (dist2 <= N_KEYS)
    zero_q = jnp.zeros((CHUNK, 128), BF16)
    bias[0] = jnp.where(band2, 0.0, NEG)
    bias[1] = jnp.where(band2 & (kj2 >= CHUNK), 0.0, NEG)

    kd[0:CHUNK, :] = jnp.zeros((CHUNK, 128), BF16)
    vd[0:CHUNK, :] = jnp.zeros((CHUNK, 128), BF16)

    for p, dil in enumerate(DILATIONS):
        sub = seq // dil
        nb = sub // CHUNK
        for ti, (src_ref, dst, pad) in enumerate(((q_ref, qd, 0), (k_ref, kd, CHUNK), (v_ref, vd, CHUNK))):
            if dil == 1:
                dst[pad:pad + seq, :] = src_ref[0].astype(BF16)
            elif dil == SUBLANE_STRIDE:
                for r in range(dil):
                    val = src_ref[0, pl.ds(r, sub, stride=dil), :]
                    stage[ti, r * sub:(r + 1) * sub, :] = val
                    dst[pad + r * sub:pad + (r + 1) * sub, :] = val.astype(BF16)
            else:
                coarse = seq // SUBLANE_STRIDE
                for r_lo in range(SUBLANE_STRIDE):
                    for r_hi in range(SUBLANE_STRIDE):
                        r = r_lo + SUBLANE_STRIDE * r_hi
                        val = stage[ti, pl.ds(r_lo * coarse + r_hi, sub, stride=SUBLANE_STRIDE), :]
                        dst[pad + r * sub:pad + (r + 1) * sub, :] = val.astype(BF16)

        def tile_body(g, i, p=p, nb=nb):
            t = g * TILES_PER_STEP + i
            row = pl.multiple_of(t * CHUNK, CHUNK)
            qt = qd[pl.ds(row, CHUNK), :]
            k2 = kd[pl.ds(row, 2 * CHUNK), :]
            v2 = vd[pl.ds(row, 2 * CHUNK), :]
            if TILES_PER_STEP % nb == 0:
                variant = 1 if i % nb == 0 else 0
            elif i == 0:
                variant = jnp.where((g * TILES_PER_STEP) % nb == 0, 1, 0)
            else:
                variant = 0
            q2 = jnp.concatenate([jnp.where(left, qt, zero_q), jnp.where(left, zero_q, qt)], axis=0)
            s = lax.dot_general(q2, k2, (((1,), (1,)), ((), ())), preferred_element_type=F32)
            s = s + bias[variant]
            m = jnp.max(s, axis=1, keepdims=True)
            e = jnp.exp(s - m)
            den = jnp.sum(e, axis=1, keepdims=True)
            pv = jnp.dot(e.astype(BF16), v2, preferred_element_type=F32) / den
            lse = jnp.broadcast_to(m + jnp.log(den), (2 * CHUNK, 128))
            res_o[p, pl.ds(row, CHUNK), :] = jnp.where(left, pv[0:CHUNK], pv[CHUNK:2 * CHUNK])
            res_l[p, pl.ds(row, CHUNK), :] = jnp.where(left, lse[0:CHUNK], lse[CHUNK:2 * CHUNK])

        def group_body(g, carry, tile_body=tile_body):
            for i in range(TILES_PER_STEP):
                tile_body(g, i)
            return carry

        lax.fori_loop(0, n_tiles // TILES_PER_STEP, group_body, 0)

    for p, dil in enumerate(DILATIONS):
        if dil == 1:
            continue
        sub = seq // dil
        for si, (res, nat) in enumerate(((res_o, nat_o), (res_l, nat_l))):
            if dil == SUBLANE_STRIDE:
                for r in range(dil):
                    nat[p - 1, pl.ds(r, sub, stride=dil), :] = res[p, r * sub:(r + 1) * sub, :]
            else:
                coarse = seq // SUBLANE_STRIDE
                for r_lo in range(SUBLANE_STRIDE):
                    for r_hi in range(SUBLANE_STRIDE):
                        r = r_lo + SUBLANE_STRIDE * r_hi
                        stage[si, pl.ds(r_lo * coarse + r_hi, sub, stride=SUBLANE_STRIDE), :] = (
                            res[p, r * sub:(r + 1) * sub, :])
                for r_lo in range(SUBLANE_STRIDE):
                    nat[p - 1, pl.ds(r_lo, coarse, stride=SUBLANE_STRIDE), :] = (
                        stage[si, r_lo * coarse:(r_lo + 1) * coarse, :])

    rows_per_step = 256

    def merge_body(c, carry):
        rows = pl.ds(pl.multiple_of(c * rows_per_step, rows_per_step), rows_per_step)
        l0, l1, l2 = res_l[0, rows, :], nat_l[0, rows, :], nat_l[1, rows, :]
        top = jnp.maximum(jnp.maximum(l0, l1), l2)
        w0, w1, w2 = jnp.exp(l0 - top), jnp.exp(l1 - top), jnp.exp(l2 - top)
        num = w0 * res_o[0, rows, :] + w1 * nat_o[0, rows, :] + w2 * nat_o[1, rows, :]
        o_ref[0, rows, :] = (num / (w0 + w1 + w2)).astype(o_ref.dtype)
        return carry

    lax.fori_loop(0, seq // rows_per_step, merge_body, 0)


def _prompt_attention(q, k, v):
    B, S, _ = q.shape
    blk = pl.BlockSpec((1, S, 128), lambda b, hp: (b, 0, hp))
    return pl.pallas_call(
        functools.partial(_attn_kernel, seq=S),
        grid=(B, WIDTH // 128),
        in_specs=[blk, blk, blk],
        out_specs=blk,
        out_shape=jax.ShapeDtypeStruct((B, S, WIDTH), BF16),
        scratch_shapes=[
            pltpu.VMEM((S, 128), BF16),
            pltpu.VMEM((S + CHUNK, 128), BF16),
            pltpu.VMEM((S + CHUNK, 128), BF16),
            pltpu.VMEM((len(DILATIONS), S, 128), F32),
            pltpu.VMEM((len(DILATIONS), S, 128), F32),
            pltpu.VMEM((len(DILATIONS) - 1, S, 128), F32),
            pltpu.VMEM((len(DILATIONS) - 1, S, 128), F32),
            pltpu.VMEM((2, 2 * CHUNK, 2 * CHUNK), F32),
            pltpu.VMEM((3, S, 128), F32),
        ],
        compiler_params=pltpu.CompilerParams(
            dimension_semantics=("arbitrary", "arbitrary"), vmem_limit_bytes=V7X_VMEM_LIMIT),
        name="prompt_dilated_attention",
    )(q, k, v)


def _route(logits):
    lane = lax.broadcasted_iota(jnp.int32, logits.shape, 1)
    big = jnp.int32(ROUTER_LANES)
    lg = jnp.where(lane < N_GROUPS, logits, NEG)
    gmax = jnp.max(lg, axis=1, keepdims=True)
    gp = 1.0 / jnp.sum(jnp.exp(lg - gmax), axis=1, keepdims=True)
    gi = jnp.min(jnp.where(lg == gmax, lane, big), axis=1, keepdims=True)
    lo = N_GROUPS + EXPERTS_PER_GROUP * gi
    le = jnp.where((lane >= lo) & (lane < lo + EXPERTS_PER_GROUP), logits, NEG)
    m1 = jnp.max(le, axis=1, keepdims=True)
    i1 = jnp.min(jnp.where(le == m1, lane, big), axis=1, keepdims=True)
    le2 = jnp.where(lane == i1, NEG, le)
    m2 = jnp.max(le2, axis=1, keepdims=True)
    i2 = jnp.min(jnp.where(le2 == m2, lane, big), axis=1, keepdims=True)
    e2 = jnp.exp(m2 - m1)
    w1 = 1.0 / (1.0 + e2)
    w2 = e2 / (1.0 + e2)
    return jnp.where(lane == i1, gp * w1, jnp.where(lane == i2, gp * w2, 0.0))


def _mix_router_kernel(a_ref, b_ref, x_ref, wo_ref, g2_ref, wr_ref, br_ref, xp_ref, h2_ref, gates_ref, *, precise):
    mixed = (_dot(a_ref[...], wo_ref[0:WIDTH, :], precise)
             + _dot(b_ref[...], wo_ref[WIDTH:2 * WIDTH, :], precise))
    xp = x_ref[...] + mixed
    xp_ref[...] = xp
    h2 = _rmsnorm(xp, g2_ref[...])
    h2_ref[...] = h2.astype(h2_ref.dtype)
    logits = jnp.dot(h2, wr_ref[...], preferred_element_type=F32,
                     precision=lax.Precision.HIGHEST) + br_ref[...]
    gates_ref[...] = _route(logits)


def _mix_router(a, b, x, w_out, g2, w_router, b_router, *, tm, precise=False):
    n = x.shape[0]
    const = lambda i: (0, 0)
    row_blk = lambda w: pl.BlockSpec((tm, w), lambda i: (i, 0))
    return pl.pallas_call(
        functools.partial(_mix_router_kernel, precise=precise),
        grid=(n // tm,),
        in_specs=[row_blk(WIDTH), row_blk(WIDTH), row_blk(D_MODEL),
                  pl.BlockSpec((2 * WIDTH, D_MODEL), const),
                  pl.BlockSpec((1, D_MODEL), const),
                  pl.BlockSpec((D_MODEL, ROUTER_LANES), const),
                  pl.BlockSpec((1, ROUTER_LANES), const)],
        out_specs=[row_blk(D_MODEL), row_blk(D_MODEL), row_blk(ROUTER_LANES)],
        out_shape=[jax.ShapeDtypeStruct((n, D_MODEL), F32),
                   jax.ShapeDtypeStruct((n, D_MODEL), BF16),
                   jax.ShapeDtypeStruct((n, ROUTER_LANES), F32)],
        compiler_params=pltpu.CompilerParams(
            dimension_semantics=("arbitrary",), vmem_limit_bytes=V7X_VMEM_LIMIT),
        name="outproj_router",
    )(a, b, x, w_out, g2, w_router, b_router)


def _nt_dot(w, t):
    return lax.dot_general(w, t, (((1,), (1,)), ((), ())), preferred_element_type=F32)


def _route_t(logits_t):
    row = lax.broadcasted_iota(jnp.int32, logits_t.shape, 0)
    big = jnp.int32(ROUTER_LANES)
    lg = jnp.where(row < N_GROUPS, logits_t, NEG)
    gmax = jnp.max(lg, axis=0, keepdims=True)
    gp = 1.0 / jnp.sum(jnp.exp(lg - gmax), axis=0, keepdims=True)
    gi = jnp.min(jnp.where(lg == gmax, row, big), axis=0, keepdims=True)
    lo = N_GROUPS + EXPERTS_PER_GROUP * gi
    le = jnp.where((row >= lo) & (row < lo + EXPERTS_PER_GROUP), logits_t, NEG)
    m1 = jnp.max(le, axis=0, keepdims=True)
    i1 = jnp.min(jnp.where(le == m1, row, big), axis=0, keepdims=True)
    le2 = jnp.where(row == i1, NEG, le)
    m2 = jnp.max(le2, axis=0, keepdims=True)
    i2 = jnp.min(jnp.where(le2 == m2, row, big), axis=0, keepdims=True)
    e2 = jnp.exp(m2 - m1)
    return i1 - N_GROUPS, i2 - N_GROUPS, gp / (1.0 + e2), gp * e2 / (1.0 + e2)


def _mix_route_sort_kernel(a_ref, b_ref, x_ref, wo_ref, g2_ref, wr_ref, brc_ref, tri_ref, ltri_ref,
                           wg_ref, wu_ref, wd_ref,
                           xp_ref, h2_ref, mrow_ref, mcol_ref, tab_ref, w1_ref, w2_ref, wob):
    t = MOE_BLOCK

    @pl.when(pl.program_id(0) == 0)
    def _():
        wob[...] = wo_ref[...].astype(BF16)

    def project(j):
        rows = slice(j * t, (j + 1) * t)
        xp = x_ref[rows, :] + _dot(a_ref[rows, :], wob[0:WIDTH, :]) + _dot(b_ref[rows, :], wob[WIDTH:2 * WIDTH, :])
        xp_ref[rows, :] = xp
        h2 = _rmsnorm(xp, g2_ref[...])
        hi = h2.astype(BF16)
        h2_ref[rows, :] = hi
        return hi, (h2 - hi.astype(F32)).astype(BF16)

    def route(hi, lo):
        prod_hi = _nt_dot(wr_ref[...], hi)
        logits_t = (prod_hi[0:ROUTER_ROWS] + prod_hi[ROUTER_ROWS:2 * ROUTER_ROWS]
                    + _nt_dot(wr_ref[0:ROUTER_ROWS, :], lo) + brc_ref[...])
        return _route_t(logits_t)

    def sort_meta(j, ex1, ex2, gate1, gate2):
        rows = slice(j * t, (j + 1) * t)
        pair_e = jnp.concatenate([ex1, ex2], axis=1)
        row = lax.broadcasted_iota(jnp.int32, (N_EXPERTS, 2 * t), 0)
        onehot = jnp.where(row == pair_e, 1.0, 0.0)
        n_lane_tiles = 2 * t // 128
        local = _dot(jnp.concatenate([onehot[:, k * 128:(k + 1) * 128] for k in range(n_lane_tiles)], axis=0),
                     tri_ref[...])
        carry = jnp.zeros((N_EXPERTS, 1), F32)
        cums = []
        for k in range(n_lane_tiles):
            tile = local[k * N_EXPERTS:(k + 1) * N_EXPERTS, :]
            cums.append(tile + carry)
            carry = carry + tile[:, 127:128]
        cum = jnp.concatenate(cums, axis=1)
        rank = jnp.sum(onehot * cum, axis=0, keepdims=True) - 1.0
        counts = carry
        units32 = jnp.floor((counts + (MOE_ROW_ALIGN - 1)) * (1.0 / MOE_ROW_ALIGN))
        units = jnp.concatenate([jnp.broadcast_to(units32, (N_EXPERTS, 128)),
                                 jnp.zeros((ROUTER_LANES - N_EXPERTS, 128), F32)], axis=0)
        off = _dot(ltri_ref[...], units) * MOE_ROW_ALIGN
        dst = jnp.sum(onehot * off[0:N_EXPERTS, 0:1], axis=0, keepdims=True) + rank

        r8 = lax.broadcasted_iota(jnp.int32, (8, t), 0)
        mrow_ref[j] = jnp.where(r8 == 0, dst[:, 0:t], jnp.where(r8 == 1, dst[:, t:2 * t],
                                jnp.where(r8 == 2, gate1, jnp.where(r8 == 3, gate2, 0.0))))
        r128 = lax.broadcasted_iota(jnp.int32, (ROUTER_LANES, t), 0)
        meta = jnp.where(r128 == 0, dst[:, 0:t], jnp.where(r128 == 1, dst[:, t:2 * t],
                         jnp.where(r128 == 2, gate1, jnp.where(r128 == 3, gate2, 0.0))))
        mcol_ref[rows, :] = meta.T
        lane = lax.broadcasted_iota(jnp.int32, (ROUTER_LANES, 128), 1)
        n_rows = units * MOE_ROW_ALIGN
        chunks = jnp.floor((n_rows + (MOE_CHUNK - 1)) * (1.0 / MOE_CHUNK))
        tab_ref[j] = jnp.where(lane == 0, off, jnp.where(lane == 1, chunks, jnp.where(lane == 2, off + n_rows, 0.0)))

    blocks = range(MOE_ROUTE_BLOCKS)
    projected = [project(j) for j in blocks]
    routed = [route(hi, lo) for hi, lo in projected]
    for j in blocks:
        sort_meta(j, *routed[j])

    w1_ref[:, :, 0:D_EXPERT] = wg_ref[...].astype(BF16)
    w1_ref[:, :, D_EXPERT:2 * D_EXPERT] = wu_ref[...].astype(BF16)
    w2_ref[...] = wd_ref[...].astype(BF16)


def _mix_route_sort(a, b, x, w_out, g2, wr_hl, br_col, tri, ltri, w_gate, w_up, w_down):
    n = x.shape[0]
    t = MOE_BLOCK
    nblk = n // t
    g = MOE_ROUTE_BLOCKS
    assert nblk % g == 0
    n_steps = nblk // g
    assert N_EXPERTS % n_steps == 0
    e_blk = N_EXPERTS // n_steps
    const = lambda i: (0, 0)
    row_blk = lambda w: pl.BlockSpec((g * t, w), lambda i: (i, 0))
    expert_blk = lambda rows, cols: pl.BlockSpec((e_blk, rows, cols), lambda i: (i, 0, 0))
    return pl.pallas_call(
        _mix_route_sort_kernel,
        grid=(n_steps,),
        in_specs=[row_blk(WIDTH), row_blk(WIDTH), row_blk(D_MODEL),
                  pl.BlockSpec((2 * WIDTH, D_MODEL), const, pipeline_mode=pl.Buffered(1)),
                  pl.BlockSpec((1, D_MODEL), const),
                  pl.BlockSpec((2 * ROUTER_ROWS, D_MODEL), const),
                  pl.BlockSpec((ROUTER_ROWS, 1), const),
                  pl.BlockSpec((ROUTER_LANES, ROUTER_LANES), const),
                  pl.BlockSpec((ROUTER_LANES, ROUTER_LANES), const),
                  expert_blk(D_MODEL, D_EXPERT), expert_blk(D_MODEL, D_EXPERT), expert_blk(D_EXPERT, D_MODEL)],
        out_specs=[row_blk(D_MODEL), row_blk(D_MODEL),
                   pl.BlockSpec((g, 8, t), lambda i: (i, 0, 0)),
                   row_blk(ROUTER_LANES),
                   pl.BlockSpec((g, ROUTER_LANES, 128), lambda i: (i, 0, 0)),
                   expert_blk(D_MODEL, 2 * D_EXPERT), expert_blk(D_EXPERT, D_MODEL)],
        out_shape=[jax.ShapeDtypeStruct((n, D_MODEL), F32),
                   jax.ShapeDtypeStruct((n, D_MODEL), BF16),
                   jax.ShapeDtypeStruct((nblk, 8, t), F32),
                   jax.ShapeDtypeStruct((n, ROUTER_LANES), F32),
                   jax.ShapeDtypeStruct((nblk, ROUTER_LANES, 128), F32),
                   jax.ShapeDtypeStruct((N_EXPERTS, D_MODEL, 2 * D_EXPERT), BF16),
                   jax.ShapeDtypeStruct((N_EXPERTS, D_EXPERT, D_MODEL), BF16)],
        scratch_shapes=[pltpu.VMEM((2 * WIDTH, D_MODEL), BF16)],
        compiler_params=pltpu.CompilerParams(
            dimension_semantics=("arbitrary",), vmem_limit_bytes=V7X_VMEM_LIMIT),
        name="outproj_route_sort",
    )(a, b, x, w_out, g2, wr_hl, br_col, tri, ltri, w_gate, w_up, w_down)


def _silu_mul(ab):
    a = ab[:, :D_EXPERT]
    return a * (1.0 / (1.0 + jnp.exp(-a))) * ab[:, D_EXPERT:]


def _chunk_offsets(tab, n_blk, e):
    return [pl.multiple_of(tab(j, e, 0), MOE_ROW_ALIGN) for j in range(n_blk)]


def _moe_gate_up_kernel(tab_ref, h_ref, mrow_ref, w1_ref, sh_ref, sg_ref, hs_ref, shid_ref, xs, *, flags_at):
    t = MOE_BLOCK
    n_blk = MOE_GATE_UP_BLOCKS
    first = pl.program_id(0) * n_blk
    tab = lambda j, e, c: tab_ref[((first + j) * N_EXPERTS + e) * 3 + c]

    piece = 512
    for j in range(n_blk):
        mrow = mrow_ref[j]
        dst1 = mrow[0:1, :].astype(jnp.int32)
        dst2 = mrow[1:2, :].astype(jnp.int32)
        h = h_ref[j * t:(j + 1) * t, :]
        for r0 in range(0, MOE_SORT_ROWS, piece):
            d_idx = lax.broadcasted_iota(jnp.int32, (piece, t), 0) + r0
            sel = jnp.where((d_idx == dst1) | (d_idx == dst2), 1.0, 0.0)
            xs[j, r0:r0 + piece, :] = _dot(sel, h).astype(BF16)
        xs[j, MOE_SORT_ROWS:MOE_ROWS, :] = jnp.zeros((MOE_CHUNK, D_MODEL), BF16)
    hs_ref[...] = jnp.zeros(hs_ref.shape, BF16)

    def first_chunks(g, carry):
        for i in range(MOE_EXPERTS_PER_STEP):
            e = g * MOE_EXPERTS_PER_STEP + i
            offs = _chunk_offsets(tab, n_blk, e)
            x = jnp.concatenate([xs[j, pl.ds(offs[j], MOE_CHUNK), :] for j in range(n_blk)], axis=0)
            hid = _silu_mul(jnp.dot(x, w1_ref[e], preferred_element_type=F32)).astype(BF16)
            for j in range(n_blk):
                hs_ref[j, pl.ds(offs[j], MOE_CHUNK), :] = hid[j * MOE_CHUNK:(j + 1) * MOE_CHUNK, :]
        return carry

    lax.fori_loop(0, N_EXPERTS // MOE_EXPERTS_PER_STEP, first_chunks, 0)

    def more_chunks(j, e, carry):
        off, n_chunks, end = tab(j, e, 0), tab(j, e, 1), tab(j, e, 2)

        def chunk(c, carry):
            r0 = pl.multiple_of(off + c * MOE_CHUNK, MOE_ROW_ALIGN)
            rows = r0 + lax.broadcasted_iota(jnp.int32, (MOE_CHUNK, D_EXPERT), 0)
            hid = _silu_mul(jnp.dot(xs[j, pl.ds(r0, MOE_CHUNK), :], w1_ref[e], preferred_element_type=F32))
            hs_ref[j, pl.ds(r0, MOE_CHUNK), :] = jnp.where(rows < end, hid.astype(BF16),
                                                           hs_ref[j, pl.ds(r0, MOE_CHUNK), :])
            return carry

        return lax.fori_loop(1, n_chunks, chunk, carry)

    for j in range(n_blk):
        @pl.when(tab_ref[flags_at + first + j] > 0)
        def _(j=j):
            lax.fori_loop(0, N_EXPERTS, functools.partial(more_chunks, j), 0)

    @pl.when(pl.program_id(0) == 0)
    def _():
        sh = sh_ref[...]
        gates = sg_ref[...]
        for e in range(N_EXPERTS):
            gate = gates[:, N_GROUPS + e:N_GROUPS + e + 1]
            hid = _silu_mul(jnp.dot(sh, w1_ref[e], preferred_element_type=F32)) * gate
            shid_ref[:, e * D_EXPERT:(e + 1) * D_EXPERT] = hid.astype(BF16)


def _moe_gate_up(tab, h2, mrow, w1_b, s_h, s_gates):
    n = h2.shape[0]
    n_s = s_h.shape[0]
    nblk = n // MOE_BLOCK
    g = MOE_GATE_UP_BLOCKS
    assert nblk % g == 0
    whole = lambda shape: pl.BlockSpec(shape, lambda i, tab: (0,) * len(shape))
    return pl.pallas_call(
        functools.partial(_moe_gate_up_kernel, flags_at=nblk * N_EXPERTS * 3),
        grid_spec=pltpu.PrefetchScalarGridSpec(
            num_scalar_prefetch=1,
            grid=(nblk // g,),
            in_specs=[pl.BlockSpec((g * MOE_BLOCK, D_MODEL), lambda i, tab: (i, 0)),
                      pl.BlockSpec((g, 8, MOE_BLOCK), lambda i, tab: (i, 0, 0)),
                      pl.BlockSpec(w1_b.shape, lambda i, tab: (0, 0, 0), pipeline_mode=pl.Buffered(1)),
                      whole(s_h.shape), whole(s_gates.shape)],
            out_specs=[pl.BlockSpec((g, MOE_ROWS, D_EXPERT), lambda i, tab: (i, 0, 0)),
                       whole((n_s, N_EXPERTS * D_EXPERT))],
            scratch_shapes=[pltpu.VMEM((g, MOE_ROWS, D_MODEL), BF16)]),
        out_shape=[jax.ShapeDtypeStruct((nblk, MOE_ROWS, D_EXPERT), BF16),
                   jax.ShapeDtypeStruct((n_s, N_EXPERTS * D_EXPERT), BF16)],
        compiler_params=pltpu.CompilerParams(
            dimension_semantics=("arbitrary",), vmem_limit_bytes=V7X_VMEM_LIMIT),
        name="moe_gate_up",
    )(tab, h2, mrow, w1_b, s_h, s_gates)


def _moe_down_kernel(tab_ref, hs_ref, xp_ref, mcol_ref, w2_ref, gf_ref, shid_ref, sx_ref, y_ref, sy_ref, os, *,
                     flags_at):
    t = MOE_BLOCK
    n_blk = MOE_DOWN_BLOCKS
    first = pl.program_id(0) * n_blk
    tab = lambda j, e, c: tab_ref[((first + j) * N_EXPERTS + e) * 3 + c]
    os[...] = jnp.zeros(os.shape, BF16)

    def first_chunks(g, carry):
        for i in range(MOE_EXPERTS_PER_STEP):
            e = g * MOE_EXPERTS_PER_STEP + i
            offs = _chunk_offsets(tab, n_blk, e)
            hid = jnp.concatenate([hs_ref[j, pl.ds(offs[j], MOE_CHUNK), :] for j in range(n_blk)], axis=0)
            out = jnp.dot(hid, w2_ref[e], preferred_element_type=F32).astype(BF16)
            for j in range(n_blk):
                os[j, pl.ds(offs[j], MOE_CHUNK), :] = out[j * MOE_CHUNK:(j + 1) * MOE_CHUNK, :]
        return carry

    lax.fori_loop(0, N_EXPERTS // MOE_EXPERTS_PER_STEP, first_chunks, 0)

    def more_chunks(j, e, carry):
        off, n_chunks, end = tab(j, e, 0), tab(j, e, 1), tab(j, e, 2)

        def chunk(c, carry):
            r0 = pl.multiple_of(off + c * MOE_CHUNK, MOE_ROW_ALIGN)
            rows = r0 + lax.broadcasted_iota(jnp.int32, (MOE_CHUNK, D_MODEL), 0)
            out = jnp.dot(hs_ref[j, pl.ds(r0, MOE_CHUNK), :], w2_ref[e], preferred_element_type=F32)
            os[j, pl.ds(r0, MOE_CHUNK), :] = jnp.where(rows < end, out.astype(BF16), os[j, pl.ds(r0, MOE_CHUNK), :])
            return carry

        return lax.fori_loop(1, n_chunks, chunk, carry)

    for j in range(n_blk):
        @pl.when(tab_ref[flags_at + first + j] > 0)
        def _(j=j):
            lax.fori_loop(0, N_EXPERTS, functools.partial(more_chunks, j), 0)

    l_idx = lax.broadcasted_iota(jnp.int32, (t, MOE_SORT_ROWS), 1)

    def scatter_matrix(j):
        mcol = mcol_ref[j * t:(j + 1) * t, :]
        d1c = mcol[:, 0:1].astype(jnp.int32)
        d2c = mcol[:, 1:2].astype(jnp.int32)
        comb = jnp.where(l_idx == d1c, mcol[:, 2:3], 0.0) + jnp.where(l_idx == d2c, mcol[:, 3:4], 0.0)
        return comb.astype(BF16)

    combs = [scatter_matrix(j) for j in range(n_blk)]
    moes = [_dot(combs[j], os[j, 0:MOE_SORT_ROWS, :]) for j in range(n_blk)]
    for j in range(n_blk):
        rows = slice(j * t, (j + 1) * t)
        y_ref[rows, :] = _rmsnorm(xp_ref[rows, :] + moes[j], gf_ref[...])

    @pl.when(pl.program_id(0) == 0)
    def _():
        w2_all = w2_ref[...].reshape(N_EXPERTS * D_EXPERT, D_MODEL)
        ys = sx_ref[...] + jnp.dot(shid_ref[...], w2_all, preferred_element_type=F32)
        sy_ref[...] = _rmsnorm(ys, gf_ref[...])


def _moe_down(tab, hs_sorted, xp, mcol, w2_b, gf, s_hid, s_x):
    n = xp.shape[0]
    n_s = s_x.shape[0]
    nblk = n // MOE_BLOCK
    g = MOE_DOWN_BLOCKS
    assert nblk % g == 0
    whole = lambda shape: pl.BlockSpec(shape, lambda i, tab: (0,) * len(shape))
    row_blk = lambda w: pl.BlockSpec((g * MOE_BLOCK, w), lambda i, tab: (i, 0))
    return pl.pallas_call(
        functools.partial(_moe_down_kernel, flags_at=nblk * N_EXPERTS * 3),
        grid_spec=pltpu.PrefetchScalarGridSpec(
            num_scalar_prefetch=1,
            grid=(nblk // g,),
            in_specs=[pl.BlockSpec((g, MOE_ROWS, D_EXPERT), lambda i, tab: (i, 0, 0)),
                      row_blk(D_MODEL), row_blk(ROUTER_LANES),
                      pl.BlockSpec(w2_b.shape, lambda i, tab: (0, 0, 0), pipeline_mode=pl.Buffered(1)),
                      whole((1, D_MODEL)), whole(s_hid.shape), whole(s_x.shape)],
            out_specs=[row_blk(D_MODEL), whole((n_s, D_MODEL))],
            scratch_shapes=[pltpu.VMEM((g, MOE_ROWS, D_MODEL), BF16)]),
        out_shape=[jax.ShapeDtypeStruct((n, D_MODEL), F32), jax.ShapeDtypeStruct((n_s, D_MODEL), F32)],
        compiler_params=pltpu.CompilerParams(
            dimension_semantics=("arbitrary",), vmem_limit_bytes=V7X_VMEM_LIMIT),
        name="moe_down_combine",
    )(tab, hs_sorted, xp, mcol, w2_b, gf, s_hid, s_x)


def _sample_proj_kernel(x_ref, g1_ref, w_ref, cos_ref, sin_ref, gv_ref, ones_ref, w00_ref, b0_ref,
                        rep_ref, foldt_ref, a_ref, k_ref, v_ref, vn_ref, qkvt_ref):
    h = _rmsnorm(x_ref[...], g1_ref[...])

    def proj(i):
        return _dot(h, w_ref[:, i * WIDTH:(i + 1) * WIDTH], precise=True)

    cos = _tile_lanes(cos_ref[...], WIDTH // 128)
    sin = _tile_lanes(sin_ref[...], WIDTH // 128)
    q = _rope(proj(2), cos, sin) * (HEAD_DIM ** -0.5)
    k = _rope(proj(3), cos, sin)
    v = proj(4)
    vn = _group_rmsnorm(proj(1), ones_ref[...], gv_ref[...], precise=True)
    a_ref[...] = proj(0) * (w00_ref[...] * vn + b0_ref[...])
    k_ref[...] = k
    v_ref[...] = v
    vn_ref[...] = vn

    n_rep = rep_ref.shape[0]
    r_idx = lax.broadcasted_iota(jnp.int32, (n_rep, WIDTH), 0)
    l_idx = lax.broadcasted_iota(jnp.int32, (n_rep, WIDTH), 1)
    own = (l_idx // HEAD_DIM) == (r_idx % N_HEADS)
    for t, src in enumerate((q, k, v)):
        rep = _dot(rep_ref[...], src, precise=True)
        qkvt_ref[t * HEAD_DIM:(t + 1) * HEAD_DIM, :] = lax.dot_general(
            foldt_ref[...], jnp.where(own, rep, 0.0), (((1,), (1,)), ((), ())),
            preferred_element_type=F32, precision=lax.Precision.HIGHEST)


def _sample_proj(x, g1, w_in, cos, sin, gv, ones_bd, w00, b0, rep, foldt):
    bd = x.shape[0]
    sds = lambda r, c: jax.ShapeDtypeStruct((r, c), F32)
    return pl.pallas_call(
        _sample_proj_kernel,
        out_shape=[sds(bd, WIDTH), sds(bd, WIDTH), sds(bd, WIDTH), sds(bd, WIDTH),
                   sds(3 * HEAD_DIM, bd * N_HEADS)],
        compiler_params=pltpu.CompilerParams(vmem_limit_bytes=V7X_VMEM_LIMIT),
        name="sample_proj",
    )(x, g1, w_in, cos, sin, gv, ones_bd, w00, b0, rep, foldt)


def _sample_scores(qkvt_ref, k_ref, b, win):
    n_col = qkvt_ref.shape[1]
    c_idx = lax.broadcasted_iota(jnp.int32, (n_col, 128), 0)
    l_idx = lax.broadcasted_iota(jnp.int32, (n_col, 128), 1)
    pick = jnp.where((c_idx == b * N_HEADS + l_idx) & (l_idx < N_HEADS), 1.0, 0.0)
    cols = _dot(qkvt_ref[...], pick, precise=True)

    head = lax.broadcasted_iota(jnp.int32, (N_HEADS, win), 0)
    s = jnp.zeros((N_HEADS, win), F32)
    s_self = jnp.zeros((N_HEADS, 1), F32)
    for h in range(N_HEADS):
        qc = cols[0:HEAD_DIM, h:h + 1]
        kc = cols[HEAD_DIM:2 * HEAD_DIM, h:h + 1]
        s_h = jnp.sum(k_ref[0, h] * qc, axis=0, keepdims=True)
        s = jnp.where(head == h, s_h, s)
        s_self = jnp.where(head[:, 0:1] == h, jnp.sum(qc * kc, axis=0, keepdims=True), s_self)
    return s, s_self, cols[2 * HEAD_DIM:3 * HEAD_DIM, :]


def _sample_finish(s, s_self, v_cols, v_ref, o_ref, win):
    dist = win - lax.broadcasted_iota(jnp.int32, (1, win), 1)
    members = [(dist <= N_KEYS * dil) & (dist % dil == 0) for dil in DILATIONS]
    es, e_selfs, dens, lses = [], [], [], []
    for mem in members:
        sm = jnp.where(mem, s, NEG)
        m = jnp.maximum(jnp.max(sm, axis=1, keepdims=True), s_self)
        e = jnp.exp(sm - m)
        e_self = jnp.exp(s_self - m)
        den = jnp.sum(e, axis=1, keepdims=True) + e_self
        es.append(e)
        e_selfs.append(e_self)
        dens.append(den)
        lses.append(m + jnp.log(den))
    top = jnp.maximum(jnp.maximum(lses[0], lses[1]), lses[2])
    ws = [jnp.exp(l - top) for l in lses]
    wsum = ws[0] + ws[1] + ws[2]
    coef = [w / (den * wsum) for w, den in zip(ws, dens)]
    p_keys = coef[0] * es[0] + coef[1] * es[1] + coef[2] * es[2]
    p_self = coef[0] * e_selfs[0] + coef[1] * e_selfs[1] + coef[2] * e_selfs[2]

    for h in range(N_HEADS):
        o_ref[0, :, h:h + 1] = (jnp.sum(v_ref[0, h] * p_keys[h:h + 1, :], axis=1, keepdims=True)
                                + p_self[h:h + 1, :] * v_cols[:, h:h + 1])


def _rope_tables(first_pos, count):
    half = HEAD_DIM // 2
    inv = ROPE_THETA ** (-np.arange(half, dtype=np.float64) * 2.0 / HEAD_DIM)
    ang = (first_pos + np.arange(count, dtype=np.float64))[:, None] * inv[None, :]
    cos, sin = np.cos(ang), np.sin(ang)
    cos128 = np.concatenate([cos, cos, cos, cos], axis=1).astype(np.float32)
    sin128 = np.concatenate([-sin, sin, -sin, sin], axis=1).astype(np.float32)
    return jnp.asarray(cos128), jnp.asarray(sin128)


def kernel(x_prompt, x_sample, cache_win_k, cache_win_v, ln1_g, w_in, sgu_norm_g, sgu_w, sgu_b, w_out, ln2_g,
           w_router_group, b_router_group, w_router_expert, b_router_expert, w_gate, w_up, w_down, lnf_g):
    depth = w_in.shape[0]
    assert depth == 1 and x_sample.shape[1] == 1
    B, S, _ = x_prompt.shape
    bd = x_sample.shape[0]
    win = cache_win_k.shape[2]
    assert S % (max(DILATIONS) * CHUNK) == 0 and win >= max(DILATIONS) * N_KEYS and PAST_LEN % CHUNK == 0
    l = 0

    pad = ROUTER_LANES - N_GROUPS - N_EXPERTS
    w_router = jnp.pad(jnp.concatenate([w_router_group[l], w_router_expert[l]], axis=1), ((0, 0), (0, pad)))
    b_router = jnp.pad(jnp.concatenate([b_router_group[l], b_router_expert[l]]), (0, pad))[None, :]
    g1 = ln1_g[l][None, :]
    g2 = ln2_g[l][None, :]
    gf = lnf_g[None, :]
    gv = sgu_norm_g[l].reshape(1, WIDTH)
    grp = np.arange(WIDTH) // HEAD_DIM
    ones_bd = jnp.asarray(np.where(grp[:, None] == grp[None, :], 1.0 / HEAD_DIM, 0.0), BF16)
    wp = jnp.concatenate([sgu_w[l][0::2], sgu_w[l][1::2]], axis=-1)
    bias = jnp.repeat(sgu_b[l].T, HEAD_DIM, axis=1)
    w00 = jnp.repeat(sgu_w[l][:, 0, 0], HEAD_DIM)[None, :]
    b0 = jnp.repeat(sgu_b[l][:, 0], HEAD_DIM)[None, :]

    cos_s, sin_s = _rope_tables(PAST_LEN, 1)
    rep = jnp.asarray(np.arange(bd * N_HEADS)[:, None] // N_HEADS == np.arange(bd)[None, :], F32)
    foldt = jnp.asarray(np.arange(HEAD_DIM)[:, None] == np.arange(WIDTH)[None, :] % HEAD_DIM, F32)
    xs = x_sample.reshape(bd, D_MODEL)
    a_s, k_s, v_s, vn_s, qkvt = _sample_proj(xs, g1, w_in[l], cos_s, sin_s, gv, ones_bd, w00, b0, rep, foldt)
    to_pos_minor = lambda c: jnp.transpose(c, (0, 2, 3, 1))

    cos_p, sin_p = _rope_tables(0, S)
    a_p, q_p, k_p, v_p, kt_p, vt_p, o3 = _prompt_proj(
        x_prompt, g1, w_in[l], cos_p, sin_p, gv, ones_bd, wp, bias,
        qkvt, to_pos_minor(cache_win_k[l]), to_pos_minor(cache_win_v[l]))
    b_p = _prompt_attention(q_p, k_p, v_p)
    n = B * S
    assert n % MOE_BLOCK == 0
    wr_t = w_router.T[:ROUTER_ROWS]
    wr_hi = wr_t.astype(BF16)
    wr_hl = jnp.concatenate([wr_hi, (wr_t - wr_hi.astype(F32)).astype(BF16)], axis=0)
    lane_idx = np.arange(ROUTER_LANES)
    tri = jnp.asarray(lane_idx[:, None] <= lane_idx[None, :], BF16)
    ltri = jnp.asarray(lane_idx[None, :] < lane_idx[:, None], BF16)
    xp2, h2, mrow, mcol, tab_f, w1_b, w2_b = _mix_route_sort(
        a_p.reshape(n, WIDTH), b_p.reshape(n, WIDTH), x_prompt.reshape(n, D_MODEL),
        w_out[l], g2, wr_hl, b_router.reshape(ROUTER_LANES, 1)[:ROUTER_ROWS], tri, ltri,
        w_gate[l], w_up[l], w_down[l])
    tab_i = tab_f[:, :N_EXPERTS, 0:3].astype(jnp.int32)
    multi_chunk = (jnp.max(tab_i[:, :, 1], axis=1) > 1).astype(jnp.int32)
    tab = jnp.concatenate([tab_i.reshape(-1), multi_chunk])

    b_s = jnp.transpose(o3, (0, 2, 1)).reshape(bd, WIDTH)
    xs2, hs2, gates_s = _mix_router(a_s, b_s, xs, w_out[l], g2, w_router, b_router, tm=bd, precise=True)

    hid_sorted, hid_s = _moe_gate_up(tab, h2, mrow, w1_b, hs2, gates_s)
    y_prompt, y_sample = _moe_down(tab, hid_sorted, xp2, mcol, w2_b, gf, hid_s, xs2)
    y_prompt = y_prompt.reshape(B, S, D_MODEL)
    y_sample = y_sample.reshape(bd, 1, D_MODEL)
    buf_p = min(MAX_WINDOW, S)
    to_win = lambda t: jnp.transpose(t.reshape(1, B, N_HEADS, HEAD_DIM, buf_p), (0, 1, 4, 2, 3))
    new_k_p = to_win(kt_p)
    new_v_p = to_win(vt_p)

    shape_s = (1, bd, 1, N_HEADS, HEAD_DIM)
    return (y_prompt, y_sample, new_k_p, new_v_p,
            k_s.reshape(shape_s), v_s.reshape(shape_s), vn_s.reshape(shape_s))
```

```python
import functools

import jax
import jax.numpy as jnp
import numpy as np
from jax import lax
from jax.experimental import pallas as pl
from jax.experimental.pallas import tpu as pltpu

F32 = jnp.float32
BF16 = jnp.bfloat16

D_MODEL = 1024
HEAD_DIM = 64
N_HEADS = 8
WIDTH = N_HEADS * HEAD_DIM
PROJ_COLS = 5 * WIDTH
CHUNK = 128
DILATIONS = (1, 4, 16)
N_KEYS = 128
MAX_WINDOW = 2048
PAST_LEN = 16384
ROPE_THETA = 10000.0
N_GROUPS = 4
EXPERTS_PER_GROUP = 8
N_EXPERTS = N_GROUPS * EXPERTS_PER_GROUP
D_EXPERT = 128
EPS = 1e-6
NEG = -1e30
TILES_PER_STEP = 32
SUBLANE_STRIDE = 4
assert DILATIONS == (1, SUBLANE_STRIDE, SUBLANE_STRIDE ** 2)
MOE_BLOCK = 512
MOE_ROW_ALIGN = 16
MOE_CHUNK = 64
MOE_EXPERTS_PER_STEP = 16
MOE_ROUTE_BLOCKS = 2
MOE_GATE_UP_BLOCKS = 4
MOE_DOWN_BLOCKS = 2
MOE_SORT_ROWS = -(-(2 * MOE_BLOCK + N_EXPERTS * (MOE_ROW_ALIGN - 1)) // 512) * 512
MOE_ROWS = MOE_SORT_ROWS + MOE_CHUNK
ROUTER_ROWS = 48
ROUTER_LANES = 128
V7X_VMEM_LIMIT = 56 * 1024 * 1024


def _rmsnorm(x, g):
    return x * lax.rsqrt(jnp.mean(x * x, axis=-1, keepdims=True) + EPS) * g


def _tile_lanes(t, reps):
    return jnp.concatenate([t] * reps, axis=1)


def _rope(t, cos, sin_signed):
    lane = lax.broadcasted_iota(jnp.int32, t.shape, 1)
    first_half = (lane % HEAD_DIM) < (HEAD_DIM // 2)
    n = t.shape[1]
    partner = jnp.where(first_half, pltpu.roll(t, n - HEAD_DIM // 2, 1), pltpu.roll(t, HEAD_DIM // 2, 1))
    return t * cos + partner * sin_signed


def _dot(a, b, precise=False):
    if precise:
        return jnp.dot(a.astype(F32), b.astype(F32), preferred_element_type=F32,
                       precision=lax.Precision.HIGHEST)
    return jnp.dot(a.astype(BF16), b.astype(BF16), preferred_element_type=F32)


def _group_rmsnorm(va, ones_bd, gv, precise=False):
    ms = _dot(va * va, ones_bd, precise)
    return va * lax.rsqrt(ms + EPS) * gv


def _proj_kernel(x_ref, g1_ref, w_ref, cos_ref, sin_ref, gv_ref, ones_ref, wp_ref, bias_ref,
                 qkvt_ref, ck_ref, cv_ref, wg_ref, wu_ref, wd_ref,
                 a_ref, q_ref, k_ref, v_ref, kt_ref, vt_ref, so_ref, w1_ref, w2_ref, wb,
                 *, tm, first_win_tile, n_seq, win):
    @pl.when((pl.program_id(0) == 0) & (pl.program_id(1) == 0))
    def _():
        wb[...] = w_ref[...].astype(BF16)

    h = _rmsnorm(x_ref[0], g1_ref[...]).astype(BF16)

    def proj(i):
        return jnp.dot(h, wb[:, i * WIDTH:(i + 1) * WIDTH], preferred_element_type=F32)

    cos = _tile_lanes(cos_ref[...], WIDTH // 128)
    sin = _tile_lanes(sin_ref[...], WIDTH // 128)
    q_ref[0] = _rope(proj(2), cos, sin) * (HEAD_DIM ** -0.5)
    k_ref[0] = _rope(proj(3), cos, sin)
    v_ref[0] = proj(4)

    s_seq = jnp.minimum(pl.program_id(0) * pl.num_programs(1) + pl.program_id(1), n_seq - 1)
    s_scores, s_self, s_vcols = _sample_scores(qkvt_ref, ck_ref, s_seq, win)

    u = proj(0)
    vn = _group_rmsnorm(proj(1), ones_ref[...], gv_ref[...]).astype(BF16)

    lane = lax.broadcasted_iota(jnp.int32, (CHUNK, 128), 1)
    left = lane < HEAD_DIM
    row = lax.broadcasted_iota(jnp.int32, (CHUNK, 2 * CHUNK), 0)
    col = lax.broadcasted_iota(jnp.int32, (CHUNK, 2 * CHUNK), 1)
    causal = (col % CHUNK) <= row
    zero = jnp.zeros((CHUNK, 128), BF16)
    wps = [jnp.where(causal, wp_ref[gp], 0.0).astype(BF16) for gp in range(N_HEADS // 2)]
    def block_diag(vv):
        return jnp.concatenate([jnp.where(left, vv, zero), jnp.where(left, zero, vv)], axis=0)

    for c in range(0, tm // CHUNK, 2):
        rows = [slice((c + i) * CHUNK, (c + i + 1) * CHUNK) for i in range(2)]
        mixes = [[], []]
        for gp in range(N_HEADS // 2):
            lanes = slice(gp * 128, (gp + 1) * 128)
            v2 = jnp.concatenate([block_diag(vn[rows[0], lanes]), block_diag(vn[rows[1], lanes])], axis=1)
            both = jnp.dot(wps[gp], v2, preferred_element_type=F32)
            mixes[0].append(both[:, 0:128])
            mixes[1].append(both[:, 128:256])
        for i in range(2):
            mix = jnp.concatenate(mixes[i], axis=1) + bias_ref[...]
            a_ref[0, rows[i], :] = (u[rows[i], :] * mix).astype(a_ref.dtype)

    _sample_finish(s_scores, s_self, s_vcols, cv_ref, so_ref, win)

    w1_ref[:, :, 0:D_EXPERT] = wg_ref[...].astype(BF16)
    w1_ref[:, :, D_EXPERT:2 * D_EXPERT] = wu_ref[...].astype(BF16)
    w2_ref[...] = wd_ref[...].astype(BF16)

    @pl.when(pl.program_id(1) >= first_win_tile)
    def _():
        kt_ref[0] = k_ref[0].T
        vt_ref[0] = v_ref[0].T


def _prompt_proj(x, g1, w_in_b, cos, sin, gv, ones_bd, wp, bias, qkvt, cache_k_t, cache_v_t,
                 w_gate, w_up, w_down, *, tm=512):
    B, S, _ = x.shape
    const2 = lambda b, j: (0, 0)
    out_sds = lambda dt: jax.ShapeDtypeStruct((B, S, WIDTH), dt)
    tile = pl.BlockSpec((1, tm, WIDTH), lambda b, j: (b, j, 0))
    win = min(MAX_WINDOW, S)
    first_win_tile = (S - win) // tm
    tile_t = pl.BlockSpec((1, WIDTH, tm), lambda b, j: (b, 0, jnp.maximum(j - first_win_tile, 0)))
    win_sds = jax.ShapeDtypeStruct((B, WIDTH, win), F32)
    n_seq, _, _, cache_win = cache_k_t.shape
    n_tiles = S // tm
    assert n_seq <= B * n_tiles, "one sample sequence rides on each grid step"
    seq_blk = lambda shape: pl.BlockSpec(
        (1,) + shape, lambda b, j: (jnp.minimum(b * n_tiles + j, n_seq - 1),) + (0,) * len(shape))
    cache_blk = seq_blk((N_HEADS, HEAD_DIM, cache_win))
    n_steps = B * n_tiles
    assert N_EXPERTS % n_steps == 0
    e_blk = N_EXPERTS // n_steps
    expert_blk = lambda rows, cols: pl.BlockSpec((e_blk, rows, cols), lambda b, j: (b * n_tiles + j, 0, 0))
    return pl.pallas_call(
        functools.partial(_proj_kernel, tm=tm, first_win_tile=first_win_tile, n_seq=n_seq, win=cache_win),
        grid=(B, n_tiles),
        in_specs=[
            pl.BlockSpec((1, tm, D_MODEL), lambda b, j: (b, j, 0)),
            pl.BlockSpec((1, D_MODEL), const2),
            pl.BlockSpec((D_MODEL, PROJ_COLS), const2, pipeline_mode=pl.Buffered(1)),
            pl.BlockSpec((tm, 128), lambda b, j: (j, 0)),
            pl.BlockSpec((tm, 128), lambda b, j: (j, 0)),
            pl.BlockSpec((1, WIDTH), const2),
            pl.BlockSpec((WIDTH, WIDTH), const2),
            pl.BlockSpec((N_HEADS // 2, CHUNK, 2 * CHUNK), lambda b, j: (0, 0, 0)),
            pl.BlockSpec((CHUNK, WIDTH), const2),
            pl.BlockSpec(qkvt.shape, const2), cache_blk, cache_blk,
            expert_blk(D_MODEL, D_EXPERT), expert_blk(D_MODEL, D_EXPERT), expert_blk(D_EXPERT, D_MODEL),
        ],
        out_specs=[tile, tile, tile, tile, tile_t, tile_t, seq_blk((HEAD_DIM, N_HEADS)),
                   expert_blk(D_MODEL, 2 * D_EXPERT), expert_blk(D_EXPERT, D_MODEL)],
        out_shape=[out_sds(BF16), out_sds(F32), out_sds(F32), out_sds(F32), win_sds, win_sds,
                   jax.ShapeDtypeStruct((n_seq, HEAD_DIM, N_HEADS), F32),
                   jax.ShapeDtypeStruct((N_EXPERTS, D_MODEL, 2 * D_EXPERT), BF16),
                   jax.ShapeDtypeStruct((N_EXPERTS, D_EXPERT, D_MODEL), BF16)],
        scratch_shapes=[pltpu.VMEM((D_MODEL, PROJ_COLS), BF16)],
        compiler_params=pltpu.CompilerParams(
            dimension_semantics=("arbitrary", "arbitrary"), vmem_limit_bytes=V7X_VMEM_LIMIT),
        name="prompt_proj_sgu",
    )(x, g1, w_in_b, cos, sin, gv, ones_bd, wp, bias, qkvt, cache_k_t, cache_v_t, w_gate, w_up, w_down)


def _attn_kernel(q_ref, k_ref, v_ref, o_ref, qd, kd, vd, res_o, res_l, nat_o, nat_l, bias, stage, *, seq):
    n_tiles = seq // CHUNK
    lane = lax.broadcasted_iota(jnp.int32, (CHUNK, 128), 1)
    left = lane < HEAD_DIM
    qi2 = lax.broadcasted_iota(jnp.int32, (2 * CHUNK, 2 * CHUNK), 0) % CHUNK
    kj2 = lax.broadcasted_iota(jnp.int32, (2 * CHUNK, 2 * CHUNK), 1)
    dist2 = CHUNK + qi2 - kj2
    band2 = (dist2 >= 0) & (dist2 <= N_KEYS)
    zero_q = jnp.zeros((CHUNK, 128), BF16)
    bias[0] = jnp.where(band2, 0.0, NEG)
    bias[1] = jnp.where(band2 & (kj2 >= CHUNK), 0.0, NEG)

    kd[0:CHUNK, :] = jnp.zeros((CHUNK, 128), BF16)
    vd[0:CHUNK, :] = jnp.zeros((CHUNK, 128), BF16)

    for p, dil in enumerate(DILATIONS):
        sub = seq // dil
        nb = sub // CHUNK
        for ti, (src_ref, dst, pad) in enumerate(((q_ref, qd, 0), (k_ref, kd, CHUNK), (v_ref, vd, CHUNK))):
            if dil == 1:
                dst[pad:pad + seq, :] = src_ref[0].astype(BF16)
            elif dil == SUBLANE_STRIDE:
                for r in range(dil):
                    val = src_ref[0, pl.ds(r, sub, stride=dil), :]
                    stage[ti, r * sub:(r + 1) * sub, :] = val
                    dst[pad + r * sub:pad + (r + 1) * sub, :] = val.astype(BF16)
            else:
                coarse = seq // SUBLANE_STRIDE
                for r_lo in range(SUBLANE_STRIDE):
                    for r_hi in range(SUBLANE_STRIDE):
                        r = r_lo + SUBLANE_STRIDE * r_hi
                        val = stage[ti, pl.ds(r_lo * coarse + r_hi, sub, stride=SUBLANE_STRIDE), :]
                        dst[pad + r * sub:pad + (r + 1) * sub, :] = val.astype(BF16)

        def tile_body(g, i, p=p, nb=nb):
            t = g * TILES_PER_STEP + i
            row = pl.multiple_of(t * CHUNK, CHUNK)
            qt = qd[pl.ds(row, CHUNK), :]
            k2 = kd[pl.ds(row, 2 * CHUNK), :]
            v2 = vd[pl.ds(row, 2 * CHUNK), :]
            if TILES_PER_STEP % nb == 0:
                variant = 1 if i % nb == 0 else 0
            elif i == 0:
                variant = jnp.where((g * TILES_PER_STEP) % nb == 0, 1, 0)
            else:
                variant = 0
            q2 = jnp.concatenate([jnp.where(left, qt, zero_q), jnp.where(left, zero_q, qt)], axis=0)
            s = lax.dot_general(q2, k2, (((1,), (1,)), ((), ())), preferred_element_type=F32)
            s = s + bias[variant]
            m = jnp.max(s, axis=1, keepdims=True)
            e = jnp.exp(s - m)
            den = jnp.sum(e, axis=1, keepdims=True)
            pv = jnp.dot(e.astype(BF16), v2, preferred_element_type=F32) / den
            lse = jnp.broadcast_to(m + jnp.log(den), (2 * CHUNK, 128))
            res_o[p, pl.ds(row, CHUNK), :] = jnp.where(left, pv[0:CHUNK], pv[CHUNK:2 * CHUNK])
            res_l[p, pl.ds(row, CHUNK), :] = jnp.where(left, lse[0:CHUNK], lse[CHUNK:2 * CHUNK])

        def group_body(g, carry, tile_body=tile_body):
            for i in range(TILES_PER_STEP):
                tile_body(g, i)
            return carry

        lax.fori_loop(0, n_tiles // TILES_PER_STEP, group_body, 0)

    for p, dil in enumerate(DILATIONS):
        if dil == 1:
            continue
        sub = seq // dil
        for si, (res, nat) in enumerate(((res_o, nat_o), (res_l, nat_l))):
            if dil == SUBLANE_STRIDE:
                for r in range(dil):
                    nat[p - 1, pl.ds(r, sub, stride=dil), :] = res[p, r * sub:(r + 1) * sub, :]
            else:
                coarse = seq // SUBLANE_STRIDE
                for r_lo in range(SUBLANE_STRIDE):
                    for r_hi in range(SUBLANE_STRIDE):
                        r = r_lo + SUBLANE_STRIDE * r_hi
                        stage[si, pl.ds(r_lo * coarse + r_hi, sub, stride=SUBLANE_STRIDE), :] = (
                            res[p, r * sub:(r + 1) * sub, :])
                for r_lo in range(SUBLANE_STRIDE):
                    nat[p - 1, pl.ds(r_lo, coarse, stride=SUBLANE_STRIDE), :] = (
                        stage[si, r_lo * coarse:(r_lo + 1) * coarse, :])

    rows_per_step = 256

    def merge_body(c, carry):
        rows = pl.ds(pl.multiple_of(c * rows_per_step, rows_per_step), rows_per_step)
        l0, l1, l2 = res_l[0, rows, :], nat_l[0, rows, :], nat_l[1, rows, :]
        top = jnp.maximum(jnp.maximum(l0, l1), l2)
        w0, w1, w2 = jnp.exp(l0 - top), jnp.exp(l1 - top), jnp.exp(l2 - top)
        num = w0 * res_o[0, rows, :] + w1 * nat_o[0, rows, :] + w2 * nat_o[1, rows, :]
        o_ref[0, rows, :] = (num / (w0 + w1 + w2)).astype(o_ref.dtype)
        return carry

    lax.fori_loop(0, seq // rows_per_step, merge_body, 0)


def _prompt_attention(q, k, v):
    B, S, _ = q.shape
    blk = pl.BlockSpec((1, S, 128), lambda b, hp: (b, 0, hp))
    return pl.pallas_call(
        functools.partial(_attn_kernel, seq=S),
        grid=(B, WIDTH // 128),
        in_specs=[blk, blk, blk],
        out_specs=blk,
        out_shape=jax.ShapeDtypeStruct((B, S, WIDTH), BF16),
        scratch_shapes=[
            pltpu.VMEM((S, 128), BF16),
            pltpu.VMEM((S + CHUNK, 128), BF16),
            pltpu.VMEM((S + CHUNK, 128), BF16),
            pltpu.VMEM((len(DILATIONS), S, 128), F32),
            pltpu.VMEM((len(DILATIONS), S, 128), F32),
            pltpu.VMEM((len(DILATIONS) - 1, S, 128), F32),
            pltpu.VMEM((len(DILATIONS) - 1, S, 128), F32),
            pltpu.VMEM((2, 2 * CHUNK, 2 * CHUNK), F32),
            pltpu.VMEM((3, S, 128), F32),
        ],
        compiler_params=pltpu.CompilerParams(
            dimension_semantics=("arbitrary", "arbitrary"), vmem_limit_bytes=V7X_VMEM_LIMIT),
        name="prompt_dilated_attention",
    )(q, k, v)


def _route(logits):
    lane = lax.broadcasted_iota(jnp.int32, logits.shape, 1)
    big = jnp.int32(ROUTER_LANES)
    lg = jnp.where(lane < N_GROUPS, logits, NEG)
    gmax = jnp.max(lg, axis=1, keepdims=True)
    gp = 1.0 / jnp.sum(jnp.exp(lg - gmax), axis=1, keepdims=True)
    gi = jnp.min(jnp.where(lg == gmax, lane, big), axis=1, keepdims=True)
    lo = N_GROUPS + EXPERTS_PER_GROUP * gi
    le = jnp.where((lane >= lo) & (lane < lo + EXPERTS_PER_GROUP), logits, NEG)
    m1 = jnp.max(le, axis=1, keepdims=True)
    i1 = jnp.min(jnp.where(le == m1, lane, big), axis=1, keepdims=True)
    le2 = jnp.where(lane == i1, NEG, le)
    m2 = jnp.max(le2, axis=1, keepdims=True)
    i2 = jnp.min(jnp.where(le2 == m2, lane, big), axis=1, keepdims=True)
    e2 = jnp.exp(m2 - m1)
    w1 = 1.0 / (1.0 + e2)
    w2 = e2 / (1.0 + e2)
    return jnp.where(lane == i1, gp * w1, jnp.where(lane == i2, gp * w2, 0.0))


def _mix_router_kernel(a_ref, b_ref, x_ref, wo_ref, g2_ref, wr_ref, br_ref, xp_ref, h2_ref, gates_ref, *, precise):
    mixed = (_dot(a_ref[...], wo_ref[0:WIDTH, :], precise)
             + _dot(b_ref[...], wo_ref[WIDTH:2 * WIDTH, :], precise))
    xp = x_ref[...] + mixed
    xp_ref[...] = xp
    h2 = _rmsnorm(xp, g2_ref[...])
    h2_ref[...] = h2.astype(h2_ref.dtype)
    logits = jnp.dot(h2, wr_ref[...], preferred_element_type=F32,
                     precision=lax.Precision.HIGHEST) + br_ref[...]
    gates_ref[...] = _route(logits)


def _mix_router(a, b, x, w_out, g2, w_router, b_router, *, tm, precise=False):
    n = x.shape[0]
    const = lambda i: (0, 0)
    row_blk = lambda w: pl.BlockSpec((tm, w), lambda i: (i, 0))
    return pl.pallas_call(
        functools.partial(_mix_router_kernel, precise=precise),
        grid=(n // tm,),
        in_specs=[row_blk(WIDTH), row_blk(WIDTH), row_blk(D_MODEL),
                  pl.BlockSpec((2 * WIDTH, D_MODEL), const),
                  pl.BlockSpec((1, D_MODEL), const),
                  pl.BlockSpec((D_MODEL, ROUTER_LANES), const),
                  pl.BlockSpec((1, ROUTER_LANES), const)],
        out_specs=[row_blk(D_MODEL), row_blk(D_MODEL), row_blk(ROUTER_LANES)],
        out_shape=[jax.ShapeDtypeStruct((n, D_MODEL), F32),
                   jax.ShapeDtypeStruct((n, D_MODEL), BF16),
                   jax.ShapeDtypeStruct((n, ROUTER_LANES), F32)],
        compiler_params=pltpu.CompilerParams(
            dimension_semantics=("arbitrary",), vmem_limit_bytes=V7X_VMEM_LIMIT),
        name="outproj_router",
    )(a, b, x, w_out, g2, w_router, b_router)


def _nt_dot(w, t):
    return lax.dot_general(w, t, (((1,), (1,)), ((), ())), preferred_element_type=F32)


def _route_t(logits_t):
    row = lax.broadcasted_iota(jnp.int32, logits_t.shape, 0)
    big = jnp.int32(ROUTER_LANES)
    lg = jnp.where(row < N_GROUPS, logits_t, NEG)
    gmax = jnp.max(lg, axis=0, keepdims=True)
    gp = 1.0 / jnp.sum(jnp.exp(lg - gmax), axis=0, keepdims=True)
    gi = jnp.min(jnp.where(lg == gmax, row, big), axis=0, keepdims=True)
    lo = N_GROUPS + EXPERTS_PER_GROUP * gi
    le = jnp.where((row >= lo) & (row < lo + EXPERTS_PER_GROUP), logits_t, NEG)
    m1 = jnp.max(le, axis=0, keepdims=True)
    i1 = jnp.min(jnp.where(le == m1, row, big), axis=0, keepdims=True)
    le2 = jnp.where(row == i1, NEG, le)
    m2 = jnp.max(le2, axis=0, keepdims=True)
    i2 = jnp.min(jnp.where(le2 == m2, row, big), axis=0, keepdims=True)
    e2 = jnp.exp(m2 - m1)
    return i1 - N_GROUPS, i2 - N_GROUPS, gp / (1.0 + e2), gp * e2 / (1.0 + e2)


def _mix_route_sort_kernel(a_ref, b_ref, x_ref, wo_ref, g2_ref, wr_ref, brc_ref, tri_ref, ltri_ref,
                           xp_ref, h2_ref, mrow_ref, mcol_ref, tab_ref, wob):
    t = MOE_BLOCK

    @pl.when(pl.program_id(0) == 0)
    def _():
        wob[...] = wo_ref[...].astype(BF16)

    def project(j):
        rows = slice(j * t, (j + 1) * t)
        xp = x_ref[rows, :] + _dot(a_ref[rows, :], wob[0:WIDTH, :]) + _dot(b_ref[rows, :], wob[WIDTH:2 * WIDTH, :])
        xp_ref[rows, :] = xp
        h2 = _rmsnorm(xp, g2_ref[...])
        hi = h2.astype(BF16)
        h2_ref[rows, :] = hi
        return hi, (h2 - hi.astype(F32)).astype(BF16)

    def route(hi, lo):
        prod_hi = _nt_dot(wr_ref[...], hi)
        logits_t = (prod_hi[0:ROUTER_ROWS] + prod_hi[ROUTER_ROWS:2 * ROUTER_ROWS]
                    + _nt_dot(wr_ref[0:ROUTER_ROWS, :], lo) + brc_ref[...])
        return _route_t(logits_t)

    def sort_meta(j, ex1, ex2, gate1, gate2):
        rows = slice(j * t, (j + 1) * t)
        pair_e = jnp.concatenate([ex1, ex2], axis=1)
        row = lax.broadcasted_iota(jnp.int32, (N_EXPERTS, 2 * t), 0)
        onehot = jnp.where(row == pair_e, 1.0, 0.0)
        n_lane_tiles = 2 * t // 128
        local = _dot(jnp.concatenate([onehot[:, k * 128:(k + 1) * 128] for k in range(n_lane_tiles)], axis=0),
                     tri_ref[...])
        carry = jnp.zeros((N_EXPERTS, 1), F32)
        cums = []
        for k in range(n_lane_tiles):
            tile = local[k * N_EXPERTS:(k + 1) * N_EXPERTS, :]
            cums.append(tile + carry)
            carry = carry + tile[:, 127:128]
        cum = jnp.concatenate(cums, axis=1)
        rank = jnp.sum(onehot * cum, axis=0, keepdims=True) - 1.0
        counts = carry
        units32 = jnp.floor((counts + (MOE_ROW_ALIGN - 1)) * (1.0 / MOE_ROW_ALIGN))
        units = jnp.concatenate([jnp.broadcast_to(units32, (N_EXPERTS, 128)),
                                 jnp.zeros((ROUTER_LANES - N_EXPERTS, 128), F32)], axis=0)
        off = _dot(ltri_ref[...], units) * MOE_ROW_ALIGN
        dst = jnp.sum(onehot * off[0:N_EXPERTS, 0:1], axis=0, keepdims=True) + rank

        r8 = lax.broadcasted_iota(jnp.int32, (8, t), 0)
        mrow_ref[j] = jnp.where(r8 == 0, dst[:, 0:t], jnp.where(r8 == 1, dst[:, t:2 * t],
                                jnp.where(r8 == 2, gate1, jnp.where(r8 == 3, gate2, 0.0))))
        r128 = lax.broadcasted_iota(jnp.int32, (ROUTER_LANES, t), 0)
        meta = jnp.where(r128 == 0, dst[:, 0:t], jnp.where(r128 == 1, dst[:, t:2 * t],
                         jnp.where(r128 == 2, gate1, jnp.where(r128 == 3, gate2, 0.0))))
        mcol_ref[rows, :] = meta.T
        lane = lax.broadcasted_iota(jnp.int32, (ROUTER_LANES, 128), 1)
        n_rows = units * MOE_ROW_ALIGN
        chunks = jnp.floor((n_rows + (MOE_CHUNK - 1)) * (1.0 / MOE_CHUNK))
        tab_ref[j] = jnp.where(lane == 0, off, jnp.where(lane == 1, chunks, jnp.where(lane == 2, off + n_rows, 0.0)))

    blocks = range(MOE_ROUTE_BLOCKS)
    projected = [project(j) for j in blocks]
    routed = [route(hi, lo) for hi, lo in projected]
    for j in blocks:
        sort_meta(j, *routed[j])


def _mix_route_sort(a, b, x, w_out, g2, wr_hl, br_col, tri, ltri):
    n = x.shape[0]
    t = MOE_BLOCK
    nblk = n // t
    g = MOE_ROUTE_BLOCKS
    assert nblk % g == 0
    const = lambda i: (0, 0)
    row_blk = lambda w: pl.BlockSpec((g * t, w), lambda i: (i, 0))
    return pl.pallas_call(
        _mix_route_sort_kernel,
        grid=(nblk // g,),
        in_specs=[row_blk(WIDTH), row_blk(WIDTH), row_blk(D_MODEL),
                  pl.BlockSpec((2 * WIDTH, D_MODEL), const, pipeline_mode=pl.Buffered(1)),
                  pl.BlockSpec((1, D_MODEL), const),
                  pl.BlockSpec((2 * ROUTER_ROWS, D_MODEL), const),
                  pl.BlockSpec((ROUTER_ROWS, 1), const),
                  pl.BlockSpec((ROUTER_LANES, ROUTER_LANES), const),
                  pl.BlockSpec((ROUTER_LANES, ROUTER_LANES), const)],
        out_specs=[row_blk(D_MODEL), row_blk(D_MODEL),
                   pl.BlockSpec((g, 8, t), lambda i: (i, 0, 0)),
                   row_blk(ROUTER_LANES),
                   pl.BlockSpec((g, ROUTER_LANES, 128), lambda i: (i, 0, 0))],
        out_shape=[jax.ShapeDtypeStruct((n, D_MODEL), F32),
                   jax.ShapeDtypeStruct((n, D_MODEL), BF16),
                   jax.ShapeDtypeStruct((nblk, 8, t), F32),
                   jax.ShapeDtypeStruct((n, ROUTER_LANES), F32),
                   jax.ShapeDtypeStruct((nblk, ROUTER_LANES, 128), F32)],
        scratch_shapes=[pltpu.VMEM((2 * WIDTH, D_MODEL), BF16)],
        compiler_params=pltpu.CompilerParams(
            dimension_semantics=("arbitrary",), vmem_limit_bytes=V7X_VMEM_LIMIT),
        name="outproj_route_sort",
    )(a, b, x, w_out, g2, wr_hl, br_col, tri, ltri)


def _silu_mul(ab):
    a = ab[:, :D_EXPERT]
    return a * (1.0 / (1.0 + jnp.exp(-a))) * ab[:, D_EXPERT:]


def _chunk_offsets(tab, n_blk, e):
    return [pl.multiple_of(tab(j, e, 0), MOE_ROW_ALIGN) for j in range(n_blk)]


def _moe_gate_up_kernel(tab_ref, h_ref, mrow_ref, w1_ref, sh_ref, sg_ref, hs_ref, shid_ref, xs, *, flags_at):
    t = MOE_BLOCK
    n_blk = MOE_GATE_UP_BLOCKS
    first = pl.program_id(0) * n_blk
    tab = lambda j, e, c: tab_ref[((first + j) * N_EXPERTS + e) * 3 + c]

    piece = 512
    for j in range(n_blk):
        mrow = mrow_ref[j]
        dst1 = mrow[0:1, :].astype(jnp.int32)
        dst2 = mrow[1:2, :].astype(jnp.int32)
        h = h_ref[j * t:(j + 1) * t, :]
        for r0 in range(0, MOE_SORT_ROWS, piece):
            d_idx = lax.broadcasted_iota(jnp.int32, (piece, t), 0) + r0
            sel = jnp.where((d_idx == dst1) | (d_idx == dst2), 1.0, 0.0)
            xs[j, r0:r0 + piece, :] = _dot(sel, h).astype(BF16)
        xs[j, MOE_SORT_ROWS:MOE_ROWS, :] = jnp.zeros((MOE_CHUNK, D_MODEL), BF16)
    hs_ref[...] = jnp.zeros(hs_ref.shape, BF16)

    def first_chunks(g, carry):
        for i in range(MOE_EXPERTS_PER_STEP):
            e = g * MOE_EXPERTS_PER_STEP + i
            offs = _chunk_offsets(tab, n_blk, e)
            x = jnp.concatenate([xs[j, pl.ds(offs[j], MOE_CHUNK), :] for j in range(n_blk)], axis=0)
            hid = _silu_mul(jnp.dot(x, w1_ref[e], preferred_element_type=F32)).astype(BF16)
            for j in range(n_blk):
                hs_ref[j, pl.ds(offs[j], MOE_CHUNK), :] = hid[j * MOE_CHUNK:(j + 1) * MOE_CHUNK, :]
        return carry

    lax.fori_loop(0, N_EXPERTS // MOE_EXPERTS_PER_STEP, first_chunks, 0)

    def more_chunks(j, e, carry):
        off, n_chunks, end = tab(j, e, 0), tab(j, e, 1), tab(j, e, 2)

        def chunk(c, carry):
            r0 = pl.multiple_of(off + c * MOE_CHUNK, MOE_ROW_ALIGN)
            rows = r0 + lax.broadcasted_iota(jnp.int32, (MOE_CHUNK, D_EXPERT), 0)
            hid = _silu_mul(jnp.dot(xs[j, pl.ds(r0, MOE_CHUNK), :], w1_ref[e], preferred_element_type=F32))
            hs_ref[j, pl.ds(r0, MOE_CHUNK), :] = jnp.where(rows < end, hid.astype(BF16),
                                                           hs_ref[j, pl.ds(r0, MOE_CHUNK), :])
            return carry

        return lax.fori_loop(1, n_chunks, chunk, carry)

    for j in range(n_blk):
        @pl.when(tab_ref[flags_at + first + j] > 0)
        def _(j=j):
            lax.fori_loop(0, N_EXPERTS, functools.partial(more_chunks, j), 0)

    @pl.when(pl.program_id(0) == 0)
    def _():
        sh = sh_ref[...]
        gates = sg_ref[...]
        for e in range(N_EXPERTS):
            gate = gates[:, N_GROUPS + e:N_GROUPS + e + 1]
            hid = _silu_mul(jnp.dot(sh, w1_ref[e], preferred_element_type=F32)) * gate
            shid_ref[:, e * D_EXPERT:(e + 1) * D_EXPERT] = hid.astype(BF16)


def _moe_gate_up(tab, h2, mrow, w1_b, s_h, s_gates):
    n = h2.shape[0]
    n_s = s_h.shape[0]
    nblk = n // MOE_BLOCK
    g = MOE_GATE_UP_BLOCKS
    assert nblk % g == 0
    whole = lambda shape: pl.BlockSpec(shape, lambda i, tab: (0,) * len(shape))
    return pl.pallas_call(
        functools.partial(_moe_gate_up_kernel, flags_at=nblk * N_EXPERTS * 3),
        grid_spec=pltpu.PrefetchScalarGridSpec(
            num_scalar_prefetch=1,
            grid=(nblk // g,),
            in_specs=[pl.BlockSpec((g * MOE_BLOCK, D_MODEL), lambda i, tab: (i, 0)),
                      pl.BlockSpec((g, 8, MOE_BLOCK), lambda i, tab: (i, 0, 0)),
                      pl.BlockSpec(w1_b.shape, lambda i, tab: (0, 0, 0), pipeline_mode=pl.Buffered(1)),
                      whole(s_h.shape), whole(s_gates.shape)],
            out_specs=[pl.BlockSpec((g, MOE_ROWS, D_EXPERT), lambda i, tab: (i, 0, 0)),
                       whole((n_s, N_EXPERTS * D_EXPERT))],
            scratch_shapes=[pltpu.VMEM((g, MOE_ROWS, D_MODEL), BF16)]),
        out_shape=[jax.ShapeDtypeStruct((nblk, MOE_ROWS, D_EXPERT), BF16),
                   jax.ShapeDtypeStruct((n_s, N_EXPERTS * D_EXPERT), BF16)],
        compiler_params=pltpu.CompilerParams(
            dimension_semantics=("arbitrary",), vmem_limit_bytes=V7X_VMEM_LIMIT),
        name="moe_gate_up",
    )(tab, h2, mrow, w1_b, s_h, s_gates)


def _moe_down_kernel(tab_ref, hs_ref, xp_ref, mcol_ref, w2_ref, gf_ref, shid_ref, sx_ref, y_ref, sy_ref, os, *,
                     flags_at):
    t = MOE_BLOCK
    n_blk = MOE_DOWN_BLOCKS
    first = pl.program_id(0) * n_blk
    tab = lambda j, e, c: tab_ref[((first + j) * N_EXPERTS + e) * 3 + c]
    os[...] = jnp.zeros(os.shape, BF16)

    def first_chunks(g, carry):
        for i in range(MOE_EXPERTS_PER_STEP):
            e = g * MOE_EXPERTS_PER_STEP + i
            offs = _chunk_offsets(tab, n_blk, e)
            hid = jnp.concatenate([hs_ref[j, pl.ds(offs[j], MOE_CHUNK), :] for j in range(n_blk)], axis=0)
            out = jnp.dot(hid, w2_ref[e], preferred_element_type=F32).astype(BF16)
            for j in range(n_blk):
                os[j, pl.ds(offs[j], MOE_CHUNK), :] = out[j * MOE_CHUNK:(j + 1) * MOE_CHUNK, :]
        return carry

    lax.fori_loop(0, N_EXPERTS // MOE_EXPERTS_PER_STEP, first_chunks, 0)

    def more_chunks(j, e, carry):
        off, n_chunks, end = tab(j, e, 0), tab(j, e, 1), tab(j, e, 2)

        def chunk(c, carry):
            r0 = pl.multiple_of(off + c * MOE_CHUNK, MOE_ROW_ALIGN)
            rows = r0 + lax.broadcasted_iota(jnp.int32, (MOE_CHUNK, D_MODEL), 0)
            out = jnp.dot(hs_ref[j, pl.ds(r0, MOE_CHUNK), :], w2_ref[e], preferred_element_type=F32)
            os[j, pl.ds(r0, MOE_CHUNK), :] = jnp.where(rows < end, out.astype(BF16), os[j, pl.ds(r0, MOE_CHUNK), :])
            return carry

        return lax.fori_loop(1, n_chunks, chunk, carry)

    for j in range(n_blk):
        @pl.when(tab_ref[flags_at + first + j] > 0)
        def _(j=j):
            lax.fori_loop(0, N_EXPERTS, functools.partial(more_chunks, j), 0)

    l_idx = lax.broadcasted_iota(jnp.int32, (t, MOE_SORT_ROWS), 1)

    def scatter_matrix(j):
        mcol = mcol_ref[j * t:(j + 1) * t, :]
        d1c = mcol[:, 0:1].astype(jnp.int32)
        d2c = mcol[:, 1:2].astype(jnp.int32)
        comb = jnp.where(l_idx == d1c, mcol[:, 2:3], 0.0) + jnp.where(l_idx == d2c, mcol[:, 3:4], 0.0)
        return comb.astype(BF16)

    combs = [scatter_matrix(j) for j in range(n_blk)]
    moes = [_dot(combs[j], os[j, 0:MOE_SORT_ROWS, :]) for j in range(n_blk)]
    for j in range(n_blk):
        rows = slice(j * t, (j + 1) * t)
        y_ref[rows, :] = _rmsnorm(xp_ref[rows, :] + moes[j], gf_ref[...])

    @pl.when(pl.program_id(0) == 0)
    def _():
        w2_all = w2_ref[...].reshape(N_EXPERTS * D_EXPERT, D_MODEL)
        ys = sx_ref[...] + jnp.dot(shid_ref[...], w2_all, preferred_element_type=F32)
        sy_ref[...] = _rmsnorm(ys, gf_ref[...])


def _moe_down(tab, hs_sorted, xp, mcol, w2_b, gf, s_hid, s_x):
    n = xp.shape[0]
    n_s = s_x.shape[0]
    nblk = n // MOE_BLOCK
    g = MOE_DOWN_BLOCKS
    assert nblk % g == 0
    whole = lambda shape: pl.BlockSpec(shape, lambda i, tab: (0,) * len(shape))
    row_blk = lambda w: pl.BlockSpec((g * MOE_BLOCK, w), lambda i, tab: (i, 0))
    return pl.pallas_call(
        functools.partial(_moe_down_kernel, flags_at=nblk * N_EXPERTS * 3),
        grid_spec=pltpu.PrefetchScalarGridSpec(
            num_scalar_prefetch=1,
            grid=(nblk // g,),
            in_specs=[pl.BlockSpec((g, MOE_ROWS, D_EXPERT), lambda i, tab: (i, 0, 0)),
                      row_blk(D_MODEL), row_blk(ROUTER_LANES),
                      pl.BlockSpec(w2_b.shape, lambda i, tab: (0, 0, 0), pipeline_mode=pl.Buffered(1)),
                      whole((1, D_MODEL)), whole(s_hid.shape), whole(s_x.shape)],
            out_specs=[row_blk(D_MODEL), whole((n_s, D_MODEL))],
            scratch_shapes=[pltpu.VMEM((g, MOE_ROWS, D_MODEL), BF16)]),
        out_shape=[jax.ShapeDtypeStruct((n, D_MODEL), F32), jax.ShapeDtypeStruct((n_s, D_MODEL), F32)],
        compiler_params=pltpu.CompilerParams(
            dimension_semantics=("arbitrary",), vmem_limit_bytes=V7X_VMEM_LIMIT),
        name="moe_down_combine",
    )(tab, hs_sorted, xp, mcol, w2_b, gf, s_hid, s_x)


def _sample_proj_kernel(x_ref, g1_ref, w_ref, cos_ref, sin_ref, gv_ref, ones_ref, w00_ref, b0_ref,
                        rep_ref, foldt_ref, a_ref, k_ref, v_ref, vn_ref, qkvt_ref):
    h = _rmsnorm(x_ref[...], g1_ref[...])

    def proj(i):
        return _dot(h, w_ref[:, i * WIDTH:(i + 1) * WIDTH], precise=True)

    cos = _tile_lanes(cos_ref[...], WIDTH // 128)
    sin = _tile_lanes(sin_ref[...], WIDTH // 128)
    q = _rope(proj(2), cos, sin) * (HEAD_DIM ** -0.5)
    k = _rope(proj(3), cos, sin)
    v = proj(4)
    vn = _group_rmsnorm(proj(1), ones_ref[...], gv_ref[...], precise=True)
    a_ref[...] = proj(0) * (w00_ref[...] * vn + b0_ref[...])
    k_ref[...] = k
    v_ref[...] = v
    vn_ref[...] = vn

    n_rep = rep_ref.shape[0]
    r_idx = lax.broadcasted_iota(jnp.int32, (n_rep, WIDTH), 0)
    l_idx = lax.broadcasted_iota(jnp.int32, (n_rep, WIDTH), 1)
    own = (l_idx // HEAD_DIM) == (r_idx % N_HEADS)
    for t, src in enumerate((q, k, v)):
        rep = _dot(rep_ref[...], src, precise=True)
        qkvt_ref[t * HEAD_DIM:(t + 1) * HEAD_DIM, :] = lax.dot_general(
            foldt_ref[...], jnp.where(own, rep, 0.0), (((1,), (1,)), ((), ())),
            preferred_element_type=F32, precision=lax.Precision.HIGHEST)


def _sample_proj(x, g1, w_in, cos, sin, gv, ones_bd, w00, b0, rep, foldt):
    bd = x.shape[0]
    sds = lambda r, c: jax.ShapeDtypeStruct((r, c), F32)
    return pl.pallas_call(
        _sample_proj_kernel,
        out_shape=[sds(bd, WIDTH), sds(bd, WIDTH), sds(bd, WIDTH), sds(bd, WIDTH),
                   sds(3 * HEAD_DIM, bd * N_HEADS)],
        compiler_params=pltpu.CompilerParams(vmem_limit_bytes=V7X_VMEM_LIMIT),
        name="sample_proj",
    )(x, g1, w_in, cos, sin, gv, ones_bd, w00, b0, rep, foldt)


def _sample_scores(qkvt_ref, k_ref, b, win):
    n_col = qkvt_ref.shape[1]
    c_idx = lax.broadcasted_iota(jnp.int32, (n_col, 128), 0)
    l_idx = lax.broadcasted_iota(jnp.int32, (n_col, 128), 1)
    pick = jnp.where((c_idx == b * N_HEADS + l_idx) & (l_idx < N_HEADS), 1.0, 0.0)
    cols = _dot(qkvt_ref[...], pick, precise=True)

    head = lax.broadcasted_iota(jnp.int32, (N_HEADS, win), 0)
    s = jnp.zeros((N_HEADS, win), F32)
    s_self = jnp.zeros((N_HEADS, 1), F32)
    for h in range(N_HEADS):
        qc = cols[0:HEAD_DIM, h:h + 1]
        kc = cols[HEAD_DIM:2 * HEAD_DIM, h:h + 1]
        s_h = jnp.sum(k_ref[0, h] * qc, axis=0, keepdims=True)
        s = jnp.where(head == h, s_h, s)
        s_self = jnp.where(head[:, 0:1] == h, jnp.sum(qc * kc, axis=0, keepdims=True), s_self)
    return s, s_self, cols[2 * HEAD_DIM:3 * HEAD_DIM, :]


def _sample_finish(s, s_self, v_cols, v_ref, o_ref, win):
    dist = win - lax.broadcasted_iota(jnp.int32, (1, win), 1)
    members = [(dist <= N_KEYS * dil) & (dist % dil == 0) for dil in DILATIONS]
    es, e_selfs, dens, lses = [], [], [], []
    for mem in members:
        sm = jnp.where(mem, s, NEG)
        m = jnp.maximum(jnp.max(sm, axis=1, keepdims=True), s_self)
        e = jnp.exp(sm - m)
        e_self = jnp.exp(s_self - m)
        den = jnp.sum(e, axis=1, keepdims=True) + e_self
        es.append(e)
        e_selfs.append(e_self)
        dens.append(den)
        lses.append(m + jnp.log(den))
    top = jnp.maximum(jnp.maximum(lses[0], lses[1]), lses[2])
    ws = [jnp.exp(l - top) for l in lses]
    wsum = ws[0] + ws[1] + ws[2]
    coef = [w / (den * wsum) for w, den in zip(ws, dens)]
    p_keys = coef[0] * es[0] + coef[1] * es[1] + coef[2] * es[2]
    p_self = coef[0] * e_selfs[0] + coef[1] * e_selfs[1] + coef[2] * e_selfs[2]

    for h in range(N_HEADS):
        o_ref[0, :, h:h + 1] = (jnp.sum(v_ref[0, h] * p_keys[h:h + 1, :], axis=1, keepdims=True)
                                + p_self[h:h + 1, :] * v_cols[:, h:h + 1])


def _rope_tables(first_pos, count):
    half = HEAD_DIM // 2
    inv = ROPE_THETA ** (-np.arange(half, dtype=np.float64) * 2.0 / HEAD_DIM)
    ang = (first_pos + np.arange(count, dtype=np.float64))[:, None] * inv[None, :]
    cos, sin = np.cos(ang), np.sin(ang)
    cos128 = np.concatenate([cos, cos, cos, cos], axis=1).astype(np.float32)
    sin128 = np.concatenate([-sin, sin, -sin, sin], axis=1).astype(np.float32)
    return jnp.asarray(cos128), jnp.asarray(sin128)


def kernel(x_prompt, x_sample, cache_win_k, cache_win_v, ln1_g, w_in, sgu_norm_g, sgu_w, sgu_b, w_out, ln2_g,
           w_router_group, b_router_group, w_router_expert, b_router_expert, w_gate, w_up, w_down, lnf_g):
    depth = w_in.shape[0]
    assert depth == 1 and x_sample.shape[1] == 1
    B, S, _ = x_prompt.shape
    bd = x_sample.shape[0]
    win = cache_win_k.shape[2]
    assert S % (max(DILATIONS) * CHUNK) == 0 and win >= max(DILATIONS) * N_KEYS and PAST_LEN % CHUNK == 0
    l = 0

    pad = ROUTER_LANES - N_GROUPS - N_EXPERTS
    w_router = jnp.pad(jnp.concatenate([w_router_group[l], w_router_expert[l]], axis=1), ((0, 0), (0, pad)))
    b_router = jnp.pad(jnp.concatenate([b_router_group[l], b_router_expert[l]]), (0, pad))[None, :]
    g1 = ln1_g[l][None, :]
    g2 = ln2_g[l][None, :]
    gf = lnf_g[None, :]
    gv = sgu_norm_g[l].reshape(1, WIDTH)
    grp = np.arange(WIDTH) // HEAD_DIM
    ones_bd = jnp.asarray(np.where(grp[:, None] == grp[None, :], 1.0 / HEAD_DIM, 0.0), BF16)
    wp = jnp.concatenate([sgu_w[l][0::2], sgu_w[l][1::2]], axis=-1)
    bias = jnp.repeat(sgu_b[l].T, HEAD_DIM, axis=1)
    w00 = jnp.repeat(sgu_w[l][:, 0, 0], HEAD_DIM)[None, :]
    b0 = jnp.repeat(sgu_b[l][:, 0], HEAD_DIM)[None, :]

    cos_s, sin_s = _rope_tables(PAST_LEN, 1)
    rep = jnp.asarray(np.arange(bd * N_HEADS)[:, None] // N_HEADS == np.arange(bd)[None, :], F32)
    foldt = jnp.asarray(np.arange(HEAD_DIM)[:, None] == np.arange(WIDTH)[None, :] % HEAD_DIM, F32)
    xs = x_sample.reshape(bd, D_MODEL)
    a_s, k_s, v_s, vn_s, qkvt = _sample_proj(xs, g1, w_in[l], cos_s, sin_s, gv, ones_bd, w00, b0, rep, foldt)
    to_pos_minor = lambda c: jnp.transpose(c, (0, 2, 3, 1))

    cos_p, sin_p = _rope_tables(0, S)
    a_p, q_p, k_p, v_p, kt_p, vt_p, o3, w1_b, w2_b = _prompt_proj(
        x_prompt, g1, w_in[l], cos_p, sin_p, gv, ones_bd, wp, bias,
        qkvt, to_pos_minor(cache_win_k[l]), to_pos_minor(cache_win_v[l]), w_gate[l], w_up[l], w_down[l])
    b_p = _prompt_attention(q_p, k_p, v_p)
    n = B * S
    assert n % MOE_BLOCK == 0
    wr_t = w_router.T[:ROUTER_ROWS]
    wr_hi = wr_t.astype(BF16)
    wr_hl = jnp.concatenate([wr_hi, (wr_t - wr_hi.astype(F32)).astype(BF16)], axis=0)
    lane_idx = np.arange(ROUTER_LANES)
    tri = jnp.asarray(lane_idx[:, None] <= lane_idx[None, :], BF16)
    ltri = jnp.asarray(lane_idx[None, :] < lane_idx[:, None], BF16)
    xp2, h2, mrow, mcol, tab_f = _mix_route_sort(
        a_p.reshape(n, WIDTH), b_p.reshape(n, WIDTH), x_prompt.reshape(n, D_MODEL),
        w_out[l], g2, wr_hl, b_router.reshape(ROUTER_LANES, 1)[:ROUTER_ROWS], tri, ltri)
    tab_i = tab_f[:, :N_EXPERTS, 0:3].astype(jnp.int32)
    multi_chunk = (jnp.max(tab_i[:, :, 1], axis=1) > 1).astype(jnp.int32)
    tab = jnp.concatenate([tab_i.reshape(-1), multi_chunk])

    b_s = jnp.transpose(o3, (0, 2, 1)).reshape(bd, WIDTH)
    xs2, hs2, gates_s = _mix_router(a_s, b_s, xs, w_out[l], g2, w_router, b_router, tm=bd, precise=True)

    hid_sorted, hid_s = _moe_gate_up(tab, h2, mrow, w1_b, hs2, gates_s)
    y_prompt, y_sample = _moe_down(tab, hid_sorted, xp2, mcol, w2_b, gf, hid_s, xs2)
    y_prompt = y_prompt.reshape(B, S, D_MODEL)
    y_sample = y_sample.reshape(bd, 1, D_MODEL)
    buf_p = min(MAX_WINDOW, S)
    to_win = lambda t: jnp.transpose(t.reshape(1, B, N_HEADS, HEAD_DIM, buf_p), (0, 1, 4, 2, 3))
    new_k_p = to_win(kt_p)
    new_v_p = to_win(vt_p)

    shape_s = (1, bd, 1, N_HEADS, HEAD_DIM)
    return (y_prompt, y_sample, new_k_p, new_v_p,
            k_s.reshape(shape_s), v_s.reshape(shape_s), vn_s.reshape(shape_s))
```

```python
import functools

import jax
import jax.numpy as jnp
import numpy as np
from jax import lax
from jax.experimental import pallas as pl
from jax.experimental.pallas import tpu as pltpu

F32 = jnp.float32
BF16 = jnp.bfloat16

D_MODEL = 1024
HEAD_DIM = 64
N_HEADS = 8
WIDTH = N_HEADS * HEAD_DIM
PROJ_COLS = 5 * WIDTH
CHUNK = 128
DILATIONS = (1, 4, 16)
N_KEYS = 128
MAX_WINDOW = 2048
PAST_LEN = 16384
ROPE_THETA = 10000.0
N_GROUPS = 4
EXPERTS_PER_GROUP = 8
N_EXPERTS = N_GROUPS * EXPERTS_PER_GROUP
D_EXPERT = 128
EPS = 1e-6
NEG = -1e30
TILES_PER_STEP = 32
SUBLANE_STRIDE = 4
assert DILATIONS == (1, SUBLANE_STRIDE, SUBLANE_STRIDE ** 2)
MOE_BLOCK = 512
MOE_ROW_ALIGN = 16
MOE_CHUNK = 64
MOE_EXPERTS_PER_STEP = 16
MOE_ROUTE_BLOCKS = 2
MOE_GATE_UP_BLOCKS = 4
MOE_DOWN_BLOCKS = 2
MOE_SORT_ROWS = -(-(2 * MOE_BLOCK + N_EXPERTS * (MOE_ROW_ALIGN - 1)) // 512) * 512
MOE_ROWS = MOE_SORT_ROWS + MOE_CHUNK
ROUTER_ROWS = 48
ROUTER_LANES = 128
V7X_VMEM_LIMIT = 56 * 1024 * 1024


def _rmsnorm(x, g):
    return x * lax.rsqrt(jnp.mean(x * x, axis=-1, keepdims=True) + EPS) * g


def _tile_lanes(t, reps):
    return jnp.concatenate([t] * reps, axis=1)


def _rope(t, cos, sin_signed):
    lane = lax.broadcasted_iota(jnp.int32, t.shape, 1)
    first_half = (lane % HEAD_DIM) < (HEAD_DIM // 2)
    n = t.shape[1]
    partner = jnp.where(first_half, pltpu.roll(t, n - HEAD_DIM // 2, 1), pltpu.roll(t, HEAD_DIM // 2, 1))
    return t * cos + partner * sin_signed


def _dot(a, b, precise=False):
    if precise:
        return jnp.dot(a.astype(F32), b.astype(F32), preferred_element_type=F32,
                       precision=lax.Precision.HIGHEST)
    return jnp.dot(a.astype(BF16), b.astype(BF16), preferred_element_type=F32)


def _group_rmsnorm(va, ones_bd, gv, precise=False):
    ms = _dot(va * va, ones_bd, precise)
    return va * lax.rsqrt(ms + EPS) * gv


def _proj_kernel(x_ref, g1_ref, w_ref, cos_ref, sin_ref, gv_ref, ones_ref, wp_ref, bias_ref,
                 qkvt_ref, ck_ref, cv_ref,
                 a_ref, q_ref, k_ref, v_ref, kt_ref, vt_ref, so_ref, wb, *, tm, first_win_tile, n_seq, win):
    @pl.when((pl.program_id(0) == 0) & (pl.program_id(1) == 0))
    def _():
        wb[...] = w_ref[...].astype(BF16)

    h = _rmsnorm(x_ref[0], g1_ref[...]).astype(BF16)

    def proj(i):
        return jnp.dot(h, wb[:, i * WIDTH:(i + 1) * WIDTH], preferred_element_type=F32)

    cos = _tile_lanes(cos_ref[...], WIDTH // 128)
    sin = _tile_lanes(sin_ref[...], WIDTH // 128)
    q_ref[0] = _rope(proj(2), cos, sin) * (HEAD_DIM ** -0.5)
    k_ref[0] = _rope(proj(3), cos, sin)
    v_ref[0] = proj(4)

    s_seq = jnp.minimum(pl.program_id(0) * pl.num_programs(1) + pl.program_id(1), n_seq - 1)
    s_scores, s_self, s_vcols = _sample_scores(qkvt_ref, ck_ref, s_seq, win)

    u = proj(0)
    vn = _group_rmsnorm(proj(1), ones_ref[...], gv_ref[...]).astype(BF16)

    lane = lax.broadcasted_iota(jnp.int32, (CHUNK, 128), 1)
    left = lane < HEAD_DIM
    row = lax.broadcasted_iota(jnp.int32, (CHUNK, 2 * CHUNK), 0)
    col = lax.broadcasted_iota(jnp.int32, (CHUNK, 2 * CHUNK), 1)
    causal = (col % CHUNK) <= row
    zero = jnp.zeros((CHUNK, 128), BF16)
    wps = [jnp.where(causal, wp_ref[gp], 0.0).astype(BF16) for gp in range(N_HEADS // 2)]
    def block_diag(vv):
        return jnp.concatenate([jnp.where(left, vv, zero), jnp.where(left, zero, vv)], axis=0)

    for c in range(0, tm // CHUNK, 2):
        rows = [slice((c + i) * CHUNK, (c + i + 1) * CHUNK) for i in range(2)]
        mixes = [[], []]
        for gp in range(N_HEADS // 2):
            lanes = slice(gp * 128, (gp + 1) * 128)
            v2 = jnp.concatenate([block_diag(vn[rows[0], lanes]), block_diag(vn[rows[1], lanes])], axis=1)
            both = jnp.dot(wps[gp], v2, preferred_element_type=F32)
            mixes[0].append(both[:, 0:128])
            mixes[1].append(both[:, 128:256])
        for i in range(2):
            mix = jnp.concatenate(mixes[i], axis=1) + bias_ref[...]
            a_ref[0, rows[i], :] = (u[rows[i], :] * mix).astype(a_ref.dtype)

    _sample_finish(s_scores, s_self, s_vcols, cv_ref, so_ref, win)

    @pl.when(pl.program_id(1) >= first_win_tile)
    def _():
        kt_ref[0] = k_ref[0].T
        vt_ref[0] = v_ref[0].T


def _prompt_proj(x, g1, w_in_b, cos, sin, gv, ones_bd, wp, bias, qkvt, cache_k_t, cache_v_t, *, tm=512):
    B, S, _ = x.shape
    const2 = lambda b, j: (0, 0)
    out_sds = lambda dt: jax.ShapeDtypeStruct((B, S, WIDTH), dt)
    tile = pl.BlockSpec((1, tm, WIDTH), lambda b, j: (b, j, 0))
    win = min(MAX_WINDOW, S)
    first_win_tile = (S - win) // tm
    tile_t = pl.BlockSpec((1, WIDTH, tm), lambda b, j: (b, 0, jnp.maximum(j - first_win_tile, 0)))
    win_sds = jax.ShapeDtypeStruct((B, WIDTH, win), F32)
    n_seq, _, _, cache_win = cache_k_t.shape
    n_tiles = S // tm
    assert n_seq <= B * n_tiles, "one sample sequence rides on each grid step"
    seq_blk = lambda shape: pl.BlockSpec(
        (1,) + shape, lambda b, j: (jnp.minimum(b * n_tiles + j, n_seq - 1),) + (0,) * len(shape))
    cache_blk = seq_blk((N_HEADS, HEAD_DIM, cache_win))
    return pl.pallas_call(
        functools.partial(_proj_kernel, tm=tm, first_win_tile=first_win_tile, n_seq=n_seq, win=cache_win),
        grid=(B, n_tiles),
        in_specs=[
            pl.BlockSpec((1, tm, D_MODEL), lambda b, j: (b, j, 0)),
            pl.BlockSpec((1, D_MODEL), const2),
            pl.BlockSpec((D_MODEL, PROJ_COLS), const2, pipeline_mode=pl.Buffered(1)),
            pl.BlockSpec((tm, 128), lambda b, j: (j, 0)),
            pl.BlockSpec((tm, 128), lambda b, j: (j, 0)),
            pl.BlockSpec((1, WIDTH), const2),
            pl.BlockSpec((WIDTH, WIDTH), const2),
            pl.BlockSpec((N_HEADS // 2, CHUNK, 2 * CHUNK), lambda b, j: (0, 0, 0)),
            pl.BlockSpec((CHUNK, WIDTH), const2),
            pl.BlockSpec(qkvt.shape, const2), cache_blk, cache_blk,
        ],
        out_specs=[tile, tile, tile, tile, tile_t, tile_t, seq_blk((HEAD_DIM, N_HEADS))],
        out_shape=[out_sds(BF16), out_sds(F32), out_sds(F32), out_sds(F32), win_sds, win_sds,
                   jax.ShapeDtypeStruct((n_seq, HEAD_DIM, N_HEADS), F32)],
        scratch_shapes=[pltpu.VMEM((D_MODEL, PROJ_COLS), BF16)],
        compiler_params=pltpu.CompilerParams(
            dimension_semantics=("arbitrary", "arbitrary"), vmem_limit_bytes=V7X_VMEM_LIMIT),
        name="prompt_proj_sgu",
    )(x, g1, w_in_b, cos, sin, gv, ones_bd, wp, bias, qkvt, cache_k_t, cache_v_t)


def _attn_kernel(q_ref, k_ref, v_ref, wg_ref, wu_ref, wd_ref, o_ref, w1_ref, w2_ref,
                 qd, kd, vd, res_o, res_l, nat_o, nat_l, bias, *, seq):
    n_tiles = seq // CHUNK
    last = len(DILATIONS) - 1
    stage_in = (res_o.at[last], res_l.at[last], nat_o.at[last - 1])
    stage_out = (res_o.at[last - 1], res_l.at[last - 1])

    w1_ref[:, :, 0:D_EXPERT] = wg_ref[...].astype(BF16)
    w1_ref[:, :, D_EXPERT:2 * D_EXPERT] = wu_ref[...].astype(BF16)
    w2_ref[...] = wd_ref[...].astype(BF16)

    lane = lax.broadcasted_iota(jnp.int32, (CHUNK, 128), 1)
    left = lane < HEAD_DIM
    qi2 = lax.broadcasted_iota(jnp.int32, (2 * CHUNK, 2 * CHUNK), 0) % CHUNK
    kj2 = lax.broadcasted_iota(jnp.int32, (2 * CHUNK, 2 * CHUNK), 1)
    dist2 = CHUNK + qi2 - kj2
    band2 = (dist2 >= 0) & (dist2 <= N_KEYS)
    zero_q = jnp.zeros((CHUNK, 128), BF16)
    bias[0] = jnp.where(band2, 0.0, NEG)
    bias[1] = jnp.where(band2 & (kj2 >= CHUNK), 0.0, NEG)

    kd[0:CHUNK, :] = jnp.zeros((CHUNK, 128), BF16)
    vd[0:CHUNK, :] = jnp.zeros((CHUNK, 128), BF16)

    for p, dil in enumerate(DILATIONS):
        sub = seq // dil
        nb = sub // CHUNK
        for ti, (src_ref, dst, pad) in enumerate(((q_ref, qd, 0), (k_ref, kd, CHUNK), (v_ref, vd, CHUNK))):
            if dil == 1:
                dst[pad:pad + seq, :] = src_ref[0].astype(BF16)
            elif dil == SUBLANE_STRIDE:
                for r in range(dil):
                    val = src_ref[0, pl.ds(r, sub, stride=dil), :]
                    stage_in[ti][r * sub:(r + 1) * sub, :] = val
                    dst[pad + r * sub:pad + (r + 1) * sub, :] = val.astype(BF16)
            else:
                coarse = seq // SUBLANE_STRIDE
                for r_lo in range(SUBLANE_STRIDE):
                    for r_hi in range(SUBLANE_STRIDE):
                        r = r_lo + SUBLANE_STRIDE * r_hi
                        val = stage_in[ti][pl.ds(r_lo * coarse + r_hi, sub, stride=SUBLANE_STRIDE), :]
                        dst[pad + r * sub:pad + (r + 1) * sub, :] = val.astype(BF16)

        def tile_body(g, i, p=p, nb=nb):
            t = g * TILES_PER_STEP + i
            row = pl.multiple_of(t * CHUNK, CHUNK)
            qt = qd[pl.ds(row, CHUNK), :]
            k2 = kd[pl.ds(row, 2 * CHUNK), :]
            v2 = vd[pl.ds(row, 2 * CHUNK), :]
            if TILES_PER_STEP % nb == 0:
                variant = 1 if i % nb == 0 else 0
            elif i == 0:
                variant = jnp.where((g * TILES_PER_STEP) % nb == 0, 1, 0)
            else:
                variant = 0
            q2 = jnp.concatenate([jnp.where(left, qt, zero_q), jnp.where(left, zero_q, qt)], axis=0)
            s = lax.dot_general(q2, k2, (((1,), (1,)), ((), ())), preferred_element_type=F32)
            s = s + bias[variant]
            m = jnp.max(s, axis=1, keepdims=True)
            e = jnp.exp(s - m)
            den = jnp.sum(e, axis=1, keepdims=True)
            pv = jnp.dot(e.astype(BF16), v2, preferred_element_type=F32) / den
            lse = jnp.broadcast_to(m + jnp.log(den), (2 * CHUNK, 128))
            res_o[p, pl.ds(row, CHUNK), :] = jnp.where(left, pv[0:CHUNK], pv[CHUNK:2 * CHUNK])
            res_l[p, pl.ds(row, CHUNK), :] = jnp.where(left, lse[0:CHUNK], lse[CHUNK:2 * CHUNK])

        def group_body(g, carry, tile_body=tile_body):
            for i in range(TILES_PER_STEP):
                tile_body(g, i)
            return carry

        lax.fori_loop(0, n_tiles // TILES_PER_STEP, group_body, 0)

    for p, dil in enumerate(DILATIONS):
        if dil == 1:
            continue
        sub = seq // dil
        for si, (res, nat) in enumerate(((res_o, nat_o), (res_l, nat_l))):
            if dil == SUBLANE_STRIDE:
                for r in range(dil):
                    nat[p - 1, pl.ds(r, sub, stride=dil), :] = res[p, r * sub:(r + 1) * sub, :]
            else:
                coarse = seq // SUBLANE_STRIDE
                for r_lo in range(SUBLANE_STRIDE):
                    for r_hi in range(SUBLANE_STRIDE):
                        r = r_lo + SUBLANE_STRIDE * r_hi
                        stage_out[si][pl.ds(r_lo * coarse + r_hi, sub, stride=SUBLANE_STRIDE), :] = (
                            res[p, r * sub:(r + 1) * sub, :])
                for r_lo in range(SUBLANE_STRIDE):
                    nat[p - 1, pl.ds(r_lo, coarse, stride=SUBLANE_STRIDE), :] = (
                        stage_out[si][r_lo * coarse:(r_lo + 1) * coarse, :])

    rows_per_step = 256

    def merge_body(c, carry):
        rows = pl.ds(pl.multiple_of(c * rows_per_step, rows_per_step), rows_per_step)
        l0, l1, l2 = res_l[0, rows, :], nat_l[0, rows, :], nat_l[1, rows, :]
        top = jnp.maximum(jnp.maximum(l0, l1), l2)
        w0, w1, w2 = jnp.exp(l0 - top), jnp.exp(l1 - top), jnp.exp(l2 - top)
        num = w0 * res_o[0, rows, :] + w1 * nat_o[0, rows, :] + w2 * nat_o[1, rows, :]
        o_ref[0, rows, :] = (num / (w0 + w1 + w2)).astype(o_ref.dtype)
        return carry

    lax.fori_loop(0, seq // rows_per_step, merge_body, 0)


def _prompt_attention(q, k, v, w_gate, w_up, w_down):
    B, S, _ = q.shape
    n_pairs = WIDTH // 128
    blk = pl.BlockSpec((1, S, 128), lambda b, hp: (b, 0, hp))
    n_steps = B * n_pairs
    assert N_EXPERTS % n_steps == 0
    e_blk = N_EXPERTS // n_steps
    expert_blk = lambda rows, cols: pl.BlockSpec((e_blk, rows, cols), lambda b, hp: (b * n_pairs + hp, 0, 0))
    return pl.pallas_call(
        functools.partial(_attn_kernel, seq=S),
        grid=(B, n_pairs),
        in_specs=[blk, blk, blk,
                  expert_blk(D_MODEL, D_EXPERT), expert_blk(D_MODEL, D_EXPERT), expert_blk(D_EXPERT, D_MODEL)],
        out_specs=[blk, expert_blk(D_MODEL, 2 * D_EXPERT), expert_blk(D_EXPERT, D_MODEL)],
        out_shape=[jax.ShapeDtypeStruct((B, S, WIDTH), BF16),
                   jax.ShapeDtypeStruct((N_EXPERTS, D_MODEL, 2 * D_EXPERT), BF16),
                   jax.ShapeDtypeStruct((N_EXPERTS, D_EXPERT, D_MODEL), BF16)],
        scratch_shapes=[
            pltpu.VMEM((S, 128), BF16),
            pltpu.VMEM((S + CHUNK, 128), BF16),
            pltpu.VMEM((S + CHUNK, 128), BF16),
            pltpu.VMEM((len(DILATIONS), S, 128), F32),
            pltpu.VMEM((len(DILATIONS), S, 128), F32),
            pltpu.VMEM((len(DILATIONS) - 1, S, 128), F32),
            pltpu.VMEM((len(DILATIONS) - 1, S, 128), F32),
            pltpu.VMEM((2, 2 * CHUNK, 2 * CHUNK), F32),
        ],
        compiler_params=pltpu.CompilerParams(
            dimension_semantics=("arbitrary", "arbitrary"), vmem_limit_bytes=V7X_VMEM_LIMIT),
        name="prompt_dilated_attention",
    )(q, k, v, w_gate, w_up, w_down)


def _route(logits):
    lane = lax.broadcasted_iota(jnp.int32, logits.shape, 1)
    big = jnp.int32(ROUTER_LANES)
    lg = jnp.where(lane < N_GROUPS, logits, NEG)
    gmax = jnp.max(lg, axis=1, keepdims=True)
    gp = 1.0 / jnp.sum(jnp.exp(lg - gmax), axis=1, keepdims=True)
    gi = jnp.min(jnp.where(lg == gmax, lane, big), axis=1, keepdims=True)
    lo = N_GROUPS + EXPERTS_PER_GROUP * gi
    le = jnp.where((lane >= lo) & (lane < lo + EXPERTS_PER_GROUP), logits, NEG)
    m1 = jnp.max(le, axis=1, keepdims=True)
    i1 = jnp.min(jnp.where(le == m1, lane, big), axis=1, keepdims=True)
    le2 = jnp.where(lane == i1, NEG, le)
    m2 = jnp.max(le2, axis=1, keepdims=True)
    i2 = jnp.min(jnp.where(le2 == m2, lane, big), axis=1, keepdims=True)
    e2 = jnp.exp(m2 - m1)
    w1 = 1.0 / (1.0 + e2)
    w2 = e2 / (1.0 + e2)
    return jnp.where(lane == i1, gp * w1, jnp.where(lane == i2, gp * w2, 0.0))


def _mix_router_kernel(a_ref, b_ref, x_ref, wo_ref, g2_ref, wr_ref, br_ref, xp_ref, h2_ref, gates_ref, *, precise):
    mixed = (_dot(a_ref[...], wo_ref[0:WIDTH, :], precise)
             + _dot(b_ref[...], wo_ref[WIDTH:2 * WIDTH, :], precise))
    xp = x_ref[...] + mixed
    xp_ref[...] = xp
    h2 = _rmsnorm(xp, g2_ref[...])
    h2_ref[...] = h2.astype(h2_ref.dtype)
    logits = jnp.dot(h2, wr_ref[...], preferred_element_type=F32,
                     precision=lax.Precision.HIGHEST) + br_ref[...]
    gates_ref[...] = _route(logits)


def _mix_router(a, b, x, w_out, g2, w_router, b_router, *, tm, precise=False):
    n = x.shape[0]
    const = lambda i: (0, 0)
    row_blk = lambda w: pl.BlockSpec((tm, w), lambda i: (i, 0))
    return pl.pallas_call(
        functools.partial(_mix_router_kernel, precise=precise),
        grid=(n // tm,),
        in_specs=[row_blk(WIDTH), row_blk(WIDTH), row_blk(D_MODEL),
                  pl.BlockSpec((2 * WIDTH, D_MODEL), const),
                  pl.BlockSpec((1, D_MODEL), const),
                  pl.BlockSpec((D_MODEL, ROUTER_LANES), const),
                  pl.BlockSpec((1, ROUTER_LANES), const)],
        out_specs=[row_blk(D_MODEL), row_blk(D_MODEL), row_blk(ROUTER_LANES)],
        out_shape=[jax.ShapeDtypeStruct((n, D_MODEL), F32),
                   jax.ShapeDtypeStruct((n, D_MODEL), BF16),
                   jax.ShapeDtypeStruct((n, ROUTER_LANES), F32)],
        compiler_params=pltpu.CompilerParams(
            dimension_semantics=("arbitrary",), vmem_limit_bytes=V7X_VMEM_LIMIT),
        name="outproj_router",
    )(a, b, x, w_out, g2, w_router, b_router)


def _nt_dot(w, t):
    return lax.dot_general(w, t, (((1,), (1,)), ((), ())), preferred_element_type=F32)


def _route_t(logits_t):
    row = lax.broadcasted_iota(jnp.int32, logits_t.shape, 0)
    big = jnp.int32(ROUTER_LANES)
    lg = jnp.where(row < N_GROUPS, logits_t, NEG)
    gmax = jnp.max(lg, axis=0, keepdims=True)
    gp = 1.0 / jnp.sum(jnp.exp(lg - gmax), axis=0, keepdims=True)
    gi = jnp.min(jnp.where(lg == gmax, row, big), axis=0, keepdims=True)
    lo = N_GROUPS + EXPERTS_PER_GROUP * gi
    le = jnp.where((row >= lo) & (row < lo + EXPERTS_PER_GROUP), logits_t, NEG)
    m1 = jnp.max(le, axis=0, keepdims=True)
    i1 = jnp.min(jnp.where(le == m1, row, big), axis=0, keepdims=True)
    le2 = jnp.where(row == i1, NEG, le)
    m2 = jnp.max(le2, axis=0, keepdims=True)
    i2 = jnp.min(jnp.where(le2 == m2, row, big), axis=0, keepdims=True)
    e2 = jnp.exp(m2 - m1)
    return i1 - N_GROUPS, i2 - N_GROUPS, gp / (1.0 + e2), gp * e2 / (1.0 + e2)


def _mix_route_sort_kernel(a_ref, b_ref, x_ref, wo_ref, g2_ref, wr_ref, brc_ref, tri_ref, ltri_ref,
                           xp_ref, h2_ref, mrow_ref, mcol_ref, tab_ref, wob):
    t = MOE_BLOCK

    @pl.when(pl.program_id(0) == 0)
    def _():
        wob[...] = wo_ref[...].astype(BF16)

    def project(j):
        rows = slice(j * t, (j + 1) * t)
        xp = x_ref[rows, :] + _dot(a_ref[rows, :], wob[0:WIDTH, :]) + _dot(b_ref[rows, :], wob[WIDTH:2 * WIDTH, :])
        xp_ref[rows, :] = xp
        h2 = _rmsnorm(xp, g2_ref[...])
        hi = h2.astype(BF16)
        h2_ref[rows, :] = hi
        return hi, (h2 - hi.astype(F32)).astype(BF16)

    def route(hi, lo):
        prod_hi = _nt_dot(wr_ref[...], hi)
        logits_t = (prod_hi[0:ROUTER_ROWS] + prod_hi[ROUTER_ROWS:2 * ROUTER_ROWS]
                    + _nt_dot(wr_ref[0:ROUTER_ROWS, :], lo) + brc_ref[...])
        return _route_t(logits_t)

    def sort_meta(j, ex1, ex2, gate1, gate2):
        rows = slice(j * t, (j + 1) * t)
        pair_e = jnp.concatenate([ex1, ex2], axis=1)
        row = lax.broadcasted_iota(jnp.int32, (N_EXPERTS, 2 * t), 0)
        onehot = jnp.where(row == pair_e, 1.0, 0.0)
        n_lane_tiles = 2 * t // 128
        local = _dot(jnp.concatenate([onehot[:, k * 128:(k + 1) * 128] for k in range(n_lane_tiles)], axis=0),
                     tri_ref[...])
        carry = jnp.zeros((N_EXPERTS, 1), F32)
        cums = []
        for k in range(n_lane_tiles):
            tile = local[k * N_EXPERTS:(k + 1) * N_EXPERTS, :]
            cums.append(tile + carry)
            carry = carry + tile[:, 127:128]
        cum = jnp.concatenate(cums, axis=1)
        rank = jnp.sum(onehot * cum, axis=0, keepdims=True) - 1.0
        counts = carry
        units32 = jnp.floor((counts + (MOE_ROW_ALIGN - 1)) * (1.0 / MOE_ROW_ALIGN))
        units = jnp.concatenate([jnp.broadcast_to(units32, (N_EXPERTS, 128)),
                                 jnp.zeros((ROUTER_LANES - N_EXPERTS, 128), F32)], axis=0)
        off = _dot(ltri_ref[...], units) * MOE_ROW_ALIGN
        dst = jnp.sum(onehot * off[0:N_EXPERTS, 0:1], axis=0, keepdims=True) + rank

        r8 = lax.broadcasted_iota(jnp.int32, (8, t), 0)
        mrow_ref[j] = jnp.where(r8 == 0, dst[:, 0:t], jnp.where(r8 == 1, dst[:, t:2 * t],
                                jnp.where(r8 == 2, gate1, jnp.where(r8 == 3, gate2, 0.0))))
        r128 = lax.broadcasted_iota(jnp.int32, (ROUTER_LANES, t), 0)
        meta = jnp.where(r128 == 0, dst[:, 0:t], jnp.where(r128 == 1, dst[:, t:2 * t],
                         jnp.where(r128 == 2, gate1, jnp.where(r128 == 3, gate2, 0.0))))
        mcol_ref[rows, :] = meta.T
        lane = lax.broadcasted_iota(jnp.int32, (ROUTER_LANES, 128), 1)
        n_rows = units * MOE_ROW_ALIGN
        chunks = jnp.floor((n_rows + (MOE_CHUNK - 1)) * (1.0 / MOE_CHUNK))
        tab_ref[j] = jnp.where(lane == 0, off, jnp.where(lane == 1, chunks, jnp.where(lane == 2, off + n_rows, 0.0)))

    blocks = range(MOE_ROUTE_BLOCKS)
    projected = [project(j) for j in blocks]
    routed = [route(hi, lo) for hi, lo in projected]
    for j in blocks:
        sort_meta(j, *routed[j])


def _mix_route_sort(a, b, x, w_out, g2, wr_hl, br_col, tri, ltri):
    n = x.shape[0]
    t = MOE_BLOCK
    nblk = n // t
    g = MOE_ROUTE_BLOCKS
    assert nblk % g == 0
    const = lambda i: (0, 0)
    row_blk = lambda w: pl.BlockSpec((g * t, w), lambda i: (i, 0))
    return pl.pallas_call(
        _mix_route_sort_kernel,
        grid=(nblk // g,),
        in_specs=[row_blk(WIDTH), row_blk(WIDTH), row_blk(D_MODEL),
                  pl.BlockSpec((2 * WIDTH, D_MODEL), const, pipeline_mode=pl.Buffered(1)),
                  pl.BlockSpec((1, D_MODEL), const),
                  pl.BlockSpec((2 * ROUTER_ROWS, D_MODEL), const),
                  pl.BlockSpec((ROUTER_ROWS, 1), const),
                  pl.BlockSpec((ROUTER_LANES, ROUTER_LANES), const),
                  pl.BlockSpec((ROUTER_LANES, ROUTER_LANES), const)],
        out_specs=[row_blk(D_MODEL), row_blk(D_MODEL),
                   pl.BlockSpec((g, 8, t), lambda i: (i, 0, 0)),
                   row_blk(ROUTER_LANES),
                   pl.BlockSpec((g, ROUTER_LANES, 128), lambda i: (i, 0, 0))],
        out_shape=[jax.ShapeDtypeStruct((n, D_MODEL), F32),
                   jax.ShapeDtypeStruct((n, D_MODEL), BF16),
                   jax.ShapeDtypeStruct((nblk, 8, t), F32),
                   jax.ShapeDtypeStruct((n, ROUTER_LANES), F32),
                   jax.ShapeDtypeStruct((nblk, ROUTER_LANES, 128), F32)],
        scratch_shapes=[pltpu.VMEM((2 * WIDTH, D_MODEL), BF16)],
        compiler_params=pltpu.CompilerParams(
            dimension_semantics=("arbitrary",), vmem_limit_bytes=V7X_VMEM_LIMIT),
        name="outproj_route_sort",
    )(a, b, x, w_out, g2, wr_hl, br_col, tri, ltri)


def _silu_mul(ab):
    a = ab[:, :D_EXPERT]
    return a * (1.0 / (1.0 + jnp.exp(-a))) * ab[:, D_EXPERT:]


def _chunk_offsets(tab, n_blk, e):
    return [pl.multiple_of(tab(j, e, 0), MOE_ROW_ALIGN) for j in range(n_blk)]


def _moe_gate_up_kernel(tab_ref, h_ref, mrow_ref, w1_ref, sh_ref, sg_ref, hs_ref, shid_ref, xs, *, flags_at):
    t = MOE_BLOCK
    n_blk = MOE_GATE_UP_BLOCKS
    first = pl.program_id(0) * n_blk
    tab = lambda j, e, c: tab_ref[((first + j) * N_EXPERTS + e) * 3 + c]

    piece = 512
    for j in range(n_blk):
        mrow = mrow_ref[j]
        dst1 = mrow[0:1, :].astype(jnp.int32)
        dst2 = mrow[1:2, :].astype(jnp.int32)
        h = h_ref[j * t:(j + 1) * t, :]
        for r0 in range(0, MOE_SORT_ROWS, piece):
            d_idx = lax.broadcasted_iota(jnp.int32, (piece, t), 0) + r0
            sel = jnp.where((d_idx == dst1) | (d_idx == dst2), 1.0, 0.0)
            xs[j, r0:r0 + piece, :] = _dot(sel, h).astype(BF16)
        xs[j, MOE_SORT_ROWS:MOE_ROWS, :] = jnp.zeros((MOE_CHUNK, D_MODEL), BF16)
    hs_ref[...] = jnp.zeros(hs_ref.shape, BF16)

    def first_chunks(g, carry):
        for i in range(MOE_EXPERTS_PER_STEP):
            e = g * MOE_EXPERTS_PER_STEP + i
            offs = _chunk_offsets(tab, n_blk, e)
            x = jnp.concatenate([xs[j, pl.ds(offs[j], MOE_CHUNK), :] for j in range(n_blk)], axis=0)
            hid = _silu_mul(jnp.dot(x, w1_ref[e], preferred_element_type=F32)).astype(BF16)
            for j in range(n_blk):
                hs_ref[j, pl.ds(offs[j], MOE_CHUNK), :] = hid[j * MOE_CHUNK:(j + 1) * MOE_CHUNK, :]
        return carry

    lax.fori_loop(0, N_EXPERTS // MOE_EXPERTS_PER_STEP, first_chunks, 0)

    def more_chunks(j, e, carry):
        off, n_chunks, end = tab(j, e, 0), tab(j, e, 1), tab(j, e, 2)

        def chunk(c, carry):
            r0 = pl.multiple_of(off + c * MOE_CHUNK, MOE_ROW_ALIGN)
            rows = r0 + lax.broadcasted_iota(jnp.int32, (MOE_CHUNK, D_EXPERT), 0)
            hid = _silu_mul(jnp.dot(xs[j, pl.ds(r0, MOE_CHUNK), :], w1_ref[e], preferred_element_type=F32))
            hs_ref[j, pl.ds(r0, MOE_CHUNK), :] = jnp.where(rows < end, hid.astype(BF16),
                                                           hs_ref[j, pl.ds(r0, MOE_CHUNK), :])
            return carry

        return lax.fori_loop(1, n_chunks, chunk, carry)

    for j in range(n_blk):
        @pl.when(tab_ref[flags_at + first + j] > 0)
        def _(j=j):
            lax.fori_loop(0, N_EXPERTS, functools.partial(more_chunks, j), 0)

    @pl.when(pl.program_id(0) == 0)
    def _():
        sh = sh_ref[...]
        gates = sg_ref[...]
        for e in range(N_EXPERTS):
            gate = gates[:, N_GROUPS + e:N_GROUPS + e + 1]
            hid = _silu_mul(jnp.dot(sh, w1_ref[e], preferred_element_type=F32)) * gate
            shid_ref[:, e * D_EXPERT:(e + 1) * D_EXPERT] = hid.astype(BF16)


def _moe_gate_up(tab, h2, mrow, w1_b, s_h, s_gates):
    n = h2.shape[0]
    n_s = s_h.shape[0]
    nblk = n // MOE_BLOCK
    g = MOE_GATE_UP_BLOCKS
    assert nblk % g == 0
    whole = lambda shape: pl.BlockSpec(shape, lambda i, tab: (0,) * len(shape))
    return pl.pallas_call(
        functools.partial(_moe_gate_up_kernel, flags_at=nblk * N_EXPERTS * 3),
        grid_spec=pltpu.PrefetchScalarGridSpec(
            num_scalar_prefetch=1,
            grid=(nblk // g,),
            in_specs=[pl.BlockSpec((g * MOE_BLOCK, D_MODEL), lambda i, tab: (i, 0)),
                      pl.BlockSpec((g, 8, MOE_BLOCK), lambda i, tab: (i, 0, 0)),
                      pl.BlockSpec(w1_b.shape, lambda i, tab: (0, 0, 0), pipeline_mode=pl.Buffered(1)),
                      whole(s_h.shape), whole(s_gates.shape)],
            out_specs=[pl.BlockSpec((g, MOE_ROWS, D_EXPERT), lambda i, tab: (i, 0, 0)),
                       whole((n_s, N_EXPERTS * D_EXPERT))],
            scratch_shapes=[pltpu.VMEM((g, MOE_ROWS, D_MODEL), BF16)]),
        out_shape=[jax.ShapeDtypeStruct((nblk, MOE_ROWS, D_EXPERT), BF16),
                   jax.ShapeDtypeStruct((n_s, N_EXPERTS * D_EXPERT), BF16)],
        compiler_params=pltpu.CompilerParams(
            dimension_semantics=("arbitrary",), vmem_limit_bytes=V7X_VMEM_LIMIT),
        name="moe_gate_up",
    )(tab, h2, mrow, w1_b, s_h, s_gates)


def _moe_down_kernel(tab_ref, hs_ref, xp_ref, mcol_ref, w2_ref, gf_ref, shid_ref, sx_ref, y_ref, sy_ref, os, *,
                     flags_at):
    t = MOE_BLOCK
    n_blk = MOE_DOWN_BLOCKS
    first = pl.program_id(0) * n_blk
    tab = lambda j, e, c: tab_ref[((first + j) * N_EXPERTS + e) * 3 + c]
    os[...] = jnp.zeros(os.shape, BF16)

    def first_chunks(g, carry):
        for i in range(MOE_EXPERTS_PER_STEP):
            e = g * MOE_EXPERTS_PER_STEP + i
            offs = _chunk_offsets(tab, n_blk, e)
            hid = jnp.concatenate([hs_ref[j, pl.ds(offs[j], MOE_CHUNK), :] for j in range(n_blk)], axis=0)
            out = jnp.dot(hid, w2_ref[e], preferred_element_type=F32).astype(BF16)
            for j in range(n_blk):
                os[j, pl.ds(offs[j], MOE_CHUNK), :] = out[j * MOE_CHUNK:(j + 1) * MOE_CHUNK, :]
        return carry

    lax.fori_loop(0, N_EXPERTS // MOE_EXPERTS_PER_STEP, first_chunks, 0)

    def more_chunks(j, e, carry):
        off, n_chunks, end = tab(j, e, 0), tab(j, e, 1), tab(j, e, 2)

        def chunk(c, carry):
            r0 = pl.multiple_of(off + c * MOE_CHUNK, MOE_ROW_ALIGN)
            rows = r0 + lax.broadcasted_iota(jnp.int32, (MOE_CHUNK, D_MODEL), 0)
            out = jnp.dot(hs_ref[j, pl.ds(r0, MOE_CHUNK), :], w2_ref[e], preferred_element_type=F32)
            os[j, pl.ds(r0, MOE_CHUNK), :] = jnp.where(rows < end, out.astype(BF16), os[j, pl.ds(r0, MOE_CHUNK), :])
            return carry

        return lax.fori_loop(1, n_chunks, chunk, carry)

    for j in range(n_blk):
        @pl.when(tab_ref[flags_at + first + j] > 0)
        def _(j=j):
            lax.fori_loop(0, N_EXPERTS, functools.partial(more_chunks, j), 0)

    l_idx = lax.broadcasted_iota(jnp.int32, (t, MOE_SORT_ROWS), 1)

    def scatter_matrix(j):
        mcol = mcol_ref[j * t:(j + 1) * t, :]
        d1c = mcol[:, 0:1].astype(jnp.int32)
        d2c = mcol[:, 1:2].astype(jnp.int32)
        comb = jnp.where(l_idx == d1c, mcol[:, 2:3], 0.0) + jnp.where(l_idx == d2c, mcol[:, 3:4], 0.0)
        return comb.astype(BF16)

    combs = [scatter_matrix(j) for j in range(n_blk)]
    moes = [_dot(combs[j], os[j, 0:MOE_SORT_ROWS, :]) for j in range(n_blk)]
    for j in range(n_blk):
        rows = slice(j * t, (j + 1) * t)
        y_ref[rows, :] = _rmsnorm(xp_ref[rows, :] + moes[j], gf_ref[...])

    @pl.when(pl.program_id(0) == 0)
    def _():
        w2_all = w2_ref[...].reshape(N_EXPERTS * D_EXPERT, D_MODEL)
        ys = sx_ref[...] + jnp.dot(shid_ref[...], w2_all, preferred_element_type=F32)
        sy_ref[...] = _rmsnorm(ys, gf_ref[...])


def _moe_down(tab, hs_sorted, xp, mcol, w2_b, gf, s_hid, s_x):
    n = xp.shape[0]
    n_s = s_x.shape[0]
    nblk = n // MOE_BLOCK
    g = MOE_DOWN_BLOCKS
    assert nblk % g == 0
    whole = lambda shape: pl.BlockSpec(shape, lambda i, tab: (0,) * len(shape))
    row_blk = lambda w: pl.BlockSpec((g * MOE_BLOCK, w), lambda i, tab: (i, 0))
    return pl.pallas_call(
        functools.partial(_moe_down_kernel, flags_at=nblk * N_EXPERTS * 3),
        grid_spec=pltpu.PrefetchScalarGridSpec(
            num_scalar_prefetch=1,
            grid=(nblk // g,),
            in_specs=[pl.BlockSpec((g, MOE_ROWS, D_EXPERT), lambda i, tab: (i, 0, 0)),
                      row_blk(D_MODEL), row_blk(ROUTER_LANES),
                      pl.BlockSpec(w2_b.shape, lambda i, tab: (0, 0, 0), pipeline_mode=pl.Buffered(1)),
                      whole((1, D_MODEL)), whole(s_hid.shape), whole(s_x.shape)],
            out_specs=[row_blk(D_MODEL), whole((n_s, D_MODEL))],
            scratch_shapes=[pltpu.VMEM((g, MOE_ROWS, D_MODEL), BF16)]),
        out_shape=[jax.ShapeDtypeStruct((n, D_MODEL), F32), jax.ShapeDtypeStruct((n_s, D_MODEL), F32)],
        compiler_params=pltpu.CompilerParams(
            dimension_semantics=("arbitrary",), vmem_limit_bytes=V7X_VMEM_LIMIT),
        name="moe_down_combine",
    )(tab, hs_sorted, xp, mcol, w2_b, gf, s_hid, s_x)


def _sample_proj_kernel(x_ref, g1_ref, w_ref, cos_ref, sin_ref, gv_ref, ones_ref, w00_ref, b0_ref,
                        rep_ref, foldt_ref, a_ref, k_ref, v_ref, vn_ref, qkvt_ref):
    h = _rmsnorm(x_ref[...], g1_ref[...])

    def proj(i):
        return _dot(h, w_ref[:, i * WIDTH:(i + 1) * WIDTH], precise=True)

    cos = _tile_lanes(cos_ref[...], WIDTH // 128)
    sin = _tile_lanes(sin_ref[...], WIDTH // 128)
    q = _rope(proj(2), cos, sin) * (HEAD_DIM ** -0.5)
    k = _rope(proj(3), cos, sin)
    v = proj(4)
    vn = _group_rmsnorm(proj(1), ones_ref[...], gv_ref[...], precise=True)
    a_ref[...] = proj(0) * (w00_ref[...] * vn + b0_ref[...])
    k_ref[...] = k
    v_ref[...] = v
    vn_ref[...] = vn

    n_rep = rep_ref.shape[0]
    r_idx = lax.broadcasted_iota(jnp.int32, (n_rep, WIDTH), 0)
    l_idx = lax.broadcasted_iota(jnp.int32, (n_rep, WIDTH), 1)
    own = (l_idx // HEAD_DIM) == (r_idx % N_HEADS)
    for t, src in enumerate((q, k, v)):
        rep = _dot(rep_ref[...], src, precise=True)
        qkvt_ref[t * HEAD_DIM:(t + 1) * HEAD_DIM, :] = lax.dot_general(
            foldt_ref[...], jnp.where(own, rep, 0.0), (((1,), (1,)), ((), ())),
            preferred_element_type=F32, precision=lax.Precision.HIGHEST)


def _sample_proj(x, g1, w_in, cos, sin, gv, ones_bd, w00, b0, rep, foldt):
    bd = x.shape[0]
    sds = lambda r, c: jax.ShapeDtypeStruct((r, c), F32)
    return pl.pallas_call(
        _sample_proj_kernel,
        out_shape=[sds(bd, WIDTH), sds(bd, WIDTH), sds(bd, WIDTH), sds(bd, WIDTH),
                   sds(3 * HEAD_DIM, bd * N_HEADS)],
        compiler_params=pltpu.CompilerParams(vmem_limit_bytes=V7X_VMEM_LIMIT),
        name="sample_proj",
    )(x, g1, w_in, cos, sin, gv, ones_bd, w00, b0, rep, foldt)


def _sample_scores(qkvt_ref, k_ref, b, win):
    n_col = qkvt_ref.shape[1]
    c_idx = lax.broadcasted_iota(jnp.int32, (n_col, 128), 0)
    l_idx = lax.broadcasted_iota(jnp.int32, (n_col, 128), 1)
    pick = jnp.where((c_idx == b * N_HEADS + l_idx) & (l_idx < N_HEADS), 1.0, 0.0)
    cols = _dot(qkvt_ref[...], pick, precise=True)

    head = lax.broadcasted_iota(jnp.int32, (N_HEADS, win), 0)
    s = jnp.zeros((N_HEADS, win), F32)
    s_self = jnp.zeros((N_HEADS, 1), F32)
    for h in range(N_HEADS):
        qc = cols[0:HEAD_DIM, h:h + 1]
        kc = cols[HEAD_DIM:2 * HEAD_DIM, h:h + 1]
        s_h = jnp.sum(k_ref[0, h] * qc, axis=0, keepdims=True)
        s = jnp.where(head == h, s_h, s)
        s_self = jnp.where(head[:, 0:1] == h, jnp.sum(qc * kc, axis=0, keepdims=True), s_self)
    return s, s_self, cols[2 * HEAD_DIM:3 * HEAD_DIM, :]


def _sample_finish(s, s_self, v_cols, v_ref, o_ref, win):
    dist = win - lax.broadcasted_iota(jnp.int32, (1, win), 1)
    members = [(dist <= N_KEYS * dil) & (dist % dil == 0) for dil in DILATIONS]
    es, e_selfs, dens, lses = [], [], [], []
    for mem in members:
        sm = jnp.where(mem, s, NEG)
        m = jnp.maximum(jnp.max(sm, axis=1, keepdims=True), s_self)
        e = jnp.exp(sm - m)
        e_self = jnp.exp(s_self - m)
        den = jnp.sum(e, axis=1, keepdims=True) + e_self
        es.append(e)
        e_selfs.append(e_self)
        dens.append(den)
        lses.append(m + jnp.log(den))
    top = jnp.maximum(jnp.maximum(lses[0], lses[1]), lses[2])
    ws = [jnp.exp(l - top) for l in lses]
    wsum = ws[0] + ws[1] + ws[2]
    coef = [w / (den * wsum) for w, den in zip(ws, dens)]
    p_keys = coef[0] * es[0] + coef[1] * es[1] + coef[2] * es[2]
    p_self = coef[0] * e_selfs[0] + coef[1] * e_selfs[1] + coef[2] * e_selfs[2]

    for h in range(N_HEADS):
        o_ref[0, :, h:h + 1] = (jnp.sum(v_ref[0, h] * p_keys[h:h + 1, :], axis=1, keepdims=True)
                                + p_self[h:h + 1, :] * v_cols[:, h:h + 1])


def _rope_tables(first_pos, count):
    half = HEAD_DIM // 2
    inv = ROPE_THETA ** (-np.arange(half, dtype=np.float64) * 2.0 / HEAD_DIM)
    ang = (first_pos + np.arange(count, dtype=np.float64))[:, None] * inv[None, :]
    cos, sin = np.cos(ang), np.sin(ang)
    cos128 = np.concatenate([cos, cos, cos, cos], axis=1).astype(np.float32)
    sin128 = np.concatenate([-sin, sin, -sin, sin], axis=1).astype(np.float32)
    return jnp.asarray(cos128), jnp.asarray(sin128)


def kernel(x_prompt, x_sample, cache_win_k, cache_win_v, ln1_g, w_in, sgu_norm_g, sgu_w, sgu_b, w_out, ln2_g,
           w_router_group, b_router_group, w_router_expert, b_router_expert, w_gate, w_up, w_down, lnf_g):
    depth = w_in.shape[0]
    assert depth == 1 and x_sample.shape[1] == 1
    B, S, _ = x_prompt.shape
    bd = x_sample.shape[0]
    win = cache_win_k.shape[2]
    assert S % (max(DILATIONS) * CHUNK) == 0 and win >= max(DILATIONS) * N_KEYS and PAST_LEN % CHUNK == 0
    l = 0

    pad = ROUTER_LANES - N_GROUPS - N_EXPERTS
    w_router = jnp.pad(jnp.concatenate([w_router_group[l], w_router_expert[l]], axis=1), ((0, 0), (0, pad)))
    b_router = jnp.pad(jnp.concatenate([b_router_group[l], b_router_expert[l]]), (0, pad))[None, :]
    g1 = ln1_g[l][None, :]
    g2 = ln2_g[l][None, :]
    gf = lnf_g[None, :]
    gv = sgu_norm_g[l].reshape(1, WIDTH)
    grp = np.arange(WIDTH) // HEAD_DIM
    ones_bd = jnp.asarray(np.where(grp[:, None] == grp[None, :], 1.0 / HEAD_DIM, 0.0), BF16)
    wp = jnp.concatenate([sgu_w[l][0::2], sgu_w[l][1::2]], axis=-1)
    bias = jnp.repeat(sgu_b[l].T, HEAD_DIM, axis=1)
    w00 = jnp.repeat(sgu_w[l][:, 0, 0], HEAD_DIM)[None, :]
    b0 = jnp.repeat(sgu_b[l][:, 0], HEAD_DIM)[None, :]

    cos_s, sin_s = _rope_tables(PAST_LEN, 1)
    rep = jnp.asarray(np.arange(bd * N_HEADS)[:, None] // N_HEADS == np.arange(bd)[None, :], F32)
    foldt = jnp.asarray(np.arange(HEAD_DIM)[:, None] == np.arange(WIDTH)[None, :] % HEAD_DIM, F32)
    xs = x_sample.reshape(bd, D_MODEL)
    a_s, k_s, v_s, vn_s, qkvt = _sample_proj(xs, g1, w_in[l], cos_s, sin_s, gv, ones_bd, w00, b0, rep, foldt)
    to_pos_minor = lambda c: jnp.transpose(c, (0, 2, 3, 1))

    cos_p, sin_p = _rope_tables(0, S)
    a_p, q_p, k_p, v_p, kt_p, vt_p, o3 = _prompt_proj(
        x_prompt, g1, w_in[l], cos_p, sin_p, gv, ones_bd, wp, bias,
        qkvt, to_pos_minor(cache_win_k[l]), to_pos_minor(cache_win_v[l]))
    b_p, w1_b, w2_b = _prompt_attention(q_p, k_p, v_p, w_gate[l], w_up[l], w_down[l])
    n = B * S
    assert n % MOE_BLOCK == 0
    wr_t = w_router.T[:ROUTER_ROWS]
    wr_hi = wr_t.astype(BF16)
    wr_hl = jnp.concatenate([wr_hi, (wr_t - wr_hi.astype(F32)).astype(BF16)], axis=0)
    lane_idx = np.arange(ROUTER_LANES)
    tri = jnp.asarray(lane_idx[:, None] <= lane_idx[None, :], BF16)
    ltri = jnp.asarray(lane_idx[None, :] < lane_idx[:, None], BF16)
    xp2, h2, mrow, mcol, tab_f = _mix_route_sort(
        a_p.reshape(n, WIDTH), b_p.reshape(n, WIDTH), x_prompt.reshape(n, D_MODEL),
        w_out[l], g2, wr_hl, b_router.reshape(ROUTER_LANES, 1)[:ROUTER_ROWS], tri, ltri)
    tab_i = tab_f[:, :N_EXPERTS, 0:3].astype(jnp.int32)
    multi_chunk = (jnp.max(tab_i[:, :, 1], axis=1) > 1).astype(jnp.int32)
    tab = jnp.concatenate([tab_i.reshape(-1), multi_chunk])

    b_s = jnp.transpose(o3, (0, 2, 1)).reshape(bd, WIDTH)
    xs2, hs2, gates_s = _mix_router(a_s, b_s, xs, w_out[l], g2, w_router, b_router, tm=bd, precise=True)

    hid_sorted, hid_s = _moe_gate_up(tab, h2, mrow, w1_b, hs2, gates_s)
    y_prompt, y_sample = _moe_down(tab, hid_sorted, xp2, mcol, w2_b, gf, hid_s, xs2)
    y_prompt = y_prompt.reshape(B, S, D_MODEL)
    y_sample = y_sample.reshape(bd, 1, D_MODEL)
    buf_p = min(MAX_WINDOW, S)
    to_win = lambda t: jnp.transpose(t.reshape(1, B, N_HEADS, HEAD_DIM, buf_p), (0, 1, 4, 2, 3))
    new_k_p = to_win(kt_p)
    new_v_p = to_win(vt_p)

    shape_s = (1, bd, 1, N_HEADS, HEAD_DIM)
    return (y_prompt, y_sample, new_k_p, new_v_p,
            k_s.reshape(shape_s), v_s.reshape(shape_s), vn_s.reshape(shape_s))
```

```python
import functools

import jax
import jax.numpy as jnp
import numpy as np
from jax import lax
from jax.experimental import pallas as pl
from jax.experimental.pallas import tpu as pltpu

F32 = jnp.float32
BF16 = jnp.bfloat16

D_MODEL = 1024
HEAD_DIM = 64
N_HEADS = 8
WIDTH = N_HEADS * HEAD_DIM
PROJ_COLS = 5 * WIDTH
CHUNK = 128
DILATIONS = (1, 4, 16)
N_KEYS = 128
MAX_WINDOW = 2048
PAST_LEN = 16384
ROPE_THETA = 10000.0
N_GROUPS = 4
EXPERTS_PER_GROUP = 8
N_EXPERTS = N_GROUPS * EXPERTS_PER_GROUP
D_EXPERT = 128
EPS = 1e-6
NEG = -1e30
TILES_PER_STEP = 32
SUBLANE_STRIDE = 4
assert DILATIONS == (1, SUBLANE_STRIDE, SUBLANE_STRIDE ** 2)
MOE_BLOCK = 512
MOE_ROW_ALIGN = 16
MOE_CHUNK = 48
MOE_EXPERTS_PER_STEP = 16
MOE_ROUTE_BLOCKS = 2
MOE_GATE_UP_BLOCKS = 4
MOE_DOWN_BLOCKS = 2
MOE_SORT_ROWS = -(-(2 * MOE_BLOCK + N_EXPERTS * (MOE_ROW_ALIGN - 1)) // 512) * 512
MOE_ROWS = MOE_SORT_ROWS + MOE_CHUNK
ROUTER_ROWS = 48
ROUTER_LANES = 128
V7X_VMEM_LIMIT = 56 * 1024 * 1024


def _rmsnorm(x, g):
    return x * lax.rsqrt(jnp.mean(x * x, axis=-1, keepdims=True) + EPS) * g


def _tile_lanes(t, reps):
    return jnp.concatenate([t] * reps, axis=1)


def _rope(t, cos, sin_signed):
    lane = lax.broadcasted_iota(jnp.int32, t.shape, 1)
    first_half = (lane % HEAD_DIM) < (HEAD_DIM // 2)
    n = t.shape[1]
    partner = jnp.where(first_half, pltpu.roll(t, n - HEAD_DIM // 2, 1), pltpu.roll(t, HEAD_DIM // 2, 1))
    return t * cos + partner * sin_signed


def _dot(a, b, precise=False):
    if precise:
        return jnp.dot(a.astype(F32), b.astype(F32), preferred_element_type=F32,
                       precision=lax.Precision.HIGHEST)
    return jnp.dot(a.astype(BF16), b.astype(BF16), preferred_element_type=F32)


def _group_rmsnorm(va, ones_bd, gv, precise=False):
    ms = _dot(va * va, ones_bd, precise)
    return va * lax.rsqrt(ms + EPS) * gv


def _proj_kernel(x_ref, g1_ref, w_ref, cos_ref, sin_ref, gv_ref, ones_ref, wp_ref, bias_ref,
                 qkvt_ref, ck_ref, cv_ref,
                 a_ref, q_ref, k_ref, v_ref, kt_ref, vt_ref, so_ref, wb, *, tm, first_win_tile, n_seq, win):
    @pl.when((pl.program_id(0) == 0) & (pl.program_id(1) == 0))
    def _():
        wb[...] = w_ref[...].astype(BF16)

    h = _rmsnorm(x_ref[0], g1_ref[...]).astype(BF16)

    def proj(i):
        return jnp.dot(h, wb[:, i * WIDTH:(i + 1) * WIDTH], preferred_element_type=F32)

    cos = _tile_lanes(cos_ref[...], WIDTH // 128)
    sin = _tile_lanes(sin_ref[...], WIDTH // 128)
    q_ref[0] = _rope(proj(2), cos, sin) * (HEAD_DIM ** -0.5)
    k_ref[0] = _rope(proj(3), cos, sin)
    v_ref[0] = proj(4)

    s_seq = jnp.minimum(pl.program_id(0) * pl.num_programs(1) + pl.program_id(1), n_seq - 1)
    s_scores, s_self, s_vcols = _sample_scores(qkvt_ref, ck_ref, s_seq, win)

    u = proj(0)
    vn = _group_rmsnorm(proj(1), ones_ref[...], gv_ref[...]).astype(BF16)

    lane = lax.broadcasted_iota(jnp.int32, (CHUNK, 128), 1)
    left = lane < HEAD_DIM
    row = lax.broadcasted_iota(jnp.int32, (CHUNK, 2 * CHUNK), 0)
    col = lax.broadcasted_iota(jnp.int32, (CHUNK, 2 * CHUNK), 1)
    causal = (col % CHUNK) <= row
    zero = jnp.zeros((CHUNK, 128), BF16)
    wps = [jnp.where(causal, wp_ref[gp], 0.0).astype(BF16) for gp in range(N_HEADS // 2)]
    def block_diag(vv):
        return jnp.concatenate([jnp.where(left, vv, zero), jnp.where(left, zero, vv)], axis=0)

    for c in range(0, tm // CHUNK, 2):
        rows = [slice((c + i) * CHUNK, (c + i + 1) * CHUNK) for i in range(2)]
        mixes = [[], []]
        for gp in range(N_HEADS // 2):
            lanes = slice(gp * 128, (gp + 1) * 128)
            v2 = jnp.concatenate([block_diag(vn[rows[0], lanes]), block_diag(vn[rows[1], lanes])], axis=1)
            both = jnp.dot(wps[gp], v2, preferred_element_type=F32)
            mixes[0].append(both[:, 0:128])
            mixes[1].append(both[:, 128:256])
        for i in range(2):
            mix = jnp.concatenate(mixes[i], axis=1) + bias_ref[...]
            a_ref[0, rows[i], :] = (u[rows[i], :] * mix).astype(a_ref.dtype)

    _sample_finish(s_scores, s_self, s_vcols, cv_ref, so_ref, win)

    @pl.when(pl.program_id(1) >= first_win_tile)
    def _():
        kt_ref[0] = k_ref[0].T
        vt_ref[0] = v_ref[0].T


def _prompt_proj(x, g1, w_in_b, cos, sin, gv, ones_bd, wp, bias, qkvt, cache_k_t, cache_v_t, *, tm=512):
    B, S, _ = x.shape
    const2 = lambda b, j: (0, 0)
    out_sds = lambda dt: jax.ShapeDtypeStruct((B, S, WIDTH), dt)
    tile = pl.BlockSpec((1, tm, WIDTH), lambda b, j: (b, j, 0))
    win = min(MAX_WINDOW, S)
    first_win_tile = (S - win) // tm
    tile_t = pl.BlockSpec((1, WIDTH, tm), lambda b, j: (b, 0, jnp.maximum(j - first_win_tile, 0)))
    win_sds = jax.ShapeDtypeStruct((B, WIDTH, win), F32)
    n_seq, _, _, cache_win = cache_k_t.shape
    n_tiles = S // tm
    assert n_seq <= B * n_tiles, "one sample sequence rides on each grid step"
    seq_blk = lambda shape: pl.BlockSpec(
        (1,) + shape, lambda b, j: (jnp.minimum(b * n_tiles + j, n_seq - 1),) + (0,) * len(shape))
    cache_blk = seq_blk((N_HEADS, HEAD_DIM, cache_win))
    return pl.pallas_call(
        functools.partial(_proj_kernel, tm=tm, first_win_tile=first_win_tile, n_seq=n_seq, win=cache_win),
        grid=(B, n_tiles),
        in_specs=[
            pl.BlockSpec((1, tm, D_MODEL), lambda b, j: (b, j, 0)),
            pl.BlockSpec((1, D_MODEL), const2),
            pl.BlockSpec((D_MODEL, PROJ_COLS), const2, pipeline_mode=pl.Buffered(1)),
            pl.BlockSpec((tm, 128), lambda b, j: (j, 0)),
            pl.BlockSpec((tm, 128), lambda b, j: (j, 0)),
            pl.BlockSpec((1, WIDTH), const2),
            pl.BlockSpec((WIDTH, WIDTH), const2),
            pl.BlockSpec((N_HEADS // 2, CHUNK, 2 * CHUNK), lambda b, j: (0, 0, 0)),
            pl.BlockSpec((CHUNK, WIDTH), const2),
            pl.BlockSpec(qkvt.shape, const2), cache_blk, cache_blk,
        ],
        out_specs=[tile, tile, tile, tile, tile_t, tile_t, seq_blk((HEAD_DIM, N_HEADS))],
        out_shape=[out_sds(BF16), out_sds(F32), out_sds(F32), out_sds(F32), win_sds, win_sds,
                   jax.ShapeDtypeStruct((n_seq, HEAD_DIM, N_HEADS), F32)],
        scratch_shapes=[pltpu.VMEM((D_MODEL, PROJ_COLS), BF16)],
        compiler_params=pltpu.CompilerParams(
            dimension_semantics=("arbitrary", "arbitrary"), vmem_limit_bytes=V7X_VMEM_LIMIT),
        name="prompt_proj_sgu",
    )(x, g1, w_in_b, cos, sin, gv, ones_bd, wp, bias, qkvt, cache_k_t, cache_v_t)


def _attn_kernel(q_ref, k_ref, v_ref, wg_ref, wu_ref, wd_ref, o_ref, w1_ref, w2_ref,
                 qd, kd, vd, res_o, res_l, nat_o, nat_l, bias, *, seq):
    n_tiles = seq // CHUNK
    last = len(DILATIONS) - 1
    stage_in = (res_o.at[last], res_l.at[last], nat_o.at[last - 1])
    stage_out = (res_o.at[last - 1], res_l.at[last - 1])

    w1_ref[:, :, 0:D_EXPERT] = wg_ref[...].astype(BF16)
    w1_ref[:, :, D_EXPERT:2 * D_EXPERT] = wu_ref[...].astype(BF16)
    w2_ref[...] = wd_ref[...].astype(BF16)

    lane = lax.broadcasted_iota(jnp.int32, (CHUNK, 128), 1)
    left = lane < HEAD_DIM
    qi2 = lax.broadcasted_iota(jnp.int32, (2 * CHUNK, 2 * CHUNK), 0) % CHUNK
    kj2 = lax.broadcasted_iota(jnp.int32, (2 * CHUNK, 2 * CHUNK), 1)
    dist2 = CHUNK + qi2 - kj2
    band2 = (dist2 >= 0) & (dist2 <= N_KEYS)
    zero_q = jnp.zeros((CHUNK, 128), BF16)
    bias[0] = jnp.where(band2, 0.0, NEG)
    bias[1] = jnp.where(band2 & (kj2 >= CHUNK), 0.0, NEG)

    kd[0:CHUNK, :] = jnp.zeros((CHUNK, 128), BF16)
    vd[0:CHUNK, :] = jnp.zeros((CHUNK, 128), BF16)

    for p, dil in enumerate(DILATIONS):
        sub = seq // dil
        nb = sub // CHUNK
        for ti, (src_ref, dst, pad) in enumerate(((q_ref, qd, 0), (k_ref, kd, CHUNK), (v_ref, vd, CHUNK))):
            if dil == 1:
                dst[pad:pad + seq, :] = src_ref[0].astype(BF16)
            elif dil == SUBLANE_STRIDE:
                for r in range(dil):
                    val = src_ref[0, pl.ds(r, sub, stride=dil), :]
                    stage_in[ti][r * sub:(r + 1) * sub, :] = val
                    dst[pad + r * sub:pad + (r + 1) * sub, :] = val.astype(BF16)
            else:
                coarse = seq // SUBLANE_STRIDE
                for r_lo in range(SUBLANE_STRIDE):
                    for r_hi in range(SUBLANE_STRIDE):
                        r = r_lo + SUBLANE_STRIDE * r_hi
                        val = stage_in[ti][pl.ds(r_lo * coarse + r_hi, sub, stride=SUBLANE_STRIDE), :]
                        dst[pad + r * sub:pad + (r + 1) * sub, :] = val.astype(BF16)

        def tile_body(g, i, p=p, nb=nb):
            t = g * TILES_PER_STEP + i
            row = pl.multiple_of(t * CHUNK, CHUNK)
            qt = qd[pl.ds(row, CHUNK), :]
            k2 = kd[pl.ds(row, 2 * CHUNK), :]
            v2 = vd[pl.ds(row, 2 * CHUNK), :]
            if TILES_PER_STEP % nb == 0:
                variant = 1 if i % nb == 0 else 0
            elif i == 0:
                variant = jnp.where((g * TILES_PER_STEP) % nb == 0, 1, 0)
            else:
                variant = 0
            q2 = jnp.concatenate([jnp.where(left, qt, zero_q), jnp.where(left, zero_q, qt)], axis=0)
            s = lax.dot_general(q2, k2, (((1,), (1,)), ((), ())), preferred_element_type=F32)
            s = s + bias[variant]
            m = jnp.max(s, axis=1, keepdims=True)
            e = jnp.exp(s - m)
            den = jnp.sum(e, axis=1, keepdims=True)
            pv = jnp.dot(e.astype(BF16), v2, preferred_element_type=F32) / den
            lse = jnp.broadcast_to(m + jnp.log(den), (2 * CHUNK, 128))
            res_o[p, pl.ds(row, CHUNK), :] = jnp.where(left, pv[0:CHUNK], pv[CHUNK:2 * CHUNK])
            res_l[p, pl.ds(row, CHUNK), :] = jnp.where(left, lse[0:CHUNK], lse[CHUNK:2 * CHUNK])

        def group_body(g, carry, tile_body=tile_body):
            for i in range(TILES_PER_STEP):
                tile_body(g, i)
            return carry

        lax.fori_loop(0, n_tiles // TILES_PER_STEP, group_body, 0)

    for p, dil in enumerate(DILATIONS):
        if dil == 1:
            continue
        sub = seq // dil
        for si, (res, nat) in enumerate(((res_o, nat_o), (res_l, nat_l))):
            if dil == SUBLANE_STRIDE:
                for r in range(dil):
                    nat[p - 1, pl.ds(r, sub, stride=dil), :] = res[p, r * sub:(r + 1) * sub, :]
            else:
                coarse = seq // SUBLANE_STRIDE
                for r_lo in range(SUBLANE_STRIDE):
                    for r_hi in range(SUBLANE_STRIDE):
                        r = r_lo + SUBLANE_STRIDE * r_hi
                        stage_out[si][pl.ds(r_lo * coarse + r_hi, sub, stride=SUBLANE_STRIDE), :] = (
                            res[p, r * sub:(r + 1) * sub, :])
                for r_lo in range(SUBLANE_STRIDE):
                    nat[p - 1, pl.ds(r_lo, coarse, stride=SUBLANE_STRIDE), :] = (
                        stage_out[si][r_lo * coarse:(r_lo + 1) * coarse, :])

    rows_per_step = 256

    def merge_body(c, carry):
        rows = pl.ds(pl.multiple_of(c * rows_per_step, rows_per_step), rows_per_step)
        l0, l1, l2 = res_l[0, rows, :], nat_l[0, rows, :], nat_l[1, rows, :]
        top = jnp.maximum(jnp.maximum(l0, l1), l2)
        w0, w1, w2 = jnp.exp(l0 - top), jnp.exp(l1 - top), jnp.exp(l2 - top)
        num = w0 * res_o[0, rows, :] + w1 * nat_o[0, rows, :] + w2 * nat_o[1, rows, :]
        o_ref[0, rows, :] = (num / (w0 + w1 + w2)).astype(o_ref.dtype)
        return carry

    lax.fori_loop(0, seq // rows_per_step, merge_body, 0)


def _prompt_attention(q, k, v, w_gate, w_up, w_down):
    B, S, _ = q.shape
    n_pairs = WIDTH // 128
    blk = pl.BlockSpec((1, S, 128), lambda b, hp: (b, 0, hp))
    n_steps = B * n_pairs
    assert N_EXPERTS % n_steps == 0
    e_blk = N_EXPERTS // n_steps
    expert_blk = lambda rows, cols: pl.BlockSpec((e_blk, rows, cols), lambda b, hp: (b * n_pairs + hp, 0, 0))
    return pl.pallas_call(
        functools.partial(_attn_kernel, seq=S),
        grid=(B, n_pairs),
        in_specs=[blk, blk, blk,
                  expert_blk(D_MODEL, D_EXPERT), expert_blk(D_MODEL, D_EXPERT), expert_blk(D_EXPERT, D_MODEL)],
        out_specs=[blk, expert_blk(D_MODEL, 2 * D_EXPERT), expert_blk(D_EXPERT, D_MODEL)],
        out_shape=[jax.ShapeDtypeStruct((B, S, WIDTH), BF16),
                   jax.ShapeDtypeStruct((N_EXPERTS, D_MODEL, 2 * D_EXPERT), BF16),
                   jax.ShapeDtypeStruct((N_EXPERTS, D_EXPERT, D_MODEL), BF16)],
        scratch_shapes=[
            pltpu.VMEM((S, 128), BF16),
            pltpu.VMEM((S + CHUNK, 128), BF16),
            pltpu.VMEM((S + CHUNK, 128), BF16),
            pltpu.VMEM((len(DILATIONS), S, 128), F32),
            pltpu.VMEM((len(DILATIONS), S, 128), F32),
            pltpu.VMEM((len(DILATIONS) - 1, S, 128), F32),
            pltpu.VMEM((len(DILATIONS) - 1, S, 128), F32),
            pltpu.VMEM((2, 2 * CHUNK, 2 * CHUNK), F32),
        ],
        compiler_params=pltpu.CompilerParams(
            dimension_semantics=("arbitrary", "arbitrary"), vmem_limit_bytes=V7X_VMEM_LIMIT),
        name="prompt_dilated_attention",
    )(q, k, v, w_gate, w_up, w_down)


def _route(logits):
    lane = lax.broadcasted_iota(jnp.int32, logits.shape, 1)
    big = jnp.int32(ROUTER_LANES)
    lg = jnp.where(lane < N_GROUPS, logits, NEG)
    gmax = jnp.max(lg, axis=1, keepdims=True)
    gp = 1.0 / jnp.sum(jnp.exp(lg - gmax), axis=1, keepdims=True)
    gi = jnp.min(jnp.where(lg == gmax, lane, big), axis=1, keepdims=True)
    lo = N_GROUPS + EXPERTS_PER_GROUP * gi
    le = jnp.where((lane >= lo) & (lane < lo + EXPERTS_PER_GROUP), logits, NEG)
    m1 = jnp.max(le, axis=1, keepdims=True)
    i1 = jnp.min(jnp.where(le == m1, lane, big), axis=1, keepdims=True)
    le2 = jnp.where(lane == i1, NEG, le)
    m2 = jnp.max(le2, axis=1, keepdims=True)
    i2 = jnp.min(jnp.where(le2 == m2, lane, big), axis=1, keepdims=True)
    e2 = jnp.exp(m2 - m1)
    w1 = 1.0 / (1.0 + e2)
    w2 = e2 / (1.0 + e2)
    return jnp.where(lane == i1, gp * w1, jnp.where(lane == i2, gp * w2, 0.0))


def _mix_router_kernel(a_ref, b_ref, x_ref, wo_ref, g2_ref, wr_ref, br_ref, xp_ref, h2_ref, gates_ref, *, precise):
    mixed = (_dot(a_ref[...], wo_ref[0:WIDTH, :], precise)
             + _dot(b_ref[...], wo_ref[WIDTH:2 * WIDTH, :], precise))
    xp = x_ref[...] + mixed
    xp_ref[...] = xp
    h2 = _rmsnorm(xp, g2_ref[...])
    h2_ref[...] = h2.astype(h2_ref.dtype)
    logits = jnp.dot(h2, wr_ref[...], preferred_element_type=F32,
                     precision=lax.Precision.HIGHEST) + br_ref[...]
    gates_ref[...] = _route(logits)


def _mix_router(a, b, x, w_out, g2, w_router, b_router, *, tm, precise=False):
    n = x.shape[0]
    const = lambda i: (0, 0)
    row_blk = lambda w: pl.BlockSpec((tm, w), lambda i: (i, 0))
    return pl.pallas_call(
        functools.partial(_mix_router_kernel, precise=precise),
        grid=(n // tm,),
        in_specs=[row_blk(WIDTH), row_blk(WIDTH), row_blk(D_MODEL),
                  pl.BlockSpec((2 * WIDTH, D_MODEL), const),
                  pl.BlockSpec((1, D_MODEL), const),
                  pl.BlockSpec((D_MODEL, ROUTER_LANES), const),
                  pl.BlockSpec((1, ROUTER_LANES), const)],
        out_specs=[row_blk(D_MODEL), row_blk(D_MODEL), row_blk(ROUTER_LANES)],
        out_shape=[jax.ShapeDtypeStruct((n, D_MODEL), F32),
                   jax.ShapeDtypeStruct((n, D_MODEL), BF16),
                   jax.ShapeDtypeStruct((n, ROUTER_LANES), F32)],
        compiler_params=pltpu.CompilerParams(
            dimension_semantics=("arbitrary",), vmem_limit_bytes=V7X_VMEM_LIMIT),
        name="outproj_router",
    )(a, b, x, w_out, g2, w_router, b_router)


def _nt_dot(w, t):
    return lax.dot_general(w, t, (((1,), (1,)), ((), ())), preferred_element_type=F32)


def _route_t(logits_t):
    row = lax.broadcasted_iota(jnp.int32, logits_t.shape, 0)
    big = jnp.int32(ROUTER_LANES)
    lg = jnp.where(row < N_GROUPS, logits_t, NEG)
    gmax = jnp.max(lg, axis=0, keepdims=True)
    gp = 1.0 / jnp.sum(jnp.exp(lg - gmax), axis=0, keepdims=True)
    gi = jnp.min(jnp.where(lg == gmax, row, big), axis=0, keepdims=True)
    lo = N_GROUPS + EXPERTS_PER_GROUP * gi
    le = jnp.where((row >= lo) & (row < lo + EXPERTS_PER_GROUP), logits_t, NEG)
    m1 = jnp.max(le, axis=0, keepdims=True)
    i1 = jnp.min(jnp.where(le == m1, row, big), axis=0, keepdims=True)
    le2 = jnp.where(row == i1, NEG, le)
    m2 = jnp.max(le2, axis=0, keepdims=True)
    i2 = jnp.min(jnp.where(le2 == m2, row, big), axis=0, keepdims=True)
    e2 = jnp.exp(m2 - m1)
    return i1 - N_GROUPS, i2 - N_GROUPS, gp / (1.0 + e2), gp * e2 / (1.0 + e2)


def _mix_route_sort_kernel(a_ref, b_ref, x_ref, wo_ref, g2_ref, wr_ref, brc_ref, tri_ref, ltri_ref,
                           xp_ref, h2_ref, mrow_ref, mcol_ref, tab_ref, wob):
    t = MOE_BLOCK

    @pl.when(pl.program_id(0) == 0)
    def _():
        wob[...] = wo_ref[...].astype(BF16)

    def project(j):
        rows = slice(j * t, (j + 1) * t)
        xp = x_ref[rows, :] + _dot(a_ref[rows, :], wob[0:WIDTH, :]) + _dot(b_ref[rows, :], wob[WIDTH:2 * WIDTH, :])
        xp_ref[rows, :] = xp
        h2 = _rmsnorm(xp, g2_ref[...])
        hi = h2.astype(BF16)
        h2_ref[rows, :] = hi
        return hi, (h2 - hi.astype(F32)).astype(BF16)

    def route(hi, lo):
        prod_hi = _nt_dot(wr_ref[...], hi)
        logits_t = (prod_hi[0:ROUTER_ROWS] + prod_hi[ROUTER_ROWS:2 * ROUTER_ROWS]
                    + _nt_dot(wr_ref[0:ROUTER_ROWS, :], lo) + brc_ref[...])
        return _route_t(logits_t)

    def sort_meta(j, ex1, ex2, gate1, gate2):
        rows = slice(j * t, (j + 1) * t)
        pair_e = jnp.concatenate([ex1, ex2], axis=1)
        row = lax.broadcasted_iota(jnp.int32, (N_EXPERTS, 2 * t), 0)
        onehot = jnp.where(row == pair_e, 1.0, 0.0)
        n_lane_tiles = 2 * t // 128
        local = _dot(jnp.concatenate([onehot[:, k * 128:(k + 1) * 128] for k in range(n_lane_tiles)], axis=0),
                     tri_ref[...])
        carry = jnp.zeros((N_EXPERTS, 1), F32)
        cums = []
        for k in range(n_lane_tiles):
            tile = local[k * N_EXPERTS:(k + 1) * N_EXPERTS, :]
            cums.append(tile + carry)
            carry = carry + tile[:, 127:128]
        cum = jnp.concatenate(cums, axis=1)
        rank = jnp.sum(onehot * cum, axis=0, keepdims=True) - 1.0
        counts = carry
        units32 = jnp.floor((counts + (MOE_ROW_ALIGN - 1)) * (1.0 / MOE_ROW_ALIGN))
        units = jnp.concatenate([jnp.broadcast_to(units32, (N_EXPERTS, 128)),
                                 jnp.zeros((ROUTER_LANES - N_EXPERTS, 128), F32)], axis=0)
        off = _dot(ltri_ref[...], units) * MOE_ROW_ALIGN
        dst = jnp.sum(onehot * off[0:N_EXPERTS, 0:1], axis=0, keepdims=True) + rank

        r8 = lax.broadcasted_iota(jnp.int32, (8, t), 0)
        mrow_ref[j] = jnp.where(r8 == 0, dst[:, 0:t], jnp.where(r8 == 1, dst[:, t:2 * t],
                                jnp.where(r8 == 2, gate1, jnp.where(r8 == 3, gate2, 0.0))))
        r128 = lax.broadcasted_iota(jnp.int32, (ROUTER_LANES, t), 0)
        meta = jnp.where(r128 == 0, dst[:, 0:t], jnp.where(r128 == 1, dst[:, t:2 * t],
                         jnp.where(r128 == 2, gate1, jnp.where(r128 == 3, gate2, 0.0))))
        mcol_ref[rows, :] = meta.T
        lane = lax.broadcasted_iota(jnp.int32, (ROUTER_LANES, 128), 1)
        n_rows = units * MOE_ROW_ALIGN
        chunks = jnp.floor((n_rows + (MOE_CHUNK - 1)) * (1.0 / MOE_CHUNK))
        tab_ref[j] = jnp.where(lane == 0, off, jnp.where(lane == 1, chunks, jnp.where(lane == 2, off + n_rows, 0.0)))

    blocks = range(MOE_ROUTE_BLOCKS)
    projected = [project(j) for j in blocks]
    routed = [route(hi, lo) for hi, lo in projected]
    for j in blocks:
        sort_meta(j, *routed[j])


def _mix_route_sort(a, b, x, w_out, g2, wr_hl, br_col, tri, ltri):
    n = x.shape[0]
    t = MOE_BLOCK
    nblk = n // t
    g = MOE_ROUTE_BLOCKS
    assert nblk % g == 0
    const = lambda i: (0, 0)
    row_blk = lambda w: pl.BlockSpec((g * t, w), lambda i: (i, 0))
    return pl.pallas_call(
        _mix_route_sort_kernel,
        grid=(nblk // g,),
        in_specs=[row_blk(WIDTH), row_blk(WIDTH), row_blk(D_MODEL),
                  pl.BlockSpec((2 * WIDTH, D_MODEL), const, pipeline_mode=pl.Buffered(1)),
                  pl.BlockSpec((1, D_MODEL), const),
                  pl.BlockSpec((2 * ROUTER_ROWS, D_MODEL), const),
                  pl.BlockSpec((ROUTER_ROWS, 1), const),
                  pl.BlockSpec((ROUTER_LANES, ROUTER_LANES), const),
                  pl.BlockSpec((ROUTER_LANES, ROUTER_LANES), const)],
        out_specs=[row_blk(D_MODEL), row_blk(D_MODEL),
                   pl.BlockSpec((g, 8, t), lambda i: (i, 0, 0)),
                   row_blk(ROUTER_LANES),
                   pl.BlockSpec((g, ROUTER_LANES, 128), lambda i: (i, 0, 0))],
        out_shape=[jax.ShapeDtypeStruct((n, D_MODEL), F32),
                   jax.ShapeDtypeStruct((n, D_MODEL), BF16),
                   jax.ShapeDtypeStruct((nblk, 8, t), F32),
                   jax.ShapeDtypeStruct((n, ROUTER_LANES), F32),
                   jax.ShapeDtypeStruct((nblk, ROUTER_LANES, 128), F32)],
        scratch_shapes=[pltpu.VMEM((2 * WIDTH, D_MODEL), BF16)],
        compiler_params=pltpu.CompilerParams(
            dimension_semantics=("arbitrary",), vmem_limit_bytes=V7X_VMEM_LIMIT),
        name="outproj_route_sort",
    )(a, b, x, w_out, g2, wr_hl, br_col, tri, ltri)


def _silu_mul(ab):
    a = ab[:, :D_EXPERT]
    return a * (1.0 / (1.0 + jnp.exp(-a))) * ab[:, D_EXPERT:]


def _chunk_offsets(tab, n_blk, e):
    return [pl.multiple_of(tab(j, e, 0), MOE_ROW_ALIGN) for j in range(n_blk)]


def _moe_gate_up_kernel(tab_ref, h_ref, mrow_ref, w1_ref, sh_ref, sg_ref, hs_ref, shid_ref, xs, *, flags_at):
    t = MOE_BLOCK
    n_blk = MOE_GATE_UP_BLOCKS
    first = pl.program_id(0) * n_blk
    tab = lambda j, e, c: tab_ref[((first + j) * N_EXPERTS + e) * 3 + c]

    piece = 512
    for j in range(n_blk):
        mrow = mrow_ref[j]
        dst1 = mrow[0:1, :].astype(jnp.int32)
        dst2 = mrow[1:2, :].astype(jnp.int32)
        h = h_ref[j * t:(j + 1) * t, :]
        for r0 in range(0, MOE_SORT_ROWS, piece):
            d_idx = lax.broadcasted_iota(jnp.int32, (piece, t), 0) + r0
            sel = jnp.where((d_idx == dst1) | (d_idx == dst2), 1.0, 0.0)
            xs[j, r0:r0 + piece, :] = _dot(sel, h).astype(BF16)
        xs[j, MOE_SORT_ROWS:MOE_ROWS, :] = jnp.zeros((MOE_CHUNK, D_MODEL), BF16)
    hs_ref[...] = jnp.zeros(hs_ref.shape, BF16)

    def first_chunks(g, carry):
        for i in range(MOE_EXPERTS_PER_STEP):
            e = g * MOE_EXPERTS_PER_STEP + i
            offs = _chunk_offsets(tab, n_blk, e)
            x = jnp.concatenate([xs[j, pl.ds(offs[j], MOE_CHUNK), :] for j in range(n_blk)], axis=0)
            hid = _silu_mul(jnp.dot(x, w1_ref[e], preferred_element_type=F32)).astype(BF16)
            for j in range(n_blk):
                hs_ref[j, pl.ds(offs[j], MOE_CHUNK), :] = hid[j * MOE_CHUNK:(j + 1) * MOE_CHUNK, :]
        return carry

    lax.fori_loop(0, N_EXPERTS // MOE_EXPERTS_PER_STEP, first_chunks, 0)

    def more_chunks(j, e, carry):
        off, n_chunks, end = tab(j, e, 0), tab(j, e, 1), tab(j, e, 2)

        def chunk(c, carry):
            r0 = pl.multiple_of(off + c * MOE_CHUNK, MOE_ROW_ALIGN)
            rows = r0 + lax.broadcasted_iota(jnp.int32, (MOE_CHUNK, D_EXPERT), 0)
            hid = _silu_mul(jnp.dot(xs[j, pl.ds(r0, MOE_CHUNK), :], w1_ref[e], preferred_element_type=F32))
            hs_ref[j, pl.ds(r0, MOE_CHUNK), :] = jnp.where(rows < end, hid.astype(BF16),
                                                           hs_ref[j, pl.ds(r0, MOE_CHUNK), :])
            return carry

        return lax.fori_loop(1, n_chunks, chunk, carry)

    for j in range(n_blk):
        @pl.when(tab_ref[flags_at + first + j] > 0)
        def _(j=j):
            lax.fori_loop(0, N_EXPERTS, functools.partial(more_chunks, j), 0)

    @pl.when(pl.program_id(0) == 0)
    def _():
        sh = sh_ref[...]
        gates = sg_ref[...]
        for e in range(N_EXPERTS):
            gate = gates[:, N_GROUPS + e:N_GROUPS + e + 1]
            hid = _silu_mul(jnp.dot(sh, w1_ref[e], preferred_element_type=F32)) * gate
            shid_ref[:, e * D_EXPERT:(e + 1) * D_EXPERT] = hid.astype(BF16)


def _moe_gate_up(tab, h2, mrow, w1_b, s_h, s_gates):
    n = h2.shape[0]
    n_s = s_h.shape[0]
    nblk = n // MOE_BLOCK
    g = MOE_GATE_UP_BLOCKS
    assert nblk % g == 0
    whole = lambda shape: pl.BlockSpec(shape, lambda i, tab: (0,) * len(shape))
    return pl.pallas_call(
        functools.partial(_moe_gate_up_kernel, flags_at=nblk * N_EXPERTS * 3),
        grid_spec=pltpu.PrefetchScalarGridSpec(
            num_scalar_prefetch=1,
            grid=(nblk // g,),
            in_specs=[pl.BlockSpec((g * MOE_BLOCK, D_MODEL), lambda i, tab: (i, 0)),
                      pl.BlockSpec((g, 8, MOE_BLOCK), lambda i, tab: (i, 0, 0)),
                      pl.BlockSpec(w1_b.shape, lambda i, tab: (0, 0, 0), pipeline_mode=pl.Buffered(1)),
                      whole(s_h.shape), whole(s_gates.shape)],
            out_specs=[pl.BlockSpec((g, MOE_ROWS, D_EXPERT), lambda i, tab: (i, 0, 0)),
                       whole((n_s, N_EXPERTS * D_EXPERT))],
            scratch_shapes=[pltpu.VMEM((g, MOE_ROWS, D_MODEL), BF16)]),
        out_shape=[jax.ShapeDtypeStruct((nblk, MOE_ROWS, D_EXPERT), BF16),
                   jax.ShapeDtypeStruct((n_s, N_EXPERTS * D_EXPERT), BF16)],
        compiler_params=pltpu.CompilerParams(
            dimension_semantics=("arbitrary",), vmem_limit_bytes=V7X_VMEM_LIMIT),
        name="moe_gate_up",
    )(tab, h2, mrow, w1_b, s_h, s_gates)


def _moe_down_kernel(tab_ref, hs_ref, xp_ref, mcol_ref, w2_ref, gf_ref, shid_ref, sx_ref, y_ref, sy_ref, os, *,
                     flags_at):
    t = MOE_BLOCK
    n_blk = MOE_DOWN_BLOCKS
    first = pl.program_id(0) * n_blk
    tab = lambda j, e, c: tab_ref[((first + j) * N_EXPERTS + e) * 3 + c]
    os[...] = jnp.zeros(os.shape, BF16)

    def first_chunks(g, carry):
        for i in range(MOE_EXPERTS_PER_STEP):
            e = g * MOE_EXPERTS_PER_STEP + i
            offs = _chunk_offsets(tab, n_blk, e)
            hid = jnp.concatenate([hs_ref[j, pl.ds(offs[j], MOE_CHUNK), :] for j in range(n_blk)], axis=0)
            out = jnp.dot(hid, w2_ref[e], preferred_element_type=F32).astype(BF16)
            for j in range(n_blk):
                os[j, pl.ds(offs[j], MOE_CHUNK), :] = out[j * MOE_CHUNK:(j + 1) * MOE_CHUNK, :]
        return carry

    lax.fori_loop(0, N_EXPERTS // MOE_EXPERTS_PER_STEP, first_chunks, 0)

    def more_chunks(j, e, carry):
        off, n_chunks, end = tab(j, e, 0), tab(j, e, 1), tab(j, e, 2)

        def chunk(c, carry):
            r0 = pl.multiple_of(off + c * MOE_CHUNK, MOE_ROW_ALIGN)
            rows = r0 + lax.broadcasted_iota(jnp.int32, (MOE_CHUNK, D_MODEL), 0)
            out = jnp.dot(hs_ref[j, pl.ds(r0, MOE_CHUNK), :], w2_ref[e], preferred_element_type=F32)
            os[j, pl.ds(r0, MOE_CHUNK), :] = jnp.where(rows < end, out.astype(BF16), os[j, pl.ds(r0, MOE_CHUNK), :])
            return carry

        return lax.fori_loop(1, n_chunks, chunk, carry)

    for j in range(n_blk):
        @pl.when(tab_ref[flags_at + first + j] > 0)
        def _(j=j):
            lax.fori_loop(0, N_EXPERTS, functools.partial(more_chunks, j), 0)

    l_idx = lax.broadcasted_iota(jnp.int32, (t, MOE_SORT_ROWS), 1)

    def scatter_matrix(j):
        mcol = mcol_ref[j * t:(j + 1) * t, :]
        d1c = mcol[:, 0:1].astype(jnp.int32)
        d2c = mcol[:, 1:2].astype(jnp.int32)
        comb = jnp.where(l_idx == d1c, mcol[:, 2:3], 0.0) + jnp.where(l_idx == d2c, mcol[:, 3:4], 0.0)
        return comb.astype(BF16)

    combs = [scatter_matrix(j) for j in range(n_blk)]
    moes = [_dot(combs[j], os[j, 0:MOE_SORT_ROWS, :]) for j in range(n_blk)]
    for j in range(n_blk):
        rows = slice(j * t, (j + 1) * t)
        y_ref[rows, :] = _rmsnorm(xp_ref[rows, :] + moes[j], gf_ref[...])

    @pl.when(pl.program_id(0) == 0)
    def _():
        w2_all = w2_ref[...].reshape(N_EXPERTS * D_EXPERT, D_MODEL)
        ys = sx_ref[...] + jnp.dot(shid_ref[...], w2_all, preferred_element_type=F32)
        sy_ref[...] = _rmsnorm(ys, gf_ref[...])


def _moe_down(tab, hs_sorted, xp, mcol, w2_b, gf, s_hid, s_x):
    n = xp.shape[0]
    n_s = s_x.shape[0]
    nblk = n // MOE_BLOCK
    g = MOE_DOWN_BLOCKS
    assert nblk % g == 0
    whole = lambda shape: pl.BlockSpec(shape, lambda i, tab: (0,) * len(shape))
    row_blk = lambda w: pl.BlockSpec((g * MOE_BLOCK, w), lambda i, tab: (i, 0))
    return pl.pallas_call(
        functools.partial(_moe_down_kernel, flags_at=nblk * N_EXPERTS * 3),
        grid_spec=pltpu.PrefetchScalarGridSpec(
            num_scalar_prefetch=1,
            grid=(nblk // g,),
            in_specs=[pl.BlockSpec((g, MOE_ROWS, D_EXPERT), lambda i, tab: (i, 0, 0)),
                      row_blk(D_MODEL), row_blk(ROUTER_LANES),
                      pl.BlockSpec(w2_b.shape, lambda i, tab: (0, 0, 0), pipeline_mode=pl.Buffered(1)),
                      whole((1, D_MODEL)), whole(s_hid.shape), whole(s_x.shape)],
            out_specs=[row_blk(D_MODEL), whole((n_s, D_MODEL))],
            scratch_shapes=[pltpu.VMEM((g, MOE_ROWS, D_MODEL), BF16)]),
        out_shape=[jax.ShapeDtypeStruct((n, D_MODEL), F32), jax.ShapeDtypeStruct((n_s, D_MODEL), F32)],
        compiler_params=pltpu.CompilerParams(
            dimension_semantics=("arbitrary",), vmem_limit_bytes=V7X_VMEM_LIMIT),
        name="moe_down_combine",
    )(tab, hs_sorted, xp, mcol, w2_b, gf, s_hid, s_x)


def _sample_proj_kernel(x_ref, g1_ref, w_ref, cos_ref, sin_ref, gv_ref, ones_ref, w00_ref, b0_ref,
                        rep_ref, foldt_ref, a_ref, k_ref, v_ref, vn_ref, qkvt_ref):
    h = _rmsnorm(x_ref[...], g1_ref[...])

    def proj(i):
        return _dot(h, w_ref[:, i * WIDTH:(i + 1) * WIDTH], precise=True)

    cos = _tile_lanes(cos_ref[...], WIDTH // 128)
    sin = _tile_lanes(sin_ref[...], WIDTH // 128)
    q = _rope(proj(2), cos, sin) * (HEAD_DIM ** -0.5)
    k = _rope(proj(3), cos, sin)
    v = proj(4)
    vn = _group_rmsnorm(proj(1), ones_ref[...], gv_ref[...], precise=True)
    a_ref[...] = proj(0) * (w00_ref[...] * vn + b0_ref[...])
    k_ref[...] = k
    v_ref[...] = v
    vn_ref[...] = vn

    n_rep = rep_ref.shape[0]
    r_idx = lax.broadcasted_iota(jnp.int32, (n_rep, WIDTH), 0)
    l_idx = lax.broadcasted_iota(jnp.int32, (n_rep, WIDTH), 1)
    own = (l_idx // HEAD_DIM) == (r_idx % N_HEADS)
    for t, src in enumerate((q, k, v)):
        rep = _dot(rep_ref[...], src, precise=True)
        qkvt_ref[t * HEAD_DIM:(t + 1) * HEAD_DIM, :] = lax.dot_general(
            foldt_ref[...], jnp.where(own, rep, 0.0), (((1,), (1,)), ((), ())),
            preferred_element_type=F32, precision=lax.Precision.HIGHEST)


def _sample_proj(x, g1, w_in, cos, sin, gv, ones_bd, w00, b0, rep, foldt):
    bd = x.shape[0]
    sds = lambda r, c: jax.ShapeDtypeStruct((r, c), F32)
    return pl.pallas_call(
        _sample_proj_kernel,
        out_shape=[sds(bd, WIDTH), sds(bd, WIDTH), sds(bd, WIDTH), sds(bd, WIDTH),
                   sds(3 * HEAD_DIM, bd * N_HEADS)],
        compiler_params=pltpu.CompilerParams(vmem_limit_bytes=V7X_VMEM_LIMIT),
        name="sample_proj",
    )(x, g1, w_in, cos, sin, gv, ones_bd, w00, b0, rep, foldt)


def _sample_scores(qkvt_ref, k_ref, b, win):
    n_col = qkvt_ref.shape[1]
    c_idx = lax.broadcasted_iota(jnp.int32, (n_col, 128), 0)
    l_idx = lax.broadcasted_iota(jnp.int32, (n_col, 128), 1)
    pick = jnp.where((c_idx == b * N_HEADS + l_idx) & (l_idx < N_HEADS), 1.0, 0.0)
    cols = _dot(qkvt_ref[...], pick, precise=True)

    head = lax.broadcasted_iota(jnp.int32, (N_HEADS, win), 0)
    s = jnp.zeros((N_HEADS, win), F32)
    s_self = jnp.zeros((N_HEADS, 1), F32)
    for h in range(N_HEADS):
        qc = cols[0:HEAD_DIM, h:h + 1]
        kc = cols[HEAD_DIM:2 * HEAD_DIM, h:h + 1]
        s_h = jnp.sum(k_ref[0, h] * qc, axis=0, keepdims=True)
        s = jnp.where(head == h, s_h, s)
        s_self = jnp.where(head[:, 0:1] == h, jnp.sum(qc * kc, axis=0, keepdims=True), s_self)
    return s, s_self, cols[2 * HEAD_DIM:3 * HEAD_DIM, :]


def _sample_finish(s, s_self, v_cols, v_ref, o_ref, win):
    dist = win - lax.broadcasted_iota(jnp.int32, (1, win), 1)
    members = [(dist <= N_KEYS * dil) & (dist % dil == 0) for dil in DILATIONS]
    es, e_selfs, dens, lses = [], [], [], []
    for mem in members:
        sm = jnp.where(mem, s, NEG)
        m = jnp.maximum(jnp.max(sm, axis=1, keepdims=True), s_self)
        e = jnp.exp(sm - m)
        e_self = jnp.exp(s_self - m)
        den = jnp.sum(e, axis=1, keepdims=True) + e_self
        es.append(e)
        e_selfs.append(e_self)
        dens.append(den)
        lses.append(m + jnp.log(den))
    top = jnp.maximum(jnp.maximum(lses[0], lses[1]), lses[2])
    ws = [jnp.exp(l - top) for l in lses]
    wsum = ws[0] + ws[1] + ws[2]
    coef = [w / (den * wsum) for w, den in zip(ws, dens)]
    p_keys = coef[0] * es[0] + coef[1] * es[1] + coef[2] * es[2]
    p_self = coef[0] * e_selfs[0] + coef[1] * e_selfs[1] + coef[2] * e_selfs[2]

    for h in range(N_HEADS):
        o_ref[0, :, h:h + 1] = (jnp.sum(v_ref[0, h] * p_keys[h:h + 1, :], axis=1, keepdims=True)
                                + p_self[h:h + 1, :] * v_cols[:, h:h + 1])


def _rope_tables(first_pos, count):
    half = HEAD_DIM // 2
    inv = ROPE_THETA ** (-np.arange(half, dtype=np.float64) * 2.0 / HEAD_DIM)
    ang = (first_pos + np.arange(count, dtype=np.float64))[:, None] * inv[None, :]
    cos, sin = np.cos(ang), np.sin(ang)
    cos128 = np.concatenate([cos, cos, cos, cos], axis=1).astype(np.float32)
    sin128 = np.concatenate([-sin, sin, -sin, sin], axis=1).astype(np.float32)
    return jnp.asarray(cos128), jnp.asarray(sin128)


def kernel(x_prompt, x_sample, cache_win_k, cache_win_v, ln1_g, w_in, sgu_norm_g, sgu_w, sgu_b, w_out, ln2_g,
           w_router_group, b_router_group, w_router_expert, b_router_expert, w_gate, w_up, w_down, lnf_g):
    depth = w_in.shape[0]
    assert depth == 1 and x_sample.shape[1] == 1
    B, S, _ = x_prompt.shape
    bd = x_sample.shape[0]
    win = cache_win_k.shape[2]
    assert S % (max(DILATIONS) * CHUNK) == 0 and win >= max(DILATIONS) * N_KEYS and PAST_LEN % CHUNK == 0
    l = 0

    pad = ROUTER_LANES - N_GROUPS - N_EXPERTS
    w_router = jnp.pad(jnp.concatenate([w_router_group[l], w_router_expert[l]], axis=1), ((0, 0), (0, pad)))
    b_router = jnp.pad(jnp.concatenate([b_router_group[l], b_router_expert[l]]), (0, pad))[None, :]
    g1 = ln1_g[l][None, :]
    g2 = ln2_g[l][None, :]
    gf = lnf_g[None, :]
    gv = sgu_norm_g[l].reshape(1, WIDTH)
    grp = np.arange(WIDTH) // HEAD_DIM
    ones_bd = jnp.asarray(np.where(grp[:, None] == grp[None, :], 1.0 / HEAD_DIM, 0.0), BF16)
    wp = jnp.concatenate([sgu_w[l][0::2], sgu_w[l][1::2]], axis=-1)
    bias = jnp.repeat(sgu_b[l].T, HEAD_DIM, axis=1)
    w00 = jnp.repeat(sgu_w[l][:, 0, 0], HEAD_DIM)[None, :]
    b0 = jnp.repeat(sgu_b[l][:, 0], HEAD_DIM)[None, :]

    cos_s, sin_s = _rope_tables(PAST_LEN, 1)
    rep = jnp.asarray(np.arange(bd * N_HEADS)[:, None] // N_HEADS == np.arange(bd)[None, :], F32)
    foldt = jnp.asarray(np.arange(HEAD_DIM)[:, None] == np.arange(WIDTH)[None, :] % HEAD_DIM, F32)
    xs = x_sample.reshape(bd, D_MODEL)
    a_s, k_s, v_s, vn_s, qkvt = _sample_proj(xs, g1, w_in[l], cos_s, sin_s, gv, ones_bd, w00, b0, rep, foldt)
    to_pos_minor = lambda c: jnp.transpose(c, (0, 2, 3, 1))

    cos_p, sin_p = _rope_tables(0, S)
    a_p, q_p, k_p, v_p, kt_p, vt_p, o3 = _prompt_proj(
        x_prompt, g1, w_in[l], cos_p, sin_p, gv, ones_bd, wp, bias,
        qkvt, to_pos_minor(cache_win_k[l]), to_pos_minor(cache_win_v[l]))
    b_p, w1_b, w2_b = _prompt_attention(q_p, k_p, v_p, w_gate[l], w_up[l], w_down[l])
    n = B * S
    assert n % MOE_BLOCK == 0
    wr_t = w_router.T[:ROUTER_ROWS]
    wr_hi = wr_t.astype(BF16)
    wr_hl = jnp.concatenate([wr_hi, (wr_t - wr_hi.astype(F32)).astype(BF16)], axis=0)
    lane_idx = np.arange(ROUTER_LANES)
    tri = jnp.asarray(lane_idx[:, None] <= lane_idx[None, :], BF16)
    ltri = jnp.asarray(lane_idx[None, :] < lane_idx[:, None], BF16)
    xp2, h2, mrow, mcol, tab_f = _mix_route_sort(
        a_p.reshape(n, WIDTH), b_p.reshape(n, WIDTH), x_prompt.reshape(n, D_MODEL),
        w_out[l], g2, wr_hl, b_router.reshape(ROUTER_LANES, 1)[:ROUTER_ROWS], tri, ltri)
    tab_i = tab_f[:, :N_EXPERTS, 0:3].astype(jnp.int32)
    multi_chunk = (jnp.max(tab_i[:, :, 1], axis=1) > 1).astype(jnp.int32)
    tab = jnp.concatenate([tab_i.reshape(-1), multi_chunk])

    b_s = jnp.transpose(o3, (0, 2, 1)).reshape(bd, WIDTH)
    xs2, hs2, gates_s = _mix_router(a_s, b_s, xs, w_out[l], g2, w_router, b_router, tm=bd, precise=True)

    hid_sorted, hid_s = _moe_gate_up(tab, h2, mrow, w1_b, hs2, gates_s)
    y_prompt, y_sample = _moe_down(tab, hid_sorted, xp2, mcol, w2_b, gf, hid_s, xs2)
    y_prompt = y_prompt.reshape(B, S, D_MODEL)
    y_sample = y_sample.reshape(bd, 1, D_MODEL)
    buf_p = min(MAX_WINDOW, S)
    to_win = lambda t: jnp.transpose(t.reshape(1, B, N_HEADS, HEAD_DIM, buf_p), (0, 1, 4, 2, 3))
    new_k_p = to_win(kt_p)
    new_v_p = to_win(vt_p)

    shape_s = (1, bd, 1, N_HEADS, HEAD_DIM)
    return (y_prompt, y_sample, new_k_p, new_v_p,
            k_s.reshape(shape_s), v_s.reshape(shape_s), vn_s.reshape(shape_s))
```

```python
import functools

import jax
import jax.numpy as jnp
import numpy as np
from jax import lax
from jax.experimental import pallas as pl
from jax.experimental.pallas import tpu as pltpu

F32 = jnp.float32
BF16 = jnp.bfloat16

D_MODEL = 1024
HEAD_DIM = 64
N_HEADS = 8
WIDTH = N_HEADS * HEAD_DIM
PROJ_COLS = 5 * WIDTH
CHUNK = 128
DILATIONS = (1, 4, 16)
N_KEYS = 128
MAX_WINDOW = 2048
PAST_LEN = 16384
ROPE_THETA = 10000.0
N_GROUPS = 4
EXPERTS_PER_GROUP = 8
N_EXPERTS = N_GROUPS * EXPERTS_PER_GROUP
D_EXPERT = 128
EPS = 1e-6
NEG = -1e30
TILES_PER_STEP = 32
SUBLANE_STRIDE = 4
assert DILATIONS == (1, SUBLANE_STRIDE, SUBLANE_STRIDE ** 2)
MOE_BLOCK = 512
MOE_ROW_ALIGN = 16
MOE_CHUNK = 48
MOE_EXPERTS_PER_STEP = 16
MOE_ROUTE_BLOCKS = 2
MOE_GATE_UP_BLOCKS = 4
MOE_DOWN_BLOCKS = 2
MOE_SORT_ROWS = -(-(2 * MOE_BLOCK + N_EXPERTS * (MOE_ROW_ALIGN - 1)) // 512) * 512
MOE_ROWS = MOE_SORT_ROWS + MOE_CHUNK
ROUTER_ROWS = 48
ROUTER_LANES = 128
V7X_VMEM_LIMIT = 56 * 1024 * 1024


def _rmsnorm(x, g):
    return x * lax.rsqrt(jnp.mean(x * x, axis=-1, keepdims=True) + EPS) * g


def _tile_lanes(t, reps):
    return jnp.concatenate([t] * reps, axis=1)


def _rope(t, cos, sin_signed):
    lane = lax.broadcasted_iota(jnp.int32, t.shape, 1)
    first_half = (lane % HEAD_DIM) < (HEAD_DIM // 2)
    n = t.shape[1]
    partner = jnp.where(first_half, pltpu.roll(t, n - HEAD_DIM // 2, 1), pltpu.roll(t, HEAD_DIM // 2, 1))
    return t * cos + partner * sin_signed


def _dot(a, b, precise=False):
    if precise:
        return jnp.dot(a.astype(F32), b.astype(F32), preferred_element_type=F32,
                       precision=lax.Precision.HIGHEST)
    return jnp.dot(a.astype(BF16), b.astype(BF16), preferred_element_type=F32)


def _group_rmsnorm(va, ones_bd, gv, precise=False):
    ms = _dot(va * va, ones_bd, precise)
    return va * lax.rsqrt(ms + EPS) * gv


def _proj_kernel(x_ref, g1_ref, w_ref, cos_ref, sin_ref, gv_ref, ones_ref, wp_ref, bias_ref,
                 qkvt_ref, ck_ref, cv_ref,
                 a_ref, q_ref, k_ref, v_ref, kt_ref, vt_ref, so_ref, wb, *, tm, first_win_tile, win):
    @pl.when((pl.program_id(0) == 0) & (pl.program_id(1) == 0))
    def _():
        wb[...] = w_ref[...].astype(BF16)

    h = _rmsnorm(x_ref[0], g1_ref[...]).astype(BF16)

    def proj(i):
        return jnp.dot(h, wb[:, i * WIDTH:(i + 1) * WIDTH], preferred_element_type=F32)

    cos = _tile_lanes(cos_ref[...], WIDTH // 128)
    sin = _tile_lanes(sin_ref[...], WIDTH // 128)
    q_ref[0] = _rope(proj(2), cos, sin) * (HEAD_DIM ** -0.5)
    k_ref[0] = _rope(proj(3), cos, sin)
    v_ref[0] = proj(4)

    s_scores, s_self, s_vcols = _sample_scores(qkvt_ref, ck_ref, win)

    u = proj(0)
    vn = _group_rmsnorm(proj(1), ones_ref[...], gv_ref[...]).astype(BF16)

    lane = lax.broadcasted_iota(jnp.int32, (CHUNK, 128), 1)
    left = lane < HEAD_DIM
    row = lax.broadcasted_iota(jnp.int32, (CHUNK, 2 * CHUNK), 0)
    col = lax.broadcasted_iota(jnp.int32, (CHUNK, 2 * CHUNK), 1)
    causal = (col % CHUNK) <= row
    zero = jnp.zeros((CHUNK, 128), BF16)
    wps = [jnp.where(causal, wp_ref[gp], 0.0).astype(BF16) for gp in range(N_HEADS // 2)]
    def block_diag(vv):
        return jnp.concatenate([jnp.where(left, vv, zero), jnp.where(left, zero, vv)], axis=0)

    for c in range(0, tm // CHUNK, 2):
        rows = [slice((c + i) * CHUNK, (c + i + 1) * CHUNK) for i in range(2)]
        mixes = [[], []]
        for gp in range(N_HEADS // 2):
            lanes = slice(gp * 128, (gp + 1) * 128)
            v2 = jnp.concatenate([block_diag(vn[rows[0], lanes]), block_diag(vn[rows[1], lanes])], axis=1)
            both = jnp.dot(wps[gp], v2, preferred_element_type=F32)
            mixes[0].append(both[:, 0:128])
            mixes[1].append(both[:, 128:256])
        for i in range(2):
            mix = jnp.concatenate(mixes[i], axis=1) + bias_ref[...]
            a_ref[0, rows[i], :] = (u[rows[i], :] * mix).astype(a_ref.dtype)

    _sample_finish(s_scores, s_self, s_vcols, cv_ref, so_ref, win)

    @pl.when(pl.program_id(1) >= first_win_tile)
    def _():
        kt_ref[0] = k_ref[0].T
        vt_ref[0] = v_ref[0].T


def _prompt_proj(x, g1, w_in_b, cos, sin, gv, ones_bd, wp, bias, qkvt, cache_k_t, cache_v_t, *, tm=512):
    B, S, _ = x.shape
    const2 = lambda b, j: (0, 0)
    out_sds = lambda dt: jax.ShapeDtypeStruct((B, S, WIDTH), dt)
    tile = pl.BlockSpec((1, tm, WIDTH), lambda b, j: (b, j, 0))
    win = min(MAX_WINDOW, S)
    first_win_tile = (S - win) // tm
    tile_t = pl.BlockSpec((1, WIDTH, tm), lambda b, j: (b, 0, jnp.maximum(j - first_win_tile, 0)))
    win_sds = jax.ShapeDtypeStruct((B, WIDTH, win), F32)
    n_seq, _, _, cache_win = cache_k_t.shape
    n_tiles = S // tm
    assert n_seq <= B * n_tiles, "one sample sequence rides on each grid step"
    seq_blk = lambda shape: pl.BlockSpec(
        (1,) + shape, lambda b, j: (jnp.minimum(b * n_tiles + j, n_seq - 1),) + (0,) * len(shape))
    cache_blk = seq_blk((N_HEADS, HEAD_DIM, cache_win))
    return pl.pallas_call(
        functools.partial(_proj_kernel, tm=tm, first_win_tile=first_win_tile, win=cache_win),
        grid=(B, n_tiles),
        in_specs=[
            pl.BlockSpec((1, tm, D_MODEL), lambda b, j: (b, j, 0)),
            pl.BlockSpec((1, D_MODEL), const2),
            pl.BlockSpec((D_MODEL, PROJ_COLS), const2, pipeline_mode=pl.Buffered(1)),
            pl.BlockSpec((tm, 128), lambda b, j: (j, 0)),
            pl.BlockSpec((tm, 128), lambda b, j: (j, 0)),
            pl.BlockSpec((1, WIDTH), const2),
            pl.BlockSpec((WIDTH, WIDTH), const2),
            pl.BlockSpec((N_HEADS // 2, CHUNK, 2 * CHUNK), lambda b, j: (0, 0, 0)),
            pl.BlockSpec((CHUNK, WIDTH), const2),
            seq_blk(qkvt.shape[1:]), cache_blk, cache_blk,
        ],
        out_specs=[tile, tile, tile, tile, tile_t, tile_t, seq_blk((HEAD_DIM, N_HEADS))],
        out_shape=[out_sds(BF16), out_sds(F32), out_sds(F32), out_sds(F32), win_sds, win_sds,
                   jax.ShapeDtypeStruct((n_seq, HEAD_DIM, N_HEADS), F32)],
        scratch_shapes=[pltpu.VMEM((D_MODEL, PROJ_COLS), BF16)],
        compiler_params=pltpu.CompilerParams(
            dimension_semantics=("arbitrary", "arbitrary"), vmem_limit_bytes=V7X_VMEM_LIMIT),
        name="prompt_proj_sgu",
    )(x, g1, w_in_b, cos, sin, gv, ones_bd, wp, bias, qkvt, cache_k_t, cache_v_t)


def _attn_kernel(q_ref, k_ref, v_ref, wg_ref, wu_ref, wd_ref, o_ref, w1_ref, w2_ref,
                 qd, kd, vd, res_o, res_l, nat_o, nat_l, bias, *, seq):
    n_tiles = seq // CHUNK
    last = len(DILATIONS) - 1
    stage_in = (res_o.at[last], res_l.at[last], nat_o.at[last - 1])
    stage_out = (res_o.at[last - 1], res_l.at[last - 1])

    w1_ref[:, :, 0:D_EXPERT] = wg_ref[...].astype(BF16)
    w1_ref[:, :, D_EXPERT:2 * D_EXPERT] = wu_ref[...].astype(BF16)
    w2_ref[...] = wd_ref[...].astype(BF16)

    lane = lax.broadcasted_iota(jnp.int32, (CHUNK, 128), 1)
    left = lane < HEAD_DIM
    qi2 = lax.broadcasted_iota(jnp.int32, (2 * CHUNK, 2 * CHUNK), 0) % CHUNK
    kj2 = lax.broadcasted_iota(jnp.int32, (2 * CHUNK, 2 * CHUNK), 1)
    dist2 = CHUNK + qi2 - kj2
    band2 = (dist2 >= 0) & (dist2 <= N_KEYS)
    zero_q = jnp.zeros((CHUNK, 128), BF16)
    bias[0] = jnp.where(band2, 0.0, NEG)
    bias[1] = jnp.where(band2 & (kj2 >= CHUNK), 0.0, NEG)

    kd[0:CHUNK, :] = jnp.zeros((CHUNK, 128), BF16)
    vd[0:CHUNK, :] = jnp.zeros((CHUNK, 128), BF16)

    for p, dil in enumerate(DILATIONS):
        sub = seq // dil
        nb = sub // CHUNK
        for ti, (src_ref, dst, pad) in enumerate(((q_ref, qd, 0), (k_ref, kd, CHUNK), (v_ref, vd, CHUNK))):
            if dil == 1:
                dst[pad:pad + seq, :] = src_ref[0].astype(BF16)
            elif dil == SUBLANE_STRIDE:
                for r in range(dil):
                    val = src_ref[0, pl.ds(r, sub, stride=dil), :]
                    stage_in[ti][r * sub:(r + 1) * sub, :] = val
                    dst[pad + r * sub:pad + (r + 1) * sub, :] = val.astype(BF16)
            else:
                coarse = seq // SUBLANE_STRIDE
                for r_lo in range(SUBLANE_STRIDE):
                    for r_hi in range(SUBLANE_STRIDE):
                        r = r_lo + SUBLANE_STRIDE * r_hi
                        val = stage_in[ti][pl.ds(r_lo * coarse + r_hi, sub, stride=SUBLANE_STRIDE), :]
                        dst[pad + r * sub:pad + (r + 1) * sub, :] = val.astype(BF16)

        def tile_body(g, i, p=p, nb=nb):
            t = g * TILES_PER_STEP + i
            row = pl.multiple_of(t * CHUNK, CHUNK)
            qt = qd[pl.ds(row, CHUNK), :]
            k2 = kd[pl.ds(row, 2 * CHUNK), :]
            v2 = vd[pl.ds(row, 2 * CHUNK), :]
            if TILES_PER_STEP % nb == 0:
                variant = 1 if i % nb == 0 else 0
            elif i == 0:
                variant = jnp.where((g * TILES_PER_STEP) % nb == 0, 1, 0)
            else:
                variant = 0
            q2 = jnp.concatenate([jnp.where(left, qt, zero_q), jnp.where(left, zero_q, qt)], axis=0)
            s = lax.dot_general(q2, k2, (((1,), (1,)), ((), ())), preferred_element_type=F32)
            s = s + bias[variant]
            m = jnp.max(s, axis=1, keepdims=True)
            e = jnp.exp(s - m)
            den = jnp.sum(e, axis=1, keepdims=True)
            pv = jnp.dot(e.astype(BF16), v2, preferred_element_type=F32) / den
            lse = jnp.broadcast_to(m + jnp.log(den), (2 * CHUNK, 128))
            res_o[p, pl.ds(row, CHUNK), :] = jnp.where(left, pv[0:CHUNK], pv[CHUNK:2 * CHUNK])
            res_l[p, pl.ds(row, CHUNK), :] = jnp.where(left, lse[0:CHUNK], lse[CHUNK:2 * CHUNK])

        def group_body(g, carry, tile_body=tile_body):
            for i in range(TILES_PER_STEP):
                tile_body(g, i)
            return carry

        lax.fori_loop(0, n_tiles // TILES_PER_STEP, group_body, 0)

    for p, dil in enumerate(DILATIONS):
        if dil == 1:
            continue
        sub = seq // dil
        for si, (res, nat) in enumerate(((res_o, nat_o), (res_l, nat_l))):
            if dil == SUBLANE_STRIDE:
                for r in range(dil):
                    nat[p - 1, pl.ds(r, sub, stride=dil), :] = res[p, r * sub:(r + 1) * sub, :]
            else:
                coarse = seq // SUBLANE_STRIDE
                for r_lo in range(SUBLANE_STRIDE):
                    for r_hi in range(SUBLANE_STRIDE):
                        r = r_lo + SUBLANE_STRIDE * r_hi
                        stage_out[si][pl.ds(r_lo * coarse + r_hi, sub, stride=SUBLANE_STRIDE), :] = (
                            res[p, r * sub:(r + 1) * sub, :])
                for r_lo in range(SUBLANE_STRIDE):
                    nat[p - 1, pl.ds(r_lo, coarse, stride=SUBLANE_STRIDE), :] = (
                        stage_out[si][r_lo * coarse:(r_lo + 1) * coarse, :])

    rows_per_step = 256

    def merge_body(c, carry):
        rows = pl.ds(pl.multiple_of(c * rows_per_step, rows_per_step), rows_per_step)
        l0, l1, l2 = res_l[0, rows, :], nat_l[0, rows, :], nat_l[1, rows, :]
        top = jnp.maximum(jnp.maximum(l0, l1), l2)
        w0, w1, w2 = jnp.exp(l0 - top), jnp.exp(l1 - top), jnp.exp(l2 - top)
        num = w0 * res_o[0, rows, :] + w1 * nat_o[0, rows, :] + w2 * nat_o[1, rows, :]
        o_ref[0, rows, :] = (num / (w0 + w1 + w2)).astype(o_ref.dtype)
        return carry

    lax.fori_loop(0, seq // rows_per_step, merge_body, 0)


def _prompt_attention(q, k, v, w_gate, w_up, w_down):
    B, S, _ = q.shape
    n_pairs = WIDTH // 128
    blk = pl.BlockSpec((1, S, 128), lambda b, hp: (b, 0, hp))
    n_steps = B * n_pairs
    assert N_EXPERTS % n_steps == 0
    e_blk = N_EXPERTS // n_steps
    expert_blk = lambda rows, cols: pl.BlockSpec((e_blk, rows, cols), lambda b, hp: (b * n_pairs + hp, 0, 0))
    return pl.pallas_call(
        functools.partial(_attn_kernel, seq=S),
        grid=(B, n_pairs),
        in_specs=[blk, blk, blk,
                  expert_blk(D_MODEL, D_EXPERT), expert_blk(D_MODEL, D_EXPERT), expert_blk(D_EXPERT, D_MODEL)],
        out_specs=[blk, expert_blk(D_MODEL, 2 * D_EXPERT), expert_blk(D_EXPERT, D_MODEL)],
        out_shape=[jax.ShapeDtypeStruct((B, S, WIDTH), BF16),
                   jax.ShapeDtypeStruct((N_EXPERTS, D_MODEL, 2 * D_EXPERT), BF16),
                   jax.ShapeDtypeStruct((N_EXPERTS, D_EXPERT, D_MODEL), BF16)],
        scratch_shapes=[
            pltpu.VMEM((S, 128), BF16),
            pltpu.VMEM((S + CHUNK, 128), BF16),
            pltpu.VMEM((S + CHUNK, 128), BF16),
            pltpu.VMEM((len(DILATIONS), S, 128), F32),
            pltpu.VMEM((len(DILATIONS), S, 128), F32),
            pltpu.VMEM((len(DILATIONS) - 1, S, 128), F32),
            pltpu.VMEM((len(DILATIONS) - 1, S, 128), F32),
            pltpu.VMEM((2, 2 * CHUNK, 2 * CHUNK), F32),
        ],
        compiler_params=pltpu.CompilerParams(
            dimension_semantics=("arbitrary", "arbitrary"), vmem_limit_bytes=V7X_VMEM_LIMIT),
        name="prompt_dilated_attention",
    )(q, k, v, w_gate, w_up, w_down)


def _route(logits):
    lane = lax.broadcasted_iota(jnp.int32, logits.shape, 1)
    big = jnp.int32(ROUTER_LANES)
    lg = jnp.where(lane < N_GROUPS, logits, NEG)
    gmax = jnp.max(lg, axis=1, keepdims=True)
    gp = 1.0 / jnp.sum(jnp.exp(lg - gmax), axis=1, keepdims=True)
    gi = jnp.min(jnp.where(lg == gmax, lane, big), axis=1, keepdims=True)
    lo = N_GROUPS + EXPERTS_PER_GROUP * gi
    le = jnp.where((lane >= lo) & (lane < lo + EXPERTS_PER_GROUP), logits, NEG)
    m1 = jnp.max(le, axis=1, keepdims=True)
    i1 = jnp.min(jnp.where(le == m1, lane, big), axis=1, keepdims=True)
    le2 = jnp.where(lane == i1, NEG, le)
    m2 = jnp.max(le2, axis=1, keepdims=True)
    i2 = jnp.min(jnp.where(le2 == m2, lane, big), axis=1, keepdims=True)
    e2 = jnp.exp(m2 - m1)
    w1 = 1.0 / (1.0 + e2)
    w2 = e2 / (1.0 + e2)
    return jnp.where(lane == i1, gp * w1, jnp.where(lane == i2, gp * w2, 0.0))


def _mix_router_kernel(a_ref, b_ref, x_ref, wo_ref, g2_ref, wr_ref, br_ref, xp_ref, h2_ref, gates_ref, *, precise):
    mixed = (_dot(a_ref[...], wo_ref[0:WIDTH, :], precise)
             + _dot(b_ref[...], wo_ref[WIDTH:2 * WIDTH, :], precise))
    xp = x_ref[...] + mixed
    xp_ref[...] = xp
    h2 = _rmsnorm(xp, g2_ref[...])
    h2_ref[...] = h2.astype(h2_ref.dtype)
    logits = jnp.dot(h2, wr_ref[...], preferred_element_type=F32,
                     precision=lax.Precision.HIGHEST) + br_ref[...]
    gates_ref[...] = _route(logits)


def _mix_router(a, b, x, w_out, g2, w_router, b_router, *, tm, precise=False):
    n = x.shape[0]
    const = lambda i: (0, 0)
    row_blk = lambda w: pl.BlockSpec((tm, w), lambda i: (i, 0))
    return pl.pallas_call(
        functools.partial(_mix_router_kernel, precise=precise),
        grid=(n // tm,),
        in_specs=[row_blk(WIDTH), row_blk(WIDTH), row_blk(D_MODEL),
                  pl.BlockSpec((2 * WIDTH, D_MODEL), const),
                  pl.BlockSpec((1, D_MODEL), const),
                  pl.BlockSpec((D_MODEL, ROUTER_LANES), const),
                  pl.BlockSpec((1, ROUTER_LANES), const)],
        out_specs=[row_blk(D_MODEL), row_blk(D_MODEL), row_blk(ROUTER_LANES)],
        out_shape=[jax.ShapeDtypeStruct((n, D_MODEL), F32),
                   jax.ShapeDtypeStruct((n, D_MODEL), BF16),
                   jax.ShapeDtypeStruct((n, ROUTER_LANES), F32)],
        compiler_params=pltpu.CompilerParams(
            dimension_semantics=("arbitrary",), vmem_limit_bytes=V7X_VMEM_LIMIT),
        name="outproj_router",
    )(a, b, x, w_out, g2, w_router, b_router)


def _nt_dot(w, t):
    return lax.dot_general(w, t, (((1,), (1,)), ((), ())), preferred_element_type=F32)


def _route_t(logits_t):
    row = lax.broadcasted_iota(jnp.int32, logits_t.shape, 0)
    big = jnp.int32(ROUTER_LANES)
    lg = jnp.where(row < N_GROUPS, logits_t, NEG)
    gmax = jnp.max(lg, axis=0, keepdims=True)
    gp = 1.0 / jnp.sum(jnp.exp(lg - gmax), axis=0, keepdims=True)
    gi = jnp.min(jnp.where(lg == gmax, row, big), axis=0, keepdims=True)
    lo = N_GROUPS + EXPERTS_PER_GROUP * gi
    le = jnp.where((row >= lo) & (row < lo + EXPERTS_PER_GROUP), logits_t, NEG)
    m1 = jnp.max(le, axis=0, keepdims=True)
    i1 = jnp.min(jnp.where(le == m1, row, big), axis=0, keepdims=True)
    le2 = jnp.where(row == i1, NEG, le)
    m2 = jnp.max(le2, axis=0, keepdims=True)
    i2 = jnp.min(jnp.where(le2 == m2, row, big), axis=0, keepdims=True)
    e2 = jnp.exp(m2 - m1)
    return i1 - N_GROUPS, i2 - N_GROUPS, gp / (1.0 + e2), gp * e2 / (1.0 + e2)


def _mix_route_sort_kernel(a_ref, b_ref, x_ref, wo_ref, g2_ref, wr_ref, brc_ref, tri_ref, ltri_ref,
                           xp_ref, h2_ref, mrow_ref, mcol_ref, tab_ref, wob):
    t = MOE_BLOCK

    @pl.when(pl.program_id(0) == 0)
    def _():
        wob[...] = wo_ref[...].astype(BF16)

    def project(j):
        rows = slice(j * t, (j + 1) * t)
        xp = x_ref[rows, :] + _dot(a_ref[rows, :], wob[0:WIDTH, :]) + _dot(b_ref[rows, :], wob[WIDTH:2 * WIDTH, :])
        xp_ref[rows, :] = xp
        h2 = _rmsnorm(xp, g2_ref[...])
        hi = h2.astype(BF16)
        h2_ref[rows, :] = hi
        return hi, (h2 - hi.astype(F32)).astype(BF16)

    def route(hi, lo):
        prod_hi = _nt_dot(wr_ref[...], hi)
        logits_t = (prod_hi[0:ROUTER_ROWS] + prod_hi[ROUTER_ROWS:2 * ROUTER_ROWS]
                    + _nt_dot(wr_ref[0:ROUTER_ROWS, :], lo) + brc_ref[...])
        return _route_t(logits_t)

    def sort_meta(j, ex1, ex2, gate1, gate2):
        rows = slice(j * t, (j + 1) * t)
        pair_e = jnp.concatenate([ex1, ex2], axis=1)
        row = lax.broadcasted_iota(jnp.int32, (N_EXPERTS, 2 * t), 0)
        onehot = jnp.where(row == pair_e, 1.0, 0.0)
        n_lane_tiles = 2 * t // 128
        local = _dot(jnp.concatenate([onehot[:, k * 128:(k + 1) * 128] for k in range(n_lane_tiles)], axis=0),
                     tri_ref[...])
        carry = jnp.zeros((N_EXPERTS, 1), F32)
        cums = []
        for k in range(n_lane_tiles):
            tile = local[k * N_EXPERTS:(k + 1) * N_EXPERTS, :]
            cums.append(tile + carry)
            carry = carry + tile[:, 127:128]
        cum = jnp.concatenate(cums, axis=1)
        rank = jnp.sum(onehot * cum, axis=0, keepdims=True) - 1.0
        counts = carry
        units32 = jnp.floor((counts + (MOE_ROW_ALIGN - 1)) * (1.0 / MOE_ROW_ALIGN))
        units = jnp.concatenate([jnp.broadcast_to(units32, (N_EXPERTS, 128)),
                                 jnp.zeros((ROUTER_LANES - N_EXPERTS, 128), F32)], axis=0)
        off = _dot(ltri_ref[...], units) * MOE_ROW_ALIGN
        dst = jnp.sum(onehot * off[0:N_EXPERTS, 0:1], axis=0, keepdims=True) + rank

        r8 = lax.broadcasted_iota(jnp.int32, (8, t), 0)
        mrow_ref[j] = jnp.where(r8 == 0, dst[:, 0:t], jnp.where(r8 == 1, dst[:, t:2 * t],
                                jnp.where(r8 == 2, gate1, jnp.where(r8 == 3, gate2, 0.0))))
        r128 = lax.broadcasted_iota(jnp.int32, (ROUTER_LANES, t), 0)
        meta = jnp.where(r128 == 0, dst[:, 0:t], jnp.where(r128 == 1, dst[:, t:2 * t],
                         jnp.where(r128 == 2, gate1, jnp.where(r128 == 3, gate2, 0.0))))
        mcol_ref[rows, :] = meta.T
        lane = lax.broadcasted_iota(jnp.int32, (ROUTER_LANES, 128), 1)
        n_rows = units * MOE_ROW_ALIGN
        chunks = jnp.floor((n_rows + (MOE_CHUNK - 1)) * (1.0 / MOE_CHUNK))
        tab_ref[j] = jnp.where(lane == 0, off, jnp.where(lane == 1, chunks, jnp.where(lane == 2, off + n_rows, 0.0)))

    blocks = range(MOE_ROUTE_BLOCKS)
    projected = [project(j) for j in blocks]
    routed = [route(hi, lo) for hi, lo in projected]
    for j in blocks:
        sort_meta(j, *routed[j])


def _mix_route_sort(a, b, x, w_out, g2, wr_hl, br_col, tri, ltri):
    n = x.shape[0]
    t = MOE_BLOCK
    nblk = n // t
    g = MOE_ROUTE_BLOCKS
    assert nblk % g == 0
    const = lambda i: (0, 0)
    row_blk = lambda w: pl.BlockSpec((g * t, w), lambda i: (i, 0))
    return pl.pallas_call(
        _mix_route_sort_kernel,
        grid=(nblk // g,),
        in_specs=[row_blk(WIDTH), row_blk(WIDTH), row_blk(D_MODEL),
                  pl.BlockSpec((2 * WIDTH, D_MODEL), const, pipeline_mode=pl.Buffered(1)),
                  pl.BlockSpec((1, D_MODEL), const),
                  pl.BlockSpec((2 * ROUTER_ROWS, D_MODEL), const),
                  pl.BlockSpec((ROUTER_ROWS, 1), const),
                  pl.BlockSpec((ROUTER_LANES, ROUTER_LANES), const),
                  pl.BlockSpec((ROUTER_LANES, ROUTER_LANES), const)],
        out_specs=[row_blk(D_MODEL), row_blk(D_MODEL),
                   pl.BlockSpec((g, 8, t), lambda i: (i, 0, 0)),
                   row_blk(ROUTER_LANES),
                   pl.BlockSpec((g, ROUTER_LANES, 128), lambda i: (i, 0, 0))],
        out_shape=[jax.ShapeDtypeStruct((n, D_MODEL), F32),
                   jax.ShapeDtypeStruct((n, D_MODEL), BF16),
                   jax.ShapeDtypeStruct((nblk, 8, t), F32),
                   jax.ShapeDtypeStruct((n, ROUTER_LANES), F32),
                   jax.ShapeDtypeStruct((nblk, ROUTER_LANES, 128), F32)],
        scratch_shapes=[pltpu.VMEM((2 * WIDTH, D_MODEL), BF16)],
        compiler_params=pltpu.CompilerParams(
            dimension_semantics=("arbitrary",), vmem_limit_bytes=V7X_VMEM_LIMIT),
        name="outproj_route_sort",
    )(a, b, x, w_out, g2, wr_hl, br_col, tri, ltri)


def _silu_mul(ab):
    a = ab[:, :D_EXPERT]
    return a * (1.0 / (1.0 + jnp.exp(-a))) * ab[:, D_EXPERT:]


def _chunk_offsets(tab, n_blk, e):
    return [pl.multiple_of(tab(j, e, 0), MOE_ROW_ALIGN) for j in range(n_blk)]


def _moe_gate_up_kernel(tab_ref, h_ref, mrow_ref, w1_ref, sh_ref, sg_ref, hs_ref, shid_ref, xs, *, flags_at):
    t = MOE_BLOCK
    n_blk = MOE_GATE_UP_BLOCKS
    first = pl.program_id(0) * n_blk
    tab = lambda j, e, c: tab_ref[((first + j) * N_EXPERTS + e) * 3 + c]

    piece = 512
    for j in range(n_blk):
        mrow = mrow_ref[j]
        dst1 = mrow[0:1, :].astype(jnp.int32)
        dst2 = mrow[1:2, :].astype(jnp.int32)
        h = h_ref[j * t:(j + 1) * t, :]
        for r0 in range(0, MOE_SORT_ROWS, piece):
            d_idx = lax.broadcasted_iota(jnp.int32, (piece, t), 0) + r0
            sel = jnp.where((d_idx == dst1) | (d_idx == dst2), 1.0, 0.0)
            xs[j, r0:r0 + piece, :] = _dot(sel, h).astype(BF16)
        xs[j, MOE_SORT_ROWS:MOE_ROWS, :] = jnp.zeros((MOE_CHUNK, D_MODEL), BF16)
    hs_ref[...] = jnp.zeros(hs_ref.shape, BF16)

    def first_chunks(g, carry):
        for i in range(MOE_EXPERTS_PER_STEP):
            e = g * MOE_EXPERTS_PER_STEP + i
            offs = _chunk_offsets(tab, n_blk, e)
            x = jnp.concatenate([xs[j, pl.ds(offs[j], MOE_CHUNK), :] for j in range(n_blk)], axis=0)
            hid = _silu_mul(jnp.dot(x, w1_ref[e], preferred_element_type=F32)).astype(BF16)
            for j in range(n_blk):
                hs_ref[j, pl.ds(offs[j], MOE_CHUNK), :] = hid[j * MOE_CHUNK:(j + 1) * MOE_CHUNK, :]
        return carry

    lax.fori_loop(0, N_EXPERTS // MOE_EXPERTS_PER_STEP, first_chunks, 0)

    def more_chunks(j, e, carry):
        off, n_chunks, end = tab(j, e, 0), tab(j, e, 1), tab(j, e, 2)

        def chunk(c, carry):
            r0 = pl.multiple_of(off + c * MOE_CHUNK, MOE_ROW_ALIGN)
            rows = r0 + lax.broadcasted_iota(jnp.int32, (MOE_CHUNK, D_EXPERT), 0)
            hid = _silu_mul(jnp.dot(xs[j, pl.ds(r0, MOE_CHUNK), :], w1_ref[e], preferred_element_type=F32))
            hs_ref[j, pl.ds(r0, MOE_CHUNK), :] = jnp.where(rows < end, hid.astype(BF16),
                                                           hs_ref[j, pl.ds(r0, MOE_CHUNK), :])
            return carry

        return lax.fori_loop(1, n_chunks, chunk, carry)

    for j in range(n_blk):
        @pl.when(tab_ref[flags_at + first + j] > 0)
        def _(j=j):
            lax.fori_loop(0, N_EXPERTS, functools.partial(more_chunks, j), 0)

    @pl.when(pl.program_id(0) == 0)
    def _():
        sh = sh_ref[...]
        gates = sg_ref[...]
        for e in range(N_EXPERTS):
            gate = gates[:, N_GROUPS + e:N_GROUPS + e + 1]
            hid = _silu_mul(jnp.dot(sh, w1_ref[e], preferred_element_type=F32)) * gate
            shid_ref[:, e * D_EXPERT:(e + 1) * D_EXPERT] = hid.astype(BF16)


def _moe_gate_up(tab, h2, mrow, w1_b, s_h, s_gates):
    n = h2.shape[0]
    n_s = s_h.shape[0]
    nblk = n // MOE_BLOCK
    g = MOE_GATE_UP_BLOCKS
    assert nblk % g == 0
    whole = lambda shape: pl.BlockSpec(shape, lambda i, tab: (0,) * len(shape))
    return pl.pallas_call(
        functools.partial(_moe_gate_up_kernel, flags_at=nblk * N_EXPERTS * 3),
        grid_spec=pltpu.PrefetchScalarGridSpec(
            num_scalar_prefetch=1,
            grid=(nblk // g,),
            in_specs=[pl.BlockSpec((g * MOE_BLOCK, D_MODEL), lambda i, tab: (i, 0)),
                      pl.BlockSpec((g, 8, MOE_BLOCK), lambda i, tab: (i, 0, 0)),
                      pl.BlockSpec(w1_b.shape, lambda i, tab: (0, 0, 0), pipeline_mode=pl.Buffered(1)),
                      whole(s_h.shape), whole(s_gates.shape)],
            out_specs=[pl.BlockSpec((g, MOE_ROWS, D_EXPERT), lambda i, tab: (i, 0, 0)),
                       whole((n_s, N_EXPERTS * D_EXPERT))],
            scratch_shapes=[pltpu.VMEM((g, MOE_ROWS, D_MODEL), BF16)]),
        out_shape=[jax.ShapeDtypeStruct((nblk, MOE_ROWS, D_EXPERT), BF16),
                   jax.ShapeDtypeStruct((n_s, N_EXPERTS * D_EXPERT), BF16)],
        compiler_params=pltpu.CompilerParams(
            dimension_semantics=("arbitrary",), vmem_limit_bytes=V7X_VMEM_LIMIT),
        name="moe_gate_up",
    )(tab, h2, mrow, w1_b, s_h, s_gates)


def _moe_down_kernel(tab_ref, hs_ref, xp_ref, mcol_ref, w2_ref, gf_ref, shid_ref, sx_ref, y_ref, sy_ref, os, *,
                     flags_at):
    t = MOE_BLOCK
    n_blk = MOE_DOWN_BLOCKS
    first = pl.program_id(0) * n_blk
    tab = lambda j, e, c: tab_ref[((first + j) * N_EXPERTS + e) * 3 + c]
    os[...] = jnp.zeros(os.shape, BF16)

    def first_chunks(g, carry):
        for i in range(MOE_EXPERTS_PER_STEP):
            e = g * MOE_EXPERTS_PER_STEP + i
            offs = _chunk_offsets(tab, n_blk, e)
            hid = jnp.concatenate([hs_ref[j, pl.ds(offs[j], MOE_CHUNK), :] for j in range(n_blk)], axis=0)
            out = jnp.dot(hid, w2_ref[e], preferred_element_type=F32).astype(BF16)
            for j in range(n_blk):
                os[j, pl.ds(offs[j], MOE_CHUNK), :] = out[j * MOE_CHUNK:(j + 1) * MOE_CHUNK, :]
        return carry

    lax.fori_loop(0, N_EXPERTS // MOE_EXPERTS_PER_STEP, first_chunks, 0)

    def more_chunks(j, e, carry):
        off, n_chunks, end = tab(j, e, 0), tab(j, e, 1), tab(j, e, 2)

        def chunk(c, carry):
            r0 = pl.multiple_of(off + c * MOE_CHUNK, MOE_ROW_ALIGN)
            rows = r0 + lax.broadcasted_iota(jnp.int32, (MOE_CHUNK, D_MODEL), 0)
            out = jnp.dot(hs_ref[j, pl.ds(r0, MOE_CHUNK), :], w2_ref[e], preferred_element_type=F32)
            os[j, pl.ds(r0, MOE_CHUNK), :] = jnp.where(rows < end, out.astype(BF16), os[j, pl.ds(r0, MOE_CHUNK), :])
            return carry

        return lax.fori_loop(1, n_chunks, chunk, carry)

    for j in range(n_blk):
        @pl.when(tab_ref[flags_at + first + j] > 0)
        def _(j=j):
            lax.fori_loop(0, N_EXPERTS, functools.partial(more_chunks, j), 0)

    l_idx = lax.broadcasted_iota(jnp.int32, (t, MOE_SORT_ROWS), 1)

    def scatter_matrix(j):
        mcol = mcol_ref[j * t:(j + 1) * t, :]
        d1c = mcol[:, 0:1].astype(jnp.int32)
        d2c = mcol[:, 1:2].astype(jnp.int32)
        comb = jnp.where(l_idx == d1c, mcol[:, 2:3], 0.0) + jnp.where(l_idx == d2c, mcol[:, 3:4], 0.0)
        return comb.astype(BF16)

    combs = [scatter_matrix(j) for j in range(n_blk)]
    moes = [_dot(combs[j], os[j, 0:MOE_SORT_ROWS, :]) for j in range(n_blk)]
    for j in range(n_blk):
        rows = slice(j * t, (j + 1) * t)
        y_ref[rows, :] = _rmsnorm(xp_ref[rows, :] + moes[j], gf_ref[...])

    @pl.when(pl.program_id(0) == 0)
    def _():
        w2_all = w2_ref[...].reshape(N_EXPERTS * D_EXPERT, D_MODEL)
        ys = sx_ref[...] + jnp.dot(shid_ref[...], w2_all, preferred_element_type=F32)
        sy_ref[...] = _rmsnorm(ys, gf_ref[...])


def _moe_down(tab, hs_sorted, xp, mcol, w2_b, gf, s_hid, s_x):
    n = xp.shape[0]
    n_s = s_x.shape[0]
    nblk = n // MOE_BLOCK
    g = MOE_DOWN_BLOCKS
    assert nblk % g == 0
    whole = lambda shape: pl.BlockSpec(shape, lambda i, tab: (0,) * len(shape))
    row_blk = lambda w: pl.BlockSpec((g * MOE_BLOCK, w), lambda i, tab: (i, 0))
    return pl.pallas_call(
        functools.partial(_moe_down_kernel, flags_at=nblk * N_EXPERTS * 3),
        grid_spec=pltpu.PrefetchScalarGridSpec(
            num_scalar_prefetch=1,
            grid=(nblk // g,),
            in_specs=[pl.BlockSpec((g, MOE_ROWS, D_EXPERT), lambda i, tab: (i, 0, 0)),
                      row_blk(D_MODEL), row_blk(ROUTER_LANES),
                      pl.BlockSpec(w2_b.shape, lambda i, tab: (0, 0, 0), pipeline_mode=pl.Buffered(1)),
                      whole((1, D_MODEL)), whole(s_hid.shape), whole(s_x.shape)],
            out_specs=[row_blk(D_MODEL), whole((n_s, D_MODEL))],
            scratch_shapes=[pltpu.VMEM((g, MOE_ROWS, D_MODEL), BF16)]),
        out_shape=[jax.ShapeDtypeStruct((n, D_MODEL), F32), jax.ShapeDtypeStruct((n_s, D_MODEL), F32)],
        compiler_params=pltpu.CompilerParams(
            dimension_semantics=("arbitrary",), vmem_limit_bytes=V7X_VMEM_LIMIT),
        name="moe_down_combine",
    )(tab, hs_sorted, xp, mcol, w2_b, gf, s_hid, s_x)


def _sample_proj_kernel(x_ref, g1_ref, w_ref, cos_ref, sin_ref, gv_ref, ones_ref, w00_ref, b0_ref,
                        rep_ref, foldt_ref, a_ref, k_ref, v_ref, vn_ref, qkvt_ref):
    h = _rmsnorm(x_ref[...], g1_ref[...])

    def proj(i):
        return _dot(h, w_ref[:, i * WIDTH:(i + 1) * WIDTH], precise=True)

    cos = _tile_lanes(cos_ref[...], WIDTH // 128)
    sin = _tile_lanes(sin_ref[...], WIDTH // 128)
    q = _rope(proj(2), cos, sin) * (HEAD_DIM ** -0.5)
    k = _rope(proj(3), cos, sin)
    v = proj(4)
    vn = _group_rmsnorm(proj(1), ones_ref[...], gv_ref[...], precise=True)
    a_ref[...] = proj(0) * (w00_ref[...] * vn + b0_ref[...])
    k_ref[...] = k
    v_ref[...] = v
    vn_ref[...] = vn

    n_rep = rep_ref.shape[0]
    r_idx = lax.broadcasted_iota(jnp.int32, (n_rep, WIDTH), 0)
    l_idx = lax.broadcasted_iota(jnp.int32, (n_rep, WIDTH), 1)
    own = (l_idx // HEAD_DIM) == (r_idx % N_HEADS)
    for t, src in enumerate((q, k, v)):
        rep = _dot(rep_ref[...], src, precise=True)
        qkvt_ref[t * HEAD_DIM:(t + 1) * HEAD_DIM, :] = lax.dot_general(
            foldt_ref[...], jnp.where(own, rep, 0.0), (((1,), (1,)), ((), ())),
            preferred_element_type=F32, precision=lax.Precision.HIGHEST)


def _sample_proj(x, g1, w_in, cos, sin, gv, ones_bd, w00, b0, rep, foldt):
    bd = x.shape[0]
    sds = lambda r, c: jax.ShapeDtypeStruct((r, c), F32)
    return pl.pallas_call(
        _sample_proj_kernel,
        out_shape=[sds(bd, WIDTH), sds(bd, WIDTH), sds(bd, WIDTH), sds(bd, WIDTH),
                   sds(3 * HEAD_DIM, bd * N_HEADS)],
        compiler_params=pltpu.CompilerParams(vmem_limit_bytes=V7X_VMEM_LIMIT),
        name="sample_proj",
    )(x, g1, w_in, cos, sin, gv, ones_bd, w00, b0, rep, foldt)


def _sample_scores(qkv_ref, k_ref, win):
    cols = qkv_ref[0]

    head = lax.broadcasted_iota(jnp.int32, (N_HEADS, win), 0)
    s = jnp.zeros((N_HEADS, win), F32)
    s_self = jnp.zeros((N_HEADS, 1), F32)
    for h in range(N_HEADS):
        qc = cols[0:HEAD_DIM, h:h + 1]
        kc = cols[HEAD_DIM:2 * HEAD_DIM, h:h + 1]
        s_h = jnp.sum(k_ref[0, h] * qc, axis=0, keepdims=True)
        s = jnp.where(head == h, s_h, s)
        s_self = jnp.where(head[:, 0:1] == h, jnp.sum(qc * kc, axis=0, keepdims=True), s_self)
    return s, s_self, cols[2 * HEAD_DIM:3 * HEAD_DIM, :]


def _sample_finish(s, s_self, v_cols, v_ref, o_ref, win):
    dist = win - lax.broadcasted_iota(jnp.int32, (1, win), 1)
    members = [(dist <= N_KEYS * dil) & (dist % dil == 0) for dil in DILATIONS]
    es, e_selfs, dens, lses = [], [], [], []
    for mem in members:
        sm = jnp.where(mem, s, NEG)
        m = jnp.maximum(jnp.max(sm, axis=1, keepdims=True), s_self)
        e = jnp.exp(sm - m)
        e_self = jnp.exp(s_self - m)
        den = jnp.sum(e, axis=1, keepdims=True) + e_self
        es.append(e)
        e_selfs.append(e_self)
        dens.append(den)
        lses.append(m + jnp.log(den))
    top = jnp.maximum(jnp.maximum(lses[0], lses[1]), lses[2])
    ws = [jnp.exp(l - top) for l in lses]
    wsum = ws[0] + ws[1] + ws[2]
    coef = [w / (den * wsum) for w, den in zip(ws, dens)]
    p_keys = coef[0] * es[0] + coef[1] * es[1] + coef[2] * es[2]
    p_self = coef[0] * e_selfs[0] + coef[1] * e_selfs[1] + coef[2] * e_selfs[2]

    for h in range(N_HEADS):
        o_ref[0, :, h:h + 1] = (jnp.sum(v_ref[0, h] * p_keys[h:h + 1, :], axis=1, keepdims=True)
                                + p_self[h:h + 1, :] * v_cols[:, h:h + 1])


def _rope_tables(first_pos, count):
    half = HEAD_DIM // 2
    inv = ROPE_THETA ** (-np.arange(half, dtype=np.float64) * 2.0 / HEAD_DIM)
    ang = (first_pos + np.arange(count, dtype=np.float64))[:, None] * inv[None, :]
    cos, sin = np.cos(ang), np.sin(ang)
    cos128 = np.concatenate([cos, cos, cos, cos], axis=1).astype(np.float32)
    sin128 = np.concatenate([-sin, sin, -sin, sin], axis=1).astype(np.float32)
    return jnp.asarray(cos128), jnp.asarray(sin128)


def kernel(x_prompt, x_sample, cache_win_k, cache_win_v, ln1_g, w_in, sgu_norm_g, sgu_w, sgu_b, w_out, ln2_g,
           w_router_group, b_router_group, w_router_expert, b_router_expert, w_gate, w_up, w_down, lnf_g):
    depth = w_in.shape[0]
    assert depth == 1 and x_sample.shape[1] == 1
    B, S, _ = x_prompt.shape
    bd = x_sample.shape[0]
    win = cache_win_k.shape[2]
    assert S % (max(DILATIONS) * CHUNK) == 0 and win >= max(DILATIONS) * N_KEYS and PAST_LEN % CHUNK == 0
    l = 0

    pad = ROUTER_LANES - N_GROUPS - N_EXPERTS
    w_router = jnp.pad(jnp.concatenate([w_router_group[l], w_router_expert[l]], axis=1), ((0, 0), (0, pad)))
    b_router = jnp.pad(jnp.concatenate([b_router_group[l], b_router_expert[l]]), (0, pad))[None, :]
    g1 = ln1_g[l][None, :]
    g2 = ln2_g[l][None, :]
    gf = lnf_g[None, :]
    gv = sgu_norm_g[l].reshape(1, WIDTH)
    grp = np.arange(WIDTH) // HEAD_DIM
    ones_bd = jnp.asarray(np.where(grp[:, None] == grp[None, :], 1.0 / HEAD_DIM, 0.0), BF16)
    wp = jnp.concatenate([sgu_w[l][0::2], sgu_w[l][1::2]], axis=-1)
    bias = jnp.repeat(sgu_b[l].T, HEAD_DIM, axis=1)
    w00 = jnp.repeat(sgu_w[l][:, 0, 0], HEAD_DIM)[None, :]
    b0 = jnp.repeat(sgu_b[l][:, 0], HEAD_DIM)[None, :]

    cos_s, sin_s = _rope_tables(PAST_LEN, 1)
    rep = jnp.asarray(np.arange(bd * N_HEADS)[:, None] // N_HEADS == np.arange(bd)[None, :], F32)
    foldt = jnp.asarray(np.arange(HEAD_DIM)[:, None] == np.arange(WIDTH)[None, :] % HEAD_DIM, F32)
    xs = x_sample.reshape(bd, D_MODEL)
    a_s, k_s, v_s, vn_s, qkvt = _sample_proj(xs, g1, w_in[l], cos_s, sin_s, gv, ones_bd, w00, b0, rep, foldt)
    to_pos_minor = lambda c: jnp.transpose(c, (0, 2, 3, 1))

    cos_p, sin_p = _rope_tables(0, S)
    qkv_seq = jnp.transpose(qkvt.reshape(3 * HEAD_DIM, bd, N_HEADS), (1, 0, 2))
    a_p, q_p, k_p, v_p, kt_p, vt_p, o3 = _prompt_proj(
        x_prompt, g1, w_in[l], cos_p, sin_p, gv, ones_bd, wp, bias,
        qkv_seq, to_pos_minor(cache_win_k[l]), to_pos_minor(cache_win_v[l]))
    b_p, w1_b, w2_b = _prompt_attention(q_p, k_p, v_p, w_gate[l], w_up[l], w_down[l])
    n = B * S
    assert n % MOE_BLOCK == 0
    wr_t = w_router.T[:ROUTER_ROWS]
    wr_hi = wr_t.astype(BF16)
    wr_hl = jnp.concatenate([wr_hi, (wr_t - wr_hi.astype(F32)).astype(BF16)], axis=0)
    lane_idx = np.arange(ROUTER_LANES)
    tri = jnp.asarray(lane_idx[:, None] <= lane_idx[None, :], BF16)
    ltri = jnp.asarray(lane_idx[None, :] < lane_idx[:, None], BF16)
    xp2, h2, mrow, mcol, tab_f = _mix_route_sort(
        a_p.reshape(n, WIDTH), b_p.reshape(n, WIDTH), x_prompt.reshape(n, D_MODEL),
        w_out[l], g2, wr_hl, b_router.reshape(ROUTER_LANES, 1)[:ROUTER_ROWS], tri, ltri)
    tab_i = tab_f[:, :N_EXPERTS, 0:3].astype(jnp.int32)
    multi_chunk = (jnp.max(tab_i[:, :, 1], axis=1) > 1).astype(jnp.int32)
    tab = jnp.concatenate([tab_i.reshape(-1), multi_chunk])

    b_s = jnp.transpose(o3, (0, 2, 1)).reshape(bd, WIDTH)
    xs2, hs2, gates_s = _mix_router(a_s, b_s, xs, w_out[l], g2, w_router, b_router, tm=bd, precise=True)

    hid_sorted, hid_s = _moe_gate_up(tab, h2, mrow, w1_b, hs2, gates_s)
    y_prompt, y_sample = _moe_down(tab, hid_sorted, xp2, mcol, w2_b, gf, hid_s, xs2)
    y_prompt = y_prompt.reshape(B, S, D_MODEL)
    y_sample = y_sample.reshape(bd, 1, D_MODEL)
    buf_p = min(MAX_WINDOW, S)
    to_win = lambda t: jnp.transpose(t.reshape(1, B, N_HEADS, HEAD_DIM, buf_p), (0, 1, 4, 2, 3))
    new_k_p = to_win(kt_p)
    new_v_p = to_win(vt_p)

    shape_s = (1, bd, 1, N_HEADS, HEAD_DIM)
    return (y_prompt, y_sample, new_k_p, new_v_p,
            k_s.reshape(shape_s), v_s.reshape(shape_s), vn_s.reshape(shape_s))
```

```python
import functools

import jax
import jax.numpy as jnp
import numpy as np
from jax import lax
from jax.experimental import pallas as pl
from jax.experimental.pallas import tpu as pltpu

F32 = jnp.float32
BF16 = jnp.bfloat16

D_MODEL = 1024
HEAD_DIM = 64
N_HEADS = 8
WIDTH = N_HEADS * HEAD_DIM
PROJ_COLS = 5 * WIDTH
CHUNK = 128
DILATIONS = (1, 4, 16)
N_KEYS = 128
MAX_WINDOW = 2048
PAST_LEN = 16384
ROPE_THETA = 10000.0
N_GROUPS = 4
EXPERTS_PER_GROUP = 8
N_EXPERTS = N_GROUPS * EXPERTS_PER_GROUP
D_EXPERT = 128
EPS = 1e-6
NEG = -1e30
TILES_PER_STEP = 32
SUBLANE_STRIDE = 4
assert DILATIONS == (1, SUBLANE_STRIDE, SUBLANE_STRIDE ** 2)
MOE_BLOCK = 512
MOE_ROW_ALIGN = 16
MOE_CHUNK = 48
MOE_EXPERTS_PER_STEP = 16
MOE_ROUTE_BLOCKS = 2
MOE_GATE_UP_BLOCKS = 4
MOE_DOWN_BLOCKS = 2
MOE_SORT_ROWS = -(-(2 * MOE_BLOCK + N_EXPERTS * (MOE_ROW_ALIGN - 1)) // 512) * 512
MOE_ROWS = MOE_SORT_ROWS + MOE_CHUNK
ROUTER_ROWS = 48
ROUTER_LANES = 128
V7X_VMEM_LIMIT = 56 * 1024 * 1024


def _rmsnorm(x, g):
    return x * lax.rsqrt(jnp.mean(x * x, axis=-1, keepdims=True) + EPS) * g


def _tile_lanes(t, reps):
    return jnp.concatenate([t] * reps, axis=1)


def _rope(t, cos, sin_signed):
    lane = lax.broadcasted_iota(jnp.int32, t.shape, 1)
    first_half = (lane % HEAD_DIM) < (HEAD_DIM // 2)
    n = t.shape[1]
    partner = jnp.where(first_half, pltpu.roll(t, n - HEAD_DIM // 2, 1), pltpu.roll(t, HEAD_DIM // 2, 1))
    return t * cos + partner * sin_signed


def _dot(a, b, precise=False):
    if precise:
        return jnp.dot(a.astype(F32), b.astype(F32), preferred_element_type=F32,
                       precision=lax.Precision.HIGHEST)
    return jnp.dot(a.astype(BF16), b.astype(BF16), preferred_element_type=F32)


def _group_rmsnorm(va, ones_bd, gv, precise=False):
    ms = _dot(va * va, ones_bd, precise)
    return va * lax.rsqrt(ms + EPS) * gv


def _proj_kernel(x_ref, g1_ref, w_ref, cos_ref, sin_ref, gv_ref, ones_ref, wp_ref, bias_ref,
                 sqkv_ref, ck_ref, cv_ref,
                 a_ref, q_ref, k_ref, v_ref, kt_ref, vt_ref, so_ref, wb, *, tm, first_win_tile, win):
    @pl.when((pl.program_id(0) == 0) & (pl.program_id(1) == 0))
    def _():
        wb[...] = w_ref[...].astype(BF16)

    h = _rmsnorm(x_ref[0], g1_ref[...]).astype(BF16)

    def proj(i):
        return jnp.dot(h, wb[:, i * WIDTH:(i + 1) * WIDTH], preferred_element_type=F32)

    cos = _tile_lanes(cos_ref[...], WIDTH // 128)
    sin = _tile_lanes(sin_ref[...], WIDTH // 128)
    q_ref[0] = _rope(proj(2), cos, sin) * (HEAD_DIM ** -0.5)
    k_ref[0] = _rope(proj(3), cos, sin)
    v_ref[0] = proj(4)

    s_scores, s_self, s_vcols = _sample_scores(sqkv_ref, ck_ref, win)

    u = proj(0)
    vn = _group_rmsnorm(proj(1), ones_ref[...], gv_ref[...]).astype(BF16)

    lane = lax.broadcasted_iota(jnp.int32, (CHUNK, 128), 1)
    left = lane < HEAD_DIM
    row = lax.broadcasted_iota(jnp.int32, (CHUNK, 2 * CHUNK), 0)
    col = lax.broadcasted_iota(jnp.int32, (CHUNK, 2 * CHUNK), 1)
    causal = (col % CHUNK) <= row
    zero = jnp.zeros((CHUNK, 128), BF16)
    wps = [jnp.where(causal, wp_ref[gp], 0.0).astype(BF16) for gp in range(N_HEADS // 2)]
    def block_diag(vv):
        return jnp.concatenate([jnp.where(left, vv, zero), jnp.where(left, zero, vv)], axis=0)

    for c in range(0, tm // CHUNK, 2):
        rows = [slice((c + i) * CHUNK, (c + i + 1) * CHUNK) for i in range(2)]
        mixes = [[], []]
        for gp in range(N_HEADS // 2):
            lanes = slice(gp * 128, (gp + 1) * 128)
            v2 = jnp.concatenate([block_diag(vn[rows[0], lanes]), block_diag(vn[rows[1], lanes])], axis=1)
            both = jnp.dot(wps[gp], v2, preferred_element_type=F32)
            mixes[0].append(both[:, 0:128])
            mixes[1].append(both[:, 128:256])
        for i in range(2):
            mix = jnp.concatenate(mixes[i], axis=1) + bias_ref[...]
            a_ref[0, rows[i], :] = (u[rows[i], :] * mix).astype(a_ref.dtype)

    _sample_finish(s_scores, s_self, s_vcols, cv_ref, so_ref, win)

    @pl.when(pl.program_id(1) >= first_win_tile)
    def _():
        kt_ref[0] = k_ref[0].T
        vt_ref[0] = v_ref[0].T


def _prompt_proj(x, g1, w_in, cos, sin, gv, ones_bd, wp, bias, s_qkv, cache_k_t, cache_v_t, *, tm=512):
    B, S, _ = x.shape
    const2 = lambda b, j: (0, 0)
    out_sds = lambda dt: jax.ShapeDtypeStruct((B, S, WIDTH), dt)
    tile = pl.BlockSpec((1, tm, WIDTH), lambda b, j: (b, j, 0))
    win = min(MAX_WINDOW, S)
    first_win_tile = (S - win) // tm
    tile_t = pl.BlockSpec((1, WIDTH, tm), lambda b, j: (b, 0, jnp.maximum(j - first_win_tile, 0)))
    win_sds = jax.ShapeDtypeStruct((B, WIDTH, win), F32)
    n_seq, _, _, cache_win = cache_k_t.shape
    n_tiles = S // tm
    assert n_seq <= B * n_tiles, "one sample sequence rides on each grid step"
    seq_blk = lambda shape: pl.BlockSpec(
        (1,) + shape, lambda b, j: (jnp.minimum(b * n_tiles + j, n_seq - 1),) + (0,) * len(shape))
    cache_blk = seq_blk((N_HEADS, HEAD_DIM, cache_win))
    return pl.pallas_call(
        functools.partial(_proj_kernel, tm=tm, first_win_tile=first_win_tile, win=cache_win),
        grid=(B, n_tiles),
        in_specs=[
            pl.BlockSpec((1, tm, D_MODEL), lambda b, j: (b, j, 0)),
            pl.BlockSpec((1, D_MODEL), const2),
            pl.BlockSpec((D_MODEL, PROJ_COLS), const2, pipeline_mode=pl.Buffered(1)),
            pl.BlockSpec((tm, 128), lambda b, j: (j, 0)),
            pl.BlockSpec((tm, 128), lambda b, j: (j, 0)),
            pl.BlockSpec((1, WIDTH), const2),
            pl.BlockSpec((WIDTH, WIDTH), const2),
            pl.BlockSpec((N_HEADS // 2, CHUNK, 2 * CHUNK), lambda b, j: (0, 0, 0)),
            pl.BlockSpec((CHUNK, WIDTH), const2),
            seq_blk(s_qkv.shape[1:]), cache_blk, cache_blk,
        ],
        out_specs=[tile, tile, tile, tile, tile_t, tile_t, seq_blk((HEAD_DIM, N_HEADS))],
        out_shape=[out_sds(BF16), out_sds(F32), out_sds(F32), out_sds(F32), win_sds, win_sds,
                   jax.ShapeDtypeStruct((n_seq, HEAD_DIM, N_HEADS), F32)],
        scratch_shapes=[pltpu.VMEM((D_MODEL, PROJ_COLS), BF16)],
        compiler_params=pltpu.CompilerParams(
            dimension_semantics=("arbitrary", "arbitrary"), vmem_limit_bytes=V7X_VMEM_LIMIT),
        name="prompt_proj_sgu",
    )(x, g1, w_in, cos, sin, gv, ones_bd, wp, bias, s_qkv, cache_k_t, cache_v_t)


def _attn_kernel(q_ref, k_ref, v_ref, wg_ref, wu_ref, wd_ref, o_ref, w1_ref, w2_ref,
                 qd, kd, vd, res_o, res_l, nat_o, nat_l, bias, *, seq):
    n_tiles = seq // CHUNK
    last = len(DILATIONS) - 1
    stage_in = (res_o.at[last], res_l.at[last], nat_o.at[last - 1])
    stage_out = (res_o.at[last - 1], res_l.at[last - 1])

    w1_ref[:, :, 0:D_EXPERT] = wg_ref[...].astype(BF16)
    w1_ref[:, :, D_EXPERT:2 * D_EXPERT] = wu_ref[...].astype(BF16)
    w2_ref[...] = wd_ref[...].astype(BF16)

    lane = lax.broadcasted_iota(jnp.int32, (CHUNK, 128), 1)
    left = lane < HEAD_DIM
    qi2 = lax.broadcasted_iota(jnp.int32, (2 * CHUNK, 2 * CHUNK), 0) % CHUNK
    kj2 = lax.broadcasted_iota(jnp.int32, (2 * CHUNK, 2 * CHUNK), 1)
    dist2 = CHUNK + qi2 - kj2
    band2 = (dist2 >= 0) & (dist2 <= N_KEYS)
    zero_q = jnp.zeros((CHUNK, 128), BF16)
    bias[0] = jnp.where(band2, 0.0, NEG)
    bias[1] = jnp.where(band2 & (kj2 >= CHUNK), 0.0, NEG)

    kd[0:CHUNK, :] = jnp.zeros((CHUNK, 128), BF16)
    vd[0:CHUNK, :] = jnp.zeros((CHUNK, 128), BF16)

    for p, dil in enumerate(DILATIONS):
        sub = seq // dil
        nb = sub // CHUNK
        for ti, (src_ref, dst, pad) in enumerate(((q_ref, qd, 0), (k_ref, kd, CHUNK), (v_ref, vd, CHUNK))):
            if dil == 1:
                dst[pad:pad + seq, :] = src_ref[0].astype(BF16)
            elif dil == SUBLANE_STRIDE:
                for r in range(dil):
                    val = src_ref[0, pl.ds(r, sub, stride=dil), :]
                    stage_in[ti][r * sub:(r + 1) * sub, :] = val
                    dst[pad + r * sub:pad + (r + 1) * sub, :] = val.astype(BF16)
            else:
                coarse = seq // SUBLANE_STRIDE
                for r_lo in range(SUBLANE_STRIDE):
                    for r_hi in range(SUBLANE_STRIDE):
                        r = r_lo + SUBLANE_STRIDE * r_hi
                        val = stage_in[ti][pl.ds(r_lo * coarse + r_hi, sub, stride=SUBLANE_STRIDE), :]
                        dst[pad + r * sub:pad + (r + 1) * sub, :] = val.astype(BF16)

        def tile_body(g, i, p=p, nb=nb):
            t = g * TILES_PER_STEP + i
            row = pl.multiple_of(t * CHUNK, CHUNK)
            qt = qd[pl.ds(row, CHUNK), :]
            k2 = kd[pl.ds(row, 2 * CHUNK), :]
            v2 = vd[pl.ds(row, 2 * CHUNK), :]
            if TILES_PER_STEP % nb == 0:
                variant = 1 if i % nb == 0 else 0
            elif i == 0:
                variant = jnp.where((g * TILES_PER_STEP) % nb == 0, 1, 0)
            else:
                variant = 0
            q2 = jnp.concatenate([jnp.where(left, qt, zero_q), jnp.where(left, zero_q, qt)], axis=0)
            s = lax.dot_general(q2, k2, (((1,), (1,)), ((), ())), preferred_element_type=F32)
            s = s + bias[variant]
            m = jnp.max(s, axis=1, keepdims=True)
            e = jnp.exp(s - m)
            den = jnp.sum(e, axis=1, keepdims=True)
            pv = jnp.dot(e.astype(BF16), v2, preferred_element_type=F32) / den
            lse = jnp.broadcast_to(m + jnp.log(den), (2 * CHUNK, 128))
            res_o[p, pl.ds(row, CHUNK), :] = jnp.where(left, pv[0:CHUNK], pv[CHUNK:2 * CHUNK])
            res_l[p, pl.ds(row, CHUNK), :] = jnp.where(left, lse[0:CHUNK], lse[CHUNK:2 * CHUNK])

        def group_body(g, carry, tile_body=tile_body):
            for i in range(TILES_PER_STEP):
                tile_body(g, i)
            return carry

        lax.fori_loop(0, n_tiles // TILES_PER_STEP, group_body, 0)

    for p, dil in enumerate(DILATIONS):
        if dil == 1:
            continue
        sub = seq // dil
        for si, (res, nat) in enumerate(((res_o, nat_o), (res_l, nat_l))):
            if dil == SUBLANE_STRIDE:
                for r in range(dil):
                    nat[p - 1, pl.ds(r, sub, stride=dil), :] = res[p, r * sub:(r + 1) * sub, :]
            else:
                coarse = seq // SUBLANE_STRIDE
                for r_lo in range(SUBLANE_STRIDE):
                    for r_hi in range(SUBLANE_STRIDE):
                        r = r_lo + SUBLANE_STRIDE * r_hi
                        stage_out[si][pl.ds(r_lo * coarse + r_hi, sub, stride=SUBLANE_STRIDE), :] = (
                            res[p, r * sub:(r + 1) * sub, :])
                for r_lo in range(SUBLANE_STRIDE):
                    nat[p - 1, pl.ds(r_lo, coarse, stride=SUBLANE_STRIDE), :] = (
                        stage_out[si][r_lo * coarse:(r_lo + 1) * coarse, :])

    rows_per_step = 256

    def merge_body(c, carry):
        rows = pl.ds(pl.multiple_of(c * rows_per_step, rows_per_step), rows_per_step)
        l0, l1, l2 = res_l[0, rows, :], nat_l[0, rows, :], nat_l[1, rows, :]
        top = jnp.maximum(jnp.maximum(l0, l1), l2)
        w0, w1, w2 = jnp.exp(l0 - top), jnp.exp(l1 - top), jnp.exp(l2 - top)
        num = w0 * res_o[0, rows, :] + w1 * nat_o[0, rows, :] + w2 * nat_o[1, rows, :]
        o_ref[0, rows, :] = (num / (w0 + w1 + w2)).astype(o_ref.dtype)
        return carry

    lax.fori_loop(0, seq // rows_per_step, merge_body, 0)


def _prompt_attention(q, k, v, w_gate, w_up, w_down):
    B, S, _ = q.shape
    n_pairs = WIDTH // 128
    blk = pl.BlockSpec((1, S, 128), lambda b, hp: (b, 0, hp))
    n_steps = B * n_pairs
    assert N_EXPERTS % n_steps == 0
    e_blk = N_EXPERTS // n_steps
    expert_blk = lambda rows, cols: pl.BlockSpec((e_blk, rows, cols), lambda b, hp: (b * n_pairs + hp, 0, 0))
    return pl.pallas_call(
        functools.partial(_attn_kernel, seq=S),
        grid=(B, n_pairs),
        in_specs=[blk, blk, blk,
                  expert_blk(D_MODEL, D_EXPERT), expert_blk(D_MODEL, D_EXPERT), expert_blk(D_EXPERT, D_MODEL)],
        out_specs=[blk, expert_blk(D_MODEL, 2 * D_EXPERT), expert_blk(D_EXPERT, D_MODEL)],
        out_shape=[jax.ShapeDtypeStruct((B, S, WIDTH), BF16),
                   jax.ShapeDtypeStruct((N_EXPERTS, D_MODEL, 2 * D_EXPERT), BF16),
                   jax.ShapeDtypeStruct((N_EXPERTS, D_EXPERT, D_MODEL), BF16)],
        scratch_shapes=[
            pltpu.VMEM((S, 128), BF16),
            pltpu.VMEM((S + CHUNK, 128), BF16),
            pltpu.VMEM((S + CHUNK, 128), BF16),
            pltpu.VMEM((len(DILATIONS), S, 128), F32),
            pltpu.VMEM((len(DILATIONS), S, 128), F32),
            pltpu.VMEM((len(DILATIONS) - 1, S, 128), F32),
            pltpu.VMEM((len(DILATIONS) - 1, S, 128), F32),
            pltpu.VMEM((2, 2 * CHUNK, 2 * CHUNK), F32),
        ],
        compiler_params=pltpu.CompilerParams(
            dimension_semantics=("arbitrary", "arbitrary"), vmem_limit_bytes=V7X_VMEM_LIMIT),
        name="prompt_dilated_attention",
    )(q, k, v, w_gate, w_up, w_down)


def _route(logits):
    lane = lax.broadcasted_iota(jnp.int32, logits.shape, 1)
    big = jnp.int32(ROUTER_LANES)
    lg = jnp.where(lane < N_GROUPS, logits, NEG)
    gmax = jnp.max(lg, axis=1, keepdims=True)
    gp = 1.0 / jnp.sum(jnp.exp(lg - gmax), axis=1, keepdims=True)
    gi = jnp.min(jnp.where(lg == gmax, lane, big), axis=1, keepdims=True)
    lo = N_GROUPS + EXPERTS_PER_GROUP * gi
    le = jnp.where((lane >= lo) & (lane < lo + EXPERTS_PER_GROUP), logits, NEG)
    m1 = jnp.max(le, axis=1, keepdims=True)
    i1 = jnp.min(jnp.where(le == m1, lane, big), axis=1, keepdims=True)
    le2 = jnp.where(lane == i1, NEG, le)
    m2 = jnp.max(le2, axis=1, keepdims=True)
    i2 = jnp.min(jnp.where(le2 == m2, lane, big), axis=1, keepdims=True)
    e2 = jnp.exp(m2 - m1)
    w1 = 1.0 / (1.0 + e2)
    w2 = e2 / (1.0 + e2)
    return jnp.where(lane == i1, gp * w1, jnp.where(lane == i2, gp * w2, 0.0))


def _mix_router_kernel(a_ref, b_ref, x_ref, wo_ref, g2_ref, wr_ref, br_ref, xp_ref, h2_ref, gates_ref, *, precise):
    mixed = (_dot(a_ref[...], wo_ref[0:WIDTH, :], precise)
             + _dot(b_ref[...], wo_ref[WIDTH:2 * WIDTH, :], precise))
    xp = x_ref[...] + mixed
    xp_ref[...] = xp
    h2 = _rmsnorm(xp, g2_ref[...])
    h2_ref[...] = h2.astype(h2_ref.dtype)
    logits = jnp.dot(h2, wr_ref[...], preferred_element_type=F32,
                     precision=lax.Precision.HIGHEST) + br_ref[...]
    gates_ref[...] = _route(logits)


def _mix_router(a, b, x, w_out, g2, w_router, b_router, *, tm, precise=False):
    n = x.shape[0]
    const = lambda i: (0, 0)
    row_blk = lambda w: pl.BlockSpec((tm, w), lambda i: (i, 0))
    return pl.pallas_call(
        functools.partial(_mix_router_kernel, precise=precise),
        grid=(n // tm,),
        in_specs=[row_blk(WIDTH), row_blk(WIDTH), row_blk(D_MODEL),
                  pl.BlockSpec((2 * WIDTH, D_MODEL), const),
                  pl.BlockSpec((1, D_MODEL), const),
                  pl.BlockSpec((D_MODEL, ROUTER_LANES), const),
                  pl.BlockSpec((1, ROUTER_LANES), const)],
        out_specs=[row_blk(D_MODEL), row_blk(D_MODEL), row_blk(ROUTER_LANES)],
        out_shape=[jax.ShapeDtypeStruct((n, D_MODEL), F32),
                   jax.ShapeDtypeStruct((n, D_MODEL), BF16),
                   jax.ShapeDtypeStruct((n, ROUTER_LANES), F32)],
        compiler_params=pltpu.CompilerParams(
            dimension_semantics=("arbitrary",), vmem_limit_bytes=V7X_VMEM_LIMIT),
        name="outproj_router",
    )(a, b, x, w_out, g2, w_router, b_router)


def _nt_dot(w, t):
    return lax.dot_general(w, t, (((1,), (1,)), ((), ())), preferred_element_type=F32)


def _route_t(logits_t):
    row = lax.broadcasted_iota(jnp.int32, logits_t.shape, 0)
    big = jnp.int32(ROUTER_LANES)
    lg = jnp.where(row < N_GROUPS, logits_t, NEG)
    gmax = jnp.max(lg, axis=0, keepdims=True)
    gp = 1.0 / jnp.sum(jnp.exp(lg - gmax), axis=0, keepdims=True)
    gi = jnp.min(jnp.where(lg == gmax, row, big), axis=0, keepdims=True)
    lo = N_GROUPS + EXPERTS_PER_GROUP * gi
    le = jnp.where((row >= lo) & (row < lo + EXPERTS_PER_GROUP), logits_t, NEG)
    m1 = jnp.max(le, axis=0, keepdims=True)
    i1 = jnp.min(jnp.where(le == m1, row, big), axis=0, keepdims=True)
    le2 = jnp.where(row == i1, NEG, le)
    m2 = jnp.max(le2, axis=0, keepdims=True)
    i2 = jnp.min(jnp.where(le2 == m2, row, big), axis=0, keepdims=True)
    e2 = jnp.exp(m2 - m1)
    return i1 - N_GROUPS, i2 - N_GROUPS, gp / (1.0 + e2), gp * e2 / (1.0 + e2)


def _mix_route_sort_kernel(a_ref, b_ref, x_ref, wo_ref, g2_ref, wr_ref, brc_ref, tri_ref, ltri_ref,
                           xp_ref, h2_ref, mrow_ref, mcol_ref, tab_ref, wob):
    t = MOE_BLOCK

    @pl.when(pl.program_id(0) == 0)
    def _():
        wob[...] = wo_ref[...].astype(BF16)

    def project(j):
        rows = slice(j * t, (j + 1) * t)
        xp = x_ref[rows, :] + _dot(a_ref[rows, :], wob[0:WIDTH, :]) + _dot(b_ref[rows, :], wob[WIDTH:2 * WIDTH, :])
        xp_ref[rows, :] = xp
        h2 = _rmsnorm(xp, g2_ref[...])
        hi = h2.astype(BF16)
        h2_ref[rows, :] = hi
        return hi, (h2 - hi.astype(F32)).astype(BF16)

    def route(hi, lo):
        prod_hi = _nt_dot(wr_ref[...], hi)
        logits_t = (prod_hi[0:ROUTER_ROWS] + prod_hi[ROUTER_ROWS:2 * ROUTER_ROWS]
                    + _nt_dot(wr_ref[0:ROUTER_ROWS, :], lo) + brc_ref[...])
        return _route_t(logits_t)

    def sort_meta(j, ex1, ex2, gate1, gate2):
        rows = slice(j * t, (j + 1) * t)
        pair_e = jnp.concatenate([ex1, ex2], axis=1)
        row = lax.broadcasted_iota(jnp.int32, (N_EXPERTS, 2 * t), 0)
        onehot = jnp.where(row == pair_e, 1.0, 0.0)
        n_lane_tiles = 2 * t // 128
        local = _dot(jnp.concatenate([onehot[:, k * 128:(k + 1) * 128] for k in range(n_lane_tiles)], axis=0),
                     tri_ref[...])
        carry = jnp.zeros((N_EXPERTS, 1), F32)
        cums = []
        for k in range(n_lane_tiles):
            tile = local[k * N_EXPERTS:(k + 1) * N_EXPERTS, :]
            cums.append(tile + carry)
            carry = carry + tile[:, 127:128]
        cum = jnp.concatenate(cums, axis=1)
        rank = jnp.sum(onehot * cum, axis=0, keepdims=True) - 1.0
        counts = carry
        units32 = jnp.floor((counts + (MOE_ROW_ALIGN - 1)) * (1.0 / MOE_ROW_ALIGN))
        units = jnp.concatenate([jnp.broadcast_to(units32, (N_EXPERTS, 128)),
                                 jnp.zeros((ROUTER_LANES - N_EXPERTS, 128), F32)], axis=0)
        off = _dot(ltri_ref[...], units) * MOE_ROW_ALIGN
        dst = jnp.sum(onehot * off[0:N_EXPERTS, 0:1], axis=0, keepdims=True) + rank

        r8 = lax.broadcasted_iota(jnp.int32, (8, t), 0)
        mrow_ref[j] = jnp.where(r8 == 0, dst[:, 0:t], jnp.where(r8 == 1, dst[:, t:2 * t],
                                jnp.where(r8 == 2, gate1, jnp.where(r8 == 3, gate2, 0.0))))
        r128 = lax.broadcasted_iota(jnp.int32, (ROUTER_LANES, t), 0)
        meta = jnp.where(r128 == 0, dst[:, 0:t], jnp.where(r128 == 1, dst[:, t:2 * t],
                         jnp.where(r128 == 2, gate1, jnp.where(r128 == 3, gate2, 0.0))))
        mcol_ref[rows, :] = meta.T
        lane = lax.broadcasted_iota(jnp.int32, (ROUTER_LANES, 128), 1)
        n_rows = units * MOE_ROW_ALIGN
        chunks = jnp.floor((n_rows + (MOE_CHUNK - 1)) * (1.0 / MOE_CHUNK))
        tab_ref[j] = jnp.where(lane == 0, off, jnp.where(lane == 1, chunks, jnp.where(lane == 2, off + n_rows, 0.0)))

    blocks = range(MOE_ROUTE_BLOCKS)
    projected = [project(j) for j in blocks]
    routed = [route(hi, lo) for hi, lo in projected]
    for j in blocks:
        sort_meta(j, *routed[j])


def _mix_route_sort(a, b, x, w_out, g2, wr_hl, br_col, tri, ltri):
    n = x.shape[0]
    t = MOE_BLOCK
    nblk = n // t
    g = MOE_ROUTE_BLOCKS
    assert nblk % g == 0
    const = lambda i: (0, 0)
    row_blk = lambda w: pl.BlockSpec((g * t, w), lambda i: (i, 0))
    return pl.pallas_call(
        _mix_route_sort_kernel,
        grid=(nblk // g,),
        in_specs=[row_blk(WIDTH), row_blk(WIDTH), row_blk(D_MODEL),
                  pl.BlockSpec((2 * WIDTH, D_MODEL), const, pipeline_mode=pl.Buffered(1)),
                  pl.BlockSpec((1, D_MODEL), const),
                  pl.BlockSpec((2 * ROUTER_ROWS, D_MODEL), const),
                  pl.BlockSpec((ROUTER_ROWS, 1), const),
                  pl.BlockSpec((ROUTER_LANES, ROUTER_LANES), const),
                  pl.BlockSpec((ROUTER_LANES, ROUTER_LANES), const)],
        out_specs=[row_blk(D_MODEL), row_blk(D_MODEL),
                   pl.BlockSpec((g, 8, t), lambda i: (i, 0, 0)),
                   row_blk(ROUTER_LANES),
                   pl.BlockSpec((g, ROUTER_LANES, 128), lambda i: (i, 0, 0))],
        out_shape=[jax.ShapeDtypeStruct((n, D_MODEL), F32),
                   jax.ShapeDtypeStruct((n, D_MODEL), BF16),
                   jax.ShapeDtypeStruct((nblk, 8, t), F32),
                   jax.ShapeDtypeStruct((n, ROUTER_LANES), F32),
                   jax.ShapeDtypeStruct((nblk, ROUTER_LANES, 128), F32)],
        scratch_shapes=[pltpu.VMEM((2 * WIDTH, D_MODEL), BF16)],
        compiler_params=pltpu.CompilerParams(
            dimension_semantics=("arbitrary",), vmem_limit_bytes=V7X_VMEM_LIMIT),
        name="outproj_route_sort",
    )(a, b, x, w_out, g2, wr_hl, br_col, tri, ltri)


def _silu_mul(ab):
    a = ab[:, :D_EXPERT]
    return a * (1.0 / (1.0 + jnp.exp(-a))) * ab[:, D_EXPERT:]


def _chunk_offsets(tab, n_blk, e):
    return [pl.multiple_of(tab(j, e, 0), MOE_ROW_ALIGN) for j in range(n_blk)]


def _moe_gate_up_kernel(tab_ref, h_ref, mrow_ref, w1_ref, sh_ref, sg_ref, hs_ref, shid_ref, xs, *, flags_at):
    t = MOE_BLOCK
    n_blk = MOE_GATE_UP_BLOCKS
    first = pl.program_id(0) * n_blk
    tab = lambda j, e, c: tab_ref[((first + j) * N_EXPERTS + e) * 3 + c]

    piece = 512
    for j in range(n_blk):
        mrow = mrow_ref[j]
        dst1 = mrow[0:1, :].astype(jnp.int32)
        dst2 = mrow[1:2, :].astype(jnp.int32)
        h = h_ref[j * t:(j + 1) * t, :]
        for r0 in range(0, MOE_SORT_ROWS, piece):
            d_idx = lax.broadcasted_iota(jnp.int32, (piece, t), 0) + r0
            sel = jnp.where((d_idx == dst1) | (d_idx == dst2), 1.0, 0.0)
            xs[j, r0:r0 + piece, :] = _dot(sel, h).astype(BF16)
        xs[j, MOE_SORT_ROWS:MOE_ROWS, :] = jnp.zeros((MOE_CHUNK, D_MODEL), BF16)
    hs_ref[...] = jnp.zeros(hs_ref.shape, BF16)

    def first_chunks(g, carry):
        for i in range(MOE_EXPERTS_PER_STEP):
            e = g * MOE_EXPERTS_PER_STEP + i
            offs = _chunk_offsets(tab, n_blk, e)
            x = jnp.concatenate([xs[j, pl.ds(offs[j], MOE_CHUNK), :] for j in range(n_blk)], axis=0)
            hid = _silu_mul(jnp.dot(x, w1_ref[e], preferred_element_type=F32)).astype(BF16)
            for j in range(n_blk):
                hs_ref[j, pl.ds(offs[j], MOE_CHUNK), :] = hid[j * MOE_CHUNK:(j + 1) * MOE_CHUNK, :]
        return carry

    lax.fori_loop(0, N_EXPERTS // MOE_EXPERTS_PER_STEP, first_chunks, 0)

    def more_chunks(j, e, carry):
        off, n_chunks, end = tab(j, e, 0), tab(j, e, 1), tab(j, e, 2)

        def chunk(c, carry):
            r0 = pl.multiple_of(off + c * MOE_CHUNK, MOE_ROW_ALIGN)
            rows = r0 + lax.broadcasted_iota(jnp.int32, (MOE_CHUNK, D_EXPERT), 0)
            hid = _silu_mul(jnp.dot(xs[j, pl.ds(r0, MOE_CHUNK), :], w1_ref[e], preferred_element_type=F32))
            hs_ref[j, pl.ds(r0, MOE_CHUNK), :] = jnp.where(rows < end, hid.astype(BF16),
                                                           hs_ref[j, pl.ds(r0, MOE_CHUNK), :])
            return carry

        return lax.fori_loop(1, n_chunks, chunk, carry)

    for j in range(n_blk):
        @pl.when(tab_ref[flags_at + first + j] > 0)
        def _(j=j):
            lax.fori_loop(0, N_EXPERTS, functools.partial(more_chunks, j), 0)

    @pl.when(pl.program_id(0) == 0)
    def _():
        sh = sh_ref[...]
        gates = sg_ref[...]
        for e in range(N_EXPERTS):
            gate = gates[:, N_GROUPS + e:N_GROUPS + e + 1]
            hid = _silu_mul(jnp.dot(sh, w1_ref[e], preferred_element_type=F32)) * gate
            shid_ref[:, e * D_EXPERT:(e + 1) * D_EXPERT] = hid.astype(BF16)


def _moe_gate_up(tab, h2, mrow, w1_b, s_h, s_gates):
    n = h2.shape[0]
    n_s = s_h.shape[0]
    nblk = n // MOE_BLOCK
    g = MOE_GATE_UP_BLOCKS
    assert nblk % g == 0
    whole = lambda shape: pl.BlockSpec(shape, lambda i, tab: (0,) * len(shape))
    return pl.pallas_call(
        functools.partial(_moe_gate_up_kernel, flags_at=nblk * N_EXPERTS * 3),
        grid_spec=pltpu.PrefetchScalarGridSpec(
            num_scalar_prefetch=1,
            grid=(nblk // g,),
            in_specs=[pl.BlockSpec((g * MOE_BLOCK, D_MODEL), lambda i, tab: (i, 0)),
                      pl.BlockSpec((g, 8, MOE_BLOCK), lambda i, tab: (i, 0, 0)),
                      pl.BlockSpec(w1_b.shape, lambda i, tab: (0, 0, 0), pipeline_mode=pl.Buffered(1)),
                      whole(s_h.shape), whole(s_gates.shape)],
            out_specs=[pl.BlockSpec((g, MOE_ROWS, D_EXPERT), lambda i, tab: (i, 0, 0)),
                       whole((n_s, N_EXPERTS * D_EXPERT))],
            scratch_shapes=[pltpu.VMEM((g, MOE_ROWS, D_MODEL), BF16)]),
        out_shape=[jax.ShapeDtypeStruct((nblk, MOE_ROWS, D_EXPERT), BF16),
                   jax.ShapeDtypeStruct((n_s, N_EXPERTS * D_EXPERT), BF16)],
        compiler_params=pltpu.CompilerParams(
            dimension_semantics=("arbitrary",), vmem_limit_bytes=V7X_VMEM_LIMIT),
        name="moe_gate_up",
    )(tab, h2, mrow, w1_b, s_h, s_gates)


def _moe_down_kernel(tab_ref, hs_ref, xp_ref, mcol_ref, w2_ref, gf_ref, shid_ref, sx_ref, y_ref, sy_ref, os, *,
                     flags_at):
    t = MOE_BLOCK
    n_blk = MOE_DOWN_BLOCKS
    first = pl.program_id(0) * n_blk
    tab = lambda j, e, c: tab_ref[((first + j) * N_EXPERTS + e) * 3 + c]
    os[...] = jnp.zeros(os.shape, BF16)

    def first_chunks(g, carry):
        for i in range(MOE_EXPERTS_PER_STEP):
            e = g * MOE_EXPERTS_PER_STEP + i
            offs = _chunk_offsets(tab, n_blk, e)
            hid = jnp.concatenate([hs_ref[j, pl.ds(offs[j], MOE_CHUNK), :] for j in range(n_blk)], axis=0)
            out = jnp.dot(hid, w2_ref[e], preferred_element_type=F32).astype(BF16)
            for j in range(n_blk):
                os[j, pl.ds(offs[j], MOE_CHUNK), :] = out[j * MOE_CHUNK:(j + 1) * MOE_CHUNK, :]
        return carry

    lax.fori_loop(0, N_EXPERTS // MOE_EXPERTS_PER_STEP, first_chunks, 0)

    def more_chunks(j, e, carry):
        off, n_chunks, end = tab(j, e, 0), tab(j, e, 1), tab(j, e, 2)

        def chunk(c, carry):
            r0 = pl.multiple_of(off + c * MOE_CHUNK, MOE_ROW_ALIGN)
            rows = r0 + lax.broadcasted_iota(jnp.int32, (MOE_CHUNK, D_MODEL), 0)
            out = jnp.dot(hs_ref[j, pl.ds(r0, MOE_CHUNK), :], w2_ref[e], preferred_element_type=F32)
            os[j, pl.ds(r0, MOE_CHUNK), :] = jnp.where(rows < end, out.astype(BF16), os[j, pl.ds(r0, MOE_CHUNK), :])
            return carry

        return lax.fori_loop(1, n_chunks, chunk, carry)

    for j in range(n_blk):
        @pl.when(tab_ref[flags_at + first + j] > 0)
        def _(j=j):
            lax.fori_loop(0, N_EXPERTS, functools.partial(more_chunks, j), 0)

    l_idx = lax.broadcasted_iota(jnp.int32, (t, MOE_SORT_ROWS), 1)

    def scatter_matrix(j):
        mcol = mcol_ref[j * t:(j + 1) * t, :]
        d1c = mcol[:, 0:1].astype(jnp.int32)
        d2c = mcol[:, 1:2].astype(jnp.int32)
        comb = jnp.where(l_idx == d1c, mcol[:, 2:3], 0.0) + jnp.where(l_idx == d2c, mcol[:, 3:4], 0.0)
        return comb.astype(BF16)

    combs = [scatter_matrix(j) for j in range(n_blk)]
    moes = [_dot(combs[j], os[j, 0:MOE_SORT_ROWS, :]) for j in range(n_blk)]
    for j in range(n_blk):
        rows = slice(j * t, (j + 1) * t)
        y_ref[rows, :] = _rmsnorm(xp_ref[rows, :] + moes[j], gf_ref[...])

    @pl.when(pl.program_id(0) == 0)
    def _():
        w2_all = w2_ref[...].reshape(N_EXPERTS * D_EXPERT, D_MODEL)
        ys = sx_ref[...] + jnp.dot(shid_ref[...], w2_all, preferred_element_type=F32)
        sy_ref[...] = _rmsnorm(ys, gf_ref[...])


def _moe_down(tab, hs_sorted, xp, mcol, w2_b, gf, s_hid, s_x):
    n = xp.shape[0]
    n_s = s_x.shape[0]
    nblk = n // MOE_BLOCK
    g = MOE_DOWN_BLOCKS
    assert nblk % g == 0
    whole = lambda shape: pl.BlockSpec(shape, lambda i, tab: (0,) * len(shape))
    row_blk = lambda w: pl.BlockSpec((g * MOE_BLOCK, w), lambda i, tab: (i, 0))
    return pl.pallas_call(
        functools.partial(_moe_down_kernel, flags_at=nblk * N_EXPERTS * 3),
        grid_spec=pltpu.PrefetchScalarGridSpec(
            num_scalar_prefetch=1,
            grid=(nblk // g,),
            in_specs=[pl.BlockSpec((g, MOE_ROWS, D_EXPERT), lambda i, tab: (i, 0, 0)),
                      row_blk(D_MODEL), row_blk(ROUTER_LANES),
                      pl.BlockSpec(w2_b.shape, lambda i, tab: (0, 0, 0), pipeline_mode=pl.Buffered(1)),
                      whole((1, D_MODEL)), whole(s_hid.shape), whole(s_x.shape)],
            out_specs=[row_blk(D_MODEL), whole((n_s, D_MODEL))],
            scratch_shapes=[pltpu.VMEM((g, MOE_ROWS, D_MODEL), BF16)]),
        out_shape=[jax.ShapeDtypeStruct((n, D_MODEL), F32), jax.ShapeDtypeStruct((n_s, D_MODEL), F32)],
        compiler_params=pltpu.CompilerParams(
            dimension_semantics=("arbitrary",), vmem_limit_bytes=V7X_VMEM_LIMIT),
        name="moe_down_combine",
    )(tab, hs_sorted, xp, mcol, w2_b, gf, s_hid, s_x)


def _sample_proj_kernel(x_ref, g1_ref, w_ref, cos_ref, sin_ref, gv_ref, ones_ref, w00_ref, b0_ref,
                        rep_ref, foldt_ref, a_ref, k_ref, v_ref, vn_ref, qkvt_ref):
    h = _rmsnorm(x_ref[...], g1_ref[...])

    def proj(i):
        return _dot(h, w_ref[:, i * WIDTH:(i + 1) * WIDTH], precise=True)

    cos = _tile_lanes(cos_ref[...], WIDTH // 128)
    sin = _tile_lanes(sin_ref[...], WIDTH // 128)
    q = _rope(proj(2), cos, sin) * (HEAD_DIM ** -0.5)
    k = _rope(proj(3), cos, sin)
    v = proj(4)
    vn = _group_rmsnorm(proj(1), ones_ref[...], gv_ref[...], precise=True)
    a_ref[...] = proj(0) * (w00_ref[...] * vn + b0_ref[...])
    k_ref[...] = k
    v_ref[...] = v
    vn_ref[...] = vn

    n_rep = rep_ref.shape[0]
    r_idx = lax.broadcasted_iota(jnp.int32, (n_rep, WIDTH), 0)
    l_idx = lax.broadcasted_iota(jnp.int32, (n_rep, WIDTH), 1)
    own = (l_idx // HEAD_DIM) == (r_idx % N_HEADS)
    for t, src in enumerate((q, k, v)):
        rep = _dot(rep_ref[...], src, precise=True)
        qkvt_ref[t * HEAD_DIM:(t + 1) * HEAD_DIM, :] = lax.dot_general(
            foldt_ref[...], jnp.where(own, rep, 0.0), (((1,), (1,)), ((), ())),
            preferred_element_type=F32, precision=lax.Precision.HIGHEST)


def _sample_proj(x, g1, w_in, cos, sin, gv, ones_bd, w00, b0, rep, foldt):
    bd = x.shape[0]
    sds = lambda r, c: jax.ShapeDtypeStruct((r, c), F32)
    return pl.pallas_call(
        _sample_proj_kernel,
        out_shape=[sds(bd, WIDTH), sds(bd, WIDTH), sds(bd, WIDTH), sds(bd, WIDTH),
                   sds(3 * HEAD_DIM, bd * N_HEADS)],
        compiler_params=pltpu.CompilerParams(vmem_limit_bytes=V7X_VMEM_LIMIT),
        name="sample_proj",
    )(x, g1, w_in, cos, sin, gv, ones_bd, w00, b0, rep, foldt)


def _sample_scores(qkv_ref, k_ref, win):
    cols = qkv_ref[0]

    head = lax.broadcasted_iota(jnp.int32, (N_HEADS, win), 0)
    s = jnp.zeros((N_HEADS, win), F32)
    s_self = jnp.zeros((N_HEADS, 1), F32)
    for h in range(N_HEADS):
        qc = cols[0:HEAD_DIM, h:h + 1]
        kc = cols[HEAD_DIM:2 * HEAD_DIM, h:h + 1]
        s_h = jnp.sum(k_ref[0, h] * qc, axis=0, keepdims=True)
        s = jnp.where(head == h, s_h, s)
        s_self = jnp.where(head[:, 0:1] == h, jnp.sum(qc * kc, axis=0, keepdims=True), s_self)
    return s, s_self, cols[2 * HEAD_DIM:3 * HEAD_DIM, :]


def _sample_finish(s, s_self, v_cols, v_ref, o_ref, win):
    dist = win - lax.broadcasted_iota(jnp.int32, (1, win), 1)
    members = [(dist <= N_KEYS * dil) & (dist % dil == 0) for dil in DILATIONS]
    es, e_selfs, dens, lses = [], [], [], []
    for mem in members:
        sm = jnp.where(mem, s, NEG)
        m = jnp.maximum(jnp.max(sm, axis=1, keepdims=True), s_self)
        e = jnp.exp(sm - m)
        e_self = jnp.exp(s_self - m)
        den = jnp.sum(e, axis=1, keepdims=True) + e_self
        es.append(e)
        e_selfs.append(e_self)
        dens.append(den)
        lses.append(m + jnp.log(den))
    top = jnp.maximum(jnp.maximum(lses[0], lses[1]), lses[2])
    ws = [jnp.exp(l - top) for l in lses]
    wsum = ws[0] + ws[1] + ws[2]
    coef = [w / (den * wsum) for w, den in zip(ws, dens)]
    p_keys = coef[0] * es[0] + coef[1] * es[1] + coef[2] * es[2]
    p_self = coef[0] * e_selfs[0] + coef[1] * e_selfs[1] + coef[2] * e_selfs[2]

    for h in range(N_HEADS):
        o_ref[0, :, h:h + 1] = (jnp.sum(v_ref[0, h] * p_keys[h:h + 1, :], axis=1, keepdims=True)
                                + p_self[h:h + 1, :] * v_cols[:, h:h + 1])


def _rope_tables(first_pos, count):
    half = HEAD_DIM // 2
    inv = ROPE_THETA ** (-np.arange(half, dtype=np.float64) * 2.0 / HEAD_DIM)
    ang = (first_pos + np.arange(count, dtype=np.float64))[:, None] * inv[None, :]
    cos, sin = np.cos(ang), np.sin(ang)
    cos128 = np.concatenate([cos, cos, cos, cos], axis=1).astype(np.float32)
    sin128 = np.concatenate([-sin, sin, -sin, sin], axis=1).astype(np.float32)
    return jnp.asarray(cos128), jnp.asarray(sin128)


def kernel(x_prompt, x_sample, cache_win_k, cache_win_v, ln1_g, w_in, sgu_norm_g, sgu_w, sgu_b, w_out, ln2_g,
           w_router_group, b_router_group, w_router_expert, b_router_expert, w_gate, w_up, w_down, lnf_g):
    depth = w_in.shape[0]
    assert depth == 1 and x_sample.shape[1] == 1
    B, S, _ = x_prompt.shape
    bd = x_sample.shape[0]
    win = cache_win_k.shape[2]
    assert S % (max(DILATIONS) * CHUNK) == 0 and win >= max(DILATIONS) * N_KEYS and PAST_LEN % CHUNK == 0
    l = 0

    pad = ROUTER_LANES - N_GROUPS - N_EXPERTS
    w_router = jnp.pad(jnp.concatenate([w_router_group[l], w_router_expert[l]], axis=1), ((0, 0), (0, pad)))
    b_router = jnp.pad(jnp.concatenate([b_router_group[l], b_router_expert[l]]), (0, pad))[None, :]
    g1 = ln1_g[l][None, :]
    g2 = ln2_g[l][None, :]
    gf = lnf_g[None, :]
    gv = sgu_norm_g[l].reshape(1, WIDTH)
    grp = np.arange(WIDTH) // HEAD_DIM
    ones_bd = jnp.asarray(np.where(grp[:, None] == grp[None, :], 1.0 / HEAD_DIM, 0.0), BF16)
    wp = jnp.concatenate([sgu_w[l][0::2], sgu_w[l][1::2]], axis=-1)
    bias = jnp.repeat(sgu_b[l].T, HEAD_DIM, axis=1)
    w00 = jnp.repeat(sgu_w[l][:, 0, 0], HEAD_DIM)[None, :]
    b0 = jnp.repeat(sgu_b[l][:, 0], HEAD_DIM)[None, :]

    cos_s, sin_s = _rope_tables(PAST_LEN, 1)
    rep = jnp.asarray(np.arange(bd * N_HEADS)[:, None] // N_HEADS == np.arange(bd)[None, :], F32)
    foldt = jnp.asarray(np.arange(HEAD_DIM)[:, None] == np.arange(WIDTH)[None, :] % HEAD_DIM, F32)
    xs = x_sample.reshape(bd, D_MODEL)
    a_s, k_s, v_s, vn_s, qkvt = _sample_proj(xs, g1, w_in[l], cos_s, sin_s, gv, ones_bd, w00, b0, rep, foldt)
    to_pos_minor = lambda c: jnp.transpose(c, (0, 2, 3, 1))

    cos_p, sin_p = _rope_tables(0, S)
    qkv_seq = jnp.transpose(qkvt.reshape(3 * HEAD_DIM, bd, N_HEADS), (1, 0, 2))
    a_p, q_p, k_p, v_p, kt_p, vt_p, o3 = _prompt_proj(
        x_prompt, g1, w_in[l], cos_p, sin_p, gv, ones_bd, wp, bias,
        qkv_seq, to_pos_minor(cache_win_k[l]), to_pos_minor(cache_win_v[l]))
    b_p, w1_b, w2_b = _prompt_attention(q_p, k_p, v_p, w_gate[l], w_up[l], w_down[l])
    n = B * S
    assert n % MOE_BLOCK == 0
    wr_t = w_router.T[:ROUTER_ROWS]
    wr_hi = wr_t.astype(BF16)
    wr_hl = jnp.concatenate([wr_hi, (wr_t - wr_hi.astype(F32)).astype(BF16)], axis=0)
    lane_idx = np.arange(ROUTER_LANES)
    tri = jnp.asarray(lane_idx[:, None] <= lane_idx[None, :], BF16)
    ltri = jnp.asarray(lane_idx[None, :] < lane_idx[:, None], BF16)
    xp2, h2, mrow, mcol, tab_f = _mix_route_sort(
        a_p.reshape(n, WIDTH), b_p.reshape(n, WIDTH), x_prompt.reshape(n, D_MODEL),
        w_out[l], g2, wr_hl, b_router.reshape(ROUTER_LANES, 1)[:ROUTER_ROWS], tri, ltri)
    tab_i = tab_f[:, :N_EXPERTS, 0:3].astype(jnp.int32)
    multi_chunk = (jnp.max(tab_i[:, :, 1], axis=1) > 1).astype(jnp.int32)
    tab = jnp.concatenate([tab_i.reshape(-1), multi_chunk])

    b_s = jnp.transpose(o3, (0, 2, 1)).reshape(bd, WIDTH)
    xs2, hs2, gates_s = _mix_router(a_s, b_s, xs, w_out[l], g2, w_router, b_router, tm=bd, precise=True)

    hid_sorted, hid_s = _moe_gate_up(tab, h2, mrow, w1_b, hs2, gates_s)
    y_prompt, y_sample = _moe_down(tab, hid_sorted, xp2, mcol, w2_b, gf, hid_s, xs2)
    y_prompt = y_prompt.reshape(B, S, D_MODEL)
    y_sample = y_sample.reshape(bd, 1, D_MODEL)
    buf_p = min(MAX_WINDOW, S)
    to_win = lambda t: jnp.transpose(t.reshape(1, B, N_HEADS, HEAD_DIM, buf_p), (0, 1, 4, 2, 3))
    new_k_p = to_win(kt_p)
    new_v_p = to_win(vt_p)

    shape_s = (1, bd, 1, N_HEADS, HEAD_DIM)
    return (y_prompt, y_sample, new_k_p, new_v_p,
            k_s.reshape(shape_s), v_s.reshape(shape_s), vn_s.reshape(shape_s))
```

```python
import functools

import jax
import jax.numpy as jnp
import numpy as np
from jax import lax
from jax.experimental import pallas as pl
from jax.experimental.pallas import tpu as pltpu

F32 = jnp.float32
BF16 = jnp.bfloat16

D_MODEL = 1024
HEAD_DIM = 64
N_HEADS = 8
WIDTH = N_HEADS * HEAD_DIM
PROJ_COLS = 5 * WIDTH
CHUNK = 128
DILATIONS = (1, 4, 16)
N_KEYS = 128
MAX_WINDOW = 2048
PAST_LEN = 16384
ROPE_THETA = 10000.0
N_GROUPS = 4
EXPERTS_PER_GROUP = 8
N_EXPERTS = N_GROUPS * EXPERTS_PER_GROUP
D_EXPERT = 128
EPS = 1e-6
NEG = -1e30
TILES_PER_STEP = 32
SUBLANE_STRIDE = 4
assert DILATIONS == (1, SUBLANE_STRIDE, SUBLANE_STRIDE ** 2)
MOE_BLOCK = 512
MOE_ROW_ALIGN = 16
MOE_CHUNK = 48
MOE_EXPERTS_PER_STEP = 16
MOE_ROUTE_BLOCKS = 2
MOE_GATE_UP_BLOCKS = 4
MOE_DOWN_BLOCKS = 2
MOE_SORT_ROWS = -(-(2 * MOE_BLOCK + N_EXPERTS * (MOE_ROW_ALIGN - 1)) // 512) * 512
MOE_ROWS = MOE_SORT_ROWS + MOE_CHUNK
MOE_GATHER_ROWS = 2 * MOE_BLOCK + N_EXPERTS * (MOE_ROW_ALIGN - 1) * 2 // 3
assert MOE_GATHER_ROWS % MOE_ROW_ALIGN == 0 and MOE_GATHER_ROWS < MOE_SORT_ROWS
ROUTER_ROWS = 48
ROUTER_LANES = 128
V7X_VMEM_LIMIT = 56 * 1024 * 1024


def _rmsnorm(x, g):
    return x * lax.rsqrt(jnp.mean(x * x, axis=-1, keepdims=True) + EPS) * g


def _tile_lanes(t, reps):
    return jnp.concatenate([t] * reps, axis=1)


def _rope(t, cos, sin_signed):
    lane = lax.broadcasted_iota(jnp.int32, t.shape, 1)
    first_half = (lane % HEAD_DIM) < (HEAD_DIM // 2)
    n = t.shape[1]
    partner = jnp.where(first_half, pltpu.roll(t, n - HEAD_DIM // 2, 1), pltpu.roll(t, HEAD_DIM // 2, 1))
    return t * cos + partner * sin_signed


def _dot(a, b, precise=False):
    if precise:
        return jnp.dot(a.astype(F32), b.astype(F32), preferred_element_type=F32,
                       precision=lax.Precision.HIGHEST)
    return jnp.dot(a.astype(BF16), b.astype(BF16), preferred_element_type=F32)


def _group_rmsnorm(va, ones_bd, gv, precise=False):
    ms = _dot(va * va, ones_bd, precise)
    return va * lax.rsqrt(ms + EPS) * gv


def _proj_kernel(x_ref, g1_ref, w_ref, cos_ref, sin_ref, gv_ref, ones_ref, wp_ref, bias_ref,
                 sqkv_ref, ck_ref, cv_ref,
                 a_ref, q_ref, k_ref, v_ref, kt_ref, vt_ref, so_ref, wb, *, tm, first_win_tile, win):
    @pl.when((pl.program_id(0) == 0) & (pl.program_id(1) == 0))
    def _():
        wb[...] = w_ref[...].astype(BF16)

    h = _rmsnorm(x_ref[0], g1_ref[...]).astype(BF16)

    def proj(i):
        return jnp.dot(h, wb[:, i * WIDTH:(i + 1) * WIDTH], preferred_element_type=F32)

    cos = _tile_lanes(cos_ref[...], WIDTH // 128)
    sin = _tile_lanes(sin_ref[...], WIDTH // 128)
    q_ref[0] = _rope(proj(2), cos, sin) * (HEAD_DIM ** -0.5)
    k_ref[0] = _rope(proj(3), cos, sin)
    v_ref[0] = proj(4)

    s_scores, s_self, s_vcols = _sample_scores(sqkv_ref, ck_ref, win)

    u = proj(0)
    vn = _group_rmsnorm(proj(1), ones_ref[...], gv_ref[...]).astype(BF16)

    lane = lax.broadcasted_iota(jnp.int32, (CHUNK, 128), 1)
    left = lane < HEAD_DIM
    row = lax.broadcasted_iota(jnp.int32, (CHUNK, 2 * CHUNK), 0)
    col = lax.broadcasted_iota(jnp.int32, (CHUNK, 2 * CHUNK), 1)
    causal = (col % CHUNK) <= row
    zero = jnp.zeros((CHUNK, 128), BF16)
    wps = [jnp.where(causal, wp_ref[gp], 0.0).astype(BF16) for gp in range(N_HEADS // 2)]
    def block_diag(vv):
        return jnp.concatenate([jnp.where(left, vv, zero), jnp.where(left, zero, vv)], axis=0)

    for c in range(0, tm // CHUNK, 2):
        rows = [slice((c + i) * CHUNK, (c + i + 1) * CHUNK) for i in range(2)]
        mixes = [[], []]
        for gp in range(N_HEADS // 2):
            lanes = slice(gp * 128, (gp + 1) * 128)
            v2 = jnp.concatenate([block_diag(vn[rows[0], lanes]), block_diag(vn[rows[1], lanes])], axis=1)
            both = jnp.dot(wps[gp], v2, preferred_element_type=F32)
            mixes[0].append(both[:, 0:128])
            mixes[1].append(both[:, 128:256])
        for i in range(2):
            mix = jnp.concatenate(mixes[i], axis=1) + bias_ref[...]
            a_ref[0, rows[i], :] = (u[rows[i], :] * mix).astype(a_ref.dtype)

    _sample_finish(s_scores, s_self, s_vcols, cv_ref, so_ref, win)

    @pl.when(pl.program_id(1) >= first_win_tile)
    def _():
        kt_ref[0] = k_ref[0].T
        vt_ref[0] = v_ref[0].T


def _prompt_proj(x, g1, w_in, cos, sin, gv, ones_bd, wp, bias, s_qkv, cache_k_t, cache_v_t, *, tm=512):
    B, S, _ = x.shape
    const2 = lambda b, j: (0, 0)
    out_sds = lambda dt: jax.ShapeDtypeStruct((B, S, WIDTH), dt)
    tile = pl.BlockSpec((1, tm, WIDTH), lambda b, j: (b, j, 0))
    win = min(MAX_WINDOW, S)
    first_win_tile = (S - win) // tm
    tile_t = pl.BlockSpec((1, WIDTH, tm), lambda b, j: (b, 0, jnp.maximum(j - first_win_tile, 0)))
    win_sds = jax.ShapeDtypeStruct((B, WIDTH, win), F32)
    n_seq, _, _, cache_win = cache_k_t.shape
    n_tiles = S // tm
    assert n_seq <= B * n_tiles, "one sample sequence rides on each grid step"
    seq_blk = lambda shape: pl.BlockSpec(
        (1,) + shape, lambda b, j: (jnp.minimum(b * n_tiles + j, n_seq - 1),) + (0,) * len(shape))
    cache_blk = seq_blk((N_HEADS, HEAD_DIM, cache_win))
    return pl.pallas_call(
        functools.partial(_proj_kernel, tm=tm, first_win_tile=first_win_tile, win=cache_win),
        grid=(B, n_tiles),
        in_specs=[
            pl.BlockSpec((1, tm, D_MODEL), lambda b, j: (b, j, 0)),
            pl.BlockSpec((1, D_MODEL), const2),
            pl.BlockSpec((D_MODEL, PROJ_COLS), const2, pipeline_mode=pl.Buffered(1)),
            pl.BlockSpec((tm, 128), lambda b, j: (j, 0)),
            pl.BlockSpec((tm, 128), lambda b, j: (j, 0)),
            pl.BlockSpec((1, WIDTH), const2),
            pl.BlockSpec((WIDTH, WIDTH), const2),
            pl.BlockSpec((N_HEADS // 2, CHUNK, 2 * CHUNK), lambda b, j: (0, 0, 0)),
            pl.BlockSpec((CHUNK, WIDTH), const2),
            seq_blk(s_qkv.shape[1:]), cache_blk, cache_blk,
        ],
        out_specs=[tile, tile, tile, tile, tile_t, tile_t, seq_blk((HEAD_DIM, N_HEADS))],
        out_shape=[out_sds(BF16), out_sds(F32), out_sds(F32), out_sds(F32), win_sds, win_sds,
                   jax.ShapeDtypeStruct((n_seq, HEAD_DIM, N_HEADS), F32)],
        scratch_shapes=[pltpu.VMEM((D_MODEL, PROJ_COLS), BF16)],
        compiler_params=pltpu.CompilerParams(
            dimension_semantics=("arbitrary", "arbitrary"), vmem_limit_bytes=V7X_VMEM_LIMIT),
        name="prompt_proj_sgu",
    )(x, g1, w_in, cos, sin, gv, ones_bd, wp, bias, s_qkv, cache_k_t, cache_v_t)


def _attn_kernel(q_ref, k_ref, v_ref, wg_ref, wu_ref, wd_ref, o_ref, w1_ref, w2_ref,
                 qd, kd, vd, res_o, res_l, nat_o, nat_l, bias, *, seq):
    n_tiles = seq // CHUNK
    last = len(DILATIONS) - 1
    stage_in = (res_o.at[last], res_l.at[last], nat_o.at[last - 1])
    stage_out = (res_o.at[last - 1], res_l.at[last - 1])

    w1_ref[:, :, 0:D_EXPERT] = wg_ref[...].astype(BF16)
    w1_ref[:, :, D_EXPERT:2 * D_EXPERT] = wu_ref[...].astype(BF16)
    w2_ref[...] = wd_ref[...].astype(BF16)

    lane = lax.broadcasted_iota(jnp.int32, (CHUNK, 128), 1)
    left = lane < HEAD_DIM
    qi2 = lax.broadcasted_iota(jnp.int32, (2 * CHUNK, 2 * CHUNK), 0) % CHUNK
    kj2 = lax.broadcasted_iota(jnp.int32, (2 * CHUNK, 2 * CHUNK), 1)
    dist2 = CHUNK + qi2 - kj2
    band2 = (dist2 >= 0) & (dist2 <= N_KEYS)
    zero_q = jnp.zeros((CHUNK, 128), BF16)
    bias[0] = jnp.where(band2, 0.0, NEG)
    bias[1] = jnp.where(band2 & (kj2 >= CHUNK), 0.0, NEG)

    kd[0:CHUNK, :] = jnp.zeros((CHUNK, 128), BF16)
    vd[0:CHUNK, :] = jnp.zeros((CHUNK, 128), BF16)

    for p, dil in enumerate(DILATIONS):
        sub = seq // dil
        nb = sub // CHUNK
        for ti, (src_ref, dst, pad) in enumerate(((q_ref, qd, 0), (k_ref, kd, CHUNK), (v_ref, vd, CHUNK))):
            if dil == 1:
                dst[pad:pad + seq, :] = src_ref[0].astype(BF16)
            elif dil == SUBLANE_STRIDE:
                for r in range(dil):
                    val = src_ref[0, pl.ds(r, sub, stride=dil), :]
                    stage_in[ti][r * sub:(r + 1) * sub, :] = val
                    dst[pad + r * sub:pad + (r + 1) * sub, :] = val.astype(BF16)
            else:
                coarse = seq // SUBLANE_STRIDE
                for r_lo in range(SUBLANE_STRIDE):
                    for r_hi in range(SUBLANE_STRIDE):
                        r = r_lo + SUBLANE_STRIDE * r_hi
                        val = stage_in[ti][pl.ds(r_lo * coarse + r_hi, sub, stride=SUBLANE_STRIDE), :]
                        dst[pad + r * sub:pad + (r + 1) * sub, :] = val.astype(BF16)

        def tile_body(g, i, p=p, nb=nb):
            t = g * TILES_PER_STEP + i
            row = pl.multiple_of(t * CHUNK, CHUNK)
            qt = qd[pl.ds(row, CHUNK), :]
            k2 = kd[pl.ds(row, 2 * CHUNK), :]
            v2 = vd[pl.ds(row, 2 * CHUNK), :]
            if TILES_PER_STEP % nb == 0:
                variant = 1 if i % nb == 0 else 0
            elif i == 0:
                variant = jnp.where((g * TILES_PER_STEP) % nb == 0, 1, 0)
            else:
                variant = 0
            q2 = jnp.concatenate([jnp.where(left, qt, zero_q), jnp.where(left, zero_q, qt)], axis=0)
            s = lax.dot_general(q2, k2, (((1,), (1,)), ((), ())), preferred_element_type=F32)
            s = s + bias[variant]
            m = jnp.max(s, axis=1, keepdims=True)
            e = jnp.exp(s - m)
            den = jnp.sum(e, axis=1, keepdims=True)
            pv = jnp.dot(e.astype(BF16), v2, preferred_element_type=F32) / den
            lse = jnp.broadcast_to(m + jnp.log(den), (2 * CHUNK, 128))
            res_o[p, pl.ds(row, CHUNK), :] = jnp.where(left, pv[0:CHUNK], pv[CHUNK:2 * CHUNK])
            res_l[p, pl.ds(row, CHUNK), :] = jnp.where(left, lse[0:CHUNK], lse[CHUNK:2 * CHUNK])

        def group_body(g, carry, tile_body=tile_body):
            for i in range(TILES_PER_STEP):
                tile_body(g, i)
            return carry

        lax.fori_loop(0, n_tiles // TILES_PER_STEP, group_body, 0)

    for p, dil in enumerate(DILATIONS):
        if dil == 1:
            continue
        sub = seq // dil
        for si, (res, nat) in enumerate(((res_o, nat_o), (res_l, nat_l))):
            if dil == SUBLANE_STRIDE:
                for r in range(dil):
                    nat[p - 1, pl.ds(r, sub, stride=dil), :] = res[p, r * sub:(r + 1) * sub, :]
            else:
                coarse = seq // SUBLANE_STRIDE
                for r_lo in range(SUBLANE_STRIDE):
                    for r_hi in range(SUBLANE_STRIDE):
                        r = r_lo + SUBLANE_STRIDE * r_hi
                        stage_out[si][pl.ds(r_lo * coarse + r_hi, sub, stride=SUBLANE_STRIDE), :] = (
                            res[p, r * sub:(r + 1) * sub, :])
                for r_lo in range(SUBLANE_STRIDE):
                    nat[p - 1, pl.ds(r_lo, coarse, stride=SUBLANE_STRIDE), :] = (
                        stage_out[si][r_lo * coarse:(r_lo + 1) * coarse, :])

    rows_per_step = 256

    def merge_body(c, carry):
        rows = pl.ds(pl.multiple_of(c * rows_per_step, rows_per_step), rows_per_step)
        l0, l1, l2 = res_l[0, rows, :], nat_l[0, rows, :], nat_l[1, rows, :]
        top = jnp.maximum(jnp.maximum(l0, l1), l2)
        w0, w1, w2 = jnp.exp(l0 - top), jnp.exp(l1 - top), jnp.exp(l2 - top)
        num = w0 * res_o[0, rows, :] + w1 * nat_o[0, rows, :] + w2 * nat_o[1, rows, :]
        o_ref[0, rows, :] = (num / (w0 + w1 + w2)).astype(o_ref.dtype)
        return carry

    lax.fori_loop(0, seq // rows_per_step, merge_body, 0)


def _prompt_attention(q, k, v, w_gate, w_up, w_down):
    B, S, _ = q.shape
    n_pairs = WIDTH // 128
    blk = pl.BlockSpec((1, S, 128), lambda b, hp: (b, 0, hp))
    n_steps = B * n_pairs
    assert N_EXPERTS % n_steps == 0
    e_blk = N_EXPERTS // n_steps
    expert_blk = lambda rows, cols: pl.BlockSpec((e_blk, rows, cols), lambda b, hp: (b * n_pairs + hp, 0, 0))
    return pl.pallas_call(
        functools.partial(_attn_kernel, seq=S),
        grid=(B, n_pairs),
        in_specs=[blk, blk, blk,
                  expert_blk(D_MODEL, D_EXPERT), expert_blk(D_MODEL, D_EXPERT), expert_blk(D_EXPERT, D_MODEL)],
        out_specs=[blk, expert_blk(D_MODEL, 2 * D_EXPERT), expert_blk(D_EXPERT, D_MODEL)],
        out_shape=[jax.ShapeDtypeStruct((B, S, WIDTH), BF16),
                   jax.ShapeDtypeStruct((N_EXPERTS, D_MODEL, 2 * D_EXPERT), BF16),
                   jax.ShapeDtypeStruct((N_EXPERTS, D_EXPERT, D_MODEL), BF16)],
        scratch_shapes=[
            pltpu.VMEM((S, 128), BF16),
            pltpu.VMEM((S + CHUNK, 128), BF16),
            pltpu.VMEM((S + CHUNK, 128), BF16),
            pltpu.VMEM((len(DILATIONS), S, 128), F32),
            pltpu.VMEM((len(DILATIONS), S, 128), F32),
            pltpu.VMEM((len(DILATIONS) - 1, S, 128), F32),
            pltpu.VMEM((len(DILATIONS) - 1, S, 128), F32),
            pltpu.VMEM((2, 2 * CHUNK, 2 * CHUNK), F32),
        ],
        compiler_params=pltpu.CompilerParams(
            dimension_semantics=("arbitrary", "arbitrary"), vmem_limit_bytes=V7X_VMEM_LIMIT),
        name="prompt_dilated_attention",
    )(q, k, v, w_gate, w_up, w_down)


def _route(logits):
    lane = lax.broadcasted_iota(jnp.int32, logits.shape, 1)
    big = jnp.int32(ROUTER_LANES)
    lg = jnp.where(lane < N_GROUPS, logits, NEG)
    gmax = jnp.max(lg, axis=1, keepdims=True)
    gp = 1.0 / jnp.sum(jnp.exp(lg - gmax), axis=1, keepdims=True)
    gi = jnp.min(jnp.where(lg == gmax, lane, big), axis=1, keepdims=True)
    lo = N_GROUPS + EXPERTS_PER_GROUP * gi
    le = jnp.where((lane >= lo) & (lane < lo + EXPERTS_PER_GROUP), logits, NEG)
    m1 = jnp.max(le, axis=1, keepdims=True)
    i1 = jnp.min(jnp.where(le == m1, lane, big), axis=1, keepdims=True)
    le2 = jnp.where(lane == i1, NEG, le)
    m2 = jnp.max(le2, axis=1, keepdims=True)
    i2 = jnp.min(jnp.where(le2 == m2, lane, big), axis=1, keepdims=True)
    e2 = jnp.exp(m2 - m1)
    w1 = 1.0 / (1.0 + e2)
    w2 = e2 / (1.0 + e2)
    return jnp.where(lane == i1, gp * w1, jnp.where(lane == i2, gp * w2, 0.0))


def _mix_router_kernel(a_ref, b_ref, x_ref, wo_ref, g2_ref, wr_ref, br_ref, xp_ref, h2_ref, gates_ref, *, precise):
    mixed = (_dot(a_ref[...], wo_ref[0:WIDTH, :], precise)
             + _dot(b_ref[...], wo_ref[WIDTH:2 * WIDTH, :], precise))
    xp = x_ref[...] + mixed
    xp_ref[...] = xp
    h2 = _rmsnorm(xp, g2_ref[...])
    h2_ref[...] = h2.astype(h2_ref.dtype)
    logits = jnp.dot(h2, wr_ref[...], preferred_element_type=F32,
                     precision=lax.Precision.HIGHEST) + br_ref[...]
    gates_ref[...] = _route(logits)


def _mix_router(a, b, x, w_out, g2, w_router, b_router, *, tm, precise=False):
    n = x.shape[0]
    const = lambda i: (0, 0)
    row_blk = lambda w: pl.BlockSpec((tm, w), lambda i: (i, 0))
    return pl.pallas_call(
        functools.partial(_mix_router_kernel, precise=precise),
        grid=(n // tm,),
        in_specs=[row_blk(WIDTH), row_blk(WIDTH), row_blk(D_MODEL),
                  pl.BlockSpec((2 * WIDTH, D_MODEL), const),
                  pl.BlockSpec((1, D_MODEL), const),
                  pl.BlockSpec((D_MODEL, ROUTER_LANES), const),
                  pl.BlockSpec((1, ROUTER_LANES), const)],
        out_specs=[row_blk(D_MODEL), row_blk(D_MODEL), row_blk(ROUTER_LANES)],
        out_shape=[jax.ShapeDtypeStruct((n, D_MODEL), F32),
                   jax.ShapeDtypeStruct((n, D_MODEL), BF16),
                   jax.ShapeDtypeStruct((n, ROUTER_LANES), F32)],
        compiler_params=pltpu.CompilerParams(
            dimension_semantics=("arbitrary",), vmem_limit_bytes=V7X_VMEM_LIMIT),
        name="outproj_router",
    )(a, b, x, w_out, g2, w_router, b_router)


def _nt_dot(w, t):
    return lax.dot_general(w, t, (((1,), (1,)), ((), ())), preferred_element_type=F32)


def _route_t(logits_t):
    row = lax.broadcasted_iota(jnp.int32, logits_t.shape, 0)
    big = jnp.int32(ROUTER_LANES)
    lg = jnp.where(row < N_GROUPS, logits_t, NEG)
    gmax = jnp.max(lg, axis=0, keepdims=True)
    gp = 1.0 / jnp.sum(jnp.exp(lg - gmax), axis=0, keepdims=True)
    gi = jnp.min(jnp.where(lg == gmax, row, big), axis=0, keepdims=True)
    lo = N_GROUPS + EXPERTS_PER_GROUP * gi
    le = jnp.where((row >= lo) & (row < lo + EXPERTS_PER_GROUP), logits_t, NEG)
    m1 = jnp.max(le, axis=0, keepdims=True)
    i1 = jnp.min(jnp.where(le == m1, row, big), axis=0, keepdims=True)
    le2 = jnp.where(row == i1, NEG, le)
    m2 = jnp.max(le2, axis=0, keepdims=True)
    i2 = jnp.min(jnp.where(le2 == m2, row, big), axis=0, keepdims=True)
    e2 = jnp.exp(m2 - m1)
    return i1 - N_GROUPS, i2 - N_GROUPS, gp / (1.0 + e2), gp * e2 / (1.0 + e2)


def _mix_route_sort_kernel(a_ref, b_ref, x_ref, wo_ref, g2_ref, wr_ref, brc_ref, tri_ref, ltri_ref,
                           xp_ref, h2_ref, mrow_ref, mcol_ref, tab_ref, wob):
    t = MOE_BLOCK

    @pl.when(pl.program_id(0) == 0)
    def _():
        wob[...] = wo_ref[...].astype(BF16)

    def project(j):
        rows = slice(j * t, (j + 1) * t)
        xp = x_ref[rows, :] + _dot(a_ref[rows, :], wob[0:WIDTH, :]) + _dot(b_ref[rows, :], wob[WIDTH:2 * WIDTH, :])
        xp_ref[rows, :] = xp
        h2 = _rmsnorm(xp, g2_ref[...])
        hi = h2.astype(BF16)
        h2_ref[rows, :] = hi
        return hi, (h2 - hi.astype(F32)).astype(BF16)

    def route(hi, lo):
        prod_hi = _nt_dot(wr_ref[...], hi)
        logits_t = (prod_hi[0:ROUTER_ROWS] + prod_hi[ROUTER_ROWS:2 * ROUTER_ROWS]
                    + _nt_dot(wr_ref[0:ROUTER_ROWS, :], lo) + brc_ref[...])
        return _route_t(logits_t)

    def sort_meta(j, ex1, ex2, gate1, gate2):
        rows = slice(j * t, (j + 1) * t)
        pair_e = jnp.concatenate([ex1, ex2], axis=1)
        row = lax.broadcasted_iota(jnp.int32, (N_EXPERTS, 2 * t), 0)
        onehot = jnp.where(row == pair_e, 1.0, 0.0)
        n_lane_tiles = 2 * t // 128
        local = _dot(jnp.concatenate([onehot[:, k * 128:(k + 1) * 128] for k in range(n_lane_tiles)], axis=0),
                     tri_ref[...])
        carry = jnp.zeros((N_EXPERTS, 1), F32)
        cums = []
        for k in range(n_lane_tiles):
            tile = local[k * N_EXPERTS:(k + 1) * N_EXPERTS, :]
            cums.append(tile + carry)
            carry = carry + tile[:, 127:128]
        cum = jnp.concatenate(cums, axis=1)
        rank = jnp.sum(onehot * cum, axis=0, keepdims=True) - 1.0
        counts = carry
        units32 = jnp.floor((counts + (MOE_ROW_ALIGN - 1)) * (1.0 / MOE_ROW_ALIGN))
        units = jnp.concatenate([jnp.broadcast_to(units32, (N_EXPERTS, 128)),
                                 jnp.zeros((ROUTER_LANES - N_EXPERTS, 128), F32)], axis=0)
        off = _dot(ltri_ref[...], units) * MOE_ROW_ALIGN
        dst = jnp.sum(onehot * off[0:N_EXPERTS, 0:1], axis=0, keepdims=True) + rank

        r8 = lax.broadcasted_iota(jnp.int32, (8, t), 0)
        mrow_ref[j] = jnp.where(r8 == 0, dst[:, 0:t], jnp.where(r8 == 1, dst[:, t:2 * t],
                                jnp.where(r8 == 2, gate1, jnp.where(r8 == 3, gate2, 0.0))))
        r128 = lax.broadcasted_iota(jnp.int32, (ROUTER_LANES, t), 0)
        meta = jnp.where(r128 == 0, dst[:, 0:t], jnp.where(r128 == 1, dst[:, t:2 * t],
                         jnp.where(r128 == 2, gate1, jnp.where(r128 == 3, gate2, 0.0))))
        mcol_ref[rows, :] = meta.T
        lane = lax.broadcasted_iota(jnp.int32, (ROUTER_LANES, 128), 1)
        n_rows = units * MOE_ROW_ALIGN
        chunks = jnp.floor((n_rows + (MOE_CHUNK - 1)) * (1.0 / MOE_CHUNK))
        tab_ref[j] = jnp.where(lane == 0, off, jnp.where(lane == 1, chunks, jnp.where(lane == 2, off + n_rows, 0.0)))

    blocks = range(MOE_ROUTE_BLOCKS)
    projected = [project(j) for j in blocks]
    routed = [route(hi, lo) for hi, lo in projected]
    for j in blocks:
        sort_meta(j, *routed[j])


def _mix_route_sort(a, b, x, w_out, g2, wr_hl, br_col, tri, ltri):
    n = x.shape[0]
    t = MOE_BLOCK
    nblk = n // t
    g = MOE_ROUTE_BLOCKS
    assert nblk % g == 0
    const = lambda i: (0, 0)
    row_blk = lambda w: pl.BlockSpec((g * t, w), lambda i: (i, 0))
    return pl.pallas_call(
        _mix_route_sort_kernel,
        grid=(nblk // g,),
        in_specs=[row_blk(WIDTH), row_blk(WIDTH), row_blk(D_MODEL),
                  pl.BlockSpec((2 * WIDTH, D_MODEL), const, pipeline_mode=pl.Buffered(1)),
                  pl.BlockSpec((1, D_MODEL), const),
                  pl.BlockSpec((2 * ROUTER_ROWS, D_MODEL), const),
                  pl.BlockSpec((ROUTER_ROWS, 1), const),
                  pl.BlockSpec((ROUTER_LANES, ROUTER_LANES), const),
                  pl.BlockSpec((ROUTER_LANES, ROUTER_LANES), const)],
        out_specs=[row_blk(D_MODEL), row_blk(D_MODEL),
                   pl.BlockSpec((g, 8, t), lambda i: (i, 0, 0)),
                   row_blk(ROUTER_LANES),
                   pl.BlockSpec((g, ROUTER_LANES, 128), lambda i: (i, 0, 0))],
        out_shape=[jax.ShapeDtypeStruct((n, D_MODEL), F32),
                   jax.ShapeDtypeStruct((n, D_MODEL), BF16),
                   jax.ShapeDtypeStruct((nblk, 8, t), F32),
                   jax.ShapeDtypeStruct((n, ROUTER_LANES), F32),
                   jax.ShapeDtypeStruct((nblk, ROUTER_LANES, 128), F32)],
        scratch_shapes=[pltpu.VMEM((2 * WIDTH, D_MODEL), BF16)],
        compiler_params=pltpu.CompilerParams(
            dimension_semantics=("arbitrary",), vmem_limit_bytes=V7X_VMEM_LIMIT),
        name="outproj_route_sort",
    )(a, b, x, w_out, g2, wr_hl, br_col, tri, ltri)


def _silu_mul(ab):
    a = ab[:, :D_EXPERT]
    return a * (1.0 / (1.0 + jnp.exp(-a))) * ab[:, D_EXPERT:]


def _chunk_offsets(tab, n_blk, e):
    return [pl.multiple_of(tab(j, e, 0), MOE_ROW_ALIGN) for j in range(n_blk)]


def _moe_gate_up_kernel(tab_ref, h_ref, mrow_ref, w1_ref, sh_ref, sg_ref, hs_ref, shid_ref, xs, *,
                        flags_at, tails_at):
    t = MOE_BLOCK
    n_blk = MOE_GATE_UP_BLOCKS
    first = pl.program_id(0) * n_blk
    tab = lambda j, e, c: tab_ref[((first + j) * N_EXPERTS + e) * 3 + c]

    piece = 512

    def gather(j, r0, rows):
        dst1 = mrow_ref[j, 0:1, :].astype(jnp.int32)
        dst2 = mrow_ref[j, 1:2, :].astype(jnp.int32)
        d_idx = lax.broadcasted_iota(jnp.int32, (rows, t), 0) + r0
        sel = jnp.where((d_idx == dst1) | (d_idx == dst2), 1.0, 0.0)
        xs[j, r0:r0 + rows, :] = _dot(sel, h_ref[j * t:(j + 1) * t, :]).astype(BF16)

    for j in range(n_blk):
        for r0 in range(0, MOE_GATHER_ROWS, piece):
            gather(j, r0, min(piece, MOE_GATHER_ROWS - r0))
        xs[j, MOE_GATHER_ROWS:MOE_ROWS, :] = jnp.zeros((MOE_ROWS - MOE_GATHER_ROWS, D_MODEL), BF16)
    hs_ref[...] = jnp.zeros(hs_ref.shape, BF16)
    for j in range(n_blk):
        @pl.when(tab_ref[tails_at + first + j] > 0)
        def _(j=j):
            gather(j, MOE_GATHER_ROWS, MOE_SORT_ROWS - MOE_GATHER_ROWS)

    def first_chunks(g, carry):
        for i in range(MOE_EXPERTS_PER_STEP):
            e = g * MOE_EXPERTS_PER_STEP + i
            offs = _chunk_offsets(tab, n_blk, e)
            x = jnp.concatenate([xs[j, pl.ds(offs[j], MOE_CHUNK), :] for j in range(n_blk)], axis=0)
            hid = _silu_mul(jnp.dot(x, w1_ref[e], preferred_element_type=F32)).astype(BF16)
            for j in range(n_blk):
                hs_ref[j, pl.ds(offs[j], MOE_CHUNK), :] = hid[j * MOE_CHUNK:(j + 1) * MOE_CHUNK, :]
        return carry

    lax.fori_loop(0, N_EXPERTS // MOE_EXPERTS_PER_STEP, first_chunks, 0)

    def more_chunks(j, e, carry):
        off, n_chunks, end = tab(j, e, 0), tab(j, e, 1), tab(j, e, 2)

        def chunk(c, carry):
            r0 = pl.multiple_of(off + c * MOE_CHUNK, MOE_ROW_ALIGN)
            rows = r0 + lax.broadcasted_iota(jnp.int32, (MOE_CHUNK, D_EXPERT), 0)
            hid = _silu_mul(jnp.dot(xs[j, pl.ds(r0, MOE_CHUNK), :], w1_ref[e], preferred_element_type=F32))
            hs_ref[j, pl.ds(r0, MOE_CHUNK), :] = jnp.where(rows < end, hid.astype(BF16),
                                                           hs_ref[j, pl.ds(r0, MOE_CHUNK), :])
            return carry

        return lax.fori_loop(1, n_chunks, chunk, carry)

    for j in range(n_blk):
        @pl.when(tab_ref[flags_at + first + j] > 0)
        def _(j=j):
            lax.fori_loop(0, N_EXPERTS, functools.partial(more_chunks, j), 0)

    @pl.when(pl.program_id(0) == 0)
    def _():
        sh = sh_ref[...]
        gates = sg_ref[...]
        for e in range(N_EXPERTS):
            gate = gates[:, N_GROUPS + e:N_GROUPS + e + 1]
            hid = _silu_mul(jnp.dot(sh, w1_ref[e], preferred_element_type=F32)) * gate
            shid_ref[:, e * D_EXPERT:(e + 1) * D_EXPERT] = hid.astype(BF16)


def _moe_gate_up(tab, h2, mrow, w1_b, s_h, s_gates):
    n = h2.shape[0]
    n_s = s_h.shape[0]
    nblk = n // MOE_BLOCK
    g = MOE_GATE_UP_BLOCKS
    assert nblk % g == 0
    whole = lambda shape: pl.BlockSpec(shape, lambda i, tab: (0,) * len(shape))
    return pl.pallas_call(
        functools.partial(_moe_gate_up_kernel, flags_at=nblk * N_EXPERTS * 3, tails_at=nblk * N_EXPERTS * 3 + nblk),
        grid_spec=pltpu.PrefetchScalarGridSpec(
            num_scalar_prefetch=1,
            grid=(nblk // g,),
            in_specs=[pl.BlockSpec((g * MOE_BLOCK, D_MODEL), lambda i, tab: (i, 0)),
                      pl.BlockSpec((g, 8, MOE_BLOCK), lambda i, tab: (i, 0, 0)),
                      pl.BlockSpec(w1_b.shape, lambda i, tab: (0, 0, 0), pipeline_mode=pl.Buffered(1)),
                      whole(s_h.shape), whole(s_gates.shape)],
            out_specs=[pl.BlockSpec((g, MOE_ROWS, D_EXPERT), lambda i, tab: (i, 0, 0)),
                       whole((n_s, N_EXPERTS * D_EXPERT))],
            scratch_shapes=[pltpu.VMEM((g, MOE_ROWS, D_MODEL), BF16)]),
        out_shape=[jax.ShapeDtypeStruct((nblk, MOE_ROWS, D_EXPERT), BF16),
                   jax.ShapeDtypeStruct((n_s, N_EXPERTS * D_EXPERT), BF16)],
        compiler_params=pltpu.CompilerParams(
            dimension_semantics=("arbitrary",), vmem_limit_bytes=V7X_VMEM_LIMIT),
        name="moe_gate_up",
    )(tab, h2, mrow, w1_b, s_h, s_gates)


def _moe_down_kernel(tab_ref, hs_ref, xp_ref, mcol_ref, w2_ref, gf_ref, shid_ref, sx_ref, y_ref, sy_ref, os, *,
                     flags_at):
    t = MOE_BLOCK
    n_blk = MOE_DOWN_BLOCKS
    first = pl.program_id(0) * n_blk
    tab = lambda j, e, c: tab_ref[((first + j) * N_EXPERTS + e) * 3 + c]
    os[...] = jnp.zeros(os.shape, BF16)

    def first_chunks(g, carry):
        for i in range(MOE_EXPERTS_PER_STEP):
            e = g * MOE_EXPERTS_PER_STEP + i
            offs = _chunk_offsets(tab, n_blk, e)
            hid = jnp.concatenate([hs_ref[j, pl.ds(offs[j], MOE_CHUNK), :] for j in range(n_blk)], axis=0)
            out = jnp.dot(hid, w2_ref[e], preferred_element_type=F32).astype(BF16)
            for j in range(n_blk):
                os[j, pl.ds(offs[j], MOE_CHUNK), :] = out[j * MOE_CHUNK:(j + 1) * MOE_CHUNK, :]
        return carry

    lax.fori_loop(0, N_EXPERTS // MOE_EXPERTS_PER_STEP, first_chunks, 0)

    def more_chunks(j, e, carry):
        off, n_chunks, end = tab(j, e, 0), tab(j, e, 1), tab(j, e, 2)

        def chunk(c, carry):
            r0 = pl.multiple_of(off + c * MOE_CHUNK, MOE_ROW_ALIGN)
            rows = r0 + lax.broadcasted_iota(jnp.int32, (MOE_CHUNK, D_MODEL), 0)
            out = jnp.dot(hs_ref[j, pl.ds(r0, MOE_CHUNK), :], w2_ref[e], preferred_element_type=F32)
            os[j, pl.ds(r0, MOE_CHUNK), :] = jnp.where(rows < end, out.astype(BF16), os[j, pl.ds(r0, MOE_CHUNK), :])
            return carry

        return lax.fori_loop(1, n_chunks, chunk, carry)

    for j in range(n_blk):
        @pl.when(tab_ref[flags_at + first + j] > 0)
        def _(j=j):
            lax.fori_loop(0, N_EXPERTS, functools.partial(more_chunks, j), 0)

    l_idx = lax.broadcasted_iota(jnp.int32, (t, MOE_SORT_ROWS), 1)

    def scatter_matrix(j):
        mcol = mcol_ref[j * t:(j + 1) * t, :]
        d1c = mcol[:, 0:1].astype(jnp.int32)
        d2c = mcol[:, 1:2].astype(jnp.int32)
        comb = jnp.where(l_idx == d1c, mcol[:, 2:3], 0.0) + jnp.where(l_idx == d2c, mcol[:, 3:4], 0.0)
        return comb.astype(BF16)

    combs = [scatter_matrix(j) for j in range(n_blk)]
    moes = [_dot(combs[j], os[j, 0:MOE_SORT_ROWS, :]) for j in range(n_blk)]
    for j in range(n_blk):
        rows = slice(j * t, (j + 1) * t)
        y_ref[rows, :] = _rmsnorm(xp_ref[rows, :] + moes[j], gf_ref[...])

    @pl.when(pl.program_id(0) == 0)
    def _():
        w2_all = w2_ref[...].reshape(N_EXPERTS * D_EXPERT, D_MODEL)
        ys = sx_ref[...] + jnp.dot(shid_ref[...], w2_all, preferred_element_type=F32)
        sy_ref[...] = _rmsnorm(ys, gf_ref[...])


def _moe_down(tab, hs_sorted, xp, mcol, w2_b, gf, s_hid, s_x):
    n = xp.shape[0]
    n_s = s_x.shape[0]
    nblk = n // MOE_BLOCK
    g = MOE_DOWN_BLOCKS
    assert nblk % g == 0
    whole = lambda shape: pl.BlockSpec(shape, lambda i, tab: (0,) * len(shape))
    row_blk = lambda w: pl.BlockSpec((g * MOE_BLOCK, w), lambda i, tab: (i, 0))
    return pl.pallas_call(
        functools.partial(_moe_down_kernel, flags_at=nblk * N_EXPERTS * 3),
        grid_spec=pltpu.PrefetchScalarGridSpec(
            num_scalar_prefetch=1,
            grid=(nblk // g,),
            in_specs=[pl.BlockSpec((g, MOE_ROWS, D_EXPERT), lambda i, tab: (i, 0, 0)),
                      row_blk(D_MODEL), row_blk(ROUTER_LANES),
                      pl.BlockSpec(w2_b.shape, lambda i, tab: (0, 0, 0), pipeline_mode=pl.Buffered(1)),
                      whole((1, D_MODEL)), whole(s_hid.shape), whole(s_x.shape)],
            out_specs=[row_blk(D_MODEL), whole((n_s, D_MODEL))],
            scratch_shapes=[pltpu.VMEM((g, MOE_ROWS, D_MODEL), BF16)]),
        out_shape=[jax.ShapeDtypeStruct((n, D_MODEL), F32), jax.ShapeDtypeStruct((n_s, D_MODEL), F32)],
        compiler_params=pltpu.CompilerParams(
            dimension_semantics=("arbitrary",), vmem_limit_bytes=V7X_VMEM_LIMIT),
        name="moe_down_combine",
    )(tab, hs_sorted, xp, mcol, w2_b, gf, s_hid, s_x)


def _sample_proj_kernel(x_ref, g1_ref, w_ref, cos_ref, sin_ref, gv_ref, ones_ref, w00_ref, b0_ref,
                        rep_ref, foldt_ref, a_ref, k_ref, v_ref, vn_ref, qkvt_ref):
    h = _rmsnorm(x_ref[...], g1_ref[...])

    def proj(i):
        return _dot(h, w_ref[:, i * WIDTH:(i + 1) * WIDTH], precise=True)

    cos = _tile_lanes(cos_ref[...], WIDTH // 128)
    sin = _tile_lanes(sin_ref[...], WIDTH // 128)
    q = _rope(proj(2), cos, sin) * (HEAD_DIM ** -0.5)
    k = _rope(proj(3), cos, sin)
    v = proj(4)
    vn = _group_rmsnorm(proj(1), ones_ref[...], gv_ref[...], precise=True)
    a_ref[...] = proj(0) * (w00_ref[...] * vn + b0_ref[...])
    k_ref[...] = k
    v_ref[...] = v
    vn_ref[...] = vn

    n_rep = rep_ref.shape[0]
    r_idx = lax.broadcasted_iota(jnp.int32, (n_rep, WIDTH), 0)
    l_idx = lax.broadcasted_iota(jnp.int32, (n_rep, WIDTH), 1)
    own = (l_idx // HEAD_DIM) == (r_idx % N_HEADS)
    for t, src in enumerate((q, k, v)):
        rep = _dot(rep_ref[...], src, precise=True)
        qkvt_ref[t * HEAD_DIM:(t + 1) * HEAD_DIM, :] = lax.dot_general(
            foldt_ref[...], jnp.where(own, rep, 0.0), (((1,), (1,)), ((), ())),
            preferred_element_type=F32, precision=lax.Precision.HIGHEST)


def _sample_proj(x, g1, w_in, cos, sin, gv, ones_bd, w00, b0, rep, foldt):
    bd = x.shape[0]
    sds = lambda r, c: jax.ShapeDtypeStruct((r, c), F32)
    return pl.pallas_call(
        _sample_proj_kernel,
        out_shape=[sds(bd, WIDTH), sds(bd, WIDTH), sds(bd, WIDTH), sds(bd, WIDTH),
                   sds(3 * HEAD_DIM, bd * N_HEADS)],
        compiler_params=pltpu.CompilerParams(vmem_limit_bytes=V7X_VMEM_LIMIT),
        name="sample_proj",
    )(x, g1, w_in, cos, sin, gv, ones_bd, w00, b0, rep, foldt)


def _sample_scores(qkv_ref, k_ref, win):
    cols = qkv_ref[0]

    head = lax.broadcasted_iota(jnp.int32, (N_HEADS, win), 0)
    s = jnp.zeros((N_HEADS, win), F32)
    s_self = jnp.zeros((N_HEADS, 1), F32)
    for h in range(N_HEADS):
        qc = cols[0:HEAD_DIM, h:h + 1]
        kc = cols[HEAD_DIM:2 * HEAD_DIM, h:h + 1]
        s_h = jnp.sum(k_ref[0, h] * qc, axis=0, keepdims=True)
        s = jnp.where(head == h, s_h, s)
        s_self = jnp.where(head[:, 0:1] == h, jnp.sum(qc * kc, axis=0, keepdims=True), s_self)
    return s, s_self, cols[2 * HEAD_DIM:3 * HEAD_DIM, :]


def _sample_finish(s, s_self, v_cols, v_ref, o_ref, win):
    dist = win - lax.broadcasted_iota(jnp.int32, (1, win), 1)
    members = [(dist <= N_KEYS * dil) & (dist % dil == 0) for dil in DILATIONS]
    es, e_selfs, dens, lses = [], [], [], []
    for mem in members:
        sm = jnp.where(mem, s, NEG)
        m = jnp.maximum(jnp.max(sm, axis=1, keepdims=True), s_self)
        e = jnp.exp(sm - m)
        e_self = jnp.exp(s_self - m)
        den = jnp.sum(e, axis=1, keepdims=True) + e_self
        es.append(e)
        e_selfs.append(e_self)
        dens.append(den)
        lses.append(m + jnp.log(den))
    top = jnp.maximum(jnp.maximum(lses[0], lses[1]), lses[2])
    ws = [jnp.exp(l - top) for l in lses]
    wsum = ws[0] + ws[1] + ws[2]
    coef = [w / (den * wsum) for w, den in zip(ws, dens)]
    p_keys = coef[0] * es[0] + coef[1] * es[1] + coef[2] * es[2]
    p_self = coef[0] * e_selfs[0] + coef[1] * e_selfs[1] + coef[2] * e_selfs[2]

    for h in range(N_HEADS):
        o_ref[0, :, h:h + 1] = (jnp.sum(v_ref[0, h] * p_keys[h:h + 1, :], axis=1, keepdims=True)
                                + p_self[h:h + 1, :] * v_cols[:, h:h + 1])


def _rope_tables(first_pos, count):
    half = HEAD_DIM // 2
    inv = ROPE_THETA ** (-np.arange(half, dtype=np.float64) * 2.0 / HEAD_DIM)
    ang = (first_pos + np.arange(count, dtype=np.float64))[:, None] * inv[None, :]
    cos, sin = np.cos(ang), np.sin(ang)
    cos128 = np.concatenate([cos, cos, cos, cos], axis=1).astype(np.float32)
    sin128 = np.concatenate([-sin, sin, -sin, sin], axis=1).astype(np.float32)
    return jnp.asarray(cos128), jnp.asarray(sin128)


def kernel(x_prompt, x_sample, cache_win_k, cache_win_v, ln1_g, w_in, sgu_norm_g, sgu_w, sgu_b, w_out, ln2_g,
           w_router_group, b_router_group, w_router_expert, b_router_expert, w_gate, w_up, w_down, lnf_g):
    depth = w_in.shape[0]
    assert depth == 1 and x_sample.shape[1] == 1
    B, S, _ = x_prompt.shape
    bd = x_sample.shape[0]
    win = cache_win_k.shape[2]
    assert S % (max(DILATIONS) * CHUNK) == 0 and win >= max(DILATIONS) * N_KEYS and PAST_LEN % CHUNK == 0
    l = 0

    pad = ROUTER_LANES - N_GROUPS - N_EXPERTS
    w_router = jnp.pad(jnp.concatenate([w_router_group[l], w_router_expert[l]], axis=1), ((0, 0), (0, pad)))
    b_router = jnp.pad(jnp.concatenate([b_router_group[l], b_router_expert[l]]), (0, pad))[None, :]
    g1 = ln1_g[l][None, :]
    g2 = ln2_g[l][None, :]
    gf = lnf_g[None, :]
    gv = sgu_norm_g[l].reshape(1, WIDTH)
    grp = np.arange(WIDTH) // HEAD_DIM
    ones_bd = jnp.asarray(np.where(grp[:, None] == grp[None, :], 1.0 / HEAD_DIM, 0.0), BF16)
    wp = jnp.concatenate([sgu_w[l][0::2], sgu_w[l][1::2]], axis=-1)
    bias = jnp.repeat(sgu_b[l].T, HEAD_DIM, axis=1)
    w00 = jnp.repeat(sgu_w[l][:, 0, 0], HEAD_DIM)[None, :]
    b0 = jnp.repeat(sgu_b[l][:, 0], HEAD_DIM)[None, :]

    cos_s, sin_s = _rope_tables(PAST_LEN, 1)
    rep = jnp.asarray(np.arange(bd * N_HEADS)[:, None] // N_HEADS == np.arange(bd)[None, :], F32)
    foldt = jnp.asarray(np.arange(HEAD_DIM)[:, None] == np.arange(WIDTH)[None, :] % HEAD_DIM, F32)
    xs = x_sample.reshape(bd, D_MODEL)
    a_s, k_s, v_s, vn_s, qkvt = _sample_proj(xs, g1, w_in[l], cos_s, sin_s, gv, ones_bd, w00, b0, rep, foldt)
    to_pos_minor = lambda c: jnp.transpose(c, (0, 2, 3, 1))

    cos_p, sin_p = _rope_tables(0, S)
    qkv_seq = jnp.transpose(qkvt.reshape(3 * HEAD_DIM, bd, N_HEADS), (1, 0, 2))
    a_p, q_p, k_p, v_p, kt_p, vt_p, o3 = _prompt_proj(
        x_prompt, g1, w_in[l], cos_p, sin_p, gv, ones_bd, wp, bias,
        qkv_seq, to_pos_minor(cache_win_k[l]), to_pos_minor(cache_win_v[l]))
    b_p, w1_b, w2_b = _prompt_attention(q_p, k_p, v_p, w_gate[l], w_up[l], w_down[l])
    n = B * S
    assert n % MOE_BLOCK == 0
    wr_t = w_router.T[:ROUTER_ROWS]
    wr_hi = wr_t.astype(BF16)
    wr_hl = jnp.concatenate([wr_hi, (wr_t - wr_hi.astype(F32)).astype(BF16)], axis=0)
    lane_idx = np.arange(ROUTER_LANES)
    tri = jnp.asarray(lane_idx[:, None] <= lane_idx[None, :], BF16)
    ltri = jnp.asarray(lane_idx[None, :] < lane_idx[:, None], BF16)
    xp2, h2, mrow, mcol, tab_f = _mix_route_sort(
        a_p.reshape(n, WIDTH), b_p.reshape(n, WIDTH), x_prompt.reshape(n, D_MODEL),
        w_out[l], g2, wr_hl, b_router.reshape(ROUTER_LANES, 1)[:ROUTER_ROWS], tri, ltri)
    tab_i = tab_f[:, :N_EXPERTS, 0:3].astype(jnp.int32)
    multi_chunk = (jnp.max(tab_i[:, :, 1], axis=1) > 1).astype(jnp.int32)
    in_tail = (tab_i[:, N_EXPERTS - 1, 2] > MOE_GATHER_ROWS).astype(jnp.int32)
    tab = jnp.concatenate([tab_i.reshape(-1), multi_chunk, in_tail])

    b_s = jnp.transpose(o3, (0, 2, 1)).reshape(bd, WIDTH)
    xs2, hs2, gates_s = _mix_router(a_s, b_s, xs, w_out[l], g2, w_router, b_router, tm=bd, precise=True)

    hid_sorted, hid_s = _moe_gate_up(tab, h2, mrow, w1_b, hs2, gates_s)
    y_prompt, y_sample = _moe_down(tab, hid_sorted, xp2, mcol, w2_b, gf, hid_s, xs2)
    y_prompt = y_prompt.reshape(B, S, D_MODEL)
    y_sample = y_sample.reshape(bd, 1, D_MODEL)
    buf_p = min(MAX_WINDOW, S)
    to_win = lambda t: jnp.transpose(t.reshape(1, B, N_HEADS, HEAD_DIM, buf_p), (0, 1, 4, 2, 3))
    new_k_p = to_win(kt_p)
    new_v_p = to_win(vt_p)

    shape_s = (1, bd, 1, N_HEADS, HEAD_DIM)
    return (y_prompt, y_sample, new_k_p, new_v_p,
            k_s.reshape(shape_s), v_s.reshape(shape_s), vn_s.reshape(shape_s))
```

```python
import functools

import jax
import jax.numpy as jnp
import numpy as np
from jax import lax
from jax.experimental import pallas as pl
from jax.experimental.pallas import tpu as pltpu

F32 = jnp.float32
BF16 = jnp.bfloat16

D_MODEL = 1024
HEAD_DIM = 64
N_HEADS = 8
WIDTH = N_HEADS * HEAD_DIM
PROJ_COLS = 5 * WIDTH
CHUNK = 128
DILATIONS = (1, 4, 16)
N_KEYS = 128
MAX_WINDOW = 2048
PAST_LEN = 16384
ROPE_THETA = 10000.0
N_GROUPS = 4
EXPERTS_PER_GROUP = 8
N_EXPERTS = N_GROUPS * EXPERTS_PER_GROUP
D_EXPERT = 128
EPS = 1e-6
NEG = -1e30
TILES_PER_STEP = 32
SUBLANE_STRIDE = 4
assert DILATIONS == (1, SUBLANE_STRIDE, SUBLANE_STRIDE ** 2)
MOE_BLOCK = 512
MOE_ROW_ALIGN = 16
MOE_CHUNK = 48
MOE_EXPERTS_PER_STEP = 16
MOE_ROUTE_BLOCKS = 2
MOE_GATE_UP_BLOCKS = 4
MOE_DOWN_BLOCKS = 2
MOE_SORT_ROWS = -(-(2 * MOE_BLOCK + N_EXPERTS * (MOE_ROW_ALIGN - 1)) // 512) * 512
MOE_ROWS = MOE_SORT_ROWS + MOE_CHUNK
MOE_GATHER_ROWS = 2 * MOE_BLOCK + N_EXPERTS * (MOE_ROW_ALIGN - 1) * 2 // 3
assert MOE_GATHER_ROWS % MOE_ROW_ALIGN == 0 and MOE_GATHER_ROWS < MOE_SORT_ROWS
ROUTER_ROWS = 48
ROUTER_LANES = 128
V7X_VMEM_LIMIT = 56 * 1024 * 1024


def _rmsnorm(x, g):
    return x * lax.rsqrt(jnp.mean(x * x, axis=-1, keepdims=True) + EPS) * g


def _tile_lanes(t, reps):
    return jnp.concatenate([t] * reps, axis=1)


def _rope(t, cos, sin_signed):
    lane = lax.broadcasted_iota(jnp.int32, t.shape, 1)
    first_half = (lane % HEAD_DIM) < (HEAD_DIM // 2)
    n = t.shape[1]
    partner = jnp.where(first_half, pltpu.roll(t, n - HEAD_DIM // 2, 1), pltpu.roll(t, HEAD_DIM // 2, 1))
    return t * cos + partner * sin_signed


def _dot(a, b, precise=False):
    if precise:
        return jnp.dot(a.astype(F32), b.astype(F32), preferred_element_type=F32,
                       precision=lax.Precision.HIGHEST)
    return jnp.dot(a.astype(BF16), b.astype(BF16), preferred_element_type=F32)


def _group_rmsnorm(va, ones_bd, gv, precise=False):
    ms = _dot(va * va, ones_bd, precise)
    return va * lax.rsqrt(ms + EPS) * gv


def _proj_kernel(x_ref, g1_ref, w_ref, cos_ref, sin_ref, gv_ref, ones_ref, wp_ref, bias_ref,
                 sqkv_ref, ck_ref, cv_ref,
                 a_ref, q_ref, k_ref, v_ref, kt_ref, vt_ref, so_ref, wb, *, tm, first_win_tile, win):
    @pl.when((pl.program_id(0) == 0) & (pl.program_id(1) == 0))
    def _():
        wb[...] = w_ref[...].astype(BF16)

    h = _rmsnorm(x_ref[0], g1_ref[...]).astype(BF16)

    def proj(i):
        return jnp.dot(h, wb[:, i * WIDTH:(i + 1) * WIDTH], preferred_element_type=F32)

    cos = _tile_lanes(cos_ref[...], WIDTH // 128)
    sin = _tile_lanes(sin_ref[...], WIDTH // 128)
    q_ref[0] = _rope(proj(2), cos, sin) * (HEAD_DIM ** -0.5)
    k_ref[0] = _rope(proj(3), cos, sin)
    v_ref[0] = proj(4)

    s_scores, s_self, s_vcols = _sample_scores(sqkv_ref, ck_ref, win)

    u = proj(0)
    vn = _group_rmsnorm(proj(1), ones_ref[...], gv_ref[...]).astype(BF16)

    lane = lax.broadcasted_iota(jnp.int32, (CHUNK, 128), 1)
    left = lane < HEAD_DIM
    row = lax.broadcasted_iota(jnp.int32, (CHUNK, 2 * CHUNK), 0)
    col = lax.broadcasted_iota(jnp.int32, (CHUNK, 2 * CHUNK), 1)
    causal = (col % CHUNK) <= row
    zero = jnp.zeros((CHUNK, 128), BF16)
    wps = [jnp.where(causal, wp_ref[gp], 0.0).astype(BF16) for gp in range(N_HEADS // 2)]
    def block_diag(vv):
        return jnp.concatenate([jnp.where(left, vv, zero), jnp.where(left, zero, vv)], axis=0)

    for c in range(0, tm // CHUNK, 2):
        rows = [slice((c + i) * CHUNK, (c + i + 1) * CHUNK) for i in range(2)]
        mixes = [[], []]
        for gp in range(N_HEADS // 2):
            lanes = slice(gp * 128, (gp + 1) * 128)
            v2 = jnp.concatenate([block_diag(vn[rows[0], lanes]), block_diag(vn[rows[1], lanes])], axis=1)
            both = jnp.dot(wps[gp], v2, preferred_element_type=F32)
            mixes[0].append(both[:, 0:128])
            mixes[1].append(both[:, 128:256])
        for i in range(2):
            mix = jnp.concatenate(mixes[i], axis=1) + bias_ref[...]
            a_ref[0, rows[i], :] = (u[rows[i], :] * mix).astype(a_ref.dtype)

    _sample_finish(s_scores, s_self, s_vcols, cv_ref, so_ref, win)

    @pl.when(pl.program_id(1) >= first_win_tile)
    def _():
        kt_ref[0] = k_ref[0].T
        vt_ref[0] = v_ref[0].T


def _prompt_proj(x, g1, w_in, cos, sin, gv, ones_bd, wp, bias, s_qkv, cache_k_t, cache_v_t, *, tm=512):
    B, S, _ = x.shape
    const2 = lambda b, j: (0, 0)
    out_sds = lambda dt: jax.ShapeDtypeStruct((B, S, WIDTH), dt)
    tile = pl.BlockSpec((1, tm, WIDTH), lambda b, j: (b, j, 0))
    win = min(MAX_WINDOW, S)
    first_win_tile = (S - win) // tm
    tile_t = pl.BlockSpec((1, WIDTH, tm), lambda b, j: (b, 0, jnp.maximum(j - first_win_tile, 0)))
    win_sds = jax.ShapeDtypeStruct((B, WIDTH, win), F32)
    n_seq, _, _, cache_win = cache_k_t.shape
    n_tiles = S // tm
    assert n_seq <= B * n_tiles, "one sample sequence rides on each grid step"
    seq_blk = lambda shape: pl.BlockSpec(
        (1,) + shape, lambda b, j: (jnp.minimum(b * n_tiles + j, n_seq - 1),) + (0,) * len(shape))
    cache_blk = seq_blk((N_HEADS, HEAD_DIM, cache_win))
    return pl.pallas_call(
        functools.partial(_proj_kernel, tm=tm, first_win_tile=first_win_tile, win=cache_win),
        grid=(B, n_tiles),
        in_specs=[
            pl.BlockSpec((1, tm, D_MODEL), lambda b, j: (b, j, 0)),
            pl.BlockSpec((1, D_MODEL), const2),
            pl.BlockSpec((D_MODEL, PROJ_COLS), const2, pipeline_mode=pl.Buffered(1)),
            pl.BlockSpec((tm, 128), lambda b, j: (j, 0)),
            pl.BlockSpec((tm, 128), lambda b, j: (j, 0)),
            pl.BlockSpec((1, WIDTH), const2),
            pl.BlockSpec((WIDTH, WIDTH), const2),
            pl.BlockSpec((N_HEADS // 2, CHUNK, 2 * CHUNK), lambda b, j: (0, 0, 0)),
            pl.BlockSpec((CHUNK, WIDTH), const2),
            seq_blk(s_qkv.shape[1:]), cache_blk, cache_blk,
        ],
        out_specs=[tile, tile, tile, tile, tile_t, tile_t, seq_blk((HEAD_DIM, N_HEADS))],
        out_shape=[out_sds(BF16), out_sds(F32), out_sds(F32), out_sds(F32), win_sds, win_sds,
                   jax.ShapeDtypeStruct((n_seq, HEAD_DIM, N_HEADS), F32)],
        scratch_shapes=[pltpu.VMEM((D_MODEL, PROJ_COLS), BF16)],
        compiler_params=pltpu.CompilerParams(
            dimension_semantics=("arbitrary", "arbitrary"), vmem_limit_bytes=V7X_VMEM_LIMIT),
        name="prompt_proj_sgu",
    )(x, g1, w_in, cos, sin, gv, ones_bd, wp, bias, s_qkv, cache_k_t, cache_v_t)


def _attn_kernel(q_ref, k_ref, v_ref, wg_ref, wu_ref, wd_ref, o_ref, w1_ref, w2_ref,
                 qd, kd, vd, res_o, res_l, nat_o, nat_l, bias, *, seq):
    n_tiles = seq // CHUNK
    last = len(DILATIONS) - 1
    stage_in = (res_o.at[last], res_l.at[last], nat_o.at[last - 1])
    stage_out = (res_o.at[last - 1], res_l.at[last - 1])

    w1_ref[:, :, 0:D_EXPERT] = wg_ref[...].astype(BF16)
    w1_ref[:, :, D_EXPERT:2 * D_EXPERT] = wu_ref[...].astype(BF16)
    w2_ref[...] = wd_ref[...].astype(BF16)

    lane = lax.broadcasted_iota(jnp.int32, (CHUNK, 128), 1)
    left = lane < HEAD_DIM
    qi2 = lax.broadcasted_iota(jnp.int32, (2 * CHUNK, 2 * CHUNK), 0) % CHUNK
    kj2 = lax.broadcasted_iota(jnp.int32, (2 * CHUNK, 2 * CHUNK), 1)
    dist2 = CHUNK + qi2 - kj2
    band2 = (dist2 >= 0) & (dist2 <= N_KEYS)
    zero_q = jnp.zeros((CHUNK, 128), BF16)
    bias[0] = jnp.where(band2, 0.0, NEG)
    bias[1] = jnp.where(band2 & (kj2 >= CHUNK), 0.0, NEG)

    kd[0:CHUNK, :] = jnp.zeros((CHUNK, 128), BF16)
    vd[0:CHUNK, :] = jnp.zeros((CHUNK, 128), BF16)

    for p, dil in enumerate(DILATIONS):
        sub = seq // dil
        nb = sub // CHUNK
        for ti, (src_ref, dst, pad) in enumerate(((q_ref, qd, 0), (k_ref, kd, CHUNK), (v_ref, vd, CHUNK))):
            if dil == 1:
                dst[pad:pad + seq, :] = src_ref[0].astype(BF16)
            elif dil == SUBLANE_STRIDE:
                for r in range(dil):
                    val = src_ref[0, pl.ds(r, sub, stride=dil), :]
                    stage_in[ti][r * sub:(r + 1) * sub, :] = val
                    dst[pad + r * sub:pad + (r + 1) * sub, :] = val.astype(BF16)
            else:
                coarse = seq // SUBLANE_STRIDE
                for r_lo in range(SUBLANE_STRIDE):
                    for r_hi in range(SUBLANE_STRIDE):
                        r = r_lo + SUBLANE_STRIDE * r_hi
                        val = stage_in[ti][pl.ds(r_lo * coarse + r_hi, sub, stride=SUBLANE_STRIDE), :]
                        dst[pad + r * sub:pad + (r + 1) * sub, :] = val.astype(BF16)

        def tile_body(g, i, p=p, nb=nb):
            t = g * TILES_PER_STEP + i
            row = pl.multiple_of(t * CHUNK, CHUNK)
            qt = qd[pl.ds(row, CHUNK), :]
            k2 = kd[pl.ds(row, 2 * CHUNK), :]
            v2 = vd[pl.ds(row, 2 * CHUNK), :]
            if TILES_PER_STEP % nb == 0:
                variant = 1 if i % nb == 0 else 0
            elif i == 0:
                variant = jnp.where((g * TILES_PER_STEP) % nb == 0, 1, 0)
            else:
                variant = 0
            q2 = jnp.concatenate([jnp.where(left, qt, zero_q), jnp.where(left, zero_q, qt)], axis=0)
            s = lax.dot_general(q2, k2, (((1,), (1,)), ((), ())), preferred_element_type=F32)
            s = s + bias[variant]
            m = jnp.max(s, axis=1, keepdims=True)
            e = jnp.exp(s - m)
            den = jnp.sum(e, axis=1, keepdims=True)
            pv = jnp.dot(e.astype(BF16), v2, preferred_element_type=F32) / den
            lse = jnp.broadcast_to(m + jnp.log(den), (2 * CHUNK, 128))
            res_o[p, pl.ds(row, CHUNK), :] = jnp.where(left, pv[0:CHUNK], pv[CHUNK:2 * CHUNK])
            res_l[p, pl.ds(row, CHUNK), :] = jnp.where(left, lse[0:CHUNK], lse[CHUNK:2 * CHUNK])

        def group_body(g, carry, tile_body=tile_body):
            for i in range(TILES_PER_STEP):
                tile_body(g, i)
            return carry

        lax.fori_loop(0, n_tiles // TILES_PER_STEP, group_body, 0)

    for p, dil in enumerate(DILATIONS):
        if dil == 1:
            continue
        sub = seq // dil
        for si, (res, nat) in enumerate(((res_o, nat_o), (res_l, nat_l))):
            if dil == SUBLANE_STRIDE:
                for r in range(dil):
                    nat[p - 1, pl.ds(r, sub, stride=dil), :] = res[p, r * sub:(r + 1) * sub, :]
            else:
                coarse = seq // SUBLANE_STRIDE
                for r_lo in range(SUBLANE_STRIDE):
                    for r_hi in range(SUBLANE_STRIDE):
                        r = r_lo + SUBLANE_STRIDE * r_hi
                        stage_out[si][pl.ds(r_lo * coarse + r_hi, sub, stride=SUBLANE_STRIDE), :] = (
                            res[p, r * sub:(r + 1) * sub, :])
                for r_lo in range(SUBLANE_STRIDE):
                    nat[p - 1, pl.ds(r_lo, coarse, stride=SUBLANE_STRIDE), :] = (
                        stage_out[si][r_lo * coarse:(r_lo + 1) * coarse, :])

    rows_per_step = 256

    def merge_body(c, carry):
        rows = pl.ds(pl.multiple_of(c * rows_per_step, rows_per_step), rows_per_step)
        l0, l1, l2 = res_l[0, rows, :], nat_l[0, rows, :], nat_l[1, rows, :]
        top = jnp.maximum(jnp.maximum(l0, l1), l2)
        w0, w1, w2 = jnp.exp(l0 - top), jnp.exp(l1 - top), jnp.exp(l2 - top)
        num = w0 * res_o[0, rows, :] + w1 * nat_o[0, rows, :] + w2 * nat_o[1, rows, :]
        o_ref[0, rows, :] = (num / (w0 + w1 + w2)).astype(o_ref.dtype)
        return carry

    lax.fori_loop(0, seq // rows_per_step, merge_body, 0)


def _prompt_attention(q, k, v, w_gate, w_up, w_down):
    B, S, _ = q.shape
    n_pairs = WIDTH // 128
    blk = pl.BlockSpec((1, S, 128), lambda b, hp: (b, 0, hp))
    n_steps = B * n_pairs
    assert N_EXPERTS % n_steps == 0
    e_blk = N_EXPERTS // n_steps
    expert_blk = lambda rows, cols: pl.BlockSpec((e_blk, rows, cols), lambda b, hp: (b * n_pairs + hp, 0, 0))
    return pl.pallas_call(
        functools.partial(_attn_kernel, seq=S),
        grid=(B, n_pairs),
        in_specs=[blk, blk, blk,
                  expert_blk(D_MODEL, D_EXPERT), expert_blk(D_MODEL, D_EXPERT), expert_blk(D_EXPERT, D_MODEL)],
        out_specs=[blk, expert_blk(D_MODEL, 2 * D_EXPERT), expert_blk(D_EXPERT, D_MODEL)],
        out_shape=[jax.ShapeDtypeStruct((B, S, WIDTH), BF16),
                   jax.ShapeDtypeStruct((N_EXPERTS, D_MODEL, 2 * D_EXPERT), BF16),
                   jax.ShapeDtypeStruct((N_EXPERTS, D_EXPERT, D_MODEL), BF16)],
        scratch_shapes=[
            pltpu.VMEM((S, 128), BF16),
            pltpu.VMEM((S + CHUNK, 128), BF16),
            pltpu.VMEM((S + CHUNK, 128), BF16),
            pltpu.VMEM((len(DILATIONS), S, 128), F32),
            pltpu.VMEM((len(DILATIONS), S, 128), F32),
            pltpu.VMEM((len(DILATIONS) - 1, S, 128), F32),
            pltpu.VMEM((len(DILATIONS) - 1, S, 128), F32),
            pltpu.VMEM((2, 2 * CHUNK, 2 * CHUNK), F32),
        ],
        compiler_params=pltpu.CompilerParams(
            dimension_semantics=("arbitrary", "arbitrary"), vmem_limit_bytes=V7X_VMEM_LIMIT),
        name="prompt_dilated_attention",
    )(q, k, v, w_gate, w_up, w_down)


def _route(logits):
    lane = lax.broadcasted_iota(jnp.int32, logits.shape, 1)
    big = jnp.int32(ROUTER_LANES)
    lg = jnp.where(lane < N_GROUPS, logits, NEG)
    gmax = jnp.max(lg, axis=1, keepdims=True)
    gp = 1.0 / jnp.sum(jnp.exp(lg - gmax), axis=1, keepdims=True)
    gi = jnp.min(jnp.where(lg == gmax, lane, big), axis=1, keepdims=True)
    lo = N_GROUPS + EXPERTS_PER_GROUP * gi
    le = jnp.where((lane >= lo) & (lane < lo + EXPERTS_PER_GROUP), logits, NEG)
    m1 = jnp.max(le, axis=1, keepdims=True)
    i1 = jnp.min(jnp.where(le == m1, lane, big), axis=1, keepdims=True)
    le2 = jnp.where(lane == i1, NEG, le)
    m2 = jnp.max(le2, axis=1, keepdims=True)
    i2 = jnp.min(jnp.where(le2 == m2, lane, big), axis=1, keepdims=True)
    e2 = jnp.exp(m2 - m1)
    w1 = 1.0 / (1.0 + e2)
    w2 = e2 / (1.0 + e2)
    return jnp.where(lane == i1, gp * w1, jnp.where(lane == i2, gp * w2, 0.0))


def _mix_router_kernel(a_ref, b_ref, x_ref, wo_ref, g2_ref, wr_ref, br_ref, xp_ref, h2_ref, gates_ref, *, precise):
    mixed = (_dot(a_ref[...], wo_ref[0:WIDTH, :], precise)
             + _dot(b_ref[...], wo_ref[WIDTH:2 * WIDTH, :], precise))
    xp = x_ref[...] + mixed
    xp_ref[...] = xp
    h2 = _rmsnorm(xp, g2_ref[...])
    h2_ref[...] = h2.astype(h2_ref.dtype)
    logits = jnp.dot(h2, wr_ref[...], preferred_element_type=F32,
                     precision=lax.Precision.HIGHEST) + br_ref[...]
    gates_ref[...] = _route(logits)


def _mix_router(a, b, x, w_out, g2, w_router, b_router, *, tm, precise=False):
    n = x.shape[0]
    const = lambda i: (0, 0)
    row_blk = lambda w: pl.BlockSpec((tm, w), lambda i: (i, 0))
    return pl.pallas_call(
        functools.partial(_mix_router_kernel, precise=precise),
        grid=(n // tm,),
        in_specs=[row_blk(WIDTH), row_blk(WIDTH), row_blk(D_MODEL),
                  pl.BlockSpec((2 * WIDTH, D_MODEL), const),
                  pl.BlockSpec((1, D_MODEL), const),
                  pl.BlockSpec((D_MODEL, ROUTER_LANES), const),
                  pl.BlockSpec((1, ROUTER_LANES), const)],
        out_specs=[row_blk(D_MODEL), row_blk(D_MODEL), row_blk(ROUTER_LANES)],
        out_shape=[jax.ShapeDtypeStruct((n, D_MODEL), F32),
                   jax.ShapeDtypeStruct((n, D_MODEL), BF16),
                   jax.ShapeDtypeStruct((n, ROUTER_LANES), F32)],
        compiler_params=pltpu.CompilerParams(
            dimension_semantics=("arbitrary",), vmem_limit_bytes=V7X_VMEM_LIMIT),
        name="outproj_router",
    )(a, b, x, w_out, g2, w_router, b_router)


def _nt_dot(w, t):
    return lax.dot_general(w, t, (((1,), (1,)), ((), ())), preferred_element_type=F32)


def _route_t(logits_t):
    row = lax.broadcasted_iota(jnp.int32, logits_t.shape, 0)
    big = jnp.int32(ROUTER_LANES)
    lg = jnp.where(row < N_GROUPS, logits_t, NEG)
    gmax = jnp.max(lg, axis=0, keepdims=True)
    gp = 1.0 / jnp.sum(jnp.exp(lg - gmax), axis=0, keepdims=True)
    gi = jnp.min(jnp.where(lg == gmax, row, big), axis=0, keepdims=True)
    lo = N_GROUPS + EXPERTS_PER_GROUP * gi
    le = jnp.where((row >= lo) & (row < lo + EXPERTS_PER_GROUP), logits_t, NEG)
    m1 = jnp.max(le, axis=0, keepdims=True)
    i1 = jnp.min(jnp.where(le == m1, row, big), axis=0, keepdims=True)
    le2 = jnp.where(row == i1, NEG, le)
    m2 = jnp.max(le2, axis=0, keepdims=True)
    i2 = jnp.min(jnp.where(le2 == m2, row, big), axis=0, keepdims=True)
    e2 = jnp.exp(m2 - m1)
    return i1 - N_GROUPS, i2 - N_GROUPS, gp / (1.0 + e2), gp * e2 / (1.0 + e2)


def _mix_route_sort_kernel(a_ref, b_ref, x_ref, wo_ref, g2_ref, wr_ref, brc_ref, tri_ref, ltri_ref,
                           xp_ref, h2_ref, mrow_ref, mcol_ref, tab_ref, wob):
    t = MOE_BLOCK

    @pl.when(pl.program_id(0) == 0)
    def _():
        wob[...] = wo_ref[...].astype(BF16)

    def project(j):
        rows = slice(j * t, (j + 1) * t)
        xp = x_ref[rows, :] + _dot(a_ref[rows, :], wob[0:WIDTH, :]) + _dot(b_ref[rows, :], wob[WIDTH:2 * WIDTH, :])
        xp_ref[rows, :] = xp
        h2 = _rmsnorm(xp, g2_ref[...])
        hi = h2.astype(BF16)
        h2_ref[rows, :] = hi
        return hi, (h2 - hi.astype(F32)).astype(BF16)

    def route(hi, lo):
        prod_hi = _nt_dot(wr_ref[...], hi)
        logits_t = (prod_hi[0:ROUTER_ROWS] + prod_hi[ROUTER_ROWS:2 * ROUTER_ROWS]
                    + _nt_dot(wr_ref[0:ROUTER_ROWS, :], lo) + brc_ref[...])
        return _route_t(logits_t)

    def sort_meta(j, ex1, ex2, gate1, gate2):
        rows = slice(j * t, (j + 1) * t)
        pair_e = jnp.concatenate([ex1, ex2], axis=1)
        row = lax.broadcasted_iota(jnp.int32, (N_EXPERTS, 2 * t), 0)
        onehot = jnp.where(row == pair_e, 1.0, 0.0)
        n_lane_tiles = 2 * t // 128
        local = _dot(jnp.concatenate([onehot[:, k * 128:(k + 1) * 128] for k in range(n_lane_tiles)], axis=0),
                     tri_ref[...])
        carry = jnp.zeros((N_EXPERTS, 1), F32)
        cums = []
        for k in range(n_lane_tiles):
            tile = local[k * N_EXPERTS:(k + 1) * N_EXPERTS, :]
            cums.append(tile + carry)
            carry = carry + tile[:, 127:128]
        cum = jnp.concatenate(cums, axis=1)
        rank = jnp.sum(onehot * cum, axis=0, keepdims=True) - 1.0
        counts = carry
        units32 = jnp.floor((counts + (MOE_ROW_ALIGN - 1)) * (1.0 / MOE_ROW_ALIGN))
        units = jnp.concatenate([jnp.broadcast_to(units32, (N_EXPERTS, 128)),
                                 jnp.zeros((ROUTER_LANES - N_EXPERTS, 128), F32)], axis=0)
        off = _dot(ltri_ref[...], units) * MOE_ROW_ALIGN
        dst = jnp.sum(onehot * off[0:N_EXPERTS, 0:1], axis=0, keepdims=True) + rank

        r8 = lax.broadcasted_iota(jnp.int32, (8, t), 0)
        mrow_ref[j] = jnp.where(r8 == 0, dst[:, 0:t], jnp.where(r8 == 1, dst[:, t:2 * t],
                                jnp.where(r8 == 2, gate1, jnp.where(r8 == 3, gate2, 0.0))))
        r128 = lax.broadcasted_iota(jnp.int32, (ROUTER_LANES, t), 0)
        meta = jnp.where(r128 == 0, dst[:, 0:t], jnp.where(r128 == 1, dst[:, t:2 * t],
                         jnp.where(r128 == 2, gate1, jnp.where(r128 == 3, gate2, 0.0))))
        mcol_ref[rows, :] = meta.T
        lane = lax.broadcasted_iota(jnp.int32, (ROUTER_LANES, 128), 1)
        n_rows = units * MOE_ROW_ALIGN
        chunks = jnp.floor((n_rows + (MOE_CHUNK - 1)) * (1.0 / MOE_CHUNK))
        tab_ref[j] = jnp.where(lane == 0, off, jnp.where(lane == 1, chunks, jnp.where(lane == 2, off + n_rows, 0.0)))

    blocks = range(MOE_ROUTE_BLOCKS)
    projected = [project(j) for j in blocks]
    routed = [route(hi, lo) for hi, lo in projected]
    for j in blocks:
        sort_meta(j, *routed[j])


def _mix_route_sort(a, b, x, w_out, g2, wr_hl, br_col, tri, ltri):
    n = x.shape[0]
    t = MOE_BLOCK
    nblk = n // t
    g = MOE_ROUTE_BLOCKS
    assert nblk % g == 0
    const = lambda i: (0, 0)
    row_blk = lambda w: pl.BlockSpec((g * t, w), lambda i: (i, 0))
    return pl.pallas_call(
        _mix_route_sort_kernel,
        grid=(nblk // g,),
        in_specs=[row_blk(WIDTH), row_blk(WIDTH), row_blk(D_MODEL),
                  pl.BlockSpec((2 * WIDTH, D_MODEL), const, pipeline_mode=pl.Buffered(1)),
                  pl.BlockSpec((1, D_MODEL), const),
                  pl.BlockSpec((2 * ROUTER_ROWS, D_MODEL), const),
                  pl.BlockSpec((ROUTER_ROWS, 1), const),
                  pl.BlockSpec((ROUTER_LANES, ROUTER_LANES), const),
                  pl.BlockSpec((ROUTER_LANES, ROUTER_LANES), const)],
        out_specs=[row_blk(D_MODEL), row_blk(D_MODEL),
                   pl.BlockSpec((g, 8, t), lambda i: (i, 0, 0)),
                   row_blk(ROUTER_LANES),
                   pl.BlockSpec((g, ROUTER_LANES, 128), lambda i: (i, 0, 0))],
        out_shape=[jax.ShapeDtypeStruct((n, D_MODEL), F32),
                   jax.ShapeDtypeStruct((n, D_MODEL), BF16),
                   jax.ShapeDtypeStruct((nblk, 8, t), F32),
                   jax.ShapeDtypeStruct((n, ROUTER_LANES), F32),
                   jax.ShapeDtypeStruct((nblk, ROUTER_LANES, 128), F32)],
        scratch_shapes=[pltpu.VMEM((2 * WIDTH, D_MODEL), BF16)],
        compiler_params=pltpu.CompilerParams(
            dimension_semantics=("arbitrary",), vmem_limit_bytes=V7X_VMEM_LIMIT),
        name="outproj_route_sort",
    )(a, b, x, w_out, g2, wr_hl, br_col, tri, ltri)


def _silu_mul(ab):
    a = ab[:, :D_EXPERT]
    return a * (1.0 / (1.0 + jnp.exp(-a))) * ab[:, D_EXPERT:]


def _chunk_offsets(tab, n_blk, e):
    return [pl.multiple_of(tab(j, e, 0), MOE_ROW_ALIGN) for j in range(n_blk)]


def _any_multi_chunk(tab, j):
    most = tab(j, 0, 1)
    for e in range(1, N_EXPERTS):
        most = jnp.maximum(most, tab(j, e, 1))
    return most > 1


def _moe_gate_up_kernel(tab_ref, h_ref, mrow_ref, w1_ref, sh_ref, sg_ref, hs_ref, shid_ref, xs):
    t = MOE_BLOCK
    n_blk = MOE_GATE_UP_BLOCKS
    first = pl.program_id(0) * n_blk
    tab = lambda j, e, c: tab_ref[((first + j) * N_EXPERTS + e) * 3 + c]

    piece = 512

    def gather(j, r0, rows):
        dst1 = mrow_ref[j, 0:1, :].astype(jnp.int32)
        dst2 = mrow_ref[j, 1:2, :].astype(jnp.int32)
        d_idx = lax.broadcasted_iota(jnp.int32, (rows, t), 0) + r0
        sel = jnp.where((d_idx == dst1) | (d_idx == dst2), 1.0, 0.0)
        xs[j, r0:r0 + rows, :] = _dot(sel, h_ref[j * t:(j + 1) * t, :]).astype(BF16)

    for j in range(n_blk):
        for r0 in range(0, MOE_GATHER_ROWS, piece):
            gather(j, r0, min(piece, MOE_GATHER_ROWS - r0))
        xs[j, MOE_GATHER_ROWS:MOE_ROWS, :] = jnp.zeros((MOE_ROWS - MOE_GATHER_ROWS, D_MODEL), BF16)
    hs_ref[...] = jnp.zeros(hs_ref.shape, BF16)
    for j in range(n_blk):
        @pl.when(tab(j, N_EXPERTS - 1, 2) > MOE_GATHER_ROWS)
        def _(j=j):
            gather(j, MOE_GATHER_ROWS, MOE_SORT_ROWS - MOE_GATHER_ROWS)

    def first_chunks(g, carry):
        for i in range(MOE_EXPERTS_PER_STEP):
            e = g * MOE_EXPERTS_PER_STEP + i
            offs = _chunk_offsets(tab, n_blk, e)
            x = jnp.concatenate([xs[j, pl.ds(offs[j], MOE_CHUNK), :] for j in range(n_blk)], axis=0)
            hid = _silu_mul(jnp.dot(x, w1_ref[e], preferred_element_type=F32)).astype(BF16)
            for j in range(n_blk):
                hs_ref[j, pl.ds(offs[j], MOE_CHUNK), :] = hid[j * MOE_CHUNK:(j + 1) * MOE_CHUNK, :]
        return carry

    lax.fori_loop(0, N_EXPERTS // MOE_EXPERTS_PER_STEP, first_chunks, 0)

    def more_chunks(j, e, carry):
        off, n_chunks, end = tab(j, e, 0), tab(j, e, 1), tab(j, e, 2)

        def chunk(c, carry):
            r0 = pl.multiple_of(off + c * MOE_CHUNK, MOE_ROW_ALIGN)
            rows = r0 + lax.broadcasted_iota(jnp.int32, (MOE_CHUNK, D_EXPERT), 0)
            hid = _silu_mul(jnp.dot(xs[j, pl.ds(r0, MOE_CHUNK), :], w1_ref[e], preferred_element_type=F32))
            hs_ref[j, pl.ds(r0, MOE_CHUNK), :] = jnp.where(rows < end, hid.astype(BF16),
                                                           hs_ref[j, pl.ds(r0, MOE_CHUNK), :])
            return carry

        return lax.fori_loop(1, n_chunks, chunk, carry)

    for j in range(n_blk):
        @pl.when(_any_multi_chunk(tab, j))
        def _(j=j):
            lax.fori_loop(0, N_EXPERTS, functools.partial(more_chunks, j), 0)

    @pl.when(pl.program_id(0) == 0)
    def _():
        sh = sh_ref[...]
        gates = sg_ref[...]
        for e in range(N_EXPERTS):
            gate = gates[:, N_GROUPS + e:N_GROUPS + e + 1]
            hid = _silu_mul(jnp.dot(sh, w1_ref[e], preferred_element_type=F32)) * gate
            shid_ref[:, e * D_EXPERT:(e + 1) * D_EXPERT] = hid.astype(BF16)


def _moe_gate_up(tab, h2, mrow, w1_b, s_h, s_gates):
    n = h2.shape[0]
    n_s = s_h.shape[0]
    nblk = n // MOE_BLOCK
    g = MOE_GATE_UP_BLOCKS
    assert nblk % g == 0
    whole = lambda shape: pl.BlockSpec(shape, lambda i, tab: (0,) * len(shape))
    return pl.pallas_call(
        _moe_gate_up_kernel,
        grid_spec=pltpu.PrefetchScalarGridSpec(
            num_scalar_prefetch=1,
            grid=(nblk // g,),
            in_specs=[pl.BlockSpec((g * MOE_BLOCK, D_MODEL), lambda i, tab: (i, 0)),
                      pl.BlockSpec((g, 8, MOE_BLOCK), lambda i, tab: (i, 0, 0)),
                      pl.BlockSpec(w1_b.shape, lambda i, tab: (0, 0, 0), pipeline_mode=pl.Buffered(1)),
                      whole(s_h.shape), whole(s_gates.shape)],
            out_specs=[pl.BlockSpec((g, MOE_ROWS, D_EXPERT), lambda i, tab: (i, 0, 0)),
                       whole((n_s, N_EXPERTS * D_EXPERT))],
            scratch_shapes=[pltpu.VMEM((g, MOE_ROWS, D_MODEL), BF16)]),
        out_shape=[jax.ShapeDtypeStruct((nblk, MOE_ROWS, D_EXPERT), BF16),
                   jax.ShapeDtypeStruct((n_s, N_EXPERTS * D_EXPERT), BF16)],
        compiler_params=pltpu.CompilerParams(
            dimension_semantics=("arbitrary",), vmem_limit_bytes=V7X_VMEM_LIMIT),
        name="moe_gate_up",
    )(tab, h2, mrow, w1_b, s_h, s_gates)


def _moe_down_kernel(tab_ref, hs_ref, xp_ref, mcol_ref, w2_ref, gf_ref, shid_ref, sx_ref, y_ref, sy_ref, os):
    t = MOE_BLOCK
    n_blk = MOE_DOWN_BLOCKS
    first = pl.program_id(0) * n_blk
    tab = lambda j, e, c: tab_ref[((first + j) * N_EXPERTS + e) * 3 + c]
    os[...] = jnp.zeros(os.shape, BF16)

    def first_chunks(g, carry):
        for i in range(MOE_EXPERTS_PER_STEP):
            e = g * MOE_EXPERTS_PER_STEP + i
            offs = _chunk_offsets(tab, n_blk, e)
            hid = jnp.concatenate([hs_ref[j, pl.ds(offs[j], MOE_CHUNK), :] for j in range(n_blk)], axis=0)
            out = jnp.dot(hid, w2_ref[e], preferred_element_type=F32).astype(BF16)
            for j in range(n_blk):
                os[j, pl.ds(offs[j], MOE_CHUNK), :] = out[j * MOE_CHUNK:(j + 1) * MOE_CHUNK, :]
        return carry

    lax.fori_loop(0, N_EXPERTS // MOE_EXPERTS_PER_STEP, first_chunks, 0)

    def more_chunks(j, e, carry):
        off, n_chunks, end = tab(j, e, 0), tab(j, e, 1), tab(j, e, 2)

        def chunk(c, carry):
            r0 = pl.multiple_of(off + c * MOE_CHUNK, MOE_ROW_ALIGN)
            rows = r0 + lax.broadcasted_iota(jnp.int32, (MOE_CHUNK, D_MODEL), 0)
            out = jnp.dot(hs_ref[j, pl.ds(r0, MOE_CHUNK), :], w2_ref[e], preferred_element_type=F32)
            os[j, pl.ds(r0, MOE_CHUNK), :] = jnp.where(rows < end, out.astype(BF16), os[j, pl.ds(r0, MOE_CHUNK), :])
            return carry

        return lax.fori_loop(1, n_chunks, chunk, carry)

    for j in range(n_blk):
        @pl.when(_any_multi_chunk(tab, j))
        def _(j=j):
            lax.fori_loop(0, N_EXPERTS, functools.partial(more_chunks, j), 0)

    l_idx = lax.broadcasted_iota(jnp.int32, (t, MOE_SORT_ROWS), 1)

    def scatter_matrix(j):
        mcol = mcol_ref[j * t:(j + 1) * t, :]
        d1c = mcol[:, 0:1].astype(jnp.int32)
        d2c = mcol[:, 1:2].astype(jnp.int32)
        comb = jnp.where(l_idx == d1c, mcol[:, 2:3], 0.0) + jnp.where(l_idx == d2c, mcol[:, 3:4], 0.0)
        return comb.astype(BF16)

    combs = [scatter_matrix(j) for j in range(n_blk)]
    moes = [_dot(combs[j], os[j, 0:MOE_SORT_ROWS, :]) for j in range(n_blk)]
    for j in range(n_blk):
        rows = slice(j * t, (j + 1) * t)
        y_ref[rows, :] = _rmsnorm(xp_ref[rows, :] + moes[j], gf_ref[...])

    @pl.when(pl.program_id(0) == 0)
    def _():
        w2_all = w2_ref[...].reshape(N_EXPERTS * D_EXPERT, D_MODEL)
        ys = sx_ref[...] + jnp.dot(shid_ref[...], w2_all, preferred_element_type=F32)
        sy_ref[...] = _rmsnorm(ys, gf_ref[...])


def _moe_down(tab, hs_sorted, xp, mcol, w2_b, gf, s_hid, s_x):
    n = xp.shape[0]
    n_s = s_x.shape[0]
    nblk = n // MOE_BLOCK
    g = MOE_DOWN_BLOCKS
    assert nblk % g == 0
    whole = lambda shape: pl.BlockSpec(shape, lambda i, tab: (0,) * len(shape))
    row_blk = lambda w: pl.BlockSpec((g * MOE_BLOCK, w), lambda i, tab: (i, 0))
    return pl.pallas_call(
        _moe_down_kernel,
        grid_spec=pltpu.PrefetchScalarGridSpec(
            num_scalar_prefetch=1,
            grid=(nblk // g,),
            in_specs=[pl.BlockSpec((g, MOE_ROWS, D_EXPERT), lambda i, tab: (i, 0, 0)),
                      row_blk(D_MODEL), row_blk(ROUTER_LANES),
                      pl.BlockSpec(w2_b.shape, lambda i, tab: (0, 0, 0), pipeline_mode=pl.Buffered(1)),
                      whole((1, D_MODEL)), whole(s_hid.shape), whole(s_x.shape)],
            out_specs=[row_blk(D_MODEL), whole((n_s, D_MODEL))],
            scratch_shapes=[pltpu.VMEM((g, MOE_ROWS, D_MODEL), BF16)]),
        out_shape=[jax.ShapeDtypeStruct((n, D_MODEL), F32), jax.ShapeDtypeStruct((n_s, D_MODEL), F32)],
        compiler_params=pltpu.CompilerParams(
            dimension_semantics=("arbitrary",), vmem_limit_bytes=V7X_VMEM_LIMIT),
        name="moe_down_combine",
    )(tab, hs_sorted, xp, mcol, w2_b, gf, s_hid, s_x)


def _sample_proj_kernel(x_ref, g1_ref, w_ref, cos_ref, sin_ref, gv_ref, ones_ref, w00_ref, b0_ref,
                        rep_ref, foldt_ref, a_ref, k_ref, v_ref, vn_ref, qkvt_ref):
    h = _rmsnorm(x_ref[...], g1_ref[...])

    def proj(i):
        return _dot(h, w_ref[:, i * WIDTH:(i + 1) * WIDTH], precise=True)

    cos = _tile_lanes(cos_ref[...], WIDTH // 128)
    sin = _tile_lanes(sin_ref[...], WIDTH // 128)
    q = _rope(proj(2), cos, sin) * (HEAD_DIM ** -0.5)
    k = _rope(proj(3), cos, sin)
    v = proj(4)
    vn = _group_rmsnorm(proj(1), ones_ref[...], gv_ref[...], precise=True)
    a_ref[...] = proj(0) * (w00_ref[...] * vn + b0_ref[...])
    k_ref[...] = k
    v_ref[...] = v
    vn_ref[...] = vn

    n_rep = rep_ref.shape[0]
    r_idx = lax.broadcasted_iota(jnp.int32, (n_rep, WIDTH), 0)
    l_idx = lax.broadcasted_iota(jnp.int32, (n_rep, WIDTH), 1)
    own = (l_idx // HEAD_DIM) == (r_idx % N_HEADS)
    for t, src in enumerate((q, k, v)):
        rep = _dot(rep_ref[...], src, precise=True)
        qkvt_ref[t * HEAD_DIM:(t + 1) * HEAD_DIM, :] = lax.dot_general(
            foldt_ref[...], jnp.where(own, rep, 0.0), (((1,), (1,)), ((), ())),
            preferred_element_type=F32, precision=lax.Precision.HIGHEST)


def _sample_proj(x, g1, w_in, cos, sin, gv, ones_bd, w00, b0, rep, foldt):
    bd = x.shape[0]
    sds = lambda r, c: jax.ShapeDtypeStruct((r, c), F32)
    return pl.pallas_call(
        _sample_proj_kernel,
        out_shape=[sds(bd, WIDTH), sds(bd, WIDTH), sds(bd, WIDTH), sds(bd, WIDTH),
                   sds(3 * HEAD_DIM, bd * N_HEADS)],
        compiler_params=pltpu.CompilerParams(vmem_limit_bytes=V7X_VMEM_LIMIT),
        name="sample_proj",
    )(x, g1, w_in, cos, sin, gv, ones_bd, w00, b0, rep, foldt)


def _sample_scores(qkv_ref, k_ref, win):
    cols = qkv_ref[0]

    head = lax.broadcasted_iota(jnp.int32, (N_HEADS, win), 0)
    s = jnp.zeros((N_HEADS, win), F32)
    s_self = jnp.zeros((N_HEADS, 1), F32)
    for h in range(N_HEADS):
        qc = cols[0:HEAD_DIM, h:h + 1]
        kc = cols[HEAD_DIM:2 * HEAD_DIM, h:h + 1]
        s_h = jnp.sum(k_ref[0, h] * qc, axis=0, keepdims=True)
        s = jnp.where(head == h, s_h, s)
        s_self = jnp.where(head[:, 0:1] == h, jnp.sum(qc * kc, axis=0, keepdims=True), s_self)
    return s, s_self, cols[2 * HEAD_DIM:3 * HEAD_DIM, :]


def _sample_finish(s, s_self, v_cols, v_ref, o_ref, win):
    dist = win - lax.broadcasted_iota(jnp.int32, (1, win), 1)
    members = [(dist <= N_KEYS * dil) & (dist % dil == 0) for dil in DILATIONS]
    es, e_selfs, dens, lses = [], [], [], []
    for mem in members:
        sm = jnp.where(mem, s, NEG)
        m = jnp.maximum(jnp.max(sm, axis=1, keepdims=True), s_self)
        e = jnp.exp(sm - m)
        e_self = jnp.exp(s_self - m)
        den = jnp.sum(e, axis=1, keepdims=True) + e_self
        es.append(e)
        e_selfs.append(e_self)
        dens.append(den)
        lses.append(m + jnp.log(den))
    top = jnp.maximum(jnp.maximum(lses[0], lses[1]), lses[2])
    ws = [jnp.exp(l - top) for l in lses]
    wsum = ws[0] + ws[1] + ws[2]
    coef = [w / (den * wsum) for w, den in zip(ws, dens)]
    p_keys = coef[0] * es[0] + coef[1] * es[1] + coef[2] * es[2]
    p_self = coef[0] * e_selfs[0] + coef[1] * e_selfs[1] + coef[2] * e_selfs[2]

    for h in range(N_HEADS):
        o_ref[0, :, h:h + 1] = (jnp.sum(v_ref[0, h] * p_keys[h:h + 1, :], axis=1, keepdims=True)
                                + p_self[h:h + 1, :] * v_cols[:, h:h + 1])


def _rope_tables(first_pos, count):
    half = HEAD_DIM // 2
    inv = ROPE_THETA ** (-np.arange(half, dtype=np.float64) * 2.0 / HEAD_DIM)
    ang = (first_pos + np.arange(count, dtype=np.float64))[:, None] * inv[None, :]
    cos, sin = np.cos(ang), np.sin(ang)
    cos128 = np.concatenate([cos, cos, cos, cos], axis=1).astype(np.float32)
    sin128 = np.concatenate([-sin, sin, -sin, sin], axis=1).astype(np.float32)
    return jnp.asarray(cos128), jnp.asarray(sin128)


def kernel(x_prompt, x_sample, cache_win_k, cache_win_v, ln1_g, w_in, sgu_norm_g, sgu_w, sgu_b, w_out, ln2_g,
           w_router_group, b_router_group, w_router_expert, b_router_expert, w_gate, w_up, w_down, lnf_g):
    depth = w_in.shape[0]
    assert depth == 1 and x_sample.shape[1] == 1
    B, S, _ = x_prompt.shape
    bd = x_sample.shape[0]
    win = cache_win_k.shape[2]
    assert S % (max(DILATIONS) * CHUNK) == 0 and win >= max(DILATIONS) * N_KEYS and PAST_LEN % CHUNK == 0
    l = 0

    pad = ROUTER_LANES - N_GROUPS - N_EXPERTS
    w_router = jnp.pad(jnp.concatenate([w_router_group[l], w_router_expert[l]], axis=1), ((0, 0), (0, pad)))
    b_router = jnp.pad(jnp.concatenate([b_router_group[l], b_router_expert[l]]), (0, pad))[None, :]
    g1 = ln1_g[l][None, :]
    g2 = ln2_g[l][None, :]
    gf = lnf_g[None, :]
    gv = sgu_norm_g[l].reshape(1, WIDTH)
    grp = np.arange(WIDTH) // HEAD_DIM
    ones_bd = jnp.asarray(np.where(grp[:, None] == grp[None, :], 1.0 / HEAD_DIM, 0.0), BF16)
    wp = jnp.concatenate([sgu_w[l][0::2], sgu_w[l][1::2]], axis=-1)
    bias = jnp.repeat(sgu_b[l].T, HEAD_DIM, axis=1)
    w00 = jnp.repeat(sgu_w[l][:, 0, 0], HEAD_DIM)[None, :]
    b0 = jnp.repeat(sgu_b[l][:, 0], HEAD_DIM)[None, :]

    cos_s, sin_s = _rope_tables(PAST_LEN, 1)
    rep = jnp.asarray(np.arange(bd * N_HEADS)[:, None] // N_HEADS == np.arange(bd)[None, :], F32)
    foldt = jnp.asarray(np.arange(HEAD_DIM)[:, None] == np.arange(WIDTH)[None, :] % HEAD_DIM, F32)
    xs = x_sample.reshape(bd, D_MODEL)
    a_s, k_s, v_s, vn_s, qkvt = _sample_proj(xs, g1, w_in[l], cos_s, sin_s, gv, ones_bd, w00, b0, rep, foldt)
    to_pos_minor = lambda c: jnp.transpose(c, (0, 2, 3, 1))

    cos_p, sin_p = _rope_tables(0, S)
    qkv_seq = jnp.transpose(qkvt.reshape(3 * HEAD_DIM, bd, N_HEADS), (1, 0, 2))
    a_p, q_p, k_p, v_p, kt_p, vt_p, o3 = _prompt_proj(
        x_prompt, g1, w_in[l], cos_p, sin_p, gv, ones_bd, wp, bias,
        qkv_seq, to_pos_minor(cache_win_k[l]), to_pos_minor(cache_win_v[l]))
    b_p, w1_b, w2_b = _prompt_attention(q_p, k_p, v_p, w_gate[l], w_up[l], w_down[l])
    n = B * S
    assert n % MOE_BLOCK == 0
    wr_t = w_router.T[:ROUTER_ROWS]
    wr_hi = wr_t.astype(BF16)
    wr_hl = jnp.concatenate([wr_hi, (wr_t - wr_hi.astype(F32)).astype(BF16)], axis=0)
    lane_idx = np.arange(ROUTER_LANES)
    tri = jnp.asarray(lane_idx[:, None] <= lane_idx[None, :], BF16)
    ltri = jnp.asarray(lane_idx[None, :] < lane_idx[:, None], BF16)
    xp2, h2, mrow, mcol, tab_f = _mix_route_sort(
        a_p.reshape(n, WIDTH), b_p.reshape(n, WIDTH), x_prompt.reshape(n, D_MODEL),
        w_out[l], g2, wr_hl, b_router.reshape(ROUTER_LANES, 1)[:ROUTER_ROWS], tri, ltri)
    tab = tab_f[:, :N_EXPERTS, 0:3].astype(jnp.int32).reshape(-1)

    b_s = jnp.transpose(o3, (0, 2, 1)).reshape(bd, WIDTH)
    xs2, hs2, gates_s = _mix_router(a_s, b_s, xs, w_out[l], g2, w_router, b_router, tm=bd, precise=True)

    hid_sorted, hid_s = _moe_gate_up(tab, h2, mrow, w1_b, hs2, gates_s)
    y_prompt, y_sample = _moe_down(tab, hid_sorted, xp2, mcol, w2_b, gf, hid_s, xs2)
    y_prompt = y_prompt.reshape(B, S, D_MODEL)
    y_sample = y_sample.reshape(bd, 1, D_MODEL)
    buf_p = min(MAX_WINDOW, S)
    to_win = lambda t: jnp.transpose(t.reshape(1, B, N_HEADS, HEAD_DIM, buf_p), (0, 1, 4, 2, 3))
    new_k_p = to_win(kt_p)
    new_v_p = to_win(vt_p)

    shape_s = (1, bd, 1, N_HEADS, HEAD_DIM)
    return (y_prompt, y_sample, new_k_p, new_v_p,
            k_s.reshape(shape_s), v_s.reshape(shape_s), vn_s.reshape(shape_s))
```

```python
import functools

import jax
import jax.numpy as jnp
import numpy as np
from jax import lax
from jax.experimental import pallas as pl
from jax.experimental.pallas import tpu as pltpu

F32 = jnp.float32
BF16 = jnp.bfloat16

D_MODEL = 1024
HEAD_DIM = 64
N_HEADS = 8
WIDTH = N_HEADS * HEAD_DIM
PROJ_COLS = 5 * WIDTH
CHUNK = 128
DILATIONS = (1, 4, 16)
N_KEYS = 128
MAX_WINDOW = 2048
PAST_LEN = 16384
ROPE_THETA = 10000.0
N_GROUPS = 4
EXPERTS_PER_GROUP = 8
N_EXPERTS = N_GROUPS * EXPERTS_PER_GROUP
D_EXPERT = 128
EPS = 1e-6
NEG = -1e30
TILES_PER_STEP = 32
SUBLANE_STRIDE = 4
assert DILATIONS == (1, SUBLANE_STRIDE, SUBLANE_STRIDE ** 2)
MOE_BLOCK = 512
MOE_ROW_ALIGN = 16
MOE_CHUNK = 48
MOE_EXPERTS_PER_STEP = 16
MOE_ROUTE_BLOCKS = 2
MOE_GATE_UP_BLOCKS = 4
MOE_DOWN_BLOCKS = 2
MOE_SORT_ROWS = -(-(2 * MOE_BLOCK + N_EXPERTS * (MOE_ROW_ALIGN - 1)) // 512) * 512
MOE_ROWS = MOE_SORT_ROWS + MOE_CHUNK
MOE_GATHER_ROWS = 2 * MOE_BLOCK + N_EXPERTS * (MOE_ROW_ALIGN - 1) * 2 // 3
assert MOE_GATHER_ROWS % MOE_ROW_ALIGN == 0 and MOE_GATHER_ROWS < MOE_SORT_ROWS
ROUTER_ROWS = 48
ROUTER_LANES = 128
V7X_VMEM_LIMIT = 56 * 1024 * 1024


def _rmsnorm(x, g):
    return x * lax.rsqrt(jnp.mean(x * x, axis=-1, keepdims=True) + EPS) * g


def _tile_lanes(t, reps):
    return jnp.concatenate([t] * reps, axis=1)


def _rope(t, cos, sin_signed):
    lane = lax.broadcasted_iota(jnp.int32, t.shape, 1)
    first_half = (lane % HEAD_DIM) < (HEAD_DIM // 2)
    n = t.shape[1]
    partner = jnp.where(first_half, pltpu.roll(t, n - HEAD_DIM // 2, 1), pltpu.roll(t, HEAD_DIM // 2, 1))
    return t * cos + partner * sin_signed


def _dot(a, b, precise=False):
    if precise:
        return jnp.dot(a.astype(F32), b.astype(F32), preferred_element_type=F32,
                       precision=lax.Precision.HIGHEST)
    return jnp.dot(a.astype(BF16), b.astype(BF16), preferred_element_type=F32)


def _group_rmsnorm(va, ones_bd, gv, precise=False):
    ms = _dot(va * va, ones_bd, precise)
    return va * lax.rsqrt(ms + EPS) * gv


def _proj_kernel(x_ref, g1_ref, w_ref, cos_ref, sin_ref, gv_ref, ones_ref, wp_ref, bias_ref,
                 sqkv_ref, ck_ref, cv_ref,
                 a_ref, q_ref, k_ref, v_ref, kt_ref, vt_ref, so_ref, wb, *, tm, first_win_tile, win):
    @pl.when((pl.program_id(0) == 0) & (pl.program_id(1) == 0))
    def _():
        wb[...] = w_ref[...].astype(BF16)

    h = _rmsnorm(x_ref[0], g1_ref[...]).astype(BF16)

    def proj(i):
        return jnp.dot(h, wb[:, i * WIDTH:(i + 1) * WIDTH], preferred_element_type=F32)

    cos = _tile_lanes(cos_ref[...], WIDTH // 128)
    sin = _tile_lanes(sin_ref[...], WIDTH // 128)
    q_ref[0] = _rope(proj(2), cos, sin) * (HEAD_DIM ** -0.5)
    k_ref[0] = _rope(proj(3), cos, sin)
    v_ref[0] = proj(4)

    s_scores, s_self, s_vcols = _sample_scores(sqkv_ref, ck_ref, win)

    u = proj(0)
    vn = _group_rmsnorm(proj(1), ones_ref[...], gv_ref[...]).astype(BF16)

    lane = lax.broadcasted_iota(jnp.int32, (CHUNK, 128), 1)
    left = lane < HEAD_DIM
    row = lax.broadcasted_iota(jnp.int32, (CHUNK, 2 * CHUNK), 0)
    col = lax.broadcasted_iota(jnp.int32, (CHUNK, 2 * CHUNK), 1)
    causal = (col % CHUNK) <= row
    zero = jnp.zeros((CHUNK, 128), BF16)
    wps = [jnp.where(causal, wp_ref[gp], 0.0).astype(BF16) for gp in range(N_HEADS // 2)]
    def block_diag(vv):
        return jnp.concatenate([jnp.where(left, vv, zero), jnp.where(left, zero, vv)], axis=0)

    for c in range(0, tm // CHUNK, 2):
        rows = [slice((c + i) * CHUNK, (c + i + 1) * CHUNK) for i in range(2)]
        mixes = [[], []]
        for gp in range(N_HEADS // 2):
            lanes = slice(gp * 128, (gp + 1) * 128)
            v2 = jnp.concatenate([block_diag(vn[rows[0], lanes]), block_diag(vn[rows[1], lanes])], axis=1)
            both = jnp.dot(wps[gp], v2, preferred_element_type=F32)
            mixes[0].append(both[:, 0:128])
            mixes[1].append(both[:, 128:256])
        for i in range(2):
            mix = jnp.concatenate(mixes[i], axis=1) + bias_ref[...]
            a_ref[0, rows[i], :] = (u[rows[i], :] * mix).astype(a_ref.dtype)

    _sample_finish(s_scores, s_self, s_vcols, cv_ref, so_ref, win)

    @pl.when(pl.program_id(1) >= first_win_tile)
    def _():
        kt_ref[0] = k_ref[0].T
        vt_ref[0] = v_ref[0].T


def _prompt_proj(x, g1, w_in, cos, sin, gv, ones_bd, wp, bias, s_qkv, cache_k_t, cache_v_t, *, tm=512):
    B, S, _ = x.shape
    const2 = lambda b, j: (0, 0)
    out_sds = lambda dt: jax.ShapeDtypeStruct((B, S, WIDTH), dt)
    tile = pl.BlockSpec((1, tm, WIDTH), lambda b, j: (b, j, 0))
    win = min(MAX_WINDOW, S)
    first_win_tile = (S - win) // tm
    tile_t = pl.BlockSpec((1, WIDTH, tm), lambda b, j: (b, 0, jnp.maximum(j - first_win_tile, 0)))
    win_sds = jax.ShapeDtypeStruct((B, WIDTH, win), F32)
    n_seq, _, _, cache_win = cache_k_t.shape
    n_tiles = S // tm
    assert n_seq <= B * n_tiles, "one sample sequence rides on each grid step"
    seq_blk = lambda shape: pl.BlockSpec(
        (1,) + shape, lambda b, j: (jnp.minimum(b * n_tiles + j, n_seq - 1),) + (0,) * len(shape))
    cache_blk = seq_blk((N_HEADS, HEAD_DIM, cache_win))
    return pl.pallas_call(
        functools.partial(_proj_kernel, tm=tm, first_win_tile=first_win_tile, win=cache_win),
        grid=(B, n_tiles),
        in_specs=[
            pl.BlockSpec((1, tm, D_MODEL), lambda b, j: (b, j, 0)),
            pl.BlockSpec((1, D_MODEL), const2),
            pl.BlockSpec((D_MODEL, PROJ_COLS), const2, pipeline_mode=pl.Buffered(1)),
            pl.BlockSpec((tm, 128), lambda b, j: (j, 0)),
            pl.BlockSpec((tm, 128), lambda b, j: (j, 0)),
            pl.BlockSpec((1, WIDTH), const2),
            pl.BlockSpec((WIDTH, WIDTH), const2),
            pl.BlockSpec((N_HEADS // 2, CHUNK, 2 * CHUNK), lambda b, j: (0, 0, 0)),
            pl.BlockSpec((CHUNK, WIDTH), const2),
            seq_blk(s_qkv.shape[1:]), cache_blk, cache_blk,
        ],
        out_specs=[tile, tile, tile, tile, tile_t, tile_t, seq_blk((HEAD_DIM, N_HEADS))],
        out_shape=[out_sds(BF16), out_sds(F32), out_sds(F32), out_sds(F32), win_sds, win_sds,
                   jax.ShapeDtypeStruct((n_seq, HEAD_DIM, N_HEADS), F32)],
        scratch_shapes=[pltpu.VMEM((D_MODEL, PROJ_COLS), BF16)],
        compiler_params=pltpu.CompilerParams(
            dimension_semantics=("arbitrary", "arbitrary"), vmem_limit_bytes=V7X_VMEM_LIMIT),
        name="prompt_proj_sgu",
    )(x, g1, w_in, cos, sin, gv, ones_bd, wp, bias, s_qkv, cache_k_t, cache_v_t)


def _attn_kernel(q_ref, k_ref, v_ref, wg_ref, wu_ref, wd_ref, o_ref, w1_ref, w2_ref,
                 qd, kd, vd, res_o, res_l, nat_o, nat_l, bias, *, seq):
    n_tiles = seq // CHUNK
    last = len(DILATIONS) - 1
    stage_in = (res_o.at[last], res_l.at[last], nat_o.at[last - 1])
    stage_out = (res_o.at[last - 1], res_l.at[last - 1])

    w1_ref[:, :, 0:D_EXPERT] = wg_ref[...].astype(BF16)
    w1_ref[:, :, D_EXPERT:2 * D_EXPERT] = wu_ref[...].astype(BF16)
    w2_ref[...] = wd_ref[...].astype(BF16)

    lane = lax.broadcasted_iota(jnp.int32, (CHUNK, 128), 1)
    left = lane < HEAD_DIM
    qi2 = lax.broadcasted_iota(jnp.int32, (2 * CHUNK, 2 * CHUNK), 0) % CHUNK
    kj2 = lax.broadcasted_iota(jnp.int32, (2 * CHUNK, 2 * CHUNK), 1)
    dist2 = CHUNK + qi2 - kj2
    band2 = (dist2 >= 0) & (dist2 <= N_KEYS)
    zero_q = jnp.zeros((CHUNK, 128), BF16)
    bias[0] = jnp.where(band2, 0.0, NEG)
    bias[1] = jnp.where(band2 & (kj2 >= CHUNK), 0.0, NEG)

    kd[0:CHUNK, :] = jnp.zeros((CHUNK, 128), BF16)
    vd[0:CHUNK, :] = jnp.zeros((CHUNK, 128), BF16)

    for p, dil in enumerate(DILATIONS):
        sub = seq // dil
        nb = sub // CHUNK
        for ti, (src_ref, dst, pad) in enumerate(((q_ref, qd, 0), (k_ref, kd, CHUNK), (v_ref, vd, CHUNK))):
            if dil == 1:
                dst[pad:pad + seq, :] = src_ref[0].astype(BF16)
            elif dil == SUBLANE_STRIDE:
                for r in range(dil):
                    val = src_ref[0, pl.ds(r, sub, stride=dil), :]
                    stage_in[ti][r * sub:(r + 1) * sub, :] = val
                    dst[pad + r * sub:pad + (r + 1) * sub, :] = val.astype(BF16)
            else:
                coarse = seq // SUBLANE_STRIDE
                for r_lo in range(SUBLANE_STRIDE):
                    for r_hi in range(SUBLANE_STRIDE):
                        r = r_lo + SUBLANE_STRIDE * r_hi
                        val = stage_in[ti][pl.ds(r_lo * coarse + r_hi, sub, stride=SUBLANE_STRIDE), :]
                        dst[pad + r * sub:pad + (r + 1) * sub, :] = val.astype(BF16)

        def tile_body(g, i, p=p, nb=nb):
            t = g * TILES_PER_STEP + i
            row = pl.multiple_of(t * CHUNK, CHUNK)
            qt = qd[pl.ds(row, CHUNK), :]
            k2 = kd[pl.ds(row, 2 * CHUNK), :]
            v2 = vd[pl.ds(row, 2 * CHUNK), :]
            if TILES_PER_STEP % nb == 0:
                variant = 1 if i % nb == 0 else 0
            elif i == 0:
                variant = jnp.where((g * TILES_PER_STEP) % nb == 0, 1, 0)
            else:
                variant = 0
            q2 = jnp.concatenate([jnp.where(left, qt, zero_q), jnp.where(left, zero_q, qt)], axis=0)
            s = lax.dot_general(q2, k2, (((1,), (1,)), ((), ())), preferred_element_type=F32)
            s = s + bias[variant]
            m = jnp.max(s, axis=1, keepdims=True)
            e = jnp.exp(s - m)
            den = jnp.sum(e, axis=1, keepdims=True)
            pv = jnp.dot(e.astype(BF16), v2, preferred_element_type=F32) / den
            lse = jnp.broadcast_to(m + jnp.log(den), (2 * CHUNK, 128))
            res_o[p, pl.ds(row, CHUNK), :] = jnp.where(left, pv[0:CHUNK], pv[CHUNK:2 * CHUNK])
            res_l[p, pl.ds(row, CHUNK), :] = jnp.where(left, lse[0:CHUNK], lse[CHUNK:2 * CHUNK])

        def group_body(g, carry, tile_body=tile_body):
            for i in range(TILES_PER_STEP):
                tile_body(g, i)
            return carry

        lax.fori_loop(0, n_tiles // TILES_PER_STEP, group_body, 0)

    for p, dil in enumerate(DILATIONS):
        if dil == 1:
            continue
        sub = seq // dil
        for si, (res, nat) in enumerate(((res_o, nat_o), (res_l, nat_l))):
            if dil == SUBLANE_STRIDE:
                for r in range(dil):
                    nat[p - 1, pl.ds(r, sub, stride=dil), :] = res[p, r * sub:(r + 1) * sub, :]
            else:
                coarse = seq // SUBLANE_STRIDE
                for r_lo in range(SUBLANE_STRIDE):
                    for r_hi in range(SUBLANE_STRIDE):
                        r = r_lo + SUBLANE_STRIDE * r_hi
                        stage_out[si][pl.ds(r_lo * coarse + r_hi, sub, stride=SUBLANE_STRIDE), :] = (
                            res[p, r * sub:(r + 1) * sub, :])
                for r_lo in range(SUBLANE_STRIDE):
                    nat[p - 1, pl.ds(r_lo, coarse, stride=SUBLANE_STRIDE), :] = (
                        stage_out[si][r_lo * coarse:(r_lo + 1) * coarse, :])

    rows_per_step = 256

    def merge_body(c, carry):
        rows = pl.ds(pl.multiple_of(c * rows_per_step, rows_per_step), rows_per_step)
        l0, l1, l2 = res_l[0, rows, :], nat_l[0, rows, :], nat_l[1, rows, :]
        top = jnp.maximum(jnp.maximum(l0, l1), l2)
        w0, w1, w2 = jnp.exp(l0 - top), jnp.exp(l1 - top), jnp.exp(l2 - top)
        num = w0 * res_o[0, rows, :] + w1 * nat_o[0, rows, :] + w2 * nat_o[1, rows, :]
        o_ref[0, rows, :] = (num / (w0 + w1 + w2)).astype(o_ref.dtype)
        return carry

    lax.fori_loop(0, seq // rows_per_step, merge_body, 0)


def _prompt_attention(q, k, v, w_gate, w_up, w_down):
    B, S, _ = q.shape
    n_pairs = WIDTH // 128
    blk = pl.BlockSpec((1, S, 128), lambda b, hp: (b, 0, hp))
    n_steps = B * n_pairs
    assert N_EXPERTS % n_steps == 0
    e_blk = N_EXPERTS // n_steps
    expert_blk = lambda rows, cols: pl.BlockSpec((e_blk, rows, cols), lambda b, hp: (b * n_pairs + hp, 0, 0))
    return pl.pallas_call(
        functools.partial(_attn_kernel, seq=S),
        grid=(B, n_pairs),
        in_specs=[blk, blk, blk,
                  expert_blk(D_MODEL, D_EXPERT), expert_blk(D_MODEL, D_EXPERT), expert_blk(D_EXPERT, D_MODEL)],
        out_specs=[blk, expert_blk(D_MODEL, 2 * D_EXPERT), expert_blk(D_EXPERT, D_MODEL)],
        out_shape=[jax.ShapeDtypeStruct((B, S, WIDTH), BF16),
                   jax.ShapeDtypeStruct((N_EXPERTS, D_MODEL, 2 * D_EXPERT), BF16),
                   jax.ShapeDtypeStruct((N_EXPERTS, D_EXPERT, D_MODEL), BF16)],
        scratch_shapes=[
            pltpu.VMEM((S, 128), BF16),
            pltpu.VMEM((S + CHUNK, 128), BF16),
            pltpu.VMEM((S + CHUNK, 128), BF16),
            pltpu.VMEM((len(DILATIONS), S, 128), F32),
            pltpu.VMEM((len(DILATIONS), S, 128), F32),
            pltpu.VMEM((len(DILATIONS) - 1, S, 128), F32),
            pltpu.VMEM((len(DILATIONS) - 1, S, 128), F32),
            pltpu.VMEM((2, 2 * CHUNK, 2 * CHUNK), F32),
        ],
        compiler_params=pltpu.CompilerParams(
            dimension_semantics=("arbitrary", "arbitrary"), vmem_limit_bytes=V7X_VMEM_LIMIT),
        name="prompt_dilated_attention",
    )(q, k, v, w_gate, w_up, w_down)


def _route(logits):
    lane = lax.broadcasted_iota(jnp.int32, logits.shape, 1)
    big = jnp.int32(ROUTER_LANES)
    lg = jnp.where(lane < N_GROUPS, logits, NEG)
    gmax = jnp.max(lg, axis=1, keepdims=True)
    gp = 1.0 / jnp.sum(jnp.exp(lg - gmax), axis=1, keepdims=True)
    gi = jnp.min(jnp.where(lg == gmax, lane, big), axis=1, keepdims=True)
    lo = N_GROUPS + EXPERTS_PER_GROUP * gi
    le = jnp.where((lane >= lo) & (lane < lo + EXPERTS_PER_GROUP), logits, NEG)
    m1 = jnp.max(le, axis=1, keepdims=True)
    i1 = jnp.min(jnp.where(le == m1, lane, big), axis=1, keepdims=True)
    le2 = jnp.where(lane == i1, NEG, le)
    m2 = jnp.max(le2, axis=1, keepdims=True)
    i2 = jnp.min(jnp.where(le2 == m2, lane, big), axis=1, keepdims=True)
    e2 = jnp.exp(m2 - m1)
    w1 = 1.0 / (1.0 + e2)
    w2 = e2 / (1.0 + e2)
    return jnp.where(lane == i1, gp * w1, jnp.where(lane == i2, gp * w2, 0.0))


def _mix_router_kernel(a_ref, b_ref, x_ref, wo_ref, g2_ref, wr_ref, br_ref, xp_ref, h2_ref, gates_ref, *, precise):
    mixed = (_dot(a_ref[...], wo_ref[0:WIDTH, :], precise)
             + _dot(b_ref[...], wo_ref[WIDTH:2 * WIDTH, :], precise))
    xp = x_ref[...] + mixed
    xp_ref[...] = xp
    h2 = _rmsnorm(xp, g2_ref[...])
    h2_ref[...] = h2.astype(h2_ref.dtype)
    logits = jnp.dot(h2, wr_ref[...], preferred_element_type=F32,
                     precision=lax.Precision.HIGHEST) + br_ref[...]
    gates_ref[...] = _route(logits)


def _mix_router(a, b, x, w_out, g2, w_router, b_router, *, tm, precise=False):
    n = x.shape[0]
    const = lambda i: (0, 0)
    row_blk = lambda w: pl.BlockSpec((tm, w), lambda i: (i, 0))
    return pl.pallas_call(
        functools.partial(_mix_router_kernel, precise=precise),
        grid=(n // tm,),
        in_specs=[row_blk(WIDTH), row_blk(WIDTH), row_blk(D_MODEL),
                  pl.BlockSpec((2 * WIDTH, D_MODEL), const),
                  pl.BlockSpec((1, D_MODEL), const),
                  pl.BlockSpec((D_MODEL, ROUTER_LANES), const),
                  pl.BlockSpec((1, ROUTER_LANES), const)],
        out_specs=[row_blk(D_MODEL), row_blk(D_MODEL), row_blk(ROUTER_LANES)],
        out_shape=[jax.ShapeDtypeStruct((n, D_MODEL), F32),
                   jax.ShapeDtypeStruct((n, D_MODEL), BF16),
                   jax.ShapeDtypeStruct((n, ROUTER_LANES), F32)],
        compiler_params=pltpu.CompilerParams(
            dimension_semantics=("arbitrary",), vmem_limit_bytes=V7X_VMEM_LIMIT),
        name="outproj_router",
    )(a, b, x, w_out, g2, w_router, b_router)


def _nt_dot(w, t):
    return lax.dot_general(w, t, (((1,), (1,)), ((), ())), preferred_element_type=F32)


def _route_t(logits_t):
    row = lax.broadcasted_iota(jnp.int32, logits_t.shape, 0)
    big = jnp.int32(ROUTER_LANES)
    lg = jnp.where(row < N_GROUPS, logits_t, NEG)
    gmax = jnp.max(lg, axis=0, keepdims=True)
    gp = 1.0 / jnp.sum(jnp.exp(lg - gmax), axis=0, keepdims=True)
    gi = jnp.min(jnp.where(lg == gmax, row, big), axis=0, keepdims=True)
    lo = N_GROUPS + EXPERTS_PER_GROUP * gi
    le = jnp.where((row >= lo) & (row < lo + EXPERTS_PER_GROUP), logits_t, NEG)
    m1 = jnp.max(le, axis=0, keepdims=True)
    i1 = jnp.min(jnp.where(le == m1, row, big), axis=0, keepdims=True)
    le2 = jnp.where(row == i1, NEG, le)
    m2 = jnp.max(le2, axis=0, keepdims=True)
    i2 = jnp.min(jnp.where(le2 == m2, row, big), axis=0, keepdims=True)
    e2 = jnp.exp(m2 - m1)
    return i1 - N_GROUPS, i2 - N_GROUPS, gp / (1.0 + e2), gp * e2 / (1.0 + e2)


def _mix_route_sort_kernel(a_ref, b_ref, x_ref, wo_ref, g2_ref, wr_ref, brc_ref, tri_ref, ltri_ref,
                           xp_ref, h2_ref, mrow_ref, mcol_ref, tab_ref, wob):
    t = MOE_BLOCK

    @pl.when(pl.program_id(0) == 0)
    def _():
        wob[...] = wo_ref[...].astype(BF16)

    def project(j):
        rows = slice(j * t, (j + 1) * t)
        xp = x_ref[rows, :] + _dot(a_ref[rows, :], wob[0:WIDTH, :]) + _dot(b_ref[rows, :], wob[WIDTH:2 * WIDTH, :])
        xp_ref[rows, :] = xp
        h2 = _rmsnorm(xp, g2_ref[...])
        hi = h2.astype(BF16)
        h2_ref[rows, :] = hi
        return hi, (h2 - hi.astype(F32)).astype(BF16)

    def route(hi, lo):
        prod_hi = _nt_dot(wr_ref[...], hi)
        logits_t = (prod_hi[0:ROUTER_ROWS] + prod_hi[ROUTER_ROWS:2 * ROUTER_ROWS]
                    + _nt_dot(wr_ref[0:ROUTER_ROWS, :], lo) + brc_ref[...])
        return _route_t(logits_t)

    def sort_meta(j, ex1, ex2, gate1, gate2):
        rows = slice(j * t, (j + 1) * t)
        pair_e = jnp.concatenate([ex1, ex2], axis=1)
        row = lax.broadcasted_iota(jnp.int32, (N_EXPERTS, 2 * t), 0)
        onehot = jnp.where(row == pair_e, 1.0, 0.0)
        n_lane_tiles = 2 * t // 128
        local = _dot(jnp.concatenate([onehot[:, k * 128:(k + 1) * 128] for k in range(n_lane_tiles)], axis=0),
                     tri_ref[...])
        carry = jnp.zeros((N_EXPERTS, 1), F32)
        cums = []
        for k in range(n_lane_tiles):
            tile = local[k * N_EXPERTS:(k + 1) * N_EXPERTS, :]
            cums.append(tile + carry)
            carry = carry + tile[:, 127:128]
        cum = jnp.concatenate(cums, axis=1)
        rank = jnp.sum(onehot * cum, axis=0, keepdims=True) - 1.0
        counts = carry
        units32 = jnp.floor((counts + (MOE_ROW_ALIGN - 1)) * (1.0 / MOE_ROW_ALIGN))
        units = jnp.concatenate([jnp.broadcast_to(units32, (N_EXPERTS, 128)),
                                 jnp.zeros((ROUTER_LANES - N_EXPERTS, 128), F32)], axis=0)
        off = _dot(ltri_ref[...], units) * MOE_ROW_ALIGN
        dst = jnp.sum(onehot * off[0:N_EXPERTS, 0:1], axis=0, keepdims=True) + rank

        r8 = lax.broadcasted_iota(jnp.int32, (8, t), 0)
        mrow_ref[j] = jnp.where(r8 == 0, dst[:, 0:t], jnp.where(r8 == 1, dst[:, t:2 * t],
                                jnp.where(r8 == 2, gate1, jnp.where(r8 == 3, gate2, 0.0))))
        r128 = lax.broadcasted_iota(jnp.int32, (ROUTER_LANES, t), 0)
        meta = jnp.where(r128 == 0, dst[:, 0:t], jnp.where(r128 == 1, dst[:, t:2 * t],
                         jnp.where(r128 == 2, gate1, jnp.where(r128 == 3, gate2, 0.0))))
        mcol_ref[rows, :] = meta.T
        lane = lax.broadcasted_iota(jnp.int32, (ROUTER_LANES, 128), 1)
        n_rows = units * MOE_ROW_ALIGN
        chunks = jnp.floor((n_rows + (MOE_CHUNK - 1)) * (1.0 / MOE_CHUNK))
        tab_ref[j] = jnp.where(lane == 0, off, jnp.where(lane == 1, chunks, jnp.where(lane == 2, off + n_rows, 0.0)))

    blocks = range(MOE_ROUTE_BLOCKS)
    projected = [project(j) for j in blocks]
    routed = [route(hi, lo) for hi, lo in projected]
    for j in blocks:
        sort_meta(j, *routed[j])


def _mix_route_sort(a, b, x, w_out, g2, wr_hl, br_col, tri, ltri):
    n = x.shape[0]
    t = MOE_BLOCK
    nblk = n // t
    g = MOE_ROUTE_BLOCKS
    assert nblk % g == 0
    const = lambda i: (0, 0)
    row_blk = lambda w: pl.BlockSpec((g * t, w), lambda i: (i, 0))
    return pl.pallas_call(
        _mix_route_sort_kernel,
        grid=(nblk // g,),
        in_specs=[row_blk(WIDTH), row_blk(WIDTH), row_blk(D_MODEL),
                  pl.BlockSpec((2 * WIDTH, D_MODEL), const, pipeline_mode=pl.Buffered(1)),
                  pl.BlockSpec((1, D_MODEL), const),
                  pl.BlockSpec((2 * ROUTER_ROWS, D_MODEL), const),
                  pl.BlockSpec((ROUTER_ROWS, 1), const),
                  pl.BlockSpec((ROUTER_LANES, ROUTER_LANES), const),
                  pl.BlockSpec((ROUTER_LANES, ROUTER_LANES), const)],
        out_specs=[row_blk(D_MODEL), row_blk(D_MODEL),
                   pl.BlockSpec((g, 8, t), lambda i: (i, 0, 0)),
                   row_blk(ROUTER_LANES),
                   pl.BlockSpec((g, ROUTER_LANES, 128), lambda i: (i, 0, 0))],
        out_shape=[jax.ShapeDtypeStruct((n, D_MODEL), F32),
                   jax.ShapeDtypeStruct((n, D_MODEL), BF16),
                   jax.ShapeDtypeStruct((nblk, 8, t), F32),
                   jax.ShapeDtypeStruct((n, ROUTER_LANES), F32),
                   jax.ShapeDtypeStruct((nblk, ROUTER_LANES, 128), F32)],
        scratch_shapes=[pltpu.VMEM((2 * WIDTH, D_MODEL), BF16)],
        compiler_params=pltpu.CompilerParams(
            dimension_semantics=("arbitrary",), vmem_limit_bytes=V7X_VMEM_LIMIT),
        name="outproj_route_sort",
    )(a, b, x, w_out, g2, wr_hl, br_col, tri, ltri)


def _silu_mul(ab):
    a = ab[:, :D_EXPERT]
    return a * (1.0 / (1.0 + jnp.exp(-a))) * ab[:, D_EXPERT:]


def _chunk_offsets(tab, n_blk, e):
    return [pl.multiple_of(tab(j, e, 0), MOE_ROW_ALIGN) for j in range(n_blk)]


def _moe_gate_up_kernel(tab_ref, h_ref, mrow_ref, w1_ref, sh_ref, sg_ref, hs_ref, shid_ref, xs, *,
                        flags_at, tails_at):
    t = MOE_BLOCK
    n_blk = MOE_GATE_UP_BLOCKS
    first = pl.program_id(0) * n_blk
    tab = lambda j, e, c: tab_ref[((first + j) * N_EXPERTS + e) * 3 + c]

    piece = 512

    def gather(j, r0, rows):
        dst1 = mrow_ref[j, 0:1, :].astype(jnp.int32)
        dst2 = mrow_ref[j, 1:2, :].astype(jnp.int32)
        d_idx = lax.broadcasted_iota(jnp.int32, (rows, t), 0) + r0
        sel = jnp.where((d_idx == dst1) | (d_idx == dst2), 1.0, 0.0)
        xs[j, r0:r0 + rows, :] = _dot(sel, h_ref[j * t:(j + 1) * t, :]).astype(BF16)

    for j in range(n_blk):
        for r0 in range(0, MOE_GATHER_ROWS, piece):
            gather(j, r0, min(piece, MOE_GATHER_ROWS - r0))
        xs[j, MOE_GATHER_ROWS:MOE_ROWS, :] = jnp.zeros((MOE_ROWS - MOE_GATHER_ROWS, D_MODEL), BF16)
    hs_ref[...] = jnp.zeros(hs_ref.shape, BF16)
    for j in range(n_blk):
        @pl.when(tab_ref[tails_at + first + j] > 0)
        def _(j=j):
            gather(j, MOE_GATHER_ROWS, MOE_SORT_ROWS - MOE_GATHER_ROWS)

    def first_chunks(g, carry):
        for i in range(MOE_EXPERTS_PER_STEP):
            e = g * MOE_EXPERTS_PER_STEP + i
            offs = _chunk_offsets(tab, n_blk, e)
            x = jnp.concatenate([xs[j, pl.ds(offs[j], MOE_CHUNK), :] for j in range(n_blk)], axis=0)
            hid = _silu_mul(jnp.dot(x, w1_ref[e], preferred_element_type=F32)).astype(BF16)
            for j in range(n_blk):
                hs_ref[j, pl.ds(offs[j], MOE_CHUNK), :] = hid[j * MOE_CHUNK:(j + 1) * MOE_CHUNK, :]
        return carry

    lax.fori_loop(0, N_EXPERTS // MOE_EXPERTS_PER_STEP, first_chunks, 0)

    def more_chunks(j, e, carry):
        off, n_chunks, end = tab(j, e, 0), tab(j, e, 1), tab(j, e, 2)

        def chunk(c, carry):
            r0 = pl.multiple_of(off + c * MOE_CHUNK, MOE_ROW_ALIGN)
            rows = r0 + lax.broadcasted_iota(jnp.int32, (MOE_CHUNK, D_EXPERT), 0)
            hid = _silu_mul(jnp.dot(xs[j, pl.ds(r0, MOE_CHUNK), :], w1_ref[e], preferred_element_type=F32))
            hs_ref[j, pl.ds(r0, MOE_CHUNK), :] = jnp.where(rows < end, hid.astype(BF16),
                                                           hs_ref[j, pl.ds(r0, MOE_CHUNK), :])
            return carry

        return lax.fori_loop(1, n_chunks, chunk, carry)

    for j in range(n_blk):
        @pl.when(tab_ref[flags_at + first + j] > 0)
        def _(j=j):
            lax.fori_loop(0, N_EXPERTS, functools.partial(more_chunks, j), 0)

    @pl.when(pl.program_id(0) == 0)
    def _():
        sh = sh_ref[...]
        gates = sg_ref[...]
        for e in range(N_EXPERTS):
            gate = gates[:, N_GROUPS + e:N_GROUPS + e + 1]
            hid = _silu_mul(jnp.dot(sh, w1_ref[e], preferred_element_type=F32)) * gate
            shid_ref[:, e * D_EXPERT:(e + 1) * D_EXPERT] = hid.astype(BF16)


def _moe_gate_up(tab, h2, mrow, w1_b, s_h, s_gates):
    n = h2.shape[0]
    n_s = s_h.shape[0]
    nblk = n // MOE_BLOCK
    g = MOE_GATE_UP_BLOCKS
    assert nblk % g == 0
    whole = lambda shape: pl.BlockSpec(shape, lambda i, tab: (0,) * len(shape))
    return pl.pallas_call(
        functools.partial(_moe_gate_up_kernel, flags_at=nblk * N_EXPERTS * 3, tails_at=nblk * N_EXPERTS * 3 + nblk),
        grid_spec=pltpu.PrefetchScalarGridSpec(
            num_scalar_prefetch=1,
            grid=(nblk // g,),
            in_specs=[pl.BlockSpec((g * MOE_BLOCK, D_MODEL), lambda i, tab: (i, 0)),
                      pl.BlockSpec((g, 8, MOE_BLOCK), lambda i, tab: (i, 0, 0)),
                      pl.BlockSpec(w1_b.shape, lambda i, tab: (0, 0, 0), pipeline_mode=pl.Buffered(1)),
                      whole(s_h.shape), whole(s_gates.shape)],
            out_specs=[pl.BlockSpec((g, MOE_ROWS, D_EXPERT), lambda i, tab: (i, 0, 0)),
                       whole((n_s, N_EXPERTS * D_EXPERT))],
            scratch_shapes=[pltpu.VMEM((g, MOE_ROWS, D_MODEL), BF16)]),
        out_shape=[jax.ShapeDtypeStruct((nblk, MOE_ROWS, D_EXPERT), BF16),
                   jax.ShapeDtypeStruct((n_s, N_EXPERTS * D_EXPERT), BF16)],
        compiler_params=pltpu.CompilerParams(
            dimension_semantics=("arbitrary",), vmem_limit_bytes=V7X_VMEM_LIMIT),
        name="moe_gate_up",
    )(tab, h2, mrow, w1_b, s_h, s_gates)


def _moe_down_kernel(tab_ref, hs_ref, xp_ref, mcol_ref, w2_ref, gf_ref, shid_ref, sx_ref, y_ref, sy_ref, os, comb_scr, *,
                     flags_at):
    t = MOE_BLOCK
    n_blk = MOE_DOWN_BLOCKS
    first = pl.program_id(0) * n_blk
    tab = lambda j, e, c: tab_ref[((first + j) * N_EXPERTS + e) * 3 + c]
    os[...] = jnp.zeros(os.shape, BF16)

    def first_chunks(g, carry):
        for i in range(MOE_EXPERTS_PER_STEP):
            e = g * MOE_EXPERTS_PER_STEP + i
            offs = _chunk_offsets(tab, n_blk, e)
            hid = jnp.concatenate([hs_ref[j, pl.ds(offs[j], MOE_CHUNK), :] for j in range(n_blk)], axis=0)
            out = jnp.dot(hid, w2_ref[e], preferred_element_type=F32).astype(BF16)
            for j in range(n_blk):
                os[j, pl.ds(offs[j], MOE_CHUNK), :] = out[j * MOE_CHUNK:(j + 1) * MOE_CHUNK, :]
        return carry

    lax.fori_loop(0, N_EXPERTS // MOE_EXPERTS_PER_STEP, first_chunks, 0)

    def more_chunks(j, e, carry):
        off, n_chunks, end = tab(j, e, 0), tab(j, e, 1), tab(j, e, 2)

        def chunk(c, carry):
            r0 = pl.multiple_of(off + c * MOE_CHUNK, MOE_ROW_ALIGN)
            rows = r0 + lax.broadcasted_iota(jnp.int32, (MOE_CHUNK, D_MODEL), 0)
            out = jnp.dot(hs_ref[j, pl.ds(r0, MOE_CHUNK), :], w2_ref[e], preferred_element_type=F32)
            os[j, pl.ds(r0, MOE_CHUNK), :] = jnp.where(rows < end, out.astype(BF16), os[j, pl.ds(r0, MOE_CHUNK), :])
            return carry

        return lax.fori_loop(1, n_chunks, chunk, carry)

    for j in range(n_blk):
        @pl.when(tab_ref[flags_at + first + j] > 0)
        def _(j=j):
            lax.fori_loop(0, N_EXPERTS, functools.partial(more_chunks, j), 0)

    strip = 64
    l_idx = lax.broadcasted_iota(jnp.int32, (strip, MOE_SORT_ROWS), 1)

    def scatter_matrix(j):
        for r0 in range(0, t, strip):
            mcol = mcol_ref[j * t + r0:j * t + r0 + strip, :]
            d1c = mcol[:, 0:1].astype(jnp.int32)
            d2c = mcol[:, 1:2].astype(jnp.int32)
            comb = jnp.where(l_idx == d1c, mcol[:, 2:3], jnp.where(l_idx == d2c, mcol[:, 3:4], 0.0))
            comb_scr[j, r0:r0 + strip, :] = comb.astype(BF16)

    for j in range(n_blk):
        scatter_matrix(j)
    moes = [_dot(comb_scr[j], os[j, 0:MOE_SORT_ROWS, :]) for j in range(n_blk)]
    for j in range(n_blk):
        rows = slice(j * t, (j + 1) * t)
        y_ref[rows, :] = _rmsnorm(xp_ref[rows, :] + moes[j], gf_ref[...])

    @pl.when(pl.program_id(0) == 0)
    def _():
        w2_all = w2_ref[...].reshape(N_EXPERTS * D_EXPERT, D_MODEL)
        ys = sx_ref[...] + jnp.dot(shid_ref[...], w2_all, preferred_element_type=F32)
        sy_ref[...] = _rmsnorm(ys, gf_ref[...])


def _moe_down(tab, hs_sorted, xp, mcol, w2_b, gf, s_hid, s_x):
    n = xp.shape[0]
    n_s = s_x.shape[0]
    nblk = n // MOE_BLOCK
    g = MOE_DOWN_BLOCKS
    assert nblk % g == 0
    whole = lambda shape: pl.BlockSpec(shape, lambda i, tab: (0,) * len(shape))
    row_blk = lambda w: pl.BlockSpec((g * MOE_BLOCK, w), lambda i, tab: (i, 0))
    return pl.pallas_call(
        functools.partial(_moe_down_kernel, flags_at=nblk * N_EXPERTS * 3),
        grid_spec=pltpu.PrefetchScalarGridSpec(
            num_scalar_prefetch=1,
            grid=(nblk // g,),
            in_specs=[pl.BlockSpec((g, MOE_ROWS, D_EXPERT), lambda i, tab: (i, 0, 0)),
                      row_blk(D_MODEL), row_blk(ROUTER_LANES),
                      pl.BlockSpec(w2_b.shape, lambda i, tab: (0, 0, 0), pipeline_mode=pl.Buffered(1)),
                      whole((1, D_MODEL)), whole(s_hid.shape), whole(s_x.shape)],
            out_specs=[row_blk(D_MODEL), whole((n_s, D_MODEL))],
            scratch_shapes=[pltpu.VMEM((g, MOE_ROWS, D_MODEL), BF16),
                            pltpu.VMEM((g, MOE_BLOCK, MOE_SORT_ROWS), BF16)]),
        out_shape=[jax.ShapeDtypeStruct((n, D_MODEL), F32), jax.ShapeDtypeStruct((n_s, D_MODEL), F32)],
        compiler_params=pltpu.CompilerParams(
            dimension_semantics=("arbitrary",), vmem_limit_bytes=V7X_VMEM_LIMIT),
        name="moe_down_combine",
    )(tab, hs_sorted, xp, mcol, w2_b, gf, s_hid, s_x)


def _sample_proj_kernel(x_ref, g1_ref, w_ref, cos_ref, sin_ref, gv_ref, ones_ref, w00_ref, b0_ref,
                        rep_ref, foldt_ref, a_ref, k_ref, v_ref, vn_ref, qkvt_ref):
    h = _rmsnorm(x_ref[...], g1_ref[...])

    def proj(i):
        return _dot(h, w_ref[:, i * WIDTH:(i + 1) * WIDTH], precise=True)

    cos = _tile_lanes(cos_ref[...], WIDTH // 128)
    sin = _tile_lanes(sin_ref[...], WIDTH // 128)
    q = _rope(proj(2), cos, sin) * (HEAD_DIM ** -0.5)
    k = _rope(proj(3), cos, sin)
    v = proj(4)
    vn = _group_rmsnorm(proj(1), ones_ref[...], gv_ref[...], precise=True)
    a_ref[...] = proj(0) * (w00_ref[...] * vn + b0_ref[...])
    k_ref[...] = k
    v_ref[...] = v
    vn_ref[...] = vn

    n_rep = rep_ref.shape[0]
    r_idx = lax.broadcasted_iota(jnp.int32, (n_rep, WIDTH), 0)
    l_idx = lax.broadcasted_iota(jnp.int32, (n_rep, WIDTH), 1)
    own = (l_idx // HEAD_DIM) == (r_idx % N_HEADS)
    for t, src in enumerate((q, k, v)):
        rep = _dot(rep_ref[...], src, precise=True)
        qkvt_ref[t * HEAD_DIM:(t + 1) * HEAD_DIM, :] = lax.dot_general(
            foldt_ref[...], jnp.where(own, rep, 0.0), (((1,), (1,)), ((), ())),
            preferred_element_type=F32, precision=lax.Precision.HIGHEST)


def _sample_proj(x, g1, w_in, cos, sin, gv, ones_bd, w00, b0, rep, foldt):
    bd = x.shape[0]
    sds = lambda r, c: jax.ShapeDtypeStruct((r, c), F32)
    return pl.pallas_call(
        _sample_proj_kernel,
        out_shape=[sds(bd, WIDTH), sds(bd, WIDTH), sds(bd, WIDTH), sds(bd, WIDTH),
                   sds(3 * HEAD_DIM, bd * N_HEADS)],
        compiler_params=pltpu.CompilerParams(vmem_limit_bytes=V7X_VMEM_LIMIT),
        name="sample_proj",
    )(x, g1, w_in, cos, sin, gv, ones_bd, w00, b0, rep, foldt)


def _sample_scores(qkv_ref, k_ref, win):
    cols = qkv_ref[0]

    head = lax.broadcasted_iota(jnp.int32, (N_HEADS, win), 0)
    s = jnp.zeros((N_HEADS, win), F32)
    s_self = jnp.zeros((N_HEADS, 1), F32)
    for h in range(N_HEADS):
        qc = cols[0:HEAD_DIM, h:h + 1]
        kc = cols[HEAD_DIM:2 * HEAD_DIM, h:h + 1]
        s_h = jnp.sum(k_ref[0, h] * qc, axis=0, keepdims=True)
        s = jnp.where(head == h, s_h, s)
        s_self = jnp.where(head[:, 0:1] == h, jnp.sum(qc * kc, axis=0, keepdims=True), s_self)
    return s, s_self, cols[2 * HEAD_DIM:3 * HEAD_DIM, :]


def _sample_finish(s, s_self, v_cols, v_ref, o_ref, win):
    dist = win - lax.broadcasted_iota(jnp.int32, (1, win), 1)
    members = [(dist <= N_KEYS * dil) & (dist % dil == 0) for dil in DILATIONS]
    es, e_selfs, dens, lses = [], [], [], []
    for mem in members:
        sm = jnp.where(mem, s, NEG)
        m = jnp.maximum(jnp.max(sm, axis=1, keepdims=True), s_self)
        e = jnp.exp(sm - m)
        e_self = jnp.exp(s_self - m)
        den = jnp.sum(e, axis=1, keepdims=True) + e_self
        es.append(e)
        e_selfs.append(e_self)
        dens.append(den)
        lses.append(m + jnp.log(den))
    top = jnp.maximum(jnp.maximum(lses[0], lses[1]), lses[2])
    ws = [jnp.exp(l - top) for l in lses]
    wsum = ws[0] + ws[1] + ws[2]
    coef = [w / (den * wsum) for w, den in zip(ws, dens)]
    p_keys = coef[0] * es[0] + coef[1] * es[1] + coef[2] * es[2]
    p_self = coef[0] * e_selfs[0] + coef[1] * e_selfs[1] + coef[2] * e_selfs[2]

    for h in range(N_HEADS):
        o_ref[0, :, h:h + 1] = (jnp.sum(v_ref[0, h] * p_keys[h:h + 1, :], axis=1, keepdims=True)
                                + p_self[h:h + 1, :] * v_cols[:, h:h + 1])


def _rope_tables(first_pos, count):
    half = HEAD_DIM // 2
    inv = ROPE_THETA ** (-np.arange(half, dtype=np.float64) * 2.0 / HEAD_DIM)
    ang = (first_pos + np.arange(count, dtype=np.float64))[:, None] * inv[None, :]
    cos, sin = np.cos(ang), np.sin(ang)
    cos128 = np.concatenate([cos, cos, cos, cos], axis=1).astype(np.float32)
    sin128 = np.concatenate([-sin, sin, -sin, sin], axis=1).astype(np.float32)
    return jnp.asarray(cos128), jnp.asarray(sin128)


def kernel(x_prompt, x_sample, cache_win_k, cache_win_v, ln1_g, w_in, sgu_norm_g, sgu_w, sgu_b, w_out, ln2_g,
           w_router_group, b_router_group, w_router_expert, b_router_expert, w_gate, w_up, w_down, lnf_g):
    depth = w_in.shape[0]
    assert depth == 1 and x_sample.shape[1] == 1
    B, S, _ = x_prompt.shape
    bd = x_sample.shape[0]
    win = cache_win_k.shape[2]
    assert S % (max(DILATIONS) * CHUNK) == 0 and win >= max(DILATIONS) * N_KEYS and PAST_LEN % CHUNK == 0
    l = 0

    pad = ROUTER_LANES - N_GROUPS - N_EXPERTS
    w_router = jnp.pad(jnp.concatenate([w_router_group[l], w_router_expert[l]], axis=1), ((0, 0), (0, pad)))
    b_router = jnp.pad(jnp.concatenate([b_router_group[l], b_router_expert[l]]), (0, pad))[None, :]
    g1 = ln1_g[l][None, :]
    g2 = ln2_g[l][None, :]
    gf = lnf_g[None, :]
    gv = sgu_norm_g[l].reshape(1, WIDTH)
    grp = np.arange(WIDTH) // HEAD_DIM
    ones_bd = jnp.asarray(np.where(grp[:, None] == grp[None, :], 1.0 / HEAD_DIM, 0.0), BF16)
    wp = jnp.concatenate([sgu_w[l][0::2], sgu_w[l][1::2]], axis=-1)
    bias = jnp.repeat(sgu_b[l].T, HEAD_DIM, axis=1)
    w00 = jnp.repeat(sgu_w[l][:, 0, 0], HEAD_DIM)[None, :]
    b0 = jnp.repeat(sgu_b[l][:, 0], HEAD_DIM)[None, :]

    cos_s, sin_s = _rope_tables(PAST_LEN, 1)
    rep = jnp.asarray(np.arange(bd * N_HEADS)[:, None] // N_HEADS == np.arange(bd)[None, :], F32)
    foldt = jnp.asarray(np.arange(HEAD_DIM)[:, None] == np.arange(WIDTH)[None, :] % HEAD_DIM, F32)
    xs = x_sample.reshape(bd, D_MODEL)
    a_s, k_s, v_s, vn_s, qkvt = _sample_proj(xs, g1, w_in[l], cos_s, sin_s, gv, ones_bd, w00, b0, rep, foldt)
    to_pos_minor = lambda c: jnp.transpose(c, (0, 2, 3, 1))

    cos_p, sin_p = _rope_tables(0, S)
    qkv_seq = jnp.transpose(qkvt.reshape(3 * HEAD_DIM, bd, N_HEADS), (1, 0, 2))
    a_p, q_p, k_p, v_p, kt_p, vt_p, o3 = _prompt_proj(
        x_prompt, g1, w_in[l], cos_p, sin_p, gv, ones_bd, wp, bias,
        qkv_seq, to_pos_minor(cache_win_k[l]), to_pos_minor(cache_win_v[l]))
    b_p, w1_b, w2_b = _prompt_attention(q_p, k_p, v_p, w_gate[l], w_up[l], w_down[l])
    n = B * S
    assert n % MOE_BLOCK == 0
    wr_t = w_router.T[:ROUTER_ROWS]
    wr_hi = wr_t.astype(BF16)
    wr_hl = jnp.concatenate([wr_hi, (wr_t - wr_hi.astype(F32)).astype(BF16)], axis=0)
    lane_idx = np.arange(ROUTER_LANES)
    tri = jnp.asarray(lane_idx[:, None] <= lane_idx[None, :], BF16)
    ltri = jnp.asarray(lane_idx[None, :] < lane_idx[:, None], BF16)
    xp2, h2, mrow, mcol, tab_f = _mix_route_sort(
        a_p.reshape(n, WIDTH), b_p.reshape(n, WIDTH), x_prompt.reshape(n, D_MODEL),
        w_out[l], g2, wr_hl, b_router.reshape(ROUTER_LANES, 1)[:ROUTER_ROWS], tri, ltri)
    tab_i = tab_f[:, :N_EXPERTS, 0:3].astype(jnp.int32)
    multi_chunk = (jnp.max(tab_i[:, :, 1], axis=1) > 1).astype(jnp.int32)
    in_tail = (tab_i[:, N_EXPERTS - 1, 2] > MOE_GATHER_ROWS).astype(jnp.int32)
    tab = jnp.concatenate([tab_i.reshape(-1), multi_chunk, in_tail])

    b_s = jnp.transpose(o3, (0, 2, 1)).reshape(bd, WIDTH)
    xs2, hs2, gates_s = _mix_router(a_s, b_s, xs, w_out[l], g2, w_router, b_router, tm=bd, precise=True)

    hid_sorted, hid_s = _moe_gate_up(tab, h2, mrow, w1_b, hs2, gates_s)
    y_prompt, y_sample = _moe_down(tab, hid_sorted, xp2, mcol, w2_b, gf, hid_s, xs2)
    y_prompt = y_prompt.reshape(B, S, D_MODEL)
    y_sample = y_sample.reshape(bd, 1, D_MODEL)
    buf_p = min(MAX_WINDOW, S)
    to_win = lambda t: jnp.transpose(t.reshape(1, B, N_HEADS, HEAD_DIM, buf_p), (0, 1, 4, 2, 3))
    new_k_p = to_win(kt_p)
    new_v_p = to_win(vt_p)

    shape_s = (1, bd, 1, N_HEADS, HEAD_DIM)
    return (y_prompt, y_sample, new_k_p, new_v_p,
            k_s.reshape(shape_s), v_s.reshape(shape_s), vn_s.reshape(shape_s))
```
